```python
import jax, jax.numpy as jnp
from jax import lax
import numpy as np

D_MODEL = 1024
BATCH = 4
SEQ = 4096
DEPTH = 1

N_META = 16
BLOCK = 128
PAD_FRONT = BLOCK - N_META
HEAD_DIM = 64
ATTN_WIDTH = D_MODEL // 2
LRU_WIDTH = D_MODEL - ATTN_WIDTH
N_Q_HEADS = ATTN_WIDTH // HEAD_DIM
N_KV_HEADS = 2
Q_PER_KV = N_Q_HEADS // N_KV_HEADS
KV_WIDTH = N_KV_HEADS * HEAD_DIM
WINDOW = 128
LRU_BLOCKS = 8
LRU_BLOCK_W = LRU_WIDTH // LRU_BLOCKS
CONV_W = 4
LRU_C = 8.0
IN_SPLITS = (ATTN_WIDTH, KV_WIDTH, KV_WIDTH, LRU_WIDTH, LRU_WIDTH)
IN_COLS = ATTN_WIDTH + 2 * KV_WIDTH + 2 * LRU_WIDTH
N_GROUPS = 4
EXPERTS_PER_GROUP = 8
N_EXPERTS = N_GROUPS * EXPERTS_PER_GROUP
TOP_K = 2
D_FF_EXPERT = D_MODEL // 2
MOE_BLOCK = 128
ALPHA = (2.0 * DEPTH) ** 0.25
BETA = (8.0 * DEPTH) ** -0.25
EPS = 1e-5
NEG = -1e30

kernel_name = 'hymba_swa_rglru_hmoe_deepnorm'


def layer_norm(x, g, b):
    xf = x.astype(jnp.float32)
    mu = jnp.mean(xf, axis=-1, keepdims=True)
    var = jnp.mean(jnp.square(xf - mu), axis=-1, keepdims=True)
    return ((xf - mu) * lax.rsqrt(var + EPS) * g.astype(jnp.float32) + b.astype(jnp.float32)).astype(x.dtype)


def rms_norm(x, g):
    xf = x.astype(jnp.float32)
    ms = jnp.mean(jnp.square(xf), axis=-1, keepdims=True)
    return (xf * lax.rsqrt(ms + EPS) * g.astype(jnp.float32)).astype(x.dtype)


def alibi_slopes():
    return jnp.exp2(-8.0 * jnp.arange(1, N_Q_HEADS + 1, dtype=jnp.float32) / N_Q_HEADS)


def sliding_window_attention(q, k, v, sinks):
    bsz, l_len = q.shape[0], q.shape[1]
    nb = l_len // BLOCK
    f32 = jnp.float32
    qb = q.astype(f32).reshape(bsz, nb, BLOCK, N_KV_HEADS, Q_PER_KV, HEAD_DIM)
    kb = k.astype(f32).reshape(bsz, nb, BLOCK, N_KV_HEADS, HEAD_DIM)
    vb = v.astype(f32).reshape(bsz, nb, BLOCK, N_KV_HEADS, HEAD_DIM)
    shift = ((0, 0), (1, 0), (0, 0), (0, 0), (0, 0))
    kw = jnp.concatenate([jnp.pad(kb, shift)[:, :-1], kb], axis=2)
    vw = jnp.concatenate([jnp.pad(vb, shift)[:, :-1], vb], axis=2)
    scores = jnp.einsum('bnqhgd,bnkhd->bnhgqk', qb, kw) * (HEAD_DIM ** -0.5)
    qi = np.arange(BLOCK)[:, None]
    kj = np.arange(2 * BLOCK)[None, :]
    dist = qi - kj + BLOCK
    band = (dist >= 0) & (dist < WINDOW)
    key_pos = np.arange(nb)[:, None] * BLOCK - BLOCK + np.arange(2 * BLOCK)[None, :]
    key_valid = key_pos >= PAD_FRONT
    mask = band[None, :, :] & key_valid[:, None, :]
    slopes = alibi_slopes().reshape(N_KV_HEADS, Q_PER_KV, 1, 1)
    logits = scores - slopes * jnp.asarray(dist, f32)
    logits = jnp.where(mask[None, :, None, None], logits, NEG)
    sink_col = jnp.broadcast_to(sinks.astype(f32).reshape(N_KV_HEADS, Q_PER_KV, 1, 1), logits.shape[:-1] + (1,))
    probs = jax.nn.softmax(jnp.concatenate([logits, sink_col], axis=-1), axis=-1)[..., :-1]
    out = jnp.einsum('bnhgqk,bnkhd->bnqhgd', probs, vw)
    return out.reshape(bsz, l_len, ATTN_WIDTH).astype(q.dtype)


def causal_depthwise_conv(x, w, b):
    y = lax.conv_general_dilated(x, w[:, None, :].astype(x.dtype), window_strides=(1,),
                                 padding=[(CONV_W - 1, 0)],
                                 dimension_numbers=('NWC', 'WIO', 'NWC'),
                                 feature_group_count=x.shape[-1])
    return y + b.astype(x.dtype)


def _linear_combine(c1, c2):
    a1, b1 = c1
    a2, b2 = c2
    return a1 * a2, a2 * b1 + b2


def rg_lru(xc, w_a, b_a, w_x, b_x, lam):
    bsz, t_len, width = xc.shape
    f32 = jnp.float32
    xf = xc.astype(f32)
    xb = xf.reshape(bsz, t_len, LRU_BLOCKS, LRU_BLOCK_W)
    r = jax.nn.sigmoid(jnp.einsum('btnc,ncd->btnd', xb, w_a.astype(f32)) + b_a.astype(f32)).reshape(bsz, t_len, width)
    i = jax.nn.sigmoid(jnp.einsum('btnc,ncd->btnd', xb, w_x.astype(f32)) + b_x.astype(f32)).reshape(bsz, t_len, width)
    log_a = -LRU_C * r * jax.nn.softplus(-lam.astype(f32))
    a = jnp.exp(log_a)
    u = jnp.sqrt(-jnp.expm1(2.0 * log_a)) * (i * xf)
    _, h = lax.associative_scan(_linear_combine, (a, u), axis=1)
    return h.astype(xc.dtype)


def token_mixer(x, w_in, conv_w, conv_b, lru_wa, lru_ba, lru_wx, lru_bx, lru_lambda,
                attn_sinks, g_attn, g_lru, w_out):
    bsz, t_len, _ = x.shape
    proj = jnp.einsum('btd,dc->btc', x, w_in)
    offs = np.cumsum(IN_SPLITS)[:-1].tolist()
    q, k, v, xr, yr = jnp.split(proj, offs, axis=-1)
    pad = ((0, 0), (PAD_FRONT, 0), (0, 0))
    l_len = t_len + PAD_FRONT
    qp = jnp.pad(q, pad).reshape(bsz, l_len, N_Q_HEADS, HEAD_DIM)
    kp = jnp.pad(k, pad).reshape(bsz, l_len, N_KV_HEADS, HEAD_DIM)
    vp = jnp.pad(v, pad).reshape(bsz, l_len, N_KV_HEADS, HEAD_DIM)
    attn = sliding_window_attention(qp, kp, vp, attn_sinks)[:, PAD_FRONT:]
    xc = causal_depthwise_conv(xr, conv_w, conv_b)
    lru = rg_lru(xc, lru_wa, lru_ba, lru_wx, lru_bx, lru_lambda) * jax.nn.gelu(yr)
    merged = jnp.concatenate([rms_norm(attn, g_attn), rms_norm(lru, g_lru)], axis=-1)
    return jnp.einsum('btc,cd->btd', merged, w_out)


def hierarchical_moe(h, w_group, b_group, w_router, b_router, w_gate, w_up, w_down):
    bsz, t_len, d = h.shape
    f32 = jnp.float32
    xf = h.reshape(-1, d)
    n_tok = xf.shape[0]
    x32 = xf.astype(f32)
    g_logits = x32 @ w_group.astype(f32) + b_group.astype(f32)
    g_prob = jax.nn.softmax(g_logits, axis=-1)
    g_idx = jnp.argmax(g_logits, axis=-1)
    g_w = jnp.take_along_axis(g_prob, g_idx[:, None], axis=-1)
    e_logits = jnp.einsum('nd,gde->nge', x32, w_router.astype(f32)) + b_router.astype(f32)
    e_logits = jnp.take_along_axis(e_logits, g_idx[:, None, None], axis=1)[:, 0]
    top_v, top_i = lax.top_k(e_logits, TOP_K)
    gates = g_w * jax.nn.softmax(top_v, axis=-1)
    expert = g_idx[:, None].astype(jnp.int32) * EXPERTS_PER_GROUP + top_i.astype(jnp.int32)
    n_slots = n_tok * TOP_K
    flat_e = expert.reshape(-1)
    flat_tok = jnp.arange(n_slots, dtype=jnp.int32) // TOP_K
    flat_g = gates.reshape(-1)
    order = jnp.argsort(flat_e)
    se = flat_e[order]
    counts = jnp.bincount(flat_e, length=N_EXPERTS).astype(jnp.int32)
    starts = jnp.cumsum(counts) - counts
    padded = (counts + MOE_BLOCK - 1) // MOE_BLOCK * MOE_BLOCK
    pends = jnp.cumsum(padded)
    pstarts = pends - padded
    dest = pstarts[se] + (jnp.arange(n_slots, dtype=jnp.int32) - starts[se])
    n_blocks = -(-n_slots // MOE_BLOCK) + N_EXPERTS
    cap = n_blocks * MOE_BLOCK
    tok_buf = jnp.zeros((cap,), jnp.int32).at[dest].set(flat_tok[order])
    gate_buf = jnp.zeros((cap,), f32).at[dest].set(flat_g[order])
    block_start = jnp.arange(n_blocks, dtype=jnp.int32) * MOE_BLOCK
    block_e = jnp.minimum(jnp.searchsorted(pends, block_start, side='right'), N_EXPERTS - 1)

    def run_block(args):
        toks, e = args
        xb = xf[toks]
        return (jax.nn.silu(xb @ w_gate[e]) * (xb @ w_up[e])) @ w_down[e]

    yb = lax.map(run_block, (tok_buf.reshape(n_blocks, MOE_BLOCK), block_e))
    y = jnp.zeros((n_tok, d), f32).at[tok_buf].add(yb.reshape(cap, d).astype(f32) * gate_buf[:, None])
    return y.reshape(bsz, t_len, d).astype(h.dtype)


def setup_inputs(seed: int = 0) -> dict:
    key = jax.random.key(seed)
    ks = jax.random.split(key, 26)
    f32 = jnp.float32

    def nrm(k, shape, scale):
        return jax.random.normal(k, shape, f32) * scale

    x = nrm(ks[0], (BATCH, SEQ, D_MODEL), 1.0)
    meta_tokens = nrm(ks[1], (N_META, D_MODEL), 1.0)
    col_scale = jnp.concatenate([jnp.ones((ATTN_WIDTH + KV_WIDTH,), f32),
                                 jnp.full((KV_WIDTH,), BETA, f32),
                                 jnp.ones((2 * LRU_WIDTH,), f32)])
    w_in = nrm(ks[2], (DEPTH, D_MODEL, IN_COLS), D_MODEL ** -0.5) * col_scale
    conv_w = nrm(ks[3], (DEPTH, CONV_W, LRU_WIDTH), CONV_W ** -0.5)
    conv_b = nrm(ks[4], (DEPTH, LRU_WIDTH), 0.01)
    lru_wa = nrm(ks[5], (DEPTH, LRU_BLOCKS, LRU_BLOCK_W, LRU_BLOCK_W), LRU_BLOCK_W ** -0.5)
    lru_ba = nrm(ks[6], (DEPTH, LRU_BLOCKS, LRU_BLOCK_W), 0.01)
    lru_wx = nrm(ks[7], (DEPTH, LRU_BLOCKS, LRU_BLOCK_W, LRU_BLOCK_W), LRU_BLOCK_W ** -0.5)
    lru_bx = nrm(ks[8], (DEPTH, LRU_BLOCKS, LRU_BLOCK_W), 0.01)
    base = jax.random.uniform(ks[9], (DEPTH, LRU_WIDTH), f32, 0.9, 0.999)
    lru_lambda = jnp.log(base) - jnp.log1p(-base)
    attn_sinks = nrm(ks[10], (DEPTH, N_Q_HEADS), 1.0)
    g_attn = 1.0 + nrm(ks[11], (DEPTH, ATTN_WIDTH), 0.02)
    g_lru = 1.0 + nrm(ks[12], (DEPTH, LRU_WIDTH), 0.02)
    w_out = nrm(ks[13], (DEPTH, ATTN_WIDTH + LRU_WIDTH, D_MODEL), D_MODEL ** -0.5 * BETA)
    ln1_g = 1.0 + nrm(ks[14], (DEPTH, D_MODEL), 0.02)
    ln1_b = nrm(ks[15], (DEPTH, D_MODEL), 0.01)
    w_group = nrm(ks[16], (DEPTH, D_MODEL, N_GROUPS), D_MODEL ** -0.5)
    b_group = nrm(ks[17], (DEPTH, N_GROUPS), 0.01)
    w_router = nrm(ks[18], (DEPTH, N_GROUPS, D_MODEL, EXPERTS_PER_GROUP), D_MODEL ** -0.5)
    b_router = nrm(ks[19], (DEPTH, N_GROUPS, EXPERTS_PER_GROUP), 0.01)
    w_gate = nrm(ks[20], (DEPTH, N_EXPERTS, D_MODEL, D_FF_EXPERT), D_MODEL ** -0.5)
    w_up = nrm(ks[21], (DEPTH, N_EXPERTS, D_MODEL, D_FF_EXPERT), D_MODEL ** -0.5)
    w_down = nrm(ks[22], (DEPTH, N_EXPERTS, D_FF_EXPERT, D_MODEL), D_FF_EXPERT ** -0.5 * BETA)
    ln2_g = 1.0 + nrm(ks[23], (DEPTH, D_MODEL), 0.02)
    ln2_b = nrm(ks[24], (DEPTH, D_MODEL), 0.01)
    return {'x': x, 'meta_tokens': meta_tokens, 'w_in': w_in, 'conv_w': conv_w, 'conv_b': conv_b,
            'lru_wa': lru_wa, 'lru_ba': lru_ba, 'lru_wx': lru_wx, 'lru_bx': lru_bx,
            'lru_lambda': lru_lambda, 'attn_sinks': attn_sinks, 'g_attn': g_attn, 'g_lru': g_lru,
            'w_out': w_out, 'ln1_g': ln1_g, 'ln1_b': ln1_b, 'w_group': w_group, 'b_group': b_group,
            'w_router': w_router, 'b_router': b_router, 'w_gate': w_gate, 'w_up': w_up,
            'w_down': w_down, 'ln2_g': ln2_g, 'ln2_b': ln2_b}


def reference(x, meta_tokens, w_in, conv_w, conv_b, lru_wa, lru_ba, lru_wx, lru_bx, lru_lambda,
              attn_sinks, g_attn, g_lru, w_out, ln1_g, ln1_b, w_group, b_group, w_router,
              b_router, w_gate, w_up, w_down, ln2_g, ln2_b):
    bsz = x.shape[0]
    meta = jnp.broadcast_to(meta_tokens[None].astype(x.dtype), (bsz, N_META, D_MODEL))
    h = jnp.concatenate([meta, x], axis=1)
    for l in range(DEPTH):
        mix = token_mixer(h, w_in[l], conv_w[l], conv_b[l], lru_wa[l], lru_ba[l], lru_wx[l],
                          lru_bx[l], lru_lambda[l], attn_sinks[l], g_attn[l], g_lru[l], w_out[l])
        h = layer_norm(ALPHA * h + mix, ln1_g[l], ln1_b[l])
        ff = hierarchical_moe(h, w_group[l], b_group[l], w_router[l], b_router[l],
                              w_gate[l], w_up[l], w_down[l])
        h = layer_norm(ALPHA * h + ff, ln2_g[l], ln2_b[l])
    return h[:, N_META:]
```

```python
import functools

import jax
import jax.numpy as jnp
import numpy as np
from jax import lax
from jax.experimental import pallas as pl
from jax.experimental.pallas import tpu as pltpu

F32 = jnp.float32
BF16 = jnp.bfloat16

D_MODEL = 1024
N_META = 16
BLOCK = 128
PAD_FRONT = BLOCK - N_META
HEAD_DIM = 64
ATTN_WIDTH = 512
LRU_WIDTH = 512
N_Q_HEADS = 8
N_KV_HEADS = 2
KV_WIDTH = N_KV_HEADS * HEAD_DIM
LRU_BLOCKS = 8
CONV_W = 4
LRU_C = 8.0
IN_COLS = ATTN_WIDTH + 2 * KV_WIDTH + 2 * LRU_WIDTH
N_GROUPS = 4
EXPERTS_PER_GROUP = 8
N_EXPERTS = N_GROUPS * EXPERTS_PER_GROUP
TOP_K = 2
D_FF = 512
MOE_BLOCK = 128
ALPHA = 2.0 ** 0.25
EPS = 1e-5
NEG = -1e30
LANES = 128
SUBLANES = 8

PROJ_ROWS = 512
VMEM_LIMIT = 48 * 1024 * 1024


def _cparams(n_axes):
    return pltpu.CompilerParams(
        dimension_semantics=("arbitrary",) * n_axes, vmem_limit_bytes=VMEM_LIMIT)


def _in_proj_kernel(x_ref, w_ref, q_ref, kv_ref, xr_ref, yr_ref):
    proj = jnp.dot(x_ref[...].astype(BF16), w_ref[...], preferred_element_type=F32)
    o = 0
    for ref, width in ((q_ref, ATTN_WIDTH), (kv_ref, 2 * KV_WIDTH),
                       (xr_ref, LRU_WIDTH), (yr_ref, LRU_WIDTH)):
        ref[...] = proj[:, o:o + width].astype(ref.dtype)
        o += width


def _in_proj(x2d, w_bf16, rows):
    n = x2d.shape[0]
    widths = (ATTN_WIDTH, 2 * KV_WIDTH, LRU_WIDTH, LRU_WIDTH)
    return pl.pallas_call(
        _in_proj_kernel,
        grid=(n // rows,),
        in_specs=[pl.BlockSpec((rows, D_MODEL), lambda i: (i, 0)),
                  pl.BlockSpec((D_MODEL, IN_COLS), lambda i: (0, 0))],
        out_specs=[pl.BlockSpec((rows, w), lambda i: (i, 0)) for w in widths],
        out_shape=[jax.ShapeDtypeStruct((n, w), BF16) for w in widths],
        compiler_params=_cparams(1),
        name="in_proj",
    )(x2d, w_bf16)


def _attn_kernel(sinks_ref, q_ref, kv_ref, qm_ref, kvm_ref, bias_ref, g_ref, o_ref,
                 kvprev, qcur, kvcur):
    n = pl.program_id(1)

    @pl.when(n == 0)
    def _():
        qcur[...] = qm_ref[...]
        kvcur[...] = kvm_ref[...]
        kvprev[...] = jnp.zeros_like(kvprev)

    @pl.when(n > 0)
    def _():
        qcur[...] = q_ref[...]
        kvcur[...] = kv_ref[...]

    kvp = kvprev[...]
    kvc = kvcur[...]
    kvprev[...] = kvc
    kw = jnp.concatenate([kvp[:, :KV_WIDTH], kvc[:, :KV_WIDTH]], axis=0).astype(F32)
    vw = jnp.concatenate([kvp[:, KV_WIDTH:], kvc[:, KV_WIDTH:]], axis=0).astype(F32)
    kr = pltpu.roll(kw, HEAD_DIM, axis=1)
    vr = pltpu.roll(vw, HEAD_DIM, axis=1)
    lo = lax.broadcasted_iota(jnp.int32, kw.shape, 1) < HEAD_DIM
    zero = jnp.zeros_like(kw)

    def stacked(own, other, j):
        lo_src, hi_src = (own, other) if j == 0 else (other, own)
        return jnp.concatenate([jnp.where(lo, lo_src, zero), jnp.where(lo, zero, hi_src)],
                               axis=0).astype(BF16)

    kpos = n * BLOCK - BLOCK + lax.broadcasted_iota(jnp.int32, (BLOCK, 2 * BLOCK), 1)
    key_valid = kpos >= PAD_FRONT
    q = qcur[...]
    outs = []
    for pair in range(N_Q_HEADS // 2):
        j = pair // 2
        kc = stacked(kw, kr, j)
        vc = stacked(vw, vr, j)
        qp = q[:, pair * LANES:(pair + 1) * LANES]
        s = lax.dot_general(qp, kc, (((1,), (1,)), ((), ())), preferred_element_type=F32)
        probs = []
        for c in range(2):
            h = 2 * pair + c
            logits = s[:, c * 2 * BLOCK:(c + 1) * 2 * BLOCK] * (HEAD_DIM ** -0.5) + bias_ref[h]
            logits = jnp.where(key_valid, logits, NEG)
            sink = sinks_ref[h]
            m = jnp.maximum(jnp.max(logits, axis=1, keepdims=True), sink)
            p = jnp.exp(logits - m)
            denom = jnp.sum(p, axis=1, keepdims=True) + jnp.exp(sink - m)
            probs.append(p * (1.0 / denom))
        pp = jnp.concatenate(probs, axis=1).astype(BF16)
        outs.append(jnp.dot(pp, vc, preferred_element_type=F32))
    out = jnp.concatenate(outs, axis=1)
    ms = jnp.mean(out * out, axis=1, keepdims=True)
    o_ref[...] = (out * lax.rsqrt(ms + EPS) * g_ref[...]).astype(o_ref.dtype)


def _attention(sinks, q, kv, qm, kvm, bias, g_attn, bsz, nbx):
    main = lambda b, n: (b, jnp.maximum(n - 1, 0), 0)
    const2 = lambda b, n: (0, 0)
    return pl.pallas_call(
        _attn_kernel,
        grid=(bsz, nbx + 1),
        in_specs=[pl.BlockSpec(memory_space=pltpu.SMEM),
                  pl.BlockSpec((None, BLOCK, ATTN_WIDTH), main),
                  pl.BlockSpec((None, BLOCK, 2 * KV_WIDTH), main),
                  pl.BlockSpec((BLOCK, ATTN_WIDTH), const2),
                  pl.BlockSpec((BLOCK, 2 * KV_WIDTH), const2),
                  pl.BlockSpec((N_Q_HEADS, BLOCK, 2 * BLOCK), lambda b, n: (0, 0, 0)),
                  pl.BlockSpec((1, ATTN_WIDTH), const2)],
        out_specs=pl.BlockSpec((None, BLOCK, ATTN_WIDTH), main),
        out_shape=jax.ShapeDtypeStruct((bsz, nbx * BLOCK, ATTN_WIDTH), BF16),
        scratch_shapes=[pltpu.VMEM((BLOCK, 2 * KV_WIDTH), BF16),
                        pltpu.VMEM((BLOCK, ATTN_WIDTH), BF16),
                        pltpu.VMEM((BLOCK, 2 * KV_WIDTH), BF16)],
        compiler_params=_cparams(2),
        name="attention",
    )(sinks, q, kv, qm, kvm, bias, g_attn)


def _gelu_tanh(y):
    c = np.sqrt(2.0 / np.pi).astype(np.float32)
    return 0.5 * y * (1.0 + jnp.tanh(c * (y + 0.044715 * (y * y * y))))


def _lru_kernel(xr_ref, yr_ref, xrm_ref, yrm_ref, cw_ref, cb_ref, wa_ref, wx_ref, ba_ref,
                bx_ref, lam_ref, g_ref, o_ref, xbuf, ycur, hprev):
    n = pl.program_id(1)

    @pl.when(n == 0)
    def _():
        xbuf[0:SUBLANES, :] = jnp.zeros((SUBLANES, LRU_WIDTH), F32)
        xbuf[SUBLANES:, :] = xrm_ref[...].astype(F32)
        ycur[...] = yrm_ref[...].astype(F32)
        hprev[...] = jnp.zeros_like(hprev)

    @pl.when(n > 0)
    def _():
        xbuf[SUBLANES:, :] = xr_ref[...].astype(F32)
        ycur[...] = yr_ref[...].astype(F32)

    xc = jnp.broadcast_to(cb_ref[...], (BLOCK, LRU_WIDTH))
    for k in range(CONV_W):
        off = SUBLANES - (CONV_W - 1) + k
        xc = xc + cw_ref[k:k + 1, :] * xbuf[off:off + BLOCK, :]
    xbuf[0:SUBLANES, :] = xbuf[BLOCK:BLOCK + SUBLANES, :]

    xcb = xc.astype(BF16)
    r = jax.nn.sigmoid(jnp.dot(xcb, wa_ref[...], preferred_element_type=F32) + ba_ref[...])
    i = jax.nn.sigmoid(jnp.dot(xcb, wx_ref[...], preferred_element_type=F32) + bx_ref[...])
    lam = lam_ref[...]
    softplus_neg = jnp.maximum(-lam, 0.0) + jnp.log(1.0 + jnp.exp(-jnp.abs(lam)))
    a = jnp.exp(-LRU_C * r * softplus_neg)
    u = jnp.sqrt(1.0 - a * a) * (i * xc)
    row = lax.broadcasted_iota(jnp.int32, (BLOCK, LRU_WIDTH), 0)
    u = jnp.where((n > 0) | (row >= PAD_FRONT), u, 0.0)

    r8 = row & (SUBLANES - 1)
    for d in (1, 2, 4):
        keep = r8 >= d
        a_s = jnp.where(keep, pltpu.roll(a, d, axis=0), 1.0)
        u_s = jnp.where(keep, pltpu.roll(u, d, axis=0), 0.0)
        u = a * u_s + u
        a = a * a_s
    h_in = hprev[...]
    hs = []
    for g in range(BLOCK // SUBLANES):
        sl = slice(g * SUBLANES, (g + 1) * SUBLANES)
        hg = a[sl, :] * h_in + u[sl, :]
        hs.append(hg)
        h_in = jnp.broadcast_to(hg[SUBLANES - 1:SUBLANES, :], (SUBLANES, LRU_WIDTH))
    hprev[...] = h_in
    h = jnp.concatenate(hs, axis=0)

    out = h * _gelu_tanh(ycur[...])
    ms = jnp.mean(out * out, axis=1, keepdims=True)
    o_ref[...] = (out * lax.rsqrt(ms + EPS) * g_ref[...]).astype(o_ref.dtype)


def _rglru(xr, yr, xrm, yrm, cw, cb, wa, wx, ba, bx, lam, g_lru, bsz, nbx):
    main = lambda b, n: (b, jnp.maximum(n - 1, 0), 0)
    const2 = lambda b, n: (0, 0)
    row_spec = pl.BlockSpec((1, LRU_WIDTH), const2)
    return pl.pallas_call(
        _lru_kernel,
        grid=(bsz, nbx + 1),
        in_specs=[pl.BlockSpec((None, BLOCK, LRU_WIDTH), main),
                  pl.BlockSpec((None, BLOCK, LRU_WIDTH), main),
                  pl.BlockSpec((BLOCK, LRU_WIDTH), const2),
                  pl.BlockSpec((BLOCK, LRU_WIDTH), const2),
                  pl.BlockSpec((CONV_W, LRU_WIDTH), const2),
                  row_spec,
                  pl.BlockSpec((LRU_WIDTH, LRU_WIDTH), const2),
                  pl.BlockSpec((LRU_WIDTH, LRU_WIDTH), const2),
                  row_spec, row_spec, row_spec, row_spec],
        out_specs=pl.BlockSpec((None, BLOCK, LRU_WIDTH), main),
        out_shape=jax.ShapeDtypeStruct((bsz, nbx * BLOCK, LRU_WIDTH), BF16),
        scratch_shapes=[pltpu.VMEM((BLOCK + SUBLANES, LRU_WIDTH), F32),
                        pltpu.VMEM((BLOCK, LRU_WIDTH), F32),
                        pltpu.VMEM((SUBLANES, LRU_WIDTH), F32)],
        compiler_params=_cparams(2),
        name="rglru",
    )(xr, yr, xrm, yrm, cw, cb, wa, wx, ba, bx, lam, g_lru)


def _layer_norm(z, g, b):
    mu = jnp.mean(z, axis=1, keepdims=True)
    zc = z - mu
    var = jnp.mean(zc * zc, axis=1, keepdims=True)
    return zc * lax.rsqrt(var + EPS) * g + b


def _out_proj_kernel(a_ref, l_ref, x_ref, wa_ref, wl_ref, g_ref, b_ref, wrt_ref, brt_ref,
                     h_ref, info_ref):
    mix = jnp.dot(a_ref[...], wa_ref[...], preferred_element_type=F32)
    mix = mix + jnp.dot(l_ref[...], wl_ref[...], preferred_element_type=F32)
    h = _layer_norm(ALPHA * x_ref[...] + mix, g_ref[...], b_ref[...])
    h_ref[...] = h

    lg = jnp.dot(h, wrt_ref[...], preferred_element_type=F32,
                 precision=lax.Precision.HIGHEST) + brt_ref[...]
    lane = lax.broadcasted_iota(jnp.int32, lg.shape, 1)
    ninf = -jnp.inf
    gl = jnp.where(lane < N_GROUPS, lg, ninf)
    gmax = jnp.max(gl, axis=1, keepdims=True)
    g_idx = jnp.min(jnp.where(gl == gmax, lane, LANES), axis=1, keepdims=True)
    g_w = 1.0 / jnp.sum(jnp.where(lane < N_GROUPS, jnp.exp(lg - gmax), 0.0),
                        axis=1, keepdims=True)
    e_lo = N_GROUPS + EXPERTS_PER_GROUP * g_idx
    el = jnp.where((lane >= e_lo) & (lane < e_lo + EXPERTS_PER_GROUP), lg, ninf)
    v1 = jnp.max(el, axis=1, keepdims=True)
    i1 = jnp.min(jnp.where(el == v1, lane, LANES), axis=1, keepdims=True)
    el2 = jnp.where(lane == i1, ninf, el)
    v2 = jnp.max(el2, axis=1, keepdims=True)
    i2 = jnp.min(jnp.where(el2 == v2, lane, LANES), axis=1, keepdims=True)
    t = jnp.exp(v2 - v1)
    w1 = 1.0 / (1.0 + t)
    w2 = t * w1
    info = jnp.where(lane == 0, (i1 - N_GROUPS).astype(F32),
                     jnp.where(lane == 1, (i2 - N_GROUPS).astype(F32),
                               jnp.where(lane == 2, g_w * w1,
                                         jnp.where(lane == 3, g_w * w2, 0.0))))
    info_ref[...] = info


def _out_proj(attn_n, lru_n, x2d, wo_a, wo_l, ln_g, ln_b, w_rt, b_rt, rows):
    n = x2d.shape[0]
    const = lambda i: (0, 0)
    return pl.pallas_call(
        _out_proj_kernel,
        grid=(n // rows,),
        in_specs=[pl.BlockSpec((rows, ATTN_WIDTH), lambda i: (i, 0)),
                  pl.BlockSpec((rows, LRU_WIDTH), lambda i: (i, 0)),
                  pl.BlockSpec((rows, D_MODEL), lambda i: (i, 0)),
                  pl.BlockSpec((ATTN_WIDTH, D_MODEL), const),
                  pl.BlockSpec((LRU_WIDTH, D_MODEL), const),
                  pl.BlockSpec((1, D_MODEL), const),
                  pl.BlockSpec((1, D_MODEL), const),
                  pl.BlockSpec((D_MODEL, LANES), const),
                  pl.BlockSpec((1, LANES), const)],
        out_specs=[pl.BlockSpec((rows, D_MODEL), lambda i: (i, 0)),
                   pl.BlockSpec((rows, LANES), lambda i: (i, 0))],
        out_shape=[jax.ShapeDtypeStruct((n, D_MODEL), F32),
                   jax.ShapeDtypeStruct((n, LANES), F32)],
        compiler_params=_cparams(1),
        name="out_proj",
    )(attn_n, lru_n, x2d, wo_a, wo_l, ln_g, ln_b, w_rt, b_rt)


def _row_gather_start(src_hbm, idx_ref, dst, sem):
    def body(r, carry):
        pltpu.make_async_copy(src_hbm.at[pl.ds(idx_ref[0, 0, r], 1)],
                              dst.at[pl.ds(r, 1)], sem).start()
        return carry
    lax.fori_loop(0, MOE_BLOCK, body, 0)


def _row_gather_wait(src_hbm, dst, sem):
    pltpu.make_async_copy(src_hbm.at[pl.ds(0, MOE_BLOCK)], dst, sem).wait()


def _expert_kernel(be_ref, nused_ref, tok_ref, tok_next_ref, h_hbm, wg_ref, wu_ref, wd_ref,
                   y_ref, xbuf, sems, wg_bf, wu_bf, wd_bf):
    i = pl.program_id(0)
    nused = nused_ref[0]
    slot = i % 2

    @pl.when(i == 0)
    def _():
        _row_gather_start(h_hbm, tok_ref, xbuf.at[0], sems.at[0])

    @pl.when(i + 1 < nused)
    def _():
        _row_gather_start(h_hbm, tok_next_ref, xbuf.at[1 - slot], sems.at[1 - slot])

    new_expert = (i == 0) | (be_ref[i] != be_ref[jnp.maximum(i - 1, 0)])

    @pl.when(new_expert & (i < nused))
    def _():
        wg_bf[...] = wg_ref[...].astype(BF16)
        wu_bf[...] = wu_ref[...].astype(BF16)
        wd_bf[...] = wd_ref[...].astype(BF16)

    @pl.when(i < nused)
    def _():
        _row_gather_wait(h_hbm, xbuf.at[slot], sems.at[slot])
        xb = xbuf[slot].astype(BF16)
        g = jnp.dot(xb, wg_bf[...], preferred_element_type=F32)
        u = jnp.dot(xb, wu_bf[...], preferred_element_type=F32)
        mid = (g * jax.nn.sigmoid(g) * u).astype(BF16)
        y_ref[...] = jnp.dot(mid, wd_bf[...], preferred_element_type=F32)

    @pl.when(i >= nused)
    def _():
        y_ref[...] = jnp.zeros_like(y_ref)


def _experts(block_e, nused, tok3, h1, w_gate, w_up, w_down):
    n_blocks = tok3.shape[0]
    cap = n_blocks * MOE_BLOCK
    w_idx = lambda i, be, nu: (be[i], 0, 0)
    grid_spec = pltpu.PrefetchScalarGridSpec(
        num_scalar_prefetch=2,
        grid=(n_blocks,),
        in_specs=[pl.BlockSpec((1, 1, MOE_BLOCK), lambda i, be, nu: (i, 0, 0),
                               memory_space=pltpu.SMEM),
                  pl.BlockSpec((1, 1, MOE_BLOCK),
                               lambda i, be, nu: (jnp.minimum(i + 1, n_blocks - 1), 0, 0),
                               memory_space=pltpu.SMEM),
                  pl.BlockSpec(memory_space=pl.ANY),
                  pl.BlockSpec((None, D_MODEL, D_FF), w_idx),
                  pl.BlockSpec((None, D_MODEL, D_FF), w_idx),
                  pl.BlockSpec((None, D_FF, D_MODEL), w_idx)],
        out_specs=pl.BlockSpec((MOE_BLOCK, D_MODEL), lambda i, be, nu: (i, 0)),
        scratch_shapes=[pltpu.VMEM((2, MOE_BLOCK, D_MODEL), F32),
                        pltpu.SemaphoreType.DMA((2,)),
                        pltpu.VMEM((D_MODEL, D_FF), BF16),
                        pltpu.VMEM((D_MODEL, D_FF), BF16),
                        pltpu.VMEM((D_FF, D_MODEL), BF16)])
    return pl.pallas_call(
        _expert_kernel,
        grid_spec=grid_spec,
        out_shape=jax.ShapeDtypeStruct((cap, D_MODEL), F32),
        compiler_params=_cparams(1),
        name="experts",
    )(block_e, nused, tok3, tok3, h1, w_gate, w_up, w_down)


def _combine_kernel(d_ref, d_next_ref, y_hbm, h_ref, info_ref, g_ref, b_ref, o_ref,
                    ybuf, sems):
    i = pl.program_id(0)
    n_steps = pl.num_programs(0)
    slot = i % 2

    @pl.when(i == 0)
    def _():
        _row_gather_start(y_hbm, d_ref, ybuf.at[0], sems.at[0])

    @pl.when(i + 1 < n_steps)
    def _():
        _row_gather_start(y_hbm, d_next_ref, ybuf.at[1 - slot], sems.at[1 - slot])

    _row_gather_wait(y_hbm, ybuf.at[slot], sems.at[slot])
    half = MOE_BLOCK // TOP_K
    info = info_ref[...]
    y = (info[:, 2:3] * ybuf[slot, 0:half, :] + info[:, 3:4] * ybuf[slot, half:, :])
    o_ref[...] = _layer_norm(ALPHA * h_ref[...] + y, g_ref[...], b_ref[...])


def _combine(dest3, yb, h1, info, ln_g, ln_b):
    n = h1.shape[0]
    half = MOE_BLOCK // TOP_K
    n_steps = n // half
    const = lambda i: (0, 0)
    return pl.pallas_call(
        _combine_kernel,
        grid=(n_steps,),
        in_specs=[pl.BlockSpec((1, 1, MOE_BLOCK), lambda i: (i, 0, 0),
                               memory_space=pltpu.SMEM),
                  pl.BlockSpec((1, 1, MOE_BLOCK),
                               lambda i: (jnp.minimum(i + 1, n_steps - 1), 0, 0),
                               memory_space=pltpu.SMEM),
                  pl.BlockSpec(memory_space=pl.ANY),
                  pl.BlockSpec((half, D_MODEL), lambda i: (i, 0)),
                  pl.BlockSpec((half, LANES), lambda i: (i, 0)),
                  pl.BlockSpec((1, D_MODEL), const),
                  pl.BlockSpec((1, D_MODEL), const)],
        out_specs=pl.BlockSpec((half, D_MODEL), lambda i: (i, 0)),
        out_shape=jax.ShapeDtypeStruct((n, D_MODEL), F32),
        scratch_shapes=[pltpu.VMEM((2, MOE_BLOCK, D_MODEL), F32),
                        pltpu.SemaphoreType.DMA((2,))],
        compiler_params=_cparams(1),
        name="combine",
    )(dest3, dest3, yb, h1, info, ln_g, ln_b)


def _alibi_bias():
    qi = np.arange(BLOCK)[:, None]
    kj = np.arange(2 * BLOCK)[None, :]
    dist = qi - kj + BLOCK
    band = (dist >= 0) & (dist < BLOCK)
    slopes = np.exp2(-8.0 * np.arange(1, N_Q_HEADS + 1, dtype=np.float32) / N_Q_HEADS)
    bias = np.where(band[None], -slopes[:, None, None] * dist[None].astype(np.float32), NEG)
    return jnp.asarray(bias, F32)


def _block_diag(w):
    nb, c, _ = w.shape
    eye = jnp.eye(nb, dtype=w.dtype)
    return jnp.einsum('ncd,nm->ncmd', w, eye).reshape(nb * c, nb * c)


def kernel(x, meta_tokens, w_in, conv_w, conv_b, lru_wa, lru_ba, lru_wx, lru_bx, lru_lambda,
           attn_sinks, g_attn, g_lru, w_out, ln1_g, ln1_b, w_group, b_group, w_router,
           b_router, w_gate, w_up, w_down, ln2_g, ln2_b):
    bsz, seq, d = x.shape
    nbx = seq // BLOCK
    n_tok = bsz * seq
    x2d = x.reshape(n_tok, d)
    row = lambda v: v.reshape(1, -1).astype(F32)

    w_in_b = w_in[0].astype(BF16)
    meta_blk = jnp.concatenate([jnp.zeros((PAD_FRONT, d), F32), meta_tokens.astype(F32)], axis=0)
    q, kv, xr, yr = _in_proj(x2d, w_in_b, PROJ_ROWS)
    qm, kvm, xrm, yrm = _in_proj(meta_blk, w_in_b, BLOCK)
    shp = lambda a: a.reshape(bsz, seq, a.shape[-1])

    attn_n = _attention(attn_sinks[0].astype(F32), shp(q), shp(kv), qm, kvm, _alibi_bias(),
                        row(g_attn[0]), bsz, nbx)
    lru_n = _rglru(shp(xr), shp(yr), xrm, yrm, conv_w[0].astype(F32), row(conv_b[0]),
                   _block_diag(lru_wa[0]).astype(BF16), _block_diag(lru_wx[0]).astype(BF16),
                   row(lru_ba[0]), row(lru_bx[0]), row(lru_lambda[0]), row(g_lru[0]), bsz, nbx)

    w_out_b = w_out[0].astype(BF16)
    w_rt = jnp.concatenate(
        [w_group[0], jnp.transpose(w_router[0], (1, 0, 2)).reshape(d, N_EXPERTS),
         jnp.zeros((d, LANES - N_GROUPS - N_EXPERTS), F32)], axis=1).astype(F32)
    b_rt = jnp.concatenate([b_group[0], b_router[0].reshape(-1),
                            jnp.zeros((LANES - N_GROUPS - N_EXPERTS,), F32)]).reshape(1, LANES)
    h1, info = _out_proj(attn_n.reshape(n_tok, ATTN_WIDTH), lru_n.reshape(n_tok, LRU_WIDTH),
                         x2d, w_out_b[:ATTN_WIDTH], w_out_b[ATTN_WIDTH:], row(ln1_g[0]),
                         row(ln1_b[0]), w_rt, b_rt, PROJ_ROWS)

    n_slots = n_tok * TOP_K
    flat_e = info[:, :TOP_K].astype(jnp.int32).reshape(n_slots)
    onehot = (flat_e[:, None] == jnp.arange(N_EXPERTS, dtype=jnp.int32)[None, :]).astype(jnp.int32)
    csum = jnp.cumsum(onehot, axis=0)
    counts = csum[-1]
    padded = (counts + MOE_BLOCK - 1) // MOE_BLOCK * MOE_BLOCK
    pends = jnp.cumsum(padded)
    pstarts = pends - padded
    dest = jnp.sum(onehot * (csum + pstarts[None, :]), axis=1) - 1
    n_blocks = n_slots // MOE_BLOCK + N_EXPERTS
    cap = n_blocks * MOE_BLOCK
    tok_buf = jnp.zeros((cap,), jnp.int32).at[dest].set(
        jnp.arange(n_slots, dtype=jnp.int32) // TOP_K)
    block_start = jnp.arange(n_blocks, dtype=jnp.int32) * MOE_BLOCK
    block_e = jnp.minimum(jnp.searchsorted(pends, block_start, side='right'),
                          N_EXPERTS - 1).astype(jnp.int32)
    nused = (pends[-1:] // MOE_BLOCK).astype(jnp.int32)

    yb = _experts(block_e, nused, tok_buf.reshape(n_blocks, 1, MOE_BLOCK), h1,
                  w_gate[0], w_up[0], w_down[0])
    half = MOE_BLOCK // TOP_K
    dest3 = dest.reshape(n_tok // half, half, TOP_K).transpose(0, 2, 1).reshape(
        n_tok // half, 1, MOE_BLOCK)
    out = _combine(dest3, yb, h1, info, row(ln2_g[0]), row(ln2_b[0]))
    return out.reshape(bsz, seq, d)
```

```python
import jax
import jax.numpy as jnp
import numpy as np
from jax import lax
from jax.experimental import pallas as pl
from jax.experimental.pallas import tpu as pltpu
from jax.experimental.pallas import tpu_sc as plsc

F32 = jnp.float32
BF16 = jnp.bfloat16
U32 = jnp.uint32
I32 = jnp.int32

D_MODEL = 1024
N_META = 16
BLOCK = 128
PAD_FRONT = BLOCK - N_META
HEAD_DIM = 64
ATTN_WIDTH = 512
LRU_WIDTH = 512
N_Q_HEADS = 8
N_KV_HEADS = 2
KV_WIDTH = N_KV_HEADS * HEAD_DIM
LRU_BLOCKS = 8
CONV_W = 4
LRU_C = 8.0
IN_COLS = ATTN_WIDTH + 2 * KV_WIDTH + 2 * LRU_WIDTH
N_GROUPS = 4
EXPERTS_PER_GROUP = 8
N_EXPERTS = N_GROUPS * EXPERTS_PER_GROUP
TOP_K = 2
D_FF = 512
MOE_BLOCK = 128
ALPHA = 2.0 ** 0.25
EPS = 1e-5
NEG = -1e30
LANES = 128
SUBLANES = 8
PACKED = D_MODEL // 2

PROJ_ROWS = 512
ROUTE_ROWS = 512
COMBINE_ROWS = 256
VMEM_LIMIT = 48 * 1024 * 1024

SC_CORES = 2
SC_SUBCORES = 16
SC_WORKERS = SC_CORES * SC_SUBCORES
SC_WINDOW = 128


def _cparams(n_axes):
    return pltpu.CompilerParams(
        dimension_semantics=("arbitrary",) * n_axes, vmem_limit_bytes=VMEM_LIMIT)


def _in_proj_kernel(x_ref, w_ref, q_ref, kv_ref, xr_ref, yr_ref):
    proj = jnp.dot(x_ref[...].astype(BF16), w_ref[...], preferred_element_type=F32)
    o = 0
    for ref, width in ((q_ref, ATTN_WIDTH), (kv_ref, 2 * KV_WIDTH),
                       (xr_ref, LRU_WIDTH), (yr_ref, LRU_WIDTH)):
        ref[...] = proj[:, o:o + width].astype(ref.dtype)
        o += width


def _in_proj(x2d, w_bf16, rows):
    n = x2d.shape[0]
    widths = (ATTN_WIDTH, 2 * KV_WIDTH, LRU_WIDTH, LRU_WIDTH)
    return pl.pallas_call(
        _in_proj_kernel,
        grid=(n // rows,),
        in_specs=[pl.BlockSpec((rows, D_MODEL), lambda i: (i, 0)),
                  pl.BlockSpec((D_MODEL, IN_COLS), lambda i: (0, 0))],
        out_specs=[pl.BlockSpec((rows, w), lambda i: (i, 0)) for w in widths],
        out_shape=[jax.ShapeDtypeStruct((n, w), BF16) for w in widths],
        compiler_params=_cparams(1),
        name="in_proj",
    )(x2d, w_bf16)


def _attn_kernel(sinks_ref, q_ref, kv_ref, qm_ref, kvm_ref, bias_ref, g_ref, o_ref,
                 kvprev, qcur, kvcur):
    n = pl.program_id(1)

    @pl.when(n == 0)
    def _():
        qcur[...] = qm_ref[...]
        kvcur[...] = kvm_ref[...]
        kvprev[...] = jnp.zeros_like(kvprev)

    @pl.when(n > 0)
    def _():
        qcur[...] = q_ref[...]
        kvcur[...] = kv_ref[...]

    kvp = kvprev[...]
    kvc = kvcur[...]
    kvprev[...] = kvc
    kw = jnp.concatenate([kvp[:, :KV_WIDTH], kvc[:, :KV_WIDTH]], axis=0).astype(F32)
    vw = jnp.concatenate([kvp[:, KV_WIDTH:], kvc[:, KV_WIDTH:]], axis=0).astype(F32)
    kr = pltpu.roll(kw, HEAD_DIM, axis=1)
    vr = pltpu.roll(vw, HEAD_DIM, axis=1)
    lo = lax.broadcasted_iota(jnp.int32, kw.shape, 1) < HEAD_DIM
    zero = jnp.zeros_like(kw)

    def stacked(own, other, j):
        lo_src, hi_src = (own, other) if j == 0 else (other, own)
        return jnp.concatenate([jnp.where(lo, lo_src, zero), jnp.where(lo, zero, hi_src)],
                               axis=0).astype(BF16)

    kpos = n * BLOCK - BLOCK + lax.broadcasted_iota(jnp.int32, (BLOCK, 2 * BLOCK), 1)
    key_valid = kpos >= PAD_FRONT
    q = qcur[...]
    outs = []
    for pair in range(N_Q_HEADS // 2):
        j = pair // 2
        kc = stacked(kw, kr, j)
        vc = stacked(vw, vr, j)
        qp = q[:, pair * LANES:(pair + 1) * LANES]
        s = lax.dot_general(qp, kc, (((1,), (1,)), ((), ())), preferred_element_type=F32)
        probs = []
        for c in range(2):
            h = 2 * pair + c
            logits = s[:, c * 2 * BLOCK:(c + 1) * 2 * BLOCK] * (HEAD_DIM ** -0.5) + bias_ref[h]
            logits = jnp.where(key_valid, logits, NEG)
            sink = sinks_ref[h]
            m = jnp.maximum(jnp.max(logits, axis=1, keepdims=True), sink)
            p = jnp.exp(logits - m)
            denom = jnp.sum(p, axis=1, keepdims=True) + jnp.exp(sink - m)
            probs.append(p * (1.0 / denom))
        pp = jnp.concatenate(probs, axis=1).astype(BF16)
        outs.append(jnp.dot(pp, vc, preferred_element_type=F32))
    out = jnp.concatenate(outs, axis=1)
    ms = jnp.mean(out * out, axis=1, keepdims=True)
    o_ref[...] = (out * lax.rsqrt(ms + EPS) * g_ref[...]).astype(o_ref.dtype)


def _attention(sinks, q, kv, qm, kvm, bias, g_attn, bsz, nbx):
    main = lambda b, n: (b, jnp.maximum(n - 1, 0), 0)
    const2 = lambda b, n: (0, 0)
    return pl.pallas_call(
        _attn_kernel,
        grid=(bsz, nbx + 1),
        in_specs=[pl.BlockSpec(memory_space=pltpu.SMEM),
                  pl.BlockSpec((None, BLOCK, ATTN_WIDTH), main),
                  pl.BlockSpec((None, BLOCK, 2 * KV_WIDTH), main),
                  pl.BlockSpec((BLOCK, ATTN_WIDTH), const2),
                  pl.BlockSpec((BLOCK, 2 * KV_WIDTH), const2),
                  pl.BlockSpec((N_Q_HEADS, BLOCK, 2 * BLOCK), lambda b, n: (0, 0, 0)),
                  pl.BlockSpec((1, ATTN_WIDTH), const2)],
        out_specs=pl.BlockSpec((None, BLOCK, ATTN_WIDTH), main),
        out_shape=jax.ShapeDtypeStruct((bsz, nbx * BLOCK, ATTN_WIDTH), BF16),
        scratch_shapes=[pltpu.VMEM((BLOCK, 2 * KV_WIDTH), BF16),
                        pltpu.VMEM((BLOCK, ATTN_WIDTH), BF16),
                        pltpu.VMEM((BLOCK, 2 * KV_WIDTH), BF16)],
        compiler_params=_cparams(2),
        name="attention",
    )(sinks, q, kv, qm, kvm, bias, g_attn)


def _gelu_tanh(y):
    c = np.sqrt(2.0 / np.pi).astype(np.float32)
    return 0.5 * y * (1.0 + jnp.tanh(c * (y + 0.044715 * (y * y * y))))


def _lru_kernel(xr_ref, yr_ref, xrm_ref, yrm_ref, cw_ref, cb_ref, wa_ref, wx_ref, ba_ref,
                bx_ref, lam_ref, g_ref, o_ref, xbuf, ycur, hprev):
    n = pl.program_id(1)

    @pl.when(n == 0)
    def _():
        xbuf[0:SUBLANES, :] = jnp.zeros((SUBLANES, LRU_WIDTH), F32)
        xbuf[SUBLANES:, :] = xrm_ref[...].astype(F32)
        ycur[...] = yrm_ref[...].astype(F32)
        hprev[...] = jnp.zeros_like(hprev)

    @pl.when(n > 0)
    def _():
        xbuf[SUBLANES:, :] = xr_ref[...].astype(F32)
        ycur[...] = yr_ref[...].astype(F32)

    xc = jnp.broadcast_to(cb_ref[...], (BLOCK, LRU_WIDTH))
    for k in range(CONV_W):
        off = SUBLANES - (CONV_W - 1) + k
        xc = xc + cw_ref[k:k + 1, :] * xbuf[off:off + BLOCK, :]
    xbuf[0:SUBLANES, :] = xbuf[BLOCK:BLOCK + SUBLANES, :]

    xcb = xc.astype(BF16)
    r = jax.nn.sigmoid(jnp.dot(xcb, wa_ref[...], preferred_element_type=F32) + ba_ref[...])
    i = jax.nn.sigmoid(jnp.dot(xcb, wx_ref[...], preferred_element_type=F32) + bx_ref[...])
    lam = lam_ref[...]
    softplus_neg = jnp.maximum(-lam, 0.0) + jnp.log(1.0 + jnp.exp(-jnp.abs(lam)))
    a = jnp.exp(-LRU_C * r * softplus_neg)
    u = jnp.sqrt(1.0 - a * a) * (i * xc)
    row = lax.broadcasted_iota(jnp.int32, (BLOCK, LRU_WIDTH), 0)
    u = jnp.where((n > 0) | (row >= PAD_FRONT), u, 0.0)

    r8 = row & (SUBLANES - 1)
    for d in (1, 2, 4):
        keep = r8 >= d
        a_s = jnp.where(keep, pltpu.roll(a, d, axis=0), 1.0)
        u_s = jnp.where(keep, pltpu.roll(u, d, axis=0), 0.0)
        u = a * u_s + u
        a = a * a_s
    h_in = hprev[...]
    hs = []
    for g in range(BLOCK // SUBLANES):
        sl = slice(g * SUBLANES, (g + 1) * SUBLANES)
        hg = a[sl, :] * h_in + u[sl, :]
        hs.append(hg)
        h_in = jnp.broadcast_to(hg[SUBLANES - 1:SUBLANES, :], (SUBLANES, LRU_WIDTH))
    hprev[...] = h_in
    h = jnp.concatenate(hs, axis=0)

    out = h * _gelu_tanh(ycur[...])
    ms = jnp.mean(out * out, axis=1, keepdims=True)
    o_ref[...] = (out * lax.rsqrt(ms + EPS) * g_ref[...]).astype(o_ref.dtype)


def _rglru(xr, yr, xrm, yrm, cw, cb, wa, wx, ba, bx, lam, g_lru, bsz, nbx):
    main = lambda b, n: (b, jnp.maximum(n - 1, 0), 0)
    const2 = lambda b, n: (0, 0)
    row_spec = pl.BlockSpec((1, LRU_WIDTH), const2)
    return pl.pallas_call(
        _lru_kernel,
        grid=(bsz, nbx + 1),
        in_specs=[pl.BlockSpec((None, BLOCK, LRU_WIDTH), main),
                  pl.BlockSpec((None, BLOCK, LRU_WIDTH), main),
                  pl.BlockSpec((BLOCK, LRU_WIDTH), const2),
                  pl.BlockSpec((BLOCK, LRU_WIDTH), const2),
                  pl.BlockSpec((CONV_W, LRU_WIDTH), const2),
                  row_spec,
                  pl.BlockSpec((LRU_WIDTH, LRU_WIDTH), const2),
                  pl.BlockSpec((LRU_WIDTH, LRU_WIDTH), const2),
                  row_spec, row_spec, row_spec, row_spec],
        out_specs=pl.BlockSpec((None, BLOCK, LRU_WIDTH), main),
        out_shape=jax.ShapeDtypeStruct((bsz, nbx * BLOCK, LRU_WIDTH), BF16),
        scratch_shapes=[pltpu.VMEM((BLOCK + SUBLANES, LRU_WIDTH), F32),
                        pltpu.VMEM((BLOCK, LRU_WIDTH), F32),
                        pltpu.VMEM((SUBLANES, LRU_WIDTH), F32)],
        compiler_params=_cparams(2),
        name="rglru",
    )(xr, yr, xrm, yrm, cw, cb, wa, wx, ba, bx, lam, g_lru)


def _pack_rows(v):
    bits = lax.bitcast_convert_type(v.astype(BF16).astype(F32), U32)
    return (bits[:, :PACKED] >> 16) | (bits[:, PACKED:] & jnp.uint32(0xFFFF0000))


def _unpack_rows(w):
    lo = lax.bitcast_convert_type(w << 16, F32)
    hi = lax.bitcast_convert_type(w & jnp.uint32(0xFFFF0000), F32)
    return lo, hi


def _layer_norm(z, g, b):
    mu = jnp.mean(z, axis=1, keepdims=True)
    zc = z - mu
    var = jnp.mean(zc * zc, axis=1, keepdims=True)
    return zc * lax.rsqrt(var + EPS) * g + b


def _out_proj_kernel(a_ref, l_ref, x_ref, wa_ref, wl_ref, g_ref, b_ref, wrt_hi_ref, wrt_lo_ref,
                     brt_ref, h_ref, hp_ref, info_ref):
    mix = jnp.dot(a_ref[...], wa_ref[...], preferred_element_type=F32)
    mix = mix + jnp.dot(l_ref[...], wl_ref[...], preferred_element_type=F32)
    h = _layer_norm(ALPHA * x_ref[...] + mix, g_ref[...], b_ref[...])
    h_ref[...] = h
    hp_ref[...] = _pack_rows(h)

    h_hi = h.astype(BF16)
    h_lo = (h - h_hi.astype(F32)).astype(BF16)
    lg = (jnp.dot(h_hi, wrt_hi_ref[...], preferred_element_type=F32)
          + jnp.dot(h_hi, wrt_lo_ref[...], preferred_element_type=F32)
          + jnp.dot(h_lo, wrt_hi_ref[...], preferred_element_type=F32)) + brt_ref[...]
    lane = lax.broadcasted_iota(jnp.int32, lg.shape, 1)
    ninf = -jnp.inf
    gl = jnp.where(lane < N_GROUPS, lg, ninf)
    gmax = jnp.max(gl, axis=1, keepdims=True)
    g_idx = jnp.min(jnp.where(gl == gmax, lane, LANES), axis=1, keepdims=True)
    g_w = 1.0 / jnp.sum(jnp.where(lane < N_GROUPS, jnp.exp(lg - gmax), 0.0),
                        axis=1, keepdims=True)
    e_lo = N_GROUPS + EXPERTS_PER_GROUP * g_idx
    el = jnp.where((lane >= e_lo) & (lane < e_lo + EXPERTS_PER_GROUP), lg, ninf)
    v1 = jnp.max(el, axis=1, keepdims=True)
    i1 = jnp.min(jnp.where(el == v1, lane, LANES), axis=1, keepdims=True)
    el2 = jnp.where(lane == i1, ninf, el)
    v2 = jnp.max(el2, axis=1, keepdims=True)
    i2 = jnp.min(jnp.where(el2 == v2, lane, LANES), axis=1, keepdims=True)
    t = jnp.exp(v2 - v1)
    w1 = 1.0 / (1.0 + t)
    w2 = t * w1
    info = jnp.where(lane == 0, (i1 - N_GROUPS).astype(F32),
                     jnp.where(lane == 1, (i2 - N_GROUPS).astype(F32),
                               jnp.where(lane == 2, g_w * w1,
                                         jnp.where(lane == 3, g_w * w2, 0.0))))
    info_ref[...] = info


def _out_proj(attn_n, lru_n, x2d, wo_a, wo_l, ln_g, ln_b, w_rt_hi, w_rt_lo, b_rt, rows):
    n = x2d.shape[0]
    const = lambda i: (0, 0)
    tile = lambda w: pl.BlockSpec((rows, w), lambda i: (i, 0))
    return pl.pallas_call(
        _out_proj_kernel,
        grid=(n // rows,),
        in_specs=[tile(ATTN_WIDTH), tile(LRU_WIDTH), tile(D_MODEL),
                  pl.BlockSpec((ATTN_WIDTH, D_MODEL), const),
                  pl.BlockSpec((LRU_WIDTH, D_MODEL), const),
                  pl.BlockSpec((1, D_MODEL), const),
                  pl.BlockSpec((1, D_MODEL), const),
                  pl.BlockSpec((D_MODEL, LANES), const),
                  pl.BlockSpec((D_MODEL, LANES), const),
                  pl.BlockSpec((1, LANES), const)],
        out_specs=[tile(D_MODEL), tile(PACKED), tile(LANES)],
        out_shape=[jax.ShapeDtypeStruct((n, D_MODEL), F32),
                   jax.ShapeDtypeStruct((n, PACKED), U32),
                   jax.ShapeDtypeStruct((n, LANES), F32)],
        compiler_params=_cparams(1),
        name="out_proj",
    )(attn_n, lru_n, x2d, wo_a, wo_l, ln_g, ln_b, w_rt_hi, w_rt_lo, b_rt)


def _route_kernel(info_ref, tri_ref, dest_ref, cnt_ref, counts, carry, pstart):
    p = pl.program_id(0)
    t = pl.program_id(1)
    info = info_ref[...]
    lane = lax.broadcasted_iota(I32, info.shape, 1)
    oh1 = (lane == info[:, 0:1].astype(I32)).astype(F32)
    oh2 = (lane == info[:, 1:2].astype(I32)).astype(F32)
    both = oh1 + oh2
    tile_counts = jnp.sum(both, axis=0, keepdims=True)

    @pl.when((p == 0) & (t == 0))
    def _():
        counts[...] = jnp.zeros_like(counts)

    @pl.when(p == 0)
    def _():
        counts[...] += tile_counts

    @pl.when((p == 1) & (t == 0))
    def _():
        c = jnp.broadcast_to(counts[...], (SUBLANES, LANES)).astype(I32)
        padded = ((c + (MOE_BLOCK - 1)) // MOE_BLOCK) * MOE_BLOCK
        lane8 = lax.broadcasted_iota(I32, (SUBLANES, LANES), 1)
        scan = padded
        for d in (1, 2, 4, 8, 16, 32, 64):
            scan = scan + jnp.where(lane8 >= d, pltpu.roll(scan, d, axis=1), 0)
        pstart[...] = (scan - padded).astype(F32)
        carry[...] = jnp.zeros_like(carry)
        cnt_ref[...] = c

    @pl.when(p == 1)
    def _():
        before = jnp.dot(tri_ref[...], both.astype(BF16), preferred_element_type=F32)
        row_of = before + carry[...] + pstart[0:1, :]
        r1 = jnp.sum(oh1 * row_of, axis=1, keepdims=True)
        r2 = jnp.sum(oh2 * row_of, axis=1, keepdims=True)
        dest_ref[...] = jnp.where(lane == 0, r1, jnp.where(lane == 1, r2, 0.0)).astype(I32)
        carry[...] += tile_counts


def _route(info, rows):
    n = info.shape[0]
    tri = jnp.asarray(np.tril(np.ones((rows, rows), np.float32), -1), BF16)
    return pl.pallas_call(
        _route_kernel,
        grid=(2, n // rows),
        in_specs=[pl.BlockSpec((rows, LANES), lambda p, t: (t, 0)),
                  pl.BlockSpec((rows, rows), lambda p, t: (0, 0))],
        out_specs=[pl.BlockSpec((rows, LANES), lambda p, t: (t * p, 0)),
                   pl.BlockSpec((SUBLANES, LANES), lambda p, t: (0, 0))],
        out_shape=[jax.ShapeDtypeStruct((n, LANES), I32),
                   jax.ShapeDtypeStruct((SUBLANES, LANES), I32)],
        scratch_shapes=[pltpu.VMEM((1, LANES), F32), pltpu.VMEM((1, LANES), F32),
                        pltpu.VMEM((SUBLANES, LANES), F32)],
        compiler_params=_cparams(2),
        name="route",
    )(info, tri)


def _sc_mesh():
    return plsc.VectorSubcoreMesh(core_axis_name="core", subcore_axis_name="subcore")


def _sc_worker_id():
    return lax.axis_index("subcore") * SC_CORES + lax.axis_index("core")


def _sc_scatter_rows(rows, d0, d1, cap):
    n, width = rows.shape
    per_worker = n // SC_WORKERS
    n_win = per_worker // SC_WINDOW

    def body(x_hbm, d0_hbm, d1_hbm, o_hbm, i0_v, i1_v, rows_v):
        wid = _sc_worker_id()
        pltpu.sync_copy(d0_hbm.at[wid], i0_v)
        pltpu.sync_copy(d1_hbm.at[wid], i1_v)
        for j in range(n_win):
            pltpu.sync_copy(x_hbm.at[pl.ds(wid * per_worker + j * SC_WINDOW, SC_WINDOW)], rows_v)
            pltpu.sync_copy(rows_v, o_hbm.at[i0_v.at[j]])
            pltpu.sync_copy(rows_v, o_hbm.at[i1_v.at[j]])

    return pl.kernel(
        body,
        out_type=jax.ShapeDtypeStruct((cap, width), rows.dtype),
        mesh=_sc_mesh(),
        scratch_types=[pltpu.VMEM((n_win, SC_WINDOW), I32), pltpu.VMEM((n_win, SC_WINDOW), I32),
                       pltpu.VMEM((SC_WINDOW, width), rows.dtype)],
        name="dispatch",
    )(rows, d0, d1)


def _sc_gather_rows(table, idx):
    width = table.shape[1]
    n_win = idx.shape[1]
    per_worker = n_win * SC_WINDOW

    def body(y_hbm, i_hbm, o_hbm, i_v, rows_v):
        wid = _sc_worker_id()
        pltpu.sync_copy(i_hbm.at[wid], i_v)
        for j in range(n_win):
            pltpu.sync_copy(y_hbm.at[i_v.at[j]], rows_v)
            pltpu.sync_copy(rows_v, o_hbm.at[pl.ds(wid * per_worker + j * SC_WINDOW, SC_WINDOW)])

    return pl.kernel(
        body,
        out_type=jax.ShapeDtypeStruct((SC_WORKERS * per_worker, width), table.dtype),
        mesh=_sc_mesh(),
        scratch_types=[pltpu.VMEM((n_win, SC_WINDOW), I32),
                       pltpu.VMEM((SC_WINDOW, width), table.dtype)],
        name="collect",
    )(table, idx)


def _expert_weight_copies(w_hbm, bufs, sems, e, slot):
    return [pltpu.make_async_copy(w.at[e], buf.at[slot], sems.at[slot, k])
            for k, (w, buf) in enumerate(zip(w_hbm, bufs))]


def _expert_kernel(be_ref, nxt_ref, nused_ref, x_ref, wg_hbm, wu_hbm, wd_hbm, y_ref,
                   wg_f, wu_f, wd_f, sems, wg_b, wu_b, wd_b, run):
    i = pl.program_id(0)
    nused = nused_ref[0]
    e = be_ref[i]
    first = (i == 0) | (e != be_ref[jnp.maximum(i - 1, 0)])
    active = i < nused
    w_hbm = (wg_hbm, wu_hbm, wd_hbm)
    bufs = (wg_f, wu_f, wd_f)

    @pl.when(i == 0)
    def _():
        run[0] = 0
        for cp in _expert_weight_copies(w_hbm, bufs, sems, e, 0):
            cp.start()

    @pl.when(first & active)
    def _():
        slot = run[0] % 2
        for cp in _expert_weight_copies(w_hbm, bufs, sems, e, slot):
            cp.wait()
        wg_b[...] = wg_f[slot].astype(BF16)
        wu_b[...] = wu_f[slot].astype(BF16)
        wd_b[...] = wd_f[slot].astype(BF16)
        nxt = nxt_ref[i]

        @pl.when(nxt != e)
        def _():
            for cp in _expert_weight_copies(w_hbm, bufs, sems, nxt, 1 - slot):
                cp.start()

        run[0] = run[0] + 1

    @pl.when(active)
    def _():
        lo, hi = _unpack_rows(x_ref[...])
        lo = lo.astype(BF16)
        hi = hi.astype(BF16)
        g = (jnp.dot(lo, wg_b[0:PACKED, :], preferred_element_type=F32)
             + jnp.dot(hi, wg_b[PACKED:, :], preferred_element_type=F32))
        u = (jnp.dot(lo, wu_b[0:PACKED, :], preferred_element_type=F32)
             + jnp.dot(hi, wu_b[PACKED:, :], preferred_element_type=F32))
        mid = (g * jax.nn.sigmoid(g) * u).astype(BF16)
        y_ref[...] = _pack_rows(jnp.dot(mid, wd_b[...], preferred_element_type=F32))

    @pl.when(jnp.logical_not(active))
    def _():
        y_ref[...] = jnp.zeros_like(y_ref)


def _experts(block_e, next_e, nused, xs, w_gate, w_up, w_down):
    cap = xs.shape[0]
    n_blocks = cap // MOE_BLOCK
    grid_spec = pltpu.PrefetchScalarGridSpec(
        num_scalar_prefetch=3,
        grid=(n_blocks,),
        in_specs=[pl.BlockSpec((MOE_BLOCK, PACKED),
                               lambda i, be, nx, nu: (jnp.minimum(i, nu[0] - 1), 0)),
                  pl.BlockSpec(memory_space=pl.ANY),
                  pl.BlockSpec(memory_space=pl.ANY),
                  pl.BlockSpec(memory_space=pl.ANY)],
        out_specs=pl.BlockSpec((MOE_BLOCK, PACKED), lambda i, be, nx, nu: (i, 0)),
        scratch_shapes=[pltpu.VMEM((2, D_MODEL, D_FF), F32),
                        pltpu.VMEM((2, D_MODEL, D_FF), F32),
                        pltpu.VMEM((2, D_FF, D_MODEL), F32),
                        pltpu.SemaphoreType.DMA((2, 3)),
                        pltpu.VMEM((D_MODEL, D_FF), BF16),
                        pltpu.VMEM((D_MODEL, D_FF), BF16),
                        pltpu.VMEM((D_FF, D_MODEL), BF16),
                        pltpu.SMEM((1,), I32)])
    return pl.pallas_call(
        _expert_kernel,
        grid_spec=grid_spec,
        out_shape=jax.ShapeDtypeStruct((cap, PACKED), U32),
        compiler_params=_cparams(1),
        name="experts",
    )(block_e, next_e, nused, xs, w_gate, w_up, w_down)


def _combine_kernel(y0_ref, y1_ref, h_ref, info_ref, g_ref, b_ref, o_ref):
    info = info_ref[...]
    g0 = info[:, 2:3]
    g1 = info[:, 3:4]
    lo0, hi0 = _unpack_rows(y0_ref[...])
    lo1, hi1 = _unpack_rows(y1_ref[...])
    y = jnp.concatenate([g0 * lo0 + g1 * lo1, g0 * hi0 + g1 * hi1], axis=1)
    o_ref[...] = _layer_norm(ALPHA * h_ref[...] + y, g_ref[...], b_ref[...])


def _combine(ys, h1, info, ln_g, ln_b, rows):
    n = h1.shape[0]
    steps = n // rows
    const = lambda i: (0, 0)
    return pl.pallas_call(
        _combine_kernel,
        grid=(steps,),
        in_specs=[pl.BlockSpec((rows, PACKED), lambda i: (i, 0)),
                  pl.BlockSpec((rows, PACKED), lambda i: (i + steps, 0)),
                  pl.BlockSpec((rows, D_MODEL), lambda i: (i, 0)),
                  pl.BlockSpec((rows, LANES), lambda i: (i, 0)),
                  pl.BlockSpec((1, D_MODEL), const),
                  pl.BlockSpec((1, D_MODEL), const)],
        out_specs=pl.BlockSpec((rows, D_MODEL), lambda i: (i, 0)),
        out_shape=jax.ShapeDtypeStruct((n, D_MODEL), F32),
        compiler_params=_cparams(1),
        name="combine",
    )(ys, ys, h1, info, ln_g, ln_b)


def _alibi_bias():
    qi = np.arange(BLOCK)[:, None]
    kj = np.arange(2 * BLOCK)[None, :]
    dist = qi - kj + BLOCK
    band = (dist >= 0) & (dist < BLOCK)
    slopes = np.exp2(-8.0 * np.arange(1, N_Q_HEADS + 1, dtype=np.float32) / N_Q_HEADS)
    bias = np.where(band[None], -slopes[:, None, None] * dist[None].astype(np.float32), NEG)
    return jnp.asarray(bias, F32)


def _block_diag(w):
    nb, c, _ = w.shape
    eye = jnp.eye(nb, dtype=w.dtype)
    return jnp.einsum('ncd,nm->ncmd', w, eye).reshape(nb * c, nb * c)


def kernel(x, meta_tokens, w_in, conv_w, conv_b, lru_wa, lru_ba, lru_wx, lru_bx, lru_lambda,
           attn_sinks, g_attn, g_lru, w_out, ln1_g, ln1_b, w_group, b_group, w_router,
           b_router, w_gate, w_up, w_down, ln2_g, ln2_b):
    bsz, seq, d = x.shape
    nbx = seq // BLOCK
    n_tok = bsz * seq
    x2d = x.reshape(n_tok, d)
    row = lambda v: v.reshape(1, -1).astype(F32)

    w_in_b = w_in[0].astype(BF16)
    meta_blk = jnp.concatenate([jnp.zeros((PAD_FRONT, d), F32), meta_tokens.astype(F32)], axis=0)
    q, kv, xr, yr = _in_proj(x2d, w_in_b, PROJ_ROWS)
    qm, kvm, xrm, yrm = _in_proj(meta_blk, w_in_b, BLOCK)
    shp = lambda a: a.reshape(bsz, seq, a.shape[-1])

    attn_n = _attention(attn_sinks[0].astype(F32), shp(q), shp(kv), qm, kvm, _alibi_bias(),
                        row(g_attn[0]), bsz, nbx)
    lru_n = _rglru(shp(xr), shp(yr), xrm, yrm, conv_w[0].astype(F32), row(conv_b[0]),
                   _block_diag(lru_wa[0]).astype(BF16), _block_diag(lru_wx[0]).astype(BF16),
                   row(lru_ba[0]), row(lru_bx[0]), row(lru_lambda[0]), row(g_lru[0]), bsz, nbx)

    w_out_b = w_out[0].astype(BF16)
    w_rt = jnp.concatenate(
        [w_group[0], jnp.transpose(w_router[0], (1, 0, 2)).reshape(d, N_EXPERTS),
         jnp.zeros((d, LANES - N_GROUPS - N_EXPERTS), F32)], axis=1).astype(F32)
    w_rt_hi = w_rt.astype(BF16)
    w_rt_lo = (w_rt - w_rt_hi.astype(F32)).astype(BF16)
    b_rt = jnp.concatenate([b_group[0], b_router[0].reshape(-1),
                            jnp.zeros((LANES - N_GROUPS - N_EXPERTS,), F32)]).reshape(1, LANES)
    h1, hp, info = _out_proj(attn_n.reshape(n_tok, ATTN_WIDTH), lru_n.reshape(n_tok, LRU_WIDTH),
                             x2d, w_out_b[:ATTN_WIDTH], w_out_b[ATTN_WIDTH:], row(ln1_g[0]),
                             row(ln1_b[0]), w_rt_hi, w_rt_lo, b_rt, PROJ_ROWS)

    dest, cnt = _route(info, ROUTE_ROWS)
    n_slots = n_tok * TOP_K
    n_blocks = n_slots // MOE_BLOCK + N_EXPERTS
    cap = n_blocks * MOE_BLOCK
    counts = cnt[0, :N_EXPERTS]
    pends = jnp.cumsum((counts + MOE_BLOCK - 1) // MOE_BLOCK * MOE_BLOCK)
    nused = (pends[-1:] // MOE_BLOCK).astype(I32)
    block_start = jnp.arange(n_blocks, dtype=I32) * MOE_BLOCK
    block_e = jnp.minimum(jnp.searchsorted(pends, block_start, side='right'),
                          N_EXPERTS - 1).astype(I32)
    next_e = block_e[jnp.minimum(pends[block_e] // MOE_BLOCK, nused[0] - 1)]
    windows = lambda v: v.reshape(SC_WORKERS, -1, SC_WINDOW)
    d0 = dest[:, 0]
    d1 = dest[:, 1]

    xs = _sc_scatter_rows(hp, windows(d0), windows(d1), cap)
    yb = _experts(block_e, next_e, nused, xs, w_gate[0], w_up[0], w_down[0])
    ys = _sc_gather_rows(yb, windows(jnp.concatenate([d0, d1])))
    out = _combine(ys, h1, info, row(ln2_g[0]), row(ln2_b[0]), COMBINE_ROWS)
    return out.reshape(bsz, seq, d)
```

```python
import jax
import jax.numpy as jnp
import numpy as np
from jax import lax
from jax.experimental import pallas as pl
from jax.experimental.pallas import tpu as pltpu
from jax.experimental.pallas import tpu_sc as plsc

F32 = jnp.float32
BF16 = jnp.bfloat16
U32 = jnp.uint32
I32 = jnp.int32

D_MODEL = 1024
N_META = 16
BLOCK = 128
PAD_FRONT = BLOCK - N_META
HEAD_DIM = 64
ATTN_WIDTH = 512
LRU_WIDTH = 512
N_Q_HEADS = 8
N_KV_HEADS = 2
KV_WIDTH = N_KV_HEADS * HEAD_DIM
LRU_BLOCKS = 8
CONV_W = 4
LRU_C = 8.0
IN_COLS = ATTN_WIDTH + 2 * KV_WIDTH + 2 * LRU_WIDTH
N_GROUPS = 4
EXPERTS_PER_GROUP = 8
N_EXPERTS = N_GROUPS * EXPERTS_PER_GROUP
TOP_K = 2
D_FF = 512
MOE_BLOCK = 128
ALPHA = 2.0 ** 0.25
EPS = 1e-5
NEG = -1e30
LANES = 128
SUBLANES = 8
PACKED = D_MODEL // 2

PROJ_ROWS = 512
ROUTE_ROWS = 512
COMBINE_ROWS = 256
VMEM_LIMIT = 48 * 1024 * 1024

SC_CORES = 2
SC_SUBCORES = 16
SC_WORKERS = SC_CORES * SC_SUBCORES
SC_WINDOW = 128


def _cparams(n_axes):
    return pltpu.CompilerParams(
        dimension_semantics=("arbitrary",) * n_axes, vmem_limit_bytes=VMEM_LIMIT)


def _in_proj_kernel(x_ref, w_ref, q_ref, kv_ref, xr_ref, yr_ref):
    proj = jnp.dot(x_ref[...].astype(BF16), w_ref[...], preferred_element_type=F32)
    o = 0
    for ref, width in ((q_ref, ATTN_WIDTH), (kv_ref, 2 * KV_WIDTH),
                       (xr_ref, LRU_WIDTH), (yr_ref, LRU_WIDTH)):
        ref[...] = proj[:, o:o + width].astype(ref.dtype)
        o += width


def _in_proj(x2d, w_bf16, rows):
    n = x2d.shape[0]
    widths = (ATTN_WIDTH, 2 * KV_WIDTH, LRU_WIDTH, LRU_WIDTH)
    return pl.pallas_call(
        _in_proj_kernel,
        grid=(n // rows,),
        in_specs=[pl.BlockSpec((rows, D_MODEL), lambda i: (i, 0)),
                  pl.BlockSpec((D_MODEL, IN_COLS), lambda i: (0, 0))],
        out_specs=[pl.BlockSpec((rows, w), lambda i: (i, 0)) for w in widths],
        out_shape=[jax.ShapeDtypeStruct((n, w), BF16) for w in widths],
        compiler_params=_cparams(1),
        name="in_proj",
    )(x2d, w_bf16)


def _attn_kernel(sinks_ref, q_ref, kv_ref, qm_ref, kvm_ref, bias_ref, g_ref, o_ref,
                 kvprev, qcur, kvcur):
    n = pl.program_id(1)

    @pl.when(n == 0)
    def _():
        qcur[...] = qm_ref[...]
        kvcur[...] = kvm_ref[...]
        kvprev[...] = jnp.zeros_like(kvprev)

    @pl.when(n > 0)
    def _():
        qcur[...] = q_ref[...]
        kvcur[...] = kv_ref[...]

    kvp = kvprev[...]
    kvc = kvcur[...]
    kvprev[...] = kvc
    kw = jnp.concatenate([kvp[:, :KV_WIDTH], kvc[:, :KV_WIDTH]], axis=0).astype(F32)
    vw = jnp.concatenate([kvp[:, KV_WIDTH:], kvc[:, KV_WIDTH:]], axis=0).astype(F32)
    kr = pltpu.roll(kw, HEAD_DIM, axis=1)
    vr = pltpu.roll(vw, HEAD_DIM, axis=1)
    lo = lax.broadcasted_iota(jnp.int32, kw.shape, 1) < HEAD_DIM
    zero = jnp.zeros_like(kw)

    def stacked(own, other, j):
        lo_src, hi_src = (own, other) if j == 0 else (other, own)
        return jnp.concatenate([jnp.where(lo, lo_src, zero), jnp.where(lo, zero, hi_src)],
                               axis=0).astype(BF16)

    kpos = n * BLOCK - BLOCK + lax.broadcasted_iota(jnp.int32, (BLOCK, 2 * BLOCK), 1)
    key_valid = kpos >= PAD_FRONT
    q = qcur[...]
    outs = []
    for pair in range(N_Q_HEADS // 2):
        j = pair // 2
        kc = stacked(kw, kr, j)
        vc = stacked(vw, vr, j)
        qp = q[:, pair * LANES:(pair + 1) * LANES]
        s = lax.dot_general(qp, kc, (((1,), (1,)), ((), ())), preferred_element_type=F32)
        probs = []
        for c in range(2):
            h = 2 * pair + c
            logits = s[:, c * 2 * BLOCK:(c + 1) * 2 * BLOCK] * (HEAD_DIM ** -0.5) + bias_ref[h]
            logits = jnp.where(key_valid, logits, NEG)
            sink = sinks_ref[h]
            m = jnp.maximum(jnp.max(logits, axis=1, keepdims=True), sink)
            p = jnp.exp(logits - m)
            denom = jnp.sum(p, axis=1, keepdims=True) + jnp.exp(sink - m)
            probs.append(p * (1.0 / denom))
        pp = jnp.concatenate(probs, axis=1).astype(BF16)
        outs.append(jnp.dot(pp, vc, preferred_element_type=F32))
    out = jnp.concatenate(outs, axis=1)
    ms = jnp.mean(out * out, axis=1, keepdims=True)
    o_ref[...] = (out * lax.rsqrt(ms + EPS) * g_ref[...]).astype(o_ref.dtype)


def _attention(sinks, q, kv, qm, kvm, bias, g_attn, bsz, nbx):
    main = lambda b, n: (b, jnp.maximum(n - 1, 0), 0)
    const2 = lambda b, n: (0, 0)
    return pl.pallas_call(
        _attn_kernel,
        grid=(bsz, nbx + 1),
        in_specs=[pl.BlockSpec(memory_space=pltpu.SMEM),
                  pl.BlockSpec((None, BLOCK, ATTN_WIDTH), main),
                  pl.BlockSpec((None, BLOCK, 2 * KV_WIDTH), main),
                  pl.BlockSpec((BLOCK, ATTN_WIDTH), const2),
                  pl.BlockSpec((BLOCK, 2 * KV_WIDTH), const2),
                  pl.BlockSpec((N_Q_HEADS, BLOCK, 2 * BLOCK), lambda b, n: (0, 0, 0)),
                  pl.BlockSpec((1, ATTN_WIDTH), const2)],
        out_specs=pl.BlockSpec((None, BLOCK, ATTN_WIDTH), main),
        out_shape=jax.ShapeDtypeStruct((bsz, nbx * BLOCK, ATTN_WIDTH), BF16),
        scratch_shapes=[pltpu.VMEM((BLOCK, 2 * KV_WIDTH), BF16),
                        pltpu.VMEM((BLOCK, ATTN_WIDTH), BF16),
                        pltpu.VMEM((BLOCK, 2 * KV_WIDTH), BF16)],
        compiler_params=_cparams(2),
        name="attention",
    )(sinks, q, kv, qm, kvm, bias, g_attn)


def _gelu_tanh(y):
    c = np.sqrt(2.0 / np.pi).astype(np.float32)
    return 0.5 * y * (1.0 + jnp.tanh(c * (y + 0.044715 * (y * y * y))))


def _lru_kernel(xr_ref, yr_ref, xrm_ref, yrm_ref, cw_ref, cb_ref, wa_ref, wx_ref, ba_ref,
                bx_ref, lam_ref, g_ref, o_ref, xbuf, ycur, hprev):
    n = pl.program_id(1)

    @pl.when(n == 0)
    def _():
        xbuf[0:SUBLANES, :] = jnp.zeros((SUBLANES, LRU_WIDTH), F32)
        xbuf[SUBLANES:, :] = xrm_ref[...].astype(F32)
        ycur[...] = yrm_ref[...].astype(F32)
        hprev[...] = jnp.zeros_like(hprev)

    @pl.when(n > 0)
    def _():
        xbuf[SUBLANES:, :] = xr_ref[...].astype(F32)
        ycur[...] = yr_ref[...].astype(F32)

    xc = jnp.broadcast_to(cb_ref[...], (BLOCK, LRU_WIDTH))
    for k in range(CONV_W):
        off = SUBLANES - (CONV_W - 1) + k
        xc = xc + cw_ref[k:k + 1, :] * xbuf[off:off + BLOCK, :]
    xbuf[0:SUBLANES, :] = xbuf[BLOCK:BLOCK + SUBLANES, :]

    xcb = xc.astype(BF16)
    r = jax.nn.sigmoid(jnp.dot(xcb, wa_ref[...], preferred_element_type=F32) + ba_ref[...])
    i = jax.nn.sigmoid(jnp.dot(xcb, wx_ref[...], preferred_element_type=F32) + bx_ref[...])
    lam = lam_ref[...]
    softplus_neg = jnp.maximum(-lam, 0.0) + jnp.log(1.0 + jnp.exp(-jnp.abs(lam)))
    a = jnp.exp(-LRU_C * r * softplus_neg)
    u = jnp.sqrt(1.0 - a * a) * (i * xc)
    row = lax.broadcasted_iota(jnp.int32, (BLOCK, LRU_WIDTH), 0)
    u = jnp.where((n > 0) | (row >= PAD_FRONT), u, 0.0)

    r8 = row & (SUBLANES - 1)
    for d in (1, 2, 4):
        keep = r8 >= d
        a_s = jnp.where(keep, pltpu.roll(a, d, axis=0), 1.0)
        u_s = jnp.where(keep, pltpu.roll(u, d, axis=0), 0.0)
        u = a * u_s + u
        a = a * a_s
    h_in = hprev[...]
    hs = []
    for g in range(BLOCK // SUBLANES):
        sl = slice(g * SUBLANES, (g + 1) * SUBLANES)
        hg = a[sl, :] * h_in + u[sl, :]
        hs.append(hg)
        h_in = jnp.broadcast_to(hg[SUBLANES - 1:SUBLANES, :], (SUBLANES, LRU_WIDTH))
    hprev[...] = h_in
    h = jnp.concatenate(hs, axis=0)

    out = h * _gelu_tanh(ycur[...])
    ms = jnp.mean(out * out, axis=1, keepdims=True)
    o_ref[...] = (out * lax.rsqrt(ms + EPS) * g_ref[...]).astype(o_ref.dtype)


def _rglru(xr, yr, xrm, yrm, cw, cb, wa, wx, ba, bx, lam, g_lru, bsz, nbx):
    main = lambda b, n: (b, jnp.maximum(n - 1, 0), 0)
    const2 = lambda b, n: (0, 0)
    row_spec = pl.BlockSpec((1, LRU_WIDTH), const2)
    return pl.pallas_call(
        _lru_kernel,
        grid=(bsz, nbx + 1),
        in_specs=[pl.BlockSpec((None, BLOCK, LRU_WIDTH), main),
                  pl.BlockSpec((None, BLOCK, LRU_WIDTH), main),
                  pl.BlockSpec((BLOCK, LRU_WIDTH), const2),
                  pl.BlockSpec((BLOCK, LRU_WIDTH), const2),
                  pl.BlockSpec((CONV_W, LRU_WIDTH), const2),
                  row_spec,
                  pl.BlockSpec((LRU_WIDTH, LRU_WIDTH), const2),
                  pl.BlockSpec((LRU_WIDTH, LRU_WIDTH), const2),
                  row_spec, row_spec, row_spec, row_spec],
        out_specs=pl.BlockSpec((None, BLOCK, LRU_WIDTH), main),
        out_shape=jax.ShapeDtypeStruct((bsz, nbx * BLOCK, LRU_WIDTH), BF16),
        scratch_shapes=[pltpu.VMEM((BLOCK + SUBLANES, LRU_WIDTH), F32),
                        pltpu.VMEM((BLOCK, LRU_WIDTH), F32),
                        pltpu.VMEM((SUBLANES, LRU_WIDTH), F32)],
        compiler_params=_cparams(2),
        name="rglru",
    )(xr, yr, xrm, yrm, cw, cb, wa, wx, ba, bx, lam, g_lru)


def _pack_rows(v):
    bits = lax.bitcast_convert_type(v.astype(BF16).astype(F32), U32)
    return (bits[:, :PACKED] >> 16) | (bits[:, PACKED:] & jnp.uint32(0xFFFF0000))


def _unpack_rows(w):
    lo = lax.bitcast_convert_type(w << 16, F32)
    hi = lax.bitcast_convert_type(w & jnp.uint32(0xFFFF0000), F32)
    return lo, hi


def _layer_norm(z, g, b):
    mu = jnp.mean(z, axis=1, keepdims=True)
    zc = z - mu
    var = jnp.mean(zc * zc, axis=1, keepdims=True)
    return zc * lax.rsqrt(var + EPS) * g + b


def _out_proj_kernel(a_ref, l_ref, x_ref, wa_ref, wl_ref, g_ref, b_ref, wrt_hi_ref, wrt_lo_ref,
                     brt_ref, h_ref, hp_ref, info_ref):
    mix = jnp.dot(a_ref[...], wa_ref[...], preferred_element_type=F32)
    mix = mix + jnp.dot(l_ref[...], wl_ref[...], preferred_element_type=F32)
    h = _layer_norm(ALPHA * x_ref[...] + mix, g_ref[...], b_ref[...])
    h_ref[...] = h
    hp_ref[...] = _pack_rows(h)

    h_hi = h.astype(BF16)
    h_lo = (h - h_hi.astype(F32)).astype(BF16)
    lg = (jnp.dot(h_hi, wrt_hi_ref[...], preferred_element_type=F32)
          + jnp.dot(h_hi, wrt_lo_ref[...], preferred_element_type=F32)
          + jnp.dot(h_lo, wrt_hi_ref[...], preferred_element_type=F32)) + brt_ref[...]
    lane = lax.broadcasted_iota(jnp.int32, lg.shape, 1)
    ninf = -jnp.inf
    gl = jnp.where(lane < N_GROUPS, lg, ninf)
    gmax = jnp.max(gl, axis=1, keepdims=True)
    g_idx = jnp.min(jnp.where(gl == gmax, lane, LANES), axis=1, keepdims=True)
    g_w = 1.0 / jnp.sum(jnp.where(lane < N_GROUPS, jnp.exp(lg - gmax), 0.0),
                        axis=1, keepdims=True)
    e_lo = N_GROUPS + EXPERTS_PER_GROUP * g_idx
    el = jnp.where((lane >= e_lo) & (lane < e_lo + EXPERTS_PER_GROUP), lg, ninf)
    v1 = jnp.max(el, axis=1, keepdims=True)
    i1 = jnp.min(jnp.where(el == v1, lane, LANES), axis=1, keepdims=True)
    el2 = jnp.where(lane == i1, ninf, el)
    v2 = jnp.max(el2, axis=1, keepdims=True)
    i2 = jnp.min(jnp.where(el2 == v2, lane, LANES), axis=1, keepdims=True)
    t = jnp.exp(v2 - v1)
    w1 = 1.0 / (1.0 + t)
    w2 = t * w1
    info = jnp.where(lane == 0, (i1 - N_GROUPS).astype(F32),
                     jnp.where(lane == 1, (i2 - N_GROUPS).astype(F32),
                               jnp.where(lane == 2, g_w * w1,
                                         jnp.where(lane == 3, g_w * w2, 0.0))))
    info_ref[...] = info


def _out_proj(attn_n, lru_n, x2d, wo_a, wo_l, ln_g, ln_b, w_rt_hi, w_rt_lo, b_rt, rows):
    n = x2d.shape[0]
    const = lambda i: (0, 0)
    tile = lambda w: pl.BlockSpec((rows, w), lambda i: (i, 0))
    return pl.pallas_call(
        _out_proj_kernel,
        grid=(n // rows,),
        in_specs=[tile(ATTN_WIDTH), tile(LRU_WIDTH), tile(D_MODEL),
                  pl.BlockSpec((ATTN_WIDTH, D_MODEL), const),
                  pl.BlockSpec((LRU_WIDTH, D_MODEL), const),
                  pl.BlockSpec((1, D_MODEL), const),
                  pl.BlockSpec((1, D_MODEL), const),
                  pl.BlockSpec((D_MODEL, LANES), const),
                  pl.BlockSpec((D_MODEL, LANES), const),
                  pl.BlockSpec((1, LANES), const)],
        out_specs=[tile(D_MODEL), tile(PACKED), tile(LANES)],
        out_shape=[jax.ShapeDtypeStruct((n, D_MODEL), F32),
                   jax.ShapeDtypeStruct((n, PACKED), U32),
                   jax.ShapeDtypeStruct((n, LANES), F32)],
        compiler_params=_cparams(1),
        name="out_proj",
    )(attn_n, lru_n, x2d, wo_a, wo_l, ln_g, ln_b, w_rt_hi, w_rt_lo, b_rt)


def _route_kernel(info_ref, tri_ref, dest_ref, cnt_ref, counts, carry, pstart):
    p = pl.program_id(0)
    t = pl.program_id(1)
    info = info_ref[...]
    lane = lax.broadcasted_iota(I32, info.shape, 1)
    oh1 = (lane == info[:, 0:1].astype(I32)).astype(F32)
    oh2 = (lane == info[:, 1:2].astype(I32)).astype(F32)
    both = oh1 + oh2
    tile_counts = jnp.sum(both, axis=0, keepdims=True)

    @pl.when((p == 0) & (t == 0))
    def _():
        counts[...] = jnp.zeros_like(counts)

    @pl.when(p == 0)
    def _():
        counts[...] += tile_counts

    @pl.when((p == 1) & (t == 0))
    def _():
        c = jnp.broadcast_to(counts[...], (SUBLANES, LANES)).astype(I32)
        padded = ((c + (MOE_BLOCK - 1)) // MOE_BLOCK) * MOE_BLOCK
        lane8 = lax.broadcasted_iota(I32, (SUBLANES, LANES), 1)
        scan = padded
        for d in (1, 2, 4, 8, 16, 32, 64):
            scan = scan + jnp.where(lane8 >= d, pltpu.roll(scan, d, axis=1), 0)
        pstart[...] = (scan - padded).astype(F32)
        carry[...] = jnp.zeros_like(carry)
        cnt_ref[...] = c

    @pl.when(p == 1)
    def _():
        before = jnp.dot(tri_ref[...], both.astype(BF16), preferred_element_type=F32)
        row_of = before + carry[...] + pstart[0:1, :]
        r1 = jnp.sum(oh1 * row_of, axis=1, keepdims=True)
        r2 = jnp.sum(oh2 * row_of, axis=1, keepdims=True)
        dest_ref[...] = jnp.where(lane == 0, r1, jnp.where(lane == 1, r2, 0.0)).astype(I32)
        carry[...] += tile_counts


def _route(info, rows):
    n = info.shape[0]
    tri = jnp.asarray(np.tril(np.ones((rows, rows), np.float32), -1), BF16)
    return pl.pallas_call(
        _route_kernel,
        grid=(2, n // rows),
        in_specs=[pl.BlockSpec((rows, LANES), lambda p, t: (t, 0)),
                  pl.BlockSpec((rows, rows), lambda p, t: (0, 0))],
        out_specs=[pl.BlockSpec((rows, LANES), lambda p, t: (t * p, 0)),
                   pl.BlockSpec((SUBLANES, LANES), lambda p, t: (0, 0))],
        out_shape=[jax.ShapeDtypeStruct((n, LANES), I32),
                   jax.ShapeDtypeStruct((SUBLANES, LANES), I32)],
        scratch_shapes=[pltpu.VMEM((1, LANES), F32), pltpu.VMEM((1, LANES), F32),
                        pltpu.VMEM((SUBLANES, LANES), F32)],
        compiler_params=_cparams(2),
        name="route",
    )(info, tri)


def _sc_mesh():
    return plsc.VectorSubcoreMesh(core_axis_name="core", subcore_axis_name="subcore")


def _sc_worker_id():
    return lax.axis_index("subcore") * SC_CORES + lax.axis_index("core")


def _sc_scatter_rows(rows, d0, d1, cap):
    n, width = rows.shape
    per_worker = n // SC_WORKERS
    n_win = per_worker // SC_WINDOW

    def body(x_hbm, d0_hbm, d1_hbm, o_hbm, i0_v, i1_v, rows_v):
        wid = _sc_worker_id()
        pltpu.sync_copy(d0_hbm.at[wid], i0_v)
        pltpu.sync_copy(d1_hbm.at[wid], i1_v)
        for j in range(n_win):
            pltpu.sync_copy(x_hbm.at[pl.ds(wid * per_worker + j * SC_WINDOW, SC_WINDOW)], rows_v)
            pltpu.sync_copy(rows_v, o_hbm.at[i0_v.at[j]])
            pltpu.sync_copy(rows_v, o_hbm.at[i1_v.at[j]])

    return pl.kernel(
        body,
        out_type=jax.ShapeDtypeStruct((cap, width), rows.dtype),
        mesh=_sc_mesh(),
        scratch_types=[pltpu.VMEM((n_win, SC_WINDOW), I32), pltpu.VMEM((n_win, SC_WINDOW), I32),
                       pltpu.VMEM((SC_WINDOW, width), rows.dtype)],
        name="dispatch",
    )(rows, d0, d1)


def _sc_gather_rows(table, idx):
    width = table.shape[1]
    n_win = idx.shape[1]
    per_worker = n_win * SC_WINDOW

    def body(y_hbm, i_hbm, o_hbm, i_v, rows_v):
        wid = _sc_worker_id()
        pltpu.sync_copy(i_hbm.at[wid], i_v)
        for j in range(n_win):
            pltpu.sync_copy(y_hbm.at[i_v.at[j]], rows_v)
            pltpu.sync_copy(rows_v, o_hbm.at[pl.ds(wid * per_worker + j * SC_WINDOW, SC_WINDOW)])

    return pl.kernel(
        body,
        out_type=jax.ShapeDtypeStruct((SC_WORKERS * per_worker, width), table.dtype),
        mesh=_sc_mesh(),
        scratch_types=[pltpu.VMEM((n_win, SC_WINDOW), I32),
                       pltpu.VMEM((SC_WINDOW, width), table.dtype)],
        name="collect",
    )(table, idx)


def _expert_kernel(bstart_ref, nblk_ref, nused_ref, xs_hbm, wg_ref, wu_ref, wd_ref, yb_hbm,
                   xbuf, ybuf, zbuf, xsem, ysem, zsem, wg_b, wu_b, wd_b):
    e = pl.program_id(0)
    nused = nused_ref[0]
    n_blocks = yb_hbm.shape[0] // MOE_BLOCK

    def rows(b):
        return pl.ds(pl.multiple_of(b * MOE_BLOCK, MOE_BLOCK), MOE_BLOCK)

    def x_copy(b, slot):
        return pltpu.make_async_copy(xs_hbm.at[rows(b)], xbuf.at[slot], xsem.at[slot])

    def y_copy(b, slot):
        return pltpu.make_async_copy(ybuf.at[slot], yb_hbm.at[rows(b)], ysem.at[slot])

    @pl.when((e == 0) & (nused > 0))
    def _():
        x_copy(0, 0).start()

    wg_b[...] = wg_ref[...].astype(BF16)
    wu_b[...] = wu_ref[...].astype(BF16)
    wd_b[...] = wd_ref[...].astype(BF16)

    def block(b, carry):
        slot = b % 2
        x_copy(b, slot).wait()

        @pl.when(b + 1 < nused)
        def _():
            x_copy(b + 1, 1 - slot).start()

        lo, hi = _unpack_rows(xbuf[slot])
        lo = lo.astype(BF16)
        hi = hi.astype(BF16)
        g = (jnp.dot(lo, wg_b[0:PACKED, :], preferred_element_type=F32)
             + jnp.dot(hi, wg_b[PACKED:, :], preferred_element_type=F32))
        u = (jnp.dot(lo, wu_b[0:PACKED, :], preferred_element_type=F32)
             + jnp.dot(hi, wu_b[PACKED:, :], preferred_element_type=F32))
        mid = (g * jax.nn.sigmoid(g) * u).astype(BF16)
        y = _pack_rows(jnp.dot(mid, wd_b[...], preferred_element_type=F32))

        @pl.when(b >= 2)
        def _():
            y_copy(b - 2, slot).wait()

        ybuf[slot] = y
        y_copy(b, slot).start()
        return carry

    b0 = bstart_ref[e]
    lax.fori_loop(b0, b0 + nblk_ref[e], block, 0)

    @pl.when(e == pl.num_programs(0) - 1)
    def _():
        for back in (2, 1):
            @pl.when(nused >= back)
            def _():
                y_copy(nused - back, (nused - back) % 2).wait()

        zbuf[...] = jnp.zeros_like(zbuf)

        def z_copy(b):
            return pltpu.make_async_copy(zbuf, yb_hbm.at[rows(b)], zsem.at[0])

        def z_start(b, carry):
            z_copy(b).start()
            return carry

        def z_wait(b, carry):
            z_copy(b).wait()
            return carry

        lax.fori_loop(nused, n_blocks, z_start, 0)
        lax.fori_loop(nused, n_blocks, z_wait, 0)


def _experts(bstart, nblk, nused, xs, w_gate, w_up, w_down):
    cap = xs.shape[0]
    w_idx = lambda e, bs, nb, nu: (e, 0, 0)
    grid_spec = pltpu.PrefetchScalarGridSpec(
        num_scalar_prefetch=3,
        grid=(N_EXPERTS,),
        in_specs=[pl.BlockSpec(memory_space=pl.ANY),
                  pl.BlockSpec((None, D_MODEL, D_FF), w_idx),
                  pl.BlockSpec((None, D_MODEL, D_FF), w_idx),
                  pl.BlockSpec((None, D_FF, D_MODEL), w_idx)],
        out_specs=pl.BlockSpec(memory_space=pl.ANY),
        scratch_shapes=[pltpu.VMEM((2, MOE_BLOCK, PACKED), U32),
                        pltpu.VMEM((2, MOE_BLOCK, PACKED), U32),
                        pltpu.VMEM((MOE_BLOCK, PACKED), U32),
                        pltpu.SemaphoreType.DMA((2,)),
                        pltpu.SemaphoreType.DMA((2,)),
                        pltpu.SemaphoreType.DMA((1,)),
                        pltpu.VMEM((D_MODEL, D_FF), BF16),
                        pltpu.VMEM((D_MODEL, D_FF), BF16),
                        pltpu.VMEM((D_FF, D_MODEL), BF16)])
    return pl.pallas_call(
        _expert_kernel,
        grid_spec=grid_spec,
        out_shape=jax.ShapeDtypeStruct((cap, PACKED), U32),
        compiler_params=_cparams(1),
        name="experts",
    )(bstart, nblk, nused, xs, w_gate, w_up, w_down)


def _combine_kernel(y0_ref, y1_ref, h_ref, info_ref, g_ref, b_ref, o_ref):
    info = info_ref[...]
    g0 = info[:, 2:3]
    g1 = info[:, 3:4]
    lo0, hi0 = _unpack_rows(y0_ref[...])
    lo1, hi1 = _unpack_rows(y1_ref[...])
    y = jnp.concatenate([g0 * lo0 + g1 * lo1, g0 * hi0 + g1 * hi1], axis=1)
    o_ref[...] = _layer_norm(ALPHA * h_ref[...] + y, g_ref[...], b_ref[...])


def _combine(ys, h1, info, ln_g, ln_b, rows):
    n = h1.shape[0]
    steps = n // rows
    const = lambda i: (0, 0)
    return pl.pallas_call(
        _combine_kernel,
        grid=(steps,),
        in_specs=[pl.BlockSpec((rows, PACKED), lambda i: (i, 0)),
                  pl.BlockSpec((rows, PACKED), lambda i: (i + steps, 0)),
                  pl.BlockSpec((rows, D_MODEL), lambda i: (i, 0)),
                  pl.BlockSpec((rows, LANES), lambda i: (i, 0)),
                  pl.BlockSpec((1, D_MODEL), const),
                  pl.BlockSpec((1, D_MODEL), const)],
        out_specs=pl.BlockSpec((rows, D_MODEL), lambda i: (i, 0)),
        out_shape=jax.ShapeDtypeStruct((n, D_MODEL), F32),
        compiler_params=_cparams(1),
        name="combine",
    )(ys, ys, h1, info, ln_g, ln_b)


def _alibi_bias():
    qi = np.arange(BLOCK)[:, None]
    kj = np.arange(2 * BLOCK)[None, :]
    dist = qi - kj + BLOCK
    band = (dist >= 0) & (dist < BLOCK)
    slopes = np.exp2(-8.0 * np.arange(1, N_Q_HEADS + 1, dtype=np.float32) / N_Q_HEADS)
    bias = np.where(band[None], -slopes[:, None, None] * dist[None].astype(np.float32), NEG)
    return jnp.asarray(bias, F32)


def _block_diag(w):
    nb, c, _ = w.shape
    eye = jnp.eye(nb, dtype=w.dtype)
    return jnp.einsum('ncd,nm->ncmd', w, eye).reshape(nb * c, nb * c)


def kernel(x, meta_tokens, w_in, conv_w, conv_b, lru_wa, lru_ba, lru_wx, lru_bx, lru_lambda,
           attn_sinks, g_attn, g_lru, w_out, ln1_g, ln1_b, w_group, b_group, w_router,
           b_router, w_gate, w_up, w_down, ln2_g, ln2_b):
    bsz, seq, d = x.shape
    nbx = seq // BLOCK
    n_tok = bsz * seq
    x2d = x.reshape(n_tok, d)
    row = lambda v: v.reshape(1, -1).astype(F32)

    w_in_b = w_in[0].astype(BF16)
    meta_blk = jnp.concatenate([jnp.zeros((PAD_FRONT, d), F32), meta_tokens.astype(F32)], axis=0)
    q, kv, xr, yr = _in_proj(x2d, w_in_b, PROJ_ROWS)
    qm, kvm, xrm, yrm = _in_proj(meta_blk, w_in_b, BLOCK)
    shp = lambda a: a.reshape(bsz, seq, a.shape[-1])

    attn_n = _attention(attn_sinks[0].astype(F32), shp(q), shp(kv), qm, kvm, _alibi_bias(),
                        row(g_attn[0]), bsz, nbx)
    lru_n = _rglru(shp(xr), shp(yr), xrm, yrm, conv_w[0].astype(F32), row(conv_b[0]),
                   _block_diag(lru_wa[0]).astype(BF16), _block_diag(lru_wx[0]).astype(BF16),
                   row(lru_ba[0]), row(lru_bx[0]), row(lru_lambda[0]), row(g_lru[0]), bsz, nbx)

    w_out_b = w_out[0].astype(BF16)
    w_rt = jnp.concatenate(
        [w_group[0], jnp.transpose(w_router[0], (1, 0, 2)).reshape(d, N_EXPERTS),
         jnp.zeros((d, LANES - N_GROUPS - N_EXPERTS), F32)], axis=1).astype(F32)
    w_rt_hi = w_rt.astype(BF16)
    w_rt_lo = (w_rt - w_rt_hi.astype(F32)).astype(BF16)
    b_rt = jnp.concatenate([b_group[0], b_router[0].reshape(-1),
                            jnp.zeros((LANES - N_GROUPS - N_EXPERTS,), F32)]).reshape(1, LANES)
    h1, hp, info = _out_proj(attn_n.reshape(n_tok, ATTN_WIDTH), lru_n.reshape(n_tok, LRU_WIDTH),
                             x2d, w_out_b[:ATTN_WIDTH], w_out_b[ATTN_WIDTH:], row(ln1_g[0]),
                             row(ln1_b[0]), w_rt_hi, w_rt_lo, b_rt, PROJ_ROWS)

    dest, cnt = _route(info, ROUTE_ROWS)
    n_slots = n_tok * TOP_K
    n_blocks = n_slots // MOE_BLOCK + N_EXPERTS
    cap = n_blocks * MOE_BLOCK
    nblk = (cnt[0, :N_EXPERTS] + MOE_BLOCK - 1) // MOE_BLOCK
    bends = jnp.cumsum(nblk)
    bstart = (bends - nblk).astype(I32)
    nused = bends[-1:].astype(I32)
    windows = lambda v: v.reshape(SC_WORKERS, -1, SC_WINDOW)
    d0 = dest[:, 0]
    d1 = dest[:, 1]

    xs = _sc_scatter_rows(hp, windows(d0), windows(d1), cap)
    yb = _experts(bstart, nblk.astype(I32), nused, xs, w_gate[0], w_up[0], w_down[0])
    ys = _sc_gather_rows(yb, windows(jnp.concatenate([d0, d1])))
    out = _combine(ys, h1, info, row(ln2_g[0]), row(ln2_b[0]), COMBINE_ROWS)
    return out.reshape(bsz, seq, d)
```

```python
import jax
import jax.numpy as jnp
import numpy as np
from jax import lax
from jax.experimental import pallas as pl
from jax.experimental.pallas import tpu as pltpu
from jax.experimental.pallas import tpu_sc as plsc

F32 = jnp.float32
BF16 = jnp.bfloat16
U32 = jnp.uint32
I32 = jnp.int32

D_MODEL = 1024
N_META = 16
BLOCK = 128
PAD_FRONT = BLOCK - N_META
HEAD_DIM = 64
ATTN_WIDTH = 512
LRU_WIDTH = 512
N_Q_HEADS = 8
N_KV_HEADS = 2
KV_WIDTH = N_KV_HEADS * HEAD_DIM
LRU_BLOCKS = 8
CONV_W = 4
LRU_C = 8.0
IN_COLS = ATTN_WIDTH + 2 * KV_WIDTH + 2 * LRU_WIDTH
N_GROUPS = 4
EXPERTS_PER_GROUP = 8
N_EXPERTS = N_GROUPS * EXPERTS_PER_GROUP
TOP_K = 2
D_FF = 512
MOE_BLOCK = 256
ALPHA = 2.0 ** 0.25
EPS = 1e-5
NEG = -1e30
LANES = 128
SUBLANES = 8
PACKED = D_MODEL // 2

PROJ_ROWS = 512
ROUTE_ROWS = 512
COMBINE_ROWS = 256
X_RING = 4
VMEM_LIMIT = 48 * 1024 * 1024

SC_CORES = 2
SC_SUBCORES = 16
SC_WORKERS = SC_CORES * SC_SUBCORES
SC_WINDOW = 128


def _cparams(n_axes):
    return pltpu.CompilerParams(
        dimension_semantics=("arbitrary",) * n_axes, vmem_limit_bytes=VMEM_LIMIT)


def _in_proj_kernel(x_ref, w_ref, q_ref, kv_ref, xr_ref, yr_ref):
    proj = jnp.dot(x_ref[...].astype(BF16), w_ref[...], preferred_element_type=F32)
    o = 0
    for ref, width in ((q_ref, ATTN_WIDTH), (kv_ref, 2 * KV_WIDTH),
                       (xr_ref, LRU_WIDTH), (yr_ref, LRU_WIDTH)):
        ref[...] = proj[:, o:o + width].astype(ref.dtype)
        o += width


def _in_proj(x2d, w_bf16, rows):
    n = x2d.shape[0]
    widths = (ATTN_WIDTH, 2 * KV_WIDTH, LRU_WIDTH, LRU_WIDTH)
    return pl.pallas_call(
        _in_proj_kernel,
        grid=(n // rows,),
        in_specs=[pl.BlockSpec((rows, D_MODEL), lambda i: (i, 0)),
                  pl.BlockSpec((D_MODEL, IN_COLS), lambda i: (0, 0))],
        out_specs=[pl.BlockSpec((rows, w), lambda i: (i, 0)) for w in widths],
        out_shape=[jax.ShapeDtypeStruct((n, w), BF16) for w in widths],
        compiler_params=_cparams(1),
        name="in_proj",
    )(x2d, w_bf16)


def _attn_kernel(sinks_ref, q_ref, kv_ref, kvm_ref, bias_ref, g_ref, o_ref,
                 klo, khi, vlo, vhi):
    nbx = q_ref.shape[0] // BLOCK
    lo_lanes = lax.broadcasted_iota(I32, (BLOCK, LANES), 1) < HEAD_DIM

    def layout_block(n, blk):
        rows = pl.ds(pl.multiple_of(n * BLOCK, BLOCK), BLOCK)
        for src, dst_lo, dst_hi in ((blk[:, :KV_WIDTH], klo, khi), (blk[:, KV_WIDTH:], vlo, vhi)):
            w = src.astype(F32)
            r = pltpu.roll(w, HEAD_DIM, axis=1)
            dst_lo[0, rows, :] = jnp.where(lo_lanes, w, 0.0).astype(BF16)
            dst_hi[0, rows, :] = jnp.where(lo_lanes, 0.0, r).astype(BF16)
            dst_lo[1, rows, :] = jnp.where(lo_lanes, r, 0.0).astype(BF16)
            dst_hi[1, rows, :] = jnp.where(lo_lanes, 0.0, w).astype(BF16)

    layout_block(0, kvm_ref[...])

    def layout_body(n, carry):
        layout_block(n + 1, kv_ref[pl.ds(pl.multiple_of(n * BLOCK, BLOCK), BLOCK), :])
        return carry

    lax.fori_loop(0, nbx, layout_body, 0)

    ones_lo = jnp.where(lax.broadcasted_iota(I32, (2 * BLOCK, LANES), 1) < HEAD_DIM,
                        1.0, 0.0).astype(BF16)
    ones_hi = (1.0 - ones_lo.astype(F32)).astype(BF16)
    top_rows = lax.broadcasted_iota(I32, (2 * BLOCK, 1), 0) < BLOCK
    lo_half = lax.broadcasted_iota(I32, (2 * BLOCK, LANES), 1) < HEAD_DIM

    def block(i, carry):
        q_rows = pl.ds(pl.multiple_of(i * BLOCK, BLOCK), BLOCK)
        win = pl.ds(pl.multiple_of(i * BLOCK, BLOCK), 2 * BLOCK)
        q = q_ref[q_rows, :] * (HEAD_DIM ** -0.5)
        first = jnp.minimum(i, 1)
        outs = []
        for j in range(N_KV_HEADS):
            q2 = jnp.concatenate([q[:, (2 * j) * LANES:(2 * j + 1) * LANES],
                                  q[:, (2 * j + 1) * LANES:(2 * j + 2) * LANES]], axis=0)
            kc = jnp.concatenate([klo[j, win, :], khi[j, win, :]], axis=0)
            s = lax.dot_general(q2, kc, (((1,), (1,)), ((), ())), preferred_element_type=F32)
            s = s + bias_ref[first, j]
            ps, es = [], []
            for c in range(2):
                sink = jnp.where(top_rows, sinks_ref[4 * j + c], sinks_ref[4 * j + 2 + c])
                sc = s[:, c * 2 * BLOCK:(c + 1) * 2 * BLOCK]
                m = jnp.maximum(jnp.max(sc, axis=1, keepdims=True), sink)
                ps.append(jnp.exp(sc - m).astype(BF16))
                es.append(jnp.exp(sink - m))
            v_lo = jnp.concatenate([vlo[j, win, :], ones_lo], axis=1)
            v_hi = jnp.concatenate([vhi[j, win, :], ones_hi], axis=1)
            r = (jnp.dot(ps[0], v_lo, preferred_element_type=F32)
                 + jnp.dot(ps[1], v_hi, preferred_element_type=F32))
            den = r[:, LANES:] + jnp.where(lo_half, es[0], es[1])
            o2 = r[:, :LANES] * (1.0 / den)
            outs += [o2[:BLOCK], o2[BLOCK:]]
        out = jnp.concatenate(outs, axis=1)
        ms = jnp.mean(out * out, axis=1, keepdims=True)
        o_ref[q_rows, :] = (out * lax.rsqrt(ms + EPS) * g_ref[...]).astype(o_ref.dtype)
        return carry

    lax.fori_loop(0, nbx, block, 0, unroll=2)


def _attention(sinks, q, kv, kvm, bias, g_attn, bsz, nbx):
    seq = nbx * BLOCK
    const2 = lambda b: (0, 0)
    kv_scratch = pltpu.VMEM((N_KV_HEADS, seq + BLOCK, LANES), BF16)
    return pl.pallas_call(
        _attn_kernel,
        grid=(bsz,),
        in_specs=[pl.BlockSpec(memory_space=pltpu.SMEM),
                  pl.BlockSpec((None, seq, ATTN_WIDTH), lambda b: (b, 0, 0)),
                  pl.BlockSpec((None, seq, 2 * KV_WIDTH), lambda b: (b, 0, 0)),
                  pl.BlockSpec((BLOCK, 2 * KV_WIDTH), const2),
                  pl.BlockSpec((2, N_KV_HEADS, 2 * BLOCK, 4 * BLOCK), lambda b: (0, 0, 0, 0)),
                  pl.BlockSpec((1, ATTN_WIDTH), const2)],
        out_specs=pl.BlockSpec((None, seq, ATTN_WIDTH), lambda b: (b, 0, 0)),
        out_shape=jax.ShapeDtypeStruct((bsz, seq, ATTN_WIDTH), BF16),
        scratch_shapes=[kv_scratch, kv_scratch, kv_scratch, kv_scratch],
        compiler_params=_cparams(1),
        name="attention",
    )(sinks, q, kv, kvm, bias, g_attn)


def _gelu_tanh(y):
    c = np.sqrt(2.0 / np.pi).astype(np.float32)
    return 0.5 * y * (1.0 + jnp.tanh(c * (y + 0.044715 * (y * y * y))))


def _lru_kernel(xr_ref, yr_ref, xrm_ref, yrm_ref, cw_ref, cb_ref, wa_ref, wx_ref, ba_ref,
                bx_ref, lam_ref, g_ref, o_ref, xbuf, ycur, hprev):
    n = pl.program_id(1)

    @pl.when(n == 0)
    def _():
        xbuf[0:SUBLANES, :] = jnp.zeros((SUBLANES, LRU_WIDTH), F32)
        xbuf[SUBLANES:, :] = xrm_ref[...].astype(F32)
        ycur[...] = yrm_ref[...].astype(F32)
        hprev[...] = jnp.zeros_like(hprev)

    @pl.when(n > 0)
    def _():
        xbuf[SUBLANES:, :] = xr_ref[...].astype(F32)
        ycur[...] = yr_ref[...].astype(F32)

    xc = jnp.broadcast_to(cb_ref[...], (BLOCK, LRU_WIDTH))
    for k in range(CONV_W):
        off = SUBLANES - (CONV_W - 1) + k
        xc = xc + cw_ref[k:k + 1, :] * xbuf[off:off + BLOCK, :]
    xbuf[0:SUBLANES, :] = xbuf[BLOCK:BLOCK + SUBLANES, :]

    xcb = xc.astype(BF16)
    r = jax.nn.sigmoid(jnp.dot(xcb, wa_ref[...], preferred_element_type=F32) + ba_ref[...])
    i = jax.nn.sigmoid(jnp.dot(xcb, wx_ref[...], preferred_element_type=F32) + bx_ref[...])
    lam = lam_ref[...]
    softplus_neg = jnp.maximum(-lam, 0.0) + jnp.log(1.0 + jnp.exp(-jnp.abs(lam)))
    a = jnp.exp(-LRU_C * r * softplus_neg)
    u = jnp.sqrt(1.0 - a * a) * (i * xc)
    row = lax.broadcasted_iota(jnp.int32, (BLOCK, LRU_WIDTH), 0)
    u = jnp.where((n > 0) | (row >= PAD_FRONT), u, 0.0)

    r8 = row & (SUBLANES - 1)
    for d in (1, 2, 4):
        keep = r8 >= d
        a_s = jnp.where(keep, pltpu.roll(a, d, axis=0), 1.0)
        u_s = jnp.where(keep, pltpu.roll(u, d, axis=0), 0.0)
        u = a * u_s + u
        a = a * a_s
    h_in = hprev[...]
    hs = []
    for g in range(BLOCK // SUBLANES):
        sl = slice(g * SUBLANES, (g + 1) * SUBLANES)
        hg = a[sl, :] * h_in + u[sl, :]
        hs.append(hg)
        h_in = jnp.broadcast_to(hg[SUBLANES - 1:SUBLANES, :], (SUBLANES, LRU_WIDTH))
    hprev[...] = h_in
    h = jnp.concatenate(hs, axis=0)

    out = h * _gelu_tanh(ycur[...])
    ms = jnp.mean(out * out, axis=1, keepdims=True)
    o_ref[...] = (out * lax.rsqrt(ms + EPS) * g_ref[...]).astype(o_ref.dtype)


def _rglru(xr, yr, xrm, yrm, cw, cb, wa, wx, ba, bx, lam, g_lru, bsz, nbx):
    main = lambda b, n: (b, jnp.maximum(n - 1, 0), 0)
    const2 = lambda b, n: (0, 0)
    row_spec = pl.BlockSpec((1, LRU_WIDTH), const2)
    return pl.pallas_call(
        _lru_kernel,
        grid=(bsz, nbx + 1),
        in_specs=[pl.BlockSpec((None, BLOCK, LRU_WIDTH), main),
                  pl.BlockSpec((None, BLOCK, LRU_WIDTH), main),
                  pl.BlockSpec((BLOCK, LRU_WIDTH), const2),
                  pl.BlockSpec((BLOCK, LRU_WIDTH), const2),
                  pl.BlockSpec((CONV_W, LRU_WIDTH), const2),
                  row_spec,
                  pl.BlockSpec((LRU_WIDTH, LRU_WIDTH), const2),
                  pl.BlockSpec((LRU_WIDTH, LRU_WIDTH), const2),
                  row_spec, row_spec, row_spec, row_spec],
        out_specs=pl.BlockSpec((None, BLOCK, LRU_WIDTH), main),
        out_shape=jax.ShapeDtypeStruct((bsz, nbx * BLOCK, LRU_WIDTH), BF16),
        scratch_shapes=[pltpu.VMEM((BLOCK + SUBLANES, LRU_WIDTH), F32),
                        pltpu.VMEM((BLOCK, LRU_WIDTH), F32),
                        pltpu.VMEM((SUBLANES, LRU_WIDTH), F32)],
        compiler_params=_cparams(2),
        name="rglru",
    )(xr, yr, xrm, yrm, cw, cb, wa, wx, ba, bx, lam, g_lru)


def _pack_rows(v):
    bits = lax.bitcast_convert_type(v.astype(BF16).astype(F32), U32)
    return (bits[:, :PACKED] >> 16) | (bits[:, PACKED:] & jnp.uint32(0xFFFF0000))


def _unpack_rows(w):
    lo = lax.bitcast_convert_type(w << 16, F32)
    hi = lax.bitcast_convert_type(w & jnp.uint32(0xFFFF0000), F32)
    return lo, hi


def _layer_norm(z, g, b):
    mu = jnp.mean(z, axis=1, keepdims=True)
    zc = z - mu
    var = jnp.mean(zc * zc, axis=1, keepdims=True)
    return zc * lax.rsqrt(var + EPS) * g + b


def _out_proj_kernel(a_ref, l_ref, x_ref, wa_ref, wl_ref, g_ref, b_ref, wrt_hi_ref, wrt_lo_ref,
                     brt_ref, h_ref, hp_ref, info_ref):
    mix = jnp.dot(a_ref[...], wa_ref[...], preferred_element_type=F32)
    mix = mix + jnp.dot(l_ref[...], wl_ref[...], preferred_element_type=F32)
    h = _layer_norm(ALPHA * x_ref[...] + mix, g_ref[...], b_ref[...])
    h_ref[...] = h
    hp_ref[...] = _pack_rows(h)

    h_hi = h.astype(BF16)
    h_lo = (h - h_hi.astype(F32)).astype(BF16)
    lg = (jnp.dot(h_hi, wrt_hi_ref[...], preferred_element_type=F32)
          + jnp.dot(h_hi, wrt_lo_ref[...], preferred_element_type=F32)
          + jnp.dot(h_lo, wrt_hi_ref[...], preferred_element_type=F32)) + brt_ref[...]
    lane = lax.broadcasted_iota(jnp.int32, lg.shape, 1)
    ninf = -jnp.inf
    gl = jnp.where(lane < N_GROUPS, lg, ninf)
    gmax = jnp.max(gl, axis=1, keepdims=True)
    g_idx = jnp.min(jnp.where(gl == gmax, lane, LANES), axis=1, keepdims=True)
    g_w = 1.0 / jnp.sum(jnp.where(lane < N_GROUPS, jnp.exp(lg - gmax), 0.0),
                        axis=1, keepdims=True)
    e_lo = N_GROUPS + EXPERTS_PER_GROUP * g_idx
    el = jnp.where((lane >= e_lo) & (lane < e_lo + EXPERTS_PER_GROUP), lg, ninf)
    v1 = jnp.max(el, axis=1, keepdims=True)
    i1 = jnp.min(jnp.where(el == v1, lane, LANES), axis=1, keepdims=True)
    el2 = jnp.where(lane == i1, ninf, el)
    v2 = jnp.max(el2, axis=1, keepdims=True)
    i2 = jnp.min(jnp.where(el2 == v2, lane, LANES), axis=1, keepdims=True)
    t = jnp.exp(v2 - v1)
    w1 = 1.0 / (1.0 + t)
    w2 = t * w1
    info = jnp.where(lane == 0, (i1 - N_GROUPS).astype(F32),
                     jnp.where(lane == 1, (i2 - N_GROUPS).astype(F32),
                               jnp.where(lane == 2, g_w * w1,
                                         jnp.where(lane == 3, g_w * w2, 0.0))))
    info_ref[...] = info


def _out_proj(attn_n, lru_n, x2d, wo_a, wo_l, ln_g, ln_b, w_rt_hi, w_rt_lo, b_rt, rows):
    n = x2d.shape[0]
    const = lambda i: (0, 0)
    tile = lambda w: pl.BlockSpec((rows, w), lambda i: (i, 0))
    return pl.pallas_call(
        _out_proj_kernel,
        grid=(n // rows,),
        in_specs=[tile(ATTN_WIDTH), tile(LRU_WIDTH), tile(D_MODEL),
                  pl.BlockSpec((ATTN_WIDTH, D_MODEL), const),
                  pl.BlockSpec((LRU_WIDTH, D_MODEL), const),
                  pl.BlockSpec((1, D_MODEL), const),
                  pl.BlockSpec((1, D_MODEL), const),
                  pl.BlockSpec((D_MODEL, LANES), const),
                  pl.BlockSpec((D_MODEL, LANES), const),
                  pl.BlockSpec((1, LANES), const)],
        out_specs=[tile(D_MODEL), tile(PACKED), tile(LANES)],
        out_shape=[jax.ShapeDtypeStruct((n, D_MODEL), F32),
                   jax.ShapeDtypeStruct((n, PACKED), U32),
                   jax.ShapeDtypeStruct((n, LANES), F32)],
        compiler_params=_cparams(1),
        name="out_proj",
    )(attn_n, lru_n, x2d, wo_a, wo_l, ln_g, ln_b, w_rt_hi, w_rt_lo, b_rt)


def _route_kernel(info_ref, tri_ref, dest_ref, cnt_ref, counts, carry, pstart):
    p = pl.program_id(0)
    t = pl.program_id(1)
    info = info_ref[...]
    lane = lax.broadcasted_iota(I32, info.shape, 1)
    oh1 = (lane == info[:, 0:1].astype(I32)).astype(F32)
    oh2 = (lane == info[:, 1:2].astype(I32)).astype(F32)
    both = oh1 + oh2
    tile_counts = jnp.sum(both, axis=0, keepdims=True)

    @pl.when((p == 0) & (t == 0))
    def _():
        counts[...] = jnp.zeros_like(counts)

    @pl.when(p == 0)
    def _():
        counts[...] += tile_counts

    @pl.when((p == 1) & (t == 0))
    def _():
        c = jnp.broadcast_to(counts[...], (SUBLANES, LANES)).astype(I32)
        padded = ((c + (MOE_BLOCK - 1)) // MOE_BLOCK) * MOE_BLOCK
        lane8 = lax.broadcasted_iota(I32, (SUBLANES, LANES), 1)
        scan = padded
        for d in (1, 2, 4, 8, 16, 32, 64):
            scan = scan + jnp.where(lane8 >= d, pltpu.roll(scan, d, axis=1), 0)
        pstart[...] = (scan - padded).astype(F32)
        carry[...] = jnp.zeros_like(carry)
        cnt_ref[...] = c

    @pl.when(p == 1)
    def _():
        before = jnp.dot(tri_ref[...], both.astype(BF16), preferred_element_type=F32)
        row_of = before + carry[...] + pstart[0:1, :]
        r1 = jnp.sum(oh1 * row_of, axis=1, keepdims=True)
        r2 = jnp.sum(oh2 * row_of, axis=1, keepdims=True)
        dest_ref[...] = jnp.where(lane == 0, r1, jnp.where(lane == 1, r2, 0.0)).astype(I32)
        carry[...] += tile_counts


def _route(info, rows):
    n = info.shape[0]
    tri = jnp.asarray(np.tril(np.ones((rows, rows), np.float32), -1), BF16)
    return pl.pallas_call(
        _route_kernel,
        grid=(2, n // rows),
        in_specs=[pl.BlockSpec((rows, LANES), lambda p, t: (t, 0)),
                  pl.BlockSpec((rows, rows), lambda p, t: (0, 0))],
        out_specs=[pl.BlockSpec((rows, LANES), lambda p, t: (t * p, 0)),
                   pl.BlockSpec((SUBLANES, LANES), lambda p, t: (0, 0))],
        out_shape=[jax.ShapeDtypeStruct((n, LANES), I32),
                   jax.ShapeDtypeStruct((SUBLANES, LANES), I32)],
        scratch_shapes=[pltpu.VMEM((1, LANES), F32), pltpu.VMEM((1, LANES), F32),
                        pltpu.VMEM((SUBLANES, LANES), F32)],
        compiler_params=_cparams(2),
        name="route",
    )(info, tri)


def _sc_mesh():
    return plsc.VectorSubcoreMesh(core_axis_name="core", subcore_axis_name="subcore")


def _sc_worker_id():
    return lax.axis_index("subcore") * SC_CORES + lax.axis_index("core")


def _sc_scatter_rows(rows, d0, d1, cap):
    n, width = rows.shape
    per_worker = n // SC_WORKERS
    n_win = per_worker // SC_WINDOW

    def body(x_hbm, d0_hbm, d1_hbm, o_hbm, i0_v, i1_v, rows_v):
        wid = _sc_worker_id()
        pltpu.sync_copy(d0_hbm.at[wid], i0_v)
        pltpu.sync_copy(d1_hbm.at[wid], i1_v)
        for j in range(n_win):
            pltpu.sync_copy(x_hbm.at[pl.ds(wid * per_worker + j * SC_WINDOW, SC_WINDOW)], rows_v)
            pltpu.sync_copy(rows_v, o_hbm.at[i0_v.at[j]])
            pltpu.sync_copy(rows_v, o_hbm.at[i1_v.at[j]])

    return pl.kernel(
        body,
        out_type=jax.ShapeDtypeStruct((cap, width), rows.dtype),
        mesh=_sc_mesh(),
        scratch_types=[pltpu.VMEM((n_win, SC_WINDOW), I32), pltpu.VMEM((n_win, SC_WINDOW), I32),
                       pltpu.VMEM((SC_WINDOW, width), rows.dtype)],
        name="dispatch",
    )(rows, d0, d1)


def _sc_gather_rows(table, idx):
    width = table.shape[1]
    n_win = idx.shape[1]
    per_worker = n_win * SC_WINDOW

    def body(y_hbm, i_hbm, o_hbm, i_v, rows_v):
        wid = _sc_worker_id()
        pltpu.sync_copy(i_hbm.at[wid], i_v)
        for j in range(n_win):
            pltpu.sync_copy(y_hbm.at[i_v.at[j]], rows_v)
            pltpu.sync_copy(rows_v, o_hbm.at[pl.ds(wid * per_worker + j * SC_WINDOW, SC_WINDOW)])

    return pl.kernel(
        body,
        out_type=jax.ShapeDtypeStruct((SC_WORKERS * per_worker, width), table.dtype),
        mesh=_sc_mesh(),
        scratch_types=[pltpu.VMEM((n_win, SC_WINDOW), I32),
                       pltpu.VMEM((SC_WINDOW, width), table.dtype)],
        name="collect",
    )(table, idx)


def _expert_kernel(bstart_ref, nblk_ref, nused_ref, xs_hbm, wg_ref, wu_ref, wd_ref, yb_hbm,
                   xbuf, ybuf, zbuf, xsem, ysem, zsem, wg_b, wu_b, wd_b):
    e = pl.program_id(0)
    nused = nused_ref[0]
    n_blocks = yb_hbm.shape[0] // MOE_BLOCK

    def rows(b):
        return pl.ds(pl.multiple_of(b * MOE_BLOCK, MOE_BLOCK), MOE_BLOCK)

    def x_copy(b):
        slot = b % X_RING
        return pltpu.make_async_copy(xs_hbm.at[rows(b)], xbuf.at[slot], xsem.at[slot])

    def y_copy(b, slot):
        return pltpu.make_async_copy(ybuf.at[slot], yb_hbm.at[rows(b)], ysem.at[slot])

    @pl.when(e == 0)
    def _():
        for b in range(X_RING - 1):
            @pl.when(b < nused)
            def _():
                x_copy(b).start()

    wg_b[...] = wg_ref[...].astype(BF16)
    wu_b[...] = wu_ref[...].astype(BF16)
    wd_b[...] = wd_ref[...].astype(BF16)

    def block(b, carry):
        slot = b % 2
        x_copy(b).wait()

        @pl.when(b + (X_RING - 1) < nused)
        def _():
            x_copy(b + (X_RING - 1)).start()

        lo, hi = _unpack_rows(xbuf[b % X_RING])
        lo = lo.astype(BF16)
        hi = hi.astype(BF16)
        g = (jnp.dot(lo, wg_b[0:PACKED, :], preferred_element_type=F32)
             + jnp.dot(hi, wg_b[PACKED:, :], preferred_element_type=F32))
        u = (jnp.dot(lo, wu_b[0:PACKED, :], preferred_element_type=F32)
             + jnp.dot(hi, wu_b[PACKED:, :], preferred_element_type=F32))
        mid = (g * jax.nn.sigmoid(g) * u).astype(BF16)
        y = _pack_rows(jnp.dot(mid, wd_b[...], preferred_element_type=F32))

        @pl.when(b >= 2)
        def _():
            y_copy(b - 2, slot).wait()

        ybuf[slot] = y
        y_copy(b, slot).start()
        return carry

    b0 = bstart_ref[e]
    lax.fori_loop(b0, b0 + nblk_ref[e], block, 0)

    @pl.when(e == pl.num_programs(0) - 1)
    def _():
        for back in (2, 1):
            @pl.when(nused >= back)
            def _():
                y_copy(nused - back, (nused - back) % 2).wait()

        zbuf[...] = jnp.zeros_like(zbuf)

        def z_copy(b):
            return pltpu.make_async_copy(zbuf, yb_hbm.at[rows(b)], zsem.at[0])

        def z_start(b, carry):
            z_copy(b).start()
            return carry

        def z_wait(b, carry):
            z_copy(b).wait()
            return carry

        lax.fori_loop(nused, n_blocks, z_start, 0)
        lax.fori_loop(nused, n_blocks, z_wait, 0)


def _experts(bstart, nblk, nused, xs, w_gate, w_up, w_down):
    cap = xs.shape[0]
    w_idx = lambda e, bs, nb, nu: (e, 0, 0)
    grid_spec = pltpu.PrefetchScalarGridSpec(
        num_scalar_prefetch=3,
        grid=(N_EXPERTS,),
        in_specs=[pl.BlockSpec(memory_space=pl.ANY),
                  pl.BlockSpec((None, D_MODEL, D_FF), w_idx),
                  pl.BlockSpec((None, D_MODEL, D_FF), w_idx),
                  pl.BlockSpec((None, D_FF, D_MODEL), w_idx)],
        out_specs=pl.BlockSpec(memory_space=pl.ANY),
        scratch_shapes=[pltpu.VMEM((X_RING, MOE_BLOCK, PACKED), U32),
                        pltpu.VMEM((2, MOE_BLOCK, PACKED), U32),
                        pltpu.VMEM((MOE_BLOCK, PACKED), U32),
                        pltpu.SemaphoreType.DMA((X_RING,)),
                        pltpu.SemaphoreType.DMA((2,)),
                        pltpu.SemaphoreType.DMA((1,)),
                        pltpu.VMEM((D_MODEL, D_FF), BF16),
                        pltpu.VMEM((D_MODEL, D_FF), BF16),
                        pltpu.VMEM((D_FF, D_MODEL), BF16)])
    return pl.pallas_call(
        _expert_kernel,
        grid_spec=grid_spec,
        out_shape=jax.ShapeDtypeStruct((cap, PACKED), U32),
        compiler_params=_cparams(1),
        name="experts",
    )(bstart, nblk, nused, xs, w_gate, w_up, w_down)


def _combine_kernel(y0_ref, y1_ref, h_ref, info_ref, g_ref, b_ref, o_ref):
    info = info_ref[...]
    g0 = info[:, 2:3]
    g1 = info[:, 3:4]
    lo0, hi0 = _unpack_rows(y0_ref[...])
    lo1, hi1 = _unpack_rows(y1_ref[...])
    y = jnp.concatenate([g0 * lo0 + g1 * lo1, g0 * hi0 + g1 * hi1], axis=1)
    o_ref[...] = _layer_norm(ALPHA * h_ref[...] + y, g_ref[...], b_ref[...])


def _combine(ys, h1, info, ln_g, ln_b, rows):
    n = h1.shape[0]
    steps = n // rows
    const = lambda i: (0, 0)
    return pl.pallas_call(
        _combine_kernel,
        grid=(steps,),
        in_specs=[pl.BlockSpec((rows, PACKED), lambda i: (i, 0)),
                  pl.BlockSpec((rows, PACKED), lambda i: (i + steps, 0)),
                  pl.BlockSpec((rows, D_MODEL), lambda i: (i, 0)),
                  pl.BlockSpec((rows, LANES), lambda i: (i, 0)),
                  pl.BlockSpec((1, D_MODEL), const),
                  pl.BlockSpec((1, D_MODEL), const)],
        out_specs=pl.BlockSpec((rows, D_MODEL), lambda i: (i, 0)),
        out_shape=jax.ShapeDtypeStruct((n, D_MODEL), F32),
        compiler_params=_cparams(1),
        name="combine",
    )(ys, ys, h1, info, ln_g, ln_b)


def _alibi_bias():
    qi = np.arange(BLOCK)[:, None]
    kj = np.arange(2 * BLOCK)[None, :]
    dist = qi - kj + BLOCK
    band = (dist >= 0) & (dist < BLOCK)
    slopes = np.exp2(-8.0 * np.arange(1, N_Q_HEADS + 1, dtype=np.float32) / N_Q_HEADS)
    bias = np.where(band[None], -slopes[:, None, None] * dist[None].astype(np.float32), NEG)
    first = np.where((kj >= PAD_FRONT)[None], bias, NEG)
    out = np.empty((2, N_KV_HEADS, 2 * BLOCK, 4 * BLOCK), np.float32)
    for v, per_head in enumerate((first, bias)):
        for j in range(N_KV_HEADS):
            out[v, j] = np.block([[per_head[4 * j], per_head[4 * j + 1]],
                                  [per_head[4 * j + 2], per_head[4 * j + 3]]])
    return jnp.asarray(out, F32)


def _block_diag(w):
    nb, c, _ = w.shape
    eye = jnp.eye(nb, dtype=w.dtype)
    return jnp.einsum('ncd,nm->ncmd', w, eye).reshape(nb * c, nb * c)


def kernel(x, meta_tokens, w_in, conv_w, conv_b, lru_wa, lru_ba, lru_wx, lru_bx, lru_lambda,
           attn_sinks, g_attn, g_lru, w_out, ln1_g, ln1_b, w_group, b_group, w_router,
           b_router, w_gate, w_up, w_down, ln2_g, ln2_b):
    bsz, seq, d = x.shape
    nbx = seq // BLOCK
    n_tok = bsz * seq
    x2d = x.reshape(n_tok, d)
    row = lambda v: v.reshape(1, -1).astype(F32)

    w_in_b = w_in[0].astype(BF16)
    meta_blk = jnp.concatenate([jnp.zeros((PAD_FRONT, d), F32), meta_tokens.astype(F32)], axis=0)
    q, kv, xr, yr = _in_proj(x2d, w_in_b, PROJ_ROWS)
    qm, kvm, xrm, yrm = _in_proj(meta_blk, w_in_b, BLOCK)
    shp = lambda a: a.reshape(bsz, seq, a.shape[-1])

    attn_n = _attention(attn_sinks[0].astype(F32), shp(q), shp(kv), kvm, _alibi_bias(),
                        row(g_attn[0]), bsz, nbx)
    lru_n = _rglru(shp(xr), shp(yr), xrm, yrm, conv_w[0].astype(F32), row(conv_b[0]),
                   _block_diag(lru_wa[0]).astype(BF16), _block_diag(lru_wx[0]).astype(BF16),
                   row(lru_ba[0]), row(lru_bx[0]), row(lru_lambda[0]), row(g_lru[0]), bsz, nbx)

    w_out_b = w_out[0].astype(BF16)
    w_rt = jnp.concatenate(
        [w_group[0], jnp.transpose(w_router[0], (1, 0, 2)).reshape(d, N_EXPERTS),
         jnp.zeros((d, LANES - N_GROUPS - N_EXPERTS), F32)], axis=1).astype(F32)
    w_rt_hi = w_rt.astype(BF16)
    w_rt_lo = (w_rt - w_rt_hi.astype(F32)).astype(BF16)
    b_rt = jnp.concatenate([b_group[0], b_router[0].reshape(-1),
                            jnp.zeros((LANES - N_GROUPS - N_EXPERTS,), F32)]).reshape(1, LANES)
    h1, hp, info = _out_proj(attn_n.reshape(n_tok, ATTN_WIDTH), lru_n.reshape(n_tok, LRU_WIDTH),
                             x2d, w_out_b[:ATTN_WIDTH], w_out_b[ATTN_WIDTH:], row(ln1_g[0]),
                             row(ln1_b[0]), w_rt_hi, w_rt_lo, b_rt, PROJ_ROWS)

    dest, cnt = _route(info, ROUTE_ROWS)
    n_slots = n_tok * TOP_K
    n_blocks = n_slots // MOE_BLOCK + N_EXPERTS
    cap = n_blocks * MOE_BLOCK
    nblk = (cnt[0, :N_EXPERTS] + MOE_BLOCK - 1) // MOE_BLOCK
    bends = jnp.cumsum(nblk)
    bstart = (bends - nblk).astype(I32)
    nused = bends[-1:].astype(I32)
    windows = lambda v: v.reshape(SC_WORKERS, -1, SC_WINDOW)
    d0 = dest[:, 0]
    d1 = dest[:, 1]

    xs = _sc_scatter_rows(hp, windows(d0), windows(d1), cap)
    yb = _experts(bstart, nblk.astype(I32), nused, xs, w_gate[0], w_up[0], w_down[0])
    ys = _sc_gather_rows(yb, windows(jnp.concatenate([d0, d1])))
    out = _combine(ys, h1, info, row(ln2_g[0]), row(ln2_b[0]), COMBINE_ROWS)
    return out.reshape(bsz, seq, d)
```

```python
import jax
import jax.numpy as jnp
import numpy as np
from jax import lax
from jax.experimental import pallas as pl
from jax.experimental.pallas import tpu as pltpu
from jax.experimental.pallas import tpu_sc as plsc

F32 = jnp.float32
BF16 = jnp.bfloat16
U32 = jnp.uint32
I32 = jnp.int32

D_MODEL = 1024
N_META = 16
BLOCK = 128
PAD_FRONT = BLOCK - N_META
HEAD_DIM = 64
ATTN_WIDTH = 512
LRU_WIDTH = 512
N_Q_HEADS = 8
N_KV_HEADS = 2
KV_WIDTH = N_KV_HEADS * HEAD_DIM
LRU_BLOCKS = 8
CONV_W = 4
LRU_C = 8.0
IN_COLS = ATTN_WIDTH + 2 * KV_WIDTH + 2 * LRU_WIDTH
N_GROUPS = 4
EXPERTS_PER_GROUP = 8
N_EXPERTS = N_GROUPS * EXPERTS_PER_GROUP
TOP_K = 2
D_FF = 512
MOE_BLOCK = 256
ALPHA = 2.0 ** 0.25
EPS = 1e-5
NEG = -1e30
LANES = 128
SUBLANES = 8
PACKED = D_MODEL // 2

PROJ_ROWS = 512
ROUTE_ROWS = 512
COMBINE_ROWS = 256
X_RING = 4
VMEM_LIMIT = 48 * 1024 * 1024

SC_CORES = 2
SC_SUBCORES = 16
SC_WORKERS = SC_CORES * SC_SUBCORES
SC_WINDOW = 128


def _cparams(n_axes):
    return pltpu.CompilerParams(
        dimension_semantics=("arbitrary",) * n_axes, vmem_limit_bytes=VMEM_LIMIT)


def _in_proj_kernel(x_ref, w_ref, q_ref, kv_ref, xr_ref, yr_ref):
    proj = jnp.dot(x_ref[...].astype(BF16), w_ref[...], preferred_element_type=F32)
    o = 0
    for ref, width in ((q_ref, ATTN_WIDTH), (kv_ref, 2 * KV_WIDTH),
                       (xr_ref, LRU_WIDTH), (yr_ref, LRU_WIDTH)):
        ref[...] = proj[:, o:o + width].astype(ref.dtype)
        o += width


def _in_proj(x2d, w_bf16, rows):
    n = x2d.shape[0]
    widths = (ATTN_WIDTH, 2 * KV_WIDTH, LRU_WIDTH, LRU_WIDTH)
    return pl.pallas_call(
        _in_proj_kernel,
        grid=(n // rows,),
        in_specs=[pl.BlockSpec((rows, D_MODEL), lambda i: (i, 0)),
                  pl.BlockSpec((D_MODEL, IN_COLS), lambda i: (0, 0))],
        out_specs=[pl.BlockSpec((rows, w), lambda i: (i, 0)) for w in widths],
        out_shape=[jax.ShapeDtypeStruct((n, w), BF16) for w in widths],
        compiler_params=_cparams(1),
        name="in_proj",
    )(x2d, w_bf16)


def _attn_kernel(sinks_ref, q_ref, kv_ref, kvm_ref, bias_ref, g_ref, o_ref,
                 klo, khi, vlo, vhi):
    nbx = q_ref.shape[0] // BLOCK
    lo_lanes = lax.broadcasted_iota(I32, (BLOCK, LANES), 1) < HEAD_DIM

    def layout_block(n, blk):
        rows = pl.ds(pl.multiple_of(n * BLOCK, BLOCK), BLOCK)
        for src, dst_lo, dst_hi in ((blk[:, :KV_WIDTH], klo, khi), (blk[:, KV_WIDTH:], vlo, vhi)):
            w = src.astype(F32)
            r = pltpu.roll(w, HEAD_DIM, axis=1)
            dst_lo[0, rows, :] = jnp.where(lo_lanes, w, 0.0).astype(BF16)
            dst_hi[0, rows, :] = jnp.where(lo_lanes, 0.0, r).astype(BF16)
            dst_lo[1, rows, :] = jnp.where(lo_lanes, r, 0.0).astype(BF16)
            dst_hi[1, rows, :] = jnp.where(lo_lanes, 0.0, w).astype(BF16)

    layout_block(0, kvm_ref[...])

    def layout_body(n, carry):
        layout_block(n + 1, kv_ref[pl.ds(pl.multiple_of(n * BLOCK, BLOCK), BLOCK), :])
        return carry

    lax.fori_loop(0, nbx, layout_body, 0)

    ones_lo = jnp.where(lax.broadcasted_iota(I32, (2 * BLOCK, LANES), 1) < HEAD_DIM,
                        1.0, 0.0).astype(BF16)
    ones_hi = (1.0 - ones_lo.astype(F32)).astype(BF16)
    top_rows = lax.broadcasted_iota(I32, (2 * BLOCK, 1), 0) < BLOCK
    lo_half = lax.broadcasted_iota(I32, (2 * BLOCK, LANES), 1) < HEAD_DIM

    def block(i, carry):
        q_rows = pl.ds(pl.multiple_of(i * BLOCK, BLOCK), BLOCK)
        win = pl.ds(pl.multiple_of(i * BLOCK, BLOCK), 2 * BLOCK)
        q = q_ref[q_rows, :] * (HEAD_DIM ** -0.5)
        first = jnp.minimum(i, 1)
        outs = []
        for j in range(N_KV_HEADS):
            q2 = jnp.concatenate([q[:, (2 * j) * LANES:(2 * j + 1) * LANES],
                                  q[:, (2 * j + 1) * LANES:(2 * j + 2) * LANES]], axis=0)
            kc = jnp.concatenate([klo[j, win, :], khi[j, win, :]], axis=0)
            s = lax.dot_general(q2, kc, (((1,), (1,)), ((), ())), preferred_element_type=F32)
            s = s + bias_ref[first, j]
            ps, es = [], []
            for c in range(2):
                sink = jnp.where(top_rows, sinks_ref[4 * j + c], sinks_ref[4 * j + 2 + c])
                sc = s[:, c * 2 * BLOCK:(c + 1) * 2 * BLOCK]
                m = jnp.maximum(jnp.max(sc, axis=1, keepdims=True), sink)
                ps.append(jnp.exp(sc - m).astype(BF16))
                es.append(jnp.exp(sink - m))
            v_lo = jnp.concatenate([vlo[j, win, :], ones_lo], axis=1)
            v_hi = jnp.concatenate([vhi[j, win, :], ones_hi], axis=1)
            r = (jnp.dot(ps[0], v_lo, preferred_element_type=F32)
                 + jnp.dot(ps[1], v_hi, preferred_element_type=F32))
            den = r[:, LANES:] + jnp.where(lo_half, es[0], es[1])
            o2 = r[:, :LANES] * (1.0 / den)
            outs += [o2[:BLOCK], o2[BLOCK:]]
        out = jnp.concatenate(outs, axis=1)
        ms = jnp.mean(out * out, axis=1, keepdims=True)
        o_ref[q_rows, :] = (out * lax.rsqrt(ms + EPS) * g_ref[...]).astype(o_ref.dtype)
        return carry

    lax.fori_loop(0, nbx, block, 0, unroll=2)


def _attention(sinks, q, kv, kvm, bias, g_attn, bsz, nbx):
    seq = nbx * BLOCK
    const2 = lambda b: (0, 0)
    kv_scratch = pltpu.VMEM((N_KV_HEADS, seq + BLOCK, LANES), BF16)
    return pl.pallas_call(
        _attn_kernel,
        grid=(bsz,),
        in_specs=[pl.BlockSpec(memory_space=pltpu.SMEM),
                  pl.BlockSpec((None, seq, ATTN_WIDTH), lambda b: (b, 0, 0)),
                  pl.BlockSpec((None, seq, 2 * KV_WIDTH), lambda b: (b, 0, 0)),
                  pl.BlockSpec((BLOCK, 2 * KV_WIDTH), const2),
                  pl.BlockSpec((2, N_KV_HEADS, 2 * BLOCK, 4 * BLOCK), lambda b: (0, 0, 0, 0)),
                  pl.BlockSpec((1, ATTN_WIDTH), const2)],
        out_specs=pl.BlockSpec((None, seq, ATTN_WIDTH), lambda b: (b, 0, 0)),
        out_shape=jax.ShapeDtypeStruct((bsz, seq, ATTN_WIDTH), BF16),
        scratch_shapes=[kv_scratch, kv_scratch, kv_scratch, kv_scratch],
        compiler_params=_cparams(1),
        name="attention",
    )(sinks, q, kv, kvm, bias, g_attn)


def _gelu_tanh(y):
    c = np.sqrt(2.0 / np.pi).astype(np.float32)
    return 0.5 * y * (1.0 + jnp.tanh(c * (y + 0.044715 * (y * y * y))))


def _lru_kernel(xr_ref, yr_ref, xrm_ref, yrm_ref, cw_ref, cb_ref, wa_ref, wx_ref, ba_ref,
                bx_ref, lam_ref, g_ref, o_ref, xbuf, ycur, hprev):
    n = pl.program_id(1)

    @pl.when(n == 0)
    def _():
        xbuf[0:SUBLANES, :] = jnp.zeros((SUBLANES, LRU_WIDTH), F32)
        xbuf[SUBLANES:, :] = xrm_ref[...].astype(F32)
        ycur[...] = yrm_ref[...].astype(F32)
        hprev[...] = jnp.zeros_like(hprev)

    @pl.when(n > 0)
    def _():
        xbuf[SUBLANES:, :] = xr_ref[...].astype(F32)
        ycur[...] = yr_ref[...].astype(F32)

    xc = jnp.broadcast_to(cb_ref[...], (BLOCK, LRU_WIDTH))
    for k in range(CONV_W):
        off = SUBLANES - (CONV_W - 1) + k
        xc = xc + cw_ref[k:k + 1, :] * xbuf[off:off + BLOCK, :]
    xbuf[0:SUBLANES, :] = xbuf[BLOCK:BLOCK + SUBLANES, :]

    xcb = xc.astype(BF16)
    r = jax.nn.sigmoid(jnp.dot(xcb, wa_ref[...], preferred_element_type=F32) + ba_ref[...])
    i = jax.nn.sigmoid(jnp.dot(xcb, wx_ref[...], preferred_element_type=F32) + bx_ref[...])
    lam = lam_ref[...]
    softplus_neg = jnp.maximum(-lam, 0.0) + jnp.log(1.0 + jnp.exp(-jnp.abs(lam)))
    a = jnp.exp(-LRU_C * r * softplus_neg)
    u = jnp.sqrt(1.0 - a * a) * (i * xc)
    row = lax.broadcasted_iota(jnp.int32, (BLOCK, LRU_WIDTH), 0)
    u = jnp.where((n > 0) | (row >= PAD_FRONT), u, 0.0)

    r8 = row & (SUBLANES - 1)
    for d in (1, 2, 4):
        keep = r8 >= d
        a_s = jnp.where(keep, pltpu.roll(a, d, axis=0), 1.0)
        u_s = jnp.where(keep, pltpu.roll(u, d, axis=0), 0.0)
        u = a * u_s + u
        a = a * a_s
    h_in = hprev[...]
    hs = []
    for g in range(BLOCK // SUBLANES):
        sl = slice(g * SUBLANES, (g + 1) * SUBLANES)
        hg = a[sl, :] * h_in + u[sl, :]
        hs.append(hg)
        h_in = jnp.broadcast_to(hg[SUBLANES - 1:SUBLANES, :], (SUBLANES, LRU_WIDTH))
    hprev[...] = h_in
    h = jnp.concatenate(hs, axis=0)

    out = h * _gelu_tanh(ycur[...])
    ms = jnp.mean(out * out, axis=1, keepdims=True)
    o_ref[...] = (out * lax.rsqrt(ms + EPS) * g_ref[...]).astype(o_ref.dtype)


def _rglru(xr, yr, xrm, yrm, cw, cb, wa, wx, ba, bx, lam, g_lru, bsz, nbx):
    main = lambda b, n: (b, jnp.maximum(n - 1, 0), 0)
    const2 = lambda b, n: (0, 0)
    row_spec = pl.BlockSpec((1, LRU_WIDTH), const2)
    return pl.pallas_call(
        _lru_kernel,
        grid=(bsz, nbx + 1),
        in_specs=[pl.BlockSpec((None, BLOCK, LRU_WIDTH), main),
                  pl.BlockSpec((None, BLOCK, LRU_WIDTH), main),
                  pl.BlockSpec((BLOCK, LRU_WIDTH), const2),
                  pl.BlockSpec((BLOCK, LRU_WIDTH), const2),
                  pl.BlockSpec((CONV_W, LRU_WIDTH), const2),
                  row_spec,
                  pl.BlockSpec((LRU_WIDTH, LRU_WIDTH), const2),
                  pl.BlockSpec((LRU_WIDTH, LRU_WIDTH), const2),
                  row_spec, row_spec, row_spec, row_spec],
        out_specs=pl.BlockSpec((None, BLOCK, LRU_WIDTH), main),
        out_shape=jax.ShapeDtypeStruct((bsz, nbx * BLOCK, LRU_WIDTH), BF16),
        scratch_shapes=[pltpu.VMEM((BLOCK + SUBLANES, LRU_WIDTH), F32),
                        pltpu.VMEM((BLOCK, LRU_WIDTH), F32),
                        pltpu.VMEM((SUBLANES, LRU_WIDTH), F32)],
        compiler_params=_cparams(2),
        name="rglru",
    )(xr, yr, xrm, yrm, cw, cb, wa, wx, ba, bx, lam, g_lru)


def _pack_rows(v):
    bits = lax.bitcast_convert_type(v.astype(BF16).astype(F32), U32)
    return (bits[:, :PACKED] >> 16) | (bits[:, PACKED:] & jnp.uint32(0xFFFF0000))


def _unpack_rows(w):
    lo = lax.bitcast_convert_type(w << 16, F32)
    hi = lax.bitcast_convert_type(w & jnp.uint32(0xFFFF0000), F32)
    return lo, hi


def _layer_norm(z, g, b):
    mu = jnp.mean(z, axis=1, keepdims=True)
    zc = z - mu
    var = jnp.mean(zc * zc, axis=1, keepdims=True)
    return zc * lax.rsqrt(var + EPS) * g + b


def _out_proj_kernel(a_ref, l_ref, x_ref, wa_ref, wl_ref, g_ref, b_ref, wrt_hi_ref, wrt_lo_ref,
                     brt_ref, h_ref, hp_ref, info_ref):
    mix = jnp.dot(a_ref[...], wa_ref[...], preferred_element_type=F32)
    mix = mix + jnp.dot(l_ref[...], wl_ref[...], preferred_element_type=F32)
    h = _layer_norm(ALPHA * x_ref[...] + mix, g_ref[...], b_ref[...])
    h_ref[...] = h
    hp_ref[...] = _pack_rows(h)

    h_hi = h.astype(BF16)
    h_lo = (h - h_hi.astype(F32)).astype(BF16)
    nt = (((1,), (1,)), ((), ()))
    lg = (lax.dot_general(wrt_hi_ref[...], h_hi, nt, preferred_element_type=F32)
          + lax.dot_general(wrt_lo_ref[...], h_hi, nt, preferred_element_type=F32)
          + lax.dot_general(wrt_hi_ref[...], h_lo, nt, preferred_element_type=F32)) + brt_ref[...]
    tile_shape = (SUBLANES, h.shape[0])
    sub = lax.broadcasted_iota(I32, tile_shape, 0)
    ninf = -jnp.inf
    t0 = lg[0:SUBLANES]
    gl = jnp.where(sub < N_GROUPS, t0, ninf)
    gmax = jnp.max(gl, axis=0, keepdims=True)
    g_idx = jnp.min(jnp.where(gl == gmax, sub, SUBLANES), axis=0, keepdims=True)
    g_w = 1.0 / jnp.sum(jnp.where(sub < N_GROUPS, jnp.exp(t0 - gmax), 0.0),
                        axis=0, keepdims=True)
    el = lg[SUBLANES:2 * SUBLANES]
    for g in range(1, N_GROUPS):
        el = jnp.where(g_idx == g, lg[(g + 1) * SUBLANES:(g + 2) * SUBLANES], el)
    v1 = jnp.max(el, axis=0, keepdims=True)
    i1 = jnp.min(jnp.where(el == v1, sub, SUBLANES), axis=0, keepdims=True)
    el2 = jnp.where(sub == i1, ninf, el)
    v2 = jnp.max(el2, axis=0, keepdims=True)
    i2 = jnp.min(jnp.where(el2 == v2, sub, SUBLANES), axis=0, keepdims=True)
    t = jnp.exp(v2 - v1)
    w1 = 1.0 / (1.0 + t)
    w2 = t * w1
    e_base = g_idx * EXPERTS_PER_GROUP
    info_ref[...] = jnp.where(sub == 0, (e_base + i1).astype(F32),
                              jnp.where(sub == 1, (e_base + i2).astype(F32),
                                        jnp.where(sub == 2, g_w * w1,
                                                  jnp.where(sub == 3, g_w * w2, 0.0))))


ROUTER_ROWS = (N_GROUPS + 1) * SUBLANES


def _out_proj(attn_n, lru_n, x2d, wo_a, wo_l, ln_g, ln_b, w_rt_hi, w_rt_lo, b_rt, rows):
    n = x2d.shape[0]
    const = lambda i: (0, 0)
    tile = lambda w: pl.BlockSpec((rows, w), lambda i: (i, 0))
    return pl.pallas_call(
        _out_proj_kernel,
        grid=(n // rows,),
        in_specs=[tile(ATTN_WIDTH), tile(LRU_WIDTH), tile(D_MODEL),
                  pl.BlockSpec((ATTN_WIDTH, D_MODEL), const),
                  pl.BlockSpec((LRU_WIDTH, D_MODEL), const),
                  pl.BlockSpec((1, D_MODEL), const),
                  pl.BlockSpec((1, D_MODEL), const),
                  pl.BlockSpec((ROUTER_ROWS, D_MODEL), const),
                  pl.BlockSpec((ROUTER_ROWS, D_MODEL), const),
                  pl.BlockSpec((ROUTER_ROWS, 1), const)],
        out_specs=[tile(D_MODEL), tile(PACKED),
                   pl.BlockSpec((SUBLANES, rows), lambda i: (0, i))],
        out_shape=[jax.ShapeDtypeStruct((n, D_MODEL), F32),
                   jax.ShapeDtypeStruct((n, PACKED), U32),
                   jax.ShapeDtypeStruct((SUBLANES, n), F32)],
        compiler_params=_cparams(1),
        name="out_proj",
    )(attn_n, lru_n, x2d, wo_a, wo_l, ln_g, ln_b, w_rt_hi, w_rt_lo, b_rt)


def _route_kernel(info_ref, tri_ref, dest_ref, cnt_ref, counts, carry, pstart):
    p = pl.program_id(0)
    t = pl.program_id(1)
    info = info_ref[...]
    shape = (N_EXPERTS, info.shape[1])
    expert = lax.broadcasted_iota(I32, shape, 0)
    oh1 = (expert == info[0:1, :].astype(I32)).astype(F32)
    oh2 = (expert == info[1:2, :].astype(I32)).astype(F32)
    both = oh1 + oh2
    tile_counts = jnp.sum(both, axis=1, keepdims=True)

    @pl.when((p == 0) & (t == 0))
    def _():
        counts[...] = jnp.zeros_like(counts)

    @pl.when(p == 0)
    def _():
        counts[...] += tile_counts

    @pl.when((p == 1) & (t == 0))
    def _():
        c = jnp.broadcast_to(counts[...], (N_EXPERTS, LANES)).astype(I32)
        padded = ((c + (MOE_BLOCK - 1)) // MOE_BLOCK) * MOE_BLOCK
        e = lax.broadcasted_iota(I32, (N_EXPERTS, LANES), 0)
        scan = padded
        for d in (1, 2, 4, 8, 16):
            scan = scan + jnp.where(e >= d, pltpu.roll(scan, d, axis=0), 0)
        pstart[...] = (scan - padded)[:, 0:1].astype(F32)
        carry[...] = jnp.zeros_like(carry)
        cnt_ref[...] = c

    @pl.when(p == 1)
    def _():
        before = jnp.dot(both.astype(BF16), tri_ref[...], preferred_element_type=F32)
        row_of = before + (carry[...] + pstart[...])
        r1 = jnp.sum(oh1 * row_of, axis=0, keepdims=True)
        r2 = jnp.sum(oh2 * row_of, axis=0, keepdims=True)
        sub = lax.broadcasted_iota(I32, dest_ref.shape, 0)
        dest_ref[...] = jnp.where(sub == 0, r1, jnp.where(sub == 1, r2, 0.0)).astype(I32)
        carry[...] += tile_counts


def _route(info_t, cols):
    n = info_t.shape[1]
    tri = jnp.asarray(np.triu(np.ones((cols, cols), np.float32), 1), BF16)
    return pl.pallas_call(
        _route_kernel,
        grid=(2, n // cols),
        in_specs=[pl.BlockSpec((SUBLANES, cols), lambda p, t: (0, t)),
                  pl.BlockSpec((cols, cols), lambda p, t: (0, 0))],
        out_specs=[pl.BlockSpec((SUBLANES, cols), lambda p, t: (0, t * p)),
                   pl.BlockSpec((N_EXPERTS, LANES), lambda p, t: (0, 0))],
        out_shape=[jax.ShapeDtypeStruct((SUBLANES, n), I32),
                   jax.ShapeDtypeStruct((N_EXPERTS, LANES), I32)],
        scratch_shapes=[pltpu.VMEM((N_EXPERTS, 1), F32), pltpu.VMEM((N_EXPERTS, 1), F32),
                        pltpu.VMEM((N_EXPERTS, 1), F32)],
        compiler_params=_cparams(2),
        name="route",
    )(info_t, tri)


def _sc_mesh():
    return plsc.VectorSubcoreMesh(core_axis_name="core", subcore_axis_name="subcore")


def _sc_worker_id():
    return lax.axis_index("subcore") * SC_CORES + lax.axis_index("core")


def _sc_scatter_rows(rows, d0, d1, cap):
    n, width = rows.shape
    per_worker = n // SC_WORKERS
    n_win = per_worker // SC_WINDOW

    def body(x_hbm, d0_hbm, d1_hbm, o_hbm, i0_v, i1_v, rows_v):
        wid = _sc_worker_id()
        pltpu.sync_copy(d0_hbm.at[wid], i0_v)
        pltpu.sync_copy(d1_hbm.at[wid], i1_v)
        for j in range(n_win):
            pltpu.sync_copy(x_hbm.at[pl.ds(wid * per_worker + j * SC_WINDOW, SC_WINDOW)], rows_v)
            pltpu.sync_copy(rows_v, o_hbm.at[i0_v.at[j]])
            pltpu.sync_copy(rows_v, o_hbm.at[i1_v.at[j]])

    return pl.kernel(
        body,
        out_type=jax.ShapeDtypeStruct((cap, width), rows.dtype),
        mesh=_sc_mesh(),
        scratch_types=[pltpu.VMEM((n_win, SC_WINDOW), I32), pltpu.VMEM((n_win, SC_WINDOW), I32),
                       pltpu.VMEM((SC_WINDOW, width), rows.dtype)],
        name="dispatch",
    )(rows, d0, d1)


def _sc_gather_rows(table, idx):
    width = table.shape[1]
    n_win = idx.shape[1]
    per_worker = n_win * SC_WINDOW

    def body(y_hbm, i_hbm, o_hbm, i_v, rows_v):
        wid = _sc_worker_id()
        pltpu.sync_copy(i_hbm.at[wid], i_v)
        for j in range(n_win):
            pltpu.sync_copy(y_hbm.at[i_v.at[j]], rows_v)
            pltpu.sync_copy(rows_v, o_hbm.at[pl.ds(wid * per_worker + j * SC_WINDOW, SC_WINDOW)])

    return pl.kernel(
        body,
        out_type=jax.ShapeDtypeStruct((SC_WORKERS * per_worker, width), table.dtype),
        mesh=_sc_mesh(),
        scratch_types=[pltpu.VMEM((n_win, SC_WINDOW), I32),
                       pltpu.VMEM((SC_WINDOW, width), table.dtype)],
        name="collect",
    )(table, idx)


def _expert_kernel(bstart_ref, nblk_ref, nused_ref, xs_hbm, wg_ref, wu_ref, wd_ref, yb_hbm,
                   xbuf, ybuf, zbuf, xsem, ysem, zsem, wg_b, wu_b, wd_b):
    e = pl.program_id(0)
    nused = nused_ref[0]
    n_blocks = yb_hbm.shape[0] // MOE_BLOCK

    def rows(b):
        return pl.ds(pl.multiple_of(b * MOE_BLOCK, MOE_BLOCK), MOE_BLOCK)

    def x_copy(b):
        slot = b % X_RING
        return pltpu.make_async_copy(xs_hbm.at[rows(b)], xbuf.at[slot], xsem.at[slot])

    def y_copy(b, slot):
        return pltpu.make_async_copy(ybuf.at[slot], yb_hbm.at[rows(b)], ysem.at[slot])

    @pl.when(e == 0)
    def _():
        for b in range(X_RING - 1):
            @pl.when(b < nused)
            def _():
                x_copy(b).start()

    wg_b[...] = wg_ref[...].astype(BF16)
    wu_b[...] = wu_ref[...].astype(BF16)
    wd_b[...] = wd_ref[...].astype(BF16)

    def block(b, carry):
        slot = b % 2
        x_copy(b).wait()

        @pl.when(b + (X_RING - 1) < nused)
        def _():
            x_copy(b + (X_RING - 1)).start()

        lo, hi = _unpack_rows(xbuf[b % X_RING])
        lo = lo.astype(BF16)
        hi = hi.astype(BF16)
        g = (jnp.dot(lo, wg_b[0:PACKED, :], preferred_element_type=F32)
             + jnp.dot(hi, wg_b[PACKED:, :], preferred_element_type=F32))
        u = (jnp.dot(lo, wu_b[0:PACKED, :], preferred_element_type=F32)
             + jnp.dot(hi, wu_b[PACKED:, :], preferred_element_type=F32))
        mid = (g * jax.nn.sigmoid(g) * u).astype(BF16)
        y = _pack_rows(jnp.dot(mid, wd_b[...], preferred_element_type=F32))

        @pl.when(b >= 2)
        def _():
            y_copy(b - 2, slot).wait()

        ybuf[slot] = y
        y_copy(b, slot).start()
        return carry

    b0 = bstart_ref[e]
    lax.fori_loop(b0, b0 + nblk_ref[e], block, 0)

    @pl.when(e == pl.num_programs(0) - 1)
    def _():
        for back in (2, 1):
            @pl.when(nused >= back)
            def _():
                y_copy(nused - back, (nused - back) % 2).wait()

        zbuf[...] = jnp.zeros_like(zbuf)

        def z_copy(b):
            return pltpu.make_async_copy(zbuf, yb_hbm.at[rows(b)], zsem.at[0])

        def z_start(b, carry):
            z_copy(b).start()
            return carry

        def z_wait(b, carry):
            z_copy(b).wait()
            return carry

        lax.fori_loop(nused, n_blocks, z_start, 0)
        lax.fori_loop(nused, n_blocks, z_wait, 0)


def _experts(bstart, nblk, nused, xs, w_gate, w_up, w_down):
    cap = xs.shape[0]
    w_idx = lambda e, bs, nb, nu: (e, 0, 0)
    grid_spec = pltpu.PrefetchScalarGridSpec(
        num_scalar_prefetch=3,
        grid=(N_EXPERTS,),
        in_specs=[pl.BlockSpec(memory_space=pl.ANY),
                  pl.BlockSpec((None, D_MODEL, D_FF), w_idx),
                  pl.BlockSpec((None, D_MODEL, D_FF), w_idx),
                  pl.BlockSpec((None, D_FF, D_MODEL), w_idx)],
        out_specs=pl.BlockSpec(memory_space=pl.ANY),
        scratch_shapes=[pltpu.VMEM((X_RING, MOE_BLOCK, PACKED), U32),
                        pltpu.VMEM((2, MOE_BLOCK, PACKED), U32),
                        pltpu.VMEM((MOE_BLOCK, PACKED), U32),
                        pltpu.SemaphoreType.DMA((X_RING,)),
                        pltpu.SemaphoreType.DMA((2,)),
                        pltpu.SemaphoreType.DMA((1,)),
                        pltpu.VMEM((D_MODEL, D_FF), BF16),
                        pltpu.VMEM((D_MODEL, D_FF), BF16),
                        pltpu.VMEM((D_FF, D_MODEL), BF16)])
    return pl.pallas_call(
        _expert_kernel,
        grid_spec=grid_spec,
        out_shape=jax.ShapeDtypeStruct((cap, PACKED), U32),
        compiler_params=_cparams(1),
        name="experts",
    )(bstart, nblk, nused, xs, w_gate, w_up, w_down)


def _combine_kernel(y0_ref, y1_ref, h_ref, info_ref, g_ref, b_ref, o_ref):
    info = info_ref[...].T
    g0 = info[:, 2:3]
    g1 = info[:, 3:4]
    lo0, hi0 = _unpack_rows(y0_ref[...])
    lo1, hi1 = _unpack_rows(y1_ref[...])
    y = jnp.concatenate([g0 * lo0 + g1 * lo1, g0 * hi0 + g1 * hi1], axis=1)
    o_ref[...] = _layer_norm(ALPHA * h_ref[...] + y, g_ref[...], b_ref[...])


def _combine(ys, h1, info, ln_g, ln_b, rows):
    n = h1.shape[0]
    steps = n // rows
    const = lambda i: (0, 0)
    return pl.pallas_call(
        _combine_kernel,
        grid=(steps,),
        in_specs=[pl.BlockSpec((rows, PACKED), lambda i: (i, 0)),
                  pl.BlockSpec((rows, PACKED), lambda i: (i + steps, 0)),
                  pl.BlockSpec((rows, D_MODEL), lambda i: (i, 0)),
                  pl.BlockSpec((SUBLANES, rows), lambda i: (0, i)),
                  pl.BlockSpec((1, D_MODEL), const),
                  pl.BlockSpec((1, D_MODEL), const)],
        out_specs=pl.BlockSpec((rows, D_MODEL), lambda i: (i, 0)),
        out_shape=jax.ShapeDtypeStruct((n, D_MODEL), F32),
        compiler_params=_cparams(1),
        name="combine",
    )(ys, ys, h1, info, ln_g, ln_b)


def _alibi_bias():
    qi = np.arange(BLOCK)[:, None]
    kj = np.arange(2 * BLOCK)[None, :]
    dist = qi - kj + BLOCK
    band = (dist >= 0) & (dist < BLOCK)
    slopes = np.exp2(-8.0 * np.arange(1, N_Q_HEADS + 1, dtype=np.float32) / N_Q_HEADS)
    bias = np.where(band[None], -slopes[:, None, None] * dist[None].astype(np.float32), NEG)
    first = np.where((kj >= PAD_FRONT)[None], bias, NEG)
    out = np.empty((2, N_KV_HEADS, 2 * BLOCK, 4 * BLOCK), np.float32)
    for v, per_head in enumerate((first, bias)):
        for j in range(N_KV_HEADS):
            out[v, j] = np.block([[per_head[4 * j], per_head[4 * j + 1]],
                                  [per_head[4 * j + 2], per_head[4 * j + 3]]])
    return jnp.asarray(out, F32)


def _block_diag(w):
    nb, c, _ = w.shape
    eye = jnp.eye(nb, dtype=w.dtype)
    return jnp.einsum('ncd,nm->ncmd', w, eye).reshape(nb * c, nb * c)


def kernel(x, meta_tokens, w_in, conv_w, conv_b, lru_wa, lru_ba, lru_wx, lru_bx, lru_lambda,
           attn_sinks, g_attn, g_lru, w_out, ln1_g, ln1_b, w_group, b_group, w_router,
           b_router, w_gate, w_up, w_down, ln2_g, ln2_b):
    bsz, seq, d = x.shape
    nbx = seq // BLOCK
    n_tok = bsz * seq
    x2d = x.reshape(n_tok, d)
    row = lambda v: v.reshape(1, -1).astype(F32)

    w_in_b = w_in[0].astype(BF16)
    meta_blk = jnp.concatenate([jnp.zeros((PAD_FRONT, d), F32), meta_tokens.astype(F32)], axis=0)
    q, kv, xr, yr = _in_proj(x2d, w_in_b, PROJ_ROWS)
    qm, kvm, xrm, yrm = _in_proj(meta_blk, w_in_b, BLOCK)
    shp = lambda a: a.reshape(bsz, seq, a.shape[-1])

    attn_n = _attention(attn_sinks[0].astype(F32), shp(q), shp(kv), kvm, _alibi_bias(),
                        row(g_attn[0]), bsz, nbx)
    lru_n = _rglru(shp(xr), shp(yr), xrm, yrm, conv_w[0].astype(F32), row(conv_b[0]),
                   _block_diag(lru_wa[0]).astype(BF16), _block_diag(lru_wx[0]).astype(BF16),
                   row(lru_ba[0]), row(lru_bx[0]), row(lru_lambda[0]), row(g_lru[0]), bsz, nbx)

    w_out_b = w_out[0].astype(BF16)
    gpad = SUBLANES - N_GROUPS
    w_rt = jnp.concatenate(
        [w_group[0].T, jnp.zeros((gpad, d), F32),
         jnp.transpose(w_router[0], (0, 2, 1)).reshape(N_EXPERTS, d)], axis=0).astype(F32)
    w_rt_hi = w_rt.astype(BF16)
    w_rt_lo = (w_rt - w_rt_hi.astype(F32)).astype(BF16)
    b_rt = jnp.concatenate([b_group[0], jnp.zeros((gpad,), F32),
                            b_router[0].reshape(-1)]).astype(F32).reshape(ROUTER_ROWS, 1)
    h1, hp, info = _out_proj(attn_n.reshape(n_tok, ATTN_WIDTH), lru_n.reshape(n_tok, LRU_WIDTH),
                             x2d, w_out_b[:ATTN_WIDTH], w_out_b[ATTN_WIDTH:], row(ln1_g[0]),
                             row(ln1_b[0]), w_rt_hi, w_rt_lo, b_rt, PROJ_ROWS)

    dest, cnt = _route(info, ROUTE_ROWS)
    n_slots = n_tok * TOP_K
    n_blocks = n_slots // MOE_BLOCK + N_EXPERTS
    cap = n_blocks * MOE_BLOCK
    nblk = (cnt[:, 0] + MOE_BLOCK - 1) // MOE_BLOCK
    bends = jnp.cumsum(nblk)
    bstart = (bends - nblk).astype(I32)
    nused = bends[-1:].astype(I32)
    windows = lambda v: v.reshape(SC_WORKERS, -1, SC_WINDOW)
    d0 = dest[0]
    d1 = dest[1]

    xs = _sc_scatter_rows(hp, windows(d0), windows(d1), cap)
    yb = _experts(bstart, nblk.astype(I32), nused, xs, w_gate[0], w_up[0], w_down[0])
    ys = _sc_gather_rows(yb, windows(jnp.concatenate([d0, d1])))
    out = _combine(ys, h1, info, row(ln2_g[0]), row(ln2_b[0]), COMBINE_ROWS)
    return out.reshape(bsz, seq, d)
```

```python
import jax
import jax.numpy as jnp
import numpy as np
from jax import lax
from jax.experimental import pallas as pl
from jax.experimental.pallas import tpu as pltpu
from jax.experimental.pallas import tpu_sc as plsc

F32 = jnp.float32
BF16 = jnp.bfloat16
U32 = jnp.uint32
I32 = jnp.int32

D_MODEL = 1024
N_META = 16
BLOCK = 128
PAD_FRONT = BLOCK - N_META
HEAD_DIM = 64
ATTN_WIDTH = 512
LRU_WIDTH = 512
N_Q_HEADS = 8
N_KV_HEADS = 2
KV_WIDTH = N_KV_HEADS * HEAD_DIM
LRU_BLOCKS = 8
CONV_W = 4
LRU_C = 8.0
IN_COLS = ATTN_WIDTH + 2 * KV_WIDTH + 2 * LRU_WIDTH
N_GROUPS = 4
EXPERTS_PER_GROUP = 8
N_EXPERTS = N_GROUPS * EXPERTS_PER_GROUP
TOP_K = 2
D_FF = 512
MOE_BLOCK = 256
ALPHA = 2.0 ** 0.25
EPS = 1e-5
NEG = -1e30
LANES = 128
SUBLANES = 8
PACKED = D_MODEL // 2

PROJ_ROWS = 512
ROUTE_ROWS = 512
COMBINE_ROWS = 256
COMBINE_PARTS = 2
X_RING = 4
VMEM_LIMIT = 48 * 1024 * 1024

SC_CORES = 2
SC_SUBCORES = 16
SC_WORKERS = SC_CORES * SC_SUBCORES
SC_WINDOW = 64


def _cparams(n_axes):
    return pltpu.CompilerParams(
        dimension_semantics=("arbitrary",) * n_axes, vmem_limit_bytes=VMEM_LIMIT)


def _in_proj_kernel(x_ref, w_ref, q_ref, kv_ref, xr_ref, yr_ref):
    proj = jnp.dot(x_ref[...].astype(BF16), w_ref[...], preferred_element_type=F32)
    o = 0
    for ref, width in ((q_ref, ATTN_WIDTH), (kv_ref, 2 * KV_WIDTH),
                       (xr_ref, LRU_WIDTH), (yr_ref, LRU_WIDTH)):
        ref[...] = proj[:, o:o + width].astype(ref.dtype)
        o += width


def _in_proj(x2d, w_bf16, rows):
    n = x2d.shape[0]
    widths = (ATTN_WIDTH, 2 * KV_WIDTH, LRU_WIDTH, LRU_WIDTH)
    return pl.pallas_call(
        _in_proj_kernel,
        grid=(n // rows,),
        in_specs=[pl.BlockSpec((rows, D_MODEL), lambda i: (i, 0)),
                  pl.BlockSpec((D_MODEL, IN_COLS), lambda i: (0, 0))],
        out_specs=[pl.BlockSpec((rows, w), lambda i: (i, 0)) for w in widths],
        out_shape=[jax.ShapeDtypeStruct((n, w), BF16) for w in widths],
        compiler_params=_cparams(1),
        name="in_proj",
    )(x2d, w_bf16)


def _attn_kernel(sinks_ref, q_ref, kv_ref, kvm_ref, bias_ref, g_ref, o_ref,
                 klo, khi, vlo, vhi):
    nbx = q_ref.shape[0] // BLOCK
    lo_lanes = lax.broadcasted_iota(I32, (BLOCK, LANES), 1) < HEAD_DIM

    def layout_block(n, blk):
        rows = pl.ds(pl.multiple_of(n * BLOCK, BLOCK), BLOCK)
        for src, dst_lo, dst_hi in ((blk[:, :KV_WIDTH], klo, khi), (blk[:, KV_WIDTH:], vlo, vhi)):
            w = src.astype(F32)
            r = pltpu.roll(w, HEAD_DIM, axis=1)
            dst_lo[0, rows, :] = jnp.where(lo_lanes, w, 0.0).astype(BF16)
            dst_hi[0, rows, :] = jnp.where(lo_lanes, 0.0, r).astype(BF16)
            dst_lo[1, rows, :] = jnp.where(lo_lanes, r, 0.0).astype(BF16)
            dst_hi[1, rows, :] = jnp.where(lo_lanes, 0.0, w).astype(BF16)

    layout_block(0, kvm_ref[...])

    def layout_body(n, carry):
        layout_block(n + 1, kv_ref[pl.ds(pl.multiple_of(n * BLOCK, BLOCK), BLOCK), :])
        return carry

    lax.fori_loop(0, nbx, layout_body, 0)

    ones_lo = jnp.where(lax.broadcasted_iota(I32, (2 * BLOCK, LANES), 1) < HEAD_DIM,
                        1.0, 0.0).astype(BF16)
    ones_hi = (1.0 - ones_lo.astype(F32)).astype(BF16)
    top_rows = lax.broadcasted_iota(I32, (2 * BLOCK, 1), 0) < BLOCK
    lo_half = lax.broadcasted_iota(I32, (2 * BLOCK, LANES), 1) < HEAD_DIM

    def block(i, carry):
        q_rows = pl.ds(pl.multiple_of(i * BLOCK, BLOCK), BLOCK)
        win = pl.ds(pl.multiple_of(i * BLOCK, BLOCK), 2 * BLOCK)
        q = q_ref[q_rows, :] * (HEAD_DIM ** -0.5)
        first = jnp.minimum(i, 1)
        outs = []
        for j in range(N_KV_HEADS):
            q2 = jnp.concatenate([q[:, (2 * j) * LANES:(2 * j + 1) * LANES],
                                  q[:, (2 * j + 1) * LANES:(2 * j + 2) * LANES]], axis=0)
            kc = jnp.concatenate([klo[j, win, :], khi[j, win, :]], axis=0)
            s = lax.dot_general(q2, kc, (((1,), (1,)), ((), ())), preferred_element_type=F32)
            s = s + bias_ref[first, j]
            ps, es = [], []
            for c in range(2):
                sink = jnp.where(top_rows, sinks_ref[4 * j + c], sinks_ref[4 * j + 2 + c])
                sc = s[:, c * 2 * BLOCK:(c + 1) * 2 * BLOCK]
                m = jnp.maximum(jnp.max(sc, axis=1, keepdims=True), sink)
                ps.append(jnp.exp(sc - m).astype(BF16))
                es.append(jnp.exp(sink - m))
            v_lo = jnp.concatenate([vlo[j, win, :], ones_lo], axis=1)
            v_hi = jnp.concatenate([vhi[j, win, :], ones_hi], axis=1)
            r = (jnp.dot(ps[0], v_lo, preferred_element_type=F32)
                 + jnp.dot(ps[1], v_hi, preferred_element_type=F32))
            den = r[:, LANES:] + jnp.where(lo_half, es[0], es[1])
            o2 = r[:, :LANES] * (1.0 / den)
            outs += [o2[:BLOCK], o2[BLOCK:]]
        out = jnp.concatenate(outs, axis=1)
        ms = jnp.mean(out * out, axis=1, keepdims=True)
        o_ref[q_rows, :] = (out * lax.rsqrt(ms + EPS) * g_ref[...]).astype(o_ref.dtype)
        return carry

    lax.fori_loop(0, nbx, block, 0, unroll=2)


def _attention(sinks, q, kv, kvm, bias, g_attn, bsz, nbx):
    seq = nbx * BLOCK
    const2 = lambda b: (0, 0)
    kv_scratch = pltpu.VMEM((N_KV_HEADS, seq + BLOCK, LANES), BF16)
    return pl.pallas_call(
        _attn_kernel,
        grid=(bsz,),
        in_specs=[pl.BlockSpec(memory_space=pltpu.SMEM),
                  pl.BlockSpec((None, seq, ATTN_WIDTH), lambda b: (b, 0, 0)),
                  pl.BlockSpec((None, seq, 2 * KV_WIDTH), lambda b: (b, 0, 0)),
                  pl.BlockSpec((BLOCK, 2 * KV_WIDTH), const2),
                  pl.BlockSpec((2, N_KV_HEADS, 2 * BLOCK, 4 * BLOCK), lambda b: (0, 0, 0, 0)),
                  pl.BlockSpec((1, ATTN_WIDTH), const2)],
        out_specs=pl.BlockSpec((None, seq, ATTN_WIDTH), lambda b: (b, 0, 0)),
        out_shape=jax.ShapeDtypeStruct((bsz, seq, ATTN_WIDTH), BF16),
        scratch_shapes=[kv_scratch, kv_scratch, kv_scratch, kv_scratch],
        compiler_params=_cparams(1),
        name="attention",
    )(sinks, q, kv, kvm, bias, g_attn)


def _gelu_tanh(y):
    c = np.sqrt(2.0 / np.pi).astype(np.float32)
    return 0.5 * y * (1.0 + jnp.tanh(c * (y + 0.044715 * (y * y * y))))


def _lru_kernel(xr_ref, yr_ref, xrm_ref, yrm_ref, cw_ref, cb_ref, wa_ref, wx_ref, ba_ref,
                bx_ref, lam_ref, g_ref, eg_ref, eu_ref, ed_ref,
                o_ref, eg_out, eu_out, ed_out, xbuf, ycur, hprev):
    n = pl.program_id(1)

    step = pl.program_id(0) * pl.num_programs(1) + n
    for k, (src, dst) in enumerate(((eg_ref, eg_out), (eu_ref, eu_out), (ed_ref, ed_out))):
        @pl.when((step >= k * N_EXPERTS) & (step < (k + 1) * N_EXPERTS))
        def _():
            dst[...] = src[...].astype(BF16)

    @pl.when(n == 0)
    def _():
        xbuf[0:SUBLANES, :] = jnp.zeros((SUBLANES, LRU_WIDTH), F32)
        xbuf[SUBLANES:, :] = xrm_ref[...].astype(F32)
        ycur[...] = yrm_ref[...].astype(F32)
        hprev[...] = jnp.zeros_like(hprev)

    @pl.when(n > 0)
    def _():
        xbuf[SUBLANES:, :] = xr_ref[...].astype(F32)
        ycur[...] = yr_ref[...].astype(F32)

    xc = jnp.broadcast_to(cb_ref[...], (BLOCK, LRU_WIDTH))
    for k in range(CONV_W):
        off = SUBLANES - (CONV_W - 1) + k
        xc = xc + cw_ref[k:k + 1, :] * xbuf[off:off + BLOCK, :]
    xbuf[0:SUBLANES, :] = xbuf[BLOCK:BLOCK + SUBLANES, :]

    xcb = xc.astype(BF16)
    r = jax.nn.sigmoid(jnp.dot(xcb, wa_ref[...], preferred_element_type=F32) + ba_ref[...])
    i = jax.nn.sigmoid(jnp.dot(xcb, wx_ref[...], preferred_element_type=F32) + bx_ref[...])
    lam = lam_ref[...]
    softplus_neg = jnp.maximum(-lam, 0.0) + jnp.log(1.0 + jnp.exp(-jnp.abs(lam)))
    a = jnp.exp(-LRU_C * r * softplus_neg)
    u = jnp.sqrt(1.0 - a * a) * (i * xc)
    row = lax.broadcasted_iota(jnp.int32, (BLOCK, LRU_WIDTH), 0)
    u = jnp.where((n > 0) | (row >= PAD_FRONT), u, 0.0)

    groups = (BLOCK // SUBLANES, SUBLANES, LRU_WIDTH)
    a = a.reshape(groups)
    u = u.reshape(groups)
    r8 = lax.broadcasted_iota(jnp.int32, groups, 1)
    for d in (1, 2, 4):
        keep = r8 >= d
        a_s = jnp.where(keep, pltpu.roll(a, d, axis=1), 1.0)
        u_s = jnp.where(keep, pltpu.roll(u, d, axis=1), 0.0)
        u = a * u_s + u
        a = a * a_s
    a = a.reshape(BLOCK, LRU_WIDTH)
    u = u.reshape(BLOCK, LRU_WIDTH)
    h_in = hprev[...]
    hs = []
    for g in range(BLOCK // SUBLANES):
        sl = slice(g * SUBLANES, (g + 1) * SUBLANES)
        hg = a[sl, :] * h_in + u[sl, :]
        hs.append(hg)
        h_in = jnp.broadcast_to(hg[SUBLANES - 1:SUBLANES, :], (SUBLANES, LRU_WIDTH))
    hprev[...] = h_in
    h = jnp.concatenate(hs, axis=0)

    out = h * _gelu_tanh(ycur[...])
    ms = jnp.mean(out * out, axis=1, keepdims=True)
    o_ref[...] = (out * lax.rsqrt(ms + EPS) * g_ref[...]).astype(o_ref.dtype)


def _rglru(xr, yr, xrm, yrm, cw, cb, wa, wx, ba, bx, lam, g_lru, w_gate, w_up, w_down,
           bsz, nbx):
    main = lambda b, n: (b, jnp.maximum(n - 1, 0), 0)
    const2 = lambda b, n: (0, 0)
    row_spec = pl.BlockSpec((1, LRU_WIDTH), const2)
    assert bsz * (nbx + 1) >= 3 * N_EXPERTS

    def expert_spec(k, shape):
        idx = lambda b, n: (jnp.clip(b * (nbx + 1) + n - k * N_EXPERTS, 0, N_EXPERTS - 1), 0, 0)
        return pl.BlockSpec((None,) + shape, idx)

    e_shapes = ((D_MODEL, D_FF), (D_MODEL, D_FF), (D_FF, D_MODEL))
    e_specs = [expert_spec(k, shp) for k, shp in enumerate(e_shapes)]
    return pl.pallas_call(
        _lru_kernel,
        grid=(bsz, nbx + 1),
        in_specs=[pl.BlockSpec((None, BLOCK, LRU_WIDTH), main),
                  pl.BlockSpec((None, BLOCK, LRU_WIDTH), main),
                  pl.BlockSpec((BLOCK, LRU_WIDTH), const2),
                  pl.BlockSpec((BLOCK, LRU_WIDTH), const2),
                  pl.BlockSpec((CONV_W, LRU_WIDTH), const2),
                  row_spec,
                  pl.BlockSpec((LRU_WIDTH, LRU_WIDTH), const2),
                  pl.BlockSpec((LRU_WIDTH, LRU_WIDTH), const2),
                  row_spec, row_spec, row_spec, row_spec] + e_specs,
        out_specs=[pl.BlockSpec((None, BLOCK, LRU_WIDTH), main)] + e_specs,
        out_shape=[jax.ShapeDtypeStruct((bsz, nbx * BLOCK, LRU_WIDTH), BF16)]
        + [jax.ShapeDtypeStruct((N_EXPERTS,) + shp, BF16) for shp in e_shapes],
        scratch_shapes=[pltpu.VMEM((BLOCK + SUBLANES, LRU_WIDTH), F32),
                        pltpu.VMEM((BLOCK, LRU_WIDTH), F32),
                        pltpu.VMEM((SUBLANES, LRU_WIDTH), F32)],
        compiler_params=_cparams(2),
        name="rglru",
    )(xr, yr, xrm, yrm, cw, cb, wa, wx, ba, bx, lam, g_lru, w_gate, w_up, w_down)


def _pack_rows(v):
    bits = lax.bitcast_convert_type(v.astype(BF16).astype(F32), U32)
    return (bits[:, :PACKED] >> 16) | (bits[:, PACKED:] & jnp.uint32(0xFFFF0000))


def _unpack_rows(w):
    lo = lax.bitcast_convert_type(w << 16, F32)
    hi = lax.bitcast_convert_type(w & jnp.uint32(0xFFFF0000), F32)
    return lo, hi


def _layer_norm(z, g, b):
    mu = jnp.mean(z, axis=1, keepdims=True)
    zc = z - mu
    var = jnp.mean(zc * zc, axis=1, keepdims=True)
    return zc * lax.rsqrt(var + EPS) * g + b


def _out_proj_kernel(a_ref, l_ref, x_ref, wa_ref, wl_ref, g_ref, b_ref, wrt_hi_ref, wrt_lo_ref,
                     brt_ref, h_ref, hp_ref, info_ref):
    mix = jnp.dot(a_ref[...], wa_ref[...], preferred_element_type=F32)
    mix = mix + jnp.dot(l_ref[...], wl_ref[...], preferred_element_type=F32)
    h = _layer_norm(ALPHA * x_ref[...] + mix, g_ref[...], b_ref[...])
    h_ref[...] = h
    hp_ref[...] = _pack_rows(h)

    h_hi = h.astype(BF16)
    h_lo = (h - h_hi.astype(F32)).astype(BF16)
    nt = (((1,), (1,)), ((), ()))
    lg = (lax.dot_general(wrt_hi_ref[...], h_hi, nt, preferred_element_type=F32)
          + lax.dot_general(wrt_lo_ref[...], h_hi, nt, preferred_element_type=F32)
          + lax.dot_general(wrt_hi_ref[...], h_lo, nt, preferred_element_type=F32)) + brt_ref[...]
    tile_shape = (SUBLANES, h.shape[0])
    sub = lax.broadcasted_iota(I32, tile_shape, 0)
    ninf = -jnp.inf
    t0 = lg[0:SUBLANES]
    gl = jnp.where(sub < N_GROUPS, t0, ninf)
    gmax = jnp.max(gl, axis=0, keepdims=True)
    g_idx = jnp.min(jnp.where(gl == gmax, sub, SUBLANES), axis=0, keepdims=True)
    g_w = 1.0 / jnp.sum(jnp.where(sub < N_GROUPS, jnp.exp(t0 - gmax), 0.0),
                        axis=0, keepdims=True)
    el = lg[SUBLANES:2 * SUBLANES]
    for g in range(1, N_GROUPS):
        el = jnp.where(g_idx == g, lg[(g + 1) * SUBLANES:(g + 2) * SUBLANES], el)
    v1 = jnp.max(el, axis=0, keepdims=True)
    i1 = jnp.min(jnp.where(el == v1, sub, SUBLANES), axis=0, keepdims=True)
    el2 = jnp.where(sub == i1, ninf, el)
    v2 = jnp.max(el2, axis=0, keepdims=True)
    i2 = jnp.min(jnp.where(el2 == v2, sub, SUBLANES), axis=0, keepdims=True)
    t = jnp.exp(v2 - v1)
    w1 = 1.0 / (1.0 + t)
    w2 = t * w1
    e_base = g_idx * EXPERTS_PER_GROUP
    info_ref[...] = jnp.where(sub == 0, (e_base + i1).astype(F32),
                              jnp.where(sub == 1, (e_base + i2).astype(F32),
                                        jnp.where(sub == 2, g_w * w1,
                                                  jnp.where(sub == 3, g_w * w2, 0.0))))


ROUTER_ROWS = (N_GROUPS + 1) * SUBLANES


def _out_proj(attn_n, lru_n, x2d, wo_a, wo_l, ln_g, ln_b, w_rt_hi, w_rt_lo, b_rt, rows):
    n = x2d.shape[0]
    const = lambda i: (0, 0)
    tile = lambda w: pl.BlockSpec((rows, w), lambda i: (i, 0))
    return pl.pallas_call(
        _out_proj_kernel,
        grid=(n // rows,),
        in_specs=[tile(ATTN_WIDTH), tile(LRU_WIDTH), tile(D_MODEL),
                  pl.BlockSpec((ATTN_WIDTH, D_MODEL), const),
                  pl.BlockSpec((LRU_WIDTH, D_MODEL), const),
                  pl.BlockSpec((1, D_MODEL), const),
                  pl.BlockSpec((1, D_MODEL), const),
                  pl.BlockSpec((ROUTER_ROWS, D_MODEL), const),
                  pl.BlockSpec((ROUTER_ROWS, D_MODEL), const),
                  pl.BlockSpec((ROUTER_ROWS, 1), const)],
        out_specs=[tile(D_MODEL), tile(PACKED),
                   pl.BlockSpec((SUBLANES, rows), lambda i: (0, i))],
        out_shape=[jax.ShapeDtypeStruct((n, D_MODEL), F32),
                   jax.ShapeDtypeStruct((n, PACKED), U32),
                   jax.ShapeDtypeStruct((SUBLANES, n), F32)],
        compiler_params=_cparams(1),
        name="out_proj",
    )(attn_n, lru_n, x2d, wo_a, wo_l, ln_g, ln_b, w_rt_hi, w_rt_lo, b_rt)


def _route_kernel(info_ref, tri_ref, dest_ref, cnt_ref, counts, carry, pstart):
    p = pl.program_id(0)
    t = pl.program_id(1)
    info = info_ref[...]
    shape = (N_EXPERTS, info.shape[1])
    expert = lax.broadcasted_iota(I32, shape, 0)
    oh1 = (expert == info[0:1, :].astype(I32)).astype(F32)
    oh2 = (expert == info[1:2, :].astype(I32)).astype(F32)
    both = oh1 + oh2
    tile_counts = jnp.sum(both, axis=1, keepdims=True)

    @pl.when((p == 0) & (t == 0))
    def _():
        counts[...] = jnp.zeros_like(counts)

    @pl.when(p == 0)
    def _():
        counts[...] += tile_counts

    @pl.when((p == 1) & (t == 0))
    def _():
        c = jnp.broadcast_to(counts[...], (N_EXPERTS, LANES)).astype(I32)
        padded = ((c + (MOE_BLOCK - 1)) // MOE_BLOCK) * MOE_BLOCK
        e = lax.broadcasted_iota(I32, (N_EXPERTS, LANES), 0)
        scan = padded
        for d in (1, 2, 4, 8, 16):
            scan = scan + jnp.where(e >= d, pltpu.roll(scan, d, axis=0), 0)
        pstart[...] = (scan - padded)[:, 0:1].astype(F32)
        carry[...] = jnp.zeros_like(carry)
        cnt_ref[...] = c

    @pl.when(p == 1)
    def _():
        before = jnp.dot(both.astype(BF16), tri_ref[...], preferred_element_type=F32)
        row_of = before + (carry[...] + pstart[...])
        r1 = jnp.sum(oh1 * row_of, axis=0, keepdims=True)
        r2 = jnp.sum(oh2 * row_of, axis=0, keepdims=True)
        sub = lax.broadcasted_iota(I32, dest_ref.shape, 0)
        dest_ref[...] = jnp.where(sub == 0, r1, jnp.where(sub == 1, r2, 0.0)).astype(I32)
        carry[...] += tile_counts


def _route(info_t, cols):
    n = info_t.shape[1]
    tri = jnp.asarray(np.triu(np.ones((cols, cols), np.float32), 1), BF16)
    return pl.pallas_call(
        _route_kernel,
        grid=(2, n // cols),
        in_specs=[pl.BlockSpec((SUBLANES, cols), lambda p, t: (0, t)),
                  pl.BlockSpec((cols, cols), lambda p, t: (0, 0))],
        out_specs=[pl.BlockSpec((SUBLANES, cols), lambda p, t: (0, t * p)),
                   pl.BlockSpec((N_EXPERTS, LANES), lambda p, t: (0, 0))],
        out_shape=[jax.ShapeDtypeStruct((SUBLANES, n), I32),
                   jax.ShapeDtypeStruct((N_EXPERTS, LANES), I32)],
        scratch_shapes=[pltpu.VMEM((N_EXPERTS, 1), F32), pltpu.VMEM((N_EXPERTS, 1), F32),
                        pltpu.VMEM((N_EXPERTS, 1), F32)],
        compiler_params=_cparams(2),
        name="route",
    )(info_t, tri)


def _sc_mesh():
    return plsc.VectorSubcoreMesh(core_axis_name="core", subcore_axis_name="subcore")


def _sc_worker_id():
    return lax.axis_index("subcore") * SC_CORES + lax.axis_index("core")


def _sc_scatter_rows(rows, d0, d1, cap):
    n, width = rows.shape
    per_worker = n // SC_WORKERS
    n_win = per_worker // SC_WINDOW

    def body(x_hbm, d0_hbm, d1_hbm, o_hbm, i0_v, i1_v, rows_v, rsem, sem0, sem1):
        wid = _sc_worker_id()
        pltpu.sync_copy(d0_hbm.at[wid], i0_v)
        pltpu.sync_copy(d1_hbm.at[wid], i1_v)

        def read(j):
            src = x_hbm.at[pl.ds(wid * per_worker + j * SC_WINDOW, SC_WINDOW)]
            return pltpu.make_async_copy(src, rows_v.at[j % 2], rsem.at[j % 2])

        def scatters(j):
            return (pltpu.make_async_copy(rows_v.at[j % 2], o_hbm.at[i0_v.at[j]], sem0.at[j % 2]),
                    pltpu.make_async_copy(rows_v.at[j % 2], o_hbm.at[i1_v.at[j]], sem1.at[j % 2]))

        read(0).start()
        for j in range(n_win):
            if j + 1 < n_win:
                if j >= 1:
                    for cp in scatters(j - 1):
                        cp.wait()
                read(j + 1).start()
            read(j).wait()
            for cp in scatters(j):
                cp.start()
        for j in range(max(n_win - 2, 0), n_win):
            for cp in scatters(j):
                cp.wait()

    return pl.kernel(
        body,
        out_type=jax.ShapeDtypeStruct((cap, width), rows.dtype),
        mesh=_sc_mesh(),
        scratch_types=[pltpu.VMEM((n_win, SC_WINDOW), I32), pltpu.VMEM((n_win, SC_WINDOW), I32),
                       pltpu.VMEM((2, SC_WINDOW, width), rows.dtype),
                       pltpu.SemaphoreType.DMA((2,)), pltpu.SemaphoreType.DMA((2,)),
                       pltpu.SemaphoreType.DMA((2,))],
        name="dispatch",
    )(rows, d0, d1)


def _sc_gather_rows(table, idx):
    width = table.shape[1]
    n_win = idx.shape[1]
    per_worker = n_win * SC_WINDOW

    def body(y_hbm, i_hbm, o_hbm, i_v, rows_v, gsem, wsem):
        wid = _sc_worker_id()
        pltpu.sync_copy(i_hbm.at[wid], i_v)

        def gather(j):
            return pltpu.make_async_copy(y_hbm.at[i_v.at[j]], rows_v.at[j % 2], gsem.at[j % 2])

        def write(j):
            dst = o_hbm.at[pl.ds(wid * per_worker + j * SC_WINDOW, SC_WINDOW)]
            return pltpu.make_async_copy(rows_v.at[j % 2], dst, wsem.at[j % 2])

        gather(0).start()
        for j in range(n_win):
            if j + 1 < n_win:
                if j >= 1:
                    write(j - 1).wait()
                gather(j + 1).start()
            gather(j).wait()
            write(j).start()
        for j in range(max(n_win - 2, 0), n_win):
            write(j).wait()

    return pl.kernel(
        body,
        out_type=jax.ShapeDtypeStruct((SC_WORKERS * per_worker, width), table.dtype),
        mesh=_sc_mesh(),
        scratch_types=[pltpu.VMEM((n_win, SC_WINDOW), I32),
                       pltpu.VMEM((2, SC_WINDOW, width), table.dtype),
                       pltpu.SemaphoreType.DMA((2,)), pltpu.SemaphoreType.DMA((2,))],
        name="collect",
    )(table, idx)


def _expert_kernel(bstart_ref, nblk_ref, nused_ref, xs_hbm, wg_ref, wu_ref, wd_ref, yb_hbm,
                   xbuf, ybuf, zbuf, xsem, ysem, zsem):
    e = pl.program_id(0)
    nused = nused_ref[0]
    n_blocks = yb_hbm.shape[0] // MOE_BLOCK

    def rows(b):
        return pl.ds(pl.multiple_of(b * MOE_BLOCK, MOE_BLOCK), MOE_BLOCK)

    def x_copy(b):
        slot = b % X_RING
        return pltpu.make_async_copy(xs_hbm.at[rows(b)], xbuf.at[slot], xsem.at[slot])

    def y_copy(b, slot):
        return pltpu.make_async_copy(ybuf.at[slot], yb_hbm.at[rows(b)], ysem.at[slot])

    @pl.when(e == 0)
    def _():
        for b in range(X_RING - 1):
            @pl.when(b < nused)
            def _():
                x_copy(b).start()

    wg_b, wu_b, wd_b = wg_ref, wu_ref, wd_ref

    def block(b, carry):
        slot = b % 2
        x_copy(b).wait()

        @pl.when(b + (X_RING - 1) < nused)
        def _():
            x_copy(b + (X_RING - 1)).start()

        lo, hi = _unpack_rows(xbuf[b % X_RING])
        lo = lo.astype(BF16)
        hi = hi.astype(BF16)
        g = (jnp.dot(lo, wg_b[0:PACKED, :], preferred_element_type=F32)
             + jnp.dot(hi, wg_b[PACKED:, :], preferred_element_type=F32))
        u = (jnp.dot(lo, wu_b[0:PACKED, :], preferred_element_type=F32)
             + jnp.dot(hi, wu_b[PACKED:, :], preferred_element_type=F32))
        mid = (g * jax.nn.sigmoid(g) * u).astype(BF16)
        y = _pack_rows(jnp.dot(mid, wd_b[...], preferred_element_type=F32))

        @pl.when(b >= 2)
        def _():
            y_copy(b - 2, slot).wait()

        ybuf[slot] = y
        y_copy(b, slot).start()
        return carry

    b0 = bstart_ref[e]
    lax.fori_loop(b0, b0 + nblk_ref[e], block, 0)

    @pl.when(e == pl.num_programs(0) - 1)
    def _():
        for back in (2, 1):
            @pl.when(nused >= back)
            def _():
                y_copy(nused - back, (nused - back) % 2).wait()

        zbuf[...] = jnp.zeros_like(zbuf)

        def z_copy(b):
            return pltpu.make_async_copy(zbuf, yb_hbm.at[rows(b)], zsem.at[0])

        def z_start(b, carry):
            z_copy(b).start()
            return carry

        def z_wait(b, carry):
            z_copy(b).wait()
            return carry

        lax.fori_loop(nused, n_blocks, z_start, 0)
        lax.fori_loop(nused, n_blocks, z_wait, 0)


def _experts(bstart, nblk, nused, xs, w_gate, w_up, w_down):
    cap = xs.shape[0]
    w_idx = lambda e, bs, nb, nu: (e, 0, 0)
    grid_spec = pltpu.PrefetchScalarGridSpec(
        num_scalar_prefetch=3,
        grid=(N_EXPERTS,),
        in_specs=[pl.BlockSpec(memory_space=pl.ANY),
                  pl.BlockSpec((None, D_MODEL, D_FF), w_idx),
                  pl.BlockSpec((None, D_MODEL, D_FF), w_idx),
                  pl.BlockSpec((None, D_FF, D_MODEL), w_idx)],
        out_specs=pl.BlockSpec(memory_space=pl.ANY),
        scratch_shapes=[pltpu.VMEM((X_RING, MOE_BLOCK, PACKED), U32),
                        pltpu.VMEM((2, MOE_BLOCK, PACKED), U32),
                        pltpu.VMEM((MOE_BLOCK, PACKED), U32),
                        pltpu.SemaphoreType.DMA((X_RING,)),
                        pltpu.SemaphoreType.DMA((2,)),
                        pltpu.SemaphoreType.DMA((1,))])
    return pl.pallas_call(
        _expert_kernel,
        grid_spec=grid_spec,
        out_shape=jax.ShapeDtypeStruct((cap, PACKED), U32),
        compiler_params=_cparams(1),
        name="experts",
    )(bstart, nblk, nused, xs, w_gate, w_up, w_down)


def _combine_kernel(y0_ref, y1_ref, h_ref, info_ref, g_ref, b_ref, o_ref):
    info = info_ref[...].T
    g0 = info[:, 2:3]
    g1 = info[:, 3:4]
    lo0, hi0 = _unpack_rows(y0_ref[...])
    lo1, hi1 = _unpack_rows(y1_ref[...])
    y = jnp.concatenate([g0 * lo0 + g1 * lo1, g0 * hi0 + g1 * hi1], axis=1)
    o_ref[...] = _layer_norm(ALPHA * h_ref[...] + y, g_ref[...], b_ref[...])


def _combine(ys, h, info, ln_g, ln_b, rows, part, n_parts):
    n = h.shape[0]
    steps = n // n_parts // rows
    off = part * steps
    const = lambda i: (0, 0)
    return pl.pallas_call(
        _combine_kernel,
        grid=(steps,),
        in_specs=[pl.BlockSpec((rows, PACKED), lambda i: (i, 0)),
                  pl.BlockSpec((rows, PACKED), lambda i: (i + steps, 0)),
                  pl.BlockSpec((rows, D_MODEL), lambda i: (i + off, 0)),
                  pl.BlockSpec((SUBLANES, rows), lambda i: (0, i + off)),
                  pl.BlockSpec((1, D_MODEL), const),
                  pl.BlockSpec((1, D_MODEL), const)],
        out_specs=pl.BlockSpec((rows, D_MODEL), lambda i: (i + off, 0)),
        out_shape=jax.ShapeDtypeStruct((n, D_MODEL), F32),
        input_output_aliases={2: 0},
        compiler_params=_cparams(1),
        name="combine",
    )(ys, ys, h, info, ln_g, ln_b)


def _alibi_bias():
    qi = np.arange(BLOCK)[:, None]
    kj = np.arange(2 * BLOCK)[None, :]
    dist = qi - kj + BLOCK
    band = (dist >= 0) & (dist < BLOCK)
    slopes = np.exp2(-8.0 * np.arange(1, N_Q_HEADS + 1, dtype=np.float32) / N_Q_HEADS)
    bias = np.where(band[None], -slopes[:, None, None] * dist[None].astype(np.float32), NEG)
    first = np.where((kj >= PAD_FRONT)[None], bias, NEG)
    out = np.empty((2, N_KV_HEADS, 2 * BLOCK, 4 * BLOCK), np.float32)
    for v, per_head in enumerate((first, bias)):
        for j in range(N_KV_HEADS):
            out[v, j] = np.block([[per_head[4 * j], per_head[4 * j + 1]],
                                  [per_head[4 * j + 2], per_head[4 * j + 3]]])
    return jnp.asarray(out, F32)


def _block_diag(w):
    nb, c, _ = w.shape
    eye = jnp.eye(nb, dtype=w.dtype)
    return jnp.einsum('ncd,nm->ncmd', w, eye).reshape(nb * c, nb * c)


def kernel(x, meta_tokens, w_in, conv_w, conv_b, lru_wa, lru_ba, lru_wx, lru_bx, lru_lambda,
           attn_sinks, g_attn, g_lru, w_out, ln1_g, ln1_b, w_group, b_group, w_router,
           b_router, w_gate, w_up, w_down, ln2_g, ln2_b):
    bsz, seq, d = x.shape
    nbx = seq // BLOCK
    n_tok = bsz * seq
    x2d = x.reshape(n_tok, d)
    row = lambda v: v.reshape(1, -1).astype(F32)

    w_in_b = w_in[0].astype(BF16)
    meta_blk = jnp.concatenate([jnp.zeros((PAD_FRONT, d), F32), meta_tokens.astype(F32)], axis=0)
    q, kv, xr, yr = _in_proj(x2d, w_in_b, PROJ_ROWS)
    qm, kvm, xrm, yrm = _in_proj(meta_blk, w_in_b, BLOCK)
    shp = lambda a: a.reshape(bsz, seq, a.shape[-1])

    attn_n = _attention(attn_sinks[0].astype(F32), shp(q), shp(kv), kvm, _alibi_bias(),
                        row(g_attn[0]), bsz, nbx)
    lru_n, wg_b, wu_b, wd_b = _rglru(
        shp(xr), shp(yr), xrm, yrm, conv_w[0].astype(F32), row(conv_b[0]),
        _block_diag(lru_wa[0]).astype(BF16), _block_diag(lru_wx[0]).astype(BF16),
        row(lru_ba[0]), row(lru_bx[0]), row(lru_lambda[0]), row(g_lru[0]),
        w_gate[0], w_up[0], w_down[0], bsz, nbx)

    w_out_b = w_out[0].astype(BF16)
    gpad = SUBLANES - N_GROUPS
    w_rt = jnp.concatenate(
        [w_group[0].T, jnp.zeros((gpad, d), F32),
         jnp.transpose(w_router[0], (0, 2, 1)).reshape(N_EXPERTS, d)], axis=0).astype(F32)
    w_rt_hi = w_rt.astype(BF16)
    w_rt_lo = (w_rt - w_rt_hi.astype(F32)).astype(BF16)
    b_rt = jnp.concatenate([b_group[0], jnp.zeros((gpad,), F32),
                            b_router[0].reshape(-1)]).astype(F32).reshape(ROUTER_ROWS, 1)
    h1, hp, info = _out_proj(attn_n.reshape(n_tok, ATTN_WIDTH), lru_n.reshape(n_tok, LRU_WIDTH),
                             x2d, w_out_b[:ATTN_WIDTH], w_out_b[ATTN_WIDTH:], row(ln1_g[0]),
                             row(ln1_b[0]), w_rt_hi, w_rt_lo, b_rt, PROJ_ROWS)

    dest, cnt = _route(info, ROUTE_ROWS)
    n_slots = n_tok * TOP_K
    n_blocks = n_slots // MOE_BLOCK + N_EXPERTS
    cap = n_blocks * MOE_BLOCK
    nblk = (cnt[:, 0] + MOE_BLOCK - 1) // MOE_BLOCK
    bends = jnp.cumsum(nblk)
    bstart = (bends - nblk).astype(I32)
    nused = bends[-1:].astype(I32)
    windows = lambda v: v.reshape(SC_WORKERS, -1, SC_WINDOW)
    d0 = dest[0]
    d1 = dest[1]

    xs = _sc_scatter_rows(hp, windows(d0), windows(d1), cap)
    yb = _experts(bstart, nblk.astype(I32), nused, xs, wg_b, wu_b, wd_b)
    out = h1
    part_len = n_tok // COMBINE_PARTS
    for part in range(COMBINE_PARTS):
        tok = slice(part * part_len, (part + 1) * part_len)
        ys = _sc_gather_rows(yb, windows(jnp.concatenate([d0[tok], d1[tok]])))
        out = _combine(ys, out, info, row(ln2_g[0]), row(ln2_b[0]), COMBINE_ROWS,
                       part, COMBINE_PARTS)
    return out.reshape(bsz, seq, d)
```

```python
import jax
import jax.numpy as jnp
import numpy as np
from jax import lax
from jax.experimental import pallas as pl
from jax.experimental.pallas import tpu as pltpu
from jax.experimental.pallas import tpu_sc as plsc

F32 = jnp.float32
BF16 = jnp.bfloat16
U32 = jnp.uint32
I32 = jnp.int32

D_MODEL = 1024
N_META = 16
BLOCK = 128
PAD_FRONT = BLOCK - N_META
HEAD_DIM = 64
ATTN_WIDTH = 512
LRU_WIDTH = 512
N_Q_HEADS = 8
N_KV_HEADS = 2
KV_WIDTH = N_KV_HEADS * HEAD_DIM
LRU_BLOCKS = 8
CONV_W = 4
LRU_C = 8.0
IN_COLS = ATTN_WIDTH + 2 * KV_WIDTH + 2 * LRU_WIDTH
N_GROUPS = 4
EXPERTS_PER_GROUP = 8
N_EXPERTS = N_GROUPS * EXPERTS_PER_GROUP
TOP_K = 2
D_FF = 512
MOE_BLOCK = 256
ALPHA = 2.0 ** 0.25
EPS = 1e-5
NEG = -1e30
LANES = 128
SUBLANES = 8
PACKED = D_MODEL // 2

PROJ_ROWS = 512
OUT_PROJ_ROWS = 1024
ROUTE_ROWS = 512
COMBINE_ROWS = 256
COMBINE_PARTS = 4
X_RING = 4
VMEM_LIMIT = 48 * 1024 * 1024

SC_CORES = 2
SC_SUBCORES = 16
SC_WORKERS = SC_CORES * SC_SUBCORES
SC_WINDOW = 64


def _cparams(n_axes):
    return pltpu.CompilerParams(
        dimension_semantics=("arbitrary",) * n_axes, vmem_limit_bytes=VMEM_LIMIT)


def _in_proj_kernel(x_ref, w_ref, q_ref, kv_ref, xr_ref, yr_ref):
    proj = jnp.dot(x_ref[...].astype(BF16), w_ref[...], preferred_element_type=F32)
    o = 0
    for ref, width in ((q_ref, ATTN_WIDTH), (kv_ref, 2 * KV_WIDTH),
                       (xr_ref, LRU_WIDTH), (yr_ref, LRU_WIDTH)):
        ref[...] = proj[:, o:o + width].astype(ref.dtype)
        o += width


def _in_proj(x2d, w_bf16, rows):
    n = x2d.shape[0]
    widths = (ATTN_WIDTH, 2 * KV_WIDTH, LRU_WIDTH, LRU_WIDTH)
    return pl.pallas_call(
        _in_proj_kernel,
        grid=(n // rows,),
        in_specs=[pl.BlockSpec((rows, D_MODEL), lambda i: (i, 0)),
                  pl.BlockSpec((D_MODEL, IN_COLS), lambda i: (0, 0))],
        out_specs=[pl.BlockSpec((rows, w), lambda i: (i, 0)) for w in widths],
        out_shape=[jax.ShapeDtypeStruct((n, w), BF16) for w in widths],
        compiler_params=_cparams(1),
        name="in_proj",
    )(x2d, w_bf16)


def _attn_kernel(sinks_ref, q_ref, kv_ref, kvm_ref, bias_ref, g_ref, o_ref,
                 klo, khi, vlo, vhi):
    nbx = q_ref.shape[0] // BLOCK
    lo_lanes = lax.broadcasted_iota(I32, (BLOCK, LANES), 1) < HEAD_DIM

    def layout_block(n, blk):
        rows = pl.ds(pl.multiple_of(n * BLOCK, BLOCK), BLOCK)
        for src, dst_lo, dst_hi in ((blk[:, :KV_WIDTH], klo, khi), (blk[:, KV_WIDTH:], vlo, vhi)):
            w = src.astype(F32)
            r = pltpu.roll(w, HEAD_DIM, axis=1)
            dst_lo[0, rows, :] = jnp.where(lo_lanes, w, 0.0).astype(BF16)
            dst_hi[0, rows, :] = jnp.where(lo_lanes, 0.0, r).astype(BF16)
            dst_lo[1, rows, :] = jnp.where(lo_lanes, r, 0.0).astype(BF16)
            dst_hi[1, rows, :] = jnp.where(lo_lanes, 0.0, w).astype(BF16)

    layout_block(0, kvm_ref[...])

    def layout_body(n, carry):
        layout_block(n + 1, kv_ref[pl.ds(pl.multiple_of(n * BLOCK, BLOCK), BLOCK), :])
        return carry

    lax.fori_loop(0, nbx, layout_body, 0)

    ones_lo = jnp.where(lax.broadcasted_iota(I32, (2 * BLOCK, LANES), 1) < HEAD_DIM,
                        1.0, 0.0).astype(BF16)
    ones_hi = (1.0 - ones_lo.astype(F32)).astype(BF16)
    top_rows = lax.broadcasted_iota(I32, (2 * BLOCK, 1), 0) < BLOCK
    lo_half = lax.broadcasted_iota(I32, (2 * BLOCK, LANES), 1) < HEAD_DIM

    def block(i, carry):
        q_rows = pl.ds(pl.multiple_of(i * BLOCK, BLOCK), BLOCK)
        win = pl.ds(pl.multiple_of(i * BLOCK, BLOCK), 2 * BLOCK)
        q = q_ref[q_rows, :] * (HEAD_DIM ** -0.5)
        first = jnp.minimum(i, 1)
        outs = []
        for j in range(N_KV_HEADS):
            q2 = jnp.concatenate([q[:, (2 * j) * LANES:(2 * j + 1) * LANES],
                                  q[:, (2 * j + 1) * LANES:(2 * j + 2) * LANES]], axis=0)
            kc = jnp.concatenate([klo[j, win, :], khi[j, win, :]], axis=0)
            s = lax.dot_general(q2, kc, (((1,), (1,)), ((), ())), preferred_element_type=F32)
            s = s + bias_ref[first, j]
            ps, es = [], []
            for c in range(2):
                sink = jnp.where(top_rows, sinks_ref[4 * j + c], sinks_ref[4 * j + 2 + c])
                sc = s[:, c * 2 * BLOCK:(c + 1) * 2 * BLOCK]
                m = jnp.maximum(jnp.max(sc, axis=1, keepdims=True), sink)
                ps.append(jnp.exp(sc - m).astype(BF16))
                es.append(jnp.exp(sink - m))
            v_lo = jnp.concatenate([vlo[j, win, :], ones_lo], axis=1)
            v_hi = jnp.concatenate([vhi[j, win, :], ones_hi], axis=1)
            r = (jnp.dot(ps[0], v_lo, preferred_element_type=F32)
                 + jnp.dot(ps[1], v_hi, preferred_element_type=F32))
            den = r[:, LANES:] + jnp.where(lo_half, es[0], es[1])
            o2 = r[:, :LANES] * (1.0 / den)
            outs += [o2[:BLOCK], o2[BLOCK:]]
        out = jnp.concatenate(outs, axis=1)
        ms = jnp.mean(out * out, axis=1, keepdims=True)
        o_ref[q_rows, :] = (out * lax.rsqrt(ms + EPS) * g_ref[...]).astype(o_ref.dtype)
        return carry

    lax.fori_loop(0, nbx, block, 0, unroll=2)


def _attention(sinks, q, kv, kvm, bias, g_attn, bsz, nbx):
    seq = nbx * BLOCK
    const2 = lambda b: (0, 0)
    kv_scratch = pltpu.VMEM((N_KV_HEADS, seq + BLOCK, LANES), BF16)
    return pl.pallas_call(
        _attn_kernel,
        grid=(bsz,),
        in_specs=[pl.BlockSpec(memory_space=pltpu.SMEM),
                  pl.BlockSpec((None, seq, ATTN_WIDTH), lambda b: (b, 0, 0)),
                  pl.BlockSpec((None, seq, 2 * KV_WIDTH), lambda b: (b, 0, 0)),
                  pl.BlockSpec((BLOCK, 2 * KV_WIDTH), const2),
                  pl.BlockSpec((2, N_KV_HEADS, 2 * BLOCK, 4 * BLOCK), lambda b: (0, 0, 0, 0)),
                  pl.BlockSpec((1, ATTN_WIDTH), const2)],
        out_specs=pl.BlockSpec((None, seq, ATTN_WIDTH), lambda b: (b, 0, 0)),
        out_shape=jax.ShapeDtypeStruct((bsz, seq, ATTN_WIDTH), BF16),
        scratch_shapes=[kv_scratch, kv_scratch, kv_scratch, kv_scratch],
        compiler_params=_cparams(1),
        name="attention",
    )(sinks, q, kv, kvm, bias, g_attn)


def _gelu_tanh(y):
    c = np.sqrt(2.0 / np.pi).astype(np.float32)
    return 0.5 * y * (1.0 + jnp.tanh(c * (y + 0.044715 * (y * y * y))))


def _lru_kernel(xr_ref, yr_ref, xrm_ref, yrm_ref, cw_ref, cb_ref, wa_ref, wx_ref, ba_ref,
                bx_ref, lam_ref, g_ref, o_ref, xbuf, ycur, hprev):
    n = pl.program_id(1)

    @pl.when(n == 0)
    def _():
        xbuf[0:SUBLANES, :] = jnp.zeros((SUBLANES, LRU_WIDTH), F32)
        xbuf[SUBLANES:, :] = xrm_ref[...].astype(F32)
        ycur[...] = yrm_ref[...].astype(F32)
        hprev[...] = jnp.zeros_like(hprev)

    @pl.when(n > 0)
    def _():
        xbuf[SUBLANES:, :] = xr_ref[...].astype(F32)
        ycur[...] = yr_ref[...].astype(F32)

    xc = jnp.broadcast_to(cb_ref[...], (BLOCK, LRU_WIDTH))
    for k in range(CONV_W):
        off = SUBLANES - (CONV_W - 1) + k
        xc = xc + cw_ref[k:k + 1, :] * xbuf[off:off + BLOCK, :]
    xbuf[0:SUBLANES, :] = xbuf[BLOCK:BLOCK + SUBLANES, :]

    xcb = xc.astype(BF16)
    r = jax.nn.sigmoid(jnp.dot(xcb, wa_ref[...], preferred_element_type=F32) + ba_ref[...])
    i = jax.nn.sigmoid(jnp.dot(xcb, wx_ref[...], preferred_element_type=F32) + bx_ref[...])
    lam = lam_ref[...]
    softplus_neg = jnp.maximum(-lam, 0.0) + jnp.log(1.0 + jnp.exp(-jnp.abs(lam)))
    a = jnp.exp(-LRU_C * r * softplus_neg)
    u = jnp.sqrt(1.0 - a * a) * (i * xc)
    row = lax.broadcasted_iota(jnp.int32, (BLOCK, LRU_WIDTH), 0)
    u = jnp.where((n > 0) | (row >= PAD_FRONT), u, 0.0)

    groups = (BLOCK // SUBLANES, SUBLANES, LRU_WIDTH)
    a = a.reshape(groups)
    u = u.reshape(groups)
    r8 = lax.broadcasted_iota(jnp.int32, groups, 1)
    for d in (1, 2, 4):
        keep = r8 >= d
        a_s = jnp.where(keep, pltpu.roll(a, d, axis=1), 1.0)
        u_s = jnp.where(keep, pltpu.roll(u, d, axis=1), 0.0)
        u = a * u_s + u
        a = a * a_s
    a = a.reshape(BLOCK, LRU_WIDTH)
    u = u.reshape(BLOCK, LRU_WIDTH)
    h_in = hprev[...]
    hs = []
    for g in range(BLOCK // SUBLANES):
        sl = slice(g * SUBLANES, (g + 1) * SUBLANES)
        hg = a[sl, :] * h_in + u[sl, :]
        hs.append(hg)
        h_in = jnp.broadcast_to(hg[SUBLANES - 1:SUBLANES, :], (SUBLANES, LRU_WIDTH))
    hprev[...] = h_in
    h = jnp.concatenate(hs, axis=0)

    out = h * _gelu_tanh(ycur[...])
    ms = jnp.mean(out * out, axis=1, keepdims=True)
    o_ref[...] = (out * lax.rsqrt(ms + EPS) * g_ref[...]).astype(o_ref.dtype)


def _rglru(xr, yr, xrm, yrm, cw, cb, wa, wx, ba, bx, lam, g_lru, bsz, nbx):
    main = lambda b, n: (b, jnp.maximum(n - 1, 0), 0)
    const2 = lambda b, n: (0, 0)
    row_spec = pl.BlockSpec((1, LRU_WIDTH), const2)
    return pl.pallas_call(
        _lru_kernel,
        grid=(bsz, nbx + 1),
        in_specs=[pl.BlockSpec((None, BLOCK, LRU_WIDTH), main),
                  pl.BlockSpec((None, BLOCK, LRU_WIDTH), main),
                  pl.BlockSpec((BLOCK, LRU_WIDTH), const2),
                  pl.BlockSpec((BLOCK, LRU_WIDTH), const2),
                  pl.BlockSpec((CONV_W, LRU_WIDTH), const2),
                  row_spec,
                  pl.BlockSpec((LRU_WIDTH, LRU_WIDTH), const2),
                  pl.BlockSpec((LRU_WIDTH, LRU_WIDTH), const2),
                  row_spec, row_spec, row_spec, row_spec],
        out_specs=pl.BlockSpec((None, BLOCK, LRU_WIDTH), main),
        out_shape=jax.ShapeDtypeStruct((bsz, nbx * BLOCK, LRU_WIDTH), BF16),
        scratch_shapes=[pltpu.VMEM((BLOCK + SUBLANES, LRU_WIDTH), F32),
                        pltpu.VMEM((BLOCK, LRU_WIDTH), F32),
                        pltpu.VMEM((SUBLANES, LRU_WIDTH), F32)],
        compiler_params=_cparams(2),
        name="rglru",
    )(xr, yr, xrm, yrm, cw, cb, wa, wx, ba, bx, lam, g_lru)


def _pack_rows(v):
    bits = lax.bitcast_convert_type(v.astype(BF16).astype(F32), U32)
    return (bits[:, :PACKED] >> 16) | (bits[:, PACKED:] & jnp.uint32(0xFFFF0000))


def _unpack_rows(w):
    lo = lax.bitcast_convert_type(w << 16, F32)
    hi = lax.bitcast_convert_type(w & jnp.uint32(0xFFFF0000), F32)
    return lo, hi


def _layer_norm(z, g, b):
    mu = jnp.mean(z, axis=1, keepdims=True)
    zc = z - mu
    var = jnp.mean(zc * zc, axis=1, keepdims=True)
    return zc * lax.rsqrt(var + EPS) * g + b


def _out_proj_kernel(a_ref, l_ref, x_ref, wa_ref, wl_ref, g_ref, b_ref, wrt_hi_ref, wrt_lo_ref,
                     brt_ref, h_ref, hp_ref, info_ref):
    for c in range(h_ref.shape[0] // PROJ_ROWS):
        rows = slice(c * PROJ_ROWS, (c + 1) * PROJ_ROWS)
        _out_proj_rows(a_ref[rows, :], l_ref[rows, :], x_ref[rows, :], wa_ref, wl_ref, g_ref,
                       b_ref, wrt_hi_ref, wrt_lo_ref, brt_ref,
                       h_ref.at[rows, :], hp_ref.at[rows, :], info_ref.at[:, rows])


def _out_proj_rows(a, l, x, wa_ref, wl_ref, g_ref, b_ref, wrt_hi_ref, wrt_lo_ref, brt_ref,
                   h_ref, hp_ref, info_ref):
    mix = jnp.dot(a, wa_ref[...], preferred_element_type=F32)
    mix = mix + jnp.dot(l, wl_ref[...], preferred_element_type=F32)
    h = _layer_norm(ALPHA * x + mix, g_ref[...], b_ref[...])
    h_ref[...] = h
    hp_ref[...] = _pack_rows(h)

    h_hi = h.astype(BF16)
    h_lo = (h - h_hi.astype(F32)).astype(BF16)
    nt = (((1,), (1,)), ((), ()))
    lg = (lax.dot_general(wrt_hi_ref[...], h_hi, nt, preferred_element_type=F32)
          + lax.dot_general(wrt_lo_ref[...], h_hi, nt, preferred_element_type=F32)
          + lax.dot_general(wrt_hi_ref[...], h_lo, nt, preferred_element_type=F32)) + brt_ref[...]
    tile_shape = (SUBLANES, h.shape[0])
    sub = lax.broadcasted_iota(I32, tile_shape, 0)
    ninf = -jnp.inf
    t0 = lg[0:SUBLANES]
    gl = jnp.where(sub < N_GROUPS, t0, ninf)
    gmax = jnp.max(gl, axis=0, keepdims=True)
    g_idx = jnp.min(jnp.where(gl == gmax, sub, SUBLANES), axis=0, keepdims=True)
    g_w = 1.0 / jnp.sum(jnp.where(sub < N_GROUPS, jnp.exp(t0 - gmax), 0.0),
                        axis=0, keepdims=True)
    el = lg[SUBLANES:2 * SUBLANES]
    for g in range(1, N_GROUPS):
        el = jnp.where(g_idx == g, lg[(g + 1) * SUBLANES:(g + 2) * SUBLANES], el)
    v1 = jnp.max(el, axis=0, keepdims=True)
    i1 = jnp.min(jnp.where(el == v1, sub, SUBLANES), axis=0, keepdims=True)
    el2 = jnp.where(sub == i1, ninf, el)
    v2 = jnp.max(el2, axis=0, keepdims=True)
    i2 = jnp.min(jnp.where(el2 == v2, sub, SUBLANES), axis=0, keepdims=True)
    t = jnp.exp(v2 - v1)
    w1 = 1.0 / (1.0 + t)
    w2 = t * w1
    e_base = g_idx * EXPERTS_PER_GROUP
    info_ref[...] = jnp.where(sub == 0, (e_base + i1).astype(F32),
                              jnp.where(sub == 1, (e_base + i2).astype(F32),
                                        jnp.where(sub == 2, g_w * w1,
                                                  jnp.where(sub == 3, g_w * w2, 0.0))))


ROUTER_ROWS = (N_GROUPS + 1) * SUBLANES


def _out_proj(attn_n, lru_n, x2d, wo_a, wo_l, ln_g, ln_b, w_rt_hi, w_rt_lo, b_rt, rows):
    n = x2d.shape[0]
    const = lambda i: (0, 0)
    tile = lambda w: pl.BlockSpec((rows, w), lambda i: (i, 0))
    return pl.pallas_call(
        _out_proj_kernel,
        grid=(n // rows,),
        in_specs=[tile(ATTN_WIDTH), tile(LRU_WIDTH), tile(D_MODEL),
                  pl.BlockSpec((ATTN_WIDTH, D_MODEL), const),
                  pl.BlockSpec((LRU_WIDTH, D_MODEL), const),
                  pl.BlockSpec((1, D_MODEL), const),
                  pl.BlockSpec((1, D_MODEL), const),
                  pl.BlockSpec((ROUTER_ROWS, D_MODEL), const),
                  pl.BlockSpec((ROUTER_ROWS, D_MODEL), const),
                  pl.BlockSpec((ROUTER_ROWS, 1), const)],
        out_specs=[tile(D_MODEL), tile(PACKED),
                   pl.BlockSpec((SUBLANES, rows), lambda i: (0, i))],
        out_shape=[jax.ShapeDtypeStruct((n, D_MODEL), F32),
                   jax.ShapeDtypeStruct((n, PACKED), U32),
                   jax.ShapeDtypeStruct((SUBLANES, n), F32)],
        compiler_params=_cparams(1),
        name="out_proj",
    )(attn_n, lru_n, x2d, wo_a, wo_l, ln_g, ln_b, w_rt_hi, w_rt_lo, b_rt)


def _route_kernel(info_ref, tri_ref, dest_ref, cnt_ref, counts, carry, pstart):
    p = pl.program_id(0)
    t = pl.program_id(1)
    info = info_ref[...]
    shape = (N_EXPERTS, info.shape[1])
    expert = lax.broadcasted_iota(I32, shape, 0)
    oh1 = (expert == info[0:1, :].astype(I32)).astype(F32)
    oh2 = (expert == info[1:2, :].astype(I32)).astype(F32)
    both = oh1 + oh2
    tile_counts = jnp.sum(both, axis=1, keepdims=True)

    @pl.when((p == 0) & (t == 0))
    def _():
        counts[...] = jnp.zeros_like(counts)

    @pl.when(p == 0)
    def _():
        counts[...] += tile_counts

    @pl.when((p == 1) & (t == 0))
    def _():
        c = jnp.broadcast_to(counts[...], (N_EXPERTS, LANES)).astype(I32)
        padded = ((c + (MOE_BLOCK - 1)) // MOE_BLOCK) * MOE_BLOCK
        e = lax.broadcasted_iota(I32, (N_EXPERTS, LANES), 0)
        scan = padded
        for d in (1, 2, 4, 8, 16):
            scan = scan + jnp.where(e >= d, pltpu.roll(scan, d, axis=0), 0)
        pstart[...] = (scan - padded)[:, 0:1].astype(F32)
        carry[...] = jnp.zeros_like(carry)
        cnt_ref[...] = c

    @pl.when(p == 1)
    def _():
        before = jnp.dot(both.astype(BF16), tri_ref[...], preferred_element_type=F32)
        row_of = before + (carry[...] + pstart[...])
        r1 = jnp.sum(oh1 * row_of, axis=0, keepdims=True)
        r2 = jnp.sum(oh2 * row_of, axis=0, keepdims=True)
        sub = lax.broadcasted_iota(I32, dest_ref.shape, 0)
        dest_ref[...] = jnp.where(sub == 0, r1, jnp.where(sub == 1, r2, 0.0)).astype(I32)
        carry[...] += tile_counts


def _route(info_t, cols):
    n = info_t.shape[1]
    tri = jnp.asarray(np.triu(np.ones((cols, cols), np.float32), 1), BF16)
    return pl.pallas_call(
        _route_kernel,
        grid=(2, n // cols),
        in_specs=[pl.BlockSpec((SUBLANES, cols), lambda p, t: (0, t)),
                  pl.BlockSpec((cols, cols), lambda p, t: (0, 0))],
        out_specs=[pl.BlockSpec((SUBLANES, cols), lambda p, t: (0, t * p)),
                   pl.BlockSpec((N_EXPERTS, LANES), lambda p, t: (0, 0))],
        out_shape=[jax.ShapeDtypeStruct((SUBLANES, n), I32),
                   jax.ShapeDtypeStruct((N_EXPERTS, LANES), I32)],
        scratch_shapes=[pltpu.VMEM((N_EXPERTS, 1), F32), pltpu.VMEM((N_EXPERTS, 1), F32),
                        pltpu.VMEM((N_EXPERTS, 1), F32)],
        compiler_params=_cparams(2),
        name="route",
    )(info_t, tri)


def _sc_mesh():
    return plsc.VectorSubcoreMesh(core_axis_name="core", subcore_axis_name="subcore")


def _sc_worker_id():
    return lax.axis_index("subcore") * SC_CORES + lax.axis_index("core")


def _sc_scatter_rows(rows, d0, d1, cap):
    n, width = rows.shape
    per_worker = n // SC_WORKERS
    n_win = per_worker // SC_WINDOW

    def body(x_hbm, d0_hbm, d1_hbm, o_hbm, i0_v, i1_v, rows_v, rsem, sem0, sem1):
        wid = _sc_worker_id()
        pltpu.sync_copy(d0_hbm.at[wid], i0_v)
        pltpu.sync_copy(d1_hbm.at[wid], i1_v)

        def read(j):
            src = x_hbm.at[pl.ds(wid * per_worker + j * SC_WINDOW, SC_WINDOW)]
            return pltpu.make_async_copy(src, rows_v.at[j % 2], rsem.at[j % 2])

        def scatters(j):
            return (pltpu.make_async_copy(rows_v.at[j % 2], o_hbm.at[i0_v.at[j]], sem0.at[j % 2]),
                    pltpu.make_async_copy(rows_v.at[j % 2], o_hbm.at[i1_v.at[j]], sem1.at[j % 2]))

        read(0).start()
        for j in range(n_win):
            if j + 1 < n_win:
                if j >= 1:
                    for cp in scatters(j - 1):
                        cp.wait()
                read(j + 1).start()
            read(j).wait()
            for cp in scatters(j):
                cp.start()
        for j in range(max(n_win - 2, 0), n_win):
            for cp in scatters(j):
                cp.wait()

    return pl.kernel(
        body,
        out_type=jax.ShapeDtypeStruct((cap, width), rows.dtype),
        mesh=_sc_mesh(),
        scratch_types=[pltpu.VMEM((n_win, SC_WINDOW), I32), pltpu.VMEM((n_win, SC_WINDOW), I32),
                       pltpu.VMEM((2, SC_WINDOW, width), rows.dtype),
                       pltpu.SemaphoreType.DMA((2,)), pltpu.SemaphoreType.DMA((2,)),
                       pltpu.SemaphoreType.DMA((2,))],
        name="dispatch",
    )(rows, d0, d1)


def _sc_gather_rows(table, idx):
    width = table.shape[1]
    n_win = idx.shape[1]
    per_worker = n_win * SC_WINDOW

    def body(y_hbm, i_hbm, o_hbm, i_v, rows_v, gsem, wsem):
        wid = _sc_worker_id()
        pltpu.sync_copy(i_hbm.at[wid], i_v)

        def gather(j):
            return pltpu.make_async_copy(y_hbm.at[i_v.at[j]], rows_v.at[j % 2], gsem.at[j % 2])

        def write(j):
            dst = o_hbm.at[pl.ds(wid * per_worker + j * SC_WINDOW, SC_WINDOW)]
            return pltpu.make_async_copy(rows_v.at[j % 2], dst, wsem.at[j % 2])

        gather(0).start()
        for j in range(n_win):
            if j + 1 < n_win:
                if j >= 1:
                    write(j - 1).wait()
                gather(j + 1).start()
            gather(j).wait()
            write(j).start()
        for j in range(max(n_win - 2, 0), n_win):
            write(j).wait()

    return pl.kernel(
        body,
        out_type=jax.ShapeDtypeStruct((SC_WORKERS * per_worker, width), table.dtype),
        mesh=_sc_mesh(),
        scratch_types=[pltpu.VMEM((n_win, SC_WINDOW), I32),
                       pltpu.VMEM((2, SC_WINDOW, width), table.dtype),
                       pltpu.SemaphoreType.DMA((2,)), pltpu.SemaphoreType.DMA((2,))],
        name="collect",
    )(table, idx)


def _expert_kernel(bstart_ref, nblk_ref, nused_ref, xs_hbm, wg_ref, wu_ref, wd_ref, yb_hbm,
                   xbuf, ybuf, zbuf, xsem, ysem, zsem, wg_b, wu_b, wd_b):
    e = pl.program_id(0)
    nused = nused_ref[0]
    n_blocks = yb_hbm.shape[0] // MOE_BLOCK

    def rows(b):
        return pl.ds(pl.multiple_of(b * MOE_BLOCK, MOE_BLOCK), MOE_BLOCK)

    def x_copy(b):
        slot = b % X_RING
        return pltpu.make_async_copy(xs_hbm.at[rows(b)], xbuf.at[slot], xsem.at[slot])

    def y_copy(b, slot):
        return pltpu.make_async_copy(ybuf.at[slot], yb_hbm.at[rows(b)], ysem.at[slot])

    @pl.when(e == 0)
    def _():
        for b in range(X_RING - 1):
            @pl.when(b < nused)
            def _():
                x_copy(b).start()

    wg_b[...] = wg_ref[...].astype(BF16)
    wu_b[...] = wu_ref[...].astype(BF16)
    wd_b[...] = wd_ref[...].astype(BF16)

    def block(b, carry):
        slot = b % 2
        x_copy(b).wait()

        @pl.when(b + (X_RING - 1) < nused)
        def _():
            x_copy(b + (X_RING - 1)).start()

        lo, hi = _unpack_rows(xbuf[b % X_RING])
        lo = lo.astype(BF16)
        hi = hi.astype(BF16)
        g = (jnp.dot(lo, wg_b[0:PACKED, :], preferred_element_type=F32)
             + jnp.dot(hi, wg_b[PACKED:, :], preferred_element_type=F32))
        u = (jnp.dot(lo, wu_b[0:PACKED, :], preferred_element_type=F32)
             + jnp.dot(hi, wu_b[PACKED:, :], preferred_element_type=F32))
        mid = (g * jax.nn.sigmoid(g) * u).astype(BF16)
        y = _pack_rows(jnp.dot(mid, wd_b[...], preferred_element_type=F32))

        @pl.when(b >= 2)
        def _():
            y_copy(b - 2, slot).wait()

        ybuf[slot] = y
        y_copy(b, slot).start()
        return carry

    b0 = bstart_ref[e]
    lax.fori_loop(b0, b0 + nblk_ref[e], block, 0)

    @pl.when(e == pl.num_programs(0) - 1)
    def _():
        for back in (2, 1):
            @pl.when(nused >= back)
            def _():
                y_copy(nused - back, (nused - back) % 2).wait()

        zbuf[...] = jnp.zeros_like(zbuf)

        def z_copy(b):
            return pltpu.make_async_copy(zbuf, yb_hbm.at[rows(b)], zsem.at[0])

        def z_start(b, carry):
            z_copy(b).start()
            return carry

        def z_wait(b, carry):
            z_copy(b).wait()
            return carry

        lax.fori_loop(nused, n_blocks, z_start, 0)
        lax.fori_loop(nused, n_blocks, z_wait, 0)


def _experts(bstart, nblk, nused, xs, w_gate, w_up, w_down):
    cap = xs.shape[0]
    w_idx = lambda e, bs, nb, nu: (e, 0, 0)
    grid_spec = pltpu.PrefetchScalarGridSpec(
        num_scalar_prefetch=3,
        grid=(N_EXPERTS,),
        in_specs=[pl.BlockSpec(memory_space=pl.ANY),
                  pl.BlockSpec((None, D_MODEL, D_FF), w_idx),
                  pl.BlockSpec((None, D_MODEL, D_FF), w_idx),
                  pl.BlockSpec((None, D_FF, D_MODEL), w_idx)],
        out_specs=pl.BlockSpec(memory_space=pl.ANY),
        scratch_shapes=[pltpu.VMEM((X_RING, MOE_BLOCK, PACKED), U32),
                        pltpu.VMEM((2, MOE_BLOCK, PACKED), U32),
                        pltpu.VMEM((MOE_BLOCK, PACKED), U32),
                        pltpu.SemaphoreType.DMA((X_RING,)),
                        pltpu.SemaphoreType.DMA((2,)),
                        pltpu.SemaphoreType.DMA((1,)),
                        pltpu.VMEM((D_MODEL, D_FF), BF16),
                        pltpu.VMEM((D_MODEL, D_FF), BF16),
                        pltpu.VMEM((D_FF, D_MODEL), BF16)])
    return pl.pallas_call(
        _expert_kernel,
        grid_spec=grid_spec,
        out_shape=jax.ShapeDtypeStruct((cap, PACKED), U32),
        compiler_params=_cparams(1),
        name="experts",
    )(bstart, nblk, nused, xs, w_gate, w_up, w_down)


def _combine_kernel(y0_ref, y1_ref, h_ref, info_ref, g_ref, b_ref, o_ref):
    info = info_ref[...].T
    g0 = info[:, 2:3]
    g1 = info[:, 3:4]
    lo0, hi0 = _unpack_rows(y0_ref[...])
    lo1, hi1 = _unpack_rows(y1_ref[...])
    y = jnp.concatenate([g0 * lo0 + g1 * lo1, g0 * hi0 + g1 * hi1], axis=1)
    o_ref[...] = _layer_norm(ALPHA * h_ref[...] + y, g_ref[...], b_ref[...])


def _combine(ys, h, info, ln_g, ln_b, rows, part, n_parts):
    n = h.shape[0]
    steps = n // n_parts // rows
    off = part * steps
    const = lambda i: (0, 0)
    return pl.pallas_call(
        _combine_kernel,
        grid=(steps,),
        in_specs=[pl.BlockSpec((rows, PACKED), lambda i: (i, 0)),
                  pl.BlockSpec((rows, PACKED), lambda i: (i + steps, 0)),
                  pl.BlockSpec((rows, D_MODEL), lambda i: (i + off, 0)),
                  pl.BlockSpec((SUBLANES, rows), lambda i: (0, i + off)),
                  pl.BlockSpec((1, D_MODEL), const),
                  pl.BlockSpec((1, D_MODEL), const)],
        out_specs=pl.BlockSpec((rows, D_MODEL), lambda i: (i + off, 0)),
        out_shape=jax.ShapeDtypeStruct((n, D_MODEL), F32),
        input_output_aliases={2: 0},
        compiler_params=_cparams(1),
        name="combine",
    )(ys, ys, h, info, ln_g, ln_b)


def _alibi_bias():
    qi = np.arange(BLOCK)[:, None]
    kj = np.arange(2 * BLOCK)[None, :]
    dist = qi - kj + BLOCK
    band = (dist >= 0) & (dist < BLOCK)
    slopes = np.exp2(-8.0 * np.arange(1, N_Q_HEADS + 1, dtype=np.float32) / N_Q_HEADS)
    bias = np.where(band[None], -slopes[:, None, None] * dist[None].astype(np.float32), NEG)
    first = np.where((kj >= PAD_FRONT)[None], bias, NEG)
    out = np.empty((2, N_KV_HEADS, 2 * BLOCK, 4 * BLOCK), np.float32)
    for v, per_head in enumerate((first, bias)):
        for j in range(N_KV_HEADS):
            out[v, j] = np.block([[per_head[4 * j], per_head[4 * j + 1]],
                                  [per_head[4 * j + 2], per_head[4 * j + 3]]])
    return jnp.asarray(out, F32)


def _block_diag(w):
    nb, c, _ = w.shape
    eye = jnp.eye(nb, dtype=w.dtype)
    return jnp.einsum('ncd,nm->ncmd', w, eye).reshape(nb * c, nb * c)


def kernel(x, meta_tokens, w_in, conv_w, conv_b, lru_wa, lru_ba, lru_wx, lru_bx, lru_lambda,
           attn_sinks, g_attn, g_lru, w_out, ln1_g, ln1_b, w_group, b_group, w_router,
           b_router, w_gate, w_up, w_down, ln2_g, ln2_b):
    bsz, seq, d = x.shape
    nbx = seq // BLOCK
    n_tok = bsz * seq
    x2d = x.reshape(n_tok, d)
    row = lambda v: v.reshape(1, -1).astype(F32)

    w_in_b = w_in[0].astype(BF16)
    meta_blk = jnp.concatenate([jnp.zeros((PAD_FRONT, d), F32), meta_tokens.astype(F32)], axis=0)
    q, kv, xr, yr = _in_proj(x2d, w_in_b, PROJ_ROWS)
    qm, kvm, xrm, yrm = _in_proj(meta_blk, w_in_b, BLOCK)
    shp = lambda a: a.reshape(bsz, seq, a.shape[-1])

    attn_n = _attention(attn_sinks[0].astype(F32), shp(q), shp(kv), kvm, _alibi_bias(),
                        row(g_attn[0]), bsz, nbx)
    lru_n = _rglru(shp(xr), shp(yr), xrm, yrm, conv_w[0].astype(F32), row(conv_b[0]),
                   _block_diag(lru_wa[0]).astype(BF16), _block_diag(lru_wx[0]).astype(BF16),
                   row(lru_ba[0]), row(lru_bx[0]), row(lru_lambda[0]), row(g_lru[0]), bsz, nbx)

    w_out_b = w_out[0].astype(BF16)
    gpad = SUBLANES - N_GROUPS
    w_rt = jnp.concatenate(
        [w_group[0].T, jnp.zeros((gpad, d), F32),
         jnp.transpose(w_router[0], (0, 2, 1)).reshape(N_EXPERTS, d)], axis=0).astype(F32)
    w_rt_hi = w_rt.astype(BF16)
    w_rt_lo = (w_rt - w_rt_hi.astype(F32)).astype(BF16)
    b_rt = jnp.concatenate([b_group[0], jnp.zeros((gpad,), F32),
                            b_router[0].reshape(-1)]).astype(F32).reshape(ROUTER_ROWS, 1)
    h1, hp, info = _out_proj(attn_n.reshape(n_tok, ATTN_WIDTH), lru_n.reshape(n_tok, LRU_WIDTH),
                             x2d, w_out_b[:ATTN_WIDTH], w_out_b[ATTN_WIDTH:], row(ln1_g[0]),
                             row(ln1_b[0]), w_rt_hi, w_rt_lo, b_rt, OUT_PROJ_ROWS)

    dest, cnt = _route(info, ROUTE_ROWS)
    n_slots = n_tok * TOP_K
    n_blocks = n_slots // MOE_BLOCK + N_EXPERTS
    cap = n_blocks * MOE_BLOCK
    nblk = (cnt[:, 0] + MOE_BLOCK - 1) // MOE_BLOCK
    bends = jnp.cumsum(nblk)
    bstart = (bends - nblk).astype(I32)
    nused = bends[-1:].astype(I32)
    windows = lambda v: v.reshape(SC_WORKERS, -1, SC_WINDOW)
    d0 = dest[0]
    d1 = dest[1]

    xs = _sc_scatter_rows(hp, windows(d0), windows(d1), cap)
    yb = _experts(bstart, nblk.astype(I32), nused, xs, w_gate[0], w_up[0], w_down[0])
    out = h1
    part_len = n_tok // COMBINE_PARTS
    for part in range(COMBINE_PARTS):
        tok = slice(part * part_len, (part + 1) * part_len)
        ys = _sc_gather_rows(yb, windows(jnp.concatenate([d0[tok], d1[tok]])))
        out = _combine(ys, out, info, row(ln2_g[0]), row(ln2_b[0]), COMBINE_ROWS,
                       part, COMBINE_PARTS)
    return out.reshape(bsz, seq, d)
```

```python
import jax
import jax.numpy as jnp
import numpy as np
from jax import lax
from jax.experimental import pallas as pl
from jax.experimental.pallas import tpu as pltpu
from jax.experimental.pallas import tpu_sc as plsc

F32 = jnp.float32
BF16 = jnp.bfloat16
U32 = jnp.uint32
I32 = jnp.int32

D_MODEL = 1024
N_META = 16
BLOCK = 128
PAD_FRONT = BLOCK - N_META
HEAD_DIM = 64
ATTN_WIDTH = 512
LRU_WIDTH = 512
N_Q_HEADS = 8
N_KV_HEADS = 2
KV_WIDTH = N_KV_HEADS * HEAD_DIM
LRU_BLOCKS = 8
CONV_W = 4
LRU_C = 8.0
IN_COLS = ATTN_WIDTH + 2 * KV_WIDTH + 2 * LRU_WIDTH
N_GROUPS = 4
EXPERTS_PER_GROUP = 8
N_EXPERTS = N_GROUPS * EXPERTS_PER_GROUP
TOP_K = 2
D_FF = 512
MOE_BLOCK = 256
ALPHA = 2.0 ** 0.25
EPS = 1e-5
NEG = -1e30
LANES = 128
SUBLANES = 8
PACKED = D_MODEL // 2

PROJ_ROWS = 512
OUT_PROJ_ROWS = 1024
ROUTE_ROWS = 512
COMBINE_ROWS = 512
COMBINE_PARTS = 4
X_RING = 4
VMEM_LIMIT = 48 * 1024 * 1024

SC_CORES = 2
SC_SUBCORES = 16
SC_WORKERS = SC_CORES * SC_SUBCORES
SC_WINDOW = 64


def _cparams(n_axes):
    return pltpu.CompilerParams(
        dimension_semantics=("arbitrary",) * n_axes, vmem_limit_bytes=VMEM_LIMIT)


def _in_proj_kernel(x_ref, w_ref, q_ref, kv_ref, xr_ref, yr_ref):
    proj = jnp.dot(x_ref[...].astype(BF16), w_ref[...], preferred_element_type=F32)
    o = 0
    for ref, width in ((q_ref, ATTN_WIDTH), (kv_ref, 2 * KV_WIDTH),
                       (xr_ref, LRU_WIDTH), (yr_ref, LRU_WIDTH)):
        ref[...] = proj[:, o:o + width].astype(ref.dtype)
        o += width


def _in_proj(x2d, w_bf16, rows):
    n = x2d.shape[0]
    widths = (ATTN_WIDTH, 2 * KV_WIDTH, LRU_WIDTH, LRU_WIDTH)
    return pl.pallas_call(
        _in_proj_kernel,
        grid=(n // rows,),
        in_specs=[pl.BlockSpec((rows, D_MODEL), lambda i: (i, 0)),
                  pl.BlockSpec((D_MODEL, IN_COLS), lambda i: (0, 0))],
        out_specs=[pl.BlockSpec((rows, w), lambda i: (i, 0)) for w in widths],
        out_shape=[jax.ShapeDtypeStruct((n, w), BF16) for w in widths],
        compiler_params=_cparams(1),
        name="in_proj",
    )(x2d, w_bf16)


def _attn_kernel(sinks_ref, q_ref, kv_ref, kvm_ref, bias_ref, g_ref, o_ref,
                 klo, khi, vlo, vhi):
    nbx = q_ref.shape[0] // BLOCK
    lo_lanes = lax.broadcasted_iota(I32, (BLOCK, LANES), 1) < HEAD_DIM

    def layout_block(n, blk):
        rows = pl.ds(pl.multiple_of(n * BLOCK, BLOCK), BLOCK)
        for src, dst_lo, dst_hi in ((blk[:, :KV_WIDTH], klo, khi), (blk[:, KV_WIDTH:], vlo, vhi)):
            w = src.astype(F32)
            r = pltpu.roll(w, HEAD_DIM, axis=1)
            dst_lo[0, rows, :] = jnp.where(lo_lanes, w, 0.0).astype(BF16)
            dst_hi[0, rows, :] = jnp.where(lo_lanes, 0.0, r).astype(BF16)
            dst_lo[1, rows, :] = jnp.where(lo_lanes, r, 0.0).astype(BF16)
            dst_hi[1, rows, :] = jnp.where(lo_lanes, 0.0, w).astype(BF16)

    layout_block(0, kvm_ref[...])

    def layout_body(n, carry):
        layout_block(n + 1, kv_ref[pl.ds(pl.multiple_of(n * BLOCK, BLOCK), BLOCK), :])
        return carry

    lax.fori_loop(0, nbx, layout_body, 0)

    ones_lo = jnp.where(lax.broadcasted_iota(I32, (2 * BLOCK, LANES), 1) < HEAD_DIM,
                        1.0, 0.0).astype(BF16)
    ones_hi = (1.0 - ones_lo.astype(F32)).astype(BF16)
    top_rows = lax.broadcasted_iota(I32, (2 * BLOCK, 1), 0) < BLOCK
    lo_half = lax.broadcasted_iota(I32, (2 * BLOCK, LANES), 1) < HEAD_DIM

    def block(i, carry):
        q_rows = pl.ds(pl.multiple_of(i * BLOCK, BLOCK), BLOCK)
        win = pl.ds(pl.multiple_of(i * BLOCK, BLOCK), 2 * BLOCK)
        q = q_ref[q_rows, :] * (HEAD_DIM ** -0.5)
        first = jnp.minimum(i, 1)
        outs = []
        for j in range(N_KV_HEADS):
            q2 = jnp.concatenate([q[:, (2 * j) * LANES:(2 * j + 1) * LANES],
                                  q[:, (2 * j + 1) * LANES:(2 * j + 2) * LANES]], axis=0)
            kc = jnp.concatenate([klo[j, win, :], khi[j, win, :]], axis=0)
            s = lax.dot_general(q2, kc, (((1,), (1,)), ((), ())), preferred_element_type=F32)
            s = s + bias_ref[first, j]
            ps, es = [], []
            for c in range(2):
                sink = jnp.where(top_rows, sinks_ref[4 * j + c], sinks_ref[4 * j + 2 + c])
                sc = s[:, c * 2 * BLOCK:(c + 1) * 2 * BLOCK]
                m = jnp.maximum(jnp.max(sc, axis=1, keepdims=True), sink)
                ps.append(jnp.exp(sc - m).astype(BF16))
                es.append(jnp.exp(sink - m))
            v_lo = jnp.concatenate([vlo[j, win, :], ones_lo], axis=1)
            v_hi = jnp.concatenate([vhi[j, win, :], ones_hi], axis=1)
            r = (jnp.dot(ps[0], v_lo, preferred_element_type=F32)
                 + jnp.dot(ps[1], v_hi, preferred_element_type=F32))
            den = r[:, LANES:] + jnp.where(lo_half, es[0], es[1])
            o2 = r[:, :LANES] * (1.0 / den)
            outs += [o2[:BLOCK], o2[BLOCK:]]
        out = jnp.concatenate(outs, axis=1)
        ms = jnp.mean(out * out, axis=1, keepdims=True)
        o_ref[q_rows, :] = (out * lax.rsqrt(ms + EPS) * g_ref[...]).astype(o_ref.dtype)
        return carry

    lax.fori_loop(0, nbx, block, 0, unroll=2)


def _attention(sinks, q, kv, kvm, bias, g_attn, bsz, nbx):
    seq = nbx * BLOCK
    const2 = lambda b: (0, 0)
    kv_scratch = pltpu.VMEM((N_KV_HEADS, seq + BLOCK, LANES), BF16)
    return pl.pallas_call(
        _attn_kernel,
        grid=(bsz,),
        in_specs=[pl.BlockSpec(memory_space=pltpu.SMEM),
                  pl.BlockSpec((None, seq, ATTN_WIDTH), lambda b: (b, 0, 0)),
                  pl.BlockSpec((None, seq, 2 * KV_WIDTH), lambda b: (b, 0, 0)),
                  pl.BlockSpec((BLOCK, 2 * KV_WIDTH), const2),
                  pl.BlockSpec((2, N_KV_HEADS, 2 * BLOCK, 4 * BLOCK), lambda b: (0, 0, 0, 0)),
                  pl.BlockSpec((1, ATTN_WIDTH), const2)],
        out_specs=pl.BlockSpec((None, seq, ATTN_WIDTH), lambda b: (b, 0, 0)),
        out_shape=jax.ShapeDtypeStruct((bsz, seq, ATTN_WIDTH), BF16),
        scratch_shapes=[kv_scratch, kv_scratch, kv_scratch, kv_scratch],
        compiler_params=_cparams(1),
        name="attention",
    )(sinks, q, kv, kvm, bias, g_attn)


def _gelu_tanh(y):
    c = np.sqrt(2.0 / np.pi).astype(np.float32)
    return 0.5 * y * (1.0 + jnp.tanh(c * (y + 0.044715 * (y * y * y))))


LRU_CHUNK = 44
LRU_SEG = SUBLANES * LRU_CHUNK
LRU_SLABS = LRU_WIDTH // LANES


def _lru_kernel(xr_ref, yr_ref, xrm_ref, yrm_ref, cw_ref, cb_ref, wa_ref, wx_ref, ba_ref,
                bx_ref, lam_ref, g_ref, o_ref, xp, yp, op, xbuf, a_scr, u_scr, hcar):
    seq = xr_ref.shape[0]
    n_seg = (seq + BLOCK) // LRU_SEG
    xp[0:BLOCK, :] = xrm_ref[...]
    xp[BLOCK:, :] = xr_ref[...]
    yp[0:BLOCK, :] = yrm_ref[...]
    yp[BLOCK:, :] = yr_ref[...]
    xbuf[0:SUBLANES, :] = jnp.zeros((SUBLANES, LRU_WIDTH), F32)
    hcar[...] = jnp.zeros_like(hcar)
    lam = lam_ref[...]
    softplus_neg = jnp.maximum(-lam, 0.0) + jnp.log(1.0 + jnp.exp(-jnp.abs(lam)))

    def strided(j):
        return pl.ds(j, SUBLANES, stride=LRU_CHUNK)

    def segment(k, carry):
        rows = pl.ds(pl.multiple_of(k * LRU_SEG, 2 * SUBLANES), LRU_SEG)
        xbuf[SUBLANES:, :] = xp[rows, :].astype(F32)
        xc = jnp.broadcast_to(cb_ref[...], (LRU_SEG, LRU_WIDTH))
        for tap in range(CONV_W):
            off = SUBLANES - (CONV_W - 1) + tap
            xc = xc + cw_ref[tap:tap + 1, :] * xbuf[off:off + LRU_SEG, :]
        xbuf[0:SUBLANES, :] = xbuf[LRU_SEG:LRU_SEG + SUBLANES, :]

        xcb = xc.astype(BF16)
        r = jax.nn.sigmoid(jnp.dot(xcb, wa_ref[...], preferred_element_type=F32) + ba_ref[...])
        i = jax.nn.sigmoid(jnp.dot(xcb, wx_ref[...], preferred_element_type=F32) + bx_ref[...])
        a = jnp.exp(-LRU_C * r * softplus_neg)
        u = jnp.sqrt(1.0 - a * a) * (i * xc)
        row = k * LRU_SEG + lax.broadcasted_iota(jnp.int32, (LRU_SEG, LRU_WIDTH), 0)
        u = jnp.where(row >= PAD_FRONT, u, 0.0)
        for c in range(LRU_SLABS):
            a_scr[c] = a[:, c * LANES:(c + 1) * LANES]
            u_scr[c] = u[:, c * LANES:(c + 1) * LANES]

        for c in range(LRU_SLABS):
            h = jnp.zeros((SUBLANES, LANES), F32)
            p = jnp.ones((SUBLANES, LANES), F32)
            for j in range(LRU_CHUNK):
                aj = a_scr[c, strided(j), :]
                h = aj * h + u_scr[c, strided(j), :]
                p = aj * p
                u_scr[c, strided(j), :] = h
                a_scr[c, strided(j), :] = p
            entry = [hcar[:, c * LANES:(c + 1) * LANES]]
            for s in range(SUBLANES):
                entry.append(h[s:s + 1, :] + p[s:s + 1, :] * entry[s])
            hcar[:, c * LANES:(c + 1) * LANES] = entry[SUBLANES]
            entry_rows = jnp.concatenate(entry[:SUBLANES], axis=0)
            for j in range(LRU_CHUNK):
                u_scr[c, strided(j), :] = (u_scr[c, strided(j), :]
                                           + a_scr[c, strided(j), :] * entry_rows)

        h_all = jnp.concatenate([u_scr[c] for c in range(LRU_SLABS)], axis=1)
        out = h_all * _gelu_tanh(yp[rows, :].astype(F32))
        ms = jnp.mean(out * out, axis=1, keepdims=True)
        op[rows, :] = (out * lax.rsqrt(ms + EPS) * g_ref[...]).astype(op.dtype)
        return carry

    lax.fori_loop(0, n_seg, segment, 0)
    o_ref[...] = op[BLOCK:, :]


def _rglru(xr, yr, xrm, yrm, cw, cb, wa, wx, ba, bx, lam, g_lru, bsz, nbx):
    seq = nbx * BLOCK
    assert (seq + BLOCK) % LRU_SEG == 0
    main = pl.BlockSpec((None, seq, LRU_WIDTH), lambda b: (b, 0, 0))
    const2 = lambda b: (0, 0)
    row_spec = pl.BlockSpec((1, LRU_WIDTH), const2)
    padded = pltpu.VMEM((seq + BLOCK, LRU_WIDTH), BF16)
    slabs = pltpu.VMEM((LRU_SLABS, LRU_SEG, LANES), F32)
    return pl.pallas_call(
        _lru_kernel,
        grid=(bsz,),
        in_specs=[main, main,
                  pl.BlockSpec((BLOCK, LRU_WIDTH), const2),
                  pl.BlockSpec((BLOCK, LRU_WIDTH), const2),
                  pl.BlockSpec((CONV_W, LRU_WIDTH), const2),
                  row_spec,
                  pl.BlockSpec((LRU_WIDTH, LRU_WIDTH), const2),
                  pl.BlockSpec((LRU_WIDTH, LRU_WIDTH), const2),
                  row_spec, row_spec, row_spec, row_spec],
        out_specs=main,
        out_shape=jax.ShapeDtypeStruct((bsz, seq, LRU_WIDTH), BF16),
        scratch_shapes=[padded, padded, padded,
                        pltpu.VMEM((LRU_SEG + SUBLANES, LRU_WIDTH), F32),
                        slabs, slabs,
                        pltpu.VMEM((1, LRU_WIDTH), F32)],
        compiler_params=_cparams(1),
        name="rglru",
    )(xr, yr, xrm, yrm, cw, cb, wa, wx, ba, bx, lam, g_lru)


def _pack_rows(v):
    bits = lax.bitcast_convert_type(v.astype(BF16).astype(F32), U32)
    return (bits[:, :PACKED] >> 16) | (bits[:, PACKED:] & jnp.uint32(0xFFFF0000))


def _unpack_rows(w):
    lo = lax.bitcast_convert_type(w << 16, F32)
    hi = lax.bitcast_convert_type(w & jnp.uint32(0xFFFF0000), F32)
    return lo, hi


def _layer_norm(z, g, b):
    mu = jnp.mean(z, axis=1, keepdims=True)
    zc = z - mu
    var = jnp.mean(zc * zc, axis=1, keepdims=True)
    return zc * lax.rsqrt(var + EPS) * g + b


def _out_proj_kernel(a_ref, l_ref, x_ref, wa_ref, wl_ref, g_ref, b_ref, wrt_hi_ref, wrt_lo_ref,
                     brt_ref, h_ref, hp_ref, info_ref):
    for c in range(h_ref.shape[0] // PROJ_ROWS):
        rows = slice(c * PROJ_ROWS, (c + 1) * PROJ_ROWS)
        _out_proj_rows(a_ref[rows, :], l_ref[rows, :], x_ref[rows, :], wa_ref, wl_ref, g_ref,
                       b_ref, wrt_hi_ref, wrt_lo_ref, brt_ref,
                       h_ref.at[rows, :], hp_ref.at[rows, :], info_ref.at[:, rows])


def _out_proj_rows(a, l, x, wa_ref, wl_ref, g_ref, b_ref, wrt_hi_ref, wrt_lo_ref, brt_ref,
                   h_ref, hp_ref, info_ref):
    mix = jnp.dot(a, wa_ref[...], preferred_element_type=F32)
    mix = mix + jnp.dot(l, wl_ref[...], preferred_element_type=F32)
    h = _layer_norm(ALPHA * x + mix, g_ref[...], b_ref[...])
    h_ref[...] = h
    hp_ref[...] = _pack_rows(h)

    h_hi = h.astype(BF16)
    h_lo = (h - h_hi.astype(F32)).astype(BF16)
    nt = (((1,), (1,)), ((), ()))
    lg = (lax.dot_general(wrt_hi_ref[...], h_hi, nt, preferred_element_type=F32)
          + lax.dot_general(wrt_lo_ref[...], h_hi, nt, preferred_element_type=F32)
          + lax.dot_general(wrt_hi_ref[...], h_lo, nt, preferred_element_type=F32)) + brt_ref[...]
    tile_shape = (SUBLANES, h.shape[0])
    sub = lax.broadcasted_iota(I32, tile_shape, 0)
    ninf = -jnp.inf
    t0 = lg[0:SUBLANES]
    gl = jnp.where(sub < N_GROUPS, t0, ninf)
    gmax = jnp.max(gl, axis=0, keepdims=True)
    g_idx = jnp.min(jnp.where(gl == gmax, sub, SUBLANES), axis=0, keepdims=True)
    g_w = 1.0 / jnp.sum(jnp.where(sub < N_GROUPS, jnp.exp(t0 - gmax), 0.0),
                        axis=0, keepdims=True)
    el = lg[SUBLANES:2 * SUBLANES]
    for g in range(1, N_GROUPS):
        el = jnp.where(g_idx == g, lg[(g + 1) * SUBLANES:(g + 2) * SUBLANES], el)
    v1 = jnp.max(el, axis=0, keepdims=True)
    i1 = jnp.min(jnp.where(el == v1, sub, SUBLANES), axis=0, keepdims=True)
    el2 = jnp.where(sub == i1, ninf, el)
    v2 = jnp.max(el2, axis=0, keepdims=True)
    i2 = jnp.min(jnp.where(el2 == v2, sub, SUBLANES), axis=0, keepdims=True)
    t = jnp.exp(v2 - v1)
    w1 = 1.0 / (1.0 + t)
    w2 = t * w1
    e_base = g_idx * EXPERTS_PER_GROUP
    info_ref[...] = jnp.where(sub == 0, (e_base + i1).astype(F32),
                              jnp.where(sub == 1, (e_base + i2).astype(F32),
                                        jnp.where(sub == 2, g_w * w1,
                                                  jnp.where(sub == 3, g_w * w2, 0.0))))


ROUTER_ROWS = (N_GROUPS + 1) * SUBLANES


def _out_proj(attn_n, lru_n, x2d, wo_a, wo_l, ln_g, ln_b, w_rt_hi, w_rt_lo, b_rt, rows):
    n = x2d.shape[0]
    const = lambda i: (0, 0)
    tile = lambda w: pl.BlockSpec((rows, w), lambda i: (i, 0))
    return pl.pallas_call(
        _out_proj_kernel,
        grid=(n // rows,),
        in_specs=[tile(ATTN_WIDTH), tile(LRU_WIDTH), tile(D_MODEL),
                  pl.BlockSpec((ATTN_WIDTH, D_MODEL), const),
                  pl.BlockSpec((LRU_WIDTH, D_MODEL), const),
                  pl.BlockSpec((1, D_MODEL), const),
                  pl.BlockSpec((1, D_MODEL), const),
                  pl.BlockSpec((ROUTER_ROWS, D_MODEL), const),
                  pl.BlockSpec((ROUTER_ROWS, D_MODEL), const),
                  pl.BlockSpec((ROUTER_ROWS, 1), const)],
        out_specs=[tile(D_MODEL), tile(PACKED),
                   pl.BlockSpec((SUBLANES, rows), lambda i: (0, i))],
        out_shape=[jax.ShapeDtypeStruct((n, D_MODEL), F32),
                   jax.ShapeDtypeStruct((n, PACKED), U32),
                   jax.ShapeDtypeStruct((SUBLANES, n), F32)],
        compiler_params=_cparams(1),
        name="out_proj",
    )(attn_n, lru_n, x2d, wo_a, wo_l, ln_g, ln_b, w_rt_hi, w_rt_lo, b_rt)


def _route_kernel(info_ref, tri_ref, dest_ref, cnt_ref, counts, carry, pstart):
    p = pl.program_id(0)
    t = pl.program_id(1)
    info = info_ref[...]
    shape = (N_EXPERTS, info.shape[1])
    expert = lax.broadcasted_iota(I32, shape, 0)
    oh1 = (expert == info[0:1, :].astype(I32)).astype(F32)
    oh2 = (expert == info[1:2, :].astype(I32)).astype(F32)
    both = oh1 + oh2
    tile_counts = jnp.sum(both, axis=1, keepdims=True)

    @pl.when((p == 0) & (t == 0))
    def _():
        counts[...] = jnp.zeros_like(counts)

    @pl.when(p == 0)
    def _():
        counts[...] += tile_counts

    @pl.when((p == 1) & (t == 0))
    def _():
        c = jnp.broadcast_to(counts[...], (N_EXPERTS, LANES)).astype(I32)
        padded = ((c + (MOE_BLOCK - 1)) // MOE_BLOCK) * MOE_BLOCK
        e = lax.broadcasted_iota(I32, (N_EXPERTS, LANES), 0)
        scan = padded
        for d in (1, 2, 4, 8, 16):
            scan = scan + jnp.where(e >= d, pltpu.roll(scan, d, axis=0), 0)
        pstart[...] = (scan - padded)[:, 0:1].astype(F32)
        carry[...] = jnp.zeros_like(carry)
        cnt_ref[...] = c

    @pl.when(p == 1)
    def _():
        before = jnp.dot(both.astype(BF16), tri_ref[...], preferred_element_type=F32)
        row_of = before + (carry[...] + pstart[...])
        r1 = jnp.sum(oh1 * row_of, axis=0, keepdims=True)
        r2 = jnp.sum(oh2 * row_of, axis=0, keepdims=True)
        sub = lax.broadcasted_iota(I32, dest_ref.shape, 0)
        dest_ref[...] = jnp.where(sub == 0, r1, jnp.where(sub == 1, r2, 0.0)).astype(I32)
        carry[...] += tile_counts


def _route(info_t, cols):
    n = info_t.shape[1]
    tri = jnp.asarray(np.triu(np.ones((cols, cols), np.float32), 1), BF16)
    return pl.pallas_call(
        _route_kernel,
        grid=(2, n // cols),
        in_specs=[pl.BlockSpec((SUBLANES, cols), lambda p, t: (0, t)),
                  pl.BlockSpec((cols, cols), lambda p, t: (0, 0))],
        out_specs=[pl.BlockSpec((SUBLANES, cols), lambda p, t: (0, t * p)),
                   pl.BlockSpec((N_EXPERTS, LANES), lambda p, t: (0, 0))],
        out_shape=[jax.ShapeDtypeStruct((SUBLANES, n), I32),
                   jax.ShapeDtypeStruct((N_EXPERTS, LANES), I32)],
        scratch_shapes=[pltpu.VMEM((N_EXPERTS, 1), F32), pltpu.VMEM((N_EXPERTS, 1), F32),
                        pltpu.VMEM((N_EXPERTS, 1), F32)],
        compiler_params=_cparams(2),
        name="route",
    )(info_t, tri)


def _sc_mesh():
    return plsc.VectorSubcoreMesh(core_axis_name="core", subcore_axis_name="subcore")


def _sc_worker_id():
    return lax.axis_index("subcore") * SC_CORES + lax.axis_index("core")


def _sc_scatter_rows(rows, d0, d1, cap):
    n, width = rows.shape
    per_worker = n // SC_WORKERS
    n_win = per_worker // SC_WINDOW

    def body(x_hbm, d0_hbm, d1_hbm, o_hbm, i0_v, i1_v, rows_v, rsem, sem0, sem1):
        wid = _sc_worker_id()
        pltpu.sync_copy(d0_hbm.at[wid], i0_v)
        pltpu.sync_copy(d1_hbm.at[wid], i1_v)

        def read(j):
            src = x_hbm.at[pl.ds(wid * per_worker + j * SC_WINDOW, SC_WINDOW)]
            return pltpu.make_async_copy(src, rows_v.at[j % 2], rsem.at[j % 2])

        def scatters(j):
            return (pltpu.make_async_copy(rows_v.at[j % 2], o_hbm.at[i0_v.at[j]], sem0.at[j % 2]),
                    pltpu.make_async_copy(rows_v.at[j % 2], o_hbm.at[i1_v.at[j]], sem1.at[j % 2]))

        read(0).start()
        for j in range(n_win):
            if j + 1 < n_win:
                if j >= 1:
                    for cp in scatters(j - 1):
                        cp.wait()
                read(j + 1).start()
            read(j).wait()
            for cp in scatters(j):
                cp.start()
        for j in range(max(n_win - 2, 0), n_win):
            for cp in scatters(j):
                cp.wait()

    return pl.kernel(
        body,
        out_type=jax.ShapeDtypeStruct((cap, width), rows.dtype),
        mesh=_sc_mesh(),
        scratch_types=[pltpu.VMEM((n_win, SC_WINDOW), I32), pltpu.VMEM((n_win, SC_WINDOW), I32),
                       pltpu.VMEM((2, SC_WINDOW, width), rows.dtype),
                       pltpu.SemaphoreType.DMA((2,)), pltpu.SemaphoreType.DMA((2,)),
                       pltpu.SemaphoreType.DMA((2,))],
        name="dispatch",
    )(rows, d0, d1)


def _sc_gather_rows(table, idx):
    width = table.shape[1]
    n_win = idx.shape[1]
    per_worker = n_win * SC_WINDOW

    def body(y_hbm, i_hbm, o_hbm, i_v, rows_v, gsem, wsem):
        wid = _sc_worker_id()
        pltpu.sync_copy(i_hbm.at[wid], i_v)

        def gather(j):
            return pltpu.make_async_copy(y_hbm.at[i_v.at[j]], rows_v.at[j % 2], gsem.at[j % 2])

        def write(j):
            dst = o_hbm.at[pl.ds(wid * per_worker + j * SC_WINDOW, SC_WINDOW)]
            return pltpu.make_async_copy(rows_v.at[j % 2], dst, wsem.at[j % 2])

        gather(0).start()
        for j in range(n_win):
            if j + 1 < n_win:
                if j >= 1:
                    write(j - 1).wait()
                gather(j + 1).start()
            gather(j).wait()
            write(j).start()
        for j in range(max(n_win - 2, 0), n_win):
            write(j).wait()

    return pl.kernel(
        body,
        out_type=jax.ShapeDtypeStruct((SC_WORKERS * per_worker, width), table.dtype),
        mesh=_sc_mesh(),
        scratch_types=[pltpu.VMEM((n_win, SC_WINDOW), I32),
                       pltpu.VMEM((2, SC_WINDOW, width), table.dtype),
                       pltpu.SemaphoreType.DMA((2,)), pltpu.SemaphoreType.DMA((2,))],
        name="collect",
    )(table, idx)


def _expert_kernel(bstart_ref, nblk_ref, nused_ref, xs_hbm, wg_ref, wu_ref, wd_ref, yb_hbm,
                   xbuf, ybuf, zbuf, xsem, ysem, zsem, wg_b, wu_b, wd_b):
    e = pl.program_id(0)
    nused = nused_ref[0]
    n_blocks = yb_hbm.shape[0] // MOE_BLOCK

    def rows(b):
        return pl.ds(pl.multiple_of(b * MOE_BLOCK, MOE_BLOCK), MOE_BLOCK)

    def x_copy(b):
        slot = b % X_RING
        return pltpu.make_async_copy(xs_hbm.at[rows(b)], xbuf.at[slot], xsem.at[slot])

    def y_copy(b, slot):
        return pltpu.make_async_copy(ybuf.at[slot], yb_hbm.at[rows(b)], ysem.at[slot])

    @pl.when(e == 0)
    def _():
        for b in range(X_RING - 1):
            @pl.when(b < nused)
            def _():
                x_copy(b).start()

    wg_b[...] = wg_ref[...].astype(BF16)
    wu_b[...] = wu_ref[...].astype(BF16)
    wd_b[...] = wd_ref[...].astype(BF16)

    def block(b, carry):
        slot = b % 2
        x_copy(b).wait()

        @pl.when(b + (X_RING - 1) < nused)
        def _():
            x_copy(b + (X_RING - 1)).start()

        lo, hi = _unpack_rows(xbuf[b % X_RING])
        lo = lo.astype(BF16)
        hi = hi.astype(BF16)
        g = (jnp.dot(lo, wg_b[0:PACKED, :], preferred_element_type=F32)
             + jnp.dot(hi, wg_b[PACKED:, :], preferred_element_type=F32))
        u = (jnp.dot(lo, wu_b[0:PACKED, :], preferred_element_type=F32)
             + jnp.dot(hi, wu_b[PACKED:, :], preferred_element_type=F32))
        mid = (g * jax.nn.sigmoid(g) * u).astype(BF16)
        y = _pack_rows(jnp.dot(mid, wd_b[...], preferred_element_type=F32))

        @pl.when(b >= 2)
        def _():
            y_copy(b - 2, slot).wait()

        ybuf[slot] = y
        y_copy(b, slot).start()
        return carry

    b0 = bstart_ref[e]
    lax.fori_loop(b0, b0 + nblk_ref[e], block, 0)

    @pl.when(e == pl.num_programs(0) - 1)
    def _():
        for back in (2, 1):
            @pl.when(nused >= back)
            def _():
                y_copy(nused - back, (nused - back) % 2).wait()

        zbuf[...] = jnp.zeros_like(zbuf)

        def z_copy(b):
            return pltpu.make_async_copy(zbuf, yb_hbm.at[rows(b)], zsem.at[0])

        def z_start(b, carry):
            z_copy(b).start()
            return carry

        def z_wait(b, carry):
            z_copy(b).wait()
            return carry

        lax.fori_loop(nused, n_blocks, z_start, 0)
        lax.fori_loop(nused, n_blocks, z_wait, 0)


def _experts(bstart, nblk, nused, xs, w_gate, w_up, w_down):
    cap = xs.shape[0]
    w_idx = lambda e, bs, nb, nu: (e, 0, 0)
    grid_spec = pltpu.PrefetchScalarGridSpec(
        num_scalar_prefetch=3,
        grid=(N_EXPERTS,),
        in_specs=[pl.BlockSpec(memory_space=pl.ANY),
                  pl.BlockSpec((None, D_MODEL, D_FF), w_idx),
                  pl.BlockSpec((None, D_MODEL, D_FF), w_idx),
                  pl.BlockSpec((None, D_FF, D_MODEL), w_idx)],
        out_specs=pl.BlockSpec(memory_space=pl.ANY),
        scratch_shapes=[pltpu.VMEM((X_RING, MOE_BLOCK, PACKED), U32),
                        pltpu.VMEM((2, MOE_BLOCK, PACKED), U32),
                        pltpu.VMEM((MOE_BLOCK, PACKED), U32),
                        pltpu.SemaphoreType.DMA((X_RING,)),
                        pltpu.SemaphoreType.DMA((2,)),
                        pltpu.SemaphoreType.DMA((1,)),
                        pltpu.VMEM((D_MODEL, D_FF), BF16),
                        pltpu.VMEM((D_MODEL, D_FF), BF16),
                        pltpu.VMEM((D_FF, D_MODEL), BF16)])
    return pl.pallas_call(
        _expert_kernel,
        grid_spec=grid_spec,
        out_shape=jax.ShapeDtypeStruct((cap, PACKED), U32),
        compiler_params=_cparams(1),
        name="experts",
    )(bstart, nblk, nused, xs, w_gate, w_up, w_down)


def _combine_kernel(y0_ref, y1_ref, h_ref, info_ref, g_ref, b_ref, o_ref):
    info = info_ref[...].T
    g0 = info[:, 2:3]
    g1 = info[:, 3:4]
    lo0, hi0 = _unpack_rows(y0_ref[...])
    lo1, hi1 = _unpack_rows(y1_ref[...])
    y = jnp.concatenate([g0 * lo0 + g1 * lo1, g0 * hi0 + g1 * hi1], axis=1)
    o_ref[...] = _layer_norm(ALPHA * h_ref[...] + y, g_ref[...], b_ref[...])


def _combine(ys, h, info, ln_g, ln_b, rows, part, n_parts):
    n = h.shape[0]
    steps = n // n_parts // rows
    off = part * steps
    const = lambda i: (0, 0)
    return pl.pallas_call(
        _combine_kernel,
        grid=(steps,),
        in_specs=[pl.BlockSpec((rows, PACKED), lambda i: (i, 0)),
                  pl.BlockSpec((rows, PACKED), lambda i: (i + steps, 0)),
                  pl.BlockSpec((rows, D_MODEL), lambda i: (i + off, 0)),
                  pl.BlockSpec((SUBLANES, rows), lambda i: (0, i + off)),
                  pl.BlockSpec((1, D_MODEL), const),
                  pl.BlockSpec((1, D_MODEL), const)],
        out_specs=pl.BlockSpec((rows, D_MODEL), lambda i: (i + off, 0)),
        out_shape=jax.ShapeDtypeStruct((n, D_MODEL), F32),
        input_output_aliases={2: 0},
        compiler_params=_cparams(1),
        name="combine",
    )(ys, ys, h, info, ln_g, ln_b)


def _alibi_bias():
    qi = np.arange(BLOCK)[:, None]
    kj = np.arange(2 * BLOCK)[None, :]
    dist = qi - kj + BLOCK
    band = (dist >= 0) & (dist < BLOCK)
    slopes = np.exp2(-8.0 * np.arange(1, N_Q_HEADS + 1, dtype=np.float32) / N_Q_HEADS)
    bias = np.where(band[None], -slopes[:, None, None] * dist[None].astype(np.float32), NEG)
    first = np.where((kj >= PAD_FRONT)[None], bias, NEG)
    out = np.empty((2, N_KV_HEADS, 2 * BLOCK, 4 * BLOCK), np.float32)
    for v, per_head in enumerate((first, bias)):
        for j in range(N_KV_HEADS):
            out[v, j] = np.block([[per_head[4 * j], per_head[4 * j + 1]],
                                  [per_head[4 * j + 2], per_head[4 * j + 3]]])
    return jnp.asarray(out, F32)


def _block_diag(w):
    nb, c, _ = w.shape
    eye = jnp.eye(nb, dtype=w.dtype)
    return jnp.einsum('ncd,nm->ncmd', w, eye).reshape(nb * c, nb * c)


def kernel(x, meta_tokens, w_in, conv_w, conv_b, lru_wa, lru_ba, lru_wx, lru_bx, lru_lambda,
           attn_sinks, g_attn, g_lru, w_out, ln1_g, ln1_b, w_group, b_group, w_router,
           b_router, w_gate, w_up, w_down, ln2_g, ln2_b):
    bsz, seq, d = x.shape
    nbx = seq // BLOCK
    n_tok = bsz * seq
    x2d = x.reshape(n_tok, d)
    row = lambda v: v.reshape(1, -1).astype(F32)

    w_in_b = w_in[0].astype(BF16)
    meta_blk = jnp.concatenate([jnp.zeros((PAD_FRONT, d), F32), meta_tokens.astype(F32)], axis=0)
    q, kv, xr, yr = _in_proj(x2d, w_in_b, PROJ_ROWS)
    qm, kvm, xrm, yrm = _in_proj(meta_blk, w_in_b, BLOCK)
    shp = lambda a: a.reshape(bsz, seq, a.shape[-1])

    attn_n = _attention(attn_sinks[0].astype(F32), shp(q), shp(kv), kvm, _alibi_bias(),
                        row(g_attn[0]), bsz, nbx)
    lru_n = _rglru(shp(xr), shp(yr), xrm, yrm, conv_w[0].astype(F32), row(conv_b[0]),
                   _block_diag(lru_wa[0]).astype(BF16), _block_diag(lru_wx[0]).astype(BF16),
                   row(lru_ba[0]), row(lru_bx[0]), row(lru_lambda[0]), row(g_lru[0]), bsz, nbx)

    w_out_b = w_out[0].astype(BF16)
    gpad = SUBLANES - N_GROUPS
    w_rt = jnp.concatenate(
        [w_group[0].T, jnp.zeros((gpad, d), F32),
         jnp.transpose(w_router[0], (0, 2, 1)).reshape(N_EXPERTS, d)], axis=0).astype(F32)
    w_rt_hi = w_rt.astype(BF16)
    w_rt_lo = (w_rt - w_rt_hi.astype(F32)).astype(BF16)
    b_rt = jnp.concatenate([b_group[0], jnp.zeros((gpad,), F32),
                            b_router[0].reshape(-1)]).astype(F32).reshape(ROUTER_ROWS, 1)
    h1, hp, info = _out_proj(attn_n.reshape(n_tok, ATTN_WIDTH), lru_n.reshape(n_tok, LRU_WIDTH),
                             x2d, w_out_b[:ATTN_WIDTH], w_out_b[ATTN_WIDTH:], row(ln1_g[0]),
                             row(ln1_b[0]), w_rt_hi, w_rt_lo, b_rt, OUT_PROJ_ROWS)

    dest, cnt = _route(info, ROUTE_ROWS)
    n_slots = n_tok * TOP_K
    n_blocks = n_slots // MOE_BLOCK + N_EXPERTS
    cap = n_blocks * MOE_BLOCK
    nblk = (cnt[:, 0] + MOE_BLOCK - 1) // MOE_BLOCK
    bends = jnp.cumsum(nblk)
    bstart = (bends - nblk).astype(I32)
    nused = bends[-1:].astype(I32)
    windows = lambda v: v.reshape(SC_WORKERS, -1, SC_WINDOW)
    d0 = dest[0]
    d1 = dest[1]

    xs = _sc_scatter_rows(hp, windows(d0), windows(d1), cap)
    yb = _experts(bstart, nblk.astype(I32), nused, xs, w_gate[0], w_up[0], w_down[0])
    out = h1
    part_len = n_tok // COMBINE_PARTS
    for part in range(COMBINE_PARTS):
        tok = slice(part * part_len, (part + 1) * part_len)
        ys = _sc_gather_rows(yb, windows(jnp.concatenate([d0[tok], d1[tok]])))
        out = _combine(ys, out, info, row(ln2_g[0]), row(ln2_b[0]), COMBINE_ROWS,
                       part, COMBINE_PARTS)
    return out.reshape(bsz, seq, d)
```

```python
import jax
import jax.numpy as jnp
import numpy as np
from jax import lax
from jax.experimental import pallas as pl
from jax.experimental.pallas import tpu as pltpu
from jax.experimental.pallas import tpu_sc as plsc

F32 = jnp.float32
BF16 = jnp.bfloat16
U32 = jnp.uint32
I32 = jnp.int32

D_MODEL = 1024
N_META = 16
BLOCK = 128
PAD_FRONT = BLOCK - N_META
HEAD_DIM = 64
ATTN_WIDTH = 512
LRU_WIDTH = 512
N_Q_HEADS = 8
N_KV_HEADS = 2
KV_WIDTH = N_KV_HEADS * HEAD_DIM
LRU_BLOCKS = 8
CONV_W = 4
LRU_C = 8.0
IN_COLS = ATTN_WIDTH + 2 * KV_WIDTH + 2 * LRU_WIDTH
N_GROUPS = 4
EXPERTS_PER_GROUP = 8
N_EXPERTS = N_GROUPS * EXPERTS_PER_GROUP
TOP_K = 2
D_FF = 512
MOE_BLOCK = 256
ALPHA = 2.0 ** 0.25
EPS = 1e-5
NEG = -1e30
LANES = 128
SUBLANES = 8
PACKED = D_MODEL // 2

PROJ_ROWS = 512
OUT_PROJ_ROWS = 1024
ROUTE_ROWS = 512
COMBINE_ROWS = 512
COMBINE_PARTS = 4
X_RING = 4
VMEM_LIMIT = 48 * 1024 * 1024

SC_CORES = 2
SC_SUBCORES = 16
SC_WORKERS = SC_CORES * SC_SUBCORES
SC_WINDOW = 64


def _cparams(n_axes):
    return pltpu.CompilerParams(
        dimension_semantics=("arbitrary",) * n_axes, vmem_limit_bytes=VMEM_LIMIT)


def _in_proj_kernel(x_ref, w_ref, q_ref, kv_ref, xr_ref, yr_ref):
    proj = jnp.dot(x_ref[...].astype(BF16), w_ref[...], preferred_element_type=F32)
    o = 0
    for ref, width in ((q_ref, ATTN_WIDTH), (kv_ref, 2 * KV_WIDTH),
                       (xr_ref, LRU_WIDTH), (yr_ref, LRU_WIDTH)):
        ref[...] = proj[:, o:o + width].astype(ref.dtype)
        o += width


def _in_proj(x2d, w_bf16, rows):
    n = x2d.shape[0]
    widths = (ATTN_WIDTH, 2 * KV_WIDTH, LRU_WIDTH, LRU_WIDTH)
    return pl.pallas_call(
        _in_proj_kernel,
        grid=(n // rows,),
        in_specs=[pl.BlockSpec((rows, D_MODEL), lambda i: (i, 0)),
                  pl.BlockSpec((D_MODEL, IN_COLS), lambda i: (0, 0))],
        out_specs=[pl.BlockSpec((rows, w), lambda i: (i, 0)) for w in widths],
        out_shape=[jax.ShapeDtypeStruct((n, w), BF16) for w in widths],
        compiler_params=_cparams(1),
        name="in_proj",
    )(x2d, w_bf16)


def _attn_kernel(sinks_ref, q_ref, kv_ref, kvm_ref, bias_ref, g_ref, o_ref,
                 klo, khi, vlo, vhi):
    nbx = q_ref.shape[0] // BLOCK
    lo_lanes = lax.broadcasted_iota(I32, (BLOCK, LANES), 1) < HEAD_DIM

    def layout_block(n, blk):
        rows = pl.ds(pl.multiple_of(n * BLOCK, BLOCK), BLOCK)
        for src, dst_lo, dst_hi in ((blk[:, :KV_WIDTH], klo, khi), (blk[:, KV_WIDTH:], vlo, vhi)):
            w = src.astype(F32)
            r = pltpu.roll(w, HEAD_DIM, axis=1)
            dst_lo[0, rows, :] = jnp.where(lo_lanes, w, 0.0).astype(BF16)
            dst_hi[0, rows, :] = jnp.where(lo_lanes, 0.0, r).astype(BF16)
            dst_lo[1, rows, :] = jnp.where(lo_lanes, r, 0.0).astype(BF16)
            dst_hi[1, rows, :] = jnp.where(lo_lanes, 0.0, w).astype(BF16)

    layout_block(0, kvm_ref[...])

    def layout_body(n, carry):
        layout_block(n + 1, kv_ref[pl.ds(pl.multiple_of(n * BLOCK, BLOCK), BLOCK), :])
        return carry

    lax.fori_loop(0, nbx, layout_body, 0)

    ones_lo = jnp.where(lax.broadcasted_iota(I32, (2 * BLOCK, LANES), 1) < HEAD_DIM,
                        1.0, 0.0).astype(BF16)
    ones_hi = (1.0 - ones_lo.astype(F32)).astype(BF16)
    top_rows = lax.broadcasted_iota(I32, (2 * BLOCK, 1), 0) < BLOCK
    lo_half = lax.broadcasted_iota(I32, (2 * BLOCK, LANES), 1) < HEAD_DIM

    def block(i, carry):
        q_rows = pl.ds(pl.multiple_of(i * BLOCK, BLOCK), BLOCK)
        win = pl.ds(pl.multiple_of(i * BLOCK, BLOCK), 2 * BLOCK)
        q = q_ref[q_rows, :] * (HEAD_DIM ** -0.5)
        first = jnp.minimum(i, 1)
        outs = []
        for j in range(N_KV_HEADS):
            q2 = jnp.concatenate([q[:, (2 * j) * LANES:(2 * j + 1) * LANES],
                                  q[:, (2 * j + 1) * LANES:(2 * j + 2) * LANES]], axis=0)
            kc = jnp.concatenate([klo[j, win, :], khi[j, win, :]], axis=0)
            s = lax.dot_general(q2, kc, (((1,), (1,)), ((), ())), preferred_element_type=F32)
            s = s + bias_ref[first, j]
            ps, es = [], []
            for c in range(2):
                sink = jnp.where(top_rows, sinks_ref[4 * j + c], sinks_ref[4 * j + 2 + c])
                sc = s[:, c * 2 * BLOCK:(c + 1) * 2 * BLOCK]
                m = jnp.maximum(jnp.max(sc, axis=1, keepdims=True), sink)
                ps.append(jnp.exp(sc - m).astype(BF16))
                es.append(jnp.exp(sink - m))
            v_lo = jnp.concatenate([vlo[j, win, :], ones_lo], axis=1)
            v_hi = jnp.concatenate([vhi[j, win, :], ones_hi], axis=1)
            r = (jnp.dot(ps[0], v_lo, preferred_element_type=F32)
                 + jnp.dot(ps[1], v_hi, preferred_element_type=F32))
            den = r[:, LANES:] + jnp.where(lo_half, es[0], es[1])
            o2 = r[:, :LANES] * (1.0 / den)
            outs += [o2[:BLOCK], o2[BLOCK:]]
        out = jnp.concatenate(outs, axis=1)
        ms = jnp.mean(out * out, axis=1, keepdims=True)
        o_ref[q_rows, :] = (out * lax.rsqrt(ms + EPS) * g_ref[...]).astype(o_ref.dtype)
        return carry

    lax.fori_loop(0, nbx, block, 0, unroll=2)


def _attention(sinks, q, kv, kvm, bias, g_attn, bsz, nbx):
    seq = nbx * BLOCK
    const2 = lambda b: (0, 0)
    kv_scratch = pltpu.VMEM((N_KV_HEADS, seq + BLOCK, LANES), BF16)
    return pl.pallas_call(
        _attn_kernel,
        grid=(bsz,),
        in_specs=[pl.BlockSpec(memory_space=pltpu.SMEM),
                  pl.BlockSpec((None, seq, ATTN_WIDTH), lambda b: (b, 0, 0)),
                  pl.BlockSpec((None, seq, 2 * KV_WIDTH), lambda b: (b, 0, 0)),
                  pl.BlockSpec((BLOCK, 2 * KV_WIDTH), const2),
                  pl.BlockSpec((2, N_KV_HEADS, 2 * BLOCK, 4 * BLOCK), lambda b: (0, 0, 0, 0)),
                  pl.BlockSpec((1, ATTN_WIDTH), const2)],
        out_specs=pl.BlockSpec((None, seq, ATTN_WIDTH), lambda b: (b, 0, 0)),
        out_shape=jax.ShapeDtypeStruct((bsz, seq, ATTN_WIDTH), BF16),
        scratch_shapes=[kv_scratch, kv_scratch, kv_scratch, kv_scratch],
        compiler_params=_cparams(1),
        name="attention",
    )(sinks, q, kv, kvm, bias, g_attn)


def _sigmoid(v):
    return 0.5 * jnp.tanh(0.5 * v) + 0.5


def _gelu_tanh(y):
    c = np.sqrt(2.0 / np.pi).astype(np.float32)
    return 0.5 * y * (1.0 + jnp.tanh(c * (y + 0.044715 * (y * y * y))))


LRU_CHUNK = 44
LRU_SEG = SUBLANES * LRU_CHUNK
LRU_SLABS = LRU_WIDTH // LANES


def _lru_kernel(xr_ref, yr_ref, xrm_ref, yrm_ref, cw_ref, cb_ref, wa_ref, wx_ref, ba_ref,
                bx_ref, lam_ref, g_ref, o_ref, xp, yp, op, xbuf, a_scr, u_scr, hcar):
    seq = xr_ref.shape[0]
    n_seg = (seq + BLOCK) // LRU_SEG
    xp[0:BLOCK, :] = xrm_ref[...]
    xp[BLOCK:, :] = xr_ref[...]
    yp[0:BLOCK, :] = yrm_ref[...]
    yp[BLOCK:, :] = yr_ref[...]
    xbuf[0:SUBLANES, :] = jnp.zeros((SUBLANES, LRU_WIDTH), F32)
    hcar[...] = jnp.zeros_like(hcar)
    lam = lam_ref[...]
    softplus_neg = jnp.maximum(-lam, 0.0) + jnp.log(1.0 + jnp.exp(-jnp.abs(lam)))

    def strided(j):
        return pl.ds(j, SUBLANES, stride=LRU_CHUNK)

    def segment(k, carry):
        rows = pl.ds(pl.multiple_of(k * LRU_SEG, 2 * SUBLANES), LRU_SEG)
        xbuf[SUBLANES:, :] = xp[rows, :].astype(F32)
        xc = jnp.broadcast_to(cb_ref[...], (LRU_SEG, LRU_WIDTH))
        for tap in range(CONV_W):
            off = SUBLANES - (CONV_W - 1) + tap
            xc = xc + cw_ref[tap:tap + 1, :] * xbuf[off:off + LRU_SEG, :]
        xbuf[0:SUBLANES, :] = xbuf[LRU_SEG:LRU_SEG + SUBLANES, :]

        xcb = xc.astype(BF16)
        r = _sigmoid(jnp.dot(xcb, wa_ref[...], preferred_element_type=F32) + ba_ref[...])
        i = _sigmoid(jnp.dot(xcb, wx_ref[...], preferred_element_type=F32) + bx_ref[...])
        a = jnp.exp(-LRU_C * r * softplus_neg)
        z = 1.0 - a * a
        u = jnp.where(z > 0.0, z * lax.rsqrt(z), 0.0) * (i * xc)
        row = k * LRU_SEG + lax.broadcasted_iota(jnp.int32, (LRU_SEG, LRU_WIDTH), 0)
        u = jnp.where(row >= PAD_FRONT, u, 0.0)
        for c in range(LRU_SLABS):
            a_scr[c] = a[:, c * LANES:(c + 1) * LANES]
            u_scr[c] = u[:, c * LANES:(c + 1) * LANES]

        for c in range(LRU_SLABS):
            h = jnp.zeros((SUBLANES, LANES), F32)
            p = jnp.ones((SUBLANES, LANES), F32)
            for j in range(LRU_CHUNK):
                aj = a_scr[c, strided(j), :]
                h = aj * h + u_scr[c, strided(j), :]
                p = aj * p
                u_scr[c, strided(j), :] = h
                a_scr[c, strided(j), :] = p
            entry = [hcar[:, c * LANES:(c + 1) * LANES]]
            for s in range(SUBLANES):
                entry.append(h[s:s + 1, :] + p[s:s + 1, :] * entry[s])
            hcar[:, c * LANES:(c + 1) * LANES] = entry[SUBLANES]
            entry_rows = jnp.concatenate(entry[:SUBLANES], axis=0)
            for j in range(LRU_CHUNK):
                u_scr[c, strided(j), :] = (u_scr[c, strided(j), :]
                                           + a_scr[c, strided(j), :] * entry_rows)

        h_all = jnp.concatenate([u_scr[c] for c in range(LRU_SLABS)], axis=1)
        out = h_all * _gelu_tanh(yp[rows, :].astype(F32))
        ms = jnp.mean(out * out, axis=1, keepdims=True)
        op[rows, :] = (out * lax.rsqrt(ms + EPS) * g_ref[...]).astype(op.dtype)
        return carry

    lax.fori_loop(0, n_seg, segment, 0)
    o_ref[...] = op[BLOCK:, :]


def _rglru(xr, yr, xrm, yrm, cw, cb, wa, wx, ba, bx, lam, g_lru, bsz, nbx):
    seq = nbx * BLOCK
    assert (seq + BLOCK) % LRU_SEG == 0
    main = pl.BlockSpec((None, seq, LRU_WIDTH), lambda b: (b, 0, 0))
    const2 = lambda b: (0, 0)
    row_spec = pl.BlockSpec((1, LRU_WIDTH), const2)
    padded = pltpu.VMEM((seq + BLOCK, LRU_WIDTH), BF16)
    slabs = pltpu.VMEM((LRU_SLABS, LRU_SEG, LANES), F32)
    return pl.pallas_call(
        _lru_kernel,
        grid=(bsz,),
        in_specs=[main, main,
                  pl.BlockSpec((BLOCK, LRU_WIDTH), const2),
                  pl.BlockSpec((BLOCK, LRU_WIDTH), const2),
                  pl.BlockSpec((CONV_W, LRU_WIDTH), const2),
                  row_spec,
                  pl.BlockSpec((LRU_WIDTH, LRU_WIDTH), const2),
                  pl.BlockSpec((LRU_WIDTH, LRU_WIDTH), const2),
                  row_spec, row_spec, row_spec, row_spec],
        out_specs=main,
        out_shape=jax.ShapeDtypeStruct((bsz, seq, LRU_WIDTH), BF16),
        scratch_shapes=[padded, padded, padded,
                        pltpu.VMEM((LRU_SEG + SUBLANES, LRU_WIDTH), F32),
                        slabs, slabs,
                        pltpu.VMEM((1, LRU_WIDTH), F32)],
        compiler_params=_cparams(1),
        name="rglru",
    )(xr, yr, xrm, yrm, cw, cb, wa, wx, ba, bx, lam, g_lru)


def _pack_rows(v):
    bits = lax.bitcast_convert_type(v.astype(BF16).astype(F32), U32)
    return (bits[:, :PACKED] >> 16) | (bits[:, PACKED:] & jnp.uint32(0xFFFF0000))


def _unpack_rows(w):
    lo = lax.bitcast_convert_type(w << 16, F32)
    hi = lax.bitcast_convert_type(w & jnp.uint32(0xFFFF0000), F32)
    return lo, hi


def _layer_norm(z, g, b):
    mu = jnp.mean(z, axis=1, keepdims=True)
    zc = z - mu
    var = jnp.mean(zc * zc, axis=1, keepdims=True)
    return zc * lax.rsqrt(var + EPS) * g + b


def _out_proj_kernel(a_ref, l_ref, x_ref, wa_ref, wl_ref, g_ref, b_ref, wrt_hi_ref, wrt_lo_ref,
                     brt_ref, h_ref, hp_ref, info_ref):
    for c in range(h_ref.shape[0] // PROJ_ROWS):
        rows = slice(c * PROJ_ROWS, (c + 1) * PROJ_ROWS)
        _out_proj_rows(a_ref[rows, :], l_ref[rows, :], x_ref[rows, :], wa_ref, wl_ref, g_ref,
                       b_ref, wrt_hi_ref, wrt_lo_ref, brt_ref,
                       h_ref.at[rows, :], hp_ref.at[rows, :], info_ref.at[:, rows])


def _out_proj_rows(a, l, x, wa_ref, wl_ref, g_ref, b_ref, wrt_hi_ref, wrt_lo_ref, brt_ref,
                   h_ref, hp_ref, info_ref):
    mix = jnp.dot(a, wa_ref[...], preferred_element_type=F32)
    mix = mix + jnp.dot(l, wl_ref[...], preferred_element_type=F32)
    h = _layer_norm(ALPHA * x + mix, g_ref[...], b_ref[...])
    h_ref[...] = h
    hp_ref[...] = _pack_rows(h)

    h_hi = h.astype(BF16)
    h_lo = (h - h_hi.astype(F32)).astype(BF16)
    nt = (((1,), (1,)), ((), ()))
    lg = (lax.dot_general(wrt_hi_ref[...], h_hi, nt, preferred_element_type=F32)
          + lax.dot_general(wrt_lo_ref[...], h_hi, nt, preferred_element_type=F32)
          + lax.dot_general(wrt_hi_ref[...], h_lo, nt, preferred_element_type=F32)) + brt_ref[...]
    tile_shape = (SUBLANES, h.shape[0])
    sub = lax.broadcasted_iota(I32, tile_shape, 0)
    ninf = -jnp.inf
    t0 = lg[0:SUBLANES]
    gl = jnp.where(sub < N_GROUPS, t0, ninf)
    gmax = jnp.max(gl, axis=0, keepdims=True)
    g_idx = jnp.min(jnp.where(gl == gmax, sub, SUBLANES), axis=0, keepdims=True)
    g_w = 1.0 / jnp.sum(jnp.where(sub < N_GROUPS, jnp.exp(t0 - gmax), 0.0),
                        axis=0, keepdims=True)
    el = lg[SUBLANES:2 * SUBLANES]
    for g in range(1, N_GROUPS):
        el = jnp.where(g_idx == g, lg[(g + 1) * SUBLANES:(g + 2) * SUBLANES], el)
    v1 = jnp.max(el, axis=0, keepdims=True)
    i1 = jnp.min(jnp.where(el == v1, sub, SUBLANES), axis=0, keepdims=True)
    el2 = jnp.where(sub == i1, ninf, el)
    v2 = jnp.max(el2, axis=0, keepdims=True)
    i2 = jnp.min(jnp.where(el2 == v2, sub, SUBLANES), axis=0, keepdims=True)
    t = jnp.exp(v2 - v1)
    w1 = 1.0 / (1.0 + t)
    w2 = t * w1
    e_base = g_idx * EXPERTS_PER_GROUP
    info_ref[...] = jnp.where(sub == 0, (e_base + i1).astype(F32),
                              jnp.where(sub == 1, (e_base + i2).astype(F32),
                                        jnp.where(sub == 2, g_w * w1,
                                                  jnp.where(sub == 3, g_w * w2, 0.0))))


ROUTER_ROWS = (N_GROUPS + 1) * SUBLANES


def _out_proj(attn_n, lru_n, x2d, wo_a, wo_l, ln_g, ln_b, w_rt_hi, w_rt_lo, b_rt, rows):
    n = x2d.shape[0]
    const = lambda i: (0, 0)
    tile = lambda w: pl.BlockSpec((rows, w), lambda i: (i, 0))
    return pl.pallas_call(
        _out_proj_kernel,
        grid=(n // rows,),
        in_specs=[tile(ATTN_WIDTH), tile(LRU_WIDTH), tile(D_MODEL),
                  pl.BlockSpec((ATTN_WIDTH, D_MODEL), const),
                  pl.BlockSpec((LRU_WIDTH, D_MODEL), const),
                  pl.BlockSpec((1, D_MODEL), const),
                  pl.BlockSpec((1, D_MODEL), const),
                  pl.BlockSpec((ROUTER_ROWS, D_MODEL), const),
                  pl.BlockSpec((ROUTER_ROWS, D_MODEL), const),
                  pl.BlockSpec((ROUTER_ROWS, 1), const)],
        out_specs=[tile(D_MODEL), tile(PACKED),
                   pl.BlockSpec((SUBLANES, rows), lambda i: (0, i))],
        out_shape=[jax.ShapeDtypeStruct((n, D_MODEL), F32),
                   jax.ShapeDtypeStruct((n, PACKED), U32),
                   jax.ShapeDtypeStruct((SUBLANES, n), F32)],
        compiler_params=_cparams(1),
        name="out_proj",
    )(attn_n, lru_n, x2d, wo_a, wo_l, ln_g, ln_b, w_rt_hi, w_rt_lo, b_rt)


def _route_kernel(info_ref, tri_ref, dest_ref, cnt_ref, counts, carry, pstart):
    p = pl.program_id(0)
    t = pl.program_id(1)
    info = info_ref[...]
    shape = (N_EXPERTS, info.shape[1])
    expert = lax.broadcasted_iota(I32, shape, 0)
    oh1 = (expert == info[0:1, :].astype(I32)).astype(F32)
    oh2 = (expert == info[1:2, :].astype(I32)).astype(F32)
    both = oh1 + oh2
    tile_counts = jnp.sum(both, axis=1, keepdims=True)

    @pl.when((p == 0) & (t == 0))
    def _():
        counts[...] = jnp.zeros_like(counts)

    @pl.when(p == 0)
    def _():
        counts[...] += tile_counts

    @pl.when((p == 1) & (t == 0))
    def _():
        c = jnp.broadcast_to(counts[...], (N_EXPERTS, LANES)).astype(I32)
        padded = ((c + (MOE_BLOCK - 1)) // MOE_BLOCK) * MOE_BLOCK
        e = lax.broadcasted_iota(I32, (N_EXPERTS, LANES), 0)
        scan = padded
        for d in (1, 2, 4, 8, 16):
            scan = scan + jnp.where(e >= d, pltpu.roll(scan, d, axis=0), 0)
        pstart[...] = (scan - padded)[:, 0:1].astype(F32)
        carry[...] = jnp.zeros_like(carry)
        cnt_ref[...] = c

    @pl.when(p == 1)
    def _():
        before = jnp.dot(both.astype(BF16), tri_ref[...], preferred_element_type=F32)
        row_of = before + (carry[...] + pstart[...])
        r1 = jnp.sum(oh1 * row_of, axis=0, keepdims=True)
        r2 = jnp.sum(oh2 * row_of, axis=0, keepdims=True)
        sub = lax.broadcasted_iota(I32, dest_ref.shape, 0)
        dest_ref[...] = jnp.where(sub == 0, r1, jnp.where(sub == 1, r2, 0.0)).astype(I32)
        carry[...] += tile_counts


def _route(info_t, cols):
    n = info_t.shape[1]
    tri = jnp.asarray(np.triu(np.ones((cols, cols), np.float32), 1), BF16)
    return pl.pallas_call(
        _route_kernel,
        grid=(2, n // cols),
        in_specs=[pl.BlockSpec((SUBLANES, cols), lambda p, t: (0, t)),
                  pl.BlockSpec((cols, cols), lambda p, t: (0, 0))],
        out_specs=[pl.BlockSpec((SUBLANES, cols), lambda p, t: (0, t * p)),
                   pl.BlockSpec((N_EXPERTS, LANES), lambda p, t: (0, 0))],
        out_shape=[jax.ShapeDtypeStruct((SUBLANES, n), I32),
                   jax.ShapeDtypeStruct((N_EXPERTS, LANES), I32)],
        scratch_shapes=[pltpu.VMEM((N_EXPERTS, 1), F32), pltpu.VMEM((N_EXPERTS, 1), F32),
                        pltpu.VMEM((N_EXPERTS, 1), F32)],
        compiler_params=_cparams(2),
        name="route",
    )(info_t, tri)


def _sc_mesh():
    return plsc.VectorSubcoreMesh(core_axis_name="core", subcore_axis_name="subcore")


def _sc_worker_id():
    return lax.axis_index("subcore") * SC_CORES + lax.axis_index("core")


def _sc_scatter_rows(rows, d0, d1, cap):
    n, width = rows.shape
    per_worker = n // SC_WORKERS
    n_win = per_worker // SC_WINDOW

    def body(x_hbm, d0_hbm, d1_hbm, o_hbm, i0_v, i1_v, rows_v, rsem, sem0, sem1):
        wid = _sc_worker_id()
        pltpu.sync_copy(d0_hbm.at[wid], i0_v)
        pltpu.sync_copy(d1_hbm.at[wid], i1_v)

        def read(j):
            src = x_hbm.at[pl.ds(wid * per_worker + j * SC_WINDOW, SC_WINDOW)]
            return pltpu.make_async_copy(src, rows_v.at[j % 2], rsem.at[j % 2])

        def scatters(j):
            return (pltpu.make_async_copy(rows_v.at[j % 2], o_hbm.at[i0_v.at[j]], sem0.at[j % 2]),
                    pltpu.make_async_copy(rows_v.at[j % 2], o_hbm.at[i1_v.at[j]], sem1.at[j % 2]))

        read(0).start()
        for j in range(n_win):
            if j + 1 < n_win:
                if j >= 1:
                    for cp in scatters(j - 1):
                        cp.wait()
                read(j + 1).start()
            read(j).wait()
            for cp in scatters(j):
                cp.start()
        for j in range(max(n_win - 2, 0), n_win):
            for cp in scatters(j):
                cp.wait()

    return pl.kernel(
        body,
        out_type=jax.ShapeDtypeStruct((cap, width), rows.dtype),
        mesh=_sc_mesh(),
        scratch_types=[pltpu.VMEM((n_win, SC_WINDOW), I32), pltpu.VMEM((n_win, SC_WINDOW), I32),
                       pltpu.VMEM((2, SC_WINDOW, width), rows.dtype),
                       pltpu.SemaphoreType.DMA((2,)), pltpu.SemaphoreType.DMA((2,)),
                       pltpu.SemaphoreType.DMA((2,))],
        name="dispatch",
    )(rows, d0, d1)


def _sc_gather_rows(table, idx):
    width = table.shape[1]
    n_win = idx.shape[1]
    per_worker = n_win * SC_WINDOW

    def body(y_hbm, i_hbm, o_hbm, i_v, rows_v, gsem, wsem):
        wid = _sc_worker_id()
        pltpu.sync_copy(i_hbm.at[wid], i_v)

        def gather(j):
            return pltpu.make_async_copy(y_hbm.at[i_v.at[j]], rows_v.at[j % 2], gsem.at[j % 2])

        def write(j):
            dst = o_hbm.at[pl.ds(wid * per_worker + j * SC_WINDOW, SC_WINDOW)]
            return pltpu.make_async_copy(rows_v.at[j % 2], dst, wsem.at[j % 2])

        gather(0).start()
        for j in range(n_win):
            if j + 1 < n_win:
                if j >= 1:
                    write(j - 1).wait()
                gather(j + 1).start()
            gather(j).wait()
            write(j).start()
        for j in range(max(n_win - 2, 0), n_win):
            write(j).wait()

    return pl.kernel(
        body,
        out_type=jax.ShapeDtypeStruct((SC_WORKERS * per_worker, width), table.dtype),
        mesh=_sc_mesh(),
        scratch_types=[pltpu.VMEM((n_win, SC_WINDOW), I32),
                       pltpu.VMEM((2, SC_WINDOW, width), table.dtype),
                       pltpu.SemaphoreType.DMA((2,)), pltpu.SemaphoreType.DMA((2,))],
        name="collect",
    )(table, idx)


def _expert_kernel(bstart_ref, nblk_ref, nused_ref, xs_hbm, wg_ref, wu_ref, wd_ref, yb_hbm,
                   xbuf, ybuf, zbuf, xsem, ysem, zsem, wg_b, wu_b, wd_b):
    e = pl.program_id(0)
    nused = nused_ref[0]
    n_blocks = yb_hbm.shape[0] // MOE_BLOCK

    def rows(b):
        return pl.ds(pl.multiple_of(b * MOE_BLOCK, MOE_BLOCK), MOE_BLOCK)

    def x_copy(b):
        slot = b % X_RING
        return pltpu.make_async_copy(xs_hbm.at[rows(b)], xbuf.at[slot], xsem.at[slot])

    def y_copy(b, slot):
        return pltpu.make_async_copy(ybuf.at[slot], yb_hbm.at[rows(b)], ysem.at[slot])

    @pl.when(e == 0)
    def _():
        for b in range(X_RING - 1):
            @pl.when(b < nused)
            def _():
                x_copy(b).start()

    wg_b[...] = wg_ref[...].astype(BF16)
    wu_b[...] = wu_ref[...].astype(BF16)
    wd_b[...] = wd_ref[...].astype(BF16)

    def block(b, carry):
        slot = b % 2
        x_copy(b).wait()

        @pl.when(b + (X_RING - 1) < nused)
        def _():
            x_copy(b + (X_RING - 1)).start()

        lo, hi = _unpack_rows(xbuf[b % X_RING])
        lo = lo.astype(BF16)
        hi = hi.astype(BF16)
        g = (jnp.dot(lo, wg_b[0:PACKED, :], preferred_element_type=F32)
             + jnp.dot(hi, wg_b[PACKED:, :], preferred_element_type=F32))
        u = (jnp.dot(lo, wu_b[0:PACKED, :], preferred_element_type=F32)
             + jnp.dot(hi, wu_b[PACKED:, :], preferred_element_type=F32))
        mid = (g * jax.nn.sigmoid(g) * u).astype(BF16)
        y = _pack_rows(jnp.dot(mid, wd_b[...], preferred_element_type=F32))

        @pl.when(b >= 2)
        def _():
            y_copy(b - 2, slot).wait()

        ybuf[slot] = y
        y_copy(b, slot).start()
        return carry

    b0 = bstart_ref[e]
    lax.fori_loop(b0, b0 + nblk_ref[e], block, 0)

    @pl.when(e == pl.num_programs(0) - 1)
    def _():
        for back in (2, 1):
            @pl.when(nused >= back)
            def _():
                y_copy(nused - back, (nused - back) % 2).wait()

        zbuf[...] = jnp.zeros_like(zbuf)

        def z_copy(b):
            return pltpu.make_async_copy(zbuf, yb_hbm.at[rows(b)], zsem.at[0])

        def z_start(b, carry):
            z_copy(b).start()
            return carry

        def z_wait(b, carry):
            z_copy(b).wait()
            return carry

        lax.fori_loop(nused, n_blocks, z_start, 0)
        lax.fori_loop(nused, n_blocks, z_wait, 0)


def _experts(bstart, nblk, nused, xs, w_gate, w_up, w_down):
    cap = xs.shape[0]
    w_idx = lambda e, bs, nb, nu: (e, 0, 0)
    grid_spec = pltpu.PrefetchScalarGridSpec(
        num_scalar_prefetch=3,
        grid=(N_EXPERTS,),
        in_specs=[pl.BlockSpec(memory_space=pl.ANY),
                  pl.BlockSpec((None, D_MODEL, D_FF), w_idx),
                  pl.BlockSpec((None, D_MODEL, D_FF), w_idx),
                  pl.BlockSpec((None, D_FF, D_MODEL), w_idx)],
        out_specs=pl.BlockSpec(memory_space=pl.ANY),
        scratch_shapes=[pltpu.VMEM((X_RING, MOE_BLOCK, PACKED), U32),
                        pltpu.VMEM((2, MOE_BLOCK, PACKED), U32),
                        pltpu.VMEM((MOE_BLOCK, PACKED), U32),
                        pltpu.SemaphoreType.DMA((X_RING,)),
                        pltpu.SemaphoreType.DMA((2,)),
                        pltpu.SemaphoreType.DMA((1,)),
                        pltpu.VMEM((D_MODEL, D_FF), BF16),
                        pltpu.VMEM((D_MODEL, D_FF), BF16),
                        pltpu.VMEM((D_FF, D_MODEL), BF16)])
    return pl.pallas_call(
        _expert_kernel,
        grid_spec=grid_spec,
        out_shape=jax.ShapeDtypeStruct((cap, PACKED), U32),
        compiler_params=_cparams(1),
        name="experts",
    )(bstart, nblk, nused, xs, w_gate, w_up, w_down)


def _combine_kernel(y0_ref, y1_ref, h_ref, info_ref, g_ref, b_ref, o_ref):
    info = info_ref[...].T
    g0 = info[:, 2:3]
    g1 = info[:, 3:4]
    lo0, hi0 = _unpack_rows(y0_ref[...])
    lo1, hi1 = _unpack_rows(y1_ref[...])
    y = jnp.concatenate([g0 * lo0 + g1 * lo1, g0 * hi0 + g1 * hi1], axis=1)
    o_ref[...] = _layer_norm(ALPHA * h_ref[...] + y, g_ref[...], b_ref[...])


def _combine(ys, h, info, ln_g, ln_b, rows, part, n_parts):
    n = h.shape[0]
    steps = n // n_parts // rows
    off = part * steps
    const = lambda i: (0, 0)
    return pl.pallas_call(
        _combine_kernel,
        grid=(steps,),
        in_specs=[pl.BlockSpec((rows, PACKED), lambda i: (i, 0)),
                  pl.BlockSpec((rows, PACKED), lambda i: (i + steps, 0)),
                  pl.BlockSpec((rows, D_MODEL), lambda i: (i + off, 0)),
                  pl.BlockSpec((SUBLANES, rows), lambda i: (0, i + off)),
                  pl.BlockSpec((1, D_MODEL), const),
                  pl.BlockSpec((1, D_MODEL), const)],
        out_specs=pl.BlockSpec((rows, D_MODEL), lambda i: (i + off, 0)),
        out_shape=jax.ShapeDtypeStruct((n, D_MODEL), F32),
        input_output_aliases={2: 0},
        compiler_params=_cparams(1),
        name="combine",
    )(ys, ys, h, info, ln_g, ln_b)


def _alibi_bias():
    qi = np.arange(BLOCK)[:, None]
    kj = np.arange(2 * BLOCK)[None, :]
    dist = qi - kj + BLOCK
    band = (dist >= 0) & (dist < BLOCK)
    slopes = np.exp2(-8.0 * np.arange(1, N_Q_HEADS + 1, dtype=np.float32) / N_Q_HEADS)
    bias = np.where(band[None], -slopes[:, None, None] * dist[None].astype(np.float32), NEG)
    first = np.where((kj >= PAD_FRONT)[None], bias, NEG)
    out = np.empty((2, N_KV_HEADS, 2 * BLOCK, 4 * BLOCK), np.float32)
    for v, per_head in enumerate((first, bias)):
        for j in range(N_KV_HEADS):
            out[v, j] = np.block([[per_head[4 * j], per_head[4 * j + 1]],
                                  [per_head[4 * j + 2], per_head[4 * j + 3]]])
    return jnp.asarray(out, F32)


def _block_diag(w):
    nb, c, _ = w.shape
    eye = jnp.eye(nb, dtype=w.dtype)
    return jnp.einsum('ncd,nm->ncmd', w, eye).reshape(nb * c, nb * c)


def kernel(x, meta_tokens, w_in, conv_w, conv_b, lru_wa, lru_ba, lru_wx, lru_bx, lru_lambda,
           attn_sinks, g_attn, g_lru, w_out, ln1_g, ln1_b, w_group, b_group, w_router,
           b_router, w_gate, w_up, w_down, ln2_g, ln2_b):
    bsz, seq, d = x.shape
    nbx = seq // BLOCK
    n_tok = bsz * seq
    x2d = x.reshape(n_tok, d)
    row = lambda v: v.reshape(1, -1).astype(F32)

    w_in_b = w_in[0].astype(BF16)
    meta_blk = jnp.concatenate([jnp.zeros((PAD_FRONT, d), F32), meta_tokens.astype(F32)], axis=0)
    q, kv, xr, yr = _in_proj(x2d, w_in_b, PROJ_ROWS)
    qm, kvm, xrm, yrm = _in_proj(meta_blk, w_in_b, BLOCK)
    shp = lambda a: a.reshape(bsz, seq, a.shape[-1])

    attn_n = _attention(attn_sinks[0].astype(F32), shp(q), shp(kv), kvm, _alibi_bias(),
                        row(g_attn[0]), bsz, nbx)
    lru_n = _rglru(shp(xr), shp(yr), xrm, yrm, conv_w[0].astype(F32), row(conv_b[0]),
                   _block_diag(lru_wa[0]).astype(BF16), _block_diag(lru_wx[0]).astype(BF16),
                   row(lru_ba[0]), row(lru_bx[0]), row(lru_lambda[0]), row(g_lru[0]), bsz, nbx)

    w_out_b = w_out[0].astype(BF16)
    gpad = SUBLANES - N_GROUPS
    w_rt = jnp.concatenate(
        [w_group[0].T, jnp.zeros((gpad, d), F32),
         jnp.transpose(w_router[0], (0, 2, 1)).reshape(N_EXPERTS, d)], axis=0).astype(F32)
    w_rt_hi = w_rt.astype(BF16)
    w_rt_lo = (w_rt - w_rt_hi.astype(F32)).astype(BF16)
    b_rt = jnp.concatenate([b_group[0], jnp.zeros((gpad,), F32),
                            b_router[0].reshape(-1)]).astype(F32).reshape(ROUTER_ROWS, 1)
    h1, hp, info = _out_proj(attn_n.reshape(n_tok, ATTN_WIDTH), lru_n.reshape(n_tok, LRU_WIDTH),
                             x2d, w_out_b[:ATTN_WIDTH], w_out_b[ATTN_WIDTH:], row(ln1_g[0]),
                             row(ln1_b[0]), w_rt_hi, w_rt_lo, b_rt, OUT_PROJ_ROWS)

    dest, cnt = _route(info, ROUTE_ROWS)
    n_slots = n_tok * TOP_K
    n_blocks = n_slots // MOE_BLOCK + N_EXPERTS
    cap = n_blocks * MOE_BLOCK
    nblk = (cnt[:, 0] + MOE_BLOCK - 1) // MOE_BLOCK
    bends = jnp.cumsum(nblk)
    bstart = (bends - nblk).astype(I32)
    nused = bends[-1:].astype(I32)
    windows = lambda v: v.reshape(SC_WORKERS, -1, SC_WINDOW)
    d0 = dest[0]
    d1 = dest[1]

    xs = _sc_scatter_rows(hp, windows(d0), windows(d1), cap)
    yb = _experts(bstart, nblk.astype(I32), nused, xs, w_gate[0], w_up[0], w_down[0])
    out = h1
    part_len = n_tok // COMBINE_PARTS
    for part in range(COMBINE_PARTS):
        tok = slice(part * part_len, (part + 1) * part_len)
        ys = _sc_gather_rows(yb, windows(jnp.concatenate([d0[tok], d1[tok]])))
        out = _combine(ys, out, info, row(ln2_g[0]), row(ln2_b[0]), COMBINE_ROWS,
                       part, COMBINE_PARTS)
    return out.reshape(bsz, seq, d)
```

```python
import jax
import jax.numpy as jnp
import numpy as np
from jax import lax
from jax.experimental import pallas as pl
from jax.experimental.pallas import tpu as pltpu
from jax.experimental.pallas import tpu_sc as plsc

F32 = jnp.float32
BF16 = jnp.bfloat16
U32 = jnp.uint32
I32 = jnp.int32

D_MODEL = 1024
N_META = 16
BLOCK = 128
PAD_FRONT = BLOCK - N_META
HEAD_DIM = 64
ATTN_WIDTH = 512
LRU_WIDTH = 512
N_Q_HEADS = 8
N_KV_HEADS = 2
KV_WIDTH = N_KV_HEADS * HEAD_DIM
LRU_BLOCKS = 8
CONV_W = 4
LRU_C = 8.0
IN_COLS = ATTN_WIDTH + 2 * KV_WIDTH + 2 * LRU_WIDTH
N_GROUPS = 4
EXPERTS_PER_GROUP = 8
N_EXPERTS = N_GROUPS * EXPERTS_PER_GROUP
TOP_K = 2
D_FF = 512
MOE_BLOCK = 256
ALPHA = 2.0 ** 0.25
EPS = 1e-5
NEG = -1e30
LOG2E = float(np.log2(np.e))
LANES = 128
SUBLANES = 8
PACKED = D_MODEL // 2

PROJ_ROWS = 512
OUT_PROJ_ROWS = 1024
ROUTE_ROWS = 512
COMBINE_ROWS = 512
COMBINE_PARTS = 4
X_RING = 4
VMEM_LIMIT = 48 * 1024 * 1024

SC_CORES = 2
SC_SUBCORES = 16
SC_WORKERS = SC_CORES * SC_SUBCORES
SC_WINDOW = 64


def _cparams(n_axes):
    return pltpu.CompilerParams(
        dimension_semantics=("arbitrary",) * n_axes, vmem_limit_bytes=VMEM_LIMIT)


def _in_proj_kernel(x_ref, w_ref, q_ref, kv_ref, xr_ref, yr_ref):
    proj = jnp.dot(x_ref[...].astype(BF16), w_ref[...], preferred_element_type=F32)
    o = 0
    for ref, width in ((q_ref, ATTN_WIDTH), (kv_ref, 2 * KV_WIDTH),
                       (xr_ref, LRU_WIDTH), (yr_ref, LRU_WIDTH)):
        ref[...] = proj[:, o:o + width].astype(ref.dtype)
        o += width


def _in_proj(x2d, w_bf16, rows):
    n = x2d.shape[0]
    widths = (ATTN_WIDTH, 2 * KV_WIDTH, LRU_WIDTH, LRU_WIDTH)
    return pl.pallas_call(
        _in_proj_kernel,
        grid=(n // rows,),
        in_specs=[pl.BlockSpec((rows, D_MODEL), lambda i: (i, 0)),
                  pl.BlockSpec((D_MODEL, IN_COLS), lambda i: (0, 0))],
        out_specs=[pl.BlockSpec((rows, w), lambda i: (i, 0)) for w in widths],
        out_shape=[jax.ShapeDtypeStruct((n, w), BF16) for w in widths],
        compiler_params=_cparams(1),
        name="in_proj",
    )(x2d, w_bf16)


def _attn_kernel(sinks_ref, q_ref, kv_ref, kvm_ref, bias_ref, g_ref, o_ref,
                 klo, khi, vlo, vhi):
    nbx = q_ref.shape[0] // BLOCK
    lo_lanes = lax.broadcasted_iota(I32, (BLOCK, LANES), 1) < HEAD_DIM

    def layout_block(n, blk):
        rows = pl.ds(pl.multiple_of(n * BLOCK, BLOCK), BLOCK)
        for src, dst_lo, dst_hi in ((blk[:, :KV_WIDTH], klo, khi), (blk[:, KV_WIDTH:], vlo, vhi)):
            w = src.astype(F32)
            r = pltpu.roll(w, HEAD_DIM, axis=1)
            dst_lo[0, rows, :] = jnp.where(lo_lanes, w, 0.0).astype(BF16)
            dst_hi[0, rows, :] = jnp.where(lo_lanes, 0.0, r).astype(BF16)
            dst_lo[1, rows, :] = jnp.where(lo_lanes, r, 0.0).astype(BF16)
            dst_hi[1, rows, :] = jnp.where(lo_lanes, 0.0, w).astype(BF16)

    layout_block(0, kvm_ref[...])

    def layout_body(n, carry):
        layout_block(n + 1, kv_ref[pl.ds(pl.multiple_of(n * BLOCK, BLOCK), BLOCK), :])
        return carry

    lax.fori_loop(0, nbx, layout_body, 0)

    ones_lo = jnp.where(lax.broadcasted_iota(I32, (2 * BLOCK, LANES), 1) < HEAD_DIM,
                        1.0, 0.0).astype(BF16)
    ones_hi = (1.0 - ones_lo.astype(F32)).astype(BF16)
    top_rows = lax.broadcasted_iota(I32, (2 * BLOCK, 1), 0) < BLOCK
    lo_half = lax.broadcasted_iota(I32, (2 * BLOCK, LANES), 1) < HEAD_DIM

    def block(i, carry):
        q_rows = pl.ds(pl.multiple_of(i * BLOCK, BLOCK), BLOCK)
        win = pl.ds(pl.multiple_of(i * BLOCK, BLOCK), 2 * BLOCK)
        q = q_ref[q_rows, :]
        first = jnp.minimum(i, 1)
        outs = []
        for j in range(N_KV_HEADS):
            q2 = jnp.concatenate([q[:, (2 * j) * LANES:(2 * j + 1) * LANES],
                                  q[:, (2 * j + 1) * LANES:(2 * j + 2) * LANES]], axis=0)
            kc = jnp.concatenate([klo[j, win, :], khi[j, win, :]], axis=0)
            s = lax.dot_general(q2, kc, (((1,), (1,)), ((), ())), preferred_element_type=F32)
            s = s + bias_ref[first, j]
            ps, es = [], []
            for c in range(2):
                sink = jnp.where(top_rows, sinks_ref[4 * j + c], sinks_ref[4 * j + 2 + c])
                sc = s[:, c * 2 * BLOCK:(c + 1) * 2 * BLOCK]
                m = jnp.maximum(jnp.max(sc, axis=1, keepdims=True), sink)
                ps.append(jnp.exp2(sc - m).astype(BF16))
                es.append(jnp.exp2(sink - m))
            v_lo = jnp.concatenate([vlo[j, win, :], ones_lo], axis=1)
            v_hi = jnp.concatenate([vhi[j, win, :], ones_hi], axis=1)
            r = (jnp.dot(ps[0], v_lo, preferred_element_type=F32)
                 + jnp.dot(ps[1], v_hi, preferred_element_type=F32))
            den = r[:, LANES:] + jnp.where(lo_half, es[0], es[1])
            o2 = r[:, :LANES] * (1.0 / den)
            outs += [o2[:BLOCK], o2[BLOCK:]]
        out = jnp.concatenate(outs, axis=1)
        ms = jnp.mean(out * out, axis=1, keepdims=True)
        o_ref[q_rows, :] = (out * lax.rsqrt(ms + EPS) * g_ref[...]).astype(o_ref.dtype)
        return carry

    lax.fori_loop(0, nbx, block, 0, unroll=4)


def _attention(sinks, q, kv, kvm, bias, g_attn, bsz, nbx):
    seq = nbx * BLOCK
    const2 = lambda b: (0, 0)
    kv_scratch = pltpu.VMEM((N_KV_HEADS, seq + BLOCK, LANES), BF16)
    return pl.pallas_call(
        _attn_kernel,
        grid=(bsz,),
        in_specs=[pl.BlockSpec(memory_space=pltpu.SMEM),
                  pl.BlockSpec((None, seq, ATTN_WIDTH), lambda b: (b, 0, 0)),
                  pl.BlockSpec((None, seq, 2 * KV_WIDTH), lambda b: (b, 0, 0)),
                  pl.BlockSpec((BLOCK, 2 * KV_WIDTH), const2),
                  pl.BlockSpec((2, N_KV_HEADS, 2 * BLOCK, 4 * BLOCK), lambda b: (0, 0, 0, 0)),
                  pl.BlockSpec((1, ATTN_WIDTH), const2)],
        out_specs=pl.BlockSpec((None, seq, ATTN_WIDTH), lambda b: (b, 0, 0)),
        out_shape=jax.ShapeDtypeStruct((bsz, seq, ATTN_WIDTH), BF16),
        scratch_shapes=[kv_scratch, kv_scratch, kv_scratch, kv_scratch],
        compiler_params=_cparams(1),
        name="attention",
    )(sinks, q, kv, kvm, bias, g_attn)


def _sigmoid(v):
    return 0.5 * jnp.tanh(0.5 * v) + 0.5


def _gelu_tanh(y):
    c = np.sqrt(2.0 / np.pi).astype(np.float32)
    return 0.5 * y * (1.0 + jnp.tanh(c * (y + 0.044715 * (y * y * y))))


LRU_CHUNK = 44
LRU_SEG = SUBLANES * LRU_CHUNK
LRU_SLABS = LRU_WIDTH // LANES


def _lru_kernel(xr_ref, yr_ref, xrm_ref, yrm_ref, cw_ref, cb_ref, wa_ref, wx_ref, ba_ref,
                bx_ref, lam_ref, g_ref, o_ref, xp, yp, op, xbuf, a_scr, u_scr, hcar):
    seq = xr_ref.shape[0]
    n_seg = (seq + BLOCK) // LRU_SEG
    xp[0:BLOCK, :] = xrm_ref[...]
    xp[BLOCK:, :] = xr_ref[...]
    yp[0:BLOCK, :] = yrm_ref[...]
    yp[BLOCK:, :] = yr_ref[...]
    xbuf[0:SUBLANES, :] = jnp.zeros((SUBLANES, LRU_WIDTH), F32)
    hcar[...] = jnp.zeros_like(hcar)
    lam = lam_ref[...]
    softplus_neg = jnp.maximum(-lam, 0.0) + jnp.log(1.0 + jnp.exp(-jnp.abs(lam)))

    def strided(j):
        return pl.ds(j, SUBLANES, stride=LRU_CHUNK)

    def segment(k, carry):
        rows = pl.ds(pl.multiple_of(k * LRU_SEG, 2 * SUBLANES), LRU_SEG)
        xbuf[SUBLANES:, :] = xp[rows, :].astype(F32)
        xc = jnp.broadcast_to(cb_ref[...], (LRU_SEG, LRU_WIDTH))
        for tap in range(CONV_W):
            off = SUBLANES - (CONV_W - 1) + tap
            xc = xc + cw_ref[tap:tap + 1, :] * xbuf[off:off + LRU_SEG, :]
        xbuf[0:SUBLANES, :] = xbuf[LRU_SEG:LRU_SEG + SUBLANES, :]

        xcb = xc.astype(BF16)
        r = _sigmoid(jnp.dot(xcb, wa_ref[...], preferred_element_type=F32) + ba_ref[...])
        i = _sigmoid(jnp.dot(xcb, wx_ref[...], preferred_element_type=F32) + bx_ref[...])
        a = jnp.exp(-LRU_C * r * softplus_neg)
        z = 1.0 - a * a
        u = jnp.where(z > 0.0, z * lax.rsqrt(z), 0.0) * (i * xc)
        row = k * LRU_SEG + lax.broadcasted_iota(jnp.int32, (LRU_SEG, LRU_WIDTH), 0)
        u = jnp.where(row >= PAD_FRONT, u, 0.0)
        for c in range(LRU_SLABS):
            a_scr[c] = a[:, c * LANES:(c + 1) * LANES]
            u_scr[c] = u[:, c * LANES:(c + 1) * LANES]

        for c in range(LRU_SLABS):
            h = jnp.zeros((SUBLANES, LANES), F32)
            p = jnp.ones((SUBLANES, LANES), F32)
            for j in range(LRU_CHUNK):
                aj = a_scr[c, strided(j), :]
                h = aj * h + u_scr[c, strided(j), :]
                p = aj * p
                u_scr[c, strided(j), :] = h
                a_scr[c, strided(j), :] = p
            entry = [hcar[:, c * LANES:(c + 1) * LANES]]
            for s in range(SUBLANES):
                entry.append(h[s:s + 1, :] + p[s:s + 1, :] * entry[s])
            hcar[:, c * LANES:(c + 1) * LANES] = entry[SUBLANES]
            entry_rows = jnp.concatenate(entry[:SUBLANES], axis=0)
            for j in range(LRU_CHUNK):
                u_scr[c, strided(j), :] = (u_scr[c, strided(j), :]
                                           + a_scr[c, strided(j), :] * entry_rows)

        h_all = jnp.concatenate([u_scr[c] for c in range(LRU_SLABS)], axis=1)
        out = h_all * _gelu_tanh(yp[rows, :].astype(F32))
        ms = jnp.mean(out * out, axis=1, keepdims=True)
        op[rows, :] = (out * lax.rsqrt(ms + EPS) * g_ref[...]).astype(op.dtype)
        return carry

    lax.fori_loop(0, n_seg, segment, 0)
    o_ref[...] = op[BLOCK:, :]


def _rglru(xr, yr, xrm, yrm, cw, cb, wa, wx, ba, bx, lam, g_lru, bsz, nbx):
    seq = nbx * BLOCK
    assert (seq + BLOCK) % LRU_SEG == 0
    main = pl.BlockSpec((None, seq, LRU_WIDTH), lambda b: (b, 0, 0))
    const2 = lambda b: (0, 0)
    row_spec = pl.BlockSpec((1, LRU_WIDTH), const2)
    padded = pltpu.VMEM((seq + BLOCK, LRU_WIDTH), BF16)
    slabs = pltpu.VMEM((LRU_SLABS, LRU_SEG, LANES), F32)
    return pl.pallas_call(
        _lru_kernel,
        grid=(bsz,),
        in_specs=[main, main,
                  pl.BlockSpec((BLOCK, LRU_WIDTH), const2),
                  pl.BlockSpec((BLOCK, LRU_WIDTH), const2),
                  pl.BlockSpec((CONV_W, LRU_WIDTH), const2),
                  row_spec,
                  pl.BlockSpec((LRU_WIDTH, LRU_WIDTH), const2),
                  pl.BlockSpec((LRU_WIDTH, LRU_WIDTH), const2),
                  row_spec, row_spec, row_spec, row_spec],
        out_specs=main,
        out_shape=jax.ShapeDtypeStruct((bsz, seq, LRU_WIDTH), BF16),
        scratch_shapes=[padded, padded, padded,
                        pltpu.VMEM((LRU_SEG + SUBLANES, LRU_WIDTH), F32),
                        slabs, slabs,
                        pltpu.VMEM((1, LRU_WIDTH), F32)],
        compiler_params=_cparams(1),
        name="rglru",
    )(xr, yr, xrm, yrm, cw, cb, wa, wx, ba, bx, lam, g_lru)


def _pack_rows(v):
    bits = lax.bitcast_convert_type(v.astype(BF16).astype(F32), U32)
    return (bits[:, :PACKED] >> 16) | (bits[:, PACKED:] & jnp.uint32(0xFFFF0000))


def _unpack_rows(w):
    lo = lax.bitcast_convert_type(w << 16, F32)
    hi = lax.bitcast_convert_type(w & jnp.uint32(0xFFFF0000), F32)
    return lo, hi


def _layer_norm(z, g, b):
    mu = jnp.mean(z, axis=1, keepdims=True)
    zc = z - mu
    var = jnp.mean(zc * zc, axis=1, keepdims=True)
    return zc * lax.rsqrt(var + EPS) * g + b


def _out_proj_kernel(a_ref, l_ref, x_ref, wa_ref, wl_ref, g_ref, b_ref, wrt_hi_ref, wrt_lo_ref,
                     brt_ref, h_ref, hp_ref, info_ref):
    for c in range(h_ref.shape[0] // PROJ_ROWS):
        rows = slice(c * PROJ_ROWS, (c + 1) * PROJ_ROWS)
        _out_proj_rows(a_ref[rows, :], l_ref[rows, :], x_ref[rows, :], wa_ref, wl_ref, g_ref,
                       b_ref, wrt_hi_ref, wrt_lo_ref, brt_ref,
                       h_ref.at[rows, :], hp_ref.at[rows, :], info_ref.at[:, rows])


def _out_proj_rows(a, l, x, wa_ref, wl_ref, g_ref, b_ref, wrt_hi_ref, wrt_lo_ref, brt_ref,
                   h_ref, hp_ref, info_ref):
    mix = jnp.dot(a, wa_ref[...], preferred_element_type=F32)
    mix = mix + jnp.dot(l, wl_ref[...], preferred_element_type=F32)
    h = _layer_norm(ALPHA * x + mix, g_ref[...], b_ref[...])
    h_ref[...] = h
    hp_ref[...] = _pack_rows(h)

    h_hi = h.astype(BF16)
    h_lo = (h - h_hi.astype(F32)).astype(BF16)
    nt = (((1,), (1,)), ((), ()))
    lg = (lax.dot_general(wrt_hi_ref[...], h_hi, nt, preferred_element_type=F32)
          + lax.dot_general(wrt_lo_ref[...], h_hi, nt, preferred_element_type=F32)
          + lax.dot_general(wrt_hi_ref[...], h_lo, nt, preferred_element_type=F32)) + brt_ref[...]
    tile_shape = (SUBLANES, h.shape[0])
    sub = lax.broadcasted_iota(I32, tile_shape, 0)
    ninf = -jnp.inf
    t0 = lg[0:SUBLANES]
    gl = jnp.where(sub < N_GROUPS, t0, ninf)
    gmax = jnp.max(gl, axis=0, keepdims=True)
    g_idx = jnp.min(jnp.where(gl == gmax, sub, SUBLANES), axis=0, keepdims=True)
    g_w = 1.0 / jnp.sum(jnp.where(sub < N_GROUPS, jnp.exp(t0 - gmax), 0.0),
                        axis=0, keepdims=True)
    el = lg[SUBLANES:2 * SUBLANES]
    for g in range(1, N_GROUPS):
        el = jnp.where(g_idx == g, lg[(g + 1) * SUBLANES:(g + 2) * SUBLANES], el)
    v1 = jnp.max(el, axis=0, keepdims=True)
    i1 = jnp.min(jnp.where(el == v1, sub, SUBLANES), axis=0, keepdims=True)
    el2 = jnp.where(sub == i1, ninf, el)
    v2 = jnp.max(el2, axis=0, keepdims=True)
    i2 = jnp.min(jnp.where(el2 == v2, sub, SUBLANES), axis=0, keepdims=True)
    t = jnp.exp(v2 - v1)
    w1 = 1.0 / (1.0 + t)
    w2 = t * w1
    e_base = g_idx * EXPERTS_PER_GROUP
    info_ref[...] = jnp.where(sub == 0, (e_base + i1).astype(F32),
                              jnp.where(sub == 1, (e_base + i2).astype(F32),
                                        jnp.where(sub == 2, g_w * w1,
                                                  jnp.where(sub == 3, g_w * w2, 0.0))))


ROUTER_ROWS = (N_GROUPS + 1) * SUBLANES


def _out_proj(attn_n, lru_n, x2d, wo_a, wo_l, ln_g, ln_b, w_rt_hi, w_rt_lo, b_rt, rows):
    n = x2d.shape[0]
    const = lambda i: (0, 0)
    tile = lambda w: pl.BlockSpec((rows, w), lambda i: (i, 0))
    return pl.pallas_call(
        _out_proj_kernel,
        grid=(n // rows,),
        in_specs=[tile(ATTN_WIDTH), tile(LRU_WIDTH), tile(D_MODEL),
                  pl.BlockSpec((ATTN_WIDTH, D_MODEL), const),
                  pl.BlockSpec((LRU_WIDTH, D_MODEL), const),
                  pl.BlockSpec((1, D_MODEL), const),
                  pl.BlockSpec((1, D_MODEL), const),
                  pl.BlockSpec((ROUTER_ROWS, D_MODEL), const),
                  pl.BlockSpec((ROUTER_ROWS, D_MODEL), const),
                  pl.BlockSpec((ROUTER_ROWS, 1), const)],
        out_specs=[tile(D_MODEL), tile(PACKED),
                   pl.BlockSpec((SUBLANES, rows), lambda i: (0, i))],
        out_shape=[jax.ShapeDtypeStruct((n, D_MODEL), F32),
                   jax.ShapeDtypeStruct((n, PACKED), U32),
                   jax.ShapeDtypeStruct((SUBLANES, n), F32)],
        compiler_params=_cparams(1),
        name="out_proj",
    )(attn_n, lru_n, x2d, wo_a, wo_l, ln_g, ln_b, w_rt_hi, w_rt_lo, b_rt)


def _route_kernel(info_ref, tri_ref, dest_ref, cnt_ref, counts, carry, pstart):
    p = pl.program_id(0)
    t = pl.program_id(1)
    info = info_ref[...]
    shape = (N_EXPERTS, info.shape[1])
    expert = lax.broadcasted_iota(I32, shape, 0)
    oh1 = (expert == info[0:1, :].astype(I32)).astype(F32)
    oh2 = (expert == info[1:2, :].astype(I32)).astype(F32)
    both = oh1 + oh2
    tile_counts = jnp.sum(both, axis=1, keepdims=True)

    @pl.when((p == 0) & (t == 0))
    def _():
        counts[...] = jnp.zeros_like(counts)

    @pl.when(p == 0)
    def _():
        counts[...] += tile_counts

    @pl.when((p == 1) & (t == 0))
    def _():
        c = jnp.broadcast_to(counts[...], (N_EXPERTS, LANES)).astype(I32)
        padded = ((c + (MOE_BLOCK - 1)) // MOE_BLOCK) * MOE_BLOCK
        e = lax.broadcasted_iota(I32, (N_EXPERTS, LANES), 0)
        scan = padded
        for d in (1, 2, 4, 8, 16):
            scan = scan + jnp.where(e >= d, pltpu.roll(scan, d, axis=0), 0)
        pstart[...] = (scan - padded)[:, 0:1].astype(F32)
        carry[...] = jnp.zeros_like(carry)
        cnt_ref[...] = c

    @pl.when(p == 1)
    def _():
        before = jnp.dot(both.astype(BF16), tri_ref[...], preferred_element_type=F32)
        row_of = before + (carry[...] + pstart[...])
        r1 = jnp.sum(oh1 * row_of, axis=0, keepdims=True)
        r2 = jnp.sum(oh2 * row_of, axis=0, keepdims=True)
        sub = lax.broadcasted_iota(I32, dest_ref.shape, 0)
        dest_ref[...] = jnp.where(sub == 0, r1, jnp.where(sub == 1, r2, 0.0)).astype(I32)
        carry[...] += tile_counts


def _route(info_t, cols):
    n = info_t.shape[1]
    tri = jnp.asarray(np.triu(np.ones((cols, cols), np.float32), 1), BF16)
    return pl.pallas_call(
        _route_kernel,
        grid=(2, n // cols),
        in_specs=[pl.BlockSpec((SUBLANES, cols), lambda p, t: (0, t)),
                  pl.BlockSpec((cols, cols), lambda p, t: (0, 0))],
        out_specs=[pl.BlockSpec((SUBLANES, cols), lambda p, t: (0, t * p)),
                   pl.BlockSpec((N_EXPERTS, LANES), lambda p, t: (0, 0))],
        out_shape=[jax.ShapeDtypeStruct((SUBLANES, n), I32),
                   jax.ShapeDtypeStruct((N_EXPERTS, LANES), I32)],
        scratch_shapes=[pltpu.VMEM((N_EXPERTS, 1), F32), pltpu.VMEM((N_EXPERTS, 1), F32),
                        pltpu.VMEM((N_EXPERTS, 1), F32)],
        compiler_params=_cparams(2),
        name="route",
    )(info_t, tri)


def _sc_mesh():
    return plsc.VectorSubcoreMesh(core_axis_name="core", subcore_axis_name="subcore")


def _sc_worker_id():
    return lax.axis_index("subcore") * SC_CORES + lax.axis_index("core")


def _sc_scatter_rows(rows, d0, d1, cap):
    n, width = rows.shape
    per_worker = n // SC_WORKERS
    n_win = per_worker // SC_WINDOW

    def body(x_hbm, d0_hbm, d1_hbm, o_hbm, i0_v, i1_v, rows_v, rsem, sem0, sem1):
        wid = _sc_worker_id()
        pltpu.sync_copy(d0_hbm.at[wid], i0_v)
        pltpu.sync_copy(d1_hbm.at[wid], i1_v)

        def read(j):
            src = x_hbm.at[pl.ds(wid * per_worker + j * SC_WINDOW, SC_WINDOW)]
            return pltpu.make_async_copy(src, rows_v.at[j % 2], rsem.at[j % 2])

        def scatters(j):
            return (pltpu.make_async_copy(rows_v.at[j % 2], o_hbm.at[i0_v.at[j]], sem0.at[j % 2]),
                    pltpu.make_async_copy(rows_v.at[j % 2], o_hbm.at[i1_v.at[j]], sem1.at[j % 2]))

        read(0).start()
        for j in range(n_win):
            if j + 1 < n_win:
                if j >= 1:
                    for cp in scatters(j - 1):
                        cp.wait()
                read(j + 1).start()
            read(j).wait()
            for cp in scatters(j):
                cp.start()
        for j in range(max(n_win - 2, 0), n_win):
            for cp in scatters(j):
                cp.wait()

    return pl.kernel(
        body,
        out_type=jax.ShapeDtypeStruct((cap, width), rows.dtype),
        mesh=_sc_mesh(),
        scratch_types=[pltpu.VMEM((n_win, SC_WINDOW), I32), pltpu.VMEM((n_win, SC_WINDOW), I32),
                       pltpu.VMEM((2, SC_WINDOW, width), rows.dtype),
                       pltpu.SemaphoreType.DMA((2,)), pltpu.SemaphoreType.DMA((2,)),
                       pltpu.SemaphoreType.DMA((2,))],
        name="dispatch",
    )(rows, d0, d1)


def _sc_gather_rows(table, idx):
    width = table.shape[1]
    n_win = idx.shape[1]
    per_worker = n_win * SC_WINDOW

    def body(y_hbm, i_hbm, o_hbm, i_v, rows_v, gsem, wsem):
        wid = _sc_worker_id()
        pltpu.sync_copy(i_hbm.at[wid], i_v)

        def gather(j):
            return pltpu.make_async_copy(y_hbm.at[i_v.at[j]], rows_v.at[j % 2], gsem.at[j % 2])

        def write(j):
            dst = o_hbm.at[pl.ds(wid * per_worker + j * SC_WINDOW, SC_WINDOW)]
            return pltpu.make_async_copy(rows_v.at[j % 2], dst, wsem.at[j % 2])

        gather(0).start()
        for j in range(n_win):
            if j + 1 < n_win:
                if j >= 1:
                    write(j - 1).wait()
                gather(j + 1).start()
            gather(j).wait()
            write(j).start()
        for j in range(max(n_win - 2, 0), n_win):
            write(j).wait()

    return pl.kernel(
        body,
        out_type=jax.ShapeDtypeStruct((SC_WORKERS * per_worker, width), table.dtype),
        mesh=_sc_mesh(),
        scratch_types=[pltpu.VMEM((n_win, SC_WINDOW), I32),
                       pltpu.VMEM((2, SC_WINDOW, width), table.dtype),
                       pltpu.SemaphoreType.DMA((2,)), pltpu.SemaphoreType.DMA((2,))],
        name="collect",
    )(table, idx)


def _expert_kernel(bstart_ref, nblk_ref, nused_ref, xs_hbm, wg_ref, wu_ref, wd_ref, yb_hbm,
                   xbuf, ybuf, zbuf, xsem, ysem, zsem, wg_b, wu_b, wd_b):
    e = pl.program_id(0)
    nused = nused_ref[0]
    n_blocks = yb_hbm.shape[0] // MOE_BLOCK

    def rows(b):
        return pl.ds(pl.multiple_of(b * MOE_BLOCK, MOE_BLOCK), MOE_BLOCK)

    def x_copy(b):
        slot = b % X_RING
        return pltpu.make_async_copy(xs_hbm.at[rows(b)], xbuf.at[slot], xsem.at[slot])

    def y_copy(b, slot):
        return pltpu.make_async_copy(ybuf.at[slot], yb_hbm.at[rows(b)], ysem.at[slot])

    @pl.when(e == 0)
    def _():
        for b in range(X_RING - 1):
            @pl.when(b < nused)
            def _():
                x_copy(b).start()

    wg_b[...] = wg_ref[...].astype(BF16)
    wu_b[...] = wu_ref[...].astype(BF16)
    wd_b[...] = wd_ref[...].astype(BF16)

    def block(b, carry):
        slot = b % 2
        x_copy(b).wait()

        @pl.when(b + (X_RING - 1) < nused)
        def _():
            x_copy(b + (X_RING - 1)).start()

        @pl.when(b >= 2)
        def _():
            y_copy(b - 2, slot).wait()

        lo, hi = _unpack_rows(xbuf[b % X_RING])
        lo = lo.astype(BF16)
        hi = hi.astype(BF16)
        g = (jnp.dot(lo, wg_b[0:PACKED, :], preferred_element_type=F32)
             + jnp.dot(hi, wg_b[PACKED:, :], preferred_element_type=F32))
        u = (jnp.dot(lo, wu_b[0:PACKED, :], preferred_element_type=F32)
             + jnp.dot(hi, wu_b[PACKED:, :], preferred_element_type=F32))
        mid = (g * _sigmoid(g) * u).astype(BF16)
        ybuf[slot] = _pack_rows(jnp.dot(mid, wd_b[...], preferred_element_type=F32))
        y_copy(b, slot).start()
        return carry

    b0 = bstart_ref[e]
    lax.fori_loop(b0, b0 + nblk_ref[e], block, 0)

    @pl.when(e == pl.num_programs(0) - 1)
    def _():
        for back in (2, 1):
            @pl.when(nused >= back)
            def _():
                y_copy(nused - back, (nused - back) % 2).wait()

        zbuf[...] = jnp.zeros_like(zbuf)

        def z_copy(b):
            return pltpu.make_async_copy(zbuf, yb_hbm.at[rows(b)], zsem.at[0])

        def z_start(b, carry):
            z_copy(b).start()
            return carry

        def z_wait(b, carry):
            z_copy(b).wait()
            return carry

        lax.fori_loop(nused, n_blocks, z_start, 0)
        lax.fori_loop(nused, n_blocks, z_wait, 0)


def _experts(bstart, nblk, nused, xs, w_gate, w_up, w_down):
    cap = xs.shape[0]
    w_idx = lambda e, bs, nb, nu: (e, 0, 0)
    grid_spec = pltpu.PrefetchScalarGridSpec(
        num_scalar_prefetch=3,
        grid=(N_EXPERTS,),
        in_specs=[pl.BlockSpec(memory_space=pl.ANY),
                  pl.BlockSpec((None, D_MODEL, D_FF), w_idx),
                  pl.BlockSpec((None, D_MODEL, D_FF), w_idx),
                  pl.BlockSpec((None, D_FF, D_MODEL), w_idx)],
        out_specs=pl.BlockSpec(memory_space=pl.ANY),
        scratch_shapes=[pltpu.VMEM((X_RING, MOE_BLOCK, PACKED), U32),
                        pltpu.VMEM((2, MOE_BLOCK, PACKED), U32),
                        pltpu.VMEM((MOE_BLOCK, PACKED), U32),
                        pltpu.SemaphoreType.DMA((X_RING,)),
                        pltpu.SemaphoreType.DMA((2,)),
                        pltpu.SemaphoreType.DMA((1,)),
                        pltpu.VMEM((D_MODEL, D_FF), BF16),
                        pltpu.VMEM((D_MODEL, D_FF), BF16),
                        pltpu.VMEM((D_FF, D_MODEL), BF16)])
    return pl.pallas_call(
        _expert_kernel,
        grid_spec=grid_spec,
        out_shape=jax.ShapeDtypeStruct((cap, PACKED), U32),
        compiler_params=_cparams(1),
        name="experts",
    )(bstart, nblk, nused, xs, w_gate, w_up, w_down)


def _combine_kernel(y0_ref, y1_ref, h_ref, info_ref, g_ref, b_ref, o_ref):
    info = info_ref[...].T
    g0 = info[:, 2:3]
    g1 = info[:, 3:4]
    lo0, hi0 = _unpack_rows(y0_ref[...])
    lo1, hi1 = _unpack_rows(y1_ref[...])
    y = jnp.concatenate([g0 * lo0 + g1 * lo1, g0 * hi0 + g1 * hi1], axis=1)
    o_ref[...] = _layer_norm(ALPHA * h_ref[...] + y, g_ref[...], b_ref[...])


def _combine(ys, h, info, ln_g, ln_b, rows, part, n_parts):
    n = h.shape[0]
    steps = n // n_parts // rows
    off = part * steps
    const = lambda i: (0, 0)
    return pl.pallas_call(
        _combine_kernel,
        grid=(steps,),
        in_specs=[pl.BlockSpec((rows, PACKED), lambda i: (i, 0)),
                  pl.BlockSpec((rows, PACKED), lambda i: (i + steps, 0)),
                  pl.BlockSpec((rows, D_MODEL), lambda i: (i + off, 0)),
                  pl.BlockSpec((SUBLANES, rows), lambda i: (0, i + off)),
                  pl.BlockSpec((1, D_MODEL), const),
                  pl.BlockSpec((1, D_MODEL), const)],
        out_specs=pl.BlockSpec((rows, D_MODEL), lambda i: (i + off, 0)),
        out_shape=jax.ShapeDtypeStruct((n, D_MODEL), F32),
        input_output_aliases={2: 0},
        compiler_params=_cparams(1),
        name="combine",
    )(ys, ys, h, info, ln_g, ln_b)


def _alibi_bias():
    qi = np.arange(BLOCK)[:, None]
    kj = np.arange(2 * BLOCK)[None, :]
    dist = qi - kj + BLOCK
    band = (dist >= 0) & (dist < BLOCK)
    slopes = np.exp2(-8.0 * np.arange(1, N_Q_HEADS + 1, dtype=np.float32) / N_Q_HEADS)
    bias = np.where(band[None], -slopes[:, None, None] * dist[None].astype(np.float32), NEG)
    bias = bias * LOG2E
    first = np.where((kj >= PAD_FRONT)[None], bias, NEG)
    out = np.empty((2, N_KV_HEADS, 2 * BLOCK, 4 * BLOCK), np.float32)
    for v, per_head in enumerate((first, bias)):
        for j in range(N_KV_HEADS):
            out[v, j] = np.block([[per_head[4 * j], per_head[4 * j + 1]],
                                  [per_head[4 * j + 2], per_head[4 * j + 3]]])
    return jnp.asarray(out, F32)


def _block_diag(w):
    nb, c, _ = w.shape
    eye = jnp.eye(nb, dtype=w.dtype)
    return jnp.einsum('ncd,nm->ncmd', w, eye).reshape(nb * c, nb * c)


def kernel(x, meta_tokens, w_in, conv_w, conv_b, lru_wa, lru_ba, lru_wx, lru_bx, lru_lambda,
           attn_sinks, g_attn, g_lru, w_out, ln1_g, ln1_b, w_group, b_group, w_router,
           b_router, w_gate, w_up, w_down, ln2_g, ln2_b):
    bsz, seq, d = x.shape
    nbx = seq // BLOCK
    n_tok = bsz * seq
    x2d = x.reshape(n_tok, d)
    row = lambda v: v.reshape(1, -1).astype(F32)

    q_scale = jnp.concatenate([jnp.full((ATTN_WIDTH,), LOG2E * HEAD_DIM ** -0.5, F32),
                               jnp.ones((IN_COLS - ATTN_WIDTH,), F32)])
    w_in_b = (w_in[0] * q_scale).astype(BF16)
    meta_blk = jnp.concatenate([jnp.zeros((PAD_FRONT, d), F32), meta_tokens.astype(F32)], axis=0)
    q, kv, xr, yr = _in_proj(x2d, w_in_b, PROJ_ROWS)
    qm, kvm, xrm, yrm = _in_proj(meta_blk, w_in_b, BLOCK)
    shp = lambda a: a.reshape(bsz, seq, a.shape[-1])

    attn_n = _attention(attn_sinks[0].astype(F32) * LOG2E, shp(q), shp(kv), kvm, _alibi_bias(),
                        row(g_attn[0]), bsz, nbx)
    lru_n = _rglru(shp(xr), shp(yr), xrm, yrm, conv_w[0].astype(F32), row(conv_b[0]),
                   _block_diag(lru_wa[0]).astype(BF16), _block_diag(lru_wx[0]).astype(BF16),
                   row(lru_ba[0]), row(lru_bx[0]), row(lru_lambda[0]), row(g_lru[0]), bsz, nbx)

    w_out_b = w_out[0].astype(BF16)
    gpad = SUBLANES - N_GROUPS
    w_rt = jnp.concatenate(
        [w_group[0].T, jnp.zeros((gpad, d), F32),
         jnp.transpose(w_router[0], (0, 2, 1)).reshape(N_EXPERTS, d)], axis=0).astype(F32)
    w_rt_hi = w_rt.astype(BF16)
    w_rt_lo = (w_rt - w_rt_hi.astype(F32)).astype(BF16)
    b_rt = jnp.concatenate([b_group[0], jnp.zeros((gpad,), F32),
                            b_router[0].reshape(-1)]).astype(F32).reshape(ROUTER_ROWS, 1)
    h1, hp, info = _out_proj(attn_n.reshape(n_tok, ATTN_WIDTH), lru_n.reshape(n_tok, LRU_WIDTH),
                             x2d, w_out_b[:ATTN_WIDTH], w_out_b[ATTN_WIDTH:], row(ln1_g[0]),
                             row(ln1_b[0]), w_rt_hi, w_rt_lo, b_rt, OUT_PROJ_ROWS)

    dest, cnt = _route(info, ROUTE_ROWS)
    n_slots = n_tok * TOP_K
    n_blocks = n_slots // MOE_BLOCK + N_EXPERTS
    cap = n_blocks * MOE_BLOCK
    nblk = (cnt[:, 0] + MOE_BLOCK - 1) // MOE_BLOCK
    bends = jnp.cumsum(nblk)
    bstart = (bends - nblk).astype(I32)
    nused = bends[-1:].astype(I32)
    windows = lambda v: v.reshape(SC_WORKERS, -1, SC_WINDOW)
    d0 = dest[0]
    d1 = dest[1]

    xs = _sc_scatter_rows(hp, windows(d0), windows(d1), cap)
    yb = _experts(bstart, nblk.astype(I32), nused, xs, w_gate[0], w_up[0], w_down[0])
    out = h1
    part_len = n_tok // COMBINE_PARTS
    for part in range(COMBINE_PARTS):
        tok = slice(part * part_len, (part + 1) * part_len)
        ys = _sc_gather_rows(yb, windows(jnp.concatenate([d0[tok], d1[tok]])))
        out = _combine(ys, out, info, row(ln2_g[0]), row(ln2_b[0]), COMBINE_ROWS,
                       part, COMBINE_PARTS)
    return out.reshape(bsz, seq, d)
```

```python
import jax
import jax.numpy as jnp
import numpy as np
from jax import lax
from jax.experimental import pallas as pl
from jax.experimental.pallas import tpu as pltpu
from jax.experimental.pallas import tpu_sc as plsc

F32 = jnp.float32
BF16 = jnp.bfloat16
U32 = jnp.uint32
I32 = jnp.int32

D_MODEL = 1024
N_META = 16
BLOCK = 128
PAD_FRONT = BLOCK - N_META
HEAD_DIM = 64
ATTN_WIDTH = 512
LRU_WIDTH = 512
N_Q_HEADS = 8
N_KV_HEADS = 2
KV_WIDTH = N_KV_HEADS * HEAD_DIM
LRU_BLOCKS = 8
CONV_W = 4
LRU_C = 8.0
IN_COLS = ATTN_WIDTH + 2 * KV_WIDTH + 2 * LRU_WIDTH
N_GROUPS = 4
EXPERTS_PER_GROUP = 8
N_EXPERTS = N_GROUPS * EXPERTS_PER_GROUP
TOP_K = 2
D_FF = 512
MOE_BLOCK = 256
ALPHA = 2.0 ** 0.25
EPS = 1e-5
NEG = -1e30
LOG2E = float(np.log2(np.e))
LANES = 128
SUBLANES = 8
PACKED = D_MODEL // 2

PROJ_ROWS = 512
OUT_PROJ_ROWS = 1024
ROUTE_ROWS = 512
COMBINE_ROWS = 512
COMBINE_PARTS = 4
X_RING = 4
VMEM_LIMIT = 48 * 1024 * 1024

SC_CORES = 2
SC_SUBCORES = 16
SC_WORKERS = SC_CORES * SC_SUBCORES
SC_WINDOW = 64


def _cparams(n_axes):
    return pltpu.CompilerParams(
        dimension_semantics=("arbitrary",) * n_axes, vmem_limit_bytes=VMEM_LIMIT)


def _in_proj_kernel(x_ref, w_ref, q_ref, kv_ref, xr_ref, yr_ref):
    proj = jnp.dot(x_ref[...].astype(BF16), w_ref[...], preferred_element_type=F32)
    o = 0
    for ref, width in ((q_ref, ATTN_WIDTH), (kv_ref, 2 * KV_WIDTH),
                       (xr_ref, LRU_WIDTH), (yr_ref, LRU_WIDTH)):
        ref[...] = proj[:, o:o + width].astype(ref.dtype)
        o += width


def _in_proj(x2d, w_bf16, rows):
    n = x2d.shape[0]
    widths = (ATTN_WIDTH, 2 * KV_WIDTH, LRU_WIDTH, LRU_WIDTH)
    return pl.pallas_call(
        _in_proj_kernel,
        grid=(n // rows,),
        in_specs=[pl.BlockSpec((rows, D_MODEL), lambda i: (i, 0)),
                  pl.BlockSpec((D_MODEL, IN_COLS), lambda i: (0, 0))],
        out_specs=[pl.BlockSpec((rows, w), lambda i: (i, 0)) for w in widths],
        out_shape=[jax.ShapeDtypeStruct((n, w), BF16) for w in widths],
        compiler_params=_cparams(1),
        name="in_proj",
    )(x2d, w_bf16)


def _attn_kernel(sinks_ref, q_ref, kv_ref, kvm_ref, bias_ref, g_ref, o_ref,
                 klo, khi, vlo, vhi):
    nbx = q_ref.shape[0] // BLOCK
    lo_lanes = lax.broadcasted_iota(I32, (BLOCK // 2, LANES), 1) < HEAD_DIM

    def layout_block(n, blk):
        rows = pl.ds(pl.multiple_of(n * BLOCK, BLOCK), BLOCK)
        as_bf16 = lambda words: pltpu.bitcast(words, BF16)
        for src, dst_lo, dst_hi in ((blk[:, :KV_WIDTH], klo, khi), (blk[:, KV_WIDTH:], vlo, vhi)):
            w = pltpu.bitcast(src, U32)
            r = pltpu.roll(w, HEAD_DIM, axis=1)
            zero = jnp.zeros_like(w)
            dst_lo[0, rows, :] = as_bf16(jnp.where(lo_lanes, w, zero))
            dst_hi[0, rows, :] = as_bf16(jnp.where(lo_lanes, zero, r))
            dst_lo[1, rows, :] = as_bf16(jnp.where(lo_lanes, r, zero))
            dst_hi[1, rows, :] = as_bf16(jnp.where(lo_lanes, zero, w))

    layout_block(0, kvm_ref[...])

    def layout_body(n, carry):
        layout_block(n + 1, kv_ref[pl.ds(pl.multiple_of(n * BLOCK, BLOCK), BLOCK), :])
        return carry

    lax.fori_loop(0, nbx, layout_body, 0, unroll=4)

    ones_lo = jnp.where(lax.broadcasted_iota(I32, (2 * BLOCK, LANES), 1) < HEAD_DIM,
                        1.0, 0.0).astype(BF16)
    ones_hi = (1.0 - ones_lo.astype(F32)).astype(BF16)
    top_rows = lax.broadcasted_iota(I32, (2 * BLOCK, 1), 0) < BLOCK
    lo_half = lax.broadcasted_iota(I32, (2 * BLOCK, LANES), 1) < HEAD_DIM

    def block(i, carry):
        q_rows = pl.ds(pl.multiple_of(i * BLOCK, BLOCK), BLOCK)
        win = pl.ds(pl.multiple_of(i * BLOCK, BLOCK), 2 * BLOCK)
        q = q_ref[q_rows, :]
        first = jnp.minimum(i, 1)
        outs = []
        for j in range(N_KV_HEADS):
            q2 = jnp.concatenate([q[:, (2 * j) * LANES:(2 * j + 1) * LANES],
                                  q[:, (2 * j + 1) * LANES:(2 * j + 2) * LANES]], axis=0)
            kc = jnp.concatenate([klo[j, win, :], khi[j, win, :]], axis=0)
            s = lax.dot_general(q2, kc, (((1,), (1,)), ((), ())), preferred_element_type=F32)
            s = s + bias_ref[first, j]
            ps, es = [], []
            for c in range(2):
                sink = jnp.where(top_rows, sinks_ref[4 * j + c], sinks_ref[4 * j + 2 + c])
                sc = s[:, c * 2 * BLOCK:(c + 1) * 2 * BLOCK]
                m = jnp.maximum(jnp.max(sc, axis=1, keepdims=True), sink)
                ps.append(jnp.exp2(sc - m).astype(BF16))
                es.append(jnp.exp2(sink - m))
            v_lo = jnp.concatenate([vlo[j, win, :], ones_lo], axis=1)
            v_hi = jnp.concatenate([vhi[j, win, :], ones_hi], axis=1)
            r = (jnp.dot(ps[0], v_lo, preferred_element_type=F32)
                 + jnp.dot(ps[1], v_hi, preferred_element_type=F32))
            den = r[:, LANES:] + jnp.where(lo_half, es[0], es[1])
            o2 = r[:, :LANES] * (1.0 / den)
            outs += [o2[:BLOCK], o2[BLOCK:]]
        out = jnp.concatenate(outs, axis=1)
        ms = jnp.mean(out * out, axis=1, keepdims=True)
        o_ref[q_rows, :] = (out * lax.rsqrt(ms + EPS) * g_ref[...]).astype(o_ref.dtype)
        return carry

    lax.fori_loop(0, nbx, block, 0, unroll=4)


def _attention(sinks, q, kv, kvm, bias, g_attn, bsz, nbx):
    seq = nbx * BLOCK
    const2 = lambda b: (0, 0)
    kv_scratch = pltpu.VMEM((N_KV_HEADS, seq + BLOCK, LANES), BF16)
    return pl.pallas_call(
        _attn_kernel,
        grid=(bsz,),
        in_specs=[pl.BlockSpec(memory_space=pltpu.SMEM),
                  pl.BlockSpec((None, seq, ATTN_WIDTH), lambda b: (b, 0, 0)),
                  pl.BlockSpec((None, seq, 2 * KV_WIDTH), lambda b: (b, 0, 0)),
                  pl.BlockSpec((BLOCK, 2 * KV_WIDTH), const2),
                  pl.BlockSpec((2, N_KV_HEADS, 2 * BLOCK, 4 * BLOCK), lambda b: (0, 0, 0, 0)),
                  pl.BlockSpec((1, ATTN_WIDTH), const2)],
        out_specs=pl.BlockSpec((None, seq, ATTN_WIDTH), lambda b: (b, 0, 0)),
        out_shape=jax.ShapeDtypeStruct((bsz, seq, ATTN_WIDTH), BF16),
        scratch_shapes=[kv_scratch, kv_scratch, kv_scratch, kv_scratch],
        compiler_params=_cparams(1),
        name="attention",
    )(sinks, q, kv, kvm, bias, g_attn)


def _sigmoid(v):
    return 0.5 * jnp.tanh(0.5 * v) + 0.5


def _gelu_tanh(y):
    c = np.sqrt(2.0 / np.pi).astype(np.float32)
    return 0.5 * y * (1.0 + jnp.tanh(c * (y + 0.044715 * (y * y * y))))


LRU_CHUNK = 44
LRU_SEG = SUBLANES * LRU_CHUNK
LRU_SLABS = LRU_WIDTH // LANES


def _lru_kernel(xr_ref, yr_ref, xrm_ref, yrm_ref, cw_ref, cb_ref, wa_ref, wx_ref, ba_ref,
                bx_ref, lam_ref, g_ref, o_ref, xp, yp, op, x_st, y_st, o_st, s_st, xtail, hcar):
    seq = xr_ref.shape[0]
    n_seg = (seq + BLOCK) // LRU_SEG
    xp[0:BLOCK, :] = xrm_ref[...]
    xp[BLOCK:, :] = xr_ref[...]
    yp[0:BLOCK, :] = yrm_ref[...]
    yp[BLOCK:, :] = yr_ref[...]
    xtail[...] = jnp.zeros_like(xtail)
    hcar[...] = jnp.zeros_like(hcar)
    lam = lam_ref[...]
    softplus_neg = jnp.maximum(-lam, 0.0) + jnp.log(1.0 + jnp.exp(-jnp.abs(lam)))
    sub = lax.broadcasted_iota(jnp.int32, (SUBLANES, LANES), 0)

    def strided(j):
        return pl.ds(j, SUBLANES, stride=LRU_CHUNK)

    def piece(v, j):
        return v[j * SUBLANES:(j + 1) * SUBLANES, :]

    def segment(k, carry):
        rows = pl.ds(pl.multiple_of(k * LRU_SEG, 2 * SUBLANES), LRU_SEG)
        x_nat = xp[rows, :].astype(F32)
        y_nat = yp[rows, :].astype(F32)
        for c in range(LRU_SLABS):
            x_st[c] = x_nat[:, c * LANES:(c + 1) * LANES]
            y_st[c] = y_nat[:, c * LANES:(c + 1) * LANES]
        first_row = k * LRU_SEG + LRU_CHUNK * sub
        sumsq = [jnp.zeros((SUBLANES, LANES), F32) for _ in range(LRU_CHUNK)]

        for c in range(LRU_SLABS):
            lanes = slice(c * LANES, (c + 1) * LANES)
            x = [x_st[c, strided(j), :] for j in range(LRU_CHUNK)]
            before = []
            for d in range(1, CONV_W):
                from_prev_chunk = pltpu.roll(x[LRU_CHUNK - d], 1, axis=0)
                before.append(jnp.where(sub == 0, xtail[d - 1:d, lanes], from_prev_chunk))
            for d in range(1, CONV_W):
                xtail[d - 1:d, lanes] = x[LRU_CHUNK - d][SUBLANES - 1:SUBLANES, :]

            def x_at(j):
                return x[j] if j >= 0 else before[-j - 1]

            taps = [cw_ref[t:t + 1, lanes] for t in range(CONV_W)]
            bias = cb_ref[:, lanes]
            xc = jnp.concatenate(
                [bias + sum(taps[t] * x_at(j - (CONV_W - 1) + t) for t in range(CONV_W))
                 for j in range(LRU_CHUNK)], axis=0)
            xcb = xc.astype(BF16)
            r = _sigmoid(jnp.dot(xcb, wa_ref[c], preferred_element_type=F32) + ba_ref[:, lanes])
            i = _sigmoid(jnp.dot(xcb, wx_ref[c], preferred_element_type=F32) + bx_ref[:, lanes])
            a = jnp.exp(-LRU_C * r * softplus_neg[:, lanes])
            z = 1.0 - a * a
            u = jnp.where(z > 0.0, z * lax.rsqrt(z), 0.0) * (i * xc)

            h = jnp.zeros((SUBLANES, LANES), F32)
            p = jnp.ones((SUBLANES, LANES), F32)
            hs, ps = [], []
            for j in range(LRU_CHUNK):
                aj = piece(a, j)
                uj = jnp.where(first_row + j >= PAD_FRONT, piece(u, j), 0.0)
                h = aj * h + uj
                p = aj * p
                hs.append(h)
                ps.append(p)
            entry = [hcar[:, lanes]]
            for s in range(SUBLANES):
                entry.append(h[s:s + 1, :] + p[s:s + 1, :] * entry[s])
            hcar[:, lanes] = entry[SUBLANES]
            entry_rows = jnp.concatenate(entry[:SUBLANES], axis=0)

            for j in range(LRU_CHUNK):
                state = hs[j] + ps[j] * entry_rows
                out = state * _gelu_tanh(y_st[c, strided(j), :])
                sumsq[j] = sumsq[j] + out * out
                o_st[c, strided(j), :] = out

        for j in range(LRU_CHUNK):
            ms = jnp.sum(sumsq[j], axis=1, keepdims=True) * (1.0 / LRU_WIDTH)
            s_st[strided(j), :] = jnp.broadcast_to(lax.rsqrt(ms + EPS), (SUBLANES, LANES))
        scale = s_st[...]
        for c in range(LRU_SLABS):
            lanes = slice(c * LANES, (c + 1) * LANES)
            op[rows, lanes] = (o_st[c] * scale * g_ref[:, lanes]).astype(op.dtype)
        return carry

    lax.fori_loop(0, n_seg, segment, 0)
    o_ref[...] = op[BLOCK:, :]


def _rglru(xr, yr, xrm, yrm, cw, cb, wa, wx, ba, bx, lam, g_lru, bsz, nbx):
    seq = nbx * BLOCK
    assert (seq + BLOCK) % LRU_SEG == 0
    main = pl.BlockSpec((None, seq, LRU_WIDTH), lambda b: (b, 0, 0))
    const2 = lambda b: (0, 0)
    row_spec = pl.BlockSpec((1, LRU_WIDTH), const2)
    gate_spec = pl.BlockSpec((LRU_SLABS, LANES, LANES), lambda b: (0, 0, 0))
    padded = pltpu.VMEM((seq + BLOCK, LRU_WIDTH), BF16)
    slabs = pltpu.VMEM((LRU_SLABS, LRU_SEG, LANES), F32)
    return pl.pallas_call(
        _lru_kernel,
        grid=(bsz,),
        in_specs=[main, main,
                  pl.BlockSpec((BLOCK, LRU_WIDTH), const2),
                  pl.BlockSpec((BLOCK, LRU_WIDTH), const2),
                  pl.BlockSpec((CONV_W, LRU_WIDTH), const2),
                  row_spec, gate_spec, gate_spec,
                  row_spec, row_spec, row_spec, row_spec],
        out_specs=main,
        out_shape=jax.ShapeDtypeStruct((bsz, seq, LRU_WIDTH), BF16),
        scratch_shapes=[padded, padded, padded, slabs, slabs, slabs,
                        pltpu.VMEM((LRU_SEG, LANES), F32),
                        pltpu.VMEM((SUBLANES, LRU_WIDTH), F32),
                        pltpu.VMEM((1, LRU_WIDTH), F32)],
        compiler_params=_cparams(1),
        name="rglru",
    )(xr, yr, xrm, yrm, cw, cb, wa, wx, ba, bx, lam, g_lru)


def _pack_rows(v):
    bits = lax.bitcast_convert_type(v.astype(BF16).astype(F32), U32)
    return (bits[:, :PACKED] >> 16) | (bits[:, PACKED:] & jnp.uint32(0xFFFF0000))


def _unpack_rows(w):
    lo = lax.bitcast_convert_type(w << 16, F32)
    hi = lax.bitcast_convert_type(w & jnp.uint32(0xFFFF0000), F32)
    return lo, hi


def _layer_norm(z, g, b):
    mu = jnp.mean(z, axis=1, keepdims=True)
    zc = z - mu
    var = jnp.mean(zc * zc, axis=1, keepdims=True)
    return zc * lax.rsqrt(var + EPS) * g + b


def _out_proj_kernel(a_ref, l_ref, x_ref, wa_ref, wl_ref, g_ref, b_ref, wrt_hi_ref, wrt_lo_ref,
                     brt_ref, h_ref, hp_ref, info_ref):
    for c in range(h_ref.shape[0] // PROJ_ROWS):
        rows = slice(c * PROJ_ROWS, (c + 1) * PROJ_ROWS)
        _out_proj_rows(a_ref[rows, :], l_ref[rows, :], x_ref[rows, :], wa_ref, wl_ref, g_ref,
                       b_ref, wrt_hi_ref, wrt_lo_ref, brt_ref,
                       h_ref.at[rows, :], hp_ref.at[rows, :], info_ref.at[:, rows])


def _out_proj_rows(a, l, x, wa_ref, wl_ref, g_ref, b_ref, wrt_hi_ref, wrt_lo_ref, brt_ref,
                   h_ref, hp_ref, info_ref):
    mix = jnp.dot(a, wa_ref[...], preferred_element_type=F32)
    mix = mix + jnp.dot(l, wl_ref[...], preferred_element_type=F32)
    h = _layer_norm(ALPHA * x + mix, g_ref[...], b_ref[...])
    h_ref[...] = h
    hp_ref[...] = _pack_rows(h)

    h_hi = h.astype(BF16)
    h_lo = (h - h_hi.astype(F32)).astype(BF16)
    nt = (((1,), (1,)), ((), ()))
    lg = (lax.dot_general(wrt_hi_ref[...], h_hi, nt, preferred_element_type=F32)
          + lax.dot_general(wrt_lo_ref[...], h_hi, nt, preferred_element_type=F32)
          + lax.dot_general(wrt_hi_ref[...], h_lo, nt, preferred_element_type=F32)) + brt_ref[...]
    tile_shape = (SUBLANES, h.shape[0])
    sub = lax.broadcasted_iota(I32, tile_shape, 0)
    ninf = -jnp.inf
    t0 = lg[0:SUBLANES]
    gl = jnp.where(sub < N_GROUPS, t0, ninf)
    gmax = jnp.max(gl, axis=0, keepdims=True)
    g_idx = jnp.min(jnp.where(gl == gmax, sub, SUBLANES), axis=0, keepdims=True)
    g_w = 1.0 / jnp.sum(jnp.where(sub < N_GROUPS, jnp.exp(t0 - gmax), 0.0),
                        axis=0, keepdims=True)
    el = lg[SUBLANES:2 * SUBLANES]
    for g in range(1, N_GROUPS):
        el = jnp.where(g_idx == g, lg[(g + 1) * SUBLANES:(g + 2) * SUBLANES], el)
    v1 = jnp.max(el, axis=0, keepdims=True)
    i1 = jnp.min(jnp.where(el == v1, sub, SUBLANES), axis=0, keepdims=True)
    el2 = jnp.where(sub == i1, ninf, el)
    v2 = jnp.max(el2, axis=0, keepdims=True)
    i2 = jnp.min(jnp.where(el2 == v2, sub, SUBLANES), axis=0, keepdims=True)
    t = jnp.exp(v2 - v1)
    w1 = 1.0 / (1.0 + t)
    w2 = t * w1
    e_base = g_idx * EXPERTS_PER_GROUP
    info_ref[...] = jnp.where(sub == 0, (e_base + i1).astype(F32),
                              jnp.where(sub == 1, (e_base + i2).astype(F32),
                                        jnp.where(sub == 2, g_w * w1,
                                                  jnp.where(sub == 3, g_w * w2, 0.0))))


ROUTER_ROWS = (N_GROUPS + 1) * SUBLANES


def _out_proj(attn_n, lru_n, x2d, wo_a, wo_l, ln_g, ln_b, w_rt_hi, w_rt_lo, b_rt, rows):
    n = x2d.shape[0]
    const = lambda i: (0, 0)
    tile = lambda w: pl.BlockSpec((rows, w), lambda i: (i, 0))
    return pl.pallas_call(
        _out_proj_kernel,
        grid=(n // rows,),
        in_specs=[tile(ATTN_WIDTH), tile(LRU_WIDTH), tile(D_MODEL),
                  pl.BlockSpec((ATTN_WIDTH, D_MODEL), const),
                  pl.BlockSpec((LRU_WIDTH, D_MODEL), const),
                  pl.BlockSpec((1, D_MODEL), const),
                  pl.BlockSpec((1, D_MODEL), const),
                  pl.BlockSpec((ROUTER_ROWS, D_MODEL), const),
                  pl.BlockSpec((ROUTER_ROWS, D_MODEL), const),
                  pl.BlockSpec((ROUTER_ROWS, 1), const)],
        out_specs=[tile(D_MODEL), tile(PACKED),
                   pl.BlockSpec((SUBLANES, rows), lambda i: (0, i))],
        out_shape=[jax.ShapeDtypeStruct((n, D_MODEL), F32),
                   jax.ShapeDtypeStruct((n, PACKED), U32),
                   jax.ShapeDtypeStruct((SUBLANES, n), F32)],
        compiler_params=_cparams(1),
        name="out_proj",
    )(attn_n, lru_n, x2d, wo_a, wo_l, ln_g, ln_b, w_rt_hi, w_rt_lo, b_rt)


def _route_kernel(info_ref, tri_ref, dest_ref, cnt_ref, counts, carry, pstart):
    p = pl.program_id(0)
    t = pl.program_id(1)
    info = info_ref[...]
    shape = (N_EXPERTS, info.shape[1])
    expert = lax.broadcasted_iota(I32, shape, 0)
    oh1 = (expert == info[0:1, :].astype(I32)).astype(F32)
    oh2 = (expert == info[1:2, :].astype(I32)).astype(F32)
    both = oh1 + oh2
    tile_counts = jnp.sum(both, axis=1, keepdims=True)

    @pl.when((p == 0) & (t == 0))
    def _():
        counts[...] = jnp.zeros_like(counts)

    @pl.when(p == 0)
    def _():
        counts[...] += tile_counts

    @pl.when((p == 1) & (t == 0))
    def _():
        c = jnp.broadcast_to(counts[...], (N_EXPERTS, LANES)).astype(I32)
        padded = ((c + (MOE_BLOCK - 1)) // MOE_BLOCK) * MOE_BLOCK
        e = lax.broadcasted_iota(I32, (N_EXPERTS, LANES), 0)
        scan = padded
        for d in (1, 2, 4, 8, 16):
            scan = scan + jnp.where(e >= d, pltpu.roll(scan, d, axis=0), 0)
        pstart[...] = (scan - padded)[:, 0:1].astype(F32)
        carry[...] = jnp.zeros_like(carry)
        cnt_ref[...] = c

    @pl.when(p == 1)
    def _():
        before = jnp.dot(both.astype(BF16), tri_ref[...], preferred_element_type=F32)
        row_of = before + (carry[...] + pstart[...])
        r1 = jnp.sum(oh1 * row_of, axis=0, keepdims=True)
        r2 = jnp.sum(oh2 * row_of, axis=0, keepdims=True)
        sub = lax.broadcasted_iota(I32, dest_ref.shape, 0)
        dest_ref[...] = jnp.where(sub == 0, r1, jnp.where(sub == 1, r2, 0.0)).astype(I32)
        carry[...] += tile_counts


def _route(info_t, cols):
    n = info_t.shape[1]
    tri = jnp.asarray(np.triu(np.ones((cols, cols), np.float32), 1), BF16)
    return pl.pallas_call(
        _route_kernel,
        grid=(2, n // cols),
        in_specs=[pl.BlockSpec((SUBLANES, cols), lambda p, t: (0, t)),
                  pl.BlockSpec((cols, cols), lambda p, t: (0, 0))],
        out_specs=[pl.BlockSpec((SUBLANES, cols), lambda p, t: (0, t * p)),
                   pl.BlockSpec((N_EXPERTS, LANES), lambda p, t: (0, 0))],
        out_shape=[jax.ShapeDtypeStruct((SUBLANES, n), I32),
                   jax.ShapeDtypeStruct((N_EXPERTS, LANES), I32)],
        scratch_shapes=[pltpu.VMEM((N_EXPERTS, 1), F32), pltpu.VMEM((N_EXPERTS, 1), F32),
                        pltpu.VMEM((N_EXPERTS, 1), F32)],
        compiler_params=_cparams(2),
        name="route",
    )(info_t, tri)


def _sc_mesh():
    return plsc.VectorSubcoreMesh(core_axis_name="core", subcore_axis_name="subcore")


def _sc_worker_id():
    return lax.axis_index("subcore") * SC_CORES + lax.axis_index("core")


def _sc_scatter_rows(rows, d0, d1, cap):
    n, width = rows.shape
    per_worker = n // SC_WORKERS
    n_win = per_worker // SC_WINDOW

    def body(x_hbm, d0_hbm, d1_hbm, o_hbm, i0_v, i1_v, rows_v, rsem, sem0, sem1):
        wid = _sc_worker_id()
        pltpu.sync_copy(d0_hbm.at[wid], i0_v)
        pltpu.sync_copy(d1_hbm.at[wid], i1_v)

        def read(j):
            src = x_hbm.at[pl.ds(wid * per_worker + j * SC_WINDOW, SC_WINDOW)]
            return pltpu.make_async_copy(src, rows_v.at[j % 2], rsem.at[j % 2])

        def scatters(j):
            return (pltpu.make_async_copy(rows_v.at[j % 2], o_hbm.at[i0_v.at[j]], sem0.at[j % 2]),
                    pltpu.make_async_copy(rows_v.at[j % 2], o_hbm.at[i1_v.at[j]], sem1.at[j % 2]))

        read(0).start()
        for j in range(n_win):
            if j + 1 < n_win:
                if j >= 1:
                    for cp in scatters(j - 1):
                        cp.wait()
                read(j + 1).start()
            read(j).wait()
            for cp in scatters(j):
                cp.start()
        for j in range(max(n_win - 2, 0), n_win):
            for cp in scatters(j):
                cp.wait()

    return pl.kernel(
        body,
        out_type=jax.ShapeDtypeStruct((cap, width), rows.dtype),
        mesh=_sc_mesh(),
        scratch_types=[pltpu.VMEM((n_win, SC_WINDOW), I32), pltpu.VMEM((n_win, SC_WINDOW), I32),
                       pltpu.VMEM((2, SC_WINDOW, width), rows.dtype),
                       pltpu.SemaphoreType.DMA((2,)), pltpu.SemaphoreType.DMA((2,)),
                       pltpu.SemaphoreType.DMA((2,))],
        name="dispatch",
    )(rows, d0, d1)


def _sc_gather_rows(table, idx):
    width = table.shape[1]
    n_win = idx.shape[1]
    per_worker = n_win * SC_WINDOW

    def body(y_hbm, i_hbm, o_hbm, i_v, rows_v, gsem, wsem):
        wid = _sc_worker_id()
        pltpu.sync_copy(i_hbm.at[wid], i_v)

        def gather(j):
            return pltpu.make_async_copy(y_hbm.at[i_v.at[j]], rows_v.at[j % 2], gsem.at[j % 2])

        def write(j):
            dst = o_hbm.at[pl.ds(wid * per_worker + j * SC_WINDOW, SC_WINDOW)]
            return pltpu.make_async_copy(rows_v.at[j % 2], dst, wsem.at[j % 2])

        gather(0).start()
        for j in range(n_win):
            if j + 1 < n_win:
                if j >= 1:
                    write(j - 1).wait()
                gather(j + 1).start()
            gather(j).wait()
            write(j).start()
        for j in range(max(n_win - 2, 0), n_win):
            write(j).wait()

    return pl.kernel(
        body,
        out_type=jax.ShapeDtypeStruct((SC_WORKERS * per_worker, width), table.dtype),
        mesh=_sc_mesh(),
        scratch_types=[pltpu.VMEM((n_win, SC_WINDOW), I32),
                       pltpu.VMEM((2, SC_WINDOW, width), table.dtype),
                       pltpu.SemaphoreType.DMA((2,)), pltpu.SemaphoreType.DMA((2,))],
        name="collect",
    )(table, idx)


def _expert_kernel(bstart_ref, nblk_ref, nused_ref, xs_hbm, wg_ref, wu_ref, wd_ref, yb_hbm,
                   xbuf, ybuf, zbuf, xsem, ysem, zsem, wg_b, wu_b, wd_b):
    e = pl.program_id(0)
    nused = nused_ref[0]
    n_blocks = yb_hbm.shape[0] // MOE_BLOCK

    def rows(b):
        return pl.ds(pl.multiple_of(b * MOE_BLOCK, MOE_BLOCK), MOE_BLOCK)

    def x_copy(b):
        slot = b % X_RING
        return pltpu.make_async_copy(xs_hbm.at[rows(b)], xbuf.at[slot], xsem.at[slot])

    def y_copy(b, slot):
        return pltpu.make_async_copy(ybuf.at[slot], yb_hbm.at[rows(b)], ysem.at[slot])

    @pl.when(e == 0)
    def _():
        for b in range(X_RING - 1):
            @pl.when(b < nused)
            def _():
                x_copy(b).start()

    wg_b[...] = wg_ref[...].astype(BF16)
    wu_b[...] = wu_ref[...].astype(BF16)
    wd_b[...] = wd_ref[...].astype(BF16)

    def block(b, carry):
        slot = b % 2
        x_copy(b).wait()

        @pl.when(b + (X_RING - 1) < nused)
        def _():
            x_copy(b + (X_RING - 1)).start()

        @pl.when(b >= 2)
        def _():
            y_copy(b - 2, slot).wait()

        lo, hi = _unpack_rows(xbuf[b % X_RING])
        lo = lo.astype(BF16)
        hi = hi.astype(BF16)
        g = (jnp.dot(lo, wg_b[0:PACKED, :], preferred_element_type=F32)
             + jnp.dot(hi, wg_b[PACKED:, :], preferred_element_type=F32))
        u = (jnp.dot(lo, wu_b[0:PACKED, :], preferred_element_type=F32)
             + jnp.dot(hi, wu_b[PACKED:, :], preferred_element_type=F32))
        mid = (g * _sigmoid(g) * u).astype(BF16)
        ybuf[slot] = _pack_rows(jnp.dot(mid, wd_b[...], preferred_element_type=F32))
        y_copy(b, slot).start()
        return carry

    b0 = bstart_ref[e]
    lax.fori_loop(b0, b0 + nblk_ref[e], block, 0)

    @pl.when(e == pl.num_programs(0) - 1)
    def _():
        for back in (2, 1):
            @pl.when(nused >= back)
            def _():
                y_copy(nused - back, (nused - back) % 2).wait()

        zbuf[...] = jnp.zeros_like(zbuf)

        def z_copy(b):
            return pltpu.make_async_copy(zbuf, yb_hbm.at[rows(b)], zsem.at[0])

        def z_start(b, carry):
            z_copy(b).start()
            return carry

        def z_wait(b, carry):
            z_copy(b).wait()
            return carry

        lax.fori_loop(nused, n_blocks, z_start, 0)
        lax.fori_loop(nused, n_blocks, z_wait, 0)


def _experts(bstart, nblk, nused, xs, w_gate, w_up, w_down):
    cap = xs.shape[0]
    w_idx = lambda e, bs, nb, nu: (e, 0, 0)
    grid_spec = pltpu.PrefetchScalarGridSpec(
        num_scalar_prefetch=3,
        grid=(N_EXPERTS,),
        in_specs=[pl.BlockSpec(memory_space=pl.ANY),
                  pl.BlockSpec((None, D_MODEL, D_FF), w_idx),
                  pl.BlockSpec((None, D_MODEL, D_FF), w_idx),
                  pl.BlockSpec((None, D_FF, D_MODEL), w_idx)],
        out_specs=pl.BlockSpec(memory_space=pl.ANY),
        scratch_shapes=[pltpu.VMEM((X_RING, MOE_BLOCK, PACKED), U32),
                        pltpu.VMEM((2, MOE_BLOCK, PACKED), U32),
                        pltpu.VMEM((MOE_BLOCK, PACKED), U32),
                        pltpu.SemaphoreType.DMA((X_RING,)),
                        pltpu.SemaphoreType.DMA((2,)),
                        pltpu.SemaphoreType.DMA((1,)),
                        pltpu.VMEM((D_MODEL, D_FF), BF16),
                        pltpu.VMEM((D_MODEL, D_FF), BF16),
                        pltpu.VMEM((D_FF, D_MODEL), BF16)])
    return pl.pallas_call(
        _expert_kernel,
        grid_spec=grid_spec,
        out_shape=jax.ShapeDtypeStruct((cap, PACKED), U32),
        compiler_params=_cparams(1),
        name="experts",
    )(bstart, nblk, nused, xs, w_gate, w_up, w_down)


def _combine_kernel(y0_ref, y1_ref, h_ref, info_ref, g_ref, b_ref, o_ref):
    info = info_ref[...].T
    g0 = info[:, 2:3]
    g1 = info[:, 3:4]
    lo0, hi0 = _unpack_rows(y0_ref[...])
    lo1, hi1 = _unpack_rows(y1_ref[...])
    y = jnp.concatenate([g0 * lo0 + g1 * lo1, g0 * hi0 + g1 * hi1], axis=1)
    o_ref[...] = _layer_norm(ALPHA * h_ref[...] + y, g_ref[...], b_ref[...])


def _combine(ys, h, info, ln_g, ln_b, rows, part, n_parts):
    n = h.shape[0]
    steps = n // n_parts // rows
    off = part * steps
    const = lambda i: (0, 0)
    return pl.pallas_call(
        _combine_kernel,
        grid=(steps,),
        in_specs=[pl.BlockSpec((rows, PACKED), lambda i: (i, 0)),
                  pl.BlockSpec((rows, PACKED), lambda i: (i + steps, 0)),
                  pl.BlockSpec((rows, D_MODEL), lambda i: (i + off, 0)),
                  pl.BlockSpec((SUBLANES, rows), lambda i: (0, i + off)),
                  pl.BlockSpec((1, D_MODEL), const),
                  pl.BlockSpec((1, D_MODEL), const)],
        out_specs=pl.BlockSpec((rows, D_MODEL), lambda i: (i + off, 0)),
        out_shape=jax.ShapeDtypeStruct((n, D_MODEL), F32),
        input_output_aliases={2: 0},
        compiler_params=_cparams(1),
        name="combine",
    )(ys, ys, h, info, ln_g, ln_b)


def _alibi_bias():
    qi = np.arange(BLOCK)[:, None]
    kj = np.arange(2 * BLOCK)[None, :]
    dist = qi - kj + BLOCK
    band = (dist >= 0) & (dist < BLOCK)
    slopes = np.exp2(-8.0 * np.arange(1, N_Q_HEADS + 1, dtype=np.float32) / N_Q_HEADS)
    bias = np.where(band[None], -slopes[:, None, None] * dist[None].astype(np.float32), NEG)
    bias = bias * LOG2E
    first = np.where((kj >= PAD_FRONT)[None], bias, NEG)
    out = np.empty((2, N_KV_HEADS, 2 * BLOCK, 4 * BLOCK), np.float32)
    for v, per_head in enumerate((first, bias)):
        for j in range(N_KV_HEADS):
            out[v, j] = np.block([[per_head[4 * j], per_head[4 * j + 1]],
                                  [per_head[4 * j + 2], per_head[4 * j + 3]]])
    return jnp.asarray(out, F32)


def _slab_gates(w):
    nb, c, _ = w.shape
    per = LANES // c
    w = w.reshape(nb // per, per, c, c)
    eye = jnp.eye(per, dtype=w.dtype)
    return jnp.einsum('spcd,pq->spcqd', w, eye).reshape(nb // per, LANES, LANES).astype(BF16)


def kernel(x, meta_tokens, w_in, conv_w, conv_b, lru_wa, lru_ba, lru_wx, lru_bx, lru_lambda,
           attn_sinks, g_attn, g_lru, w_out, ln1_g, ln1_b, w_group, b_group, w_router,
           b_router, w_gate, w_up, w_down, ln2_g, ln2_b):
    bsz, seq, d = x.shape
    nbx = seq // BLOCK
    n_tok = bsz * seq
    x2d = x.reshape(n_tok, d)
    row = lambda v: v.reshape(1, -1).astype(F32)

    q_scale = jnp.concatenate([jnp.full((ATTN_WIDTH,), LOG2E * HEAD_DIM ** -0.5, F32),
                               jnp.ones((IN_COLS - ATTN_WIDTH,), F32)])
    w_in_b = (w_in[0] * q_scale).astype(BF16)
    meta_blk = jnp.concatenate([jnp.zeros((PAD_FRONT, d), F32), meta_tokens.astype(F32)], axis=0)
    q, kv, xr, yr = _in_proj(x2d, w_in_b, PROJ_ROWS)
    qm, kvm, xrm, yrm = _in_proj(meta_blk, w_in_b, BLOCK)
    shp = lambda a: a.reshape(bsz, seq, a.shape[-1])

    attn_n = _attention(attn_sinks[0].astype(F32) * LOG2E, shp(q), shp(kv), kvm, _alibi_bias(),
                        row(g_attn[0]), bsz, nbx)
    lru_n = _rglru(shp(xr), shp(yr), xrm, yrm, conv_w[0].astype(F32), row(conv_b[0]),
                   _slab_gates(lru_wa[0]), _slab_gates(lru_wx[0]),
                   row(lru_ba[0]), row(lru_bx[0]), row(lru_lambda[0]), row(g_lru[0]), bsz, nbx)

    w_out_b = w_out[0].astype(BF16)
    gpad = SUBLANES - N_GROUPS
    w_rt = jnp.concatenate(
        [w_group[0].T, jnp.zeros((gpad, d), F32),
         jnp.transpose(w_router[0], (0, 2, 1)).reshape(N_EXPERTS, d)], axis=0).astype(F32)
    w_rt_hi = w_rt.astype(BF16)
    w_rt_lo = (w_rt - w_rt_hi.astype(F32)).astype(BF16)
    b_rt = jnp.concatenate([b_group[0], jnp.zeros((gpad,), F32),
                            b_router[0].reshape(-1)]).astype(F32).reshape(ROUTER_ROWS, 1)
    h1, hp, info = _out_proj(attn_n.reshape(n_tok, ATTN_WIDTH), lru_n.reshape(n_tok, LRU_WIDTH),
                             x2d, w_out_b[:ATTN_WIDTH], w_out_b[ATTN_WIDTH:], row(ln1_g[0]),
                             row(ln1_b[0]), w_rt_hi, w_rt_lo, b_rt, OUT_PROJ_ROWS)

    dest, cnt = _route(info, ROUTE_ROWS)
    n_slots = n_tok * TOP_K
    n_blocks = n_slots // MOE_BLOCK + N_EXPERTS
    cap = n_blocks * MOE_BLOCK
    nblk = (cnt[:, 0] + MOE_BLOCK - 1) // MOE_BLOCK
    bends = jnp.cumsum(nblk)
    bstart = (bends - nblk).astype(I32)
    nused = bends[-1:].astype(I32)
    windows = lambda v: v.reshape(SC_WORKERS, -1, SC_WINDOW)
    d0 = dest[0]
    d1 = dest[1]

    xs = _sc_scatter_rows(hp, windows(d0), windows(d1), cap)
    yb = _experts(bstart, nblk.astype(I32), nused, xs, w_gate[0], w_up[0], w_down[0])
    out = h1
    part_len = n_tok // COMBINE_PARTS
    for part in range(COMBINE_PARTS):
        tok = slice(part * part_len, (part + 1) * part_len)
        ys = _sc_gather_rows(yb, windows(jnp.concatenate([d0[tok], d1[tok]])))
        out = _combine(ys, out, info, row(ln2_g[0]), row(ln2_b[0]), COMBINE_ROWS,
                       part, COMBINE_PARTS)
    return out.reshape(bsz, seq, d)
```

```python
import jax
import jax.numpy as jnp
import numpy as np
from jax import lax
from jax.experimental import pallas as pl
from jax.experimental.pallas import tpu as pltpu
from jax.experimental.pallas import tpu_sc as plsc

F32 = jnp.float32
BF16 = jnp.bfloat16
U32 = jnp.uint32
I32 = jnp.int32

D_MODEL = 1024
N_META = 16
BLOCK = 128
PAD_FRONT = BLOCK - N_META
HEAD_DIM = 64
ATTN_WIDTH = 512
LRU_WIDTH = 512
N_Q_HEADS = 8
N_KV_HEADS = 2
KV_WIDTH = N_KV_HEADS * HEAD_DIM
LRU_BLOCKS = 8
CONV_W = 4
LRU_C = 8.0
IN_COLS = ATTN_WIDTH + 2 * KV_WIDTH + 2 * LRU_WIDTH
N_GROUPS = 4
EXPERTS_PER_GROUP = 8
N_EXPERTS = N_GROUPS * EXPERTS_PER_GROUP
TOP_K = 2
D_FF = 512
MOE_BLOCK = 256
ALPHA = 2.0 ** 0.25
EPS = 1e-5
NEG = -1e30
LOG2E = float(np.log2(np.e))
LANES = 128
SUBLANES = 8
PACKED = D_MODEL // 2

PROJ_ROWS = 512
OUT_PROJ_ROWS = 1024
ROUTE_ROWS = 512
COMBINE_ROWS = 512
COMBINE_PARTS = 4
X_RING = 4
VMEM_LIMIT = 48 * 1024 * 1024

SC_CORES = 2
SC_SUBCORES = 16
SC_WORKERS = SC_CORES * SC_SUBCORES
SC_WINDOW = 64


def _cparams(n_axes):
    return pltpu.CompilerParams(
        dimension_semantics=("arbitrary",) * n_axes, vmem_limit_bytes=VMEM_LIMIT)


def _in_proj_kernel(x_ref, w_ref, q_ref, kv_ref, xr_ref, yr_ref):
    proj = jnp.dot(x_ref[...].astype(BF16), w_ref[...], preferred_element_type=F32)
    o = 0
    for ref, width in ((q_ref, ATTN_WIDTH), (kv_ref, 2 * KV_WIDTH),
                       (xr_ref, LRU_WIDTH), (yr_ref, LRU_WIDTH)):
        ref[...] = proj[:, o:o + width].astype(ref.dtype)
        o += width


def _in_proj(x2d, w_bf16, rows):
    n = x2d.shape[0]
    widths = (ATTN_WIDTH, 2 * KV_WIDTH, LRU_WIDTH, LRU_WIDTH)
    return pl.pallas_call(
        _in_proj_kernel,
        grid=(n // rows,),
        in_specs=[pl.BlockSpec((rows, D_MODEL), lambda i: (i, 0)),
                  pl.BlockSpec((D_MODEL, IN_COLS), lambda i: (0, 0))],
        out_specs=[pl.BlockSpec((rows, w), lambda i: (i, 0)) for w in widths],
        out_shape=[jax.ShapeDtypeStruct((n, w), BF16) for w in widths],
        compiler_params=_cparams(1),
        name="in_proj",
    )(x2d, w_bf16)


def _attn_kernel(sinks_ref, q_ref, kv_ref, kvm_ref, bias_ref, g_ref, o_ref,
                 klo, khi, vlo, vhi):
    nbx = q_ref.shape[0] // BLOCK
    lo_lanes = lax.broadcasted_iota(I32, (BLOCK // 2, LANES), 1) < HEAD_DIM

    def layout_block(n, blk):
        rows = pl.ds(pl.multiple_of(n * BLOCK, BLOCK), BLOCK)
        as_bf16 = lambda words: pltpu.bitcast(words, BF16)
        for src, dst_lo, dst_hi in ((blk[:, :KV_WIDTH], klo, khi), (blk[:, KV_WIDTH:], vlo, vhi)):
            w = pltpu.bitcast(src, U32)
            r = pltpu.roll(w, HEAD_DIM, axis=1)
            zero = jnp.zeros_like(w)
            dst_lo[0, rows, :] = as_bf16(jnp.where(lo_lanes, w, zero))
            dst_hi[0, rows, :] = as_bf16(jnp.where(lo_lanes, zero, r))
            dst_lo[1, rows, :] = as_bf16(jnp.where(lo_lanes, r, zero))
            dst_hi[1, rows, :] = as_bf16(jnp.where(lo_lanes, zero, w))

    layout_block(0, kvm_ref[...])

    def layout_body(n, carry):
        layout_block(n + 1, kv_ref[pl.ds(pl.multiple_of(n * BLOCK, BLOCK), BLOCK), :])
        return carry

    lax.fori_loop(0, nbx, layout_body, 0, unroll=4)

    ones_lo = jnp.where(lax.broadcasted_iota(I32, (2 * BLOCK, LANES), 1) < HEAD_DIM,
                        1.0, 0.0).astype(BF16)
    ones_hi = (1.0 - ones_lo.astype(F32)).astype(BF16)
    top_rows = lax.broadcasted_iota(I32, (2 * BLOCK, 1), 0) < BLOCK
    lo_half = lax.broadcasted_iota(I32, (2 * BLOCK, LANES), 1) < HEAD_DIM

    def block(i, carry):
        q_rows = pl.ds(pl.multiple_of(i * BLOCK, BLOCK), BLOCK)
        win = pl.ds(pl.multiple_of(i * BLOCK, BLOCK), 2 * BLOCK)
        q = q_ref[q_rows, :]
        first = jnp.minimum(i, 1)
        outs = []
        for j in range(N_KV_HEADS):
            q2 = jnp.concatenate([q[:, (2 * j) * LANES:(2 * j + 1) * LANES],
                                  q[:, (2 * j + 1) * LANES:(2 * j + 2) * LANES]], axis=0)
            kc = jnp.concatenate([klo[j, win, :], khi[j, win, :]], axis=0)
            s = lax.dot_general(q2, kc, (((1,), (1,)), ((), ())), preferred_element_type=F32)
            s = s + bias_ref[first, j]
            ps, es = [], []
            for c in range(2):
                sink = jnp.where(top_rows, sinks_ref[4 * j + c], sinks_ref[4 * j + 2 + c])
                sc = s[:, c * 2 * BLOCK:(c + 1) * 2 * BLOCK]
                m = jnp.maximum(jnp.max(sc, axis=1, keepdims=True), sink)
                ps.append(jnp.exp2(sc - m).astype(BF16))
                es.append(jnp.exp2(sink - m))
            v_lo = jnp.concatenate([vlo[j, win, :], ones_lo], axis=1)
            v_hi = jnp.concatenate([vhi[j, win, :], ones_hi], axis=1)
            r = (jnp.dot(ps[0], v_lo, preferred_element_type=F32)
                 + jnp.dot(ps[1], v_hi, preferred_element_type=F32))
            den = r[:, LANES:] + jnp.where(lo_half, es[0], es[1])
            o2 = r[:, :LANES] * (1.0 / den)
            outs += [o2[:BLOCK], o2[BLOCK:]]
        out = jnp.concatenate(outs, axis=1)
        ms = jnp.mean(out * out, axis=1, keepdims=True)
        o_ref[q_rows, :] = (out * lax.rsqrt(ms + EPS) * g_ref[...]).astype(o_ref.dtype)
        return carry

    lax.fori_loop(0, nbx, block, 0, unroll=4)


def _attention(sinks, q, kv, kvm, bias, g_attn, bsz, nbx):
    seq = nbx * BLOCK
    const2 = lambda b: (0, 0)
    kv_scratch = pltpu.VMEM((N_KV_HEADS, seq + BLOCK, LANES), BF16)
    return pl.pallas_call(
        _attn_kernel,
        grid=(bsz,),
        in_specs=[pl.BlockSpec(memory_space=pltpu.SMEM),
                  pl.BlockSpec((None, seq, ATTN_WIDTH), lambda b: (b, 0, 0)),
                  pl.BlockSpec((None, seq, 2 * KV_WIDTH), lambda b: (b, 0, 0)),
                  pl.BlockSpec((BLOCK, 2 * KV_WIDTH), const2),
                  pl.BlockSpec((2, N_KV_HEADS, 2 * BLOCK, 4 * BLOCK), lambda b: (0, 0, 0, 0)),
                  pl.BlockSpec((1, ATTN_WIDTH), const2)],
        out_specs=pl.BlockSpec((None, seq, ATTN_WIDTH), lambda b: (b, 0, 0)),
        out_shape=jax.ShapeDtypeStruct((bsz, seq, ATTN_WIDTH), BF16),
        scratch_shapes=[kv_scratch, kv_scratch, kv_scratch, kv_scratch],
        compiler_params=_cparams(1),
        name="attention",
    )(sinks, q, kv, kvm, bias, g_attn)


def _sigmoid(v):
    return 0.5 * jnp.tanh(0.5 * v) + 0.5


def _gelu_tanh(y):
    c = float(np.sqrt(2.0 / np.pi))
    half = 0.5 * y
    return half + half * jnp.tanh(y * (c + (c * 0.044715) * (y * y)))


LRU_CHUNK = 44
LRU_SEG = SUBLANES * LRU_CHUNK
LRU_SLABS = LRU_WIDTH // LANES


def _lru_kernel(xr_ref, yr_ref, xrm_ref, yrm_ref, cw_ref, cb_ref, wa_ref, wx_ref, ba_ref,
                bx_ref, lam_ref, g_ref, o_ref, xp, yp, op, x_st, y_st, o_st, s_st, xtail, hcar):
    seq = xr_ref.shape[0]
    n_seg = (seq + BLOCK) // LRU_SEG
    xp[0:BLOCK, :] = xrm_ref[...]
    xp[BLOCK:, :] = xr_ref[...]
    yp[0:BLOCK, :] = yrm_ref[...]
    yp[BLOCK:, :] = yr_ref[...]
    xtail[...] = jnp.zeros_like(xtail)
    hcar[...] = jnp.zeros_like(hcar)
    lam = lam_ref[...]
    softplus_neg = jnp.maximum(-lam, 0.0) + jnp.log(1.0 + jnp.exp(-jnp.abs(lam)))
    sub = lax.broadcasted_iota(jnp.int32, (SUBLANES, LANES), 0)

    def strided(j):
        return pl.ds(j, SUBLANES, stride=LRU_CHUNK)

    def piece(v, j):
        return v[j * SUBLANES:(j + 1) * SUBLANES, :]

    def segment(k, has_padding):
        start = k * LRU_SEG
        if not isinstance(start, int):
            start = pl.multiple_of(start, 2 * SUBLANES)
        rows = pl.ds(start, LRU_SEG)
        x_nat = xp[rows, :].astype(F32)
        y_nat = yp[rows, :].astype(F32)
        for c in range(LRU_SLABS):
            x_st[c] = x_nat[:, c * LANES:(c + 1) * LANES]
            y_st[c] = y_nat[:, c * LANES:(c + 1) * LANES]
        first_row = k * LRU_SEG + LRU_CHUNK * sub
        sumsq = [jnp.zeros((SUBLANES, LANES), F32) for _ in range(LRU_CHUNK)]

        for c in range(LRU_SLABS):
            lanes = slice(c * LANES, (c + 1) * LANES)
            x = [x_st[c, strided(j), :] for j in range(LRU_CHUNK)]
            before = []
            for d in range(1, CONV_W):
                from_prev_chunk = pltpu.roll(x[LRU_CHUNK - d], 1, axis=0)
                before.append(jnp.where(sub == 0, xtail[d - 1:d, lanes], from_prev_chunk))
            for d in range(1, CONV_W):
                xtail[d - 1:d, lanes] = x[LRU_CHUNK - d][SUBLANES - 1:SUBLANES, :]

            def x_at(j):
                return x[j] if j >= 0 else before[-j - 1]

            taps = [cw_ref[t:t + 1, lanes] for t in range(CONV_W)]
            bias = cb_ref[:, lanes]
            xc = jnp.concatenate(
                [bias + sum(taps[t] * x_at(j - (CONV_W - 1) + t) for t in range(CONV_W))
                 for j in range(LRU_CHUNK)], axis=0)
            xcb = xc.astype(BF16)
            r = _sigmoid(jnp.dot(xcb, wa_ref[c], preferred_element_type=F32) + ba_ref[:, lanes])
            i = _sigmoid(jnp.dot(xcb, wx_ref[c], preferred_element_type=F32) + bx_ref[:, lanes])
            a = jnp.exp(-LRU_C * r * softplus_neg[:, lanes])
            z = 1.0 - a * a
            u = jnp.where(z > 0.0, z * lax.rsqrt(z), 0.0) * (i * xc)

            h = jnp.zeros((SUBLANES, LANES), F32)
            p = jnp.ones((SUBLANES, LANES), F32)
            hs, ps = [], []
            for j in range(LRU_CHUNK):
                aj = piece(a, j)
                uj = piece(u, j)
                if has_padding:
                    uj = jnp.where(first_row + j >= PAD_FRONT, uj, 0.0)
                h = aj * h + uj
                p = aj * p
                hs.append(h)
                ps.append(p)
            entry = [hcar[:, lanes]]
            for s in range(SUBLANES):
                entry.append(h[s:s + 1, :] + p[s:s + 1, :] * entry[s])
            hcar[:, lanes] = entry[SUBLANES]
            entry_rows = jnp.concatenate(entry[:SUBLANES], axis=0)

            for j in range(LRU_CHUNK):
                state = hs[j] + ps[j] * entry_rows
                out = state * _gelu_tanh(y_st[c, strided(j), :])
                sumsq[j] = sumsq[j] + out * out
                o_st[c, strided(j), :] = out

        for j in range(LRU_CHUNK):
            ms = jnp.sum(sumsq[j], axis=1, keepdims=True) * (1.0 / LRU_WIDTH)
            s_st[strided(j), :] = jnp.broadcast_to(lax.rsqrt(ms + EPS), (SUBLANES, LANES))
        scale = s_st[...]
        for c in range(LRU_SLABS):
            lanes = slice(c * LANES, (c + 1) * LANES)
            op[rows, lanes] = (o_st[c] * scale * g_ref[:, lanes]).astype(op.dtype)

    assert PAD_FRONT <= LRU_SEG
    segment(0, True)

    def later_segment(k, carry):
        segment(k, False)
        return carry

    lax.fori_loop(1, n_seg, later_segment, 0)
    o_ref[...] = op[BLOCK:, :]


def _rglru(xr, yr, xrm, yrm, cw, cb, wa, wx, ba, bx, lam, g_lru, bsz, nbx):
    seq = nbx * BLOCK
    assert (seq + BLOCK) % LRU_SEG == 0
    main = pl.BlockSpec((None, seq, LRU_WIDTH), lambda b: (b, 0, 0))
    const2 = lambda b: (0, 0)
    row_spec = pl.BlockSpec((1, LRU_WIDTH), const2)
    gate_spec = pl.BlockSpec((LRU_SLABS, LANES, LANES), lambda b: (0, 0, 0))
    padded = pltpu.VMEM((seq + BLOCK, LRU_WIDTH), BF16)
    slabs = pltpu.VMEM((LRU_SLABS, LRU_SEG, LANES), F32)
    return pl.pallas_call(
        _lru_kernel,
        grid=(bsz,),
        in_specs=[main, main,
                  pl.BlockSpec((BLOCK, LRU_WIDTH), const2),
                  pl.BlockSpec((BLOCK, LRU_WIDTH), const2),
                  pl.BlockSpec((CONV_W, LRU_WIDTH), const2),
                  row_spec, gate_spec, gate_spec,
                  row_spec, row_spec, row_spec, row_spec],
        out_specs=main,
        out_shape=jax.ShapeDtypeStruct((bsz, seq, LRU_WIDTH), BF16),
        scratch_shapes=[padded, padded, padded, slabs, slabs, slabs,
                        pltpu.VMEM((LRU_SEG, LANES), F32),
                        pltpu.VMEM((SUBLANES, LRU_WIDTH), F32),
                        pltpu.VMEM((1, LRU_WIDTH), F32)],
        compiler_params=_cparams(1),
        name="rglru",
    )(xr, yr, xrm, yrm, cw, cb, wa, wx, ba, bx, lam, g_lru)


def _pack_rows(v):
    bits = lax.bitcast_convert_type(v.astype(BF16).astype(F32), U32)
    return (bits[:, :PACKED] >> 16) | (bits[:, PACKED:] & jnp.uint32(0xFFFF0000))


def _unpack_rows(w):
    lo = lax.bitcast_convert_type(w << 16, F32)
    hi = lax.bitcast_convert_type(w & jnp.uint32(0xFFFF0000), F32)
    return lo, hi


def _layer_norm(z, g, b):
    mu = jnp.mean(z, axis=1, keepdims=True)
    zc = z - mu
    var = jnp.mean(zc * zc, axis=1, keepdims=True)
    return zc * lax.rsqrt(var + EPS) * g + b


def _out_proj_kernel(a_ref, l_ref, x_ref, wa_ref, wl_ref, g_ref, b_ref, wrt_hi_ref, wrt_lo_ref,
                     brt_ref, h_ref, hp_ref, info_ref):
    for c in range(h_ref.shape[0] // PROJ_ROWS):
        rows = slice(c * PROJ_ROWS, (c + 1) * PROJ_ROWS)
        _out_proj_rows(a_ref[rows, :], l_ref[rows, :], x_ref[rows, :], wa_ref, wl_ref, g_ref,
                       b_ref, wrt_hi_ref, wrt_lo_ref, brt_ref,
                       h_ref.at[rows, :], hp_ref.at[rows, :], info_ref.at[:, rows])


def _out_proj_rows(a, l, x, wa_ref, wl_ref, g_ref, b_ref, wrt_hi_ref, wrt_lo_ref, brt_ref,
                   h_ref, hp_ref, info_ref):
    mix = jnp.dot(a, wa_ref[...], preferred_element_type=F32)
    mix = mix + jnp.dot(l, wl_ref[...], preferred_element_type=F32)
    h = _layer_norm(ALPHA * x + mix, g_ref[...], b_ref[...])
    h_ref[...] = h
    hp_ref[...] = _pack_rows(h)

    h_hi = h.astype(BF16)
    h_lo = (h - h_hi.astype(F32)).astype(BF16)
    nt = (((1,), (1,)), ((), ()))
    lg = (lax.dot_general(wrt_hi_ref[...], h_hi, nt, preferred_element_type=F32)
          + lax.dot_general(wrt_lo_ref[...], h_hi, nt, preferred_element_type=F32)
          + lax.dot_general(wrt_hi_ref[...], h_lo, nt, preferred_element_type=F32)) + brt_ref[...]
    tile_shape = (SUBLANES, h.shape[0])
    sub = lax.broadcasted_iota(I32, tile_shape, 0)
    ninf = -jnp.inf
    t0 = lg[0:SUBLANES]
    gl = jnp.where(sub < N_GROUPS, t0, ninf)
    gmax = jnp.max(gl, axis=0, keepdims=True)
    g_idx = jnp.min(jnp.where(gl == gmax, sub, SUBLANES), axis=0, keepdims=True)
    g_w = 1.0 / jnp.sum(jnp.where(sub < N_GROUPS, jnp.exp(t0 - gmax), 0.0),
                        axis=0, keepdims=True)
    el = lg[SUBLANES:2 * SUBLANES]
    for g in range(1, N_GROUPS):
        el = jnp.where(g_idx == g, lg[(g + 1) * SUBLANES:(g + 2) * SUBLANES], el)
    v1 = jnp.max(el, axis=0, keepdims=True)
    i1 = jnp.min(jnp.where(el == v1, sub, SUBLANES), axis=0, keepdims=True)
    el2 = jnp.where(sub == i1, ninf, el)
    v2 = jnp.max(el2, axis=0, keepdims=True)
    i2 = jnp.min(jnp.where(el2 == v2, sub, SUBLANES), axis=0, keepdims=True)
    t = jnp.exp(v2 - v1)
    w1 = 1.0 / (1.0 + t)
    w2 = t * w1
    e_base = g_idx * EXPERTS_PER_GROUP
    info_ref[...] = jnp.where(sub == 0, (e_base + i1).astype(F32),
                              jnp.where(sub == 1, (e_base + i2).astype(F32),
                                        jnp.where(sub == 2, g_w * w1,
                                                  jnp.where(sub == 3, g_w * w2, 0.0))))


ROUTER_ROWS = (N_GROUPS + 1) * SUBLANES


def _out_proj(attn_n, lru_n, x2d, wo_a, wo_l, ln_g, ln_b, w_rt_hi, w_rt_lo, b_rt, rows):
    n = x2d.shape[0]
    const = lambda i: (0, 0)
    tile = lambda w: pl.BlockSpec((rows, w), lambda i: (i, 0))
    return pl.pallas_call(
        _out_proj_kernel,
        grid=(n // rows,),
        in_specs=[tile(ATTN_WIDTH), tile(LRU_WIDTH), tile(D_MODEL),
                  pl.BlockSpec((ATTN_WIDTH, D_MODEL), const),
                  pl.BlockSpec((LRU_WIDTH, D_MODEL), const),
                  pl.BlockSpec((1, D_MODEL), const),
                  pl.BlockSpec((1, D_MODEL), const),
                  pl.BlockSpec((ROUTER_ROWS, D_MODEL), const),
                  pl.BlockSpec((ROUTER_ROWS, D_MODEL), const),
                  pl.BlockSpec((ROUTER_ROWS, 1), const)],
        out_specs=[tile(D_MODEL), tile(PACKED),
                   pl.BlockSpec((SUBLANES, rows), lambda i: (0, i))],
        out_shape=[jax.ShapeDtypeStruct((n, D_MODEL), F32),
                   jax.ShapeDtypeStruct((n, PACKED), U32),
                   jax.ShapeDtypeStruct((SUBLANES, n), F32)],
        compiler_params=_cparams(1),
        name="out_proj",
    )(attn_n, lru_n, x2d, wo_a, wo_l, ln_g, ln_b, w_rt_hi, w_rt_lo, b_rt)


def _route_kernel(info_ref, tri_ref, dest_ref, cnt_ref, counts, carry, pstart):
    p = pl.program_id(0)
    t = pl.program_id(1)
    info = info_ref[...]
    shape = (N_EXPERTS, info.shape[1])
    expert = lax.broadcasted_iota(I32, shape, 0)
    oh1 = (expert == info[0:1, :].astype(I32)).astype(F32)
    oh2 = (expert == info[1:2, :].astype(I32)).astype(F32)
    both = oh1 + oh2
    tile_counts = jnp.sum(both, axis=1, keepdims=True)

    @pl.when((p == 0) & (t == 0))
    def _():
        counts[...] = jnp.zeros_like(counts)

    @pl.when(p == 0)
    def _():
        counts[...] += tile_counts

    @pl.when((p == 1) & (t == 0))
    def _():
        c = jnp.broadcast_to(counts[...], (N_EXPERTS, LANES)).astype(I32)
        padded = ((c + (MOE_BLOCK - 1)) // MOE_BLOCK) * MOE_BLOCK
        e = lax.broadcasted_iota(I32, (N_EXPERTS, LANES), 0)
        scan = padded
        for d in (1, 2, 4, 8, 16):
            scan = scan + jnp.where(e >= d, pltpu.roll(scan, d, axis=0), 0)
        pstart[...] = (scan - padded)[:, 0:1].astype(F32)
        carry[...] = jnp.zeros_like(carry)
        cnt_ref[...] = c

    @pl.when(p == 1)
    def _():
        before = jnp.dot(both.astype(BF16), tri_ref[...], preferred_element_type=F32)
        row_of = before + (carry[...] + pstart[...])
        r1 = jnp.sum(oh1 * row_of, axis=0, keepdims=True)
        r2 = jnp.sum(oh2 * row_of, axis=0, keepdims=True)
        sub = lax.broadcasted_iota(I32, dest_ref.shape, 0)
        dest_ref[...] = jnp.where(sub == 0, r1, jnp.where(sub == 1, r2, 0.0)).astype(I32)
        carry[...] += tile_counts


def _route(info_t, cols):
    n = info_t.shape[1]
    tri = jnp.asarray(np.triu(np.ones((cols, cols), np.float32), 1), BF16)
    return pl.pallas_call(
        _route_kernel,
        grid=(2, n // cols),
        in_specs=[pl.BlockSpec((SUBLANES, cols), lambda p, t: (0, t)),
                  pl.BlockSpec((cols, cols), lambda p, t: (0, 0))],
        out_specs=[pl.BlockSpec((SUBLANES, cols), lambda p, t: (0, t * p)),
                   pl.BlockSpec((N_EXPERTS, LANES), lambda p, t: (0, 0))],
        out_shape=[jax.ShapeDtypeStruct((SUBLANES, n), I32),
                   jax.ShapeDtypeStruct((N_EXPERTS, LANES), I32)],
        scratch_shapes=[pltpu.VMEM((N_EXPERTS, 1), F32), pltpu.VMEM((N_EXPERTS, 1), F32),
                        pltpu.VMEM((N_EXPERTS, 1), F32)],
        compiler_params=_cparams(2),
        name="route",
    )(info_t, tri)


def _sc_mesh():
    return plsc.VectorSubcoreMesh(core_axis_name="core", subcore_axis_name="subcore")


def _sc_worker_id():
    return lax.axis_index("subcore") * SC_CORES + lax.axis_index("core")


def _sc_scatter_rows(rows, d0, d1, cap):
    n, width = rows.shape
    per_worker = n // SC_WORKERS
    n_win = per_worker // SC_WINDOW

    def body(x_hbm, d0_hbm, d1_hbm, o_hbm, i0_v, i1_v, rows_v, rsem, sem0, sem1):
        wid = _sc_worker_id()
        pltpu.sync_copy(d0_hbm.at[wid], i0_v)
        pltpu.sync_copy(d1_hbm.at[wid], i1_v)

        def read(j):
            src = x_hbm.at[pl.ds(wid * per_worker + j * SC_WINDOW, SC_WINDOW)]
            return pltpu.make_async_copy(src, rows_v.at[j % 2], rsem.at[j % 2])

        def scatters(j):
            return (pltpu.make_async_copy(rows_v.at[j % 2], o_hbm.at[i0_v.at[j]], sem0.at[j % 2]),
                    pltpu.make_async_copy(rows_v.at[j % 2], o_hbm.at[i1_v.at[j]], sem1.at[j % 2]))

        read(0).start()
        for j in range(n_win):
            if j + 1 < n_win:
                if j >= 1:
                    for cp in scatters(j - 1):
                        cp.wait()
                read(j + 1).start()
            read(j).wait()
            for cp in scatters(j):
                cp.start()
        for j in range(max(n_win - 2, 0), n_win):
            for cp in scatters(j):
                cp.wait()

    return pl.kernel(
        body,
        out_type=jax.ShapeDtypeStruct((cap, width), rows.dtype),
        mesh=_sc_mesh(),
        scratch_types=[pltpu.VMEM((n_win, SC_WINDOW), I32), pltpu.VMEM((n_win, SC_WINDOW), I32),
                       pltpu.VMEM((2, SC_WINDOW, width), rows.dtype),
                       pltpu.SemaphoreType.DMA((2,)), pltpu.SemaphoreType.DMA((2,)),
                       pltpu.SemaphoreType.DMA((2,))],
        name="dispatch",
    )(rows, d0, d1)


def _sc_gather_rows(table, idx):
    width = table.shape[1]
    n_win = idx.shape[1]
    per_worker = n_win * SC_WINDOW

    def body(y_hbm, i_hbm, o_hbm, i_v, rows_v, gsem, wsem):
        wid = _sc_worker_id()
        pltpu.sync_copy(i_hbm.at[wid], i_v)

        def gather(j):
            return pltpu.make_async_copy(y_hbm.at[i_v.at[j]], rows_v.at[j % 2], gsem.at[j % 2])

        def write(j):
            dst = o_hbm.at[pl.ds(wid * per_worker + j * SC_WINDOW, SC_WINDOW)]
            return pltpu.make_async_copy(rows_v.at[j % 2], dst, wsem.at[j % 2])

        gather(0).start()
        for j in range(n_win):
            if j + 1 < n_win:
                if j >= 1:
                    write(j - 1).wait()
                gather(j + 1).start()
            gather(j).wait()
            write(j).start()
        for j in range(max(n_win - 2, 0), n_win):
            write(j).wait()

    return pl.kernel(
        body,
        out_type=jax.ShapeDtypeStruct((SC_WORKERS * per_worker, width), table.dtype),
        mesh=_sc_mesh(),
        scratch_types=[pltpu.VMEM((n_win, SC_WINDOW), I32),
                       pltpu.VMEM((2, SC_WINDOW, width), table.dtype),
                       pltpu.SemaphoreType.DMA((2,)), pltpu.SemaphoreType.DMA((2,))],
        name="collect",
    )(table, idx)


def _expert_kernel(bstart_ref, nblk_ref, nused_ref, xs_hbm, wg_ref, wu_ref, wd_ref, yb_hbm,
                   xbuf, ybuf, zbuf, xsem, ysem, zsem, wg_b, wu_b, wd_b):
    e = pl.program_id(0)
    nused = nused_ref[0]
    n_blocks = yb_hbm.shape[0] // MOE_BLOCK

    def rows(b):
        return pl.ds(pl.multiple_of(b * MOE_BLOCK, MOE_BLOCK), MOE_BLOCK)

    def x_copy(b):
        slot = b % X_RING
        return pltpu.make_async_copy(xs_hbm.at[rows(b)], xbuf.at[slot], xsem.at[slot])

    def y_copy(b, slot):
        return pltpu.make_async_copy(ybuf.at[slot], yb_hbm.at[rows(b)], ysem.at[slot])

    @pl.when(e == 0)
    def _():
        for b in range(X_RING - 1):
            @pl.when(b < nused)
            def _():
                x_copy(b).start()

    wg_b[...] = wg_ref[...].astype(BF16)
    wu_b[...] = wu_ref[...].astype(BF16)
    wd_b[...] = wd_ref[...].astype(BF16)

    def block(b, carry):
        slot = b % 2
        x_copy(b).wait()

        @pl.when(b + (X_RING - 1) < nused)
        def _():
            x_copy(b + (X_RING - 1)).start()

        @pl.when(b >= 2)
        def _():
            y_copy(b - 2, slot).wait()

        lo, hi = _unpack_rows(xbuf[b % X_RING])
        lo = lo.astype(BF16)
        hi = hi.astype(BF16)
        g = (jnp.dot(lo, wg_b[0:PACKED, :], preferred_element_type=F32)
             + jnp.dot(hi, wg_b[PACKED:, :], preferred_element_type=F32))
        u = (jnp.dot(lo, wu_b[0:PACKED, :], preferred_element_type=F32)
             + jnp.dot(hi, wu_b[PACKED:, :], preferred_element_type=F32))
        mid = (g * _sigmoid(g) * u).astype(BF16)
        ybuf[slot] = _pack_rows(jnp.dot(mid, wd_b[...], preferred_element_type=F32))
        y_copy(b, slot).start()
        return carry

    b0 = bstart_ref[e]
    lax.fori_loop(b0, b0 + nblk_ref[e], block, 0)

    @pl.when(e == pl.num_programs(0) - 1)
    def _():
        for back in (2, 1):
            @pl.when(nused >= back)
            def _():
                y_copy(nused - back, (nused - back) % 2).wait()

        zbuf[...] = jnp.zeros_like(zbuf)

        def z_copy(b):
            return pltpu.make_async_copy(zbuf, yb_hbm.at[rows(b)], zsem.at[0])

        def z_start(b, carry):
            z_copy(b).start()
            return carry

        def z_wait(b, carry):
            z_copy(b).wait()
            return carry

        lax.fori_loop(nused, n_blocks, z_start, 0)
        lax.fori_loop(nused, n_blocks, z_wait, 0)


def _experts(bstart, nblk, nused, xs, w_gate, w_up, w_down):
    cap = xs.shape[0]
    w_idx = lambda e, bs, nb, nu: (e, 0, 0)
    grid_spec = pltpu.PrefetchScalarGridSpec(
        num_scalar_prefetch=3,
        grid=(N_EXPERTS,),
        in_specs=[pl.BlockSpec(memory_space=pl.ANY),
                  pl.BlockSpec((None, D_MODEL, D_FF), w_idx),
                  pl.BlockSpec((None, D_MODEL, D_FF), w_idx),
                  pl.BlockSpec((None, D_FF, D_MODEL), w_idx)],
        out_specs=pl.BlockSpec(memory_space=pl.ANY),
        scratch_shapes=[pltpu.VMEM((X_RING, MOE_BLOCK, PACKED), U32),
                        pltpu.VMEM((2, MOE_BLOCK, PACKED), U32),
                        pltpu.VMEM((MOE_BLOCK, PACKED), U32),
                        pltpu.SemaphoreType.DMA((X_RING,)),
                        pltpu.SemaphoreType.DMA((2,)),
                        pltpu.SemaphoreType.DMA((1,)),
                        pltpu.VMEM((D_MODEL, D_FF), BF16),
                        pltpu.VMEM((D_MODEL, D_FF), BF16),
                        pltpu.VMEM((D_FF, D_MODEL), BF16)])
    return pl.pallas_call(
        _expert_kernel,
        grid_spec=grid_spec,
        out_shape=jax.ShapeDtypeStruct((cap, PACKED), U32),
        compiler_params=_cparams(1),
        name="experts",
    )(bstart, nblk, nused, xs, w_gate, w_up, w_down)


def _combine_kernel(y0_ref, y1_ref, h_ref, info_ref, g_ref, b_ref, o_ref):
    info = info_ref[...].T
    g0 = info[:, 2:3]
    g1 = info[:, 3:4]
    lo0, hi0 = _unpack_rows(y0_ref[...])
    lo1, hi1 = _unpack_rows(y1_ref[...])
    y = jnp.concatenate([g0 * lo0 + g1 * lo1, g0 * hi0 + g1 * hi1], axis=1)
    o_ref[...] = _layer_norm(ALPHA * h_ref[...] + y, g_ref[...], b_ref[...])


def _combine(ys, h, info, ln_g, ln_b, rows, part, n_parts):
    n = h.shape[0]
    steps = n // n_parts // rows
    off = part * steps
    const = lambda i: (0, 0)
    return pl.pallas_call(
        _combine_kernel,
        grid=(steps,),
        in_specs=[pl.BlockSpec((rows, PACKED), lambda i: (i, 0)),
                  pl.BlockSpec((rows, PACKED), lambda i: (i + steps, 0)),
                  pl.BlockSpec((rows, D_MODEL), lambda i: (i + off, 0)),
                  pl.BlockSpec((SUBLANES, rows), lambda i: (0, i + off)),
                  pl.BlockSpec((1, D_MODEL), const),
                  pl.BlockSpec((1, D_MODEL), const)],
        out_specs=pl.BlockSpec((rows, D_MODEL), lambda i: (i + off, 0)),
        out_shape=jax.ShapeDtypeStruct((n, D_MODEL), F32),
        input_output_aliases={2: 0},
        compiler_params=_cparams(1),
        name="combine",
    )(ys, ys, h, info, ln_g, ln_b)


def _alibi_bias():
    qi = np.arange(BLOCK)[:, None]
    kj = np.arange(2 * BLOCK)[None, :]
    dist = qi - kj + BLOCK
    band = (dist >= 0) & (dist < BLOCK)
    slopes = np.exp2(-8.0 * np.arange(1, N_Q_HEADS + 1, dtype=np.float32) / N_Q_HEADS)
    bias = np.where(band[None], -slopes[:, None, None] * dist[None].astype(np.float32), NEG)
    bias = bias * LOG2E
    first = np.where((kj >= PAD_FRONT)[None], bias, NEG)
    out = np.empty((2, N_KV_HEADS, 2 * BLOCK, 4 * BLOCK), np.float32)
    for v, per_head in enumerate((first, bias)):
        for j in range(N_KV_HEADS):
            out[v, j] = np.block([[per_head[4 * j], per_head[4 * j + 1]],
                                  [per_head[4 * j + 2], per_head[4 * j + 3]]])
    return jnp.asarray(out, F32)


def _slab_gates(w):
    nb, c, _ = w.shape
    per = LANES // c
    w = w.reshape(nb // per, per, c, c)
    eye = jnp.eye(per, dtype=w.dtype)
    return jnp.einsum('spcd,pq->spcqd', w, eye).reshape(nb // per, LANES, LANES).astype(BF16)


def kernel(x, meta_tokens, w_in, conv_w, conv_b, lru_wa, lru_ba, lru_wx, lru_bx, lru_lambda,
           attn_sinks, g_attn, g_lru, w_out, ln1_g, ln1_b, w_group, b_group, w_router,
           b_router, w_gate, w_up, w_down, ln2_g, ln2_b):
    bsz, seq, d = x.shape
    nbx = seq // BLOCK
    n_tok = bsz * seq
    x2d = x.reshape(n_tok, d)
    row = lambda v: v.reshape(1, -1).astype(F32)

    q_scale = jnp.concatenate([jnp.full((ATTN_WIDTH,), LOG2E * HEAD_DIM ** -0.5, F32),
                               jnp.ones((IN_COLS - ATTN_WIDTH,), F32)])
    w_in_b = (w_in[0] * q_scale).astype(BF16)
    meta_blk = jnp.concatenate([jnp.zeros((PAD_FRONT, d), F32), meta_tokens.astype(F32)], axis=0)
    q, kv, xr, yr = _in_proj(x2d, w_in_b, PROJ_ROWS)
    qm, kvm, xrm, yrm = _in_proj(meta_blk, w_in_b, BLOCK)
    shp = lambda a: a.reshape(bsz, seq, a.shape[-1])

    attn_n = _attention(attn_sinks[0].astype(F32) * LOG2E, shp(q), shp(kv), kvm, _alibi_bias(),
                        row(g_attn[0]), bsz, nbx)
    lru_n = _rglru(shp(xr), shp(yr), xrm, yrm, conv_w[0].astype(F32), row(conv_b[0]),
                   _slab_gates(lru_wa[0]), _slab_gates(lru_wx[0]),
                   row(lru_ba[0]), row(lru_bx[0]), row(lru_lambda[0]), row(g_lru[0]), bsz, nbx)

    w_out_b = w_out[0].astype(BF16)
    gpad = SUBLANES - N_GROUPS
    w_rt = jnp.concatenate(
        [w_group[0].T, jnp.zeros((gpad, d), F32),
         jnp.transpose(w_router[0], (0, 2, 1)).reshape(N_EXPERTS, d)], axis=0).astype(F32)
    w_rt_hi = w_rt.astype(BF16)
    w_rt_lo = (w_rt - w_rt_hi.astype(F32)).astype(BF16)
    b_rt = jnp.concatenate([b_group[0], jnp.zeros((gpad,), F32),
                            b_router[0].reshape(-1)]).astype(F32).reshape(ROUTER_ROWS, 1)
    h1, hp, info = _out_proj(attn_n.reshape(n_tok, ATTN_WIDTH), lru_n.reshape(n_tok, LRU_WIDTH),
                             x2d, w_out_b[:ATTN_WIDTH], w_out_b[ATTN_WIDTH:], row(ln1_g[0]),
                             row(ln1_b[0]), w_rt_hi, w_rt_lo, b_rt, OUT_PROJ_ROWS)

    dest, cnt = _route(info, ROUTE_ROWS)
    n_slots = n_tok * TOP_K
    n_blocks = n_slots // MOE_BLOCK + N_EXPERTS
    cap = n_blocks * MOE_BLOCK
    nblk = (cnt[:, 0] + MOE_BLOCK - 1) // MOE_BLOCK
    bends = jnp.cumsum(nblk)
    bstart = (bends - nblk).astype(I32)
    nused = bends[-1:].astype(I32)
    windows = lambda v: v.reshape(SC_WORKERS, -1, SC_WINDOW)
    d0 = dest[0]
    d1 = dest[1]

    xs = _sc_scatter_rows(hp, windows(d0), windows(d1), cap)
    yb = _experts(bstart, nblk.astype(I32), nused, xs, w_gate[0], w_up[0], w_down[0])
    out = h1
    part_len = n_tok // COMBINE_PARTS
    for part in range(COMBINE_PARTS):
        tok = slice(part * part_len, (part + 1) * part_len)
        ys = _sc_gather_rows(yb, windows(jnp.concatenate([d0[tok], d1[tok]])))
        out = _combine(ys, out, info, row(ln2_g[0]), row(ln2_b[0]), COMBINE_ROWS,
                       part, COMBINE_PARTS)
    return out.reshape(bsz, seq, d)
```

```python
import jax
import jax.numpy as jnp
import numpy as np
from jax import lax
from jax.experimental import pallas as pl
from jax.experimental.pallas import tpu as pltpu
from jax.experimental.pallas import tpu_sc as plsc

F32 = jnp.float32
BF16 = jnp.bfloat16
U32 = jnp.uint32
I32 = jnp.int32

D_MODEL = 1024
N_META = 16
BLOCK = 128
PAD_FRONT = BLOCK - N_META
HEAD_DIM = 64
ATTN_WIDTH = 512
LRU_WIDTH = 512
N_Q_HEADS = 8
N_KV_HEADS = 2
KV_WIDTH = N_KV_HEADS * HEAD_DIM
LRU_BLOCKS = 8
CONV_W = 4
LRU_C = 8.0
IN_COLS = ATTN_WIDTH + 2 * KV_WIDTH + 2 * LRU_WIDTH
N_GROUPS = 4
EXPERTS_PER_GROUP = 8
N_EXPERTS = N_GROUPS * EXPERTS_PER_GROUP
TOP_K = 2
D_FF = 512
MOE_BLOCK = 256
ALPHA = 2.0 ** 0.25
EPS = 1e-5
NEG = -1e30
LOG2E = float(np.log2(np.e))
LANES = 128
SUBLANES = 8
PACKED = D_MODEL // 2

PROJ_ROWS = 512
OUT_PROJ_ROWS = 1024
ROUTE_ROWS = 512
COMBINE_ROWS = 512
COMBINE_PARTS = 4
X_RING = 4
VMEM_LIMIT = 48 * 1024 * 1024

SC_CORES = 2
SC_SUBCORES = 16
SC_WORKERS = SC_CORES * SC_SUBCORES
SC_WINDOW = 64


def _cparams(n_axes):
    return pltpu.CompilerParams(
        dimension_semantics=("arbitrary",) * n_axes, vmem_limit_bytes=VMEM_LIMIT)


def _in_proj_kernel(x_ref, w_ref, q_ref, kv_ref, xr_ref, yr_ref):
    proj = jnp.dot(x_ref[...].astype(BF16), w_ref[...], preferred_element_type=F32)
    o = 0
    for ref, width in ((q_ref, ATTN_WIDTH), (kv_ref, 2 * KV_WIDTH),
                       (xr_ref, LRU_WIDTH), (yr_ref, LRU_WIDTH)):
        ref[...] = proj[:, o:o + width].astype(ref.dtype)
        o += width


def _in_proj(x2d, w_bf16, rows):
    n = x2d.shape[0]
    widths = (ATTN_WIDTH, 2 * KV_WIDTH, LRU_WIDTH, LRU_WIDTH)
    return pl.pallas_call(
        _in_proj_kernel,
        grid=(n // rows,),
        in_specs=[pl.BlockSpec((rows, D_MODEL), lambda i: (i, 0)),
                  pl.BlockSpec((D_MODEL, IN_COLS), lambda i: (0, 0))],
        out_specs=[pl.BlockSpec((rows, w), lambda i: (i, 0)) for w in widths],
        out_shape=[jax.ShapeDtypeStruct((n, w), BF16) for w in widths],
        compiler_params=_cparams(1),
        name="in_proj",
    )(x2d, w_bf16)


def _attn_kernel(sinks_ref, q_ref, kv_ref, kvm_ref, bias_ref, g_ref, o_ref,
                 klo, khi, vlo, vhi):
    nbx = q_ref.shape[0] // BLOCK
    lo_lanes = lax.broadcasted_iota(I32, (BLOCK // 2, LANES), 1) < HEAD_DIM

    def layout_block(n, blk):
        rows = pl.ds(pl.multiple_of(n * BLOCK, BLOCK), BLOCK)
        as_bf16 = lambda words: pltpu.bitcast(words, BF16)
        for src, dst_lo, dst_hi in ((blk[:, :KV_WIDTH], klo, khi), (blk[:, KV_WIDTH:], vlo, vhi)):
            w = pltpu.bitcast(src, U32)
            r = pltpu.roll(w, HEAD_DIM, axis=1)
            zero = jnp.zeros_like(w)
            dst_lo[0, rows, :] = as_bf16(jnp.where(lo_lanes, w, zero))
            dst_hi[0, rows, :] = as_bf16(jnp.where(lo_lanes, zero, r))
            dst_lo[1, rows, :] = as_bf16(jnp.where(lo_lanes, r, zero))
            dst_hi[1, rows, :] = as_bf16(jnp.where(lo_lanes, zero, w))

    layout_block(0, kvm_ref[...])

    def layout_body(n, carry):
        layout_block(n + 1, kv_ref[pl.ds(pl.multiple_of(n * BLOCK, BLOCK), BLOCK), :])
        return carry

    lax.fori_loop(0, nbx, layout_body, 0, unroll=4)

    ones_lo = jnp.where(lax.broadcasted_iota(I32, (2 * BLOCK, LANES), 1) < HEAD_DIM,
                        1.0, 0.0).astype(BF16)
    ones_hi = (1.0 - ones_lo.astype(F32)).astype(BF16)
    top_rows = lax.broadcasted_iota(I32, (2 * BLOCK, 1), 0) < BLOCK
    lo_half = lax.broadcasted_iota(I32, (2 * BLOCK, LANES), 1) < HEAD_DIM

    def block(i, carry):
        q_rows = pl.ds(pl.multiple_of(i * BLOCK, BLOCK), BLOCK)
        win = pl.ds(pl.multiple_of(i * BLOCK, BLOCK), 2 * BLOCK)
        q = q_ref[q_rows, :]
        first = jnp.minimum(i, 1)
        outs = []
        for j in range(N_KV_HEADS):
            q2 = jnp.concatenate([q[:, (2 * j) * LANES:(2 * j + 1) * LANES],
                                  q[:, (2 * j + 1) * LANES:(2 * j + 2) * LANES]], axis=0)
            kc = jnp.concatenate([klo[j, win, :], khi[j, win, :]], axis=0)
            s = lax.dot_general(q2, kc, (((1,), (1,)), ((), ())), preferred_element_type=F32)
            s = s + bias_ref[first, j]
            ps, es = [], []
            for c in range(2):
                sink = jnp.where(top_rows, sinks_ref[4 * j + c], sinks_ref[4 * j + 2 + c])
                sc = s[:, c * 2 * BLOCK:(c + 1) * 2 * BLOCK]
                m = jnp.maximum(jnp.max(sc, axis=1, keepdims=True), sink)
                ps.append(jnp.exp2(sc - m).astype(BF16))
                es.append(jnp.exp2(sink - m))
            v_lo = jnp.concatenate([vlo[j, win, :], ones_lo], axis=1)
            v_hi = jnp.concatenate([vhi[j, win, :], ones_hi], axis=1)
            r = (jnp.dot(ps[0], v_lo, preferred_element_type=F32)
                 + jnp.dot(ps[1], v_hi, preferred_element_type=F32))
            den = r[:, LANES:] + jnp.where(lo_half, es[0], es[1])
            o2 = r[:, :LANES] * (1.0 / den)
            outs += [o2[:BLOCK], o2[BLOCK:]]
        out = jnp.concatenate(outs, axis=1)
        ms = jnp.mean(out * out, axis=1, keepdims=True)
        o_ref[q_rows, :] = (out * lax.rsqrt(ms + EPS) * g_ref[...]).astype(o_ref.dtype)
        return carry

    lax.fori_loop(0, nbx, block, 0, unroll=4)


def _attention(sinks, q, kv, kvm, bias, g_attn, bsz, nbx):
    seq = nbx * BLOCK
    const2 = lambda b: (0, 0)
    kv_scratch = pltpu.VMEM((N_KV_HEADS, seq + BLOCK, LANES), BF16)
    return pl.pallas_call(
        _attn_kernel,
        grid=(bsz,),
        in_specs=[pl.BlockSpec(memory_space=pltpu.SMEM),
                  pl.BlockSpec((None, seq, ATTN_WIDTH), lambda b: (b, 0, 0)),
                  pl.BlockSpec((None, seq, 2 * KV_WIDTH), lambda b: (b, 0, 0)),
                  pl.BlockSpec((BLOCK, 2 * KV_WIDTH), const2),
                  pl.BlockSpec((2, N_KV_HEADS, 2 * BLOCK, 4 * BLOCK), lambda b: (0, 0, 0, 0)),
                  pl.BlockSpec((1, ATTN_WIDTH), const2)],
        out_specs=pl.BlockSpec((None, seq, ATTN_WIDTH), lambda b: (b, 0, 0)),
        out_shape=jax.ShapeDtypeStruct((bsz, seq, ATTN_WIDTH), BF16),
        scratch_shapes=[kv_scratch, kv_scratch, kv_scratch, kv_scratch],
        compiler_params=_cparams(1),
        name="attention",
    )(sinks, q, kv, kvm, bias, g_attn)


def _sigmoid(v):
    return 0.5 * jnp.tanh(0.5 * v) + 0.5


def _gelu_tanh(y):
    c = float(np.sqrt(2.0 / np.pi))
    half = 0.5 * y
    return half + half * jnp.tanh(y * (c + (c * 0.044715) * (y * y)))


LRU_CHUNK = 44
LRU_SEG = SUBLANES * LRU_CHUNK
LRU_SLABS = LRU_WIDTH // LANES


def _lru_kernel(xr_ref, yr_ref, xrm_ref, yrm_ref, cw_ref, cb_ref, wa_ref, wx_ref, ba_ref,
                bx_ref, lam_ref, g_ref, o_ref, xp, yp, op, x_st, y_st, o_st, s_st, xtail, hcar):
    seq = xr_ref.shape[0]
    n_seg = (seq + BLOCK) // LRU_SEG
    xp[0:BLOCK, :] = xrm_ref[...]
    xp[BLOCK:, :] = xr_ref[...]
    yp[0:BLOCK, :] = yrm_ref[...]
    yp[BLOCK:, :] = yr_ref[...]
    xtail[...] = jnp.zeros_like(xtail)
    hcar[...] = jnp.zeros_like(hcar)
    lam = lam_ref[...]
    softplus_neg = jnp.maximum(-lam, 0.0) + jnp.log(1.0 + jnp.exp(-jnp.abs(lam)))
    sub = lax.broadcasted_iota(jnp.int32, (SUBLANES, LANES), 0)

    def strided(j):
        return pl.ds(j, SUBLANES, stride=LRU_CHUNK)

    def piece(v, j):
        return v[j * SUBLANES:(j + 1) * SUBLANES, :]

    def segment(k, has_padding):
        start = k * LRU_SEG
        if not isinstance(start, int):
            start = pl.multiple_of(start, 2 * SUBLANES)
        rows = pl.ds(start, LRU_SEG)
        x_nat = xp[rows, :].astype(F32)
        y_nat = yp[rows, :].astype(F32)
        for c in range(LRU_SLABS):
            x_st[c] = x_nat[:, c * LANES:(c + 1) * LANES]
            y_st[c] = y_nat[:, c * LANES:(c + 1) * LANES]
        first_row = k * LRU_SEG + LRU_CHUNK * sub
        sumsq = [jnp.zeros((SUBLANES, LANES), F32) for _ in range(LRU_CHUNK)]

        for c in range(LRU_SLABS):
            lanes = slice(c * LANES, (c + 1) * LANES)
            x = [x_st[c, strided(j), :] for j in range(LRU_CHUNK)]
            before = []
            for d in range(1, CONV_W):
                from_prev_chunk = pltpu.roll(x[LRU_CHUNK - d], 1, axis=0)
                before.append(jnp.where(sub == 0, xtail[d - 1:d, lanes], from_prev_chunk))
            for d in range(1, CONV_W):
                xtail[d - 1:d, lanes] = x[LRU_CHUNK - d][SUBLANES - 1:SUBLANES, :]

            def x_at(j):
                return x[j] if j >= 0 else before[-j - 1]

            taps = [cw_ref[t:t + 1, lanes] for t in range(CONV_W)]
            bias = cb_ref[:, lanes]
            xc = jnp.concatenate(
                [bias + sum(taps[t] * x_at(j - (CONV_W - 1) + t) for t in range(CONV_W))
                 for j in range(LRU_CHUNK)], axis=0)
            xcb = xc.astype(BF16)
            r = _sigmoid(jnp.dot(xcb, wa_ref[c], preferred_element_type=F32) + ba_ref[:, lanes])
            i = _sigmoid(jnp.dot(xcb, wx_ref[c], preferred_element_type=F32) + bx_ref[:, lanes])
            a = jnp.exp(-LRU_C * r * softplus_neg[:, lanes])
            z = 1.0 - a * a
            u = jnp.where(z > 0.0, z * lax.rsqrt(z), 0.0) * (i * xc)

            h = jnp.zeros((SUBLANES, LANES), F32)
            p = jnp.ones((SUBLANES, LANES), F32)
            hs, ps = [], []
            for j in range(LRU_CHUNK):
                aj = piece(a, j)
                uj = piece(u, j)
                if has_padding:
                    uj = jnp.where(first_row + j >= PAD_FRONT, uj, 0.0)
                h = aj * h + uj
                p = aj * p
                hs.append(h)
                ps.append(p)
            entry = [hcar[:, lanes]]
            for s in range(SUBLANES):
                entry.append(h[s:s + 1, :] + p[s:s + 1, :] * entry[s])
            hcar[:, lanes] = entry[SUBLANES]
            entry_rows = jnp.concatenate(entry[:SUBLANES], axis=0)

            for j in range(LRU_CHUNK):
                state = hs[j] + ps[j] * entry_rows
                out = state * _gelu_tanh(y_st[c, strided(j), :])
                sumsq[j] = sumsq[j] + out * out
                o_st[c, strided(j), :] = out

        for j in range(LRU_CHUNK):
            ms = jnp.sum(sumsq[j], axis=1, keepdims=True) * (1.0 / LRU_WIDTH)
            s_st[strided(j), :] = jnp.broadcast_to(lax.rsqrt(ms + EPS), (SUBLANES, LANES))
        scale = s_st[...]
        for c in range(LRU_SLABS):
            lanes = slice(c * LANES, (c + 1) * LANES)
            op[rows, lanes] = (o_st[c] * scale * g_ref[:, lanes]).astype(op.dtype)

    assert PAD_FRONT <= LRU_SEG
    segment(0, True)

    def later_segment(k, carry):
        segment(k, False)
        return carry

    lax.fori_loop(1, n_seg, later_segment, 0)
    o_ref[...] = op[BLOCK:, :]


def _rglru(xr, yr, xrm, yrm, cw, cb, wa, wx, ba, bx, lam, g_lru, bsz, nbx):
    seq = nbx * BLOCK
    assert (seq + BLOCK) % LRU_SEG == 0
    main = pl.BlockSpec((None, seq, LRU_WIDTH), lambda b: (b, 0, 0))
    const2 = lambda b: (0, 0)
    row_spec = pl.BlockSpec((1, LRU_WIDTH), const2)
    gate_spec = pl.BlockSpec((LRU_SLABS, LANES, LANES), lambda b: (0, 0, 0))
    padded = pltpu.VMEM((seq + BLOCK, LRU_WIDTH), BF16)
    slabs = pltpu.VMEM((LRU_SLABS, LRU_SEG, LANES), F32)
    return pl.pallas_call(
        _lru_kernel,
        grid=(bsz,),
        in_specs=[main, main,
                  pl.BlockSpec((BLOCK, LRU_WIDTH), const2),
                  pl.BlockSpec((BLOCK, LRU_WIDTH), const2),
                  pl.BlockSpec((CONV_W, LRU_WIDTH), const2),
                  row_spec, gate_spec, gate_spec,
                  row_spec, row_spec, row_spec, row_spec],
        out_specs=main,
        out_shape=jax.ShapeDtypeStruct((bsz, seq, LRU_WIDTH), BF16),
        scratch_shapes=[padded, padded, padded, slabs, slabs, slabs,
                        pltpu.VMEM((LRU_SEG, LANES), F32),
                        pltpu.VMEM((SUBLANES, LRU_WIDTH), F32),
                        pltpu.VMEM((1, LRU_WIDTH), F32)],
        compiler_params=_cparams(1),
        name="rglru",
    )(xr, yr, xrm, yrm, cw, cb, wa, wx, ba, bx, lam, g_lru)


def _pack_rows(v):
    bits = lax.bitcast_convert_type(v.astype(BF16).astype(F32), U32)
    return (bits[:, :PACKED] >> 16) | (bits[:, PACKED:] & jnp.uint32(0xFFFF0000))


def _unpack_rows(w):
    lo = lax.bitcast_convert_type(w << 16, F32)
    hi = lax.bitcast_convert_type(w & jnp.uint32(0xFFFF0000), F32)
    return lo, hi


def _layer_norm(z, g, b):
    mu = jnp.mean(z, axis=1, keepdims=True)
    zc = z - mu
    var = jnp.mean(zc * zc, axis=1, keepdims=True)
    return zc * lax.rsqrt(var + EPS) * g + b


def _out_proj_kernel(a_ref, l_ref, x_ref, wa_ref, wl_ref, g_ref, b_ref, wrt_hi_ref, wrt_lo_ref,
                     brt_ref, h_ref, hp_ref, info_ref, cnt_ref):
    @pl.when(pl.program_id(0) == 0)
    def _():
        cnt_ref[...] = jnp.zeros_like(cnt_ref)

    for c in range(h_ref.shape[0] // PROJ_ROWS):
        rows = slice(c * PROJ_ROWS, (c + 1) * PROJ_ROWS)
        _out_proj_rows(a_ref[rows, :], l_ref[rows, :], x_ref[rows, :], wa_ref, wl_ref, g_ref,
                       b_ref, wrt_hi_ref, wrt_lo_ref, brt_ref,
                       h_ref.at[rows, :], hp_ref.at[rows, :], info_ref.at[:, rows], cnt_ref)


def _out_proj_rows(a, l, x, wa_ref, wl_ref, g_ref, b_ref, wrt_hi_ref, wrt_lo_ref, brt_ref,
                   h_ref, hp_ref, info_ref, cnt_ref):
    mix = jnp.dot(a, wa_ref[...], preferred_element_type=F32)
    mix = mix + jnp.dot(l, wl_ref[...], preferred_element_type=F32)
    h = _layer_norm(ALPHA * x + mix, g_ref[...], b_ref[...])
    h_ref[...] = h
    hp_ref[...] = _pack_rows(h)

    h_hi = h.astype(BF16)
    h_lo = (h - h_hi.astype(F32)).astype(BF16)
    nt = (((1,), (1,)), ((), ()))
    lg = (lax.dot_general(wrt_hi_ref[...], h_hi, nt, preferred_element_type=F32)
          + lax.dot_general(wrt_lo_ref[...], h_hi, nt, preferred_element_type=F32)
          + lax.dot_general(wrt_hi_ref[...], h_lo, nt, preferred_element_type=F32)) + brt_ref[...]
    tile_shape = (SUBLANES, h.shape[0])
    sub = lax.broadcasted_iota(I32, tile_shape, 0)
    ninf = -jnp.inf
    t0 = lg[0:SUBLANES]
    gl = jnp.where(sub < N_GROUPS, t0, ninf)
    gmax = jnp.max(gl, axis=0, keepdims=True)
    g_idx = jnp.min(jnp.where(gl == gmax, sub, SUBLANES), axis=0, keepdims=True)
    g_w = 1.0 / jnp.sum(jnp.where(sub < N_GROUPS, jnp.exp(t0 - gmax), 0.0),
                        axis=0, keepdims=True)
    el = lg[SUBLANES:2 * SUBLANES]
    for g in range(1, N_GROUPS):
        el = jnp.where(g_idx == g, lg[(g + 1) * SUBLANES:(g + 2) * SUBLANES], el)
    v1 = jnp.max(el, axis=0, keepdims=True)
    i1 = jnp.min(jnp.where(el == v1, sub, SUBLANES), axis=0, keepdims=True)
    el2 = jnp.where(sub == i1, ninf, el)
    v2 = jnp.max(el2, axis=0, keepdims=True)
    i2 = jnp.min(jnp.where(el2 == v2, sub, SUBLANES), axis=0, keepdims=True)
    t = jnp.exp(v2 - v1)
    w1 = 1.0 / (1.0 + t)
    w2 = t * w1
    e_base = g_idx * EXPERTS_PER_GROUP
    e1 = e_base + i1
    e2 = e_base + i2
    info_ref[...] = jnp.where(sub == 0, e1.astype(F32),
                              jnp.where(sub == 1, e2.astype(F32),
                                        jnp.where(sub == 2, g_w * w1,
                                                  jnp.where(sub == 3, g_w * w2, 0.0))))
    expert = lax.broadcasted_iota(I32, (N_EXPERTS, h.shape[0]), 0)
    chosen = (expert == e1).astype(F32) + (expert == e2).astype(F32)
    cnt_ref[...] += jnp.sum(chosen, axis=1, keepdims=True)


ROUTER_ROWS = (N_GROUPS + 1) * SUBLANES


def _out_proj(attn_n, lru_n, x2d, wo_a, wo_l, ln_g, ln_b, w_rt_hi, w_rt_lo, b_rt, rows):
    n = x2d.shape[0]
    const = lambda i: (0, 0)
    tile = lambda w: pl.BlockSpec((rows, w), lambda i: (i, 0))
    return pl.pallas_call(
        _out_proj_kernel,
        grid=(n // rows,),
        in_specs=[tile(ATTN_WIDTH), tile(LRU_WIDTH), tile(D_MODEL),
                  pl.BlockSpec((ATTN_WIDTH, D_MODEL), const),
                  pl.BlockSpec((LRU_WIDTH, D_MODEL), const),
                  pl.BlockSpec((1, D_MODEL), const),
                  pl.BlockSpec((1, D_MODEL), const),
                  pl.BlockSpec((ROUTER_ROWS, D_MODEL), const),
                  pl.BlockSpec((ROUTER_ROWS, D_MODEL), const),
                  pl.BlockSpec((ROUTER_ROWS, 1), const)],
        out_specs=[tile(D_MODEL), tile(PACKED),
                   pl.BlockSpec((SUBLANES, rows), lambda i: (0, i)),
                   pl.BlockSpec((N_EXPERTS, LANES), const)],
        out_shape=[jax.ShapeDtypeStruct((n, D_MODEL), F32),
                   jax.ShapeDtypeStruct((n, PACKED), U32),
                   jax.ShapeDtypeStruct((SUBLANES, n), F32),
                   jax.ShapeDtypeStruct((N_EXPERTS, LANES), F32)],
        compiler_params=_cparams(1),
        name="out_proj",
    )(attn_n, lru_n, x2d, wo_a, wo_l, ln_g, ln_b, w_rt_hi, w_rt_lo, b_rt)


def _route_kernel(info_ref, cnt_ref, tri_ref, dest_ref, carry, pstart):
    t = pl.program_id(0)
    info = info_ref[...]
    shape = (N_EXPERTS, info.shape[1])
    expert = lax.broadcasted_iota(I32, shape, 0)
    oh1 = (expert == info[0:1, :].astype(I32)).astype(F32)
    oh2 = (expert == info[1:2, :].astype(I32)).astype(F32)
    both = oh1 + oh2

    @pl.when(t == 0)
    def _():
        c = cnt_ref[...].astype(I32)
        padded = ((c + (MOE_BLOCK - 1)) // MOE_BLOCK) * MOE_BLOCK
        e = lax.broadcasted_iota(I32, (N_EXPERTS, LANES), 0)
        scan = padded
        for d in (1, 2, 4, 8, 16):
            scan = scan + jnp.where(e >= d, pltpu.roll(scan, d, axis=0), 0)
        pstart[...] = (scan - padded)[:, 0:1].astype(F32)
        carry[...] = jnp.zeros_like(carry)

    before = jnp.dot(both.astype(BF16), tri_ref[...], preferred_element_type=F32)
    row_of = before + (carry[...] + pstart[...])
    r1 = jnp.sum(oh1 * row_of, axis=0, keepdims=True)
    r2 = jnp.sum(oh2 * row_of, axis=0, keepdims=True)
    sub = lax.broadcasted_iota(I32, dest_ref.shape, 0)
    dest_ref[...] = jnp.where(sub == 0, r1, jnp.where(sub == 1, r2, 0.0)).astype(I32)
    carry[...] += jnp.sum(both, axis=1, keepdims=True)


def _route(info_t, cnt, cols):
    n = info_t.shape[1]
    tri = jnp.asarray(np.triu(np.ones((cols, cols), np.float32), 1), BF16)
    return pl.pallas_call(
        _route_kernel,
        grid=(n // cols,),
        in_specs=[pl.BlockSpec((SUBLANES, cols), lambda t: (0, t)),
                  pl.BlockSpec((N_EXPERTS, LANES), lambda t: (0, 0)),
                  pl.BlockSpec((cols, cols), lambda t: (0, 0))],
        out_specs=pl.BlockSpec((SUBLANES, cols), lambda t: (0, t)),
        out_shape=jax.ShapeDtypeStruct((SUBLANES, n), I32),
        scratch_shapes=[pltpu.VMEM((N_EXPERTS, 1), F32), pltpu.VMEM((N_EXPERTS, 1), F32)],
        compiler_params=_cparams(1),
        name="route",
    )(info_t, cnt, tri)


def _sc_mesh():
    return plsc.VectorSubcoreMesh(core_axis_name="core", subcore_axis_name="subcore")


def _sc_worker_id():
    return lax.axis_index("subcore") * SC_CORES + lax.axis_index("core")


def _sc_scatter_rows(rows, d0, d1, cap):
    n, width = rows.shape
    per_worker = n // SC_WORKERS
    n_win = per_worker // SC_WINDOW

    def body(x_hbm, d0_hbm, d1_hbm, o_hbm, i0_v, i1_v, rows_v, rsem, sem0, sem1):
        wid = _sc_worker_id()
        pltpu.sync_copy(d0_hbm.at[wid], i0_v)
        pltpu.sync_copy(d1_hbm.at[wid], i1_v)

        def read(j):
            src = x_hbm.at[pl.ds(wid * per_worker + j * SC_WINDOW, SC_WINDOW)]
            return pltpu.make_async_copy(src, rows_v.at[j % 2], rsem.at[j % 2])

        def scatters(j):
            return (pltpu.make_async_copy(rows_v.at[j % 2], o_hbm.at[i0_v.at[j]], sem0.at[j % 2]),
                    pltpu.make_async_copy(rows_v.at[j % 2], o_hbm.at[i1_v.at[j]], sem1.at[j % 2]))

        read(0).start()
        for j in range(n_win):
            if j + 1 < n_win:
                if j >= 1:
                    for cp in scatters(j - 1):
                        cp.wait()
                read(j + 1).start()
            read(j).wait()
            for cp in scatters(j):
                cp.start()
        for j in range(max(n_win - 2, 0), n_win):
            for cp in scatters(j):
                cp.wait()

    return pl.kernel(
        body,
        out_type=jax.ShapeDtypeStruct((cap, width), rows.dtype),
        mesh=_sc_mesh(),
        scratch_types=[pltpu.VMEM((n_win, SC_WINDOW), I32), pltpu.VMEM((n_win, SC_WINDOW), I32),
                       pltpu.VMEM((2, SC_WINDOW, width), rows.dtype),
                       pltpu.SemaphoreType.DMA((2,)), pltpu.SemaphoreType.DMA((2,)),
                       pltpu.SemaphoreType.DMA((2,))],
        name="dispatch",
    )(rows, d0, d1)


def _sc_gather_rows(table, idx):
    width = table.shape[1]
    n_win = idx.shape[1]
    per_worker = n_win * SC_WINDOW

    def body(y_hbm, i_hbm, o_hbm, i_v, rows_v, gsem, wsem):
        wid = _sc_worker_id()
        pltpu.sync_copy(i_hbm.at[wid], i_v)

        def gather(j):
            return pltpu.make_async_copy(y_hbm.at[i_v.at[j]], rows_v.at[j % 2], gsem.at[j % 2])

        def write(j):
            dst = o_hbm.at[pl.ds(wid * per_worker + j * SC_WINDOW, SC_WINDOW)]
            return pltpu.make_async_copy(rows_v.at[j % 2], dst, wsem.at[j % 2])

        gather(0).start()
        for j in range(n_win):
            if j + 1 < n_win:
                if j >= 1:
                    write(j - 1).wait()
                gather(j + 1).start()
            gather(j).wait()
            write(j).start()
        for j in range(max(n_win - 2, 0), n_win):
            write(j).wait()

    return pl.kernel(
        body,
        out_type=jax.ShapeDtypeStruct((SC_WORKERS * per_worker, width), table.dtype),
        mesh=_sc_mesh(),
        scratch_types=[pltpu.VMEM((n_win, SC_WINDOW), I32),
                       pltpu.VMEM((2, SC_WINDOW, width), table.dtype),
                       pltpu.SemaphoreType.DMA((2,)), pltpu.SemaphoreType.DMA((2,))],
        name="collect",
    )(table, idx)


def _expert_kernel(bstart_ref, nblk_ref, nused_ref, xs_hbm, wg_ref, wu_ref, wd_ref, yb_hbm,
                   xbuf, ybuf, zbuf, xsem, ysem, zsem, wg_b, wu_b, wd_b):
    e = pl.program_id(0)
    nused = nused_ref[0]
    n_blocks = yb_hbm.shape[0] // MOE_BLOCK

    def rows(b):
        return pl.ds(pl.multiple_of(b * MOE_BLOCK, MOE_BLOCK), MOE_BLOCK)

    def x_copy(b):
        slot = b % X_RING
        return pltpu.make_async_copy(xs_hbm.at[rows(b)], xbuf.at[slot], xsem.at[slot])

    def y_copy(b, slot):
        return pltpu.make_async_copy(ybuf.at[slot], yb_hbm.at[rows(b)], ysem.at[slot])

    @pl.when(e == 0)
    def _():
        for b in range(X_RING - 1):
            @pl.when(b < nused)
            def _():
                x_copy(b).start()

    wg_b[...] = wg_ref[...].astype(BF16)
    wu_b[...] = wu_ref[...].astype(BF16)
    wd_b[...] = wd_ref[...].astype(BF16)

    def block(b, carry):
        slot = b % 2
        x_copy(b).wait()

        @pl.when(b + (X_RING - 1) < nused)
        def _():
            x_copy(b + (X_RING - 1)).start()

        @pl.when(b >= 2)
        def _():
            y_copy(b - 2, slot).wait()

        lo, hi = _unpack_rows(xbuf[b % X_RING])
        lo = lo.astype(BF16)
        hi = hi.astype(BF16)
        g = (jnp.dot(lo, wg_b[0:PACKED, :], preferred_element_type=F32)
             + jnp.dot(hi, wg_b[PACKED:, :], preferred_element_type=F32))
        u = (jnp.dot(lo, wu_b[0:PACKED, :], preferred_element_type=F32)
             + jnp.dot(hi, wu_b[PACKED:, :], preferred_element_type=F32))
        mid = (g * _sigmoid(g) * u).astype(BF16)
        ybuf[slot] = _pack_rows(jnp.dot(mid, wd_b[...], preferred_element_type=F32))
        y_copy(b, slot).start()
        return carry

    b0 = bstart_ref[e]
    lax.fori_loop(b0, b0 + nblk_ref[e], block, 0)

    @pl.when(e == pl.num_programs(0) - 1)
    def _():
        for back in (2, 1):
            @pl.when(nused >= back)
            def _():
                y_copy(nused - back, (nused - back) % 2).wait()

        zbuf[...] = jnp.zeros_like(zbuf)

        def z_copy(b):
            return pltpu.make_async_copy(zbuf, yb_hbm.at[rows(b)], zsem.at[0])

        def z_start(b, carry):
            z_copy(b).start()
            return carry

        def z_wait(b, carry):
            z_copy(b).wait()
            return carry

        lax.fori_loop(nused, n_blocks, z_start, 0)
        lax.fori_loop(nused, n_blocks, z_wait, 0)


def _experts(bstart, nblk, nused, xs, w_gate, w_up, w_down):
    cap = xs.shape[0]
    w_idx = lambda e, bs, nb, nu: (e, 0, 0)
    grid_spec = pltpu.PrefetchScalarGridSpec(
        num_scalar_prefetch=3,
        grid=(N_EXPERTS,),
        in_specs=[pl.BlockSpec(memory_space=pl.ANY),
                  pl.BlockSpec((None, D_MODEL, D_FF), w_idx),
                  pl.BlockSpec((None, D_MODEL, D_FF), w_idx),
                  pl.BlockSpec((None, D_FF, D_MODEL), w_idx)],
        out_specs=pl.BlockSpec(memory_space=pl.ANY),
        scratch_shapes=[pltpu.VMEM((X_RING, MOE_BLOCK, PACKED), U32),
                        pltpu.VMEM((2, MOE_BLOCK, PACKED), U32),
                        pltpu.VMEM((MOE_BLOCK, PACKED), U32),
                        pltpu.SemaphoreType.DMA((X_RING,)),
                        pltpu.SemaphoreType.DMA((2,)),
                        pltpu.SemaphoreType.DMA((1,)),
                        pltpu.VMEM((D_MODEL, D_FF), BF16),
                        pltpu.VMEM((D_MODEL, D_FF), BF16),
                        pltpu.VMEM((D_FF, D_MODEL), BF16)])
    return pl.pallas_call(
        _expert_kernel,
        grid_spec=grid_spec,
        out_shape=jax.ShapeDtypeStruct((cap, PACKED), U32),
        compiler_params=_cparams(1),
        name="experts",
    )(bstart, nblk, nused, xs, w_gate, w_up, w_down)


def _combine_kernel(y0_ref, y1_ref, h_ref, info_ref, g_ref, b_ref, o_ref):
    info = info_ref[...].T
    g0 = info[:, 2:3]
    g1 = info[:, 3:4]
    lo0, hi0 = _unpack_rows(y0_ref[...])
    lo1, hi1 = _unpack_rows(y1_ref[...])
    y = jnp.concatenate([g0 * lo0 + g1 * lo1, g0 * hi0 + g1 * hi1], axis=1)
    o_ref[...] = _layer_norm(ALPHA * h_ref[...] + y, g_ref[...], b_ref[...])


def _combine(ys, h, info, ln_g, ln_b, rows, part, n_parts):
    n = h.shape[0]
    steps = n // n_parts // rows
    off = part * steps
    const = lambda i: (0, 0)
    return pl.pallas_call(
        _combine_kernel,
        grid=(steps,),
        in_specs=[pl.BlockSpec((rows, PACKED), lambda i: (i, 0)),
                  pl.BlockSpec((rows, PACKED), lambda i: (i + steps, 0)),
                  pl.BlockSpec((rows, D_MODEL), lambda i: (i + off, 0)),
                  pl.BlockSpec((SUBLANES, rows), lambda i: (0, i + off)),
                  pl.BlockSpec((1, D_MODEL), const),
                  pl.BlockSpec((1, D_MODEL), const)],
        out_specs=pl.BlockSpec((rows, D_MODEL), lambda i: (i + off, 0)),
        out_shape=jax.ShapeDtypeStruct((n, D_MODEL), F32),
        input_output_aliases={2: 0},
        compiler_params=_cparams(1),
        name="combine",
    )(ys, ys, h, info, ln_g, ln_b)


def _alibi_bias():
    qi = np.arange(BLOCK)[:, None]
    kj = np.arange(2 * BLOCK)[None, :]
    dist = qi - kj + BLOCK
    band = (dist >= 0) & (dist < BLOCK)
    slopes = np.exp2(-8.0 * np.arange(1, N_Q_HEADS + 1, dtype=np.float32) / N_Q_HEADS)
    bias = np.where(band[None], -slopes[:, None, None] * dist[None].astype(np.float32), NEG)
    bias = bias * LOG2E
    first = np.where((kj >= PAD_FRONT)[None], bias, NEG)
    out = np.empty((2, N_KV_HEADS, 2 * BLOCK, 4 * BLOCK), np.float32)
    for v, per_head in enumerate((first, bias)):
        for j in range(N_KV_HEADS):
            out[v, j] = np.block([[per_head[4 * j], per_head[4 * j + 1]],
                                  [per_head[4 * j + 2], per_head[4 * j + 3]]])
    return jnp.asarray(out, F32)


def _slab_gates(w):
    nb, c, _ = w.shape
    per = LANES // c
    w = w.reshape(nb // per, per, c, c)
    eye = jnp.eye(per, dtype=w.dtype)
    return jnp.einsum('spcd,pq->spcqd', w, eye).reshape(nb // per, LANES, LANES).astype(BF16)


def kernel(x, meta_tokens, w_in, conv_w, conv_b, lru_wa, lru_ba, lru_wx, lru_bx, lru_lambda,
           attn_sinks, g_attn, g_lru, w_out, ln1_g, ln1_b, w_group, b_group, w_router,
           b_router, w_gate, w_up, w_down, ln2_g, ln2_b):
    bsz, seq, d = x.shape
    nbx = seq // BLOCK
    n_tok = bsz * seq
    x2d = x.reshape(n_tok, d)
    row = lambda v: v.reshape(1, -1).astype(F32)

    q_scale = jnp.concatenate([jnp.full((ATTN_WIDTH,), LOG2E * HEAD_DIM ** -0.5, F32),
                               jnp.ones((IN_COLS - ATTN_WIDTH,), F32)])
    w_in_b = (w_in[0] * q_scale).astype(BF16)
    meta_blk = jnp.concatenate([jnp.zeros((PAD_FRONT, d), F32), meta_tokens.astype(F32)], axis=0)
    q, kv, xr, yr = _in_proj(x2d, w_in_b, PROJ_ROWS)
    qm, kvm, xrm, yrm = _in_proj(meta_blk, w_in_b, BLOCK)
    shp = lambda a: a.reshape(bsz, seq, a.shape[-1])

    attn_n = _attention(attn_sinks[0].astype(F32) * LOG2E, shp(q), shp(kv), kvm, _alibi_bias(),
                        row(g_attn[0]), bsz, nbx)
    lru_n = _rglru(shp(xr), shp(yr), xrm, yrm, conv_w[0].astype(F32), row(conv_b[0]),
                   _slab_gates(lru_wa[0]), _slab_gates(lru_wx[0]),
                   row(lru_ba[0]), row(lru_bx[0]), row(lru_lambda[0]), row(g_lru[0]), bsz, nbx)

    w_out_b = w_out[0].astype(BF16)
    gpad = SUBLANES - N_GROUPS
    w_rt = jnp.concatenate(
        [w_group[0].T, jnp.zeros((gpad, d), F32),
         jnp.transpose(w_router[0], (0, 2, 1)).reshape(N_EXPERTS, d)], axis=0).astype(F32)
    w_rt_hi = w_rt.astype(BF16)
    w_rt_lo = (w_rt - w_rt_hi.astype(F32)).astype(BF16)
    b_rt = jnp.concatenate([b_group[0], jnp.zeros((gpad,), F32),
                            b_router[0].reshape(-1)]).astype(F32).reshape(ROUTER_ROWS, 1)
    h1, hp, info, cnt = _out_proj(
        attn_n.reshape(n_tok, ATTN_WIDTH), lru_n.reshape(n_tok, LRU_WIDTH), x2d,
        w_out_b[:ATTN_WIDTH], w_out_b[ATTN_WIDTH:], row(ln1_g[0]), row(ln1_b[0]),
        w_rt_hi, w_rt_lo, b_rt, OUT_PROJ_ROWS)

    dest = _route(info, cnt, ROUTE_ROWS)
    n_slots = n_tok * TOP_K
    n_blocks = n_slots // MOE_BLOCK + N_EXPERTS
    cap = n_blocks * MOE_BLOCK
    nblk = (cnt[:, 0].astype(I32) + MOE_BLOCK - 1) // MOE_BLOCK
    bends = jnp.cumsum(nblk)
    bstart = (bends - nblk).astype(I32)
    nused = bends[-1:].astype(I32)
    windows = lambda v: v.reshape(SC_WORKERS, -1, SC_WINDOW)
    d0 = dest[0]
    d1 = dest[1]

    xs = _sc_scatter_rows(hp, windows(d0), windows(d1), cap)
    yb = _experts(bstart, nblk.astype(I32), nused, xs, w_gate[0], w_up[0], w_down[0])
    out = h1
    part_len = n_tok // COMBINE_PARTS
    for part in range(COMBINE_PARTS):
        tok = slice(part * part_len, (part + 1) * part_len)
        ys = _sc_gather_rows(yb, windows(jnp.concatenate([d0[tok], d1[tok]])))
        out = _combine(ys, out, info, row(ln2_g[0]), row(ln2_b[0]), COMBINE_ROWS,
                       part, COMBINE_PARTS)
    return out.reshape(bsz, seq, d)
```

```python
import jax
import jax.numpy as jnp
import numpy as np
from jax import lax
from jax.experimental import pallas as pl
from jax.experimental.pallas import tpu as pltpu
from jax.experimental.pallas import tpu_sc as plsc

F32 = jnp.float32
BF16 = jnp.bfloat16
U32 = jnp.uint32
I32 = jnp.int32

D_MODEL = 1024
N_META = 16
BLOCK = 128
PAD_FRONT = BLOCK - N_META
HEAD_DIM = 64
ATTN_WIDTH = 512
LRU_WIDTH = 512
N_Q_HEADS = 8
N_KV_HEADS = 2
KV_WIDTH = N_KV_HEADS * HEAD_DIM
LRU_BLOCKS = 8
CONV_W = 4
LRU_C = 8.0
IN_COLS = ATTN_WIDTH + 2 * KV_WIDTH + 2 * LRU_WIDTH
N_GROUPS = 4
EXPERTS_PER_GROUP = 8
N_EXPERTS = N_GROUPS * EXPERTS_PER_GROUP
TOP_K = 2
D_FF = 512
MOE_BLOCK = 256
ALPHA = 2.0 ** 0.25
EPS = 1e-5
NEG = -1e30
LOG2E = float(np.log2(np.e))
LANES = 128
SUBLANES = 8
PACKED = D_MODEL // 2

PROJ_ROWS = 512
OUT_PROJ_ROWS = 1024
ROUTE_ROWS = 512
COMBINE_ROWS = 512
COMBINE_PARTS = 4
X_AHEAD = 4
X_RING = X_AHEAD + 2
Y_RING = 4
VMEM_LIMIT = 48 * 1024 * 1024

SC_CORES = 2
SC_SUBCORES = 16
SC_WORKERS = SC_CORES * SC_SUBCORES
SC_WINDOW = 64


def _cparams(n_axes):
    return pltpu.CompilerParams(
        dimension_semantics=("arbitrary",) * n_axes, vmem_limit_bytes=VMEM_LIMIT)


def _in_proj_kernel(x_ref, w_ref, q_ref, kv_ref, xr_ref, yr_ref):
    proj = jnp.dot(x_ref[...].astype(BF16), w_ref[...], preferred_element_type=F32)
    o = 0
    for ref, width in ((q_ref, ATTN_WIDTH), (kv_ref, 2 * KV_WIDTH),
                       (xr_ref, LRU_WIDTH), (yr_ref, LRU_WIDTH)):
        ref[...] = proj[:, o:o + width].astype(ref.dtype)
        o += width


def _in_proj(x2d, w_bf16, rows):
    n = x2d.shape[0]
    widths = (ATTN_WIDTH, 2 * KV_WIDTH, LRU_WIDTH, LRU_WIDTH)
    return pl.pallas_call(
        _in_proj_kernel,
        grid=(n // rows,),
        in_specs=[pl.BlockSpec((rows, D_MODEL), lambda i: (i, 0)),
                  pl.BlockSpec((D_MODEL, IN_COLS), lambda i: (0, 0))],
        out_specs=[pl.BlockSpec((rows, w), lambda i: (i, 0)) for w in widths],
        out_shape=[jax.ShapeDtypeStruct((n, w), BF16) for w in widths],
        compiler_params=_cparams(1),
        name="in_proj",
    )(x2d, w_bf16)


def _attn_kernel(sinks_ref, q_ref, kv_ref, kvm_ref, bias_ref, g_ref, o_ref,
                 klo, khi, vlo, vhi):
    nbx = q_ref.shape[0] // BLOCK
    lo_lanes = lax.broadcasted_iota(I32, (BLOCK // 2, LANES), 1) < HEAD_DIM

    def layout_block(n, blk):
        rows = pl.ds(pl.multiple_of(n * BLOCK, BLOCK), BLOCK)
        as_bf16 = lambda words: pltpu.bitcast(words, BF16)
        for src, dst_lo, dst_hi in ((blk[:, :KV_WIDTH], klo, khi), (blk[:, KV_WIDTH:], vlo, vhi)):
            w = pltpu.bitcast(src, U32)
            r = pltpu.roll(w, HEAD_DIM, axis=1)
            zero = jnp.zeros_like(w)
            dst_lo[0, rows, :] = as_bf16(jnp.where(lo_lanes, w, zero))
            dst_hi[0, rows, :] = as_bf16(jnp.where(lo_lanes, zero, r))
            dst_lo[1, rows, :] = as_bf16(jnp.where(lo_lanes, r, zero))
            dst_hi[1, rows, :] = as_bf16(jnp.where(lo_lanes, zero, w))

    layout_block(0, kvm_ref[...])

    def layout_body(n, carry):
        layout_block(n + 1, kv_ref[pl.ds(pl.multiple_of(n * BLOCK, BLOCK), BLOCK), :])
        return carry

    lax.fori_loop(0, nbx, layout_body, 0, unroll=4)

    ones_lo = jnp.where(lax.broadcasted_iota(I32, (2 * BLOCK, LANES), 1) < HEAD_DIM,
                        1.0, 0.0).astype(BF16)
    ones_hi = (1.0 - ones_lo.astype(F32)).astype(BF16)
    top_rows = lax.broadcasted_iota(I32, (2 * BLOCK, 1), 0) < BLOCK
    lo_half = lax.broadcasted_iota(I32, (2 * BLOCK, LANES), 1) < HEAD_DIM

    def block(i, carry):
        q_rows = pl.ds(pl.multiple_of(i * BLOCK, BLOCK), BLOCK)
        win = pl.ds(pl.multiple_of(i * BLOCK, BLOCK), 2 * BLOCK)
        q = q_ref[q_rows, :]
        first = jnp.minimum(i, 1)
        outs = []
        for j in range(N_KV_HEADS):
            q2 = jnp.concatenate([q[:, (2 * j) * LANES:(2 * j + 1) * LANES],
                                  q[:, (2 * j + 1) * LANES:(2 * j + 2) * LANES]], axis=0)
            kc = jnp.concatenate([klo[j, win, :], khi[j, win, :]], axis=0)
            s = lax.dot_general(q2, kc, (((1,), (1,)), ((), ())), preferred_element_type=F32)
            s = s + bias_ref[first, j]
            ps, es = [], []
            for c in range(2):
                sink = jnp.where(top_rows, sinks_ref[4 * j + c], sinks_ref[4 * j + 2 + c])
                sc = s[:, c * 2 * BLOCK:(c + 1) * 2 * BLOCK]
                m = jnp.maximum(jnp.max(sc, axis=1, keepdims=True), sink)
                ps.append(jnp.exp2(sc - m).astype(BF16))
                es.append(jnp.exp2(sink - m))
            v_lo = jnp.concatenate([vlo[j, win, :], ones_lo], axis=1)
            v_hi = jnp.concatenate([vhi[j, win, :], ones_hi], axis=1)
            r = (jnp.dot(ps[0], v_lo, preferred_element_type=F32)
                 + jnp.dot(ps[1], v_hi, preferred_element_type=F32))
            den = r[:, LANES:] + jnp.where(lo_half, es[0], es[1])
            o2 = r[:, :LANES] * (1.0 / den)
            outs += [o2[:BLOCK], o2[BLOCK:]]
        out = jnp.concatenate(outs, axis=1)
        ms = jnp.mean(out * out, axis=1, keepdims=True)
        o_ref[q_rows, :] = (out * lax.rsqrt(ms + EPS) * g_ref[...]).astype(o_ref.dtype)
        return carry

    lax.fori_loop(0, nbx, block, 0, unroll=4)


def _attention(sinks, q, kv, kvm, bias, g_attn, bsz, nbx):
    seq = nbx * BLOCK
    const2 = lambda b: (0, 0)
    kv_scratch = pltpu.VMEM((N_KV_HEADS, seq + BLOCK, LANES), BF16)
    return pl.pallas_call(
        _attn_kernel,
        grid=(bsz,),
        in_specs=[pl.BlockSpec(memory_space=pltpu.SMEM),
                  pl.BlockSpec((None, seq, ATTN_WIDTH), lambda b: (b, 0, 0)),
                  pl.BlockSpec((None, seq, 2 * KV_WIDTH), lambda b: (b, 0, 0)),
                  pl.BlockSpec((BLOCK, 2 * KV_WIDTH), const2),
                  pl.BlockSpec((2, N_KV_HEADS, 2 * BLOCK, 4 * BLOCK), lambda b: (0, 0, 0, 0)),
                  pl.BlockSpec((1, ATTN_WIDTH), const2)],
        out_specs=pl.BlockSpec((None, seq, ATTN_WIDTH), lambda b: (b, 0, 0)),
        out_shape=jax.ShapeDtypeStruct((bsz, seq, ATTN_WIDTH), BF16),
        scratch_shapes=[kv_scratch, kv_scratch, kv_scratch, kv_scratch],
        compiler_params=_cparams(1),
        name="attention",
    )(sinks, q, kv, kvm, bias, g_attn)


def _sigmoid(v):
    return 0.5 * jnp.tanh(0.5 * v) + 0.5


def _gelu_tanh(y):
    c = float(np.sqrt(2.0 / np.pi))
    half = 0.5 * y
    return half + half * jnp.tanh(y * (c + (c * 0.044715) * (y * y)))


LRU_CHUNK = 44
LRU_SEG = SUBLANES * LRU_CHUNK
LRU_SLABS = LRU_WIDTH // LANES


def _lru_kernel(xr_ref, yr_ref, xrm_ref, yrm_ref, cw_ref, cb_ref, wa_ref, wx_ref, ba_ref,
                bx_ref, lam_ref, g_ref, o_ref, xp, yp, op, x_st, y_st, o_st, s_st, xtail, hcar):
    seq = xr_ref.shape[0]
    n_seg = (seq + BLOCK) // LRU_SEG
    xp[0:BLOCK, :] = xrm_ref[...]
    xp[BLOCK:, :] = xr_ref[...]
    yp[0:BLOCK, :] = yrm_ref[...]
    yp[BLOCK:, :] = yr_ref[...]
    xtail[...] = jnp.zeros_like(xtail)
    hcar[...] = jnp.zeros_like(hcar)
    lam = lam_ref[...]
    softplus_neg = jnp.maximum(-lam, 0.0) + jnp.log(1.0 + jnp.exp(-jnp.abs(lam)))
    sub = lax.broadcasted_iota(jnp.int32, (SUBLANES, LANES), 0)

    def strided(j):
        return pl.ds(j, SUBLANES, stride=LRU_CHUNK)

    def piece(v, j):
        return v[j * SUBLANES:(j + 1) * SUBLANES, :]

    def segment(k, has_padding):
        start = k * LRU_SEG
        if not isinstance(start, int):
            start = pl.multiple_of(start, 2 * SUBLANES)
        rows = pl.ds(start, LRU_SEG)
        x_nat = xp[rows, :].astype(F32)
        y_nat = yp[rows, :].astype(F32)
        for c in range(LRU_SLABS):
            x_st[c] = x_nat[:, c * LANES:(c + 1) * LANES]
            y_st[c] = y_nat[:, c * LANES:(c + 1) * LANES]
        first_row = k * LRU_SEG + LRU_CHUNK * sub
        sumsq = [jnp.zeros((SUBLANES, LANES), F32) for _ in range(LRU_CHUNK)]

        for c in range(LRU_SLABS):
            lanes = slice(c * LANES, (c + 1) * LANES)
            x = [x_st[c, strided(j), :] for j in range(LRU_CHUNK)]
            before = []
            for d in range(1, CONV_W):
                from_prev_chunk = pltpu.roll(x[LRU_CHUNK - d], 1, axis=0)
                before.append(jnp.where(sub == 0, xtail[d - 1:d, lanes], from_prev_chunk))
            for d in range(1, CONV_W):
                xtail[d - 1:d, lanes] = x[LRU_CHUNK - d][SUBLANES - 1:SUBLANES, :]

            def x_at(j):
                return x[j] if j >= 0 else before[-j - 1]

            taps = [cw_ref[t:t + 1, lanes] for t in range(CONV_W)]
            bias = cb_ref[:, lanes]
            xc = jnp.concatenate(
                [bias + sum(taps[t] * x_at(j - (CONV_W - 1) + t) for t in range(CONV_W))
                 for j in range(LRU_CHUNK)], axis=0)
            xcb = xc.astype(BF16)
            r = _sigmoid(jnp.dot(xcb, wa_ref[c], preferred_element_type=F32) + ba_ref[:, lanes])
            i = _sigmoid(jnp.dot(xcb, wx_ref[c], preferred_element_type=F32) + bx_ref[:, lanes])
            a = jnp.exp(-LRU_C * r * softplus_neg[:, lanes])
            z = 1.0 - a * a
            u = jnp.where(z > 0.0, z * lax.rsqrt(z), 0.0) * (i * xc)

            h = jnp.zeros((SUBLANES, LANES), F32)
            p = jnp.ones((SUBLANES, LANES), F32)
            hs, ps = [], []
            for j in range(LRU_CHUNK):
                aj = piece(a, j)
                uj = piece(u, j)
                if has_padding:
                    uj = jnp.where(first_row + j >= PAD_FRONT, uj, 0.0)
                h = aj * h + uj
                p = aj * p
                hs.append(h)
                ps.append(p)
            entry = [hcar[:, lanes]]
            for s in range(SUBLANES):
                entry.append(h[s:s + 1, :] + p[s:s + 1, :] * entry[s])
            hcar[:, lanes] = entry[SUBLANES]
            entry_rows = jnp.concatenate(entry[:SUBLANES], axis=0)

            for j in range(LRU_CHUNK):
                state = hs[j] + ps[j] * entry_rows
                out = state * _gelu_tanh(y_st[c, strided(j), :])
                sumsq[j] = sumsq[j] + out * out
                o_st[c, strided(j), :] = out

        for j in range(LRU_CHUNK):
            ms = jnp.sum(sumsq[j], axis=1, keepdims=True) * (1.0 / LRU_WIDTH)
            s_st[strided(j), :] = jnp.broadcast_to(lax.rsqrt(ms + EPS), (SUBLANES, LANES))
        scale = s_st[...]
        for c in range(LRU_SLABS):
            lanes = slice(c * LANES, (c + 1) * LANES)
            op[rows, lanes] = (o_st[c] * scale * g_ref[:, lanes]).astype(op.dtype)

    assert PAD_FRONT <= LRU_SEG
    segment(0, True)

    def later_segment(k, carry):
        segment(k, False)
        return carry

    lax.fori_loop(1, n_seg, later_segment, 0)
    o_ref[...] = op[BLOCK:, :]


def _rglru(xr, yr, xrm, yrm, cw, cb, wa, wx, ba, bx, lam, g_lru, bsz, nbx):
    seq = nbx * BLOCK
    assert (seq + BLOCK) % LRU_SEG == 0
    main = pl.BlockSpec((None, seq, LRU_WIDTH), lambda b: (b, 0, 0))
    const2 = lambda b: (0, 0)
    row_spec = pl.BlockSpec((1, LRU_WIDTH), const2)
    gate_spec = pl.BlockSpec((LRU_SLABS, LANES, LANES), lambda b: (0, 0, 0))
    padded = pltpu.VMEM((seq + BLOCK, LRU_WIDTH), BF16)
    slabs = pltpu.VMEM((LRU_SLABS, LRU_SEG, LANES), F32)
    return pl.pallas_call(
        _lru_kernel,
        grid=(bsz,),
        in_specs=[main, main,
                  pl.BlockSpec((BLOCK, LRU_WIDTH), const2),
                  pl.BlockSpec((BLOCK, LRU_WIDTH), const2),
                  pl.BlockSpec((CONV_W, LRU_WIDTH), const2),
                  row_spec, gate_spec, gate_spec,
                  row_spec, row_spec, row_spec, row_spec],
        out_specs=main,
        out_shape=jax.ShapeDtypeStruct((bsz, seq, LRU_WIDTH), BF16),
        scratch_shapes=[padded, padded, padded, slabs, slabs, slabs,
                        pltpu.VMEM((LRU_SEG, LANES), F32),
                        pltpu.VMEM((SUBLANES, LRU_WIDTH), F32),
                        pltpu.VMEM((1, LRU_WIDTH), F32)],
        compiler_params=_cparams(1),
        name="rglru",
    )(xr, yr, xrm, yrm, cw, cb, wa, wx, ba, bx, lam, g_lru)


def _pack_rows(v):
    bits = lax.bitcast_convert_type(v.astype(BF16).astype(F32), U32)
    return (bits[:, :PACKED] >> 16) | (bits[:, PACKED:] & jnp.uint32(0xFFFF0000))


def _unpack_rows(w):
    lo = lax.bitcast_convert_type(w << 16, F32)
    hi = lax.bitcast_convert_type(w & jnp.uint32(0xFFFF0000), F32)
    return lo, hi


def _layer_norm(z, g, b):
    mu = jnp.mean(z, axis=1, keepdims=True)
    zc = z - mu
    var = jnp.mean(zc * zc, axis=1, keepdims=True)
    return zc * lax.rsqrt(var + EPS) * g + b


def _out_proj_kernel(a_ref, l_ref, x_ref, wa_ref, wl_ref, g_ref, b_ref, wrt_hi_ref, wrt_lo_ref,
                     brt_ref, h_ref, hp_ref, info_ref, cnt_ref):
    @pl.when(pl.program_id(0) == 0)
    def _():
        cnt_ref[...] = jnp.zeros_like(cnt_ref)

    for c in range(h_ref.shape[0] // PROJ_ROWS):
        rows = slice(c * PROJ_ROWS, (c + 1) * PROJ_ROWS)
        _out_proj_rows(a_ref[rows, :], l_ref[rows, :], x_ref[rows, :], wa_ref, wl_ref, g_ref,
                       b_ref, wrt_hi_ref, wrt_lo_ref, brt_ref,
                       h_ref.at[rows, :], hp_ref.at[rows, :], info_ref.at[:, rows], cnt_ref)


def _out_proj_rows(a, l, x, wa_ref, wl_ref, g_ref, b_ref, wrt_hi_ref, wrt_lo_ref, brt_ref,
                   h_ref, hp_ref, info_ref, cnt_ref):
    mix = jnp.dot(a, wa_ref[...], preferred_element_type=F32)
    mix = mix + jnp.dot(l, wl_ref[...], preferred_element_type=F32)
    h = _layer_norm(ALPHA * x + mix, g_ref[...], b_ref[...])
    h_ref[...] = h
    hp_ref[...] = _pack_rows(h)

    h_hi = h.astype(BF16)
    h_lo = (h - h_hi.astype(F32)).astype(BF16)
    nt = (((1,), (1,)), ((), ()))
    lg = (lax.dot_general(wrt_hi_ref[...], h_hi, nt, preferred_element_type=F32)
          + lax.dot_general(wrt_lo_ref[...], h_hi, nt, preferred_element_type=F32)
          + lax.dot_general(wrt_hi_ref[...], h_lo, nt, preferred_element_type=F32)) + brt_ref[...]
    tile_shape = (SUBLANES, h.shape[0])
    sub = lax.broadcasted_iota(I32, tile_shape, 0)
    ninf = -jnp.inf
    t0 = lg[0:SUBLANES]
    gl = jnp.where(sub < N_GROUPS, t0, ninf)
    gmax = jnp.max(gl, axis=0, keepdims=True)
    g_idx = jnp.min(jnp.where(gl == gmax, sub, SUBLANES), axis=0, keepdims=True)
    g_w = 1.0 / jnp.sum(jnp.where(sub < N_GROUPS, jnp.exp(t0 - gmax), 0.0),
                        axis=0, keepdims=True)
    el = lg[SUBLANES:2 * SUBLANES]
    for g in range(1, N_GROUPS):
        el = jnp.where(g_idx == g, lg[(g + 1) * SUBLANES:(g + 2) * SUBLANES], el)
    v1 = jnp.max(el, axis=0, keepdims=True)
    i1 = jnp.min(jnp.where(el == v1, sub, SUBLANES), axis=0, keepdims=True)
    el2 = jnp.where(sub == i1, ninf, el)
    v2 = jnp.max(el2, axis=0, keepdims=True)
    i2 = jnp.min(jnp.where(el2 == v2, sub, SUBLANES), axis=0, keepdims=True)
    t = jnp.exp(v2 - v1)
    w1 = 1.0 / (1.0 + t)
    w2 = t * w1
    e_base = g_idx * EXPERTS_PER_GROUP
    e1 = e_base + i1
    e2 = e_base + i2
    info_ref[...] = jnp.where(sub == 0, e1.astype(F32),
                              jnp.where(sub == 1, e2.astype(F32),
                                        jnp.where(sub == 2, g_w * w1,
                                                  jnp.where(sub == 3, g_w * w2, 0.0))))
    expert = lax.broadcasted_iota(I32, (N_EXPERTS, h.shape[0]), 0)
    chosen = (expert == e1).astype(F32) + (expert == e2).astype(F32)
    cnt_ref[...] += jnp.sum(chosen, axis=1, keepdims=True)


ROUTER_ROWS = (N_GROUPS + 1) * SUBLANES


def _out_proj(attn_n, lru_n, x2d, wo_a, wo_l, ln_g, ln_b, w_rt_hi, w_rt_lo, b_rt, rows):
    n = x2d.shape[0]
    const = lambda i: (0, 0)
    tile = lambda w: pl.BlockSpec((rows, w), lambda i: (i, 0))
    return pl.pallas_call(
        _out_proj_kernel,
        grid=(n // rows,),
        in_specs=[tile(ATTN_WIDTH), tile(LRU_WIDTH), tile(D_MODEL),
                  pl.BlockSpec((ATTN_WIDTH, D_MODEL), const),
                  pl.BlockSpec((LRU_WIDTH, D_MODEL), const),
                  pl.BlockSpec((1, D_MODEL), const),
                  pl.BlockSpec((1, D_MODEL), const),
                  pl.BlockSpec((ROUTER_ROWS, D_MODEL), const),
                  pl.BlockSpec((ROUTER_ROWS, D_MODEL), const),
                  pl.BlockSpec((ROUTER_ROWS, 1), const)],
        out_specs=[tile(D_MODEL), tile(PACKED),
                   pl.BlockSpec((SUBLANES, rows), lambda i: (0, i)),
                   pl.BlockSpec((N_EXPERTS, LANES), const)],
        out_shape=[jax.ShapeDtypeStruct((n, D_MODEL), F32),
                   jax.ShapeDtypeStruct((n, PACKED), U32),
                   jax.ShapeDtypeStruct((SUBLANES, n), F32),
                   jax.ShapeDtypeStruct((N_EXPERTS, LANES), F32)],
        compiler_params=_cparams(1),
        name="out_proj",
    )(attn_n, lru_n, x2d, wo_a, wo_l, ln_g, ln_b, w_rt_hi, w_rt_lo, b_rt)


def _route_kernel(info_ref, cnt_ref, tri_ref, dest_ref, carry, pstart):
    t = pl.program_id(0)
    info = info_ref[...]
    shape = (N_EXPERTS, info.shape[1])
    expert = lax.broadcasted_iota(I32, shape, 0)
    oh1 = (expert == info[0:1, :].astype(I32)).astype(F32)
    oh2 = (expert == info[1:2, :].astype(I32)).astype(F32)
    both = oh1 + oh2

    @pl.when(t == 0)
    def _():
        c = cnt_ref[...].astype(I32)
        padded = ((c + (MOE_BLOCK - 1)) // MOE_BLOCK) * MOE_BLOCK
        e = lax.broadcasted_iota(I32, (N_EXPERTS, LANES), 0)
        scan = padded
        for d in (1, 2, 4, 8, 16):
            scan = scan + jnp.where(e >= d, pltpu.roll(scan, d, axis=0), 0)
        pstart[...] = (scan - padded)[:, 0:1].astype(F32)
        carry[...] = jnp.zeros_like(carry)

    before = jnp.dot(both.astype(BF16), tri_ref[...], preferred_element_type=F32)
    row_of = before + (carry[...] + pstart[...])
    r1 = jnp.sum(oh1 * row_of, axis=0, keepdims=True)
    r2 = jnp.sum(oh2 * row_of, axis=0, keepdims=True)
    sub = lax.broadcasted_iota(I32, dest_ref.shape, 0)
    dest_ref[...] = jnp.where(sub == 0, r1, jnp.where(sub == 1, r2, 0.0)).astype(I32)
    carry[...] += jnp.sum(both, axis=1, keepdims=True)


def _route(info_t, cnt, cols):
    n = info_t.shape[1]
    tri = jnp.asarray(np.triu(np.ones((cols, cols), np.float32), 1), BF16)
    return pl.pallas_call(
        _route_kernel,
        grid=(n // cols,),
        in_specs=[pl.BlockSpec((SUBLANES, cols), lambda t: (0, t)),
                  pl.BlockSpec((N_EXPERTS, LANES), lambda t: (0, 0)),
                  pl.BlockSpec((cols, cols), lambda t: (0, 0))],
        out_specs=pl.BlockSpec((SUBLANES, cols), lambda t: (0, t)),
        out_shape=jax.ShapeDtypeStruct((SUBLANES, n), I32),
        scratch_shapes=[pltpu.VMEM((N_EXPERTS, 1), F32), pltpu.VMEM((N_EXPERTS, 1), F32)],
        compiler_params=_cparams(1),
        name="route",
    )(info_t, cnt, tri)


def _sc_mesh():
    return plsc.VectorSubcoreMesh(core_axis_name="core", subcore_axis_name="subcore")


def _sc_worker_id():
    return lax.axis_index("subcore") * SC_CORES + lax.axis_index("core")


def _sc_scatter_rows(rows, d0, d1, cap):
    n, width = rows.shape
    per_worker = n // SC_WORKERS
    n_win = per_worker // SC_WINDOW

    def body(x_hbm, d0_hbm, d1_hbm, o_hbm, i0_v, i1_v, rows_v, rsem, sem0, sem1):
        wid = _sc_worker_id()
        pltpu.sync_copy(d0_hbm.at[wid], i0_v)
        pltpu.sync_copy(d1_hbm.at[wid], i1_v)

        def read(j):
            src = x_hbm.at[pl.ds(wid * per_worker + j * SC_WINDOW, SC_WINDOW)]
            return pltpu.make_async_copy(src, rows_v.at[j % 2], rsem.at[j % 2])

        def scatters(j):
            return (pltpu.make_async_copy(rows_v.at[j % 2], o_hbm.at[i0_v.at[j]], sem0.at[j % 2]),
                    pltpu.make_async_copy(rows_v.at[j % 2], o_hbm.at[i1_v.at[j]], sem1.at[j % 2]))

        read(0).start()
        for j in range(n_win):
            if j + 1 < n_win:
                if j >= 1:
                    for cp in scatters(j - 1):
                        cp.wait()
                read(j + 1).start()
            read(j).wait()
            for cp in scatters(j):
                cp.start()
        for j in range(max(n_win - 2, 0), n_win):
            for cp in scatters(j):
                cp.wait()

    return pl.kernel(
        body,
        out_type=jax.ShapeDtypeStruct((cap, width), rows.dtype),
        mesh=_sc_mesh(),
        scratch_types=[pltpu.VMEM((n_win, SC_WINDOW), I32), pltpu.VMEM((n_win, SC_WINDOW), I32),
                       pltpu.VMEM((2, SC_WINDOW, width), rows.dtype),
                       pltpu.SemaphoreType.DMA((2,)), pltpu.SemaphoreType.DMA((2,)),
                       pltpu.SemaphoreType.DMA((2,))],
        name="dispatch",
    )(rows, d0, d1)


def _sc_gather_rows(table, idx):
    width = table.shape[1]
    n_win = idx.shape[1]
    per_worker = n_win * SC_WINDOW

    def body(y_hbm, i_hbm, o_hbm, i_v, rows_v, gsem, wsem):
        wid = _sc_worker_id()
        pltpu.sync_copy(i_hbm.at[wid], i_v)

        def gather(j):
            return pltpu.make_async_copy(y_hbm.at[i_v.at[j]], rows_v.at[j % 2], gsem.at[j % 2])

        def write(j):
            dst = o_hbm.at[pl.ds(wid * per_worker + j * SC_WINDOW, SC_WINDOW)]
            return pltpu.make_async_copy(rows_v.at[j % 2], dst, wsem.at[j % 2])

        gather(0).start()
        for j in range(n_win):
            if j + 1 < n_win:
                if j >= 1:
                    write(j - 1).wait()
                gather(j + 1).start()
            gather(j).wait()
            write(j).start()
        for j in range(max(n_win - 2, 0), n_win):
            write(j).wait()

    return pl.kernel(
        body,
        out_type=jax.ShapeDtypeStruct((SC_WORKERS * per_worker, width), table.dtype),
        mesh=_sc_mesh(),
        scratch_types=[pltpu.VMEM((n_win, SC_WINDOW), I32),
                       pltpu.VMEM((2, SC_WINDOW, width), table.dtype),
                       pltpu.SemaphoreType.DMA((2,)), pltpu.SemaphoreType.DMA((2,))],
        name="collect",
    )(table, idx)


def _expert_kernel(bstart_ref, nblk_ref, nused_ref, xs_hbm, wg_ref, wu_ref, wd_ref, yb_hbm,
                   xbuf, ybuf, zbuf, xsem, ysem, zsem, wg_b, wu_b, wd_b):
    e = pl.program_id(0)
    nused = nused_ref[0]
    n_blocks = yb_hbm.shape[0] // MOE_BLOCK

    def rows(b):
        return pl.ds(pl.multiple_of(b * MOE_BLOCK, MOE_BLOCK), MOE_BLOCK)

    def x_copy(b):
        slot = b % X_RING
        return pltpu.make_async_copy(xs_hbm.at[rows(b)], xbuf.at[slot], xsem.at[slot])

    def y_copy(b):
        slot = b % Y_RING
        return pltpu.make_async_copy(ybuf.at[slot], yb_hbm.at[rows(b)], ysem.at[slot])

    @pl.when(e == 0)
    def _():
        for b in range(X_AHEAD):
            @pl.when(b < nused)
            def _():
                x_copy(b).start()

    wg_b[...] = wg_ref[...].astype(BF16)
    wu_b[...] = wu_ref[...].astype(BF16)
    wd_b[...] = wd_ref[...].astype(BF16)

    def run_blocks(b, count):
        for k in range(count):
            x_copy(b + k).wait()
        for k in range(count):
            nxt = b + X_AHEAD + k

            @pl.when(nxt < nused)
            def _():
                x_copy(nxt).start()

            @pl.when(b + k >= Y_RING)
            def _():
                y_copy(b + k - Y_RING).wait()

        words = jnp.concatenate([xbuf[(b + k) % X_RING] for k in range(count)], axis=0)
        lo, hi = _unpack_rows(words)
        lo = lo.astype(BF16)
        hi = hi.astype(BF16)
        g = (jnp.dot(lo, wg_b[0:PACKED, :], preferred_element_type=F32)
             + jnp.dot(hi, wg_b[PACKED:, :], preferred_element_type=F32))
        u = (jnp.dot(lo, wu_b[0:PACKED, :], preferred_element_type=F32)
             + jnp.dot(hi, wu_b[PACKED:, :], preferred_element_type=F32))
        mid = (g * _sigmoid(g) * u).astype(BF16)
        y = _pack_rows(jnp.dot(mid, wd_b[...], preferred_element_type=F32))
        for k in range(count):
            ybuf[(b + k) % Y_RING] = y[k * MOE_BLOCK:(k + 1) * MOE_BLOCK]
            y_copy(b + k).start()

    b0 = bstart_ref[e]
    nb = nblk_ref[e]

    def pair(i, carry):
        run_blocks(b0 + 2 * i, 2)
        return carry

    lax.fori_loop(0, nb // 2, pair, 0)

    @pl.when(nb % 2 == 1)
    def _():
        run_blocks(b0 + nb - 1, 1)

    @pl.when(e == pl.num_programs(0) - 1)
    def _():
        for back in range(Y_RING, 0, -1):
            @pl.when(nused >= back)
            def _():
                y_copy(nused - back).wait()

        zbuf[...] = jnp.zeros_like(zbuf)

        def z_copy(b):
            return pltpu.make_async_copy(zbuf, yb_hbm.at[rows(b)], zsem.at[0])

        def z_start(b, carry):
            z_copy(b).start()
            return carry

        def z_wait(b, carry):
            z_copy(b).wait()
            return carry

        lax.fori_loop(nused, n_blocks, z_start, 0)
        lax.fori_loop(nused, n_blocks, z_wait, 0)


def _experts(bstart, nblk, nused, xs, w_gate, w_up, w_down):
    cap = xs.shape[0]
    w_idx = lambda e, bs, nb, nu: (e, 0, 0)
    grid_spec = pltpu.PrefetchScalarGridSpec(
        num_scalar_prefetch=3,
        grid=(N_EXPERTS,),
        in_specs=[pl.BlockSpec(memory_space=pl.ANY),
                  pl.BlockSpec((None, D_MODEL, D_FF), w_idx),
                  pl.BlockSpec((None, D_MODEL, D_FF), w_idx),
                  pl.BlockSpec((None, D_FF, D_MODEL), w_idx)],
        out_specs=pl.BlockSpec(memory_space=pl.ANY),
        scratch_shapes=[pltpu.VMEM((X_RING, MOE_BLOCK, PACKED), U32),
                        pltpu.VMEM((Y_RING, MOE_BLOCK, PACKED), U32),
                        pltpu.VMEM((MOE_BLOCK, PACKED), U32),
                        pltpu.SemaphoreType.DMA((X_RING,)),
                        pltpu.SemaphoreType.DMA((Y_RING,)),
                        pltpu.SemaphoreType.DMA((1,)),
                        pltpu.VMEM((D_MODEL, D_FF), BF16),
                        pltpu.VMEM((D_MODEL, D_FF), BF16),
                        pltpu.VMEM((D_FF, D_MODEL), BF16)])
    return pl.pallas_call(
        _expert_kernel,
        grid_spec=grid_spec,
        out_shape=jax.ShapeDtypeStruct((cap, PACKED), U32),
        compiler_params=_cparams(1),
        name="experts",
    )(bstart, nblk, nused, xs, w_gate, w_up, w_down)


def _combine_kernel(y0_ref, y1_ref, h_ref, info_ref, g_ref, b_ref, o_ref):
    info = info_ref[...].T
    g0 = info[:, 2:3]
    g1 = info[:, 3:4]
    lo0, hi0 = _unpack_rows(y0_ref[...])
    lo1, hi1 = _unpack_rows(y1_ref[...])
    y = jnp.concatenate([g0 * lo0 + g1 * lo1, g0 * hi0 + g1 * hi1], axis=1)
    o_ref[...] = _layer_norm(ALPHA * h_ref[...] + y, g_ref[...], b_ref[...])


def _combine(ys, h, info, ln_g, ln_b, rows, part, n_parts):
    n = h.shape[0]
    steps = n // n_parts // rows
    off = part * steps
    const = lambda i: (0, 0)
    return pl.pallas_call(
        _combine_kernel,
        grid=(steps,),
        in_specs=[pl.BlockSpec((rows, PACKED), lambda i: (i, 0)),
                  pl.BlockSpec((rows, PACKED), lambda i: (i + steps, 0)),
                  pl.BlockSpec((rows, D_MODEL), lambda i: (i + off, 0)),
                  pl.BlockSpec((SUBLANES, rows), lambda i: (0, i + off)),
                  pl.BlockSpec((1, D_MODEL), const),
                  pl.BlockSpec((1, D_MODEL), const)],
        out_specs=pl.BlockSpec((rows, D_MODEL), lambda i: (i + off, 0)),
        out_shape=jax.ShapeDtypeStruct((n, D_MODEL), F32),
        input_output_aliases={2: 0},
        compiler_params=_cparams(1),
        name="combine",
    )(ys, ys, h, info, ln_g, ln_b)


def _alibi_bias():
    qi = np.arange(BLOCK)[:, None]
    kj = np.arange(2 * BLOCK)[None, :]
    dist = qi - kj + BLOCK
    band = (dist >= 0) & (dist < BLOCK)
    slopes = np.exp2(-8.0 * np.arange(1, N_Q_HEADS + 1, dtype=np.float32) / N_Q_HEADS)
    bias = np.where(band[None], -slopes[:, None, None] * dist[None].astype(np.float32), NEG)
    bias = bias * LOG2E
    first = np.where((kj >= PAD_FRONT)[None], bias, NEG)
    out = np.empty((2, N_KV_HEADS, 2 * BLOCK, 4 * BLOCK), np.float32)
    for v, per_head in enumerate((first, bias)):
        for j in range(N_KV_HEADS):
            out[v, j] = np.block([[per_head[4 * j], per_head[4 * j + 1]],
                                  [per_head[4 * j + 2], per_head[4 * j + 3]]])
    return jnp.asarray(out, F32)


def _slab_gates(w):
    nb, c, _ = w.shape
    per = LANES // c
    w = w.reshape(nb // per, per, c, c)
    eye = jnp.eye(per, dtype=w.dtype)
    return jnp.einsum('spcd,pq->spcqd', w, eye).reshape(nb // per, LANES, LANES).astype(BF16)


def kernel(x, meta_tokens, w_in, conv_w, conv_b, lru_wa, lru_ba, lru_wx, lru_bx, lru_lambda,
           attn_sinks, g_attn, g_lru, w_out, ln1_g, ln1_b, w_group, b_group, w_router,
           b_router, w_gate, w_up, w_down, ln2_g, ln2_b):
    bsz, seq, d = x.shape
    nbx = seq // BLOCK
    n_tok = bsz * seq
    x2d = x.reshape(n_tok, d)
    row = lambda v: v.reshape(1, -1).astype(F32)

    q_scale = jnp.concatenate([jnp.full((ATTN_WIDTH,), LOG2E * HEAD_DIM ** -0.5, F32),
                               jnp.ones((IN_COLS - ATTN_WIDTH,), F32)])
    w_in_b = (w_in[0] * q_scale).astype(BF16)
    meta_blk = jnp.concatenate([jnp.zeros((PAD_FRONT, d), F32), meta_tokens.astype(F32)], axis=0)
    q, kv, xr, yr = _in_proj(x2d, w_in_b, PROJ_ROWS)
    qm, kvm, xrm, yrm = _in_proj(meta_blk, w_in_b, BLOCK)
    shp = lambda a: a.reshape(bsz, seq, a.shape[-1])

    attn_n = _attention(attn_sinks[0].astype(F32) * LOG2E, shp(q), shp(kv), kvm, _alibi_bias(),
                        row(g_attn[0]), bsz, nbx)
    lru_n = _rglru(shp(xr), shp(yr), xrm, yrm, conv_w[0].astype(F32), row(conv_b[0]),
                   _slab_gates(lru_wa[0]), _slab_gates(lru_wx[0]),
                   row(lru_ba[0]), row(lru_bx[0]), row(lru_lambda[0]), row(g_lru[0]), bsz, nbx)

    w_out_b = w_out[0].astype(BF16)
    gpad = SUBLANES - N_GROUPS
    w_rt = jnp.concatenate(
        [w_group[0].T, jnp.zeros((gpad, d), F32),
         jnp.transpose(w_router[0], (0, 2, 1)).reshape(N_EXPERTS, d)], axis=0).astype(F32)
    w_rt_hi = w_rt.astype(BF16)
    w_rt_lo = (w_rt - w_rt_hi.astype(F32)).astype(BF16)
    b_rt = jnp.concatenate([b_group[0], jnp.zeros((gpad,), F32),
                            b_router[0].reshape(-1)]).astype(F32).reshape(ROUTER_ROWS, 1)
    h1, hp, info, cnt = _out_proj(
        attn_n.reshape(n_tok, ATTN_WIDTH), lru_n.reshape(n_tok, LRU_WIDTH), x2d,
        w_out_b[:ATTN_WIDTH], w_out_b[ATTN_WIDTH:], row(ln1_g[0]), row(ln1_b[0]),
        w_rt_hi, w_rt_lo, b_rt, OUT_PROJ_ROWS)

    dest = _route(info, cnt, ROUTE_ROWS)
    n_slots = n_tok * TOP_K
    n_blocks = n_slots // MOE_BLOCK + N_EXPERTS
    cap = n_blocks * MOE_BLOCK
    nblk = (cnt[:, 0].astype(I32) + MOE_BLOCK - 1) // MOE_BLOCK
    bends = jnp.cumsum(nblk)
    bstart = (bends - nblk).astype(I32)
    nused = bends[-1:].astype(I32)
    windows = lambda v: v.reshape(SC_WORKERS, -1, SC_WINDOW)
    d0 = dest[0]
    d1 = dest[1]

    xs = _sc_scatter_rows(hp, windows(d0), windows(d1), cap)
    yb = _experts(bstart, nblk.astype(I32), nused, xs, w_gate[0], w_up[0], w_down[0])
    out = h1
    part_len = n_tok // COMBINE_PARTS
    for part in range(COMBINE_PARTS):
        tok = slice(part * part_len, (part + 1) * part_len)
        ys = _sc_gather_rows(yb, windows(jnp.concatenate([d0[tok], d1[tok]])))
        out = _combine(ys, out, info, row(ln2_g[0]), row(ln2_b[0]), COMBINE_ROWS,
                       part, COMBINE_PARTS)
    return out.reshape(bsz, seq, d)
```

```python
import jax
import jax.numpy as jnp
import numpy as np
from jax import lax
from jax.experimental import pallas as pl
from jax.experimental.pallas import tpu as pltpu
from jax.experimental.pallas import tpu_sc as plsc

F32 = jnp.float32
BF16 = jnp.bfloat16
U32 = jnp.uint32
I32 = jnp.int32

D_MODEL = 1024
N_META = 16
BLOCK = 128
PAD_FRONT = BLOCK - N_META
HEAD_DIM = 64
ATTN_WIDTH = 512
LRU_WIDTH = 512
N_Q_HEADS = 8
N_KV_HEADS = 2
KV_WIDTH = N_KV_HEADS * HEAD_DIM
LRU_BLOCKS = 8
CONV_W = 4
LRU_C = 8.0
IN_COLS = ATTN_WIDTH + 2 * KV_WIDTH + 2 * LRU_WIDTH
N_GROUPS = 4
EXPERTS_PER_GROUP = 8
N_EXPERTS = N_GROUPS * EXPERTS_PER_GROUP
TOP_K = 2
D_FF = 512
MOE_BLOCK = 256
ALPHA = 2.0 ** 0.25
EPS = 1e-5
NEG = -1e30
LOG2E = float(np.log2(np.e))
LANES = 128
SUBLANES = 8
PACKED = D_MODEL // 2

PROJ_ROWS = 512
OUT_PROJ_ROWS = 1024
ROUTE_ROWS = 512
COMBINE_ROWS = 512
COMBINE_PARTS = 4
X_GROUP = 4
X_AHEAD = 4
X_RING = X_AHEAD + X_GROUP
Y_RING = 2 * X_GROUP
VMEM_LIMIT = 48 * 1024 * 1024

SC_CORES = 2
SC_SUBCORES = 16
SC_WORKERS = SC_CORES * SC_SUBCORES
SC_WINDOW = 64


def _cparams(n_axes):
    return pltpu.CompilerParams(
        dimension_semantics=("arbitrary",) * n_axes, vmem_limit_bytes=VMEM_LIMIT)


def _in_proj_kernel(x_ref, w_ref, q_ref, kv_ref, xr_ref, yr_ref):
    proj = jnp.dot(x_ref[...].astype(BF16), w_ref[...], preferred_element_type=F32)
    o = 0
    for ref, width in ((q_ref, ATTN_WIDTH), (kv_ref, 2 * KV_WIDTH),
                       (xr_ref, LRU_WIDTH), (yr_ref, LRU_WIDTH)):
        ref[...] = proj[:, o:o + width].astype(ref.dtype)
        o += width


def _in_proj(x2d, w_bf16, rows):
    n = x2d.shape[0]
    widths = (ATTN_WIDTH, 2 * KV_WIDTH, LRU_WIDTH, LRU_WIDTH)
    return pl.pallas_call(
        _in_proj_kernel,
        grid=(n // rows,),
        in_specs=[pl.BlockSpec((rows, D_MODEL), lambda i: (i, 0)),
                  pl.BlockSpec((D_MODEL, IN_COLS), lambda i: (0, 0))],
        out_specs=[pl.BlockSpec((rows, w), lambda i: (i, 0)) for w in widths],
        out_shape=[jax.ShapeDtypeStruct((n, w), BF16) for w in widths],
        compiler_params=_cparams(1),
        name="in_proj",
    )(x2d, w_bf16)


def _attn_kernel(sinks_ref, q_ref, kv_ref, kvm_ref, bias_ref, g_ref, o_ref,
                 klo, khi, vlo, vhi):
    nbx = q_ref.shape[0] // BLOCK
    lo_lanes = lax.broadcasted_iota(I32, (BLOCK // 2, LANES), 1) < HEAD_DIM

    def layout_block(n, blk):
        rows = pl.ds(pl.multiple_of(n * BLOCK, BLOCK), BLOCK)
        as_bf16 = lambda words: pltpu.bitcast(words, BF16)
        for src, dst_lo, dst_hi in ((blk[:, :KV_WIDTH], klo, khi), (blk[:, KV_WIDTH:], vlo, vhi)):
            w = pltpu.bitcast(src, U32)
            r = pltpu.roll(w, HEAD_DIM, axis=1)
            zero = jnp.zeros_like(w)
            dst_lo[0, rows, :] = as_bf16(jnp.where(lo_lanes, w, zero))
            dst_hi[0, rows, :] = as_bf16(jnp.where(lo_lanes, zero, r))
            dst_lo[1, rows, :] = as_bf16(jnp.where(lo_lanes, r, zero))
            dst_hi[1, rows, :] = as_bf16(jnp.where(lo_lanes, zero, w))

    layout_block(0, kvm_ref[...])

    def layout_body(n, carry):
        layout_block(n + 1, kv_ref[pl.ds(pl.multiple_of(n * BLOCK, BLOCK), BLOCK), :])
        return carry

    lax.fori_loop(0, nbx, layout_body, 0, unroll=4)

    ones_lo = jnp.where(lax.broadcasted_iota(I32, (2 * BLOCK, LANES), 1) < HEAD_DIM,
                        1.0, 0.0).astype(BF16)
    ones_hi = (1.0 - ones_lo.astype(F32)).astype(BF16)
    top_rows = lax.broadcasted_iota(I32, (2 * BLOCK, 1), 0) < BLOCK
    lo_half = lax.broadcasted_iota(I32, (2 * BLOCK, LANES), 1) < HEAD_DIM

    def block(i, carry):
        q_rows = pl.ds(pl.multiple_of(i * BLOCK, BLOCK), BLOCK)
        win = pl.ds(pl.multiple_of(i * BLOCK, BLOCK), 2 * BLOCK)
        q = q_ref[q_rows, :]
        first = jnp.minimum(i, 1)
        outs = []
        for j in range(N_KV_HEADS):
            q2 = jnp.concatenate([q[:, (2 * j) * LANES:(2 * j + 1) * LANES],
                                  q[:, (2 * j + 1) * LANES:(2 * j + 2) * LANES]], axis=0)
            kc = jnp.concatenate([klo[j, win, :], khi[j, win, :]], axis=0)
            s = lax.dot_general(q2, kc, (((1,), (1,)), ((), ())), preferred_element_type=F32)
            s = s + bias_ref[first, j]
            ps, es = [], []
            for c in range(2):
                sink = jnp.where(top_rows, sinks_ref[4 * j + c], sinks_ref[4 * j + 2 + c])
                sc = s[:, c * 2 * BLOCK:(c + 1) * 2 * BLOCK]
                m = jnp.maximum(jnp.max(sc, axis=1, keepdims=True), sink)
                ps.append(jnp.exp2(sc - m).astype(BF16))
                es.append(jnp.exp2(sink - m))
            v_lo = jnp.concatenate([vlo[j, win, :], ones_lo], axis=1)
            v_hi = jnp.concatenate([vhi[j, win, :], ones_hi], axis=1)
            r = (jnp.dot(ps[0], v_lo, preferred_element_type=F32)
                 + jnp.dot(ps[1], v_hi, preferred_element_type=F32))
            den = r[:, LANES:] + jnp.where(lo_half, es[0], es[1])
            o2 = r[:, :LANES] * (1.0 / den)
            outs += [o2[:BLOCK], o2[BLOCK:]]
        out = jnp.concatenate(outs, axis=1)
        ms = jnp.mean(out * out, axis=1, keepdims=True)
        o_ref[q_rows, :] = (out * lax.rsqrt(ms + EPS) * g_ref[...]).astype(o_ref.dtype)
        return carry

    lax.fori_loop(0, nbx, block, 0, unroll=4)


def _attention(sinks, q, kv, kvm, bias, g_attn, bsz, nbx):
    seq = nbx * BLOCK
    const2 = lambda b: (0, 0)
    kv_scratch = pltpu.VMEM((N_KV_HEADS, seq + BLOCK, LANES), BF16)
    return pl.pallas_call(
        _attn_kernel,
        grid=(bsz,),
        in_specs=[pl.BlockSpec(memory_space=pltpu.SMEM),
                  pl.BlockSpec((None, seq, ATTN_WIDTH), lambda b: (b, 0, 0)),
                  pl.BlockSpec((None, seq, 2 * KV_WIDTH), lambda b: (b, 0, 0)),
                  pl.BlockSpec((BLOCK, 2 * KV_WIDTH), const2),
                  pl.BlockSpec((2, N_KV_HEADS, 2 * BLOCK, 4 * BLOCK), lambda b: (0, 0, 0, 0)),
                  pl.BlockSpec((1, ATTN_WIDTH), const2)],
        out_specs=pl.BlockSpec((None, seq, ATTN_WIDTH), lambda b: (b, 0, 0)),
        out_shape=jax.ShapeDtypeStruct((bsz, seq, ATTN_WIDTH), BF16),
        scratch_shapes=[kv_scratch, kv_scratch, kv_scratch, kv_scratch],
        compiler_params=_cparams(1),
        name="attention",
    )(sinks, q, kv, kvm, bias, g_attn)


def _sigmoid(v):
    return 0.5 * jnp.tanh(0.5 * v) + 0.5


def _gelu_tanh(y):
    c = float(np.sqrt(2.0 / np.pi))
    half = 0.5 * y
    return half + half * jnp.tanh(y * (c + (c * 0.044715) * (y * y)))


LRU_CHUNK = 44
LRU_SEG = SUBLANES * LRU_CHUNK
LRU_SLABS = LRU_WIDTH // LANES


def _lru_kernel(xr_ref, yr_ref, xrm_ref, yrm_ref, cw_ref, cb_ref, wa_ref, wx_ref, ba_ref,
                bx_ref, lam_ref, g_ref, o_ref, xp, yp, op, x_st, y_st, o_st, s_st, xtail, hcar):
    seq = xr_ref.shape[0]
    n_seg = (seq + BLOCK) // LRU_SEG
    xp[0:BLOCK, :] = xrm_ref[...]
    xp[BLOCK:, :] = xr_ref[...]
    yp[0:BLOCK, :] = yrm_ref[...]
    yp[BLOCK:, :] = yr_ref[...]
    xtail[...] = jnp.zeros_like(xtail)
    hcar[...] = jnp.zeros_like(hcar)
    lam = lam_ref[...]
    softplus_neg = jnp.maximum(-lam, 0.0) + jnp.log(1.0 + jnp.exp(-jnp.abs(lam)))
    sub = lax.broadcasted_iota(jnp.int32, (SUBLANES, LANES), 0)

    def strided(j):
        return pl.ds(j, SUBLANES, stride=LRU_CHUNK)

    def piece(v, j):
        return v[j * SUBLANES:(j + 1) * SUBLANES, :]

    def segment(k, has_padding):
        start = k * LRU_SEG
        if not isinstance(start, int):
            start = pl.multiple_of(start, 2 * SUBLANES)
        rows = pl.ds(start, LRU_SEG)
        x_nat = xp[rows, :].astype(F32)
        y_nat = yp[rows, :].astype(F32)
        for c in range(LRU_SLABS):
            x_st[c] = x_nat[:, c * LANES:(c + 1) * LANES]
            y_st[c] = y_nat[:, c * LANES:(c + 1) * LANES]
        first_row = k * LRU_SEG + LRU_CHUNK * sub
        sumsq = [jnp.zeros((SUBLANES, LANES), F32) for _ in range(LRU_CHUNK)]

        for c in range(LRU_SLABS):
            lanes = slice(c * LANES, (c + 1) * LANES)
            x = [x_st[c, strided(j), :] for j in range(LRU_CHUNK)]
            before = []
            for d in range(1, CONV_W):
                from_prev_chunk = pltpu.roll(x[LRU_CHUNK - d], 1, axis=0)
                before.append(jnp.where(sub == 0, xtail[d - 1:d, lanes], from_prev_chunk))
            for d in range(1, CONV_W):
                xtail[d - 1:d, lanes] = x[LRU_CHUNK - d][SUBLANES - 1:SUBLANES, :]

            def x_at(j):
                return x[j] if j >= 0 else before[-j - 1]

            taps = [cw_ref[t:t + 1, lanes] for t in range(CONV_W)]
            bias = cb_ref[:, lanes]
            xc = jnp.concatenate(
                [bias + sum(taps[t] * x_at(j - (CONV_W - 1) + t) for t in range(CONV_W))
                 for j in range(LRU_CHUNK)], axis=0)
            xcb = xc.astype(BF16)
            r = _sigmoid(jnp.dot(xcb, wa_ref[c], preferred_element_type=F32) + ba_ref[:, lanes])
            i = _sigmoid(jnp.dot(xcb, wx_ref[c], preferred_element_type=F32) + bx_ref[:, lanes])
            a = jnp.exp(-LRU_C * r * softplus_neg[:, lanes])
            z = 1.0 - a * a
            u = jnp.where(z > 0.0, z * lax.rsqrt(z), 0.0) * (i * xc)

            h = jnp.zeros((SUBLANES, LANES), F32)
            p = jnp.ones((SUBLANES, LANES), F32)
            hs, ps = [], []
            for j in range(LRU_CHUNK):
                aj = piece(a, j)
                uj = piece(u, j)
                if has_padding:
                    uj = jnp.where(first_row + j >= PAD_FRONT, uj, 0.0)
                h = aj * h + uj
                p = aj * p
                hs.append(h)
                ps.append(p)
            entry = [hcar[:, lanes]]
            for s in range(SUBLANES):
                entry.append(h[s:s + 1, :] + p[s:s + 1, :] * entry[s])
            hcar[:, lanes] = entry[SUBLANES]
            entry_rows = jnp.concatenate(entry[:SUBLANES], axis=0)

            for j in range(LRU_CHUNK):
                state = hs[j] + ps[j] * entry_rows
                out = state * _gelu_tanh(y_st[c, strided(j), :])
                sumsq[j] = sumsq[j] + out * out
                o_st[c, strided(j), :] = out

        for j in range(LRU_CHUNK):
            ms = jnp.sum(sumsq[j], axis=1, keepdims=True) * (1.0 / LRU_WIDTH)
            s_st[strided(j), :] = jnp.broadcast_to(lax.rsqrt(ms + EPS), (SUBLANES, LANES))
        scale = s_st[...]
        for c in range(LRU_SLABS):
            lanes = slice(c * LANES, (c + 1) * LANES)
            op[rows, lanes] = (o_st[c] * scale * g_ref[:, lanes]).astype(op.dtype)

    assert PAD_FRONT <= LRU_SEG
    segment(0, True)

    def later_segment(k, carry):
        segment(k, False)
        return carry

    lax.fori_loop(1, n_seg, later_segment, 0)
    o_ref[...] = op[BLOCK:, :]


def _rglru(xr, yr, xrm, yrm, cw, cb, wa, wx, ba, bx, lam, g_lru, bsz, nbx):
    seq = nbx * BLOCK
    assert (seq + BLOCK) % LRU_SEG == 0
    main = pl.BlockSpec((None, seq, LRU_WIDTH), lambda b: (b, 0, 0))
    const2 = lambda b: (0, 0)
    row_spec = pl.BlockSpec((1, LRU_WIDTH), const2)
    gate_spec = pl.BlockSpec((LRU_SLABS, LANES, LANES), lambda b: (0, 0, 0))
    padded = pltpu.VMEM((seq + BLOCK, LRU_WIDTH), BF16)
    slabs = pltpu.VMEM((LRU_SLABS, LRU_SEG, LANES), F32)
    return pl.pallas_call(
        _lru_kernel,
        grid=(bsz,),
        in_specs=[main, main,
                  pl.BlockSpec((BLOCK, LRU_WIDTH), const2),
                  pl.BlockSpec((BLOCK, LRU_WIDTH), const2),
                  pl.BlockSpec((CONV_W, LRU_WIDTH), const2),
                  row_spec, gate_spec, gate_spec,
                  row_spec, row_spec, row_spec, row_spec],
        out_specs=main,
        out_shape=jax.ShapeDtypeStruct((bsz, seq, LRU_WIDTH), BF16),
        scratch_shapes=[padded, padded, padded, slabs, slabs, slabs,
                        pltpu.VMEM((LRU_SEG, LANES), F32),
                        pltpu.VMEM((SUBLANES, LRU_WIDTH), F32),
                        pltpu.VMEM((1, LRU_WIDTH), F32)],
        compiler_params=_cparams(1),
        name="rglru",
    )(xr, yr, xrm, yrm, cw, cb, wa, wx, ba, bx, lam, g_lru)


def _pack_rows(v):
    bits = lax.bitcast_convert_type(v.astype(BF16).astype(F32), U32)
    return (bits[:, :PACKED] >> 16) | (bits[:, PACKED:] & jnp.uint32(0xFFFF0000))


def _unpack_rows(w):
    lo = lax.bitcast_convert_type(w << 16, F32)
    hi = lax.bitcast_convert_type(w & jnp.uint32(0xFFFF0000), F32)
    return lo, hi


def _layer_norm(z, g, b):
    mu = jnp.mean(z, axis=1, keepdims=True)
    zc = z - mu
    var = jnp.mean(zc * zc, axis=1, keepdims=True)
    return zc * lax.rsqrt(var + EPS) * g + b


def _out_proj_kernel(a_ref, l_ref, x_ref, wa_ref, wl_ref, g_ref, b_ref, wrt_hi_ref, wrt_lo_ref,
                     brt_ref, h_ref, hp_ref, info_ref, cnt_ref):
    @pl.when(pl.program_id(0) == 0)
    def _():
        cnt_ref[...] = jnp.zeros_like(cnt_ref)

    for c in range(h_ref.shape[0] // PROJ_ROWS):
        rows = slice(c * PROJ_ROWS, (c + 1) * PROJ_ROWS)
        _out_proj_rows(a_ref[rows, :], l_ref[rows, :], x_ref[rows, :], wa_ref, wl_ref, g_ref,
                       b_ref, wrt_hi_ref, wrt_lo_ref, brt_ref,
                       h_ref.at[rows, :], hp_ref.at[rows, :], info_ref.at[:, rows], cnt_ref)


def _out_proj_rows(a, l, x, wa_ref, wl_ref, g_ref, b_ref, wrt_hi_ref, wrt_lo_ref, brt_ref,
                   h_ref, hp_ref, info_ref, cnt_ref):
    mix = jnp.dot(a, wa_ref[...], preferred_element_type=F32)
    mix = mix + jnp.dot(l, wl_ref[...], preferred_element_type=F32)
    h = _layer_norm(ALPHA * x + mix, g_ref[...], b_ref[...])
    h_ref[...] = h
    hp_ref[...] = _pack_rows(h)

    h_hi = h.astype(BF16)
    h_lo = (h - h_hi.astype(F32)).astype(BF16)
    nt = (((1,), (1,)), ((), ()))
    lg = (lax.dot_general(wrt_hi_ref[...], h_hi, nt, preferred_element_type=F32)
          + lax.dot_general(wrt_lo_ref[...], h_hi, nt, preferred_element_type=F32)
          + lax.dot_general(wrt_hi_ref[...], h_lo, nt, preferred_element_type=F32)) + brt_ref[...]
    tile_shape = (SUBLANES, h.shape[0])
    sub = lax.broadcasted_iota(I32, tile_shape, 0)
    ninf = -jnp.inf
    t0 = lg[0:SUBLANES]
    gl = jnp.where(sub < N_GROUPS, t0, ninf)
    gmax = jnp.max(gl, axis=0, keepdims=True)
    g_idx = jnp.min(jnp.where(gl == gmax, sub, SUBLANES), axis=0, keepdims=True)
    g_w = 1.0 / jnp.sum(jnp.where(sub < N_GROUPS, jnp.exp(t0 - gmax), 0.0),
                        axis=0, keepdims=True)
    el = lg[SUBLANES:2 * SUBLANES]
    for g in range(1, N_GROUPS):
        el = jnp.where(g_idx == g, lg[(g + 1) * SUBLANES:(g + 2) * SUBLANES], el)
    v1 = jnp.max(el, axis=0, keepdims=True)
    i1 = jnp.min(jnp.where(el == v1, sub, SUBLANES), axis=0, keepdims=True)
    el2 = jnp.where(sub == i1, ninf, el)
    v2 = jnp.max(el2, axis=0, keepdims=True)
    i2 = jnp.min(jnp.where(el2 == v2, sub, SUBLANES), axis=0, keepdims=True)
    t = jnp.exp(v2 - v1)
    w1 = 1.0 / (1.0 + t)
    w2 = t * w1
    e_base = g_idx * EXPERTS_PER_GROUP
    e1 = e_base + i1
    e2 = e_base + i2
    info_ref[...] = jnp.where(sub == 0, e1.astype(F32),
                              jnp.where(sub == 1, e2.astype(F32),
                                        jnp.where(sub == 2, g_w * w1,
                                                  jnp.where(sub == 3, g_w * w2, 0.0))))
    expert = lax.broadcasted_iota(I32, (N_EXPERTS, h.shape[0]), 0)
    chosen = (expert == e1).astype(F32) + (expert == e2).astype(F32)
    cnt_ref[...] += jnp.sum(chosen, axis=1, keepdims=True)


ROUTER_ROWS = (N_GROUPS + 1) * SUBLANES


def _out_proj(attn_n, lru_n, x2d, wo_a, wo_l, ln_g, ln_b, w_rt_hi, w_rt_lo, b_rt, rows):
    n = x2d.shape[0]
    const = lambda i: (0, 0)
    tile = lambda w: pl.BlockSpec((rows, w), lambda i: (i, 0))
    return pl.pallas_call(
        _out_proj_kernel,
        grid=(n // rows,),
        in_specs=[tile(ATTN_WIDTH), tile(LRU_WIDTH), tile(D_MODEL),
                  pl.BlockSpec((ATTN_WIDTH, D_MODEL), const),
                  pl.BlockSpec((LRU_WIDTH, D_MODEL), const),
                  pl.BlockSpec((1, D_MODEL), const),
                  pl.BlockSpec((1, D_MODEL), const),
                  pl.BlockSpec((ROUTER_ROWS, D_MODEL), const),
                  pl.BlockSpec((ROUTER_ROWS, D_MODEL), const),
                  pl.BlockSpec((ROUTER_ROWS, 1), const)],
        out_specs=[tile(D_MODEL), tile(PACKED),
                   pl.BlockSpec((SUBLANES, rows), lambda i: (0, i)),
                   pl.BlockSpec((N_EXPERTS, LANES), const)],
        out_shape=[jax.ShapeDtypeStruct((n, D_MODEL), F32),
                   jax.ShapeDtypeStruct((n, PACKED), U32),
                   jax.ShapeDtypeStruct((SUBLANES, n), F32),
                   jax.ShapeDtypeStruct((N_EXPERTS, LANES), F32)],
        compiler_params=_cparams(1),
        name="out_proj",
    )(attn_n, lru_n, x2d, wo_a, wo_l, ln_g, ln_b, w_rt_hi, w_rt_lo, b_rt)


def _route_kernel(info_ref, cnt_ref, tri_ref, dest_ref, carry, pstart):
    t = pl.program_id(0)
    info = info_ref[...]
    shape = (N_EXPERTS, info.shape[1])
    expert = lax.broadcasted_iota(I32, shape, 0)
    oh1 = (expert == info[0:1, :].astype(I32)).astype(F32)
    oh2 = (expert == info[1:2, :].astype(I32)).astype(F32)
    both = oh1 + oh2

    @pl.when(t == 0)
    def _():
        c = cnt_ref[...].astype(I32)
        padded = ((c + (MOE_BLOCK - 1)) // MOE_BLOCK) * MOE_BLOCK
        e = lax.broadcasted_iota(I32, (N_EXPERTS, LANES), 0)
        scan = padded
        for d in (1, 2, 4, 8, 16):
            scan = scan + jnp.where(e >= d, pltpu.roll(scan, d, axis=0), 0)
        pstart[...] = (scan - padded)[:, 0:1].astype(F32)
        carry[...] = jnp.zeros_like(carry)

    before = jnp.dot(both.astype(BF16), tri_ref[...], preferred_element_type=F32)
    row_of = before + (carry[...] + pstart[...])
    r1 = jnp.sum(oh1 * row_of, axis=0, keepdims=True)
    r2 = jnp.sum(oh2 * row_of, axis=0, keepdims=True)
    sub = lax.broadcasted_iota(I32, dest_ref.shape, 0)
    dest_ref[...] = jnp.where(sub == 0, r1, jnp.where(sub == 1, r2, 0.0)).astype(I32)
    carry[...] += jnp.sum(both, axis=1, keepdims=True)


def _route(info_t, cnt, cols):
    n = info_t.shape[1]
    tri = jnp.asarray(np.triu(np.ones((cols, cols), np.float32), 1), BF16)
    return pl.pallas_call(
        _route_kernel,
        grid=(n // cols,),
        in_specs=[pl.BlockSpec((SUBLANES, cols), lambda t: (0, t)),
                  pl.BlockSpec((N_EXPERTS, LANES), lambda t: (0, 0)),
                  pl.BlockSpec((cols, cols), lambda t: (0, 0))],
        out_specs=pl.BlockSpec((SUBLANES, cols), lambda t: (0, t)),
        out_shape=jax.ShapeDtypeStruct((SUBLANES, n), I32),
        scratch_shapes=[pltpu.VMEM((N_EXPERTS, 1), F32), pltpu.VMEM((N_EXPERTS, 1), F32)],
        compiler_params=_cparams(1),
        name="route",
    )(info_t, cnt, tri)


def _sc_mesh():
    return plsc.VectorSubcoreMesh(core_axis_name="core", subcore_axis_name="subcore")


def _sc_worker_id():
    return lax.axis_index("subcore") * SC_CORES + lax.axis_index("core")


def _sc_scatter_rows(rows, d0, d1, cap):
    n, width = rows.shape
    per_worker = n // SC_WORKERS
    n_win = per_worker // SC_WINDOW

    def body(x_hbm, d0_hbm, d1_hbm, o_hbm, i0_v, i1_v, rows_v, rsem, sem0, sem1):
        wid = _sc_worker_id()
        pltpu.sync_copy(d0_hbm.at[wid], i0_v)
        pltpu.sync_copy(d1_hbm.at[wid], i1_v)

        def read(j):
            src = x_hbm.at[pl.ds(wid * per_worker + j * SC_WINDOW, SC_WINDOW)]
            return pltpu.make_async_copy(src, rows_v.at[j % 2], rsem.at[j % 2])

        def scatters(j):
            return (pltpu.make_async_copy(rows_v.at[j % 2], o_hbm.at[i0_v.at[j]], sem0.at[j % 2]),
                    pltpu.make_async_copy(rows_v.at[j % 2], o_hbm.at[i1_v.at[j]], sem1.at[j % 2]))

        read(0).start()
        for j in range(n_win):
            if j + 1 < n_win:
                if j >= 1:
                    for cp in scatters(j - 1):
                        cp.wait()
                read(j + 1).start()
            read(j).wait()
            for cp in scatters(j):
                cp.start()
        for j in range(max(n_win - 2, 0), n_win):
            for cp in scatters(j):
                cp.wait()

    return pl.kernel(
        body,
        out_type=jax.ShapeDtypeStruct((cap, width), rows.dtype),
        mesh=_sc_mesh(),
        scratch_types=[pltpu.VMEM((n_win, SC_WINDOW), I32), pltpu.VMEM((n_win, SC_WINDOW), I32),
                       pltpu.VMEM((2, SC_WINDOW, width), rows.dtype),
                       pltpu.SemaphoreType.DMA((2,)), pltpu.SemaphoreType.DMA((2,)),
                       pltpu.SemaphoreType.DMA((2,))],
        name="dispatch",
    )(rows, d0, d1)


def _sc_gather_rows(table, idx):
    width = table.shape[1]
    n_win = idx.shape[1]
    per_worker = n_win * SC_WINDOW

    def body(y_hbm, i_hbm, o_hbm, i_v, rows_v, gsem, wsem):
        wid = _sc_worker_id()
        pltpu.sync_copy(i_hbm.at[wid], i_v)

        def gather(j):
            return pltpu.make_async_copy(y_hbm.at[i_v.at[j]], rows_v.at[j % 2], gsem.at[j % 2])

        def write(j):
            dst = o_hbm.at[pl.ds(wid * per_worker + j * SC_WINDOW, SC_WINDOW)]
            return pltpu.make_async_copy(rows_v.at[j % 2], dst, wsem.at[j % 2])

        gather(0).start()
        for j in range(n_win):
            if j + 1 < n_win:
                if j >= 1:
                    write(j - 1).wait()
                gather(j + 1).start()
            gather(j).wait()
            write(j).start()
        for j in range(max(n_win - 2, 0), n_win):
            write(j).wait()

    return pl.kernel(
        body,
        out_type=jax.ShapeDtypeStruct((SC_WORKERS * per_worker, width), table.dtype),
        mesh=_sc_mesh(),
        scratch_types=[pltpu.VMEM((n_win, SC_WINDOW), I32),
                       pltpu.VMEM((2, SC_WINDOW, width), table.dtype),
                       pltpu.SemaphoreType.DMA((2,)), pltpu.SemaphoreType.DMA((2,))],
        name="collect",
    )(table, idx)


def _expert_kernel(bstart_ref, nblk_ref, nused_ref, xs_hbm, wg_ref, wu_ref, wd_ref, yb_hbm,
                   xbuf, ybuf, zbuf, xsem, ysem, zsem, wg_b, wu_b, wd_b):
    e = pl.program_id(0)
    nused = nused_ref[0]
    n_blocks = yb_hbm.shape[0] // MOE_BLOCK

    def rows(b):
        return pl.ds(pl.multiple_of(b * MOE_BLOCK, MOE_BLOCK), MOE_BLOCK)

    def x_copy(b):
        slot = b % X_RING
        return pltpu.make_async_copy(xs_hbm.at[rows(b)], xbuf.at[slot], xsem.at[slot])

    def y_copy(b):
        slot = b % Y_RING
        return pltpu.make_async_copy(ybuf.at[slot], yb_hbm.at[rows(b)], ysem.at[slot])

    @pl.when(e == 0)
    def _():
        for b in range(X_AHEAD):
            @pl.when(b < nused)
            def _():
                x_copy(b).start()

    wg_b[...] = wg_ref[...].astype(BF16)
    wu_b[...] = wu_ref[...].astype(BF16)
    wd_b[...] = wd_ref[...].astype(BF16)

    def run_blocks(b, count):
        for k in range(count):
            x_copy(b + k).wait()
        for k in range(count):
            nxt = b + X_AHEAD + k

            @pl.when(nxt < nused)
            def _():
                x_copy(nxt).start()

            @pl.when(b + k >= Y_RING)
            def _():
                y_copy(b + k - Y_RING).wait()

        words = jnp.concatenate([xbuf[(b + k) % X_RING] for k in range(count)], axis=0)
        lo, hi = _unpack_rows(words)
        lo = lo.astype(BF16)
        hi = hi.astype(BF16)
        g = (jnp.dot(lo, wg_b[0:PACKED, :], preferred_element_type=F32)
             + jnp.dot(hi, wg_b[PACKED:, :], preferred_element_type=F32))
        u = (jnp.dot(lo, wu_b[0:PACKED, :], preferred_element_type=F32)
             + jnp.dot(hi, wu_b[PACKED:, :], preferred_element_type=F32))
        mid = (g * _sigmoid(g) * u).astype(BF16)
        y = _pack_rows(jnp.dot(mid, wd_b[...], preferred_element_type=F32))
        for k in range(count):
            ybuf[(b + k) % Y_RING] = y[k * MOE_BLOCK:(k + 1) * MOE_BLOCK]
            y_copy(b + k).start()

    b0 = bstart_ref[e]
    nb = nblk_ref[e]

    def group(i, carry):
        run_blocks(b0 + X_GROUP * i, X_GROUP)
        return carry

    lax.fori_loop(0, nb // X_GROUP, group, 0)
    done = nb - nb % X_GROUP
    size = X_GROUP // 2
    while size >= 1:
        @pl.when((nb // size) % 2 == 1)
        def _(size=size, done=done):
            run_blocks(b0 + done, size)

        done = done + (nb // size) % 2 * size
        size //= 2

    @pl.when(e == pl.num_programs(0) - 1)
    def _():
        for back in range(Y_RING, 0, -1):
            @pl.when(nused >= back)
            def _():
                y_copy(nused - back).wait()

        zbuf[...] = jnp.zeros_like(zbuf)

        def z_copy(b):
            return pltpu.make_async_copy(zbuf, yb_hbm.at[rows(b)], zsem.at[0])

        def z_start(b, carry):
            z_copy(b).start()
            return carry

        def z_wait(b, carry):
            z_copy(b).wait()
            return carry

        lax.fori_loop(nused, n_blocks, z_start, 0)
        lax.fori_loop(nused, n_blocks, z_wait, 0)


def _experts(bstart, nblk, nused, xs, w_gate, w_up, w_down):
    cap = xs.shape[0]
    w_idx = lambda e, bs, nb, nu: (e, 0, 0)
    grid_spec = pltpu.PrefetchScalarGridSpec(
        num_scalar_prefetch=3,
        grid=(N_EXPERTS,),
        in_specs=[pl.BlockSpec(memory_space=pl.ANY),
                  pl.BlockSpec((None, D_MODEL, D_FF), w_idx),
                  pl.BlockSpec((None, D_MODEL, D_FF), w_idx),
                  pl.BlockSpec((None, D_FF, D_MODEL), w_idx)],
        out_specs=pl.BlockSpec(memory_space=pl.ANY),
        scratch_shapes=[pltpu.VMEM((X_RING, MOE_BLOCK, PACKED), U32),
                        pltpu.VMEM((Y_RING, MOE_BLOCK, PACKED), U32),
                        pltpu.VMEM((MOE_BLOCK, PACKED), U32),
                        pltpu.SemaphoreType.DMA((X_RING,)),
                        pltpu.SemaphoreType.DMA((Y_RING,)),
                        pltpu.SemaphoreType.DMA((1,)),
                        pltpu.VMEM((D_MODEL, D_FF), BF16),
                        pltpu.VMEM((D_MODEL, D_FF), BF16),
                        pltpu.VMEM((D_FF, D_MODEL), BF16)])
    return pl.pallas_call(
        _expert_kernel,
        grid_spec=grid_spec,
        out_shape=jax.ShapeDtypeStruct((cap, PACKED), U32),
        compiler_params=_cparams(1),
        name="experts",
    )(bstart, nblk, nused, xs, w_gate, w_up, w_down)


def _combine_kernel(y0_ref, y1_ref, h_ref, info_ref, g_ref, b_ref, o_ref):
    info = info_ref[...].T
    g0 = info[:, 2:3]
    g1 = info[:, 3:4]
    lo0, hi0 = _unpack_rows(y0_ref[...])
    lo1, hi1 = _unpack_rows(y1_ref[...])
    y = jnp.concatenate([g0 * lo0 + g1 * lo1, g0 * hi0 + g1 * hi1], axis=1)
    o_ref[...] = _layer_norm(ALPHA * h_ref[...] + y, g_ref[...], b_ref[...])


def _combine(ys, h, info, ln_g, ln_b, rows, part, n_parts):
    n = h.shape[0]
    steps = n // n_parts // rows
    off = part * steps
    const = lambda i: (0, 0)
    return pl.pallas_call(
        _combine_kernel,
        grid=(steps,),
        in_specs=[pl.BlockSpec((rows, PACKED), lambda i: (i, 0)),
                  pl.BlockSpec((rows, PACKED), lambda i: (i + steps, 0)),
                  pl.BlockSpec((rows, D_MODEL), lambda i: (i + off, 0)),
                  pl.BlockSpec((SUBLANES, rows), lambda i: (0, i + off)),
                  pl.BlockSpec((1, D_MODEL), const),
                  pl.BlockSpec((1, D_MODEL), const)],
        out_specs=pl.BlockSpec((rows, D_MODEL), lambda i: (i + off, 0)),
        out_shape=jax.ShapeDtypeStruct((n, D_MODEL), F32),
        input_output_aliases={2: 0},
        compiler_params=_cparams(1),
        name="combine",
    )(ys, ys, h, info, ln_g, ln_b)


def _alibi_bias():
    qi = np.arange(BLOCK)[:, None]
    kj = np.arange(2 * BLOCK)[None, :]
    dist = qi - kj + BLOCK
    band = (dist >= 0) & (dist < BLOCK)
    slopes = np.exp2(-8.0 * np.arange(1, N_Q_HEADS + 1, dtype=np.float32) / N_Q_HEADS)
    bias = np.where(band[None], -slopes[:, None, None] * dist[None].astype(np.float32), NEG)
    bias = bias * LOG2E
    first = np.where((kj >= PAD_FRONT)[None], bias, NEG)
    out = np.empty((2, N_KV_HEADS, 2 * BLOCK, 4 * BLOCK), np.float32)
    for v, per_head in enumerate((first, bias)):
        for j in range(N_KV_HEADS):
            out[v, j] = np.block([[per_head[4 * j], per_head[4 * j + 1]],
                                  [per_head[4 * j + 2], per_head[4 * j + 3]]])
    return jnp.asarray(out, F32)


def _slab_gates(w):
    nb, c, _ = w.shape
    per = LANES // c
    w = w.reshape(nb // per, per, c, c)
    eye = jnp.eye(per, dtype=w.dtype)
    return jnp.einsum('spcd,pq->spcqd', w, eye).reshape(nb // per, LANES, LANES).astype(BF16)


def kernel(x, meta_tokens, w_in, conv_w, conv_b, lru_wa, lru_ba, lru_wx, lru_bx, lru_lambda,
           attn_sinks, g_attn, g_lru, w_out, ln1_g, ln1_b, w_group, b_group, w_router,
           b_router, w_gate, w_up, w_down, ln2_g, ln2_b):
    bsz, seq, d = x.shape
    nbx = seq // BLOCK
    n_tok = bsz * seq
    x2d = x.reshape(n_tok, d)
    row = lambda v: v.reshape(1, -1).astype(F32)

    q_scale = jnp.concatenate([jnp.full((ATTN_WIDTH,), LOG2E * HEAD_DIM ** -0.5, F32),
                               jnp.ones((IN_COLS - ATTN_WIDTH,), F32)])
    w_in_b = (w_in[0] * q_scale).astype(BF16)
    meta_blk = jnp.concatenate([jnp.zeros((PAD_FRONT, d), F32), meta_tokens.astype(F32)], axis=0)
    q, kv, xr, yr = _in_proj(x2d, w_in_b, PROJ_ROWS)
    qm, kvm, xrm, yrm = _in_proj(meta_blk, w_in_b, BLOCK)
    shp = lambda a: a.reshape(bsz, seq, a.shape[-1])

    attn_n = _attention(attn_sinks[0].astype(F32) * LOG2E, shp(q), shp(kv), kvm, _alibi_bias(),
                        row(g_attn[0]), bsz, nbx)
    lru_n = _rglru(shp(xr), shp(yr), xrm, yrm, conv_w[0].astype(F32), row(conv_b[0]),
                   _slab_gates(lru_wa[0]), _slab_gates(lru_wx[0]),
                   row(lru_ba[0]), row(lru_bx[0]), row(lru_lambda[0]), row(g_lru[0]), bsz, nbx)

    w_out_b = w_out[0].astype(BF16)
    gpad = SUBLANES - N_GROUPS
    w_rt = jnp.concatenate(
        [w_group[0].T, jnp.zeros((gpad, d), F32),
         jnp.transpose(w_router[0], (0, 2, 1)).reshape(N_EXPERTS, d)], axis=0).astype(F32)
    w_rt_hi = w_rt.astype(BF16)
    w_rt_lo = (w_rt - w_rt_hi.astype(F32)).astype(BF16)
    b_rt = jnp.concatenate([b_group[0], jnp.zeros((gpad,), F32),
                            b_router[0].reshape(-1)]).astype(F32).reshape(ROUTER_ROWS, 1)
    h1, hp, info, cnt = _out_proj(
        attn_n.reshape(n_tok, ATTN_WIDTH), lru_n.reshape(n_tok, LRU_WIDTH), x2d,
        w_out_b[:ATTN_WIDTH], w_out_b[ATTN_WIDTH:], row(ln1_g[0]), row(ln1_b[0]),
        w_rt_hi, w_rt_lo, b_rt, OUT_PROJ_ROWS)

    dest = _route(info, cnt, ROUTE_ROWS)
    n_slots = n_tok * TOP_K
    n_blocks = n_slots // MOE_BLOCK + N_EXPERTS
    cap = n_blocks * MOE_BLOCK
    nblk = (cnt[:, 0].astype(I32) + MOE_BLOCK - 1) // MOE_BLOCK
    bends = jnp.cumsum(nblk)
    bstart = (bends - nblk).astype(I32)
    nused = bends[-1:].astype(I32)
    windows = lambda v: v.reshape(SC_WORKERS, -1, SC_WINDOW)
    d0 = dest[0]
    d1 = dest[1]

    xs = _sc_scatter_rows(hp, windows(d0), windows(d1), cap)
    yb = _experts(bstart, nblk.astype(I32), nused, xs, w_gate[0], w_up[0], w_down[0])
    out = h1
    part_len = n_tok // COMBINE_PARTS
    for part in range(COMBINE_PARTS):
        tok = slice(part * part_len, (part + 1) * part_len)
        ys = _sc_gather_rows(yb, windows(jnp.concatenate([d0[tok], d1[tok]])))
        out = _combine(ys, out, info, row(ln2_g[0]), row(ln2_b[0]), COMBINE_ROWS,
                       part, COMBINE_PARTS)
    return out.reshape(bsz, seq, d)
```

```python
import jax
import jax.numpy as jnp
import numpy as np
from jax import lax
from jax.experimental import pallas as pl
from jax.experimental.pallas import tpu as pltpu
from jax.experimental.pallas import tpu_sc as plsc

F32 = jnp.float32
BF16 = jnp.bfloat16
U32 = jnp.uint32
I32 = jnp.int32

D_MODEL = 1024
N_META = 16
BLOCK = 128
PAD_FRONT = BLOCK - N_META
HEAD_DIM = 64
ATTN_WIDTH = 512
LRU_WIDTH = 512
N_Q_HEADS = 8
N_KV_HEADS = 2
KV_WIDTH = N_KV_HEADS * HEAD_DIM
LRU_BLOCKS = 8
CONV_W = 4
LRU_C = 8.0
IN_COLS = ATTN_WIDTH + 2 * KV_WIDTH + 2 * LRU_WIDTH
N_GROUPS = 4
EXPERTS_PER_GROUP = 8
N_EXPERTS = N_GROUPS * EXPERTS_PER_GROUP
TOP_K = 2
D_FF = 512
MOE_BLOCK = 256
ALPHA = 2.0 ** 0.25
EPS = 1e-5
NEG = -1e30
LOG2E = float(np.log2(np.e))
LANES = 128
SUBLANES = 8
PACKED = D_MODEL // 2

PROJ_ROWS = 512
OUT_PROJ_ROWS = 1024
ROUTE_ROWS = 512
COMBINE_ROWS = 1024
COMBINE_PARTS = 4
X_GROUP = 4
X_AHEAD = 4
X_RING = X_AHEAD + X_GROUP
Y_RING = 2 * X_GROUP
VMEM_LIMIT = 48 * 1024 * 1024

SC_CORES = 2
SC_SUBCORES = 16
SC_WORKERS = SC_CORES * SC_SUBCORES
SC_WINDOW = 64


def _cparams(n_axes):
    return pltpu.CompilerParams(
        dimension_semantics=("arbitrary",) * n_axes, vmem_limit_bytes=VMEM_LIMIT)


def _in_proj_kernel(x_ref, w_ref, q_ref, kv_ref, xr_ref, yr_ref):
    proj = jnp.dot(x_ref[...].astype(BF16), w_ref[...], preferred_element_type=F32)
    o = 0
    for ref, width in ((q_ref, ATTN_WIDTH), (kv_ref, 2 * KV_WIDTH),
                       (xr_ref, LRU_WIDTH), (yr_ref, LRU_WIDTH)):
        ref[...] = proj[:, o:o + width].astype(ref.dtype)
        o += width


def _in_proj(x2d, w_bf16, rows):
    n = x2d.shape[0]
    widths = (ATTN_WIDTH, 2 * KV_WIDTH, LRU_WIDTH, LRU_WIDTH)
    return pl.pallas_call(
        _in_proj_kernel,
        grid=(n // rows,),
        in_specs=[pl.BlockSpec((rows, D_MODEL), lambda i: (i, 0)),
                  pl.BlockSpec((D_MODEL, IN_COLS), lambda i: (0, 0))],
        out_specs=[pl.BlockSpec((rows, w), lambda i: (i, 0)) for w in widths],
        out_shape=[jax.ShapeDtypeStruct((n, w), BF16) for w in widths],
        compiler_params=_cparams(1),
        name="in_proj",
    )(x2d, w_bf16)


def _attn_kernel(sinks_ref, q_ref, kv_ref, kvm_ref, bias_ref, g_ref, o_ref,
                 klo, khi, vlo, vhi):
    nbx = q_ref.shape[0] // BLOCK
    lo_lanes = lax.broadcasted_iota(I32, (BLOCK // 2, LANES), 1) < HEAD_DIM

    def layout_block(n, blk):
        rows = pl.ds(pl.multiple_of(n * BLOCK, BLOCK), BLOCK)
        as_bf16 = lambda words: pltpu.bitcast(words, BF16)
        for src, dst_lo, dst_hi in ((blk[:, :KV_WIDTH], klo, khi), (blk[:, KV_WIDTH:], vlo, vhi)):
            w = pltpu.bitcast(src, U32)
            r = pltpu.roll(w, HEAD_DIM, axis=1)
            zero = jnp.zeros_like(w)
            dst_lo[0, rows, :] = as_bf16(jnp.where(lo_lanes, w, zero))
            dst_hi[0, rows, :] = as_bf16(jnp.where(lo_lanes, zero, r))
            dst_lo[1, rows, :] = as_bf16(jnp.where(lo_lanes, r, zero))
            dst_hi[1, rows, :] = as_bf16(jnp.where(lo_lanes, zero, w))

    layout_block(0, kvm_ref[...])

    def layout_body(n, carry):
        layout_block(n + 1, kv_ref[pl.ds(pl.multiple_of(n * BLOCK, BLOCK), BLOCK), :])
        return carry

    lax.fori_loop(0, nbx, layout_body, 0, unroll=4)

    ones_lo = jnp.where(lax.broadcasted_iota(I32, (2 * BLOCK, LANES), 1) < HEAD_DIM,
                        1.0, 0.0).astype(BF16)
    ones_hi = (1.0 - ones_lo.astype(F32)).astype(BF16)
    top_rows = lax.broadcasted_iota(I32, (2 * BLOCK, 1), 0) < BLOCK
    lo_half = lax.broadcasted_iota(I32, (2 * BLOCK, LANES), 1) < HEAD_DIM

    def block(i, carry):
        q_rows = pl.ds(pl.multiple_of(i * BLOCK, BLOCK), BLOCK)
        win = pl.ds(pl.multiple_of(i * BLOCK, BLOCK), 2 * BLOCK)
        q = q_ref[q_rows, :]
        first = jnp.minimum(i, 1)
        outs = []
        for j in range(N_KV_HEADS):
            q2 = jnp.concatenate([q[:, (2 * j) * LANES:(2 * j + 1) * LANES],
                                  q[:, (2 * j + 1) * LANES:(2 * j + 2) * LANES]], axis=0)
            kc = jnp.concatenate([klo[j, win, :], khi[j, win, :]], axis=0)
            s = lax.dot_general(q2, kc, (((1,), (1,)), ((), ())), preferred_element_type=F32)
            s = s + bias_ref[first, j]
            ps, es = [], []
            for c in range(2):
                sink = jnp.where(top_rows, sinks_ref[4 * j + c], sinks_ref[4 * j + 2 + c])
                sc = s[:, c * 2 * BLOCK:(c + 1) * 2 * BLOCK]
                m = jnp.maximum(jnp.max(sc, axis=1, keepdims=True), sink)
                ps.append(jnp.exp2(sc - m).astype(BF16))
                es.append(jnp.exp2(sink - m))
            v_lo = jnp.concatenate([vlo[j, win, :], ones_lo], axis=1)
            v_hi = jnp.concatenate([vhi[j, win, :], ones_hi], axis=1)
            r = (jnp.dot(ps[0], v_lo, preferred_element_type=F32)
                 + jnp.dot(ps[1], v_hi, preferred_element_type=F32))
            den = r[:, LANES:] + jnp.where(lo_half, es[0], es[1])
            o2 = r[:, :LANES] * (1.0 / den)
            outs += [o2[:BLOCK], o2[BLOCK:]]
        out = jnp.concatenate(outs, axis=1)
        ms = jnp.mean(out * out, axis=1, keepdims=True)
        o_ref[q_rows, :] = (out * lax.rsqrt(ms + EPS) * g_ref[...]).astype(o_ref.dtype)
        return carry

    lax.fori_loop(0, nbx, block, 0, unroll=4)


def _attention(sinks, q, kv, kvm, bias, g_attn, bsz, nbx):
    seq = nbx * BLOCK
    const2 = lambda b: (0, 0)
    kv_scratch = pltpu.VMEM((N_KV_HEADS, seq + BLOCK, LANES), BF16)
    return pl.pallas_call(
        _attn_kernel,
        grid=(bsz,),
        in_specs=[pl.BlockSpec(memory_space=pltpu.SMEM),
                  pl.BlockSpec((None, seq, ATTN_WIDTH), lambda b: (b, 0, 0)),
                  pl.BlockSpec((None, seq, 2 * KV_WIDTH), lambda b: (b, 0, 0)),
                  pl.BlockSpec((BLOCK, 2 * KV_WIDTH), const2),
                  pl.BlockSpec((2, N_KV_HEADS, 2 * BLOCK, 4 * BLOCK), lambda b: (0, 0, 0, 0)),
                  pl.BlockSpec((1, ATTN_WIDTH), const2)],
        out_specs=pl.BlockSpec((None, seq, ATTN_WIDTH), lambda b: (b, 0, 0)),
        out_shape=jax.ShapeDtypeStruct((bsz, seq, ATTN_WIDTH), BF16),
        scratch_shapes=[kv_scratch, kv_scratch, kv_scratch, kv_scratch],
        compiler_params=_cparams(1),
        name="attention",
    )(sinks, q, kv, kvm, bias, g_attn)


def _sigmoid(v):
    return 0.5 * jnp.tanh(0.5 * v) + 0.5


def _gelu_tanh(y):
    c = float(np.sqrt(2.0 / np.pi))
    half = 0.5 * y
    return half + half * jnp.tanh(y * (c + (c * 0.044715) * (y * y)))


LRU_CHUNK = 44
LRU_SEG = SUBLANES * LRU_CHUNK
LRU_SLABS = LRU_WIDTH // LANES


def _lru_kernel(xr_ref, yr_ref, xrm_ref, yrm_ref, cw_ref, cb_ref, wa_ref, wx_ref, ba_ref,
                bx_ref, lam_ref, g_ref, o_ref, x_st, y_st, o_st, s_st, xtail, hcar):
    seq = xr_ref.shape[0]
    n_seg = (seq + BLOCK) // LRU_SEG
    xtail[...] = jnp.zeros_like(xtail)
    hcar[...] = jnp.zeros_like(hcar)
    lam = lam_ref[...]
    softplus_neg = jnp.maximum(-lam, 0.0) + jnp.log(1.0 + jnp.exp(-jnp.abs(lam)))
    sub = lax.broadcasted_iota(jnp.int32, (SUBLANES, LANES), 0)

    def strided(j):
        return pl.ds(j, SUBLANES, stride=LRU_CHUNK)

    def piece(v, j):
        return v[j * SUBLANES:(j + 1) * SUBLANES, :]

    def segment(k, first):
        if first:
            head = LRU_SEG - BLOCK
            x_nat = jnp.concatenate([xrm_ref[...], xr_ref[0:head, :]], axis=0).astype(F32)
            y_nat = jnp.concatenate([yrm_ref[...], yr_ref[0:head, :]], axis=0).astype(F32)
        else:
            rows = pl.ds(pl.multiple_of(k * LRU_SEG - BLOCK, 2 * SUBLANES), LRU_SEG)
            x_nat = xr_ref[rows, :].astype(F32)
            y_nat = yr_ref[rows, :].astype(F32)
        for c in range(LRU_SLABS):
            x_st[c] = x_nat[:, c * LANES:(c + 1) * LANES]
            y_st[c] = y_nat[:, c * LANES:(c + 1) * LANES]
        first_row = k * LRU_SEG + LRU_CHUNK * sub
        sumsq = [jnp.zeros((SUBLANES, LANES), F32) for _ in range(LRU_CHUNK)]

        for c in range(LRU_SLABS):
            lanes = slice(c * LANES, (c + 1) * LANES)
            x = [x_st[c, strided(j), :] for j in range(LRU_CHUNK)]
            before = []
            for d in range(1, CONV_W):
                from_prev_chunk = pltpu.roll(x[LRU_CHUNK - d], 1, axis=0)
                before.append(jnp.where(sub == 0, xtail[d - 1:d, lanes], from_prev_chunk))
            for d in range(1, CONV_W):
                xtail[d - 1:d, lanes] = x[LRU_CHUNK - d][SUBLANES - 1:SUBLANES, :]

            def x_at(j):
                return x[j] if j >= 0 else before[-j - 1]

            taps = [cw_ref[t:t + 1, lanes] for t in range(CONV_W)]
            bias = cb_ref[:, lanes]
            xc = jnp.concatenate(
                [bias + sum(taps[t] * x_at(j - (CONV_W - 1) + t) for t in range(CONV_W))
                 for j in range(LRU_CHUNK)], axis=0)
            xcb = xc.astype(BF16)
            r = _sigmoid(jnp.dot(xcb, wa_ref[c], preferred_element_type=F32) + ba_ref[:, lanes])
            i = _sigmoid(jnp.dot(xcb, wx_ref[c], preferred_element_type=F32) + bx_ref[:, lanes])
            a = jnp.exp(-LRU_C * r * softplus_neg[:, lanes])
            z = 1.0 - a * a
            u = jnp.where(z > 0.0, z * lax.rsqrt(z), 0.0) * (i * xc)

            h = jnp.zeros((SUBLANES, LANES), F32)
            p = jnp.ones((SUBLANES, LANES), F32)
            hs, ps = [], []
            for j in range(LRU_CHUNK):
                aj = piece(a, j)
                uj = piece(u, j)
                if first:
                    uj = jnp.where(first_row + j >= PAD_FRONT, uj, 0.0)
                h = aj * h + uj
                p = aj * p
                hs.append(h)
                ps.append(p)
            entry = [hcar[:, lanes]]
            for s in range(SUBLANES):
                entry.append(h[s:s + 1, :] + p[s:s + 1, :] * entry[s])
            hcar[:, lanes] = entry[SUBLANES]
            entry_rows = jnp.concatenate(entry[:SUBLANES], axis=0)

            for j in range(LRU_CHUNK):
                state = hs[j] + ps[j] * entry_rows
                out = state * _gelu_tanh(y_st[c, strided(j), :])
                sumsq[j] = sumsq[j] + out * out
                o_st[c, strided(j), :] = out

        for j in range(LRU_CHUNK):
            ms = jnp.sum(sumsq[j], axis=1, keepdims=True) * (1.0 / LRU_WIDTH)
            s_st[strided(j), :] = jnp.broadcast_to(lax.rsqrt(ms + EPS), (SUBLANES, LANES))
        scale = s_st[...]
        for c in range(LRU_SLABS):
            lanes = slice(c * LANES, (c + 1) * LANES)
            normed = (o_st[c] * scale * g_ref[:, lanes]).astype(o_ref.dtype)
            if first:
                o_ref[0:LRU_SEG - BLOCK, lanes] = normed[BLOCK:, :]
            else:
                o_ref[rows, lanes] = normed

    assert BLOCK <= LRU_SEG
    segment(0, True)

    def later_segment(k, carry):
        segment(k, False)
        return carry

    lax.fori_loop(1, n_seg, later_segment, 0)


def _rglru(xr, yr, xrm, yrm, cw, cb, wa, wx, ba, bx, lam, g_lru, bsz, nbx):
    seq = nbx * BLOCK
    assert (seq + BLOCK) % LRU_SEG == 0
    main = pl.BlockSpec((None, seq, LRU_WIDTH), lambda b: (b, 0, 0))
    const2 = lambda b: (0, 0)
    row_spec = pl.BlockSpec((1, LRU_WIDTH), const2)
    gate_spec = pl.BlockSpec((LRU_SLABS, LANES, LANES), lambda b: (0, 0, 0))
    slabs = pltpu.VMEM((LRU_SLABS, LRU_SEG, LANES), F32)
    return pl.pallas_call(
        _lru_kernel,
        grid=(bsz,),
        in_specs=[main, main,
                  pl.BlockSpec((BLOCK, LRU_WIDTH), const2),
                  pl.BlockSpec((BLOCK, LRU_WIDTH), const2),
                  pl.BlockSpec((CONV_W, LRU_WIDTH), const2),
                  row_spec, gate_spec, gate_spec,
                  row_spec, row_spec, row_spec, row_spec],
        out_specs=main,
        out_shape=jax.ShapeDtypeStruct((bsz, seq, LRU_WIDTH), BF16),
        scratch_shapes=[slabs, slabs, slabs,
                        pltpu.VMEM((LRU_SEG, LANES), F32),
                        pltpu.VMEM((SUBLANES, LRU_WIDTH), F32),
                        pltpu.VMEM((1, LRU_WIDTH), F32)],
        compiler_params=_cparams(1),
        name="rglru",
    )(xr, yr, xrm, yrm, cw, cb, wa, wx, ba, bx, lam, g_lru)


def _pack_rows(v):
    bits = lax.bitcast_convert_type(v.astype(BF16).astype(F32), U32)
    return (bits[:, :PACKED] >> 16) | (bits[:, PACKED:] & jnp.uint32(0xFFFF0000))


def _unpack_rows(w):
    lo = lax.bitcast_convert_type(w << 16, F32)
    hi = lax.bitcast_convert_type(w & jnp.uint32(0xFFFF0000), F32)
    return lo, hi


def _layer_norm(z, g, b):
    mu = jnp.mean(z, axis=1, keepdims=True)
    zc = z - mu
    var = jnp.mean(zc * zc, axis=1, keepdims=True)
    return zc * lax.rsqrt(var + EPS) * g + b


def _out_proj_kernel(a_ref, l_ref, x_ref, wa_ref, wl_ref, g_ref, b_ref, wrt_hi_ref, wrt_lo_ref,
                     brt_ref, h_ref, hp_ref, info_ref, cnt_ref):
    @pl.when(pl.program_id(0) == 0)
    def _():
        cnt_ref[...] = jnp.zeros_like(cnt_ref)

    for c in range(h_ref.shape[0] // PROJ_ROWS):
        rows = slice(c * PROJ_ROWS, (c + 1) * PROJ_ROWS)
        _out_proj_rows(a_ref[rows, :], l_ref[rows, :], x_ref[rows, :], wa_ref, wl_ref, g_ref,
                       b_ref, wrt_hi_ref, wrt_lo_ref, brt_ref,
                       h_ref.at[rows, :], hp_ref.at[rows, :], info_ref.at[:, rows], cnt_ref)


def _out_proj_rows(a, l, x, wa_ref, wl_ref, g_ref, b_ref, wrt_hi_ref, wrt_lo_ref, brt_ref,
                   h_ref, hp_ref, info_ref, cnt_ref):
    mix = jnp.dot(a, wa_ref[...], preferred_element_type=F32)
    mix = mix + jnp.dot(l, wl_ref[...], preferred_element_type=F32)
    h = _layer_norm(ALPHA * x + mix, g_ref[...], b_ref[...])
    h_ref[...] = h
    hp_ref[...] = _pack_rows(h)

    h_hi = h.astype(BF16)
    h_lo = (h - h_hi.astype(F32)).astype(BF16)
    nt = (((1,), (1,)), ((), ()))
    lg = (lax.dot_general(wrt_hi_ref[...], h_hi, nt, preferred_element_type=F32)
          + lax.dot_general(wrt_lo_ref[...], h_hi, nt, preferred_element_type=F32)
          + lax.dot_general(wrt_hi_ref[...], h_lo, nt, preferred_element_type=F32)) + brt_ref[...]
    tile_shape = (SUBLANES, h.shape[0])
    sub = lax.broadcasted_iota(I32, tile_shape, 0)
    ninf = -jnp.inf
    t0 = lg[0:SUBLANES]
    gl = jnp.where(sub < N_GROUPS, t0, ninf)
    gmax = jnp.max(gl, axis=0, keepdims=True)
    g_idx = jnp.min(jnp.where(gl == gmax, sub, SUBLANES), axis=0, keepdims=True)
    g_w = 1.0 / jnp.sum(jnp.where(sub < N_GROUPS, jnp.exp(t0 - gmax), 0.0),
                        axis=0, keepdims=True)
    el = lg[SUBLANES:2 * SUBLANES]
    for g in range(1, N_GROUPS):
        el = jnp.where(g_idx == g, lg[(g + 1) * SUBLANES:(g + 2) * SUBLANES], el)
    v1 = jnp.max(el, axis=0, keepdims=True)
    i1 = jnp.min(jnp.where(el == v1, sub, SUBLANES), axis=0, keepdims=True)
    el2 = jnp.where(sub == i1, ninf, el)
    v2 = jnp.max(el2, axis=0, keepdims=True)
    i2 = jnp.min(jnp.where(el2 == v2, sub, SUBLANES), axis=0, keepdims=True)
    t = jnp.exp(v2 - v1)
    w1 = 1.0 / (1.0 + t)
    w2 = t * w1
    e_base = g_idx * EXPERTS_PER_GROUP
    e1 = e_base + i1
    e2 = e_base + i2
    info_ref[...] = jnp.where(sub == 0, e1.astype(F32),
                              jnp.where(sub == 1, e2.astype(F32),
                                        jnp.where(sub == 2, g_w * w1,
                                                  jnp.where(sub == 3, g_w * w2, 0.0))))
    expert = lax.broadcasted_iota(I32, (N_EXPERTS, h.shape[0]), 0)
    chosen = (expert == e1).astype(F32) + (expert == e2).astype(F32)
    cnt_ref[...] += jnp.sum(chosen, axis=1, keepdims=True)


ROUTER_ROWS = (N_GROUPS + 1) * SUBLANES


def _out_proj(attn_n, lru_n, x2d, wo_a, wo_l, ln_g, ln_b, w_rt_hi, w_rt_lo, b_rt, rows):
    n = x2d.shape[0]
    const = lambda i: (0, 0)
    tile = lambda w: pl.BlockSpec((rows, w), lambda i: (i, 0))
    return pl.pallas_call(
        _out_proj_kernel,
        grid=(n // rows,),
        in_specs=[tile(ATTN_WIDTH), tile(LRU_WIDTH), tile(D_MODEL),
                  pl.BlockSpec((ATTN_WIDTH, D_MODEL), const),
                  pl.BlockSpec((LRU_WIDTH, D_MODEL), const),
                  pl.BlockSpec((1, D_MODEL), const),
                  pl.BlockSpec((1, D_MODEL), const),
                  pl.BlockSpec((ROUTER_ROWS, D_MODEL), const),
                  pl.BlockSpec((ROUTER_ROWS, D_MODEL), const),
                  pl.BlockSpec((ROUTER_ROWS, 1), const)],
        out_specs=[tile(D_MODEL), tile(PACKED),
                   pl.BlockSpec((SUBLANES, rows), lambda i: (0, i)),
                   pl.BlockSpec((N_EXPERTS, LANES), const)],
        out_shape=[jax.ShapeDtypeStruct((n, D_MODEL), F32),
                   jax.ShapeDtypeStruct((n, PACKED), U32),
                   jax.ShapeDtypeStruct((SUBLANES, n), F32),
                   jax.ShapeDtypeStruct((N_EXPERTS, LANES), F32)],
        compiler_params=_cparams(1),
        name="out_proj",
    )(attn_n, lru_n, x2d, wo_a, wo_l, ln_g, ln_b, w_rt_hi, w_rt_lo, b_rt)


def _route_kernel(info_ref, cnt_ref, tri_ref, dest_ref, carry, pstart):
    t = pl.program_id(0)
    info = info_ref[...]
    shape = (N_EXPERTS, info.shape[1])
    expert = lax.broadcasted_iota(I32, shape, 0)
    oh1 = (expert == info[0:1, :].astype(I32)).astype(F32)
    oh2 = (expert == info[1:2, :].astype(I32)).astype(F32)
    both = oh1 + oh2

    @pl.when(t == 0)
    def _():
        c = cnt_ref[...].astype(I32)
        padded = ((c + (MOE_BLOCK - 1)) // MOE_BLOCK) * MOE_BLOCK
        e = lax.broadcasted_iota(I32, (N_EXPERTS, LANES), 0)
        scan = padded
        for d in (1, 2, 4, 8, 16):
            scan = scan + jnp.where(e >= d, pltpu.roll(scan, d, axis=0), 0)
        pstart[...] = (scan - padded)[:, 0:1].astype(F32)
        carry[...] = jnp.zeros_like(carry)

    before = jnp.dot(both.astype(BF16), tri_ref[...], preferred_element_type=F32)
    row_of = before + (carry[...] + pstart[...])
    r1 = jnp.sum(oh1 * row_of, axis=0, keepdims=True)
    r2 = jnp.sum(oh2 * row_of, axis=0, keepdims=True)
    sub = lax.broadcasted_iota(I32, dest_ref.shape, 0)
    dest_ref[...] = jnp.where(sub == 0, r1, jnp.where(sub == 1, r2, 0.0)).astype(I32)
    carry[...] += jnp.sum(both, axis=1, keepdims=True)


def _route(info_t, cnt, cols):
    n = info_t.shape[1]
    tri = jnp.asarray(np.triu(np.ones((cols, cols), np.float32), 1), BF16)
    return pl.pallas_call(
        _route_kernel,
        grid=(n // cols,),
        in_specs=[pl.BlockSpec((SUBLANES, cols), lambda t: (0, t)),
                  pl.BlockSpec((N_EXPERTS, LANES), lambda t: (0, 0)),
                  pl.BlockSpec((cols, cols), lambda t: (0, 0))],
        out_specs=pl.BlockSpec((SUBLANES, cols), lambda t: (0, t)),
        out_shape=jax.ShapeDtypeStruct((SUBLANES, n), I32),
        scratch_shapes=[pltpu.VMEM((N_EXPERTS, 1), F32), pltpu.VMEM((N_EXPERTS, 1), F32)],
        compiler_params=_cparams(1),
        name="route",
    )(info_t, cnt, tri)


def _sc_mesh():
    return plsc.VectorSubcoreMesh(core_axis_name="core", subcore_axis_name="subcore")


def _sc_worker_id():
    return lax.axis_index("subcore") * SC_CORES + lax.axis_index("core")


def _sc_scatter_rows(rows, d0, d1, cap):
    n, width = rows.shape
    per_worker = n // SC_WORKERS
    n_win = per_worker // SC_WINDOW

    def body(x_hbm, d0_hbm, d1_hbm, o_hbm, i0_v, i1_v, rows_v, rsem, sem0, sem1):
        wid = _sc_worker_id()
        pltpu.sync_copy(d0_hbm.at[wid], i0_v)
        pltpu.sync_copy(d1_hbm.at[wid], i1_v)

        def read(j):
            src = x_hbm.at[pl.ds(wid * per_worker + j * SC_WINDOW, SC_WINDOW)]
            return pltpu.make_async_copy(src, rows_v.at[j % 2], rsem.at[j % 2])

        def scatters(j):
            return (pltpu.make_async_copy(rows_v.at[j % 2], o_hbm.at[i0_v.at[j]], sem0.at[j % 2]),
                    pltpu.make_async_copy(rows_v.at[j % 2], o_hbm.at[i1_v.at[j]], sem1.at[j % 2]))

        read(0).start()
        for j in range(n_win):
            if j + 1 < n_win:
                if j >= 1:
                    for cp in scatters(j - 1):
                        cp.wait()
                read(j + 1).start()
            read(j).wait()
            for cp in scatters(j):
                cp.start()
        for j in range(max(n_win - 2, 0), n_win):
            for cp in scatters(j):
                cp.wait()

    return pl.kernel(
        body,
        out_type=jax.ShapeDtypeStruct((cap, width), rows.dtype),
        mesh=_sc_mesh(),
        scratch_types=[pltpu.VMEM((n_win, SC_WINDOW), I32), pltpu.VMEM((n_win, SC_WINDOW), I32),
                       pltpu.VMEM((2, SC_WINDOW, width), rows.dtype),
                       pltpu.SemaphoreType.DMA((2,)), pltpu.SemaphoreType.DMA((2,)),
                       pltpu.SemaphoreType.DMA((2,))],
        name="dispatch",
    )(rows, d0, d1)


def _sc_gather_rows(table, idx):
    width = table.shape[1]
    n_win = idx.shape[1]
    per_worker = n_win * SC_WINDOW

    def body(y_hbm, i_hbm, o_hbm, i_v, rows_v, gsem, wsem):
        wid = _sc_worker_id()
        pltpu.sync_copy(i_hbm.at[wid], i_v)

        def gather(j):
            return pltpu.make_async_copy(y_hbm.at[i_v.at[j]], rows_v.at[j % 2], gsem.at[j % 2])

        def write(j):
            dst = o_hbm.at[pl.ds(wid * per_worker + j * SC_WINDOW, SC_WINDOW)]
            return pltpu.make_async_copy(rows_v.at[j % 2], dst, wsem.at[j % 2])

        gather(0).start()
        for j in range(n_win):
            if j + 1 < n_win:
                if j >= 1:
                    write(j - 1).wait()
                gather(j + 1).start()
            gather(j).wait()
            write(j).start()
        for j in range(max(n_win - 2, 0), n_win):
            write(j).wait()

    return pl.kernel(
        body,
        out_type=jax.ShapeDtypeStruct((SC_WORKERS * per_worker, width), table.dtype),
        mesh=_sc_mesh(),
        scratch_types=[pltpu.VMEM((n_win, SC_WINDOW), I32),
                       pltpu.VMEM((2, SC_WINDOW, width), table.dtype),
                       pltpu.SemaphoreType.DMA((2,)), pltpu.SemaphoreType.DMA((2,))],
        name="collect",
    )(table, idx)


def _expert_kernel(bstart_ref, nblk_ref, nused_ref, xs_hbm, wg_ref, wu_ref, wd_ref, yb_hbm,
                   xbuf, ybuf, zbuf, xsem, ysem, zsem, wg_b, wu_b, wd_b):
    e = pl.program_id(0)
    nused = nused_ref[0]
    n_blocks = yb_hbm.shape[0] // MOE_BLOCK

    def rows(b):
        return pl.ds(pl.multiple_of(b * MOE_BLOCK, MOE_BLOCK), MOE_BLOCK)

    def x_copy(b):
        slot = b % X_RING
        return pltpu.make_async_copy(xs_hbm.at[rows(b)], xbuf.at[slot], xsem.at[slot])

    def y_copy(b):
        slot = b % Y_RING
        return pltpu.make_async_copy(ybuf.at[slot], yb_hbm.at[rows(b)], ysem.at[slot])

    @pl.when(e == 0)
    def _():
        for b in range(X_AHEAD):
            @pl.when(b < nused)
            def _():
                x_copy(b).start()

    wg_b[...] = wg_ref[...].astype(BF16)
    wu_b[...] = wu_ref[...].astype(BF16)
    wd_b[...] = wd_ref[...].astype(BF16)

    def run_blocks(b, count):
        for k in range(count):
            x_copy(b + k).wait()
        for k in range(count):
            nxt = b + X_AHEAD + k

            @pl.when(nxt < nused)
            def _():
                x_copy(nxt).start()

            @pl.when(b + k >= Y_RING)
            def _():
                y_copy(b + k - Y_RING).wait()

        words = jnp.concatenate([xbuf[(b + k) % X_RING] for k in range(count)], axis=0)
        lo, hi = _unpack_rows(words)
        lo = lo.astype(BF16)
        hi = hi.astype(BF16)
        g = (jnp.dot(lo, wg_b[0:PACKED, :], preferred_element_type=F32)
             + jnp.dot(hi, wg_b[PACKED:, :], preferred_element_type=F32))
        u = (jnp.dot(lo, wu_b[0:PACKED, :], preferred_element_type=F32)
             + jnp.dot(hi, wu_b[PACKED:, :], preferred_element_type=F32))
        mid = (g * _sigmoid(g) * u).astype(BF16)
        y = _pack_rows(jnp.dot(mid, wd_b[...], preferred_element_type=F32))
        for k in range(count):
            ybuf[(b + k) % Y_RING] = y[k * MOE_BLOCK:(k + 1) * MOE_BLOCK]
            y_copy(b + k).start()

    b0 = bstart_ref[e]
    nb = nblk_ref[e]

    def group(i, carry):
        run_blocks(b0 + X_GROUP * i, X_GROUP)
        return carry

    lax.fori_loop(0, nb // X_GROUP, group, 0)
    done = nb - nb % X_GROUP
    size = X_GROUP // 2
    while size >= 1:
        @pl.when((nb // size) % 2 == 1)
        def _(size=size, done=done):
            run_blocks(b0 + done, size)

        done = done + (nb // size) % 2 * size
        size //= 2

    @pl.when(e == pl.num_programs(0) - 1)
    def _():
        for back in range(Y_RING, 0, -1):
            @pl.when(nused >= back)
            def _():
                y_copy(nused - back).wait()

        zbuf[...] = jnp.zeros_like(zbuf)

        def z_copy(b):
            return pltpu.make_async_copy(zbuf, yb_hbm.at[rows(b)], zsem.at[0])

        def z_start(b, carry):
            z_copy(b).start()
            return carry

        def z_wait(b, carry):
            z_copy(b).wait()
            return carry

        lax.fori_loop(nused, n_blocks, z_start, 0)
        lax.fori_loop(nused, n_blocks, z_wait, 0)


def _experts(bstart, nblk, nused, xs, w_gate, w_up, w_down):
    cap = xs.shape[0]
    w_idx = lambda e, bs, nb, nu: (e, 0, 0)
    grid_spec = pltpu.PrefetchScalarGridSpec(
        num_scalar_prefetch=3,
        grid=(N_EXPERTS,),
        in_specs=[pl.BlockSpec(memory_space=pl.ANY),
                  pl.BlockSpec((None, D_MODEL, D_FF), w_idx),
                  pl.BlockSpec((None, D_MODEL, D_FF), w_idx),
                  pl.BlockSpec((None, D_FF, D_MODEL), w_idx)],
        out_specs=pl.BlockSpec(memory_space=pl.ANY),
        scratch_shapes=[pltpu.VMEM((X_RING, MOE_BLOCK, PACKED), U32),
                        pltpu.VMEM((Y_RING, MOE_BLOCK, PACKED), U32),
                        pltpu.VMEM((MOE_BLOCK, PACKED), U32),
                        pltpu.SemaphoreType.DMA((X_RING,)),
                        pltpu.SemaphoreType.DMA((Y_RING,)),
                        pltpu.SemaphoreType.DMA((1,)),
                        pltpu.VMEM((D_MODEL, D_FF), BF16),
                        pltpu.VMEM((D_MODEL, D_FF), BF16),
                        pltpu.VMEM((D_FF, D_MODEL), BF16)])
    return pl.pallas_call(
        _expert_kernel,
        grid_spec=grid_spec,
        out_shape=jax.ShapeDtypeStruct((cap, PACKED), U32),
        compiler_params=_cparams(1),
        name="experts",
    )(bstart, nblk, nused, xs, w_gate, w_up, w_down)


def _combine_kernel(y0_ref, y1_ref, h_ref, info_ref, g_ref, b_ref, o_ref):
    info = info_ref[...].T
    g0 = info[:, 2:3]
    g1 = info[:, 3:4]
    lo0, hi0 = _unpack_rows(y0_ref[...])
    lo1, hi1 = _unpack_rows(y1_ref[...])
    y = jnp.concatenate([g0 * lo0 + g1 * lo1, g0 * hi0 + g1 * hi1], axis=1)
    o_ref[...] = _layer_norm(ALPHA * h_ref[...] + y, g_ref[...], b_ref[...])


def _combine(ys, h, info, ln_g, ln_b, rows, part, n_parts):
    n = h.shape[0]
    steps = n // n_parts // rows
    off = part * steps
    const = lambda i: (0, 0)
    return pl.pallas_call(
        _combine_kernel,
        grid=(steps,),
        in_specs=[pl.BlockSpec((rows, PACKED), lambda i: (i, 0)),
                  pl.BlockSpec((rows, PACKED), lambda i: (i + steps, 0)),
                  pl.BlockSpec((rows, D_MODEL), lambda i: (i + off, 0)),
                  pl.BlockSpec((SUBLANES, rows), lambda i: (0, i + off)),
                  pl.BlockSpec((1, D_MODEL), const),
                  pl.BlockSpec((1, D_MODEL), const)],
        out_specs=pl.BlockSpec((rows, D_MODEL), lambda i: (i + off, 0)),
        out_shape=jax.ShapeDtypeStruct((n, D_MODEL), F32),
        input_output_aliases={2: 0},
        compiler_params=_cparams(1),
        name="combine",
    )(ys, ys, h, info, ln_g, ln_b)


def _alibi_bias():
    qi = np.arange(BLOCK)[:, None]
    kj = np.arange(2 * BLOCK)[None, :]
    dist = qi - kj + BLOCK
    band = (dist >= 0) & (dist < BLOCK)
    slopes = np.exp2(-8.0 * np.arange(1, N_Q_HEADS + 1, dtype=np.float32) / N_Q_HEADS)
    bias = np.where(band[None], -slopes[:, None, None] * dist[None].astype(np.float32), NEG)
    bias = bias * LOG2E
    first = np.where((kj >= PAD_FRONT)[None], bias, NEG)
    out = np.empty((2, N_KV_HEADS, 2 * BLOCK, 4 * BLOCK), np.float32)
    for v, per_head in enumerate((first, bias)):
        for j in range(N_KV_HEADS):
            out[v, j] = np.block([[per_head[4 * j], per_head[4 * j + 1]],
                                  [per_head[4 * j + 2], per_head[4 * j + 3]]])
    return jnp.asarray(out, F32)


def _slab_gates(w):
    nb, c, _ = w.shape
    per = LANES // c
    w = w.reshape(nb // per, per, c, c)
    eye = jnp.eye(per, dtype=w.dtype)
    return jnp.einsum('spcd,pq->spcqd', w, eye).reshape(nb // per, LANES, LANES).astype(BF16)


def kernel(x, meta_tokens, w_in, conv_w, conv_b, lru_wa, lru_ba, lru_wx, lru_bx, lru_lambda,
           attn_sinks, g_attn, g_lru, w_out, ln1_g, ln1_b, w_group, b_group, w_router,
           b_router, w_gate, w_up, w_down, ln2_g, ln2_b):
    bsz, seq, d = x.shape
    nbx = seq // BLOCK
    n_tok = bsz * seq
    x2d = x.reshape(n_tok, d)
    row = lambda v: v.reshape(1, -1).astype(F32)

    q_scale = jnp.concatenate([jnp.full((ATTN_WIDTH,), LOG2E * HEAD_DIM ** -0.5, F32),
                               jnp.ones((IN_COLS - ATTN_WIDTH,), F32)])
    w_in_b = (w_in[0] * q_scale).astype(BF16)
    meta_blk = jnp.concatenate([jnp.zeros((PAD_FRONT, d), F32), meta_tokens.astype(F32)], axis=0)
    q, kv, xr, yr = _in_proj(x2d, w_in_b, PROJ_ROWS)
    qm, kvm, xrm, yrm = _in_proj(meta_blk, w_in_b, BLOCK)
    shp = lambda a: a.reshape(bsz, seq, a.shape[-1])

    attn_n = _attention(attn_sinks[0].astype(F32) * LOG2E, shp(q), shp(kv), kvm, _alibi_bias(),
                        row(g_attn[0]), bsz, nbx)
    lru_n = _rglru(shp(xr), shp(yr), xrm, yrm, conv_w[0].astype(F32), row(conv_b[0]),
                   _slab_gates(lru_wa[0]), _slab_gates(lru_wx[0]),
                   row(lru_ba[0]), row(lru_bx[0]), row(lru_lambda[0]), row(g_lru[0]), bsz, nbx)

    w_out_b = w_out[0].astype(BF16)
    gpad = SUBLANES - N_GROUPS
    w_rt = jnp.concatenate(
        [w_group[0].T, jnp.zeros((gpad, d), F32),
         jnp.transpose(w_router[0], (0, 2, 1)).reshape(N_EXPERTS, d)], axis=0).astype(F32)
    w_rt_hi = w_rt.astype(BF16)
    w_rt_lo = (w_rt - w_rt_hi.astype(F32)).astype(BF16)
    b_rt = jnp.concatenate([b_group[0], jnp.zeros((gpad,), F32),
                            b_router[0].reshape(-1)]).astype(F32).reshape(ROUTER_ROWS, 1)
    h1, hp, info, cnt = _out_proj(
        attn_n.reshape(n_tok, ATTN_WIDTH), lru_n.reshape(n_tok, LRU_WIDTH), x2d,
        w_out_b[:ATTN_WIDTH], w_out_b[ATTN_WIDTH:], row(ln1_g[0]), row(ln1_b[0]),
        w_rt_hi, w_rt_lo, b_rt, OUT_PROJ_ROWS)

    dest = _route(info, cnt, ROUTE_ROWS)
    n_slots = n_tok * TOP_K
    n_blocks = n_slots // MOE_BLOCK + N_EXPERTS
    cap = n_blocks * MOE_BLOCK
    nblk = (cnt[:, 0].astype(I32) + MOE_BLOCK - 1) // MOE_BLOCK
    bends = jnp.cumsum(nblk)
    bstart = (bends - nblk).astype(I32)
    nused = bends[-1:].astype(I32)
    windows = lambda v: v.reshape(SC_WORKERS, -1, SC_WINDOW)
    d0 = dest[0]
    d1 = dest[1]

    xs = _sc_scatter_rows(hp, windows(d0), windows(d1), cap)
    yb = _experts(bstart, nblk.astype(I32), nused, xs, w_gate[0], w_up[0], w_down[0])
    out = h1
    part_len = n_tok // COMBINE_PARTS
    for part in range(COMBINE_PARTS):
        tok = slice(part * part_len, (part + 1) * part_len)
        ys = _sc_gather_rows(yb, windows(jnp.concatenate([d0[tok], d1[tok]])))
        out = _combine(ys, out, info, row(ln2_g[0]), row(ln2_b[0]), COMBINE_ROWS,
                       part, COMBINE_PARTS)
    return out.reshape(bsz, seq, d)
```

```python
import jax
import jax.numpy as jnp
import numpy as np
from jax import lax
from jax.experimental import pallas as pl
from jax.experimental.pallas import tpu as pltpu
from jax.experimental.pallas import tpu_sc as plsc

F32 = jnp.float32
BF16 = jnp.bfloat16
U32 = jnp.uint32
I32 = jnp.int32

D_MODEL = 1024
N_META = 16
BLOCK = 128
PAD_FRONT = BLOCK - N_META
HEAD_DIM = 64
ATTN_WIDTH = 512
LRU_WIDTH = 512
N_Q_HEADS = 8
N_KV_HEADS = 2
KV_WIDTH = N_KV_HEADS * HEAD_DIM
LRU_BLOCKS = 8
CONV_W = 4
LRU_C = 8.0
IN_COLS = ATTN_WIDTH + 2 * KV_WIDTH + 2 * LRU_WIDTH
N_GROUPS = 4
EXPERTS_PER_GROUP = 8
N_EXPERTS = N_GROUPS * EXPERTS_PER_GROUP
TOP_K = 2
D_FF = 512
MOE_BLOCK = 256
ALPHA = 2.0 ** 0.25
EPS = 1e-5
NEG = -1e30
LOG2E = float(np.log2(np.e))
LANES = 128
SUBLANES = 8
PACKED = D_MODEL // 2

PROJ_ROWS = 512
OUT_PROJ_ROWS = 1024
ROUTE_ROWS = 512
COMBINE_ROWS = 1024
COMBINE_PARTS = 4
X_GROUP = 4
X_AHEAD = 4
X_RING = X_AHEAD + X_GROUP
Y_RING = 2 * X_GROUP
VMEM_LIMIT = 48 * 1024 * 1024

SC_CORES = 2
SC_SUBCORES = 16
SC_WORKERS = SC_CORES * SC_SUBCORES
SC_WINDOW = 64


def _cparams(n_axes):
    return pltpu.CompilerParams(
        dimension_semantics=("arbitrary",) * n_axes, vmem_limit_bytes=VMEM_LIMIT)


def _in_proj_kernel(x_ref, w_ref, q_ref, kv_ref, xr_ref, yr_ref):
    proj = jnp.dot(x_ref[...].astype(BF16), w_ref[...], preferred_element_type=F32)
    o = 0
    for ref, width in ((q_ref, ATTN_WIDTH), (kv_ref, 2 * KV_WIDTH),
                       (xr_ref, LRU_WIDTH), (yr_ref, LRU_WIDTH)):
        ref[...] = proj[:, o:o + width].astype(ref.dtype)
        o += width


def _in_proj(x2d, w_bf16, rows):
    n = x2d.shape[0]
    widths = (ATTN_WIDTH, 2 * KV_WIDTH, LRU_WIDTH, LRU_WIDTH)
    return pl.pallas_call(
        _in_proj_kernel,
        grid=(n // rows,),
        in_specs=[pl.BlockSpec((rows, D_MODEL), lambda i: (i, 0)),
                  pl.BlockSpec((D_MODEL, IN_COLS), lambda i: (0, 0))],
        out_specs=[pl.BlockSpec((rows, w), lambda i: (i, 0)) for w in widths],
        out_shape=[jax.ShapeDtypeStruct((n, w), BF16) for w in widths],
        compiler_params=_cparams(1),
        name="in_proj",
    )(x2d, w_bf16)


def _attn_kernel(sinks_ref, q_ref, kv_ref, kvm_ref, bias_ref, g_ref, o_ref,
                 klo, khi, vlo, vhi):
    nbx = q_ref.shape[0] // BLOCK
    lo_lanes = lax.broadcasted_iota(I32, (BLOCK // 2, LANES), 1) < HEAD_DIM

    def layout_block(n, blk):
        rows = pl.ds(pl.multiple_of(n * BLOCK, BLOCK), BLOCK)
        as_bf16 = lambda words: pltpu.bitcast(words, BF16)
        for src, dst_lo, dst_hi in ((blk[:, :KV_WIDTH], klo, khi), (blk[:, KV_WIDTH:], vlo, vhi)):
            w = pltpu.bitcast(src, U32)
            r = pltpu.roll(w, HEAD_DIM, axis=1)
            zero = jnp.zeros_like(w)
            dst_lo[0, rows, :] = as_bf16(jnp.where(lo_lanes, w, zero))
            dst_hi[0, rows, :] = as_bf16(jnp.where(lo_lanes, zero, r))
            dst_lo[1, rows, :] = as_bf16(jnp.where(lo_lanes, r, zero))
            dst_hi[1, rows, :] = as_bf16(jnp.where(lo_lanes, zero, w))

    layout_block(0, kvm_ref[...])

    def layout_body(n, carry):
        layout_block(n + 1, kv_ref[pl.ds(pl.multiple_of(n * BLOCK, BLOCK), BLOCK), :])
        return carry

    lax.fori_loop(0, nbx, layout_body, 0, unroll=4)

    ones_lo = jnp.where(lax.broadcasted_iota(I32, (2 * BLOCK, LANES), 1) < HEAD_DIM,
                        1.0, 0.0).astype(BF16)
    ones_hi = (1.0 - ones_lo.astype(F32)).astype(BF16)
    top_rows = lax.broadcasted_iota(I32, (2 * BLOCK, 1), 0) < BLOCK
    lo_half = lax.broadcasted_iota(I32, (2 * BLOCK, LANES), 1) < HEAD_DIM

    def block(i, carry):
        q_rows = pl.ds(pl.multiple_of(i * BLOCK, BLOCK), BLOCK)
        win = pl.ds(pl.multiple_of(i * BLOCK, BLOCK), 2 * BLOCK)
        q = q_ref[q_rows, :]
        first = jnp.minimum(i, 1)
        outs = []
        for j in range(N_KV_HEADS):
            q2 = jnp.concatenate([q[:, (2 * j) * LANES:(2 * j + 1) * LANES],
                                  q[:, (2 * j + 1) * LANES:(2 * j + 2) * LANES]], axis=0)
            kc = jnp.concatenate([klo[j, win, :], khi[j, win, :]], axis=0)
            s = lax.dot_general(q2, kc, (((1,), (1,)), ((), ())), preferred_element_type=F32)
            s = s + bias_ref[first, j]
            ps, es = [], []
            for c in range(2):
                sink = jnp.where(top_rows, sinks_ref[4 * j + c], sinks_ref[4 * j + 2 + c])
                sc = s[:, c * 2 * BLOCK:(c + 1) * 2 * BLOCK]
                m = jnp.maximum(jnp.max(sc, axis=1, keepdims=True), sink)
                ps.append(jnp.exp2(sc - m).astype(BF16))
                es.append(jnp.exp2(sink - m))
            v_lo = jnp.concatenate([vlo[j, win, :], ones_lo], axis=1)
            v_hi = jnp.concatenate([vhi[j, win, :], ones_hi], axis=1)
            r = (jnp.dot(ps[0], v_lo, preferred_element_type=F32)
                 + jnp.dot(ps[1], v_hi, preferred_element_type=F32))
            den = r[:, LANES:] + jnp.where(lo_half, es[0], es[1])
            o2 = r[:, :LANES] * (1.0 / den)
            outs += [o2[:BLOCK], o2[BLOCK:]]
        out = jnp.concatenate(outs, axis=1)
        ms = jnp.mean(out * out, axis=1, keepdims=True)
        o_ref[q_rows, :] = (out * lax.rsqrt(ms + EPS) * g_ref[...]).astype(o_ref.dtype)
        return carry

    lax.fori_loop(0, nbx, block, 0, unroll=8)


def _attention(sinks, q, kv, kvm, bias, g_attn, bsz, nbx):
    seq = nbx * BLOCK
    const2 = lambda b: (0, 0)
    kv_scratch = pltpu.VMEM((N_KV_HEADS, seq + BLOCK, LANES), BF16)
    return pl.pallas_call(
        _attn_kernel,
        grid=(bsz,),
        in_specs=[pl.BlockSpec(memory_space=pltpu.SMEM),
                  pl.BlockSpec((None, seq, ATTN_WIDTH), lambda b: (b, 0, 0)),
                  pl.BlockSpec((None, seq, 2 * KV_WIDTH), lambda b: (b, 0, 0)),
                  pl.BlockSpec((BLOCK, 2 * KV_WIDTH), const2),
                  pl.BlockSpec((2, N_KV_HEADS, 2 * BLOCK, 4 * BLOCK), lambda b: (0, 0, 0, 0)),
                  pl.BlockSpec((1, ATTN_WIDTH), const2)],
        out_specs=pl.BlockSpec((None, seq, ATTN_WIDTH), lambda b: (b, 0, 0)),
        out_shape=jax.ShapeDtypeStruct((bsz, seq, ATTN_WIDTH), BF16),
        scratch_shapes=[kv_scratch, kv_scratch, kv_scratch, kv_scratch],
        compiler_params=_cparams(1),
        name="attention",
    )(sinks, q, kv, kvm, bias, g_attn)


def _sigmoid(v):
    return 0.5 * jnp.tanh(0.5 * v) + 0.5


def _gelu_tanh(y):
    c = float(np.sqrt(2.0 / np.pi))
    half = 0.5 * y
    return half + half * jnp.tanh(y * (c + (c * 0.044715) * (y * y)))


LRU_CHUNK = 44
LRU_SEG = SUBLANES * LRU_CHUNK
LRU_SLABS = LRU_WIDTH // LANES


def _lru_kernel(xr_ref, yr_ref, xrm_ref, yrm_ref, cw_ref, cb_ref, wa_ref, wx_ref, ba_ref,
                bx_ref, lam_ref, g_ref, o_ref, x_st, y_st, o_st, s_st, xtail, hcar):
    seq = xr_ref.shape[0]
    n_seg = (seq + BLOCK) // LRU_SEG
    xtail[...] = jnp.zeros_like(xtail)
    hcar[...] = jnp.zeros_like(hcar)
    lam = lam_ref[...]
    softplus_neg = jnp.maximum(-lam, 0.0) + jnp.log(1.0 + jnp.exp(-jnp.abs(lam)))
    sub = lax.broadcasted_iota(jnp.int32, (SUBLANES, LANES), 0)

    def strided(j):
        return pl.ds(j, SUBLANES, stride=LRU_CHUNK)

    def piece(v, j):
        return v[j * SUBLANES:(j + 1) * SUBLANES, :]

    def segment(k, first):
        if first:
            head = LRU_SEG - BLOCK
            x_nat = jnp.concatenate([xrm_ref[...], xr_ref[0:head, :]], axis=0).astype(F32)
            y_nat = jnp.concatenate([yrm_ref[...], yr_ref[0:head, :]], axis=0).astype(F32)
        else:
            rows = pl.ds(pl.multiple_of(k * LRU_SEG - BLOCK, 2 * SUBLANES), LRU_SEG)
            x_nat = xr_ref[rows, :].astype(F32)
            y_nat = yr_ref[rows, :].astype(F32)
        for c in range(LRU_SLABS):
            x_st[c] = x_nat[:, c * LANES:(c + 1) * LANES]
            y_st[c] = y_nat[:, c * LANES:(c + 1) * LANES]
        first_row = k * LRU_SEG + LRU_CHUNK * sub
        sumsq = [jnp.zeros((SUBLANES, LANES), F32) for _ in range(LRU_CHUNK)]

        for c in range(LRU_SLABS):
            lanes = slice(c * LANES, (c + 1) * LANES)
            x = [x_st[c, strided(j), :] for j in range(LRU_CHUNK)]
            before = []
            for d in range(1, CONV_W):
                from_prev_chunk = pltpu.roll(x[LRU_CHUNK - d], 1, axis=0)
                before.append(jnp.where(sub == 0, xtail[d - 1:d, lanes], from_prev_chunk))
            for d in range(1, CONV_W):
                xtail[d - 1:d, lanes] = x[LRU_CHUNK - d][SUBLANES - 1:SUBLANES, :]

            def x_at(j):
                return x[j] if j >= 0 else before[-j - 1]

            taps = [cw_ref[t:t + 1, lanes] for t in range(CONV_W)]
            bias = cb_ref[:, lanes]
            xc = jnp.concatenate(
                [bias + sum(taps[t] * x_at(j - (CONV_W - 1) + t) for t in range(CONV_W))
                 for j in range(LRU_CHUNK)], axis=0)
            xcb = xc.astype(BF16)
            tr = jnp.tanh(jnp.dot(xcb, wa_ref[c], preferred_element_type=F32) + ba_ref[:, lanes])
            ti = jnp.tanh(jnp.dot(xcb, wx_ref[c], preferred_element_type=F32) + bx_ref[:, lanes])
            log_a_half = (-0.5 * LRU_C) * softplus_neg[:, lanes]
            a = jnp.exp(log_a_half * tr + log_a_half)
            half_xc = 0.5 * xc
            gated_x = half_xc * ti + half_xc
            z = 1.0 - a * a
            u = jnp.where(z > 0.0, z * lax.rsqrt(z), 0.0) * gated_x

            h = jnp.zeros((SUBLANES, LANES), F32)
            p = jnp.ones((SUBLANES, LANES), F32)
            hs, ps = [], []
            for j in range(LRU_CHUNK):
                aj = piece(a, j)
                uj = piece(u, j)
                if first:
                    uj = jnp.where(first_row + j >= PAD_FRONT, uj, 0.0)
                h = aj * h + uj
                p = aj * p
                hs.append(h)
                ps.append(p)
            entry = [hcar[:, lanes]]
            for s in range(SUBLANES):
                entry.append(h[s:s + 1, :] + p[s:s + 1, :] * entry[s])
            hcar[:, lanes] = entry[SUBLANES]
            entry_rows = jnp.concatenate(entry[:SUBLANES], axis=0)

            for j in range(LRU_CHUNK):
                state = hs[j] + ps[j] * entry_rows
                out = state * _gelu_tanh(y_st[c, strided(j), :])
                sumsq[j] = sumsq[j] + out * out
                o_st[c, strided(j), :] = out

        for j in range(LRU_CHUNK):
            ms = jnp.sum(sumsq[j], axis=1, keepdims=True) * (1.0 / LRU_WIDTH)
            s_st[strided(j), :] = jnp.broadcast_to(lax.rsqrt(ms + EPS), (SUBLANES, LANES))
        scale = s_st[...]
        for c in range(LRU_SLABS):
            lanes = slice(c * LANES, (c + 1) * LANES)
            normed = (o_st[c] * scale * g_ref[:, lanes]).astype(o_ref.dtype)
            if first:
                o_ref[0:LRU_SEG - BLOCK, lanes] = normed[BLOCK:, :]
            else:
                o_ref[rows, lanes] = normed

    assert BLOCK <= LRU_SEG
    segment(0, True)

    def later_segment(k, carry):
        segment(k, False)
        return carry

    lax.fori_loop(1, n_seg, later_segment, 0)


def _rglru(xr, yr, xrm, yrm, cw, cb, wa, wx, ba, bx, lam, g_lru, bsz, nbx):
    seq = nbx * BLOCK
    assert (seq + BLOCK) % LRU_SEG == 0
    main = pl.BlockSpec((None, seq, LRU_WIDTH), lambda b: (b, 0, 0))
    const2 = lambda b: (0, 0)
    row_spec = pl.BlockSpec((1, LRU_WIDTH), const2)
    gate_spec = pl.BlockSpec((LRU_SLABS, LANES, LANES), lambda b: (0, 0, 0))
    slabs = pltpu.VMEM((LRU_SLABS, LRU_SEG, LANES), F32)
    return pl.pallas_call(
        _lru_kernel,
        grid=(bsz,),
        in_specs=[main, main,
                  pl.BlockSpec((BLOCK, LRU_WIDTH), const2),
                  pl.BlockSpec((BLOCK, LRU_WIDTH), const2),
                  pl.BlockSpec((CONV_W, LRU_WIDTH), const2),
                  row_spec, gate_spec, gate_spec,
                  row_spec, row_spec, row_spec, row_spec],
        out_specs=main,
        out_shape=jax.ShapeDtypeStruct((bsz, seq, LRU_WIDTH), BF16),
        scratch_shapes=[slabs, slabs, slabs,
                        pltpu.VMEM((LRU_SEG, LANES), F32),
                        pltpu.VMEM((SUBLANES, LRU_WIDTH), F32),
                        pltpu.VMEM((1, LRU_WIDTH), F32)],
        compiler_params=_cparams(1),
        name="rglru",
    )(xr, yr, xrm, yrm, cw, cb, wa, wx, ba, bx, lam, g_lru)


def _pack_rows(v):
    bits = lax.bitcast_convert_type(v.astype(BF16).astype(F32), U32)
    return (bits[:, :PACKED] >> 16) | (bits[:, PACKED:] & jnp.uint32(0xFFFF0000))


def _unpack_rows(w):
    lo = lax.bitcast_convert_type(w << 16, F32)
    hi = lax.bitcast_convert_type(w & jnp.uint32(0xFFFF0000), F32)
    return lo, hi


def _layer_norm(z, g, b):
    mu = jnp.mean(z, axis=1, keepdims=True)
    zc = z - mu
    var = jnp.mean(zc * zc, axis=1, keepdims=True)
    return zc * lax.rsqrt(var + EPS) * g + b


def _out_proj_kernel(a_ref, l_ref, x_ref, wa_ref, wl_ref, g_ref, b_ref, wrt_hi_ref, wrt_lo_ref,
                     brt_ref, h_ref, hp_ref, info_ref, cnt_ref):
    @pl.when(pl.program_id(0) == 0)
    def _():
        cnt_ref[...] = jnp.zeros_like(cnt_ref)

    for c in range(h_ref.shape[0] // PROJ_ROWS):
        rows = slice(c * PROJ_ROWS, (c + 1) * PROJ_ROWS)
        _out_proj_rows(a_ref[rows, :], l_ref[rows, :], x_ref[rows, :], wa_ref, wl_ref, g_ref,
                       b_ref, wrt_hi_ref, wrt_lo_ref, brt_ref,
                       h_ref.at[rows, :], hp_ref.at[rows, :], info_ref.at[:, rows], cnt_ref)


def _out_proj_rows(a, l, x, wa_ref, wl_ref, g_ref, b_ref, wrt_hi_ref, wrt_lo_ref, brt_ref,
                   h_ref, hp_ref, info_ref, cnt_ref):
    mix = jnp.dot(a, wa_ref[...], preferred_element_type=F32)
    mix = mix + jnp.dot(l, wl_ref[...], preferred_element_type=F32)
    h = _layer_norm(ALPHA * x + mix, g_ref[...], b_ref[...])
    h_ref[...] = h
    hp_ref[...] = _pack_rows(h)

    h_hi = h.astype(BF16)
    h_lo = (h - h_hi.astype(F32)).astype(BF16)
    nt = (((1,), (1,)), ((), ()))
    lg = (lax.dot_general(wrt_hi_ref[...], h_hi, nt, preferred_element_type=F32)
          + lax.dot_general(wrt_lo_ref[...], h_hi, nt, preferred_element_type=F32)
          + lax.dot_general(wrt_hi_ref[...], h_lo, nt, preferred_element_type=F32)) + brt_ref[...]
    tile_shape = (SUBLANES, h.shape[0])
    sub = lax.broadcasted_iota(I32, tile_shape, 0)
    ninf = -jnp.inf
    t0 = lg[0:SUBLANES]
    gl = jnp.where(sub < N_GROUPS, t0, ninf)
    gmax = jnp.max(gl, axis=0, keepdims=True)
    g_idx = jnp.min(jnp.where(gl == gmax, sub, SUBLANES), axis=0, keepdims=True)
    g_w = 1.0 / jnp.sum(jnp.where(sub < N_GROUPS, jnp.exp(t0 - gmax), 0.0),
                        axis=0, keepdims=True)
    el = lg[SUBLANES:2 * SUBLANES]
    for g in range(1, N_GROUPS):
        el = jnp.where(g_idx == g, lg[(g + 1) * SUBLANES:(g + 2) * SUBLANES], el)
    v1 = jnp.max(el, axis=0, keepdims=True)
    i1 = jnp.min(jnp.where(el == v1, sub, SUBLANES), axis=0, keepdims=True)
    el2 = jnp.where(sub == i1, ninf, el)
    v2 = jnp.max(el2, axis=0, keepdims=True)
    i2 = jnp.min(jnp.where(el2 == v2, sub, SUBLANES), axis=0, keepdims=True)
    t = jnp.exp(v2 - v1)
    w1 = 1.0 / (1.0 + t)
    w2 = t * w1
    e_base = g_idx * EXPERTS_PER_GROUP
    e1 = e_base + i1
    e2 = e_base + i2
    info_ref[...] = jnp.where(sub == 0, e1.astype(F32),
                              jnp.where(sub == 1, e2.astype(F32),
                                        jnp.where(sub == 2, g_w * w1,
                                                  jnp.where(sub == 3, g_w * w2, 0.0))))
    expert = lax.broadcasted_iota(I32, (N_EXPERTS, h.shape[0]), 0)
    chosen = (expert == e1).astype(F32) + (expert == e2).astype(F32)
    cnt_ref[...] += jnp.sum(chosen, axis=1, keepdims=True)


ROUTER_ROWS = (N_GROUPS + 1) * SUBLANES


def _out_proj(attn_n, lru_n, x2d, wo_a, wo_l, ln_g, ln_b, w_rt_hi, w_rt_lo, b_rt, rows):
    n = x2d.shape[0]
    const = lambda i: (0, 0)
    tile = lambda w: pl.BlockSpec((rows, w), lambda i: (i, 0))
    return pl.pallas_call(
        _out_proj_kernel,
        grid=(n // rows,),
        in_specs=[tile(ATTN_WIDTH), tile(LRU_WIDTH), tile(D_MODEL),
                  pl.BlockSpec((ATTN_WIDTH, D_MODEL), const),
                  pl.BlockSpec((LRU_WIDTH, D_MODEL), const),
                  pl.BlockSpec((1, D_MODEL), const),
                  pl.BlockSpec((1, D_MODEL), const),
                  pl.BlockSpec((ROUTER_ROWS, D_MODEL), const),
                  pl.BlockSpec((ROUTER_ROWS, D_MODEL), const),
                  pl.BlockSpec((ROUTER_ROWS, 1), const)],
        out_specs=[tile(D_MODEL), tile(PACKED),
                   pl.BlockSpec((SUBLANES, rows), lambda i: (0, i)),
                   pl.BlockSpec((N_EXPERTS, LANES), const)],
        out_shape=[jax.ShapeDtypeStruct((n, D_MODEL), F32),
                   jax.ShapeDtypeStruct((n, PACKED), U32),
                   jax.ShapeDtypeStruct((SUBLANES, n), F32),
                   jax.ShapeDtypeStruct((N_EXPERTS, LANES), F32)],
        compiler_params=_cparams(1),
        name="out_proj",
    )(attn_n, lru_n, x2d, wo_a, wo_l, ln_g, ln_b, w_rt_hi, w_rt_lo, b_rt)


def _route_kernel(info_ref, cnt_ref, tri_ref, dest_ref, carry, pstart):
    t = pl.program_id(0)
    info = info_ref[...]
    shape = (N_EXPERTS, info.shape[1])
    expert = lax.broadcasted_iota(I32, shape, 0)
    oh1 = (expert == info[0:1, :].astype(I32)).astype(F32)
    oh2 = (expert == info[1:2, :].astype(I32)).astype(F32)
    both = oh1 + oh2

    @pl.when(t == 0)
    def _():
        c = cnt_ref[...].astype(I32)
        padded = ((c + (MOE_BLOCK - 1)) // MOE_BLOCK) * MOE_BLOCK
        e = lax.broadcasted_iota(I32, (N_EXPERTS, LANES), 0)
        scan = padded
        for d in (1, 2, 4, 8, 16):
            scan = scan + jnp.where(e >= d, pltpu.roll(scan, d, axis=0), 0)
        pstart[...] = (scan - padded)[:, 0:1].astype(F32)
        carry[...] = jnp.zeros_like(carry)

    before = jnp.dot(both.astype(BF16), tri_ref[...], preferred_element_type=F32)
    row_of = before + (carry[...] + pstart[...])
    r1 = jnp.sum(oh1 * row_of, axis=0, keepdims=True)
    r2 = jnp.sum(oh2 * row_of, axis=0, keepdims=True)
    sub = lax.broadcasted_iota(I32, dest_ref.shape, 0)
    dest_ref[...] = jnp.where(sub == 0, r1, jnp.where(sub == 1, r2, 0.0)).astype(I32)
    carry[...] += jnp.sum(both, axis=1, keepdims=True)


def _route(info_t, cnt, cols):
    n = info_t.shape[1]
    tri = jnp.asarray(np.triu(np.ones((cols, cols), np.float32), 1), BF16)
    return pl.pallas_call(
        _route_kernel,
        grid=(n // cols,),
        in_specs=[pl.BlockSpec((SUBLANES, cols), lambda t: (0, t)),
                  pl.BlockSpec((N_EXPERTS, LANES), lambda t: (0, 0)),
                  pl.BlockSpec((cols, cols), lambda t: (0, 0))],
        out_specs=pl.BlockSpec((SUBLANES, cols), lambda t: (0, t)),
        out_shape=jax.ShapeDtypeStruct((SUBLANES, n), I32),
        scratch_shapes=[pltpu.VMEM((N_EXPERTS, 1), F32), pltpu.VMEM((N_EXPERTS, 1), F32)],
        compiler_params=_cparams(1),
        name="route",
    )(info_t, cnt, tri)


def _sc_mesh():
    return plsc.VectorSubcoreMesh(core_axis_name="core", subcore_axis_name="subcore")


def _sc_worker_id():
    return lax.axis_index("subcore") * SC_CORES + lax.axis_index("core")


def _sc_scatter_rows(rows, d0, d1, cap):
    n, width = rows.shape
    per_worker = n // SC_WORKERS
    n_win = per_worker // SC_WINDOW

    def body(x_hbm, d0_hbm, d1_hbm, o_hbm, i0_v, i1_v, rows_v, rsem, sem0, sem1):
        wid = _sc_worker_id()
        pltpu.sync_copy(d0_hbm.at[wid], i0_v)
        pltpu.sync_copy(d1_hbm.at[wid], i1_v)

        def read(j):
            src = x_hbm.at[pl.ds(wid * per_worker + j * SC_WINDOW, SC_WINDOW)]
            return pltpu.make_async_copy(src, rows_v.at[j % 2], rsem.at[j % 2])

        def scatters(j):
            return (pltpu.make_async_copy(rows_v.at[j % 2], o_hbm.at[i0_v.at[j]], sem0.at[j % 2]),
                    pltpu.make_async_copy(rows_v.at[j % 2], o_hbm.at[i1_v.at[j]], sem1.at[j % 2]))

        read(0).start()
        for j in range(n_win):
            if j + 1 < n_win:
                if j >= 1:
                    for cp in scatters(j - 1):
                        cp.wait()
                read(j + 1).start()
            read(j).wait()
            for cp in scatters(j):
                cp.start()
        for j in range(max(n_win - 2, 0), n_win):
            for cp in scatters(j):
                cp.wait()

    return pl.kernel(
        body,
        out_type=jax.ShapeDtypeStruct((cap, width), rows.dtype),
        mesh=_sc_mesh(),
        scratch_types=[pltpu.VMEM((n_win, SC_WINDOW), I32), pltpu.VMEM((n_win, SC_WINDOW), I32),
                       pltpu.VMEM((2, SC_WINDOW, width), rows.dtype),
                       pltpu.SemaphoreType.DMA((2,)), pltpu.SemaphoreType.DMA((2,)),
                       pltpu.SemaphoreType.DMA((2,))],
        name="dispatch",
    )(rows, d0, d1)


def _sc_gather_rows(table, idx):
    width = table.shape[1]
    n_win = idx.shape[1]
    per_worker = n_win * SC_WINDOW

    def body(y_hbm, i_hbm, o_hbm, i_v, rows_v, gsem, wsem):
        wid = _sc_worker_id()
        pltpu.sync_copy(i_hbm.at[wid], i_v)

        def gather(j):
            return pltpu.make_async_copy(y_hbm.at[i_v.at[j]], rows_v.at[j % 2], gsem.at[j % 2])

        def write(j):
            dst = o_hbm.at[pl.ds(wid * per_worker + j * SC_WINDOW, SC_WINDOW)]
            return pltpu.make_async_copy(rows_v.at[j % 2], dst, wsem.at[j % 2])

        gather(0).start()
        for j in range(n_win):
            if j + 1 < n_win:
                if j >= 1:
                    write(j - 1).wait()
                gather(j + 1).start()
            gather(j).wait()
            write(j).start()
        for j in range(max(n_win - 2, 0), n_win):
            write(j).wait()

    return pl.kernel(
        body,
        out_type=jax.ShapeDtypeStruct((SC_WORKERS * per_worker, width), table.dtype),
        mesh=_sc_mesh(),
        scratch_types=[pltpu.VMEM((n_win, SC_WINDOW), I32),
                       pltpu.VMEM((2, SC_WINDOW, width), table.dtype),
                       pltpu.SemaphoreType.DMA((2,)), pltpu.SemaphoreType.DMA((2,))],
        name="collect",
    )(table, idx)


def _expert_kernel(bstart_ref, nblk_ref, nused_ref, xs_hbm, wg_ref, wu_ref, wd_ref, yb_hbm,
                   xbuf, ybuf, zbuf, xsem, ysem, zsem, wg_b, wu_b, wd_b):
    e = pl.program_id(0)
    nused = nused_ref[0]
    n_blocks = yb_hbm.shape[0] // MOE_BLOCK

    def rows(b):
        return pl.ds(pl.multiple_of(b * MOE_BLOCK, MOE_BLOCK), MOE_BLOCK)

    def x_copy(b):
        slot = b % X_RING
        return pltpu.make_async_copy(xs_hbm.at[rows(b)], xbuf.at[slot], xsem.at[slot])

    def y_copy(b):
        slot = b % Y_RING
        return pltpu.make_async_copy(ybuf.at[slot], yb_hbm.at[rows(b)], ysem.at[slot])

    @pl.when(e == 0)
    def _():
        for b in range(X_AHEAD):
            @pl.when(b < nused)
            def _():
                x_copy(b).start()

    wg_b[...] = wg_ref[...].astype(BF16)
    wu_b[...] = wu_ref[...].astype(BF16)
    wd_b[...] = wd_ref[...].astype(BF16)

    def run_blocks(b, count):
        for k in range(count):
            x_copy(b + k).wait()
        for k in range(count):
            nxt = b + X_AHEAD + k

            @pl.when(nxt < nused)
            def _():
                x_copy(nxt).start()

            @pl.when(b + k >= Y_RING)
            def _():
                y_copy(b + k - Y_RING).wait()

        words = jnp.concatenate([xbuf[(b + k) % X_RING] for k in range(count)], axis=0)
        lo, hi = _unpack_rows(words)
        lo = lo.astype(BF16)
        hi = hi.astype(BF16)
        g = (jnp.dot(lo, wg_b[0:PACKED, :], preferred_element_type=F32)
             + jnp.dot(hi, wg_b[PACKED:, :], preferred_element_type=F32))
        u = (jnp.dot(lo, wu_b[0:PACKED, :], preferred_element_type=F32)
             + jnp.dot(hi, wu_b[PACKED:, :], preferred_element_type=F32))
        mid = (g * _sigmoid(g) * u).astype(BF16)
        y = _pack_rows(jnp.dot(mid, wd_b[...], preferred_element_type=F32))
        for k in range(count):
            ybuf[(b + k) % Y_RING] = y[k * MOE_BLOCK:(k + 1) * MOE_BLOCK]
            y_copy(b + k).start()

    b0 = bstart_ref[e]
    nb = nblk_ref[e]

    def group(i, carry):
        run_blocks(b0 + X_GROUP * i, X_GROUP)
        return carry

    lax.fori_loop(0, nb // X_GROUP, group, 0)
    done = nb - nb % X_GROUP
    size = X_GROUP // 2
    while size >= 1:
        @pl.when((nb // size) % 2 == 1)
        def _(size=size, done=done):
            run_blocks(b0 + done, size)

        done = done + (nb // size) % 2 * size
        size //= 2

    @pl.when(e == pl.num_programs(0) - 1)
    def _():
        for back in range(Y_RING, 0, -1):
            @pl.when(nused >= back)
            def _():
                y_copy(nused - back).wait()

        zbuf[...] = jnp.zeros_like(zbuf)

        def z_copy(b):
            return pltpu.make_async_copy(zbuf, yb_hbm.at[rows(b)], zsem.at[0])

        def z_start(b, carry):
            z_copy(b).start()
            return carry

        def z_wait(b, carry):
            z_copy(b).wait()
            return carry

        lax.fori_loop(nused, n_blocks, z_start, 0)
        lax.fori_loop(nused, n_blocks, z_wait, 0)


def _experts(bstart, nblk, nused, xs, w_gate, w_up, w_down):
    cap = xs.shape[0]
    w_idx = lambda e, bs, nb, nu: (e, 0, 0)
    grid_spec = pltpu.PrefetchScalarGridSpec(
        num_scalar_prefetch=3,
        grid=(N_EXPERTS,),
        in_specs=[pl.BlockSpec(memory_space=pl.ANY),
                  pl.BlockSpec((None, D_MODEL, D_FF), w_idx),
                  pl.BlockSpec((None, D_MODEL, D_FF), w_idx),
                  pl.BlockSpec((None, D_FF, D_MODEL), w_idx)],
        out_specs=pl.BlockSpec(memory_space=pl.ANY),
        scratch_shapes=[pltpu.VMEM((X_RING, MOE_BLOCK, PACKED), U32),
                        pltpu.VMEM((Y_RING, MOE_BLOCK, PACKED), U32),
                        pltpu.VMEM((MOE_BLOCK, PACKED), U32),
                        pltpu.SemaphoreType.DMA((X_RING,)),
                        pltpu.SemaphoreType.DMA((Y_RING,)),
                        pltpu.SemaphoreType.DMA((1,)),
                        pltpu.VMEM((D_MODEL, D_FF), BF16),
                        pltpu.VMEM((D_MODEL, D_FF), BF16),
                        pltpu.VMEM((D_FF, D_MODEL), BF16)])
    return pl.pallas_call(
        _expert_kernel,
        grid_spec=grid_spec,
        out_shape=jax.ShapeDtypeStruct((cap, PACKED), U32),
        compiler_params=_cparams(1),
        name="experts",
    )(bstart, nblk, nused, xs, w_gate, w_up, w_down)


def _combine_kernel(y0_ref, y1_ref, h_ref, info_ref, g_ref, b_ref, o_ref):
    info = info_ref[...].T
    g0 = info[:, 2:3]
    g1 = info[:, 3:4]
    lo0, hi0 = _unpack_rows(y0_ref[...])
    lo1, hi1 = _unpack_rows(y1_ref[...])
    y = jnp.concatenate([g0 * lo0 + g1 * lo1, g0 * hi0 + g1 * hi1], axis=1)
    o_ref[...] = _layer_norm(ALPHA * h_ref[...] + y, g_ref[...], b_ref[...])


def _combine(ys, h, info, ln_g, ln_b, rows, part, n_parts):
    n = h.shape[0]
    steps = n // n_parts // rows
    off = part * steps
    const = lambda i: (0, 0)
    return pl.pallas_call(
        _combine_kernel,
        grid=(steps,),
        in_specs=[pl.BlockSpec((rows, PACKED), lambda i: (i, 0)),
                  pl.BlockSpec((rows, PACKED), lambda i: (i + steps, 0)),
                  pl.BlockSpec((rows, D_MODEL), lambda i: (i + off, 0)),
                  pl.BlockSpec((SUBLANES, rows), lambda i: (0, i + off)),
                  pl.BlockSpec((1, D_MODEL), const),
                  pl.BlockSpec((1, D_MODEL), const)],
        out_specs=pl.BlockSpec((rows, D_MODEL), lambda i: (i + off, 0)),
        out_shape=jax.ShapeDtypeStruct((n, D_MODEL), F32),
        input_output_aliases={2: 0},
        compiler_params=_cparams(1),
        name="combine",
    )(ys, ys, h, info, ln_g, ln_b)


def _alibi_bias():
    qi = np.arange(BLOCK)[:, None]
    kj = np.arange(2 * BLOCK)[None, :]
    dist = qi - kj + BLOCK
    band = (dist >= 0) & (dist < BLOCK)
    slopes = np.exp2(-8.0 * np.arange(1, N_Q_HEADS + 1, dtype=np.float32) / N_Q_HEADS)
    bias = np.where(band[None], -slopes[:, None, None] * dist[None].astype(np.float32), NEG)
    bias = bias * LOG2E
    first = np.where((kj >= PAD_FRONT)[None], bias, NEG)
    out = np.empty((2, N_KV_HEADS, 2 * BLOCK, 4 * BLOCK), np.float32)
    for v, per_head in enumerate((first, bias)):
        for j in range(N_KV_HEADS):
            out[v, j] = np.block([[per_head[4 * j], per_head[4 * j + 1]],
                                  [per_head[4 * j + 2], per_head[4 * j + 3]]])
    return jnp.asarray(out, F32)


def _slab_gates(w):
    nb, c, _ = w.shape
    per = LANES // c
    w = w.reshape(nb // per, per, c, c)
    eye = jnp.eye(per, dtype=w.dtype)
    return jnp.einsum('spcd,pq->spcqd', w, eye).reshape(nb // per, LANES, LANES).astype(BF16)


def kernel(x, meta_tokens, w_in, conv_w, conv_b, lru_wa, lru_ba, lru_wx, lru_bx, lru_lambda,
           attn_sinks, g_attn, g_lru, w_out, ln1_g, ln1_b, w_group, b_group, w_router,
           b_router, w_gate, w_up, w_down, ln2_g, ln2_b):
    bsz, seq, d = x.shape
    nbx = seq // BLOCK
    n_tok = bsz * seq
    x2d = x.reshape(n_tok, d)
    row = lambda v: v.reshape(1, -1).astype(F32)

    q_scale = jnp.concatenate([jnp.full((ATTN_WIDTH,), LOG2E * HEAD_DIM ** -0.5, F32),
                               jnp.ones((IN_COLS - ATTN_WIDTH,), F32)])
    w_in_b = (w_in[0] * q_scale).astype(BF16)
    meta_blk = jnp.concatenate([jnp.zeros((PAD_FRONT, d), F32), meta_tokens.astype(F32)], axis=0)
    q, kv, xr, yr = _in_proj(x2d, w_in_b, PROJ_ROWS)
    qm, kvm, xrm, yrm = _in_proj(meta_blk, w_in_b, BLOCK)
    shp = lambda a: a.reshape(bsz, seq, a.shape[-1])

    attn_n = _attention(attn_sinks[0].astype(F32) * LOG2E, shp(q), shp(kv), kvm, _alibi_bias(),
                        row(g_attn[0]), bsz, nbx)
    lru_n = _rglru(shp(xr), shp(yr), xrm, yrm, conv_w[0].astype(F32), row(conv_b[0]),
                   _slab_gates(0.5 * lru_wa[0]), _slab_gates(0.5 * lru_wx[0]),
                   0.5 * row(lru_ba[0]), 0.5 * row(lru_bx[0]), row(lru_lambda[0]),
                   row(g_lru[0]), bsz, nbx)

    w_out_b = w_out[0].astype(BF16)
    gpad = SUBLANES - N_GROUPS
    w_rt = jnp.concatenate(
        [w_group[0].T, jnp.zeros((gpad, d), F32),
         jnp.transpose(w_router[0], (0, 2, 1)).reshape(N_EXPERTS, d)], axis=0).astype(F32)
    w_rt_hi = w_rt.astype(BF16)
    w_rt_lo = (w_rt - w_rt_hi.astype(F32)).astype(BF16)
    b_rt = jnp.concatenate([b_group[0], jnp.zeros((gpad,), F32),
                            b_router[0].reshape(-1)]).astype(F32).reshape(ROUTER_ROWS, 1)
    h1, hp, info, cnt = _out_proj(
        attn_n.reshape(n_tok, ATTN_WIDTH), lru_n.reshape(n_tok, LRU_WIDTH), x2d,
        w_out_b[:ATTN_WIDTH], w_out_b[ATTN_WIDTH:], row(ln1_g[0]), row(ln1_b[0]),
        w_rt_hi, w_rt_lo, b_rt, OUT_PROJ_ROWS)

    dest = _route(info, cnt, ROUTE_ROWS)
    n_slots = n_tok * TOP_K
    n_blocks = n_slots // MOE_BLOCK + N_EXPERTS
    cap = n_blocks * MOE_BLOCK
    nblk = (cnt[:, 0].astype(I32) + MOE_BLOCK - 1) // MOE_BLOCK
    bends = jnp.cumsum(nblk)
    bstart = (bends - nblk).astype(I32)
    nused = bends[-1:].astype(I32)
    windows = lambda v: v.reshape(SC_WORKERS, -1, SC_WINDOW)
    d0 = dest[0]
    d1 = dest[1]

    xs = _sc_scatter_rows(hp, windows(d0), windows(d1), cap)
    yb = _experts(bstart, nblk.astype(I32), nused, xs, w_gate[0], w_up[0], w_down[0])
    out = h1
    part_len = n_tok // COMBINE_PARTS
    for part in range(COMBINE_PARTS):
        tok = slice(part * part_len, (part + 1) * part_len)
        ys = _sc_gather_rows(yb, windows(jnp.concatenate([d0[tok], d1[tok]])))
        out = _combine(ys, out, info, row(ln2_g[0]), row(ln2_b[0]), COMBINE_ROWS,
                       part, COMBINE_PARTS)
    return out.reshape(bsz, seq, d)
```

```python
import jax
import jax.numpy as jnp
import numpy as np
from jax import lax
from jax.experimental import pallas as pl
from jax.experimental.pallas import tpu as pltpu
from jax.experimental.pallas import tpu_sc as plsc

F32 = jnp.float32
BF16 = jnp.bfloat16
U32 = jnp.uint32
I32 = jnp.int32

D_MODEL = 1024
N_META = 16
BLOCK = 128
PAD_FRONT = BLOCK - N_META
HEAD_DIM = 64
ATTN_WIDTH = 512
LRU_WIDTH = 512
N_Q_HEADS = 8
N_KV_HEADS = 2
KV_WIDTH = N_KV_HEADS * HEAD_DIM
LRU_BLOCKS = 8
CONV_W = 4
LRU_C = 8.0
IN_COLS = ATTN_WIDTH + 2 * KV_WIDTH + 2 * LRU_WIDTH
N_GROUPS = 4
EXPERTS_PER_GROUP = 8
N_EXPERTS = N_GROUPS * EXPERTS_PER_GROUP
TOP_K = 2
D_FF = 512
MOE_BLOCK = 256
ALPHA = 2.0 ** 0.25
EPS = 1e-5
NEG = -1e30
LOG2E = float(np.log2(np.e))
LANES = 128
SUBLANES = 8
PACKED = D_MODEL // 2

PROJ_ROWS = 512
OUT_PROJ_ROWS = 1024
OUT_PROJ_CHUNK = 1024
ROUTE_ROWS = 512
COMBINE_ROWS = 1024
COMBINE_PARTS = 4
X_GROUP = 4
X_AHEAD = 4
X_RING = X_AHEAD + X_GROUP
Y_RING = 2 * X_GROUP
VMEM_LIMIT = 48 * 1024 * 1024

SC_CORES = 2
SC_SUBCORES = 16
SC_WORKERS = SC_CORES * SC_SUBCORES
SC_WINDOW = 64


def _cparams(n_axes):
    return pltpu.CompilerParams(
        dimension_semantics=("arbitrary",) * n_axes, vmem_limit_bytes=VMEM_LIMIT)


def _in_proj_kernel(x_ref, w_ref, q_ref, kv_ref, xr_ref, yr_ref):
    proj = jnp.dot(x_ref[...].astype(BF16), w_ref[...], preferred_element_type=F32)
    o = 0
    for ref, width in ((q_ref, ATTN_WIDTH), (kv_ref, 2 * KV_WIDTH),
                       (xr_ref, LRU_WIDTH), (yr_ref, LRU_WIDTH)):
        ref[...] = proj[:, o:o + width].astype(ref.dtype)
        o += width


def _in_proj(x2d, w_bf16, rows):
    n = x2d.shape[0]
    widths = (ATTN_WIDTH, 2 * KV_WIDTH, LRU_WIDTH, LRU_WIDTH)
    return pl.pallas_call(
        _in_proj_kernel,
        grid=(n // rows,),
        in_specs=[pl.BlockSpec((rows, D_MODEL), lambda i: (i, 0)),
                  pl.BlockSpec((D_MODEL, IN_COLS), lambda i: (0, 0))],
        out_specs=[pl.BlockSpec((rows, w), lambda i: (i, 0)) for w in widths],
        out_shape=[jax.ShapeDtypeStruct((n, w), BF16) for w in widths],
        compiler_params=_cparams(1),
        name="in_proj",
    )(x2d, w_bf16)


def _attn_kernel(sinks_ref, q_ref, kv_ref, kvm_ref, bias_ref, g_ref, o_ref,
                 klo, khi, vlo, vhi):
    nbx = q_ref.shape[0] // BLOCK
    lo_lanes = lax.broadcasted_iota(I32, (BLOCK // 2, LANES), 1) < HEAD_DIM

    def layout_block(n, blk):
        rows = pl.ds(pl.multiple_of(n * BLOCK, BLOCK), BLOCK)
        as_bf16 = lambda words: pltpu.bitcast(words, BF16)
        for src, dst_lo, dst_hi in ((blk[:, :KV_WIDTH], klo, khi), (blk[:, KV_WIDTH:], vlo, vhi)):
            w = pltpu.bitcast(src, U32)
            r = pltpu.roll(w, HEAD_DIM, axis=1)
            zero = jnp.zeros_like(w)
            dst_lo[0, rows, :] = as_bf16(jnp.where(lo_lanes, w, zero))
            dst_hi[0, rows, :] = as_bf16(jnp.where(lo_lanes, zero, r))
            dst_lo[1, rows, :] = as_bf16(jnp.where(lo_lanes, r, zero))
            dst_hi[1, rows, :] = as_bf16(jnp.where(lo_lanes, zero, w))

    layout_block(0, kvm_ref[...])

    def layout_body(n, carry):
        layout_block(n + 1, kv_ref[pl.ds(pl.multiple_of(n * BLOCK, BLOCK), BLOCK), :])
        return carry

    lax.fori_loop(0, nbx, layout_body, 0, unroll=4)

    ones_lo = jnp.where(lax.broadcasted_iota(I32, (2 * BLOCK, LANES), 1) < HEAD_DIM,
                        1.0, 0.0).astype(BF16)
    ones_hi = (1.0 - ones_lo.astype(F32)).astype(BF16)
    top_rows = lax.broadcasted_iota(I32, (2 * BLOCK, 1), 0) < BLOCK
    lo_half = lax.broadcasted_iota(I32, (2 * BLOCK, LANES), 1) < HEAD_DIM

    def block(i, carry):
        q_rows = pl.ds(pl.multiple_of(i * BLOCK, BLOCK), BLOCK)
        win = pl.ds(pl.multiple_of(i * BLOCK, BLOCK), 2 * BLOCK)
        q = q_ref[q_rows, :]
        first = jnp.minimum(i, 1)
        outs = []
        for j in range(N_KV_HEADS):
            q2 = jnp.concatenate([q[:, (2 * j) * LANES:(2 * j + 1) * LANES],
                                  q[:, (2 * j + 1) * LANES:(2 * j + 2) * LANES]], axis=0)
            kc = jnp.concatenate([klo[j, win, :], khi[j, win, :]], axis=0)
            s = lax.dot_general(q2, kc, (((1,), (1,)), ((), ())), preferred_element_type=F32)
            s = s + bias_ref[first, j]
            ps, es = [], []
            for c in range(2):
                sink = jnp.where(top_rows, sinks_ref[4 * j + c], sinks_ref[4 * j + 2 + c])
                sc = s[:, c * 2 * BLOCK:(c + 1) * 2 * BLOCK]
                m = jnp.maximum(jnp.max(sc, axis=1, keepdims=True), sink)
                ps.append(jnp.exp2(sc - m).astype(BF16))
                es.append(jnp.exp2(sink - m))
            v_lo = jnp.concatenate([vlo[j, win, :], ones_lo], axis=1)
            v_hi = jnp.concatenate([vhi[j, win, :], ones_hi], axis=1)
            r = (jnp.dot(ps[0], v_lo, preferred_element_type=F32)
                 + jnp.dot(ps[1], v_hi, preferred_element_type=F32))
            den = r[:, LANES:] + jnp.where(lo_half, es[0], es[1])
            o2 = r[:, :LANES] * (1.0 / den)
            outs += [o2[:BLOCK], o2[BLOCK:]]
        out = jnp.concatenate(outs, axis=1)
        ms = jnp.mean(out * out, axis=1, keepdims=True)
        o_ref[q_rows, :] = (out * lax.rsqrt(ms + EPS) * g_ref[...]).astype(o_ref.dtype)
        return carry

    lax.fori_loop(0, nbx, block, 0, unroll=8)


def _attention(sinks, q, kv, kvm, bias, g_attn, bsz, nbx):
    seq = nbx * BLOCK
    const2 = lambda b: (0, 0)
    kv_scratch = pltpu.VMEM((N_KV_HEADS, seq + BLOCK, LANES), BF16)
    return pl.pallas_call(
        _attn_kernel,
        grid=(bsz,),
        in_specs=[pl.BlockSpec(memory_space=pltpu.SMEM),
                  pl.BlockSpec((None, seq, ATTN_WIDTH), lambda b: (b, 0, 0)),
                  pl.BlockSpec((None, seq, 2 * KV_WIDTH), lambda b: (b, 0, 0)),
                  pl.BlockSpec((BLOCK, 2 * KV_WIDTH), const2),
                  pl.BlockSpec((2, N_KV_HEADS, 2 * BLOCK, 4 * BLOCK), lambda b: (0, 0, 0, 0)),
                  pl.BlockSpec((1, ATTN_WIDTH), const2)],
        out_specs=pl.BlockSpec((None, seq, ATTN_WIDTH), lambda b: (b, 0, 0)),
        out_shape=jax.ShapeDtypeStruct((bsz, seq, ATTN_WIDTH), BF16),
        scratch_shapes=[kv_scratch, kv_scratch, kv_scratch, kv_scratch],
        compiler_params=_cparams(1),
        name="attention",
    )(sinks, q, kv, kvm, bias, g_attn)


def _sigmoid(v):
    return 0.5 * jnp.tanh(0.5 * v) + 0.5


def _gelu_tanh(y):
    c = float(np.sqrt(2.0 / np.pi))
    half = 0.5 * y
    return half + half * jnp.tanh(y * (c + (c * 0.044715) * (y * y)))


LRU_CHUNK = 44
LRU_SEG = SUBLANES * LRU_CHUNK
LRU_SLABS = LRU_WIDTH // LANES


def _lru_kernel(xr_ref, yr_ref, xrm_ref, yrm_ref, cw_ref, cb_ref, wa_ref, wx_ref, ba_ref,
                bx_ref, lam_ref, g_ref, o_ref, x_st, y_st, o_st, s_st, xtail, hcar):
    seq = xr_ref.shape[0]
    n_seg = (seq + BLOCK) // LRU_SEG
    xtail[...] = jnp.zeros_like(xtail)
    hcar[...] = jnp.zeros_like(hcar)
    lam = lam_ref[...]
    softplus_neg = jnp.maximum(-lam, 0.0) + jnp.log(1.0 + jnp.exp(-jnp.abs(lam)))
    sub = lax.broadcasted_iota(jnp.int32, (SUBLANES, LANES), 0)

    def strided(j):
        return pl.ds(j, SUBLANES, stride=LRU_CHUNK)

    def piece(v, j):
        return v[j * SUBLANES:(j + 1) * SUBLANES, :]

    def segment(k, first):
        if first:
            head = LRU_SEG - BLOCK
            x_nat = jnp.concatenate([xrm_ref[...], xr_ref[0:head, :]], axis=0).astype(F32)
            y_nat = jnp.concatenate([yrm_ref[...], yr_ref[0:head, :]], axis=0).astype(F32)
        else:
            rows = pl.ds(pl.multiple_of(k * LRU_SEG - BLOCK, 2 * SUBLANES), LRU_SEG)
            x_nat = xr_ref[rows, :].astype(F32)
            y_nat = yr_ref[rows, :].astype(F32)
        for c in range(LRU_SLABS):
            x_st[c] = x_nat[:, c * LANES:(c + 1) * LANES]
            y_st[c] = y_nat[:, c * LANES:(c + 1) * LANES]
        first_row = k * LRU_SEG + LRU_CHUNK * sub
        sumsq = [jnp.zeros((SUBLANES, LANES), F32) for _ in range(LRU_CHUNK)]

        for c in range(LRU_SLABS):
            lanes = slice(c * LANES, (c + 1) * LANES)
            x = [x_st[c, strided(j), :] for j in range(LRU_CHUNK)]
            before = []
            for d in range(1, CONV_W):
                from_prev_chunk = pltpu.roll(x[LRU_CHUNK - d], 1, axis=0)
                before.append(jnp.where(sub == 0, xtail[d - 1:d, lanes], from_prev_chunk))
            for d in range(1, CONV_W):
                xtail[d - 1:d, lanes] = x[LRU_CHUNK - d][SUBLANES - 1:SUBLANES, :]

            def x_at(j):
                return x[j] if j >= 0 else before[-j - 1]

            taps = [cw_ref[t:t + 1, lanes] for t in range(CONV_W)]
            bias = cb_ref[:, lanes]
            xc = jnp.concatenate(
                [bias + sum(taps[t] * x_at(j - (CONV_W - 1) + t) for t in range(CONV_W))
                 for j in range(LRU_CHUNK)], axis=0)
            xcb = xc.astype(BF16)
            tr = jnp.tanh(jnp.dot(xcb, wa_ref[c], preferred_element_type=F32) + ba_ref[:, lanes])
            ti = jnp.tanh(jnp.dot(xcb, wx_ref[c], preferred_element_type=F32) + bx_ref[:, lanes])
            log_a_half = (-0.5 * LRU_C) * softplus_neg[:, lanes]
            a = jnp.exp(log_a_half * tr + log_a_half)
            half_xc = 0.5 * xc
            gated_x = half_xc * ti + half_xc
            z = 1.0 - a * a
            u = jnp.where(z > 0.0, z * lax.rsqrt(z), 0.0) * gated_x

            h = jnp.zeros((SUBLANES, LANES), F32)
            p = jnp.ones((SUBLANES, LANES), F32)
            hs, ps = [], []
            for j in range(LRU_CHUNK):
                aj = piece(a, j)
                uj = piece(u, j)
                if first:
                    uj = jnp.where(first_row + j >= PAD_FRONT, uj, 0.0)
                h = aj * h + uj
                p = aj * p
                hs.append(h)
                ps.append(p)
            entry = [hcar[:, lanes]]
            for s in range(SUBLANES):
                entry.append(h[s:s + 1, :] + p[s:s + 1, :] * entry[s])
            hcar[:, lanes] = entry[SUBLANES]
            entry_rows = jnp.concatenate(entry[:SUBLANES], axis=0)

            for j in range(LRU_CHUNK):
                state = hs[j] + ps[j] * entry_rows
                out = state * _gelu_tanh(y_st[c, strided(j), :])
                sumsq[j] = sumsq[j] + out * out
                o_st[c, strided(j), :] = out

        for j in range(LRU_CHUNK):
            ms = jnp.sum(sumsq[j], axis=1, keepdims=True) * (1.0 / LRU_WIDTH)
            s_st[strided(j), :] = jnp.broadcast_to(lax.rsqrt(ms + EPS), (SUBLANES, LANES))
        scale = s_st[...]
        for c in range(LRU_SLABS):
            lanes = slice(c * LANES, (c + 1) * LANES)
            normed = (o_st[c] * scale * g_ref[:, lanes]).astype(o_ref.dtype)
            if first:
                o_ref[0:LRU_SEG - BLOCK, lanes] = normed[BLOCK:, :]
            else:
                o_ref[rows, lanes] = normed

    assert BLOCK <= LRU_SEG
    segment(0, True)

    def later_segment(k, carry):
        segment(k, False)
        return carry

    lax.fori_loop(1, n_seg, later_segment, 0)


def _rglru(xr, yr, xrm, yrm, cw, cb, wa, wx, ba, bx, lam, g_lru, bsz, nbx):
    seq = nbx * BLOCK
    assert (seq + BLOCK) % LRU_SEG == 0
    main = pl.BlockSpec((None, seq, LRU_WIDTH), lambda b: (b, 0, 0))
    const2 = lambda b: (0, 0)
    row_spec = pl.BlockSpec((1, LRU_WIDTH), const2)
    gate_spec = pl.BlockSpec((LRU_SLABS, LANES, LANES), lambda b: (0, 0, 0))
    slabs = pltpu.VMEM((LRU_SLABS, LRU_SEG, LANES), F32)
    return pl.pallas_call(
        _lru_kernel,
        grid=(bsz,),
        in_specs=[main, main,
                  pl.BlockSpec((BLOCK, LRU_WIDTH), const2),
                  pl.BlockSpec((BLOCK, LRU_WIDTH), const2),
                  pl.BlockSpec((CONV_W, LRU_WIDTH), const2),
                  row_spec, gate_spec, gate_spec,
                  row_spec, row_spec, row_spec, row_spec],
        out_specs=main,
        out_shape=jax.ShapeDtypeStruct((bsz, seq, LRU_WIDTH), BF16),
        scratch_shapes=[slabs, slabs, slabs,
                        pltpu.VMEM((LRU_SEG, LANES), F32),
                        pltpu.VMEM((SUBLANES, LRU_WIDTH), F32),
                        pltpu.VMEM((1, LRU_WIDTH), F32)],
        compiler_params=_cparams(1),
        name="rglru",
    )(xr, yr, xrm, yrm, cw, cb, wa, wx, ba, bx, lam, g_lru)


def _pack_rows(v):
    bits = lax.bitcast_convert_type(v.astype(BF16).astype(F32), U32)
    return (bits[:, :PACKED] >> 16) | (bits[:, PACKED:] & jnp.uint32(0xFFFF0000))


def _unpack_rows(w):
    lo = lax.bitcast_convert_type(w << 16, F32)
    hi = lax.bitcast_convert_type(w & jnp.uint32(0xFFFF0000), F32)
    return lo, hi


def _layer_norm(z, g, b):
    mu = jnp.mean(z, axis=1, keepdims=True)
    zc = z - mu
    var = jnp.mean(zc * zc, axis=1, keepdims=True)
    return zc * lax.rsqrt(var + EPS) * g + b


def _out_proj_kernel(a_ref, l_ref, x_ref, wa_ref, wl_ref, g_ref, b_ref, wrt_hi_ref, wrt_lo_ref,
                     brt_ref, h_ref, hp_ref, info_ref, cnt_ref):
    @pl.when(pl.program_id(0) == 0)
    def _():
        cnt_ref[...] = jnp.zeros_like(cnt_ref)

    for c in range(h_ref.shape[0] // OUT_PROJ_CHUNK):
        rows = slice(c * OUT_PROJ_CHUNK, (c + 1) * OUT_PROJ_CHUNK)
        _out_proj_rows(a_ref[rows, :], l_ref[rows, :], x_ref[rows, :], wa_ref, wl_ref, g_ref,
                       b_ref, wrt_hi_ref, wrt_lo_ref, brt_ref,
                       h_ref.at[rows, :], hp_ref.at[rows, :], info_ref.at[:, rows], cnt_ref)


def _out_proj_rows(a, l, x, wa_ref, wl_ref, g_ref, b_ref, wrt_hi_ref, wrt_lo_ref, brt_ref,
                   h_ref, hp_ref, info_ref, cnt_ref):
    mix = jnp.dot(a, wa_ref[...], preferred_element_type=F32)
    mix = mix + jnp.dot(l, wl_ref[...], preferred_element_type=F32)
    h = _layer_norm(ALPHA * x + mix, g_ref[...], b_ref[...])
    h_ref[...] = h
    hp_ref[...] = _pack_rows(h)

    h_hi = h.astype(BF16)
    h_lo = (h - h_hi.astype(F32)).astype(BF16)
    nt = (((1,), (1,)), ((), ()))
    w_both = jnp.concatenate([wrt_hi_ref[...], wrt_lo_ref[...]], axis=0)
    both = lax.dot_general(w_both, h_hi, nt, preferred_element_type=F32)
    lg = (both[:ROUTER_ROWS] + both[ROUTER_ROWS:]
          + lax.dot_general(wrt_hi_ref[...], h_lo, nt, preferred_element_type=F32)) + brt_ref[...]
    tile_shape = (SUBLANES, h.shape[0])
    sub = lax.broadcasted_iota(I32, tile_shape, 0)
    ninf = -jnp.inf
    t0 = lg[0:SUBLANES]
    gl = jnp.where(sub < N_GROUPS, t0, ninf)
    gmax = jnp.max(gl, axis=0, keepdims=True)
    g_idx = jnp.min(jnp.where(gl == gmax, sub, SUBLANES), axis=0, keepdims=True)
    g_w = 1.0 / jnp.sum(jnp.where(sub < N_GROUPS, jnp.exp(t0 - gmax), 0.0),
                        axis=0, keepdims=True)
    el = lg[SUBLANES:2 * SUBLANES]
    for g in range(1, N_GROUPS):
        el = jnp.where(g_idx == g, lg[(g + 1) * SUBLANES:(g + 2) * SUBLANES], el)
    v1 = jnp.max(el, axis=0, keepdims=True)
    i1 = jnp.min(jnp.where(el == v1, sub, SUBLANES), axis=0, keepdims=True)
    el2 = jnp.where(sub == i1, ninf, el)
    v2 = jnp.max(el2, axis=0, keepdims=True)
    i2 = jnp.min(jnp.where(el2 == v2, sub, SUBLANES), axis=0, keepdims=True)
    t = jnp.exp(v2 - v1)
    w1 = 1.0 / (1.0 + t)
    w2 = t * w1
    e_base = g_idx * EXPERTS_PER_GROUP
    e1 = e_base + i1
    e2 = e_base + i2
    info_ref[...] = jnp.where(sub == 0, e1.astype(F32),
                              jnp.where(sub == 1, e2.astype(F32),
                                        jnp.where(sub == 2, g_w * w1,
                                                  jnp.where(sub == 3, g_w * w2, 0.0))))
    expert = lax.broadcasted_iota(I32, (N_EXPERTS, h.shape[0]), 0)
    chosen = (expert == e1).astype(F32) + (expert == e2).astype(F32)
    cnt_ref[...] += jnp.sum(chosen, axis=1, keepdims=True)


ROUTER_ROWS = (N_GROUPS + 1) * SUBLANES


def _out_proj(attn_n, lru_n, x2d, wo_a, wo_l, ln_g, ln_b, w_rt_hi, w_rt_lo, b_rt, rows):
    n = x2d.shape[0]
    const = lambda i: (0, 0)
    tile = lambda w: pl.BlockSpec((rows, w), lambda i: (i, 0))
    return pl.pallas_call(
        _out_proj_kernel,
        grid=(n // rows,),
        in_specs=[tile(ATTN_WIDTH), tile(LRU_WIDTH), tile(D_MODEL),
                  pl.BlockSpec((ATTN_WIDTH, D_MODEL), const),
                  pl.BlockSpec((LRU_WIDTH, D_MODEL), const),
                  pl.BlockSpec((1, D_MODEL), const),
                  pl.BlockSpec((1, D_MODEL), const),
                  pl.BlockSpec((ROUTER_ROWS, D_MODEL), const),
                  pl.BlockSpec((ROUTER_ROWS, D_MODEL), const),
                  pl.BlockSpec((ROUTER_ROWS, 1), const)],
        out_specs=[tile(D_MODEL), tile(PACKED),
                   pl.BlockSpec((SUBLANES, rows), lambda i: (0, i)),
                   pl.BlockSpec((N_EXPERTS, LANES), const)],
        out_shape=[jax.ShapeDtypeStruct((n, D_MODEL), F32),
                   jax.ShapeDtypeStruct((n, PACKED), U32),
                   jax.ShapeDtypeStruct((SUBLANES, n), F32),
                   jax.ShapeDtypeStruct((N_EXPERTS, LANES), F32)],
        compiler_params=_cparams(1),
        name="out_proj",
    )(attn_n, lru_n, x2d, wo_a, wo_l, ln_g, ln_b, w_rt_hi, w_rt_lo, b_rt)


def _route_kernel(info_ref, cnt_ref, tri_ref, dest_ref, carry, pstart):
    t = pl.program_id(0)
    info = info_ref[...]
    shape = (N_EXPERTS, info.shape[1])
    expert = lax.broadcasted_iota(I32, shape, 0)
    oh1 = (expert == info[0:1, :].astype(I32)).astype(F32)
    oh2 = (expert == info[1:2, :].astype(I32)).astype(F32)
    both = oh1 + oh2

    @pl.when(t == 0)
    def _():
        c = cnt_ref[...].astype(I32)
        padded = ((c + (MOE_BLOCK - 1)) // MOE_BLOCK) * MOE_BLOCK
        e = lax.broadcasted_iota(I32, (N_EXPERTS, LANES), 0)
        scan = padded
        for d in (1, 2, 4, 8, 16):
            scan = scan + jnp.where(e >= d, pltpu.roll(scan, d, axis=0), 0)
        pstart[...] = (scan - padded)[:, 0:1].astype(F32)
        carry[...] = jnp.zeros_like(carry)

    before = jnp.dot(both.astype(BF16), tri_ref[...], preferred_element_type=F32)
    row_of = before + (carry[...] + pstart[...])
    r1 = jnp.sum(oh1 * row_of, axis=0, keepdims=True)
    r2 = jnp.sum(oh2 * row_of, axis=0, keepdims=True)
    sub = lax.broadcasted_iota(I32, dest_ref.shape, 0)
    dest_ref[...] = jnp.where(sub == 0, r1, jnp.where(sub == 1, r2, 0.0)).astype(I32)
    carry[...] += jnp.sum(both, axis=1, keepdims=True)


def _route(info_t, cnt, cols):
    n = info_t.shape[1]
    tri = jnp.asarray(np.triu(np.ones((cols, cols), np.float32), 1), BF16)
    return pl.pallas_call(
        _route_kernel,
        grid=(n // cols,),
        in_specs=[pl.BlockSpec((SUBLANES, cols), lambda t: (0, t)),
                  pl.BlockSpec((N_EXPERTS, LANES), lambda t: (0, 0)),
                  pl.BlockSpec((cols, cols), lambda t: (0, 0))],
        out_specs=pl.BlockSpec((SUBLANES, cols), lambda t: (0, t)),
        out_shape=jax.ShapeDtypeStruct((SUBLANES, n), I32),
        scratch_shapes=[pltpu.VMEM((N_EXPERTS, 1), F32), pltpu.VMEM((N_EXPERTS, 1), F32)],
        compiler_params=_cparams(1),
        name="route",
    )(info_t, cnt, tri)


def _sc_mesh():
    return plsc.VectorSubcoreMesh(core_axis_name="core", subcore_axis_name="subcore")


def _sc_worker_id():
    return lax.axis_index("subcore") * SC_CORES + lax.axis_index("core")


def _sc_scatter_rows(rows, d0, d1, cap):
    n, width = rows.shape
    per_worker = n // SC_WORKERS
    n_win = per_worker // SC_WINDOW

    def body(x_hbm, d0_hbm, d1_hbm, o_hbm, i0_v, i1_v, rows_v, rsem, sem0, sem1):
        wid = _sc_worker_id()
        pltpu.sync_copy(d0_hbm.at[wid], i0_v)
        pltpu.sync_copy(d1_hbm.at[wid], i1_v)

        def read(j):
            src = x_hbm.at[pl.ds(wid * per_worker + j * SC_WINDOW, SC_WINDOW)]
            return pltpu.make_async_copy(src, rows_v.at[j % 2], rsem.at[j % 2])

        def scatters(j):
            return (pltpu.make_async_copy(rows_v.at[j % 2], o_hbm.at[i0_v.at[j]], sem0.at[j % 2]),
                    pltpu.make_async_copy(rows_v.at[j % 2], o_hbm.at[i1_v.at[j]], sem1.at[j % 2]))

        read(0).start()
        for j in range(n_win):
            if j + 1 < n_win:
                if j >= 1:
                    for cp in scatters(j - 1):
                        cp.wait()
                read(j + 1).start()
            read(j).wait()
            for cp in scatters(j):
                cp.start()
        for j in range(max(n_win - 2, 0), n_win):
            for cp in scatters(j):
                cp.wait()

    return pl.kernel(
        body,
        out_type=jax.ShapeDtypeStruct((cap, width), rows.dtype),
        mesh=_sc_mesh(),
        scratch_types=[pltpu.VMEM((n_win, SC_WINDOW), I32), pltpu.VMEM((n_win, SC_WINDOW), I32),
                       pltpu.VMEM((2, SC_WINDOW, width), rows.dtype),
                       pltpu.SemaphoreType.DMA((2,)), pltpu.SemaphoreType.DMA((2,)),
                       pltpu.SemaphoreType.DMA((2,))],
        name="dispatch",
    )(rows, d0, d1)


def _sc_gather_rows(table, idx):
    width = table.shape[1]
    n_win = idx.shape[1]
    per_worker = n_win * SC_WINDOW

    def body(y_hbm, i_hbm, o_hbm, i_v, rows_v, gsem, wsem):
        wid = _sc_worker_id()
        pltpu.sync_copy(i_hbm.at[wid], i_v)

        def gather(j):
            return pltpu.make_async_copy(y_hbm.at[i_v.at[j]], rows_v.at[j % 2], gsem.at[j % 2])

        def write(j):
            dst = o_hbm.at[pl.ds(wid * per_worker + j * SC_WINDOW, SC_WINDOW)]
            return pltpu.make_async_copy(rows_v.at[j % 2], dst, wsem.at[j % 2])

        gather(0).start()
        for j in range(n_win):
            if j + 1 < n_win:
                if j >= 1:
                    write(j - 1).wait()
                gather(j + 1).start()
            gather(j).wait()
            write(j).start()
        for j in range(max(n_win - 2, 0), n_win):
            write(j).wait()

    return pl.kernel(
        body,
        out_type=jax.ShapeDtypeStruct((SC_WORKERS * per_worker, width), table.dtype),
        mesh=_sc_mesh(),
        scratch_types=[pltpu.VMEM((n_win, SC_WINDOW), I32),
                       pltpu.VMEM((2, SC_WINDOW, width), table.dtype),
                       pltpu.SemaphoreType.DMA((2,)), pltpu.SemaphoreType.DMA((2,))],
        name="collect",
    )(table, idx)


def _expert_kernel(bstart_ref, nblk_ref, nused_ref, xs_hbm, wg_ref, wu_ref, wd_ref, yb_hbm,
                   xbuf, ybuf, zbuf, xsem, ysem, zsem, wg_b, wu_b, wd_b):
    e = pl.program_id(0)
    nused = nused_ref[0]
    n_blocks = yb_hbm.shape[0] // MOE_BLOCK

    def rows(b):
        return pl.ds(pl.multiple_of(b * MOE_BLOCK, MOE_BLOCK), MOE_BLOCK)

    def x_copy(b):
        slot = b % X_RING
        return pltpu.make_async_copy(xs_hbm.at[rows(b)], xbuf.at[slot], xsem.at[slot])

    def y_copy(b):
        slot = b % Y_RING
        return pltpu.make_async_copy(ybuf.at[slot], yb_hbm.at[rows(b)], ysem.at[slot])

    @pl.when(e == 0)
    def _():
        for b in range(X_AHEAD):
            @pl.when(b < nused)
            def _():
                x_copy(b).start()

    wg_b[...] = wg_ref[...].astype(BF16)
    wu_b[...] = wu_ref[...].astype(BF16)
    wd_b[...] = wd_ref[...].astype(BF16)

    def run_blocks(b, count):
        for k in range(count):
            x_copy(b + k).wait()
        for k in range(count):
            nxt = b + X_AHEAD + k

            @pl.when(nxt < nused)
            def _():
                x_copy(nxt).start()

            @pl.when(b + k >= Y_RING)
            def _():
                y_copy(b + k - Y_RING).wait()

        words = jnp.concatenate([xbuf[(b + k) % X_RING] for k in range(count)], axis=0)
        lo, hi = _unpack_rows(words)
        lo = lo.astype(BF16)
        hi = hi.astype(BF16)
        g = (jnp.dot(lo, wg_b[0:PACKED, :], preferred_element_type=F32)
             + jnp.dot(hi, wg_b[PACKED:, :], preferred_element_type=F32))
        u = (jnp.dot(lo, wu_b[0:PACKED, :], preferred_element_type=F32)
             + jnp.dot(hi, wu_b[PACKED:, :], preferred_element_type=F32))
        mid = (g * _sigmoid(g) * u).astype(BF16)
        y = _pack_rows(jnp.dot(mid, wd_b[...], preferred_element_type=F32))
        for k in range(count):
            ybuf[(b + k) % Y_RING] = y[k * MOE_BLOCK:(k + 1) * MOE_BLOCK]
            y_copy(b + k).start()

    b0 = bstart_ref[e]
    nb = nblk_ref[e]

    def group(i, carry):
        run_blocks(b0 + X_GROUP * i, X_GROUP)
        return carry

    lax.fori_loop(0, nb // X_GROUP, group, 0)
    done = nb - nb % X_GROUP
    size = X_GROUP // 2
    while size >= 1:
        @pl.when((nb // size) % 2 == 1)
        def _(size=size, done=done):
            run_blocks(b0 + done, size)

        done = done + (nb // size) % 2 * size
        size //= 2

    @pl.when(e == pl.num_programs(0) - 1)
    def _():
        for back in range(Y_RING, 0, -1):
            @pl.when(nused >= back)
            def _():
                y_copy(nused - back).wait()

        zbuf[...] = jnp.zeros_like(zbuf)

        def z_copy(b):
            return pltpu.make_async_copy(zbuf, yb_hbm.at[rows(b)], zsem.at[0])

        def z_start(b, carry):
            z_copy(b).start()
            return carry

        def z_wait(b, carry):
            z_copy(b).wait()
            return carry

        lax.fori_loop(nused, n_blocks, z_start, 0)
        lax.fori_loop(nused, n_blocks, z_wait, 0)


def _experts(bstart, nblk, nused, xs, w_gate, w_up, w_down):
    cap = xs.shape[0]
    w_idx = lambda e, bs, nb, nu: (e, 0, 0)
    grid_spec = pltpu.PrefetchScalarGridSpec(
        num_scalar_prefetch=3,
        grid=(N_EXPERTS,),
        in_specs=[pl.BlockSpec(memory_space=pl.ANY),
                  pl.BlockSpec((None, D_MODEL, D_FF), w_idx),
                  pl.BlockSpec((None, D_MODEL, D_FF), w_idx),
                  pl.BlockSpec((None, D_FF, D_MODEL), w_idx)],
        out_specs=pl.BlockSpec(memory_space=pl.ANY),
        scratch_shapes=[pltpu.VMEM((X_RING, MOE_BLOCK, PACKED), U32),
                        pltpu.VMEM((Y_RING, MOE_BLOCK, PACKED), U32),
                        pltpu.VMEM((MOE_BLOCK, PACKED), U32),
                        pltpu.SemaphoreType.DMA((X_RING,)),
                        pltpu.SemaphoreType.DMA((Y_RING,)),
                        pltpu.SemaphoreType.DMA((1,)),
                        pltpu.VMEM((D_MODEL, D_FF), BF16),
                        pltpu.VMEM((D_MODEL, D_FF), BF16),
                        pltpu.VMEM((D_FF, D_MODEL), BF16)])
    return pl.pallas_call(
        _expert_kernel,
        grid_spec=grid_spec,
        out_shape=jax.ShapeDtypeStruct((cap, PACKED), U32),
        compiler_params=_cparams(1),
        name="experts",
    )(bstart, nblk, nused, xs, w_gate, w_up, w_down)


def _combine_kernel(y0_ref, y1_ref, h_ref, info_ref, g_ref, b_ref, o_ref):
    info = info_ref[...].T
    g0 = info[:, 2:3]
    g1 = info[:, 3:4]
    lo0, hi0 = _unpack_rows(y0_ref[...])
    lo1, hi1 = _unpack_rows(y1_ref[...])
    y = jnp.concatenate([g0 * lo0 + g1 * lo1, g0 * hi0 + g1 * hi1], axis=1)
    o_ref[...] = _layer_norm(ALPHA * h_ref[...] + y, g_ref[...], b_ref[...])


def _combine(ys, h, info, ln_g, ln_b, rows, part, n_parts):
    n = h.shape[0]
    steps = n // n_parts // rows
    off = part * steps
    const = lambda i: (0, 0)
    return pl.pallas_call(
        _combine_kernel,
        grid=(steps,),
        in_specs=[pl.BlockSpec((rows, PACKED), lambda i: (i, 0)),
                  pl.BlockSpec((rows, PACKED), lambda i: (i + steps, 0)),
                  pl.BlockSpec((rows, D_MODEL), lambda i: (i + off, 0)),
                  pl.BlockSpec((SUBLANES, rows), lambda i: (0, i + off)),
                  pl.BlockSpec((1, D_MODEL), const),
                  pl.BlockSpec((1, D_MODEL), const)],
        out_specs=pl.BlockSpec((rows, D_MODEL), lambda i: (i + off, 0)),
        out_shape=jax.ShapeDtypeStruct((n, D_MODEL), F32),
        input_output_aliases={2: 0},
        compiler_params=_cparams(1),
        name="combine",
    )(ys, ys, h, info, ln_g, ln_b)


def _alibi_bias():
    qi = np.arange(BLOCK)[:, None]
    kj = np.arange(2 * BLOCK)[None, :]
    dist = qi - kj + BLOCK
    band = (dist >= 0) & (dist < BLOCK)
    slopes = np.exp2(-8.0 * np.arange(1, N_Q_HEADS + 1, dtype=np.float32) / N_Q_HEADS)
    bias = np.where(band[None], -slopes[:, None, None] * dist[None].astype(np.float32), NEG)
    bias = bias * LOG2E
    first = np.where((kj >= PAD_FRONT)[None], bias, NEG)
    out = np.empty((2, N_KV_HEADS, 2 * BLOCK, 4 * BLOCK), np.float32)
    for v, per_head in enumerate((first, bias)):
        for j in range(N_KV_HEADS):
            out[v, j] = np.block([[per_head[4 * j], per_head[4 * j + 1]],
                                  [per_head[4 * j + 2], per_head[4 * j + 3]]])
    return jnp.asarray(out, F32)


def _slab_gates(w):
    nb, c, _ = w.shape
    per = LANES // c
    w = w.reshape(nb // per, per, c, c)
    eye = jnp.eye(per, dtype=w.dtype)
    return jnp.einsum('spcd,pq->spcqd', w, eye).reshape(nb // per, LANES, LANES).astype(BF16)


def kernel(x, meta_tokens, w_in, conv_w, conv_b, lru_wa, lru_ba, lru_wx, lru_bx, lru_lambda,
           attn_sinks, g_attn, g_lru, w_out, ln1_g, ln1_b, w_group, b_group, w_router,
           b_router, w_gate, w_up, w_down, ln2_g, ln2_b):
    bsz, seq, d = x.shape
    nbx = seq // BLOCK
    n_tok = bsz * seq
    x2d = x.reshape(n_tok, d)
    row = lambda v: v.reshape(1, -1).astype(F32)

    q_scale = jnp.concatenate([jnp.full((ATTN_WIDTH,), LOG2E * HEAD_DIM ** -0.5, F32),
                               jnp.ones((IN_COLS - ATTN_WIDTH,), F32)])
    w_in_b = (w_in[0] * q_scale).astype(BF16)
    meta_blk = jnp.concatenate([jnp.zeros((PAD_FRONT, d), F32), meta_tokens.astype(F32)], axis=0)
    q, kv, xr, yr = _in_proj(x2d, w_in_b, PROJ_ROWS)
    qm, kvm, xrm, yrm = _in_proj(meta_blk, w_in_b, BLOCK)
    shp = lambda a: a.reshape(bsz, seq, a.shape[-1])

    attn_n = _attention(attn_sinks[0].astype(F32) * LOG2E, shp(q), shp(kv), kvm, _alibi_bias(),
                        row(g_attn[0]), bsz, nbx)
    lru_n = _rglru(shp(xr), shp(yr), xrm, yrm, conv_w[0].astype(F32), row(conv_b[0]),
                   _slab_gates(0.5 * lru_wa[0]), _slab_gates(0.5 * lru_wx[0]),
                   0.5 * row(lru_ba[0]), 0.5 * row(lru_bx[0]), row(lru_lambda[0]),
                   row(g_lru[0]), bsz, nbx)

    w_out_b = w_out[0].astype(BF16)
    gpad = SUBLANES - N_GROUPS
    w_rt = jnp.concatenate(
        [w_group[0].T, jnp.zeros((gpad, d), F32),
         jnp.transpose(w_router[0], (0, 2, 1)).reshape(N_EXPERTS, d)], axis=0).astype(F32)
    w_rt_hi = w_rt.astype(BF16)
    w_rt_lo = (w_rt - w_rt_hi.astype(F32)).astype(BF16)
    b_rt = jnp.concatenate([b_group[0], jnp.zeros((gpad,), F32),
                            b_router[0].reshape(-1)]).astype(F32).reshape(ROUTER_ROWS, 1)
    h1, hp, info, cnt = _out_proj(
        attn_n.reshape(n_tok, ATTN_WIDTH), lru_n.reshape(n_tok, LRU_WIDTH), x2d,
        w_out_b[:ATTN_WIDTH], w_out_b[ATTN_WIDTH:], row(ln1_g[0]), row(ln1_b[0]),
        w_rt_hi, w_rt_lo, b_rt, OUT_PROJ_ROWS)

    dest = _route(info, cnt, ROUTE_ROWS)
    n_slots = n_tok * TOP_K
    n_blocks = n_slots // MOE_BLOCK + N_EXPERTS
    cap = n_blocks * MOE_BLOCK
    nblk = (cnt[:, 0].astype(I32) + MOE_BLOCK - 1) // MOE_BLOCK
    bends = jnp.cumsum(nblk)
    bstart = (bends - nblk).astype(I32)
    nused = bends[-1:].astype(I32)
    windows = lambda v: v.reshape(SC_WORKERS, -1, SC_WINDOW)
    d0 = dest[0]
    d1 = dest[1]

    xs = _sc_scatter_rows(hp, windows(d0), windows(d1), cap)
    yb = _experts(bstart, nblk.astype(I32), nused, xs, w_gate[0], w_up[0], w_down[0])
    out = h1
    part_len = n_tok // COMBINE_PARTS
    for part in range(COMBINE_PARTS):
        tok = slice(part * part_len, (part + 1) * part_len)
        ys = _sc_gather_rows(yb, windows(jnp.concatenate([d0[tok], d1[tok]])))
        out = _combine(ys, out, info, row(ln2_g[0]), row(ln2_b[0]), COMBINE_ROWS,
                       part, COMBINE_PARTS)
    return out.reshape(bsz, seq, d)
```

```python
import jax
import jax.numpy as jnp
import numpy as np
from jax import lax
from jax.experimental import pallas as pl
from jax.experimental.pallas import tpu as pltpu
from jax.experimental.pallas import tpu_sc as plsc

F32 = jnp.float32
BF16 = jnp.bfloat16
U32 = jnp.uint32
I32 = jnp.int32

D_MODEL = 1024
N_META = 16
BLOCK = 128
PAD_FRONT = BLOCK - N_META
HEAD_DIM = 64
ATTN_WIDTH = 512
LRU_WIDTH = 512
N_Q_HEADS = 8
N_KV_HEADS = 2
KV_WIDTH = N_KV_HEADS * HEAD_DIM
LRU_BLOCKS = 8
CONV_W = 4
LRU_C = 8.0
IN_COLS = ATTN_WIDTH + 2 * KV_WIDTH + 2 * LRU_WIDTH
N_GROUPS = 4
EXPERTS_PER_GROUP = 8
N_EXPERTS = N_GROUPS * EXPERTS_PER_GROUP
TOP_K = 2
D_FF = 512
MOE_BLOCK = 256
ALPHA = 2.0 ** 0.25
EPS = 1e-5
NEG = -1e30
LOG2E = float(np.log2(np.e))
LANES = 128
SUBLANES = 8
PACKED = D_MODEL // 2

PROJ_ROWS = 1024
OUT_PROJ_ROWS = 1024
OUT_PROJ_CHUNK = 1024
ROUTE_ROWS = 512
COMBINE_ROWS = 1024
COMBINE_PARTS = 4
X_GROUP = 4
X_AHEAD = 4
X_RING = X_AHEAD + X_GROUP
Y_RING = 2 * X_GROUP
VMEM_LIMIT = 48 * 1024 * 1024

SC_CORES = 2
SC_SUBCORES = 16
SC_WORKERS = SC_CORES * SC_SUBCORES
SC_WINDOW = 64


def _cparams(n_axes):
    return pltpu.CompilerParams(
        dimension_semantics=("arbitrary",) * n_axes, vmem_limit_bytes=VMEM_LIMIT)


def _in_proj_kernel(x_ref, w_ref, q_ref, kv_ref, xr_ref, yr_ref):
    proj = jnp.dot(x_ref[...].astype(BF16), w_ref[...], preferred_element_type=F32)
    o = 0
    for ref, width in ((q_ref, ATTN_WIDTH), (kv_ref, 2 * KV_WIDTH),
                       (xr_ref, LRU_WIDTH), (yr_ref, LRU_WIDTH)):
        ref[...] = proj[:, o:o + width].astype(ref.dtype)
        o += width


def _in_proj(x2d, w_bf16, rows):
    n = x2d.shape[0]
    widths = (ATTN_WIDTH, 2 * KV_WIDTH, LRU_WIDTH, LRU_WIDTH)
    return pl.pallas_call(
        _in_proj_kernel,
        grid=(n // rows,),
        in_specs=[pl.BlockSpec((rows, D_MODEL), lambda i: (i, 0)),
                  pl.BlockSpec((D_MODEL, IN_COLS), lambda i: (0, 0))],
        out_specs=[pl.BlockSpec((rows, w), lambda i: (i, 0)) for w in widths],
        out_shape=[jax.ShapeDtypeStruct((n, w), BF16) for w in widths],
        compiler_params=_cparams(1),
        name="in_proj",
    )(x2d, w_bf16)


def _attn_kernel(sinks_ref, q_ref, kv_ref, kvm_ref, bias_ref, g_ref, o_ref,
                 klo, khi, vlo, vhi):
    nbx = q_ref.shape[0] // BLOCK
    lo_lanes = lax.broadcasted_iota(I32, (BLOCK // 2, LANES), 1) < HEAD_DIM

    def layout_block(n, blk):
        rows = pl.ds(pl.multiple_of(n * BLOCK, BLOCK), BLOCK)
        as_bf16 = lambda words: pltpu.bitcast(words, BF16)
        for src, dst_lo, dst_hi in ((blk[:, :KV_WIDTH], klo, khi), (blk[:, KV_WIDTH:], vlo, vhi)):
            w = pltpu.bitcast(src, U32)
            r = pltpu.roll(w, HEAD_DIM, axis=1)
            zero = jnp.zeros_like(w)
            dst_lo[0, rows, :] = as_bf16(jnp.where(lo_lanes, w, zero))
            dst_hi[0, rows, :] = as_bf16(jnp.where(lo_lanes, zero, r))
            dst_lo[1, rows, :] = as_bf16(jnp.where(lo_lanes, r, zero))
            dst_hi[1, rows, :] = as_bf16(jnp.where(lo_lanes, zero, w))

    layout_block(0, kvm_ref[...])

    def layout_body(n, carry):
        layout_block(n + 1, kv_ref[pl.ds(pl.multiple_of(n * BLOCK, BLOCK), BLOCK), :])
        return carry

    lax.fori_loop(0, nbx, layout_body, 0, unroll=4)

    ones_lo = jnp.where(lax.broadcasted_iota(I32, (2 * BLOCK, LANES), 1) < HEAD_DIM,
                        1.0, 0.0).astype(BF16)
    ones_hi = (1.0 - ones_lo.astype(F32)).astype(BF16)
    top_rows = lax.broadcasted_iota(I32, (2 * BLOCK, 1), 0) < BLOCK
    lo_half = lax.broadcasted_iota(I32, (2 * BLOCK, LANES), 1) < HEAD_DIM

    def block(i, carry):
        q_rows = pl.ds(pl.multiple_of(i * BLOCK, BLOCK), BLOCK)
        win = pl.ds(pl.multiple_of(i * BLOCK, BLOCK), 2 * BLOCK)
        q = q_ref[q_rows, :]
        first = jnp.minimum(i, 1)
        outs = []
        for j in range(N_KV_HEADS):
            q2 = jnp.concatenate([q[:, (2 * j) * LANES:(2 * j + 1) * LANES],
                                  q[:, (2 * j + 1) * LANES:(2 * j + 2) * LANES]], axis=0)
            kc = jnp.concatenate([klo[j, win, :], khi[j, win, :]], axis=0)
            s = lax.dot_general(q2, kc, (((1,), (1,)), ((), ())), preferred_element_type=F32)
            s = s + bias_ref[first, j]
            ps, es = [], []
            for c in range(2):
                sink = jnp.where(top_rows, sinks_ref[4 * j + c], sinks_ref[4 * j + 2 + c])
                sc = s[:, c * 2 * BLOCK:(c + 1) * 2 * BLOCK]
                m = jnp.maximum(jnp.max(sc, axis=1, keepdims=True), sink)
                ps.append(jnp.exp2(sc - m).astype(BF16))
                es.append(jnp.exp2(sink - m))
            v_lo = jnp.concatenate([vlo[j, win, :], ones_lo], axis=1)
            v_hi = jnp.concatenate([vhi[j, win, :], ones_hi], axis=1)
            r = jnp.dot(jnp.concatenate(ps, axis=1), jnp.concatenate([v_lo, v_hi], axis=0),
                        preferred_element_type=F32)
            den = r[:, LANES:] + jnp.where(lo_half, es[0], es[1])
            o2 = r[:, :LANES] * (1.0 / den)
            outs += [o2[:BLOCK], o2[BLOCK:]]
        out = jnp.concatenate(outs, axis=1)
        ms = jnp.mean(out * out, axis=1, keepdims=True)
        o_ref[q_rows, :] = (out * lax.rsqrt(ms + EPS) * g_ref[...]).astype(o_ref.dtype)
        return carry

    lax.fori_loop(0, nbx, block, 0, unroll=8)


def _attention(sinks, q, kv, kvm, bias, g_attn, bsz, nbx):
    seq = nbx * BLOCK
    const2 = lambda b: (0, 0)
    kv_scratch = pltpu.VMEM((N_KV_HEADS, seq + BLOCK, LANES), BF16)
    return pl.pallas_call(
        _attn_kernel,
        grid=(bsz,),
        in_specs=[pl.BlockSpec(memory_space=pltpu.SMEM),
                  pl.BlockSpec((None, seq, ATTN_WIDTH), lambda b: (b, 0, 0)),
                  pl.BlockSpec((None, seq, 2 * KV_WIDTH), lambda b: (b, 0, 0)),
                  pl.BlockSpec((BLOCK, 2 * KV_WIDTH), const2),
                  pl.BlockSpec((2, N_KV_HEADS, 2 * BLOCK, 4 * BLOCK), lambda b: (0, 0, 0, 0)),
                  pl.BlockSpec((1, ATTN_WIDTH), const2)],
        out_specs=pl.BlockSpec((None, seq, ATTN_WIDTH), lambda b: (b, 0, 0)),
        out_shape=jax.ShapeDtypeStruct((bsz, seq, ATTN_WIDTH), BF16),
        scratch_shapes=[kv_scratch, kv_scratch, kv_scratch, kv_scratch],
        compiler_params=_cparams(1),
        name="attention",
    )(sinks, q, kv, kvm, bias, g_attn)


def _sigmoid(v):
    return 0.5 * jnp.tanh(0.5 * v) + 0.5


def _gelu_tanh(y):
    c = float(np.sqrt(2.0 / np.pi))
    half = 0.5 * y
    return half + half * jnp.tanh(y * (c + (c * 0.044715) * (y * y)))


LRU_CHUNK = 44
LRU_SEG = SUBLANES * LRU_CHUNK
LRU_SLABS = LRU_WIDTH // LANES


def _lru_kernel(xr_ref, yr_ref, xrm_ref, yrm_ref, cw_ref, cb_ref, wa_ref, wx_ref, ba_ref,
                bx_ref, lam_ref, g_ref, o_ref, x_st, y_st, o_st, s_st, xtail, hcar):
    seq = xr_ref.shape[0]
    n_seg = (seq + BLOCK) // LRU_SEG
    xtail[...] = jnp.zeros_like(xtail)
    hcar[...] = jnp.zeros_like(hcar)
    lam = lam_ref[...]
    softplus_neg = jnp.maximum(-lam, 0.0) + jnp.log(1.0 + jnp.exp(-jnp.abs(lam)))
    sub = lax.broadcasted_iota(jnp.int32, (SUBLANES, LANES), 0)

    def strided(j):
        return pl.ds(j, SUBLANES, stride=LRU_CHUNK)

    def piece(v, j):
        return v[j * SUBLANES:(j + 1) * SUBLANES, :]

    def segment(k, first):
        if first:
            head = LRU_SEG - BLOCK
            x_nat = jnp.concatenate([xrm_ref[...], xr_ref[0:head, :]], axis=0).astype(F32)
            y_nat = jnp.concatenate([yrm_ref[...], yr_ref[0:head, :]], axis=0).astype(F32)
        else:
            rows = pl.ds(pl.multiple_of(k * LRU_SEG - BLOCK, 2 * SUBLANES), LRU_SEG)
            x_nat = xr_ref[rows, :].astype(F32)
            y_nat = yr_ref[rows, :].astype(F32)
        for c in range(LRU_SLABS):
            x_st[c] = x_nat[:, c * LANES:(c + 1) * LANES]
            y_st[c] = y_nat[:, c * LANES:(c + 1) * LANES]
        first_row = k * LRU_SEG + LRU_CHUNK * sub
        sumsq = [jnp.zeros((SUBLANES, LANES), F32) for _ in range(LRU_CHUNK)]

        for c in range(LRU_SLABS):
            lanes = slice(c * LANES, (c + 1) * LANES)
            x = [x_st[c, strided(j), :] for j in range(LRU_CHUNK)]
            before = []
            for d in range(1, CONV_W):
                from_prev_chunk = pltpu.roll(x[LRU_CHUNK - d], 1, axis=0)
                before.append(jnp.where(sub == 0, xtail[d - 1:d, lanes], from_prev_chunk))
            for d in range(1, CONV_W):
                xtail[d - 1:d, lanes] = x[LRU_CHUNK - d][SUBLANES - 1:SUBLANES, :]

            def x_at(j):
                return x[j] if j >= 0 else before[-j - 1]

            taps = [cw_ref[t:t + 1, lanes] for t in range(CONV_W)]
            bias = cb_ref[:, lanes]
            xc = jnp.concatenate(
                [bias + sum(taps[t] * x_at(j - (CONV_W - 1) + t) for t in range(CONV_W))
                 for j in range(LRU_CHUNK)], axis=0)
            xcb = xc.astype(BF16)
            tr = jnp.tanh(jnp.dot(xcb, wa_ref[c], preferred_element_type=F32) + ba_ref[:, lanes])
            ti = jnp.tanh(jnp.dot(xcb, wx_ref[c], preferred_element_type=F32) + bx_ref[:, lanes])
            log_a_half = (-0.5 * LRU_C) * softplus_neg[:, lanes]
            a = jnp.exp(log_a_half * tr + log_a_half)
            half_xc = 0.5 * xc
            gated_x = half_xc * ti + half_xc
            z = 1.0 - a * a
            u = jnp.where(z > 0.0, z * lax.rsqrt(z), 0.0) * gated_x

            h = jnp.zeros((SUBLANES, LANES), F32)
            p = jnp.ones((SUBLANES, LANES), F32)
            hs, ps = [], []
            for j in range(LRU_CHUNK):
                aj = piece(a, j)
                uj = piece(u, j)
                if first:
                    uj = jnp.where(first_row + j >= PAD_FRONT, uj, 0.0)
                h = aj * h + uj
                p = aj * p
                hs.append(h)
                ps.append(p)
            entry = [hcar[:, lanes]]
            for s in range(SUBLANES):
                entry.append(h[s:s + 1, :] + p[s:s + 1, :] * entry[s])
            hcar[:, lanes] = entry[SUBLANES]
            entry_rows = jnp.concatenate(entry[:SUBLANES], axis=0)

            for j in range(LRU_CHUNK):
                state = hs[j] + ps[j] * entry_rows
                out = state * _gelu_tanh(y_st[c, strided(j), :])
                sumsq[j] = sumsq[j] + out * out
                o_st[c, strided(j), :] = out

        for j in range(LRU_CHUNK):
            ms = jnp.sum(sumsq[j], axis=1, keepdims=True) * (1.0 / LRU_WIDTH)
            s_st[strided(j), :] = jnp.broadcast_to(lax.rsqrt(ms + EPS), (SUBLANES, LANES))
        scale = s_st[...]
        for c in range(LRU_SLABS):
            lanes = slice(c * LANES, (c + 1) * LANES)
            normed = (o_st[c] * scale * g_ref[:, lanes]).astype(o_ref.dtype)
            if first:
                o_ref[0:LRU_SEG - BLOCK, lanes] = normed[BLOCK:, :]
            else:
                o_ref[rows, lanes] = normed

    assert BLOCK <= LRU_SEG
    segment(0, True)

    def later_segment(k, carry):
        segment(k, False)
        return carry

    lax.fori_loop(1, n_seg, later_segment, 0)


def _rglru(xr, yr, xrm, yrm, cw, cb, wa, wx, ba, bx, lam, g_lru, bsz, nbx):
    seq = nbx * BLOCK
    assert (seq + BLOCK) % LRU_SEG == 0
    main = pl.BlockSpec((None, seq, LRU_WIDTH), lambda b: (b, 0, 0))
    const2 = lambda b: (0, 0)
    row_spec = pl.BlockSpec((1, LRU_WIDTH), const2)
    gate_spec = pl.BlockSpec((LRU_SLABS, LANES, LANES), lambda b: (0, 0, 0))
    slabs = pltpu.VMEM((LRU_SLABS, LRU_SEG, LANES), F32)
    return pl.pallas_call(
        _lru_kernel,
        grid=(bsz,),
        in_specs=[main, main,
                  pl.BlockSpec((BLOCK, LRU_WIDTH), const2),
                  pl.BlockSpec((BLOCK, LRU_WIDTH), const2),
                  pl.BlockSpec((CONV_W, LRU_WIDTH), const2),
                  row_spec, gate_spec, gate_spec,
                  row_spec, row_spec, row_spec, row_spec],
        out_specs=main,
        out_shape=jax.ShapeDtypeStruct((bsz, seq, LRU_WIDTH), BF16),
        scratch_shapes=[slabs, slabs, slabs,
                        pltpu.VMEM((LRU_SEG, LANES), F32),
                        pltpu.VMEM((SUBLANES, LRU_WIDTH), F32),
                        pltpu.VMEM((1, LRU_WIDTH), F32)],
        compiler_params=_cparams(1),
        name="rglru",
    )(xr, yr, xrm, yrm, cw, cb, wa, wx, ba, bx, lam, g_lru)


def _pack_rows(v):
    bits = lax.bitcast_convert_type(v.astype(BF16).astype(F32), U32)
    return (bits[:, :PACKED] >> 16) | (bits[:, PACKED:] & jnp.uint32(0xFFFF0000))


def _unpack_rows(w):
    lo = lax.bitcast_convert_type(w << 16, F32)
    hi = lax.bitcast_convert_type(w & jnp.uint32(0xFFFF0000), F32)
    return lo, hi


def _layer_norm(z, g, b):
    mu = jnp.mean(z, axis=1, keepdims=True)
    zc = z - mu
    var = jnp.mean(zc * zc, axis=1, keepdims=True)
    return zc * lax.rsqrt(var + EPS) * g + b


def _out_proj_kernel(a_ref, l_ref, x_ref, wa_ref, wl_ref, g_ref, b_ref, wrt_hi_ref, wrt_lo_ref,
                     brt_ref, h_ref, hp_ref, info_ref, cnt_ref):
    @pl.when(pl.program_id(0) == 0)
    def _():
        cnt_ref[...] = jnp.zeros_like(cnt_ref)

    for c in range(h_ref.shape[0] // OUT_PROJ_CHUNK):
        rows = slice(c * OUT_PROJ_CHUNK, (c + 1) * OUT_PROJ_CHUNK)
        _out_proj_rows(a_ref[rows, :], l_ref[rows, :], x_ref[rows, :], wa_ref, wl_ref, g_ref,
                       b_ref, wrt_hi_ref, wrt_lo_ref, brt_ref,
                       h_ref.at[rows, :], hp_ref.at[rows, :], info_ref.at[:, rows], cnt_ref)


def _out_proj_rows(a, l, x, wa_ref, wl_ref, g_ref, b_ref, wrt_hi_ref, wrt_lo_ref, brt_ref,
                   h_ref, hp_ref, info_ref, cnt_ref):
    mix = jnp.dot(a, wa_ref[...], preferred_element_type=F32)
    mix = mix + jnp.dot(l, wl_ref[...], preferred_element_type=F32)
    h = _layer_norm(ALPHA * x + mix, g_ref[...], b_ref[...])
    h_ref[...] = h
    hp_ref[...] = _pack_rows(h)

    h_hi = h.astype(BF16)
    h_lo = (h - h_hi.astype(F32)).astype(BF16)
    nt = (((1,), (1,)), ((), ()))
    w_both = jnp.concatenate([wrt_hi_ref[...], wrt_lo_ref[...]], axis=0)
    both = lax.dot_general(w_both, h_hi, nt, preferred_element_type=F32)
    lg = (both[:ROUTER_ROWS] + both[ROUTER_ROWS:]
          + lax.dot_general(wrt_hi_ref[...], h_lo, nt, preferred_element_type=F32)) + brt_ref[...]
    tile_shape = (SUBLANES, h.shape[0])
    sub = lax.broadcasted_iota(I32, tile_shape, 0)
    ninf = -jnp.inf
    t0 = lg[0:SUBLANES]
    gl = jnp.where(sub < N_GROUPS, t0, ninf)
    gmax = jnp.max(gl, axis=0, keepdims=True)
    g_idx = jnp.min(jnp.where(gl == gmax, sub, SUBLANES), axis=0, keepdims=True)
    g_w = 1.0 / jnp.sum(jnp.where(sub < N_GROUPS, jnp.exp(t0 - gmax), 0.0),
                        axis=0, keepdims=True)
    el = lg[SUBLANES:2 * SUBLANES]
    for g in range(1, N_GROUPS):
        el = jnp.where(g_idx == g, lg[(g + 1) * SUBLANES:(g + 2) * SUBLANES], el)
    v1 = jnp.max(el, axis=0, keepdims=True)
    i1 = jnp.min(jnp.where(el == v1, sub, SUBLANES), axis=0, keepdims=True)
    el2 = jnp.where(sub == i1, ninf, el)
    v2 = jnp.max(el2, axis=0, keepdims=True)
    i2 = jnp.min(jnp.where(el2 == v2, sub, SUBLANES), axis=0, keepdims=True)
    t = jnp.exp(v2 - v1)
    w1 = 1.0 / (1.0 + t)
    w2 = t * w1
    e_base = g_idx * EXPERTS_PER_GROUP
    e1 = e_base + i1
    e2 = e_base + i2
    info_ref[...] = jnp.where(sub == 0, e1.astype(F32),
                              jnp.where(sub == 1, e2.astype(F32),
                                        jnp.where(sub == 2, g_w * w1,
                                                  jnp.where(sub == 3, g_w * w2, 0.0))))
    expert = lax.broadcasted_iota(I32, (N_EXPERTS, h.shape[0]), 0)
    chosen = (expert == e1).astype(F32) + (expert == e2).astype(F32)
    cnt_ref[...] += jnp.sum(chosen, axis=1, keepdims=True)


ROUTER_ROWS = (N_GROUPS + 1) * SUBLANES


def _out_proj(attn_n, lru_n, x2d, wo_a, wo_l, ln_g, ln_b, w_rt_hi, w_rt_lo, b_rt, rows):
    n = x2d.shape[0]
    const = lambda i: (0, 0)
    tile = lambda w: pl.BlockSpec((rows, w), lambda i: (i, 0))
    return pl.pallas_call(
        _out_proj_kernel,
        grid=(n // rows,),
        in_specs=[tile(ATTN_WIDTH), tile(LRU_WIDTH), tile(D_MODEL),
                  pl.BlockSpec((ATTN_WIDTH, D_MODEL), const),
                  pl.BlockSpec((LRU_WIDTH, D_MODEL), const),
                  pl.BlockSpec((1, D_MODEL), const),
                  pl.BlockSpec((1, D_MODEL), const),
                  pl.BlockSpec((ROUTER_ROWS, D_MODEL), const),
                  pl.BlockSpec((ROUTER_ROWS, D_MODEL), const),
                  pl.BlockSpec((ROUTER_ROWS, 1), const)],
        out_specs=[tile(D_MODEL), tile(PACKED),
                   pl.BlockSpec((SUBLANES, rows), lambda i: (0, i)),
                   pl.BlockSpec((N_EXPERTS, LANES), const)],
        out_shape=[jax.ShapeDtypeStruct((n, D_MODEL), F32),
                   jax.ShapeDtypeStruct((n, PACKED), U32),
                   jax.ShapeDtypeStruct((SUBLANES, n), F32),
                   jax.ShapeDtypeStruct((N_EXPERTS, LANES), F32)],
        compiler_params=_cparams(1),
        name="out_proj",
    )(attn_n, lru_n, x2d, wo_a, wo_l, ln_g, ln_b, w_rt_hi, w_rt_lo, b_rt)


def _route_kernel(info_ref, cnt_ref, tri_ref, dest_ref, carry, pstart):
    t = pl.program_id(0)
    info = info_ref[...]
    shape = (N_EXPERTS, info.shape[1])
    expert = lax.broadcasted_iota(I32, shape, 0)
    oh1 = (expert == info[0:1, :].astype(I32)).astype(F32)
    oh2 = (expert == info[1:2, :].astype(I32)).astype(F32)
    both = oh1 + oh2

    @pl.when(t == 0)
    def _():
        c = cnt_ref[...].astype(I32)
        padded = ((c + (MOE_BLOCK - 1)) // MOE_BLOCK) * MOE_BLOCK
        e = lax.broadcasted_iota(I32, (N_EXPERTS, LANES), 0)
        scan = padded
        for d in (1, 2, 4, 8, 16):
            scan = scan + jnp.where(e >= d, pltpu.roll(scan, d, axis=0), 0)
        pstart[...] = (scan - padded)[:, 0:1].astype(F32)
        carry[...] = jnp.zeros_like(carry)

    before = jnp.dot(both.astype(BF16), tri_ref[...], preferred_element_type=F32)
    row_of = before + (carry[...] + pstart[...])
    r1 = jnp.sum(oh1 * row_of, axis=0, keepdims=True)
    r2 = jnp.sum(oh2 * row_of, axis=0, keepdims=True)
    sub = lax.broadcasted_iota(I32, dest_ref.shape, 0)
    dest_ref[...] = jnp.where(sub == 0, r1, jnp.where(sub == 1, r2, 0.0)).astype(I32)
    carry[...] += jnp.sum(both, axis=1, keepdims=True)


def _route(info_t, cnt, cols):
    n = info_t.shape[1]
    tri = jnp.asarray(np.triu(np.ones((cols, cols), np.float32), 1), BF16)
    return pl.pallas_call(
        _route_kernel,
        grid=(n // cols,),
        in_specs=[pl.BlockSpec((SUBLANES, cols), lambda t: (0, t)),
                  pl.BlockSpec((N_EXPERTS, LANES), lambda t: (0, 0)),
                  pl.BlockSpec((cols, cols), lambda t: (0, 0))],
        out_specs=pl.BlockSpec((SUBLANES, cols), lambda t: (0, t)),
        out_shape=jax.ShapeDtypeStruct((SUBLANES, n), I32),
        scratch_shapes=[pltpu.VMEM((N_EXPERTS, 1), F32), pltpu.VMEM((N_EXPERTS, 1), F32)],
        compiler_params=_cparams(1),
        name="route",
    )(info_t, cnt, tri)


def _sc_mesh():
    return plsc.VectorSubcoreMesh(core_axis_name="core", subcore_axis_name="subcore")


def _sc_worker_id():
    return lax.axis_index("subcore") * SC_CORES + lax.axis_index("core")


def _sc_scatter_rows(rows, d0, d1, cap):
    n, width = rows.shape
    per_worker = n // SC_WORKERS
    n_win = per_worker // SC_WINDOW

    def body(x_hbm, d0_hbm, d1_hbm, o_hbm, i0_v, i1_v, rows_v, rsem, sem0, sem1):
        wid = _sc_worker_id()
        pltpu.sync_copy(d0_hbm.at[wid], i0_v)
        pltpu.sync_copy(d1_hbm.at[wid], i1_v)

        def read(j):
            src = x_hbm.at[pl.ds(wid * per_worker + j * SC_WINDOW, SC_WINDOW)]
            return pltpu.make_async_copy(src, rows_v.at[j % 2], rsem.at[j % 2])

        def scatters(j):
            return (pltpu.make_async_copy(rows_v.at[j % 2], o_hbm.at[i0_v.at[j]], sem0.at[j % 2]),
                    pltpu.make_async_copy(rows_v.at[j % 2], o_hbm.at[i1_v.at[j]], sem1.at[j % 2]))

        read(0).start()
        for j in range(n_win):
            if j + 1 < n_win:
                if j >= 1:
                    for cp in scatters(j - 1):
                        cp.wait()
                read(j + 1).start()
            read(j).wait()
            for cp in scatters(j):
                cp.start()
        for j in range(max(n_win - 2, 0), n_win):
            for cp in scatters(j):
                cp.wait()

    return pl.kernel(
        body,
        out_type=jax.ShapeDtypeStruct((cap, width), rows.dtype),
        mesh=_sc_mesh(),
        scratch_types=[pltpu.VMEM((n_win, SC_WINDOW), I32), pltpu.VMEM((n_win, SC_WINDOW), I32),
                       pltpu.VMEM((2, SC_WINDOW, width), rows.dtype),
                       pltpu.SemaphoreType.DMA((2,)), pltpu.SemaphoreType.DMA((2,)),
                       pltpu.SemaphoreType.DMA((2,))],
        name="dispatch",
    )(rows, d0, d1)


def _sc_gather_rows(table, idx):
    width = table.shape[1]
    n_win = idx.shape[1]
    per_worker = n_win * SC_WINDOW

    def body(y_hbm, i_hbm, o_hbm, i_v, rows_v, gsem, wsem):
        wid = _sc_worker_id()
        pltpu.sync_copy(i_hbm.at[wid], i_v)

        def gather(j):
            return pltpu.make_async_copy(y_hbm.at[i_v.at[j]], rows_v.at[j % 2], gsem.at[j % 2])

        def write(j):
            dst = o_hbm.at[pl.ds(wid * per_worker + j * SC_WINDOW, SC_WINDOW)]
            return pltpu.make_async_copy(rows_v.at[j % 2], dst, wsem.at[j % 2])

        gather(0).start()
        for j in range(n_win):
            if j + 1 < n_win:
                if j >= 1:
                    write(j - 1).wait()
                gather(j + 1).start()
            gather(j).wait()
            write(j).start()
        for j in range(max(n_win - 2, 0), n_win):
            write(j).wait()

    return pl.kernel(
        body,
        out_type=jax.ShapeDtypeStruct((SC_WORKERS * per_worker, width), table.dtype),
        mesh=_sc_mesh(),
        scratch_types=[pltpu.VMEM((n_win, SC_WINDOW), I32),
                       pltpu.VMEM((2, SC_WINDOW, width), table.dtype),
                       pltpu.SemaphoreType.DMA((2,)), pltpu.SemaphoreType.DMA((2,))],
        name="collect",
    )(table, idx)


def _expert_kernel(bstart_ref, nblk_ref, nused_ref, xs_hbm, wg_ref, wu_ref, wd_ref, yb_hbm,
                   xbuf, ybuf, zbuf, xsem, ysem, zsem, wg_b, wu_b, wd_b):
    e = pl.program_id(0)
    nused = nused_ref[0]
    n_blocks = yb_hbm.shape[0] // MOE_BLOCK

    def rows(b):
        return pl.ds(pl.multiple_of(b * MOE_BLOCK, MOE_BLOCK), MOE_BLOCK)

    def x_copy(b):
        slot = b % X_RING
        return pltpu.make_async_copy(xs_hbm.at[rows(b)], xbuf.at[slot], xsem.at[slot])

    def y_copy(b):
        slot = b % Y_RING
        return pltpu.make_async_copy(ybuf.at[slot], yb_hbm.at[rows(b)], ysem.at[slot])

    @pl.when(e == 0)
    def _():
        for b in range(X_AHEAD):
            @pl.when(b < nused)
            def _():
                x_copy(b).start()

    wg_b[...] = wg_ref[...].astype(BF16)
    wu_b[...] = wu_ref[...].astype(BF16)
    wd_b[...] = wd_ref[...].astype(BF16)

    def run_blocks(b, count):
        for k in range(count):
            x_copy(b + k).wait()
        for k in range(count):
            nxt = b + X_AHEAD + k

            @pl.when(nxt < nused)
            def _():
                x_copy(nxt).start()

            @pl.when(b + k >= Y_RING)
            def _():
                y_copy(b + k - Y_RING).wait()

        words = jnp.concatenate([xbuf[(b + k) % X_RING] for k in range(count)], axis=0)
        lo, hi = _unpack_rows(words)
        x = jnp.concatenate([lo.astype(BF16), hi.astype(BF16)], axis=1)
        g = jnp.dot(x, wg_b[...], preferred_element_type=F32)
        u = jnp.dot(x, wu_b[...], preferred_element_type=F32)
        mid = (g * _sigmoid(g) * u).astype(BF16)
        y = _pack_rows(jnp.dot(mid, wd_b[...], preferred_element_type=F32))
        for k in range(count):
            ybuf[(b + k) % Y_RING] = y[k * MOE_BLOCK:(k + 1) * MOE_BLOCK]
            y_copy(b + k).start()

    b0 = bstart_ref[e]
    nb = nblk_ref[e]

    def group(i, carry):
        run_blocks(b0 + X_GROUP * i, X_GROUP)
        return carry

    lax.fori_loop(0, nb // X_GROUP, group, 0)
    done = nb - nb % X_GROUP
    size = X_GROUP // 2
    while size >= 1:
        @pl.when((nb // size) % 2 == 1)
        def _(size=size, done=done):
            run_blocks(b0 + done, size)

        done = done + (nb // size) % 2 * size
        size //= 2

    @pl.when(e == pl.num_programs(0) - 1)
    def _():
        for back in range(Y_RING, 0, -1):
            @pl.when(nused >= back)
            def _():
                y_copy(nused - back).wait()

        zbuf[...] = jnp.zeros_like(zbuf)

        def z_copy(b):
            return pltpu.make_async_copy(zbuf, yb_hbm.at[rows(b)], zsem.at[0])

        def z_start(b, carry):
            z_copy(b).start()
            return carry

        def z_wait(b, carry):
            z_copy(b).wait()
            return carry

        lax.fori_loop(nused, n_blocks, z_start, 0)
        lax.fori_loop(nused, n_blocks, z_wait, 0)


def _experts(bstart, nblk, nused, xs, w_gate, w_up, w_down):
    cap = xs.shape[0]
    w_idx = lambda e, bs, nb, nu: (e, 0, 0)
    grid_spec = pltpu.PrefetchScalarGridSpec(
        num_scalar_prefetch=3,
        grid=(N_EXPERTS,),
        in_specs=[pl.BlockSpec(memory_space=pl.ANY),
                  pl.BlockSpec((None, D_MODEL, D_FF), w_idx),
                  pl.BlockSpec((None, D_MODEL, D_FF), w_idx),
                  pl.BlockSpec((None, D_FF, D_MODEL), w_idx)],
        out_specs=pl.BlockSpec(memory_space=pl.ANY),
        scratch_shapes=[pltpu.VMEM((X_RING, MOE_BLOCK, PACKED), U32),
                        pltpu.VMEM((Y_RING, MOE_BLOCK, PACKED), U32),
                        pltpu.VMEM((MOE_BLOCK, PACKED), U32),
                        pltpu.SemaphoreType.DMA((X_RING,)),
                        pltpu.SemaphoreType.DMA((Y_RING,)),
                        pltpu.SemaphoreType.DMA((1,)),
                        pltpu.VMEM((D_MODEL, D_FF), BF16),
                        pltpu.VMEM((D_MODEL, D_FF), BF16),
                        pltpu.VMEM((D_FF, D_MODEL), BF16)])
    return pl.pallas_call(
        _expert_kernel,
        grid_spec=grid_spec,
        out_shape=jax.ShapeDtypeStruct((cap, PACKED), U32),
        compiler_params=_cparams(1),
        name="experts",
    )(bstart, nblk, nused, xs, w_gate, w_up, w_down)


def _combine_kernel(y0_ref, y1_ref, h_ref, info_ref, g_ref, b_ref, o_ref):
    info = info_ref[...].T
    g0 = info[:, 2:3]
    g1 = info[:, 3:4]
    lo0, hi0 = _unpack_rows(y0_ref[...])
    lo1, hi1 = _unpack_rows(y1_ref[...])
    y = jnp.concatenate([g0 * lo0 + g1 * lo1, g0 * hi0 + g1 * hi1], axis=1)
    o_ref[...] = _layer_norm(ALPHA * h_ref[...] + y, g_ref[...], b_ref[...])


def _combine(ys, h, info, ln_g, ln_b, rows, part, n_parts):
    n = h.shape[0]
    steps = n // n_parts // rows
    off = part * steps
    const = lambda i: (0, 0)
    return pl.pallas_call(
        _combine_kernel,
        grid=(steps,),
        in_specs=[pl.BlockSpec((rows, PACKED), lambda i: (i, 0)),
                  pl.BlockSpec((rows, PACKED), lambda i: (i + steps, 0)),
                  pl.BlockSpec((rows, D_MODEL), lambda i: (i + off, 0)),
                  pl.BlockSpec((SUBLANES, rows), lambda i: (0, i + off)),
                  pl.BlockSpec((1, D_MODEL), const),
                  pl.BlockSpec((1, D_MODEL), const)],
        out_specs=pl.BlockSpec((rows, D_MODEL), lambda i: (i + off, 0)),
        out_shape=jax.ShapeDtypeStruct((n, D_MODEL), F32),
        input_output_aliases={2: 0},
        compiler_params=_cparams(1),
        name="combine",
    )(ys, ys, h, info, ln_g, ln_b)


def _alibi_bias():
    qi = np.arange(BLOCK)[:, None]
    kj = np.arange(2 * BLOCK)[None, :]
    dist = qi - kj + BLOCK
    band = (dist >= 0) & (dist < BLOCK)
    slopes = np.exp2(-8.0 * np.arange(1, N_Q_HEADS + 1, dtype=np.float32) / N_Q_HEADS)
    bias = np.where(band[None], -slopes[:, None, None] * dist[None].astype(np.float32), NEG)
    bias = bias * LOG2E
    first = np.where((kj >= PAD_FRONT)[None], bias, NEG)
    out = np.empty((2, N_KV_HEADS, 2 * BLOCK, 4 * BLOCK), np.float32)
    for v, per_head in enumerate((first, bias)):
        for j in range(N_KV_HEADS):
            out[v, j] = np.block([[per_head[4 * j], per_head[4 * j + 1]],
                                  [per_head[4 * j + 2], per_head[4 * j + 3]]])
    return jnp.asarray(out, F32)


def _slab_gates(w):
    nb, c, _ = w.shape
    per = LANES // c
    w = w.reshape(nb // per, per, c, c)
    eye = jnp.eye(per, dtype=w.dtype)
    return jnp.einsum('spcd,pq->spcqd', w, eye).reshape(nb // per, LANES, LANES).astype(BF16)


def kernel(x, meta_tokens, w_in, conv_w, conv_b, lru_wa, lru_ba, lru_wx, lru_bx, lru_lambda,
           attn_sinks, g_attn, g_lru, w_out, ln1_g, ln1_b, w_group, b_group, w_router,
           b_router, w_gate, w_up, w_down, ln2_g, ln2_b):
    bsz, seq, d = x.shape
    nbx = seq // BLOCK
    n_tok = bsz * seq
    x2d = x.reshape(n_tok, d)
    row = lambda v: v.reshape(1, -1).astype(F32)

    q_scale = jnp.concatenate([jnp.full((ATTN_WIDTH,), LOG2E * HEAD_DIM ** -0.5, F32),
                               jnp.ones((IN_COLS - ATTN_WIDTH,), F32)])
    w_in_b = (w_in[0] * q_scale).astype(BF16)
    meta_blk = jnp.concatenate([jnp.zeros((PAD_FRONT, d), F32), meta_tokens.astype(F32)], axis=0)
    q, kv, xr, yr = _in_proj(x2d, w_in_b, PROJ_ROWS)
    qm, kvm, xrm, yrm = _in_proj(meta_blk, w_in_b, BLOCK)
    shp = lambda a: a.reshape(bsz, seq, a.shape[-1])

    attn_n = _attention(attn_sinks[0].astype(F32) * LOG2E, shp(q), shp(kv), kvm, _alibi_bias(),
                        row(g_attn[0]), bsz, nbx)
    lru_n = _rglru(shp(xr), shp(yr), xrm, yrm, conv_w[0].astype(F32), row(conv_b[0]),
                   _slab_gates(0.5 * lru_wa[0]), _slab_gates(0.5 * lru_wx[0]),
                   0.5 * row(lru_ba[0]), 0.5 * row(lru_bx[0]), row(lru_lambda[0]),
                   row(g_lru[0]), bsz, nbx)

    w_out_b = w_out[0].astype(BF16)
    gpad = SUBLANES - N_GROUPS
    w_rt = jnp.concatenate(
        [w_group[0].T, jnp.zeros((gpad, d), F32),
         jnp.transpose(w_router[0], (0, 2, 1)).reshape(N_EXPERTS, d)], axis=0).astype(F32)
    w_rt_hi = w_rt.astype(BF16)
    w_rt_lo = (w_rt - w_rt_hi.astype(F32)).astype(BF16)
    b_rt = jnp.concatenate([b_group[0], jnp.zeros((gpad,), F32),
                            b_router[0].reshape(-1)]).astype(F32).reshape(ROUTER_ROWS, 1)
    h1, hp, info, cnt = _out_proj(
        attn_n.reshape(n_tok, ATTN_WIDTH), lru_n.reshape(n_tok, LRU_WIDTH), x2d,
        w_out_b[:ATTN_WIDTH], w_out_b[ATTN_WIDTH:], row(ln1_g[0]), row(ln1_b[0]),
        w_rt_hi, w_rt_lo, b_rt, OUT_PROJ_ROWS)

    dest = _route(info, cnt, ROUTE_ROWS)
    n_slots = n_tok * TOP_K
    n_blocks = n_slots // MOE_BLOCK + N_EXPERTS
    cap = n_blocks * MOE_BLOCK
    nblk = (cnt[:, 0].astype(I32) + MOE_BLOCK - 1) // MOE_BLOCK
    bends = jnp.cumsum(nblk)
    bstart = (bends - nblk).astype(I32)
    nused = bends[-1:].astype(I32)
    windows = lambda v: v.reshape(SC_WORKERS, -1, SC_WINDOW)
    d0 = dest[0]
    d1 = dest[1]

    xs = _sc_scatter_rows(hp, windows(d0), windows(d1), cap)
    yb = _experts(bstart, nblk.astype(I32), nused, xs, w_gate[0], w_up[0], w_down[0])
    out = h1
    part_len = n_tok // COMBINE_PARTS
    for part in range(COMBINE_PARTS):
        tok = slice(part * part_len, (part + 1) * part_len)
        ys = _sc_gather_rows(yb, windows(jnp.concatenate([d0[tok], d1[tok]])))
        out = _combine(ys, out, info, row(ln2_g[0]), row(ln2_b[0]), COMBINE_ROWS,
                       part, COMBINE_PARTS)
    return out.reshape(bsz, seq, d)
```

```python
import jax
import jax.numpy as jnp
import numpy as np
from jax import lax
from jax.experimental import pallas as pl
from jax.experimental.pallas import tpu as pltpu
from jax.experimental.pallas import tpu_sc as plsc

F32 = jnp.float32
BF16 = jnp.bfloat16
U32 = jnp.uint32
I32 = jnp.int32

D_MODEL = 1024
N_META = 16
BLOCK = 128
PAD_FRONT = BLOCK - N_META
HEAD_DIM = 64
ATTN_WIDTH = 512
LRU_WIDTH = 512
N_Q_HEADS = 8
N_KV_HEADS = 2
KV_WIDTH = N_KV_HEADS * HEAD_DIM
LRU_BLOCKS = 8
CONV_W = 4
LRU_C = 8.0
IN_COLS = ATTN_WIDTH + 2 * KV_WIDTH + 2 * LRU_WIDTH
N_GROUPS = 4
EXPERTS_PER_GROUP = 8
N_EXPERTS = N_GROUPS * EXPERTS_PER_GROUP
TOP_K = 2
D_FF = 512
MOE_BLOCK = 256
ALPHA = 2.0 ** 0.25
EPS = 1e-5
NEG = -1e30
LOG2E = float(np.log2(np.e))
LANES = 128
SUBLANES = 8
PACKED = D_MODEL // 2

PROJ_ROWS = 1024
OUT_PROJ_ROWS = 1024
OUT_PROJ_CHUNK = 1024
ROUTE_ROWS = 512
COMBINE_ROWS = 1024
COMBINE_PARTS = 4
X_GROUP = 4
X_AHEAD = 4
X_RING = X_AHEAD + X_GROUP
Y_RING = 2 * X_GROUP
VMEM_LIMIT = 48 * 1024 * 1024

SC_CORES = 2
SC_SUBCORES = 16
SC_WORKERS = SC_CORES * SC_SUBCORES
SC_WINDOW = 64


def _cparams(n_axes):
    return pltpu.CompilerParams(
        dimension_semantics=("arbitrary",) * n_axes, vmem_limit_bytes=VMEM_LIMIT)


def _in_proj_kernel(x_ref, w_ref, q_ref, kv_ref, xr_ref, yr_ref):
    proj = jnp.dot(x_ref[...].astype(BF16), w_ref[...], preferred_element_type=F32)
    o = 0
    for ref, width in ((q_ref, ATTN_WIDTH), (kv_ref, 2 * KV_WIDTH),
                       (xr_ref, LRU_WIDTH), (yr_ref, LRU_WIDTH)):
        ref[...] = proj[:, o:o + width].astype(ref.dtype)
        o += width


def _in_proj(x2d, w_bf16, rows):
    n = x2d.shape[0]
    widths = (ATTN_WIDTH, 2 * KV_WIDTH, LRU_WIDTH, LRU_WIDTH)
    return pl.pallas_call(
        _in_proj_kernel,
        grid=(n // rows,),
        in_specs=[pl.BlockSpec((rows, D_MODEL), lambda i: (i, 0)),
                  pl.BlockSpec((D_MODEL, IN_COLS), lambda i: (0, 0))],
        out_specs=[pl.BlockSpec((rows, w), lambda i: (i, 0)) for w in widths],
        out_shape=[jax.ShapeDtypeStruct((n, w), BF16) for w in widths],
        compiler_params=_cparams(1),
        name="in_proj",
    )(x2d, w_bf16)


def _attn_kernel(sinks_ref, q_ref, kv_ref, kvm_ref, bias_ref, g_ref, o_ref,
                 klo, khi, vlo, vhi):
    nbx = q_ref.shape[0] // BLOCK
    lo_lanes = lax.broadcasted_iota(I32, (BLOCK // 2, LANES), 1) < HEAD_DIM

    def layout_block(n, blk):
        rows = pl.ds(pl.multiple_of(n * BLOCK, BLOCK), BLOCK)
        as_bf16 = lambda words: pltpu.bitcast(words, BF16)
        for src, dst_lo, dst_hi in ((blk[:, :KV_WIDTH], klo, khi), (blk[:, KV_WIDTH:], vlo, vhi)):
            w = pltpu.bitcast(src, U32)
            r = pltpu.roll(w, HEAD_DIM, axis=1)
            zero = jnp.zeros_like(w)
            dst_lo[0, rows, :] = as_bf16(jnp.where(lo_lanes, w, zero))
            dst_hi[0, rows, :] = as_bf16(jnp.where(lo_lanes, zero, r))
            dst_lo[1, rows, :] = as_bf16(jnp.where(lo_lanes, r, zero))
            dst_hi[1, rows, :] = as_bf16(jnp.where(lo_lanes, zero, w))

    layout_block(0, kvm_ref[...])

    def layout_body(n, carry):
        layout_block(n + 1, kv_ref[pl.ds(pl.multiple_of(n * BLOCK, BLOCK), BLOCK), :])
        return carry

    lax.fori_loop(0, nbx, layout_body, 0, unroll=4)

    ones_lo = jnp.where(lax.broadcasted_iota(I32, (2 * BLOCK, LANES), 1) < HEAD_DIM,
                        1.0, 0.0).astype(BF16)
    ones_hi = (1.0 - ones_lo.astype(F32)).astype(BF16)
    top_rows = lax.broadcasted_iota(I32, (2 * BLOCK, 1), 0) < BLOCK
    lo_half = lax.broadcasted_iota(I32, (2 * BLOCK, LANES), 1) < HEAD_DIM

    def block(i, carry):
        q_rows = pl.ds(pl.multiple_of(i * BLOCK, BLOCK), BLOCK)
        win = pl.ds(pl.multiple_of(i * BLOCK, BLOCK), 2 * BLOCK)
        q = q_ref[q_rows, :]
        first = jnp.minimum(i, 1)
        outs = []
        for j in range(N_KV_HEADS):
            q2 = jnp.concatenate([q[:, (2 * j) * LANES:(2 * j + 1) * LANES],
                                  q[:, (2 * j + 1) * LANES:(2 * j + 2) * LANES]], axis=0)
            kc = jnp.concatenate([klo[j, win, :], khi[j, win, :]], axis=0)
            s = lax.dot_general(q2, kc, (((1,), (1,)), ((), ())), preferred_element_type=F32)
            s = s + bias_ref[first, j]
            ps, es = [], []
            for c in range(2):
                sink = jnp.where(top_rows, sinks_ref[4 * j + c], sinks_ref[4 * j + 2 + c])
                sc = s[:, c * 2 * BLOCK:(c + 1) * 2 * BLOCK]
                m = jnp.maximum(jnp.max(sc, axis=1, keepdims=True), sink)
                ps.append(jnp.exp2(sc - m).astype(BF16))
                es.append(jnp.exp2(sink - m))
            v_lo = jnp.concatenate([vlo[j, win, :], ones_lo], axis=1)
            v_hi = jnp.concatenate([vhi[j, win, :], ones_hi], axis=1)
            r = jnp.dot(jnp.concatenate(ps, axis=1), jnp.concatenate([v_lo, v_hi], axis=0),
                        preferred_element_type=F32)
            den = r[:, LANES:] + jnp.where(lo_half, es[0], es[1])
            o2 = r[:, :LANES] * (1.0 / den)
            outs += [o2[:BLOCK], o2[BLOCK:]]
        out = jnp.concatenate(outs, axis=1)
        ms = jnp.mean(out * out, axis=1, keepdims=True)
        o_ref[q_rows, :] = (out * lax.rsqrt(ms + EPS) * g_ref[...]).astype(o_ref.dtype)
        return carry

    lax.fori_loop(0, nbx, block, 0, unroll=8)


def _attention(sinks, q, kv, kvm, bias, g_attn, bsz, nbx):
    seq = nbx * BLOCK
    const2 = lambda b: (0, 0)
    kv_scratch = pltpu.VMEM((N_KV_HEADS, seq + BLOCK, LANES), BF16)
    return pl.pallas_call(
        _attn_kernel,
        grid=(bsz,),
        in_specs=[pl.BlockSpec(memory_space=pltpu.SMEM),
                  pl.BlockSpec((None, seq, ATTN_WIDTH), lambda b: (b, 0, 0)),
                  pl.BlockSpec((None, seq, 2 * KV_WIDTH), lambda b: (b, 0, 0)),
                  pl.BlockSpec((BLOCK, 2 * KV_WIDTH), const2),
                  pl.BlockSpec((2, N_KV_HEADS, 2 * BLOCK, 4 * BLOCK), lambda b: (0, 0, 0, 0)),
                  pl.BlockSpec((1, ATTN_WIDTH), const2)],
        out_specs=pl.BlockSpec((None, seq, ATTN_WIDTH), lambda b: (b, 0, 0)),
        out_shape=jax.ShapeDtypeStruct((bsz, seq, ATTN_WIDTH), BF16),
        scratch_shapes=[kv_scratch, kv_scratch, kv_scratch, kv_scratch],
        compiler_params=_cparams(1),
        name="attention",
    )(sinks, q, kv, kvm, bias, g_attn)


def _sigmoid(v):
    return 0.5 * jnp.tanh(0.5 * v) + 0.5


def _gelu_tanh(y):
    c = float(np.sqrt(2.0 / np.pi))
    half = 0.5 * y
    return half + half * jnp.tanh(y * (c + (c * 0.044715) * (y * y)))


LRU_CHUNK = 44
LRU_SEG = SUBLANES * LRU_CHUNK
LRU_SLABS = LRU_WIDTH // LANES


def _lru_kernel(xr_ref, yr_ref, xrm_ref, yrm_ref, cw_ref, cb_ref, wa_ref, wx_ref, ba_ref,
                bx_ref, lam_ref, g_ref, o_ref, x_st, y_st, o_st, s_st, xtail, hcar):
    seq = xr_ref.shape[0]
    n_seg = (seq + BLOCK) // LRU_SEG
    xtail[...] = jnp.zeros_like(xtail)
    hcar[...] = jnp.zeros_like(hcar)
    lam = lam_ref[...]
    softplus_neg = jnp.maximum(-lam, 0.0) + jnp.log(1.0 + jnp.exp(-jnp.abs(lam)))
    sub = lax.broadcasted_iota(jnp.int32, (SUBLANES, LANES), 0)

    def strided(j):
        return pl.ds(j, SUBLANES, stride=LRU_CHUNK)

    def piece(v, j):
        return v[j * SUBLANES:(j + 1) * SUBLANES, :]

    def segment(k, first):
        if first:
            head = LRU_SEG - BLOCK
            x_nat = jnp.concatenate([xrm_ref[...], xr_ref[0:head, :]], axis=0).astype(F32)
            y_nat = jnp.concatenate([yrm_ref[...], yr_ref[0:head, :]], axis=0).astype(F32)
        else:
            rows = pl.ds(pl.multiple_of(k * LRU_SEG - BLOCK, 2 * SUBLANES), LRU_SEG)
            x_nat = xr_ref[rows, :].astype(F32)
            y_nat = yr_ref[rows, :].astype(F32)
        for c in range(LRU_SLABS):
            x_st[c] = x_nat[:, c * LANES:(c + 1) * LANES]
            y_st[c] = y_nat[:, c * LANES:(c + 1) * LANES]
        first_row = k * LRU_SEG + LRU_CHUNK * sub
        sumsq = [jnp.zeros((SUBLANES, LANES), F32) for _ in range(LRU_CHUNK)]

        for c in range(LRU_SLABS):
            lanes = slice(c * LANES, (c + 1) * LANES)
            x = [x_st[c, strided(j), :] for j in range(LRU_CHUNK)]
            before = []
            for d in range(1, CONV_W):
                from_prev_chunk = pltpu.roll(x[LRU_CHUNK - d], 1, axis=0)
                before.append(jnp.where(sub == 0, xtail[d - 1:d, lanes], from_prev_chunk))
            for d in range(1, CONV_W):
                xtail[d - 1:d, lanes] = x[LRU_CHUNK - d][SUBLANES - 1:SUBLANES, :]

            def x_at(j):
                return x[j] if j >= 0 else before[-j - 1]

            taps = [cw_ref[t:t + 1, lanes] for t in range(CONV_W)]
            bias = cb_ref[:, lanes]
            xc = jnp.concatenate(
                [bias + sum(taps[t] * x_at(j - (CONV_W - 1) + t) for t in range(CONV_W))
                 for j in range(LRU_CHUNK)], axis=0)
            xcb = xc.astype(BF16)
            both = jnp.dot(xcb, jnp.concatenate([wa_ref[c], wx_ref[c]], axis=1),
                           preferred_element_type=F32)
            tr = jnp.tanh(both[:, :LANES] + ba_ref[:, lanes])
            ti = jnp.tanh(both[:, LANES:] + bx_ref[:, lanes])
            log_a_half = (-0.5 * LRU_C) * softplus_neg[:, lanes]
            a = jnp.exp(log_a_half * tr + log_a_half)
            half_xc = 0.5 * xc
            gated_x = half_xc * ti + half_xc
            z = 1.0 - a * a
            u = jnp.where(z > 0.0, z * lax.rsqrt(z), 0.0) * gated_x

            h = jnp.zeros((SUBLANES, LANES), F32)
            p = jnp.ones((SUBLANES, LANES), F32)
            hs, ps = [], []
            for j in range(LRU_CHUNK):
                aj = piece(a, j)
                uj = piece(u, j)
                if first:
                    uj = jnp.where(first_row + j >= PAD_FRONT, uj, 0.0)
                h = aj * h + uj
                p = aj * p
                hs.append(h)
                ps.append(p)
            entry = [hcar[:, lanes]]
            for s in range(SUBLANES):
                entry.append(h[s:s + 1, :] + p[s:s + 1, :] * entry[s])
            hcar[:, lanes] = entry[SUBLANES]
            entry_rows = jnp.concatenate(entry[:SUBLANES], axis=0)

            for j in range(LRU_CHUNK):
                state = hs[j] + ps[j] * entry_rows
                out = state * _gelu_tanh(y_st[c, strided(j), :])
                sumsq[j] = sumsq[j] + out * out
                o_st[c, strided(j), :] = out

        for j in range(LRU_CHUNK):
            ms = jnp.sum(sumsq[j], axis=1, keepdims=True) * (1.0 / LRU_WIDTH)
            s_st[strided(j), :] = jnp.broadcast_to(lax.rsqrt(ms + EPS), (SUBLANES, LANES))
        scale = s_st[...]
        for c in range(LRU_SLABS):
            lanes = slice(c * LANES, (c + 1) * LANES)
            normed = (o_st[c] * scale * g_ref[:, lanes]).astype(o_ref.dtype)
            if first:
                o_ref[0:LRU_SEG - BLOCK, lanes] = normed[BLOCK:, :]
            else:
                o_ref[rows, lanes] = normed

    assert BLOCK <= LRU_SEG
    segment(0, True)

    def later_segment(k, carry):
        segment(k, False)
        return carry

    lax.fori_loop(1, n_seg, later_segment, 0)


def _rglru(xr, yr, xrm, yrm, cw, cb, wa, wx, ba, bx, lam, g_lru, bsz, nbx):
    seq = nbx * BLOCK
    assert (seq + BLOCK) % LRU_SEG == 0
    main = pl.BlockSpec((None, seq, LRU_WIDTH), lambda b: (b, 0, 0))
    const2 = lambda b: (0, 0)
    row_spec = pl.BlockSpec((1, LRU_WIDTH), const2)
    gate_spec = pl.BlockSpec((LRU_SLABS, LANES, LANES), lambda b: (0, 0, 0))
    slabs = pltpu.VMEM((LRU_SLABS, LRU_SEG, LANES), F32)
    return pl.pallas_call(
        _lru_kernel,
        grid=(bsz,),
        in_specs=[main, main,
                  pl.BlockSpec((BLOCK, LRU_WIDTH), const2),
                  pl.BlockSpec((BLOCK, LRU_WIDTH), const2),
                  pl.BlockSpec((CONV_W, LRU_WIDTH), const2),
                  row_spec, gate_spec, gate_spec,
                  row_spec, row_spec, row_spec, row_spec],
        out_specs=main,
        out_shape=jax.ShapeDtypeStruct((bsz, seq, LRU_WIDTH), BF16),
        scratch_shapes=[slabs, slabs, slabs,
                        pltpu.VMEM((LRU_SEG, LANES), F32),
                        pltpu.VMEM((SUBLANES, LRU_WIDTH), F32),
                        pltpu.VMEM((1, LRU_WIDTH), F32)],
        compiler_params=_cparams(1),
        name="rglru",
    )(xr, yr, xrm, yrm, cw, cb, wa, wx, ba, bx, lam, g_lru)


def _pack_rows(v):
    bits = lax.bitcast_convert_type(v.astype(BF16).astype(F32), U32)
    return (bits[:, :PACKED] >> 16) | (bits[:, PACKED:] & jnp.uint32(0xFFFF0000))


def _unpack_rows(w):
    lo = lax.bitcast_convert_type(w << 16, F32)
    hi = lax.bitcast_convert_type(w & jnp.uint32(0xFFFF0000), F32)
    return lo, hi


def _layer_norm(z, g, b):
    mu = jnp.mean(z, axis=1, keepdims=True)
    zc = z - mu
    var = jnp.mean(zc * zc, axis=1, keepdims=True)
    return zc * lax.rsqrt(var + EPS) * g + b


def _out_proj_kernel(a_ref, l_ref, x_ref, wa_ref, wl_ref, g_ref, b_ref, wrt_hi_ref, wrt_lo_ref,
                     brt_ref, h_ref, hp_ref, info_ref, cnt_ref):
    @pl.when(pl.program_id(0) == 0)
    def _():
        cnt_ref[...] = jnp.zeros_like(cnt_ref)

    for c in range(h_ref.shape[0] // OUT_PROJ_CHUNK):
        rows = slice(c * OUT_PROJ_CHUNK, (c + 1) * OUT_PROJ_CHUNK)
        _out_proj_rows(a_ref[rows, :], l_ref[rows, :], x_ref[rows, :], wa_ref, wl_ref, g_ref,
                       b_ref, wrt_hi_ref, wrt_lo_ref, brt_ref,
                       h_ref.at[rows, :], hp_ref.at[rows, :], info_ref.at[:, rows], cnt_ref)


def _out_proj_rows(a, l, x, wa_ref, wl_ref, g_ref, b_ref, wrt_hi_ref, wrt_lo_ref, brt_ref,
                   h_ref, hp_ref, info_ref, cnt_ref):
    mix = jnp.dot(jnp.concatenate([a, l], axis=1),
                  jnp.concatenate([wa_ref[...], wl_ref[...]], axis=0),
                  preferred_element_type=F32)
    h = _layer_norm(ALPHA * x + mix, g_ref[...], b_ref[...])
    h_ref[...] = h
    hp_ref[...] = _pack_rows(h)

    h_hi = h.astype(BF16)
    h_lo = (h - h_hi.astype(F32)).astype(BF16)
    nt = (((1,), (1,)), ((), ()))
    w_both = jnp.concatenate([wrt_hi_ref[...], wrt_lo_ref[...]], axis=0)
    both = lax.dot_general(w_both, h_hi, nt, preferred_element_type=F32)
    lg = (both[:ROUTER_ROWS] + both[ROUTER_ROWS:]
          + lax.dot_general(wrt_hi_ref[...], h_lo, nt, preferred_element_type=F32)) + brt_ref[...]
    tile_shape = (SUBLANES, h.shape[0])
    sub = lax.broadcasted_iota(I32, tile_shape, 0)
    ninf = -jnp.inf
    t0 = lg[0:SUBLANES]
    gl = jnp.where(sub < N_GROUPS, t0, ninf)
    gmax = jnp.max(gl, axis=0, keepdims=True)
    g_idx = jnp.min(jnp.where(gl == gmax, sub, SUBLANES), axis=0, keepdims=True)
    g_w = 1.0 / jnp.sum(jnp.where(sub < N_GROUPS, jnp.exp(t0 - gmax), 0.0),
                        axis=0, keepdims=True)
    el = lg[SUBLANES:2 * SUBLANES]
    for g in range(1, N_GROUPS):
        el = jnp.where(g_idx == g, lg[(g + 1) * SUBLANES:(g + 2) * SUBLANES], el)
    v1 = jnp.max(el, axis=0, keepdims=True)
    i1 = jnp.min(jnp.where(el == v1, sub, SUBLANES), axis=0, keepdims=True)
    el2 = jnp.where(sub == i1, ninf, el)
    v2 = jnp.max(el2, axis=0, keepdims=True)
    i2 = jnp.min(jnp.where(el2 == v2, sub, SUBLANES), axis=0, keepdims=True)
    t = jnp.exp(v2 - v1)
    w1 = 1.0 / (1.0 + t)
    w2 = t * w1
    e_base = g_idx * EXPERTS_PER_GROUP
    e1 = e_base + i1
    e2 = e_base + i2
    info_ref[...] = jnp.where(sub == 0, e1.astype(F32),
                              jnp.where(sub == 1, e2.astype(F32),
                                        jnp.where(sub == 2, g_w * w1,
                                                  jnp.where(sub == 3, g_w * w2, 0.0))))
    expert = lax.broadcasted_iota(I32, (N_EXPERTS, h.shape[0]), 0)
    chosen = (expert == e1).astype(F32) + (expert == e2).astype(F32)
    cnt_ref[...] += jnp.sum(chosen, axis=1, keepdims=True)


ROUTER_ROWS = (N_GROUPS + 1) * SUBLANES


def _out_proj(attn_n, lru_n, x2d, wo_a, wo_l, ln_g, ln_b, w_rt_hi, w_rt_lo, b_rt, rows):
    n = x2d.shape[0]
    const = lambda i: (0, 0)
    tile = lambda w: pl.BlockSpec((rows, w), lambda i: (i, 0))
    return pl.pallas_call(
        _out_proj_kernel,
        grid=(n // rows,),
        in_specs=[tile(ATTN_WIDTH), tile(LRU_WIDTH), tile(D_MODEL),
                  pl.BlockSpec((ATTN_WIDTH, D_MODEL), const),
                  pl.BlockSpec((LRU_WIDTH, D_MODEL), const),
                  pl.BlockSpec((1, D_MODEL), const),
                  pl.BlockSpec((1, D_MODEL), const),
                  pl.BlockSpec((ROUTER_ROWS, D_MODEL), const),
                  pl.BlockSpec((ROUTER_ROWS, D_MODEL), const),
                  pl.BlockSpec((ROUTER_ROWS, 1), const)],
        out_specs=[tile(D_MODEL), tile(PACKED),
                   pl.BlockSpec((SUBLANES, rows), lambda i: (0, i)),
                   pl.BlockSpec((N_EXPERTS, LANES), const)],
        out_shape=[jax.ShapeDtypeStruct((n, D_MODEL), F32),
                   jax.ShapeDtypeStruct((n, PACKED), U32),
                   jax.ShapeDtypeStruct((SUBLANES, n), F32),
                   jax.ShapeDtypeStruct((N_EXPERTS, LANES), F32)],
        compiler_params=_cparams(1),
        name="out_proj",
    )(attn_n, lru_n, x2d, wo_a, wo_l, ln_g, ln_b, w_rt_hi, w_rt_lo, b_rt)


def _route_kernel(info_ref, cnt_ref, tri_ref, dest_ref, carry, pstart):
    t = pl.program_id(0)
    info = info_ref[...]
    shape = (N_EXPERTS, info.shape[1])
    expert = lax.broadcasted_iota(I32, shape, 0)
    oh1 = (expert == info[0:1, :].astype(I32)).astype(F32)
    oh2 = (expert == info[1:2, :].astype(I32)).astype(F32)
    both = oh1 + oh2

    @pl.when(t == 0)
    def _():
        c = cnt_ref[...].astype(I32)
        padded = ((c + (MOE_BLOCK - 1)) // MOE_BLOCK) * MOE_BLOCK
        e = lax.broadcasted_iota(I32, (N_EXPERTS, LANES), 0)
        scan = padded
        for d in (1, 2, 4, 8, 16):
            scan = scan + jnp.where(e >= d, pltpu.roll(scan, d, axis=0), 0)
        pstart[...] = (scan - padded)[:, 0:1].astype(F32)
        carry[...] = jnp.zeros_like(carry)

    before = jnp.dot(both.astype(BF16), tri_ref[...], preferred_element_type=F32)
    row_of = before + (carry[...] + pstart[...])
    r1 = jnp.sum(oh1 * row_of, axis=0, keepdims=True)
    r2 = jnp.sum(oh2 * row_of, axis=0, keepdims=True)
    sub = lax.broadcasted_iota(I32, dest_ref.shape, 0)
    dest_ref[...] = jnp.where(sub == 0, r1, jnp.where(sub == 1, r2, 0.0)).astype(I32)
    carry[...] += jnp.sum(both, axis=1, keepdims=True)


def _route(info_t, cnt, cols):
    n = info_t.shape[1]
    tri = jnp.asarray(np.triu(np.ones((cols, cols), np.float32), 1), BF16)
    return pl.pallas_call(
        _route_kernel,
        grid=(n // cols,),
        in_specs=[pl.BlockSpec((SUBLANES, cols), lambda t: (0, t)),
                  pl.BlockSpec((N_EXPERTS, LANES), lambda t: (0, 0)),
                  pl.BlockSpec((cols, cols), lambda t: (0, 0))],
        out_specs=pl.BlockSpec((SUBLANES, cols), lambda t: (0, t)),
        out_shape=jax.ShapeDtypeStruct((SUBLANES, n), I32),
        scratch_shapes=[pltpu.VMEM((N_EXPERTS, 1), F32), pltpu.VMEM((N_EXPERTS, 1), F32)],
        compiler_params=_cparams(1),
        name="route",
    )(info_t, cnt, tri)


def _sc_mesh():
    return plsc.VectorSubcoreMesh(core_axis_name="core", subcore_axis_name="subcore")


def _sc_worker_id():
    return lax.axis_index("subcore") * SC_CORES + lax.axis_index("core")


def _sc_scatter_rows(rows, d0, d1, cap):
    n, width = rows.shape
    per_worker = n // SC_WORKERS
    n_win = per_worker // SC_WINDOW

    def body(x_hbm, d0_hbm, d1_hbm, o_hbm, i0_v, i1_v, rows_v, rsem, sem0, sem1):
        wid = _sc_worker_id()
        pltpu.sync_copy(d0_hbm.at[wid], i0_v)
        pltpu.sync_copy(d1_hbm.at[wid], i1_v)

        def read(j):
            src = x_hbm.at[pl.ds(wid * per_worker + j * SC_WINDOW, SC_WINDOW)]
            return pltpu.make_async_copy(src, rows_v.at[j % 2], rsem.at[j % 2])

        def scatters(j):
            return (pltpu.make_async_copy(rows_v.at[j % 2], o_hbm.at[i0_v.at[j]], sem0.at[j % 2]),
                    pltpu.make_async_copy(rows_v.at[j % 2], o_hbm.at[i1_v.at[j]], sem1.at[j % 2]))

        read(0).start()
        for j in range(n_win):
            if j + 1 < n_win:
                if j >= 1:
                    for cp in scatters(j - 1):
                        cp.wait()
                read(j + 1).start()
            read(j).wait()
            for cp in scatters(j):
                cp.start()
        for j in range(max(n_win - 2, 0), n_win):
            for cp in scatters(j):
                cp.wait()

    return pl.kernel(
        body,
        out_type=jax.ShapeDtypeStruct((cap, width), rows.dtype),
        mesh=_sc_mesh(),
        scratch_types=[pltpu.VMEM((n_win, SC_WINDOW), I32), pltpu.VMEM((n_win, SC_WINDOW), I32),
                       pltpu.VMEM((2, SC_WINDOW, width), rows.dtype),
                       pltpu.SemaphoreType.DMA((2,)), pltpu.SemaphoreType.DMA((2,)),
                       pltpu.SemaphoreType.DMA((2,))],
        name="dispatch",
    )(rows, d0, d1)


def _sc_gather_rows(table, idx):
    width = table.shape[1]
    n_win = idx.shape[1]
    per_worker = n_win * SC_WINDOW

    def body(y_hbm, i_hbm, o_hbm, i_v, rows_v, gsem, wsem):
        wid = _sc_worker_id()
        pltpu.sync_copy(i_hbm.at[wid], i_v)

        def gather(j):
            return pltpu.make_async_copy(y_hbm.at[i_v.at[j]], rows_v.at[j % 2], gsem.at[j % 2])

        def write(j):
            dst = o_hbm.at[pl.ds(wid * per_worker + j * SC_WINDOW, SC_WINDOW)]
            return pltpu.make_async_copy(rows_v.at[j % 2], dst, wsem.at[j % 2])

        gather(0).start()
        for j in range(n_win):
            if j + 1 < n_win:
                if j >= 1:
                    write(j - 1).wait()
                gather(j + 1).start()
            gather(j).wait()
            write(j).start()
        for j in range(max(n_win - 2, 0), n_win):
            write(j).wait()

    return pl.kernel(
        body,
        out_type=jax.ShapeDtypeStruct((SC_WORKERS * per_worker, width), table.dtype),
        mesh=_sc_mesh(),
        scratch_types=[pltpu.VMEM((n_win, SC_WINDOW), I32),
                       pltpu.VMEM((2, SC_WINDOW, width), table.dtype),
                       pltpu.SemaphoreType.DMA((2,)), pltpu.SemaphoreType.DMA((2,))],
        name="collect",
    )(table, idx)


def _expert_kernel(bstart_ref, nblk_ref, nused_ref, xs_hbm, wg_ref, wu_ref, wd_ref, yb_hbm,
                   xbuf, ybuf, zbuf, xsem, ysem, zsem, wg_b, wu_b, wd_b):
    e = pl.program_id(0)
    nused = nused_ref[0]
    n_blocks = yb_hbm.shape[0] // MOE_BLOCK

    def rows(b):
        return pl.ds(pl.multiple_of(b * MOE_BLOCK, MOE_BLOCK), MOE_BLOCK)

    def x_copy(b):
        slot = b % X_RING
        return pltpu.make_async_copy(xs_hbm.at[rows(b)], xbuf.at[slot], xsem.at[slot])

    def y_copy(b):
        slot = b % Y_RING
        return pltpu.make_async_copy(ybuf.at[slot], yb_hbm.at[rows(b)], ysem.at[slot])

    @pl.when(e == 0)
    def _():
        for b in range(X_AHEAD):
            @pl.when(b < nused)
            def _():
                x_copy(b).start()

    wg_b[...] = wg_ref[...].astype(BF16)
    wu_b[...] = wu_ref[...].astype(BF16)
    wd_b[...] = wd_ref[...].astype(BF16)

    def run_blocks(b, count):
        for k in range(count):
            x_copy(b + k).wait()
        for k in range(count):
            nxt = b + X_AHEAD + k

            @pl.when(nxt < nused)
            def _():
                x_copy(nxt).start()

            @pl.when(b + k >= Y_RING)
            def _():
                y_copy(b + k - Y_RING).wait()

        words = jnp.concatenate([xbuf[(b + k) % X_RING] for k in range(count)], axis=0)
        lo, hi = _unpack_rows(words)
        x = jnp.concatenate([lo.astype(BF16), hi.astype(BF16)], axis=1)
        g = jnp.dot(x, wg_b[...], preferred_element_type=F32)
        u = jnp.dot(x, wu_b[...], preferred_element_type=F32)
        mid = (g * _sigmoid(g) * u).astype(BF16)
        y = _pack_rows(jnp.dot(mid, wd_b[...], preferred_element_type=F32))
        for k in range(count):
            ybuf[(b + k) % Y_RING] = y[k * MOE_BLOCK:(k + 1) * MOE_BLOCK]
            y_copy(b + k).start()

    b0 = bstart_ref[e]
    nb = nblk_ref[e]

    def group(i, carry):
        run_blocks(b0 + X_GROUP * i, X_GROUP)
        return carry

    lax.fori_loop(0, nb // X_GROUP, group, 0)
    done = nb - nb % X_GROUP
    size = X_GROUP // 2
    while size >= 1:
        @pl.when((nb // size) % 2 == 1)
        def _(size=size, done=done):
            run_blocks(b0 + done, size)

        done = done + (nb // size) % 2 * size
        size //= 2

    @pl.when(e == pl.num_programs(0) - 1)
    def _():
        for back in range(Y_RING, 0, -1):
            @pl.when(nused >= back)
            def _():
                y_copy(nused - back).wait()

        zbuf[...] = jnp.zeros_like(zbuf)

        def z_copy(b):
            return pltpu.make_async_copy(zbuf, yb_hbm.at[rows(b)], zsem.at[0])

        def z_start(b, carry):
            z_copy(b).start()
            return carry

        def z_wait(b, carry):
            z_copy(b).wait()
            return carry

        lax.fori_loop(nused, n_blocks, z_start, 0)
        lax.fori_loop(nused, n_blocks, z_wait, 0)


def _experts(bstart, nblk, nused, xs, w_gate, w_up, w_down):
    cap = xs.shape[0]
    w_idx = lambda e, bs, nb, nu: (e, 0, 0)
    grid_spec = pltpu.PrefetchScalarGridSpec(
        num_scalar_prefetch=3,
        grid=(N_EXPERTS,),
        in_specs=[pl.BlockSpec(memory_space=pl.ANY),
                  pl.BlockSpec((None, D_MODEL, D_FF), w_idx),
                  pl.BlockSpec((None, D_MODEL, D_FF), w_idx),
                  pl.BlockSpec((None, D_FF, D_MODEL), w_idx)],
        out_specs=pl.BlockSpec(memory_space=pl.ANY),
        scratch_shapes=[pltpu.VMEM((X_RING, MOE_BLOCK, PACKED), U32),
                        pltpu.VMEM((Y_RING, MOE_BLOCK, PACKED), U32),
                        pltpu.VMEM((MOE_BLOCK, PACKED), U32),
                        pltpu.SemaphoreType.DMA((X_RING,)),
                        pltpu.SemaphoreType.DMA((Y_RING,)),
                        pltpu.SemaphoreType.DMA((1,)),
                        pltpu.VMEM((D_MODEL, D_FF), BF16),
                        pltpu.VMEM((D_MODEL, D_FF), BF16),
                        pltpu.VMEM((D_FF, D_MODEL), BF16)])
    return pl.pallas_call(
        _expert_kernel,
        grid_spec=grid_spec,
        out_shape=jax.ShapeDtypeStruct((cap, PACKED), U32),
        compiler_params=_cparams(1),
        name="experts",
    )(bstart, nblk, nused, xs, w_gate, w_up, w_down)


def _combine_kernel(y0_ref, y1_ref, h_ref, info_ref, g_ref, b_ref, o_ref):
    info = info_ref[...].T
    g0 = info[:, 2:3]
    g1 = info[:, 3:4]
    lo0, hi0 = _unpack_rows(y0_ref[...])
    lo1, hi1 = _unpack_rows(y1_ref[...])
    y = jnp.concatenate([g0 * lo0 + g1 * lo1, g0 * hi0 + g1 * hi1], axis=1)
    o_ref[...] = _layer_norm(ALPHA * h_ref[...] + y, g_ref[...], b_ref[...])


def _combine(ys, h, info, ln_g, ln_b, rows, part, n_parts):
    n = h.shape[0]
    steps = n // n_parts // rows
    off = part * steps
    const = lambda i: (0, 0)
    return pl.pallas_call(
        _combine_kernel,
        grid=(steps,),
        in_specs=[pl.BlockSpec((rows, PACKED), lambda i: (i, 0)),
                  pl.BlockSpec((rows, PACKED), lambda i: (i + steps, 0)),
                  pl.BlockSpec((rows, D_MODEL), lambda i: (i + off, 0)),
                  pl.BlockSpec((SUBLANES, rows), lambda i: (0, i + off)),
                  pl.BlockSpec((1, D_MODEL), const),
                  pl.BlockSpec((1, D_MODEL), const)],
        out_specs=pl.BlockSpec((rows, D_MODEL), lambda i: (i + off, 0)),
        out_shape=jax.ShapeDtypeStruct((n, D_MODEL), F32),
        input_output_aliases={2: 0},
        compiler_params=_cparams(1),
        name="combine",
    )(ys, ys, h, info, ln_g, ln_b)


def _alibi_bias():
    qi = np.arange(BLOCK)[:, None]
    kj = np.arange(2 * BLOCK)[None, :]
    dist = qi - kj + BLOCK
    band = (dist >= 0) & (dist < BLOCK)
    slopes = np.exp2(-8.0 * np.arange(1, N_Q_HEADS + 1, dtype=np.float32) / N_Q_HEADS)
    bias = np.where(band[None], -slopes[:, None, None] * dist[None].astype(np.float32), NEG)
    bias = bias * LOG2E
    first = np.where((kj >= PAD_FRONT)[None], bias, NEG)
    out = np.empty((2, N_KV_HEADS, 2 * BLOCK, 4 * BLOCK), np.float32)
    for v, per_head in enumerate((first, bias)):
        for j in range(N_KV_HEADS):
            out[v, j] = np.block([[per_head[4 * j], per_head[4 * j + 1]],
                                  [per_head[4 * j + 2], per_head[4 * j + 3]]])
    return jnp.asarray(out, F32)


def _slab_gates(w):
    nb, c, _ = w.shape
    per = LANES // c
    w = w.reshape(nb // per, per, c, c)
    eye = jnp.eye(per, dtype=w.dtype)
    return jnp.einsum('spcd,pq->spcqd', w, eye).reshape(nb // per, LANES, LANES).astype(BF16)


def kernel(x, meta_tokens, w_in, conv_w, conv_b, lru_wa, lru_ba, lru_wx, lru_bx, lru_lambda,
           attn_sinks, g_attn, g_lru, w_out, ln1_g, ln1_b, w_group, b_group, w_router,
           b_router, w_gate, w_up, w_down, ln2_g, ln2_b):
    bsz, seq, d = x.shape
    nbx = seq // BLOCK
    n_tok = bsz * seq
    x2d = x.reshape(n_tok, d)
    row = lambda v: v.reshape(1, -1).astype(F32)

    q_scale = jnp.concatenate([jnp.full((ATTN_WIDTH,), LOG2E * HEAD_DIM ** -0.5, F32),
                               jnp.ones((IN_COLS - ATTN_WIDTH,), F32)])
    w_in_b = (w_in[0] * q_scale).astype(BF16)
    meta_blk = jnp.concatenate([jnp.zeros((PAD_FRONT, d), F32), meta_tokens.astype(F32)], axis=0)
    q, kv, xr, yr = _in_proj(x2d, w_in_b, PROJ_ROWS)
    qm, kvm, xrm, yrm = _in_proj(meta_blk, w_in_b, BLOCK)
    shp = lambda a: a.reshape(bsz, seq, a.shape[-1])

    attn_n = _attention(attn_sinks[0].astype(F32) * LOG2E, shp(q), shp(kv), kvm, _alibi_bias(),
                        row(g_attn[0]), bsz, nbx)
    lru_n = _rglru(shp(xr), shp(yr), xrm, yrm, conv_w[0].astype(F32), row(conv_b[0]),
                   _slab_gates(0.5 * lru_wa[0]), _slab_gates(0.5 * lru_wx[0]),
                   0.5 * row(lru_ba[0]), 0.5 * row(lru_bx[0]), row(lru_lambda[0]),
                   row(g_lru[0]), bsz, nbx)

    w_out_b = w_out[0].astype(BF16)
    gpad = SUBLANES - N_GROUPS
    w_rt = jnp.concatenate(
        [w_group[0].T, jnp.zeros((gpad, d), F32),
         jnp.transpose(w_router[0], (0, 2, 1)).reshape(N_EXPERTS, d)], axis=0).astype(F32)
    w_rt_hi = w_rt.astype(BF16)
    w_rt_lo = (w_rt - w_rt_hi.astype(F32)).astype(BF16)
    b_rt = jnp.concatenate([b_group[0], jnp.zeros((gpad,), F32),
                            b_router[0].reshape(-1)]).astype(F32).reshape(ROUTER_ROWS, 1)
    h1, hp, info, cnt = _out_proj(
        attn_n.reshape(n_tok, ATTN_WIDTH), lru_n.reshape(n_tok, LRU_WIDTH), x2d,
        w_out_b[:ATTN_WIDTH], w_out_b[ATTN_WIDTH:], row(ln1_g[0]), row(ln1_b[0]),
        w_rt_hi, w_rt_lo, b_rt, OUT_PROJ_ROWS)

    dest = _route(info, cnt, ROUTE_ROWS)
    n_slots = n_tok * TOP_K
    n_blocks = n_slots // MOE_BLOCK + N_EXPERTS
    cap = n_blocks * MOE_BLOCK
    nblk = (cnt[:, 0].astype(I32) + MOE_BLOCK - 1) // MOE_BLOCK
    bends = jnp.cumsum(nblk)
    bstart = (bends - nblk).astype(I32)
    nused = bends[-1:].astype(I32)
    windows = lambda v: v.reshape(SC_WORKERS, -1, SC_WINDOW)
    d0 = dest[0]
    d1 = dest[1]

    xs = _sc_scatter_rows(hp, windows(d0), windows(d1), cap)
    yb = _experts(bstart, nblk.astype(I32), nused, xs, w_gate[0], w_up[0], w_down[0])
    out = h1
    part_len = n_tok // COMBINE_PARTS
    for part in range(COMBINE_PARTS):
        tok = slice(part * part_len, (part + 1) * part_len)
        ys = _sc_gather_rows(yb, windows(jnp.concatenate([d0[tok], d1[tok]])))
        out = _combine(ys, out, info, row(ln2_g[0]), row(ln2_b[0]), COMBINE_ROWS,
                       part, COMBINE_PARTS)
    return out.reshape(bsz, seq, d)
```

```python
import jax
import jax.numpy as jnp
import numpy as np
from jax import lax
from jax.experimental import pallas as pl
from jax.experimental.pallas import tpu as pltpu
from jax.experimental.pallas import tpu_sc as plsc

F32 = jnp.float32
BF16 = jnp.bfloat16
U32 = jnp.uint32
I32 = jnp.int32

D_MODEL = 1024
N_META = 16
BLOCK = 128
PAD_FRONT = BLOCK - N_META
HEAD_DIM = 64
ATTN_WIDTH = 512
LRU_WIDTH = 512
N_Q_HEADS = 8
N_KV_HEADS = 2
KV_WIDTH = N_KV_HEADS * HEAD_DIM
LRU_BLOCKS = 8
CONV_W = 4
LRU_C = 8.0
IN_COLS = ATTN_WIDTH + 2 * KV_WIDTH + 2 * LRU_WIDTH
N_GROUPS = 4
EXPERTS_PER_GROUP = 8
N_EXPERTS = N_GROUPS * EXPERTS_PER_GROUP
TOP_K = 2
D_FF = 512
MOE_BLOCK = 256
ALPHA = 2.0 ** 0.25
EPS = 1e-5
NEG = -1e30
LOG2E = float(np.log2(np.e))
LANES = 128
SUBLANES = 8
PACKED = D_MODEL // 2

PROJ_ROWS = 1024
OUT_PROJ_ROWS = 1024
ROUTE_ROWS = 512
COMBINE_ROWS = 1024
COMBINE_PARTS = 4
X_GROUP = 4
X_AHEAD = 4
X_RING = X_AHEAD + X_GROUP
Y_RING = 2 * X_GROUP
VMEM_LIMIT = 48 * 1024 * 1024

SC_CORES = 2
SC_SUBCORES = 16
SC_WORKERS = SC_CORES * SC_SUBCORES
SC_WINDOW = 64


def _cparams(n_axes):
    return pltpu.CompilerParams(
        dimension_semantics=("arbitrary",) * n_axes, vmem_limit_bytes=VMEM_LIMIT)


def _in_proj_kernel(x_ref, w_ref, q_ref, kv_ref, xr_ref, yr_ref):
    proj = jnp.dot(x_ref[...].astype(BF16), w_ref[...], preferred_element_type=F32)
    o = 0
    for ref, width in ((q_ref, ATTN_WIDTH), (kv_ref, 2 * KV_WIDTH),
                       (xr_ref, LRU_WIDTH), (yr_ref, LRU_WIDTH)):
        ref[...] = proj[:, o:o + width].astype(ref.dtype)
        o += width


def _in_proj(x2d, w_bf16, rows):
    n = x2d.shape[0]
    widths = (ATTN_WIDTH, 2 * KV_WIDTH, LRU_WIDTH, LRU_WIDTH)
    return pl.pallas_call(
        _in_proj_kernel,
        grid=(n // rows,),
        in_specs=[pl.BlockSpec((rows, D_MODEL), lambda i: (i, 0)),
                  pl.BlockSpec((D_MODEL, IN_COLS), lambda i: (0, 0))],
        out_specs=[pl.BlockSpec((rows, w), lambda i: (i, 0)) for w in widths],
        out_shape=[jax.ShapeDtypeStruct((n, w), BF16) for w in widths],
        compiler_params=_cparams(1),
        name="in_proj",
    )(x2d, w_bf16)


def _attn_kernel(sinks_ref, q_ref, kv_ref, kvm_ref, bias_ref, g_ref, o_ref,
                 klo, khi, vlo, vhi):
    nbx = q_ref.shape[0] // BLOCK
    lo_lanes = lax.broadcasted_iota(I32, (BLOCK // 2, LANES), 1) < HEAD_DIM

    def layout_block(n, blk):
        rows = pl.ds(pl.multiple_of(n * BLOCK, BLOCK), BLOCK)
        as_bf16 = lambda words: pltpu.bitcast(words, BF16)
        for src, dst_lo, dst_hi in ((blk[:, :KV_WIDTH], klo, khi), (blk[:, KV_WIDTH:], vlo, vhi)):
            w = pltpu.bitcast(src, U32)
            r = pltpu.roll(w, HEAD_DIM, axis=1)
            zero = jnp.zeros_like(w)
            dst_lo[0, rows, :] = as_bf16(jnp.where(lo_lanes, w, zero))
            dst_hi[0, rows, :] = as_bf16(jnp.where(lo_lanes, zero, r))
            dst_lo[1, rows, :] = as_bf16(jnp.where(lo_lanes, r, zero))
            dst_hi[1, rows, :] = as_bf16(jnp.where(lo_lanes, zero, w))

    layout_block(0, kvm_ref[...])

    def layout_body(n, carry):
        layout_block(n + 1, kv_ref[pl.ds(pl.multiple_of(n * BLOCK, BLOCK), BLOCK), :])
        return carry

    lax.fori_loop(0, nbx, layout_body, 0, unroll=4)

    ones_lo = jnp.where(lax.broadcasted_iota(I32, (2 * BLOCK, LANES), 1) < HEAD_DIM,
                        1.0, 0.0).astype(BF16)
    ones_hi = (1.0 - ones_lo.astype(F32)).astype(BF16)
    top_rows = lax.broadcasted_iota(I32, (2 * BLOCK, 1), 0) < BLOCK
    lo_half = lax.broadcasted_iota(I32, (2 * BLOCK, LANES), 1) < HEAD_DIM

    def block(i, carry):
        q_rows = pl.ds(pl.multiple_of(i * BLOCK, BLOCK), BLOCK)
        win = pl.ds(pl.multiple_of(i * BLOCK, BLOCK), 2 * BLOCK)
        q = q_ref[q_rows, :]
        first = jnp.minimum(i, 1)
        outs = []
        for j in range(N_KV_HEADS):
            q2 = jnp.concatenate([q[:, (2 * j) * LANES:(2 * j + 1) * LANES],
                                  q[:, (2 * j + 1) * LANES:(2 * j + 2) * LANES]], axis=0)
            kc = jnp.concatenate([klo[j, win, :], khi[j, win, :]], axis=0)
            s = lax.dot_general(q2, kc, (((1,), (1,)), ((), ())), preferred_element_type=F32)
            s = s + bias_ref[first, j]
            ps, es = [], []
            for c in range(2):
                sink = jnp.where(top_rows, sinks_ref[4 * j + c], sinks_ref[4 * j + 2 + c])
                sc = s[:, c * 2 * BLOCK:(c + 1) * 2 * BLOCK]
                m = jnp.maximum(jnp.max(sc, axis=1, keepdims=True), sink)
                ps.append(jnp.exp2(sc - m).astype(BF16))
                es.append(jnp.exp2(sink - m))
            v_lo = jnp.concatenate([vlo[j, win, :], ones_lo], axis=1)
            v_hi = jnp.concatenate([vhi[j, win, :], ones_hi], axis=1)
            r = jnp.dot(jnp.concatenate(ps, axis=1), jnp.concatenate([v_lo, v_hi], axis=0),
                        preferred_element_type=F32)
            den = r[:, LANES:] + jnp.where(lo_half, es[0], es[1])
            o2 = r[:, :LANES] * (1.0 / den)
            outs += [o2[:BLOCK], o2[BLOCK:]]
        out = jnp.concatenate(outs, axis=1)
        ms = jnp.mean(out * out, axis=1, keepdims=True)
        o_ref[q_rows, :] = (out * lax.rsqrt(ms + EPS) * g_ref[...]).astype(o_ref.dtype)
        return carry

    lax.fori_loop(0, nbx, block, 0, unroll=8)


def _attention(sinks, q, kv, kvm, bias, g_attn, bsz, nbx):
    seq = nbx * BLOCK
    const2 = lambda b: (0, 0)
    kv_scratch = pltpu.VMEM((N_KV_HEADS, seq + BLOCK, LANES), BF16)
    return pl.pallas_call(
        _attn_kernel,
        grid=(bsz,),
        in_specs=[pl.BlockSpec(memory_space=pltpu.SMEM),
                  pl.BlockSpec((None, seq, ATTN_WIDTH), lambda b: (b, 0, 0)),
                  pl.BlockSpec((None, seq, 2 * KV_WIDTH), lambda b: (b, 0, 0)),
                  pl.BlockSpec((BLOCK, 2 * KV_WIDTH), const2),
                  pl.BlockSpec((2, N_KV_HEADS, 2 * BLOCK, 4 * BLOCK), lambda b: (0, 0, 0, 0)),
                  pl.BlockSpec((1, ATTN_WIDTH), const2)],
        out_specs=pl.BlockSpec((None, seq, ATTN_WIDTH), lambda b: (b, 0, 0)),
        out_shape=jax.ShapeDtypeStruct((bsz, seq, ATTN_WIDTH), BF16),
        scratch_shapes=[kv_scratch, kv_scratch, kv_scratch, kv_scratch],
        compiler_params=_cparams(1),
        name="attention",
    )(sinks, q, kv, kvm, bias, g_attn)


def _sigmoid(v):
    return 0.5 * jnp.tanh(0.5 * v) + 0.5


def _gelu_tanh(y):
    c = float(np.sqrt(2.0 / np.pi))
    half = 0.5 * y
    return half + half * jnp.tanh(y * (c + (c * 0.044715) * (y * y)))


LRU_CHUNK = 44
LRU_SEG = SUBLANES * LRU_CHUNK
LRU_SLABS = LRU_WIDTH // LANES


def _lru_kernel(xr_ref, yr_ref, xrm_ref, yrm_ref, cw_ref, cb_ref, wa_ref, wx_ref, ba_ref,
                bx_ref, lam_ref, g_ref, o_ref, x_st, y_st, o_st, s_st, xtail, hcar):
    seq = xr_ref.shape[0]
    n_seg = (seq + BLOCK) // LRU_SEG
    xtail[...] = jnp.zeros_like(xtail)
    hcar[...] = jnp.zeros_like(hcar)
    lam = lam_ref[...]
    softplus_neg = jnp.maximum(-lam, 0.0) + jnp.log(1.0 + jnp.exp(-jnp.abs(lam)))
    sub = lax.broadcasted_iota(jnp.int32, (SUBLANES, LANES), 0)

    def strided(j):
        return pl.ds(j, SUBLANES, stride=LRU_CHUNK)

    def piece(v, j):
        return v[j * SUBLANES:(j + 1) * SUBLANES, :]

    def segment(k, first):
        if first:
            head = LRU_SEG - BLOCK
            x_nat = jnp.concatenate([xrm_ref[...], xr_ref[0:head, :]], axis=0).astype(F32)
            y_nat = jnp.concatenate([yrm_ref[...], yr_ref[0:head, :]], axis=0).astype(F32)
        else:
            rows = pl.ds(pl.multiple_of(k * LRU_SEG - BLOCK, 2 * SUBLANES), LRU_SEG)
            x_nat = xr_ref[rows, :].astype(F32)
            y_nat = yr_ref[rows, :].astype(F32)
        for c in range(LRU_SLABS):
            x_st[c] = x_nat[:, c * LANES:(c + 1) * LANES]
            y_st[c] = y_nat[:, c * LANES:(c + 1) * LANES]
        first_row = k * LRU_SEG + LRU_CHUNK * sub
        sumsq = [jnp.zeros((SUBLANES, LANES), F32) for _ in range(LRU_CHUNK)]

        for c in range(LRU_SLABS):
            lanes = slice(c * LANES, (c + 1) * LANES)
            x = [x_st[c, strided(j), :] for j in range(LRU_CHUNK)]
            before = []
            for d in range(1, CONV_W):
                from_prev_chunk = pltpu.roll(x[LRU_CHUNK - d], 1, axis=0)
                before.append(jnp.where(sub == 0, xtail[d - 1:d, lanes], from_prev_chunk))
            for d in range(1, CONV_W):
                xtail[d - 1:d, lanes] = x[LRU_CHUNK - d][SUBLANES - 1:SUBLANES, :]

            def x_at(j):
                return x[j] if j >= 0 else before[-j - 1]

            taps = [cw_ref[t:t + 1, lanes] for t in range(CONV_W)]
            bias = cb_ref[:, lanes]
            xc = jnp.concatenate(
                [bias + sum(taps[t] * x_at(j - (CONV_W - 1) + t) for t in range(CONV_W))
                 for j in range(LRU_CHUNK)], axis=0)
            xcb = xc.astype(BF16)
            both = jnp.dot(xcb, jnp.concatenate([wa_ref[c], wx_ref[c]], axis=1),
                           preferred_element_type=F32)
            tr = jnp.tanh(both[:, :LANES] + ba_ref[:, lanes])
            ti = jnp.tanh(both[:, LANES:] + bx_ref[:, lanes])
            log_a_half = (-0.5 * LRU_C) * softplus_neg[:, lanes]
            a = jnp.exp(log_a_half * tr + log_a_half)
            half_xc = 0.5 * xc
            gated_x = half_xc * ti + half_xc
            z = 1.0 - a * a
            u = jnp.where(z > 0.0, z * lax.rsqrt(z), 0.0) * gated_x

            h = jnp.zeros((SUBLANES, LANES), F32)
            p = jnp.ones((SUBLANES, LANES), F32)
            hs, ps = [], []
            for j in range(LRU_CHUNK):
                aj = piece(a, j)
                uj = piece(u, j)
                if first:
                    uj = jnp.where(first_row + j >= PAD_FRONT, uj, 0.0)
                h = aj * h + uj
                p = aj * p
                hs.append(h)
                ps.append(p)
            entry = [hcar[:, lanes]]
            for s in range(SUBLANES):
                entry.append(h[s:s + 1, :] + p[s:s + 1, :] * entry[s])
            hcar[:, lanes] = entry[SUBLANES]
            entry_rows = jnp.concatenate(entry[:SUBLANES], axis=0)

            for j in range(LRU_CHUNK):
                state = hs[j] + ps[j] * entry_rows
                out = state * _gelu_tanh(y_st[c, strided(j), :])
                sumsq[j] = sumsq[j] + out * out
                o_st[c, strided(j), :] = out

        for j in range(LRU_CHUNK):
            ms = jnp.sum(sumsq[j], axis=1, keepdims=True) * (1.0 / LRU_WIDTH)
            s_st[strided(j), :] = jnp.broadcast_to(lax.rsqrt(ms + EPS), (SUBLANES, LANES))
        scale = s_st[...]
        for c in range(LRU_SLABS):
            lanes = slice(c * LANES, (c + 1) * LANES)
            normed = (o_st[c] * scale * g_ref[:, lanes]).astype(o_ref.dtype)
            if first:
                o_ref[0:LRU_SEG - BLOCK, lanes] = normed[BLOCK:, :]
            else:
                o_ref[rows, lanes] = normed

    assert BLOCK <= LRU_SEG
    segment(0, True)

    def later_segment(k, carry):
        segment(k, False)
        return carry

    lax.fori_loop(1, n_seg, later_segment, 0)


def _rglru(xr, yr, xrm, yrm, cw, cb, wa, wx, ba, bx, lam, g_lru, bsz, nbx):
    seq = nbx * BLOCK
    assert (seq + BLOCK) % LRU_SEG == 0
    main = pl.BlockSpec((None, seq, LRU_WIDTH), lambda b: (b, 0, 0))
    const2 = lambda b: (0, 0)
    row_spec = pl.BlockSpec((1, LRU_WIDTH), const2)
    gate_spec = pl.BlockSpec((LRU_SLABS, LANES, LANES), lambda b: (0, 0, 0))
    slabs = pltpu.VMEM((LRU_SLABS, LRU_SEG, LANES), F32)
    return pl.pallas_call(
        _lru_kernel,
        grid=(bsz,),
        in_specs=[main, main,
                  pl.BlockSpec((BLOCK, LRU_WIDTH), const2),
                  pl.BlockSpec((BLOCK, LRU_WIDTH), const2),
                  pl.BlockSpec((CONV_W, LRU_WIDTH), const2),
                  row_spec, gate_spec, gate_spec,
                  row_spec, row_spec, row_spec, row_spec],
        out_specs=main,
        out_shape=jax.ShapeDtypeStruct((bsz, seq, LRU_WIDTH), BF16),
        scratch_shapes=[slabs, slabs, slabs,
                        pltpu.VMEM((LRU_SEG, LANES), F32),
                        pltpu.VMEM((SUBLANES, LRU_WIDTH), F32),
                        pltpu.VMEM((1, LRU_WIDTH), F32)],
        compiler_params=_cparams(1),
        name="rglru",
    )(xr, yr, xrm, yrm, cw, cb, wa, wx, ba, bx, lam, g_lru)


def _pack_rows(v):
    bits = lax.bitcast_convert_type(v.astype(BF16).astype(F32), U32)
    return (bits[:, :PACKED] >> 16) | (bits[:, PACKED:] & jnp.uint32(0xFFFF0000))


def _unpack_rows(w):
    lo = lax.bitcast_convert_type(w << 16, F32)
    hi = lax.bitcast_convert_type(w & jnp.uint32(0xFFFF0000), F32)
    return lo, hi


def _layer_norm(z, g, b):
    mu = jnp.mean(z, axis=1, keepdims=True)
    zc = z - mu
    var = jnp.mean(zc * zc, axis=1, keepdims=True)
    return zc * lax.rsqrt(var + EPS) * g + b


def _out_proj_kernel(a_ref, l_ref, x_ref, wa_ref, wl_ref, g_ref, b_ref, wrt_hi_ref, wrt_lo_ref,
                     brt_ref, h_ref, hp_ref, info_ref, cnt_ref):
    @pl.when(pl.program_id(0) == 0)
    def _():
        cnt_ref[...] = jnp.zeros_like(cnt_ref)

    _out_proj_rows(a_ref[...], l_ref[...], x_ref[...], wa_ref, wl_ref, g_ref, b_ref,
                   wrt_hi_ref, wrt_lo_ref, brt_ref, h_ref, hp_ref, info_ref, cnt_ref)


def _out_proj_rows(a, l, x, wa_ref, wl_ref, g_ref, b_ref, wrt_hi_ref, wrt_lo_ref, brt_ref,
                   h_ref, hp_ref, info_ref, cnt_ref):
    mix = jnp.dot(jnp.concatenate([a, l], axis=1),
                  jnp.concatenate([wa_ref[...], wl_ref[...]], axis=0),
                  preferred_element_type=F32)
    h = _layer_norm(ALPHA * x + mix, g_ref[...], b_ref[...])
    h_ref[...] = h
    hp_ref[...] = _pack_rows(h)

    h_hi = h.astype(BF16)
    h_lo = (h - h_hi.astype(F32)).astype(BF16)
    nt = (((1,), (1,)), ((), ()))
    w_both = jnp.concatenate([wrt_hi_ref[...], wrt_lo_ref[...]], axis=0)
    both = lax.dot_general(w_both, h_hi, nt, preferred_element_type=F32)
    lg = (both[:ROUTER_ROWS] + both[ROUTER_ROWS:]
          + lax.dot_general(wrt_hi_ref[...], h_lo, nt, preferred_element_type=F32)) + brt_ref[...]
    tile_shape = (SUBLANES, h.shape[0])
    sub = lax.broadcasted_iota(I32, tile_shape, 0)
    ninf = -jnp.inf
    t0 = lg[0:SUBLANES]
    gl = jnp.where(sub < N_GROUPS, t0, ninf)
    gmax = jnp.max(gl, axis=0, keepdims=True)
    g_idx = jnp.min(jnp.where(gl == gmax, sub, SUBLANES), axis=0, keepdims=True)
    g_w = 1.0 / jnp.sum(jnp.where(sub < N_GROUPS, jnp.exp(t0 - gmax), 0.0),
                        axis=0, keepdims=True)
    el = lg[SUBLANES:2 * SUBLANES]
    for g in range(1, N_GROUPS):
        el = jnp.where(g_idx == g, lg[(g + 1) * SUBLANES:(g + 2) * SUBLANES], el)
    v1 = jnp.max(el, axis=0, keepdims=True)
    i1 = jnp.min(jnp.where(el == v1, sub, SUBLANES), axis=0, keepdims=True)
    el2 = jnp.where(sub == i1, ninf, el)
    v2 = jnp.max(el2, axis=0, keepdims=True)
    i2 = jnp.min(jnp.where(el2 == v2, sub, SUBLANES), axis=0, keepdims=True)
    t = jnp.exp(v2 - v1)
    w1 = 1.0 / (1.0 + t)
    w2 = t * w1
    e_base = g_idx * EXPERTS_PER_GROUP
    e1 = e_base + i1
    e2 = e_base + i2
    info_ref[...] = jnp.where(sub == 0, e1.astype(F32),
                              jnp.where(sub == 1, e2.astype(F32),
                                        jnp.where(sub == 2, g_w * w1,
                                                  jnp.where(sub == 3, g_w * w2, 0.0))))
    expert = lax.broadcasted_iota(I32, (N_EXPERTS, h.shape[0]), 0)
    chosen = (expert == e1).astype(F32) + (expert == e2).astype(F32)
    cnt_ref[...] += jnp.sum(chosen, axis=1, keepdims=True)


ROUTER_ROWS = (N_GROUPS + 1) * SUBLANES


def _out_proj(attn_n, lru_n, x2d, wo_a, wo_l, ln_g, ln_b, w_rt_hi, w_rt_lo, b_rt, rows):
    n = x2d.shape[0]
    const = lambda i: (0, 0)
    tile = lambda w: pl.BlockSpec((rows, w), lambda i: (i, 0))
    return pl.pallas_call(
        _out_proj_kernel,
        grid=(n // rows,),
        in_specs=[tile(ATTN_WIDTH), tile(LRU_WIDTH), tile(D_MODEL),
                  pl.BlockSpec((ATTN_WIDTH, D_MODEL), const),
                  pl.BlockSpec((LRU_WIDTH, D_MODEL), const),
                  pl.BlockSpec((1, D_MODEL), const),
                  pl.BlockSpec((1, D_MODEL), const),
                  pl.BlockSpec((ROUTER_ROWS, D_MODEL), const),
                  pl.BlockSpec((ROUTER_ROWS, D_MODEL), const),
                  pl.BlockSpec((ROUTER_ROWS, 1), const)],
        out_specs=[tile(D_MODEL), tile(PACKED),
                   pl.BlockSpec((SUBLANES, rows), lambda i: (0, i)),
                   pl.BlockSpec((N_EXPERTS, LANES), const)],
        out_shape=[jax.ShapeDtypeStruct((n, D_MODEL), F32),
                   jax.ShapeDtypeStruct((n, PACKED), U32),
                   jax.ShapeDtypeStruct((SUBLANES, n), F32),
                   jax.ShapeDtypeStruct((N_EXPERTS, LANES), F32)],
        compiler_params=_cparams(1),
        name="out_proj",
    )(attn_n, lru_n, x2d, wo_a, wo_l, ln_g, ln_b, w_rt_hi, w_rt_lo, b_rt)


def _route_kernel(info_ref, cnt_ref, tri_ref, dest_ref, carry, pstart):
    t = pl.program_id(0)
    info = info_ref[...]
    shape = (N_EXPERTS, info.shape[1])
    expert = lax.broadcasted_iota(I32, shape, 0)
    oh1 = (expert == info[0:1, :].astype(I32)).astype(F32)
    oh2 = (expert == info[1:2, :].astype(I32)).astype(F32)
    both = oh1 + oh2

    @pl.when(t == 0)
    def _():
        c = cnt_ref[...].astype(I32)
        padded = ((c + (MOE_BLOCK - 1)) // MOE_BLOCK) * MOE_BLOCK
        e = lax.broadcasted_iota(I32, (N_EXPERTS, LANES), 0)
        scan = padded
        for d in (1, 2, 4, 8, 16):
            scan = scan + jnp.where(e >= d, pltpu.roll(scan, d, axis=0), 0)
        pstart[...] = (scan - padded)[:, 0:1].astype(F32)
        carry[...] = jnp.zeros_like(carry)

    before = jnp.dot(both.astype(BF16), tri_ref[...], preferred_element_type=F32)
    row_of = before + (carry[...] + pstart[...])
    r1 = jnp.sum(oh1 * row_of, axis=0, keepdims=True)
    r2 = jnp.sum(oh2 * row_of, axis=0, keepdims=True)
    sub = lax.broadcasted_iota(I32, dest_ref.shape, 0)
    dest_ref[...] = jnp.where(sub == 0, r1, jnp.where(sub == 1, r2, 0.0)).astype(I32)
    carry[...] += jnp.sum(both, axis=1, keepdims=True)


def _route(info_t, cnt, cols):
    n = info_t.shape[1]
    tri = jnp.asarray(np.triu(np.ones((cols, cols), np.float32), 1), BF16)
    return pl.pallas_call(
        _route_kernel,
        grid=(n // cols,),
        in_specs=[pl.BlockSpec((SUBLANES, cols), lambda t: (0, t)),
                  pl.BlockSpec((N_EXPERTS, LANES), lambda t: (0, 0)),
                  pl.BlockSpec((cols, cols), lambda t: (0, 0))],
        out_specs=pl.BlockSpec((SUBLANES, cols), lambda t: (0, t)),
        out_shape=jax.ShapeDtypeStruct((SUBLANES, n), I32),
        scratch_shapes=[pltpu.VMEM((N_EXPERTS, 1), F32), pltpu.VMEM((N_EXPERTS, 1), F32)],
        compiler_params=_cparams(1),
        name="route",
    )(info_t, cnt, tri)


def _sc_mesh():
    return plsc.VectorSubcoreMesh(core_axis_name="core", subcore_axis_name="subcore")


def _sc_worker_id():
    return lax.axis_index("subcore") * SC_CORES + lax.axis_index("core")


def _sc_scatter_rows(rows, d0, d1, cap):
    n, width = rows.shape
    per_worker = n // SC_WORKERS
    n_win = per_worker // SC_WINDOW

    def body(x_hbm, d0_hbm, d1_hbm, o_hbm, i0_v, i1_v, rows_v, rsem, sem0, sem1):
        wid = _sc_worker_id()
        pltpu.sync_copy(d0_hbm.at[wid], i0_v)
        pltpu.sync_copy(d1_hbm.at[wid], i1_v)

        def read(j):
            src = x_hbm.at[pl.ds(wid * per_worker + j * SC_WINDOW, SC_WINDOW)]
            return pltpu.make_async_copy(src, rows_v.at[j % 2], rsem.at[j % 2])

        def scatters(j):
            return (pltpu.make_async_copy(rows_v.at[j % 2], o_hbm.at[i0_v.at[j]], sem0.at[j % 2]),
                    pltpu.make_async_copy(rows_v.at[j % 2], o_hbm.at[i1_v.at[j]], sem1.at[j % 2]))

        read(0).start()
        for j in range(n_win):
            if j + 1 < n_win:
                if j >= 1:
                    for cp in scatters(j - 1):
                        cp.wait()
                read(j + 1).start()
            read(j).wait()
            for cp in scatters(j):
                cp.start()
        for j in range(max(n_win - 2, 0), n_win):
            for cp in scatters(j):
                cp.wait()

    return pl.kernel(
        body,
        out_type=jax.ShapeDtypeStruct((cap, width), rows.dtype),
        mesh=_sc_mesh(),
        scratch_types=[pltpu.VMEM((n_win, SC_WINDOW), I32), pltpu.VMEM((n_win, SC_WINDOW), I32),
                       pltpu.VMEM((2, SC_WINDOW, width), rows.dtype),
                       pltpu.SemaphoreType.DMA((2,)), pltpu.SemaphoreType.DMA((2,)),
                       pltpu.SemaphoreType.DMA((2,))],
        name="dispatch",
    )(rows, d0, d1)


def _sc_gather_rows(table, idx):
    width = table.shape[1]
    n_win = idx.shape[1]
    per_worker = n_win * SC_WINDOW

    def body(y_hbm, i_hbm, o_hbm, i_v, rows_v, gsem, wsem):
        wid = _sc_worker_id()
        pltpu.sync_copy(i_hbm.at[wid], i_v)

        def gather(j):
            return pltpu.make_async_copy(y_hbm.at[i_v.at[j]], rows_v.at[j % 2], gsem.at[j % 2])

        def write(j):
            dst = o_hbm.at[pl.ds(wid * per_worker + j * SC_WINDOW, SC_WINDOW)]
            return pltpu.make_async_copy(rows_v.at[j % 2], dst, wsem.at[j % 2])

        gather(0).start()
        for j in range(n_win):
            if j + 1 < n_win:
                if j >= 1:
                    write(j - 1).wait()
                gather(j + 1).start()
            gather(j).wait()
            write(j).start()
        for j in range(max(n_win - 2, 0), n_win):
            write(j).wait()

    return pl.kernel(
        body,
        out_type=jax.ShapeDtypeStruct((SC_WORKERS * per_worker, width), table.dtype),
        mesh=_sc_mesh(),
        scratch_types=[pltpu.VMEM((n_win, SC_WINDOW), I32),
                       pltpu.VMEM((2, SC_WINDOW, width), table.dtype),
                       pltpu.SemaphoreType.DMA((2,)), pltpu.SemaphoreType.DMA((2,))],
        name="collect",
    )(table, idx)


def _expert_kernel(bstart_ref, nblk_ref, nused_ref, xs_hbm, wg_ref, wu_ref, wd_ref, yb_hbm,
                   xbuf, ybuf, zbuf, xsem, ysem, zsem, wg_b, wu_b, wd_b):
    e = pl.program_id(0)
    nused = nused_ref[0]
    n_blocks = yb_hbm.shape[0] // MOE_BLOCK

    def rows(b):
        return pl.ds(pl.multiple_of(b * MOE_BLOCK, MOE_BLOCK), MOE_BLOCK)

    def x_copy(b):
        slot = b % X_RING
        return pltpu.make_async_copy(xs_hbm.at[rows(b)], xbuf.at[slot], xsem.at[slot])

    def y_copy(b):
        slot = b % Y_RING
        return pltpu.make_async_copy(ybuf.at[slot], yb_hbm.at[rows(b)], ysem.at[slot])

    @pl.when(e == 0)
    def _():
        for b in range(X_AHEAD):
            @pl.when(b < nused)
            def _():
                x_copy(b).start()

    wg_b[...] = wg_ref[...].astype(BF16)
    wu_b[...] = wu_ref[...].astype(BF16)
    wd_b[...] = wd_ref[...].astype(BF16)

    def run_blocks(b, count):
        for k in range(count):
            x_copy(b + k).wait()
        for k in range(count):
            nxt = b + X_AHEAD + k

            @pl.when(nxt < nused)
            def _():
                x_copy(nxt).start()

            @pl.when(b + k >= Y_RING)
            def _():
                y_copy(b + k - Y_RING).wait()

        words = jnp.concatenate([xbuf[(b + k) % X_RING] for k in range(count)], axis=0)
        lo, hi = _unpack_rows(words)
        x = jnp.concatenate([lo.astype(BF16), hi.astype(BF16)], axis=1)
        g = jnp.dot(x, wg_b[...], preferred_element_type=F32)
        u = jnp.dot(x, wu_b[...], preferred_element_type=F32)
        mid = (g * _sigmoid(g) * u).astype(BF16)
        y = _pack_rows(jnp.dot(mid, wd_b[...], preferred_element_type=F32))
        for k in range(count):
            ybuf[(b + k) % Y_RING] = y[k * MOE_BLOCK:(k + 1) * MOE_BLOCK]
            y_copy(b + k).start()

    b0 = bstart_ref[e]
    nb = nblk_ref[e]

    def group(i, carry):
        run_blocks(b0 + X_GROUP * i, X_GROUP)
        return carry

    lax.fori_loop(0, nb // X_GROUP, group, 0)
    done = nb - nb % X_GROUP
    size = X_GROUP // 2
    while size >= 1:
        @pl.when((nb // size) % 2 == 1)
        def _(size=size, done=done):
            run_blocks(b0 + done, size)

        done = done + (nb // size) % 2 * size
        size //= 2

    @pl.when(e == pl.num_programs(0) - 1)
    def _():
        for back in range(Y_RING, 0, -1):
            @pl.when(nused >= back)
            def _():
                y_copy(nused - back).wait()

        zbuf[...] = jnp.zeros_like(zbuf)

        def z_copy(b):
            return pltpu.make_async_copy(zbuf, yb_hbm.at[rows(b)], zsem.at[0])

        def z_start(b, carry):
            z_copy(b).start()
            return carry

        def z_wait(b, carry):
            z_copy(b).wait()
            return carry

        lax.fori_loop(nused, n_blocks, z_start, 0)
        lax.fori_loop(nused, n_blocks, z_wait, 0)


def _experts(bstart, nblk, nused, xs, w_gate, w_up, w_down):
    cap = xs.shape[0]
    w_idx = lambda e, bs, nb, nu: (e, 0, 0)
    grid_spec = pltpu.PrefetchScalarGridSpec(
        num_scalar_prefetch=3,
        grid=(N_EXPERTS,),
        in_specs=[pl.BlockSpec(memory_space=pl.ANY),
                  pl.BlockSpec((None, D_MODEL, D_FF), w_idx),
                  pl.BlockSpec((None, D_MODEL, D_FF), w_idx),
                  pl.BlockSpec((None, D_FF, D_MODEL), w_idx)],
        out_specs=pl.BlockSpec(memory_space=pl.ANY),
        scratch_shapes=[pltpu.VMEM((X_RING, MOE_BLOCK, PACKED), U32),
                        pltpu.VMEM((Y_RING, MOE_BLOCK, PACKED), U32),
                        pltpu.VMEM((MOE_BLOCK, PACKED), U32),
                        pltpu.SemaphoreType.DMA((X_RING,)),
                        pltpu.SemaphoreType.DMA((Y_RING,)),
                        pltpu.SemaphoreType.DMA((1,)),
                        pltpu.VMEM((D_MODEL, D_FF), BF16),
                        pltpu.VMEM((D_MODEL, D_FF), BF16),
                        pltpu.VMEM((D_FF, D_MODEL), BF16)])
    return pl.pallas_call(
        _expert_kernel,
        grid_spec=grid_spec,
        out_shape=jax.ShapeDtypeStruct((cap, PACKED), U32),
        compiler_params=_cparams(1),
        name="experts",
    )(bstart, nblk, nused, xs, w_gate, w_up, w_down)


def _combine_kernel(y0_ref, y1_ref, h_ref, info_ref, g_ref, b_ref, o_ref):
    info = info_ref[...].T
    g0 = info[:, 2:3]
    g1 = info[:, 3:4]
    lo0, hi0 = _unpack_rows(y0_ref[...])
    lo1, hi1 = _unpack_rows(y1_ref[...])
    y = jnp.concatenate([g0 * lo0 + g1 * lo1, g0 * hi0 + g1 * hi1], axis=1)
    o_ref[...] = _layer_norm(ALPHA * h_ref[...] + y, g_ref[...], b_ref[...])


def _combine(ys, h, info, ln_g, ln_b, rows, part, n_parts):
    n = h.shape[0]
    steps = n // n_parts // rows
    off = part * steps
    const = lambda i: (0, 0)
    return pl.pallas_call(
        _combine_kernel,
        grid=(steps,),
        in_specs=[pl.BlockSpec((rows, PACKED), lambda i: (i, 0)),
                  pl.BlockSpec((rows, PACKED), lambda i: (i + steps, 0)),
                  pl.BlockSpec((rows, D_MODEL), lambda i: (i + off, 0)),
                  pl.BlockSpec((SUBLANES, rows), lambda i: (0, i + off)),
                  pl.BlockSpec((1, D_MODEL), const),
                  pl.BlockSpec((1, D_MODEL), const)],
        out_specs=pl.BlockSpec((rows, D_MODEL), lambda i: (i + off, 0)),
        out_shape=jax.ShapeDtypeStruct((n, D_MODEL), F32),
        input_output_aliases={2: 0},
        compiler_params=_cparams(1),
        name="combine",
    )(ys, ys, h, info, ln_g, ln_b)


def _alibi_bias():
    qi = np.arange(BLOCK)[:, None]
    kj = np.arange(2 * BLOCK)[None, :]
    dist = qi - kj + BLOCK
    band = (dist >= 0) & (dist < BLOCK)
    slopes = np.exp2(-8.0 * np.arange(1, N_Q_HEADS + 1, dtype=np.float32) / N_Q_HEADS)
    bias = np.where(band[None], -slopes[:, None, None] * dist[None].astype(np.float32), NEG)
    bias = bias * LOG2E
    first = np.where((kj >= PAD_FRONT)[None], bias, NEG)
    out = np.empty((2, N_KV_HEADS, 2 * BLOCK, 4 * BLOCK), np.float32)
    for v, per_head in enumerate((first, bias)):
        for j in range(N_KV_HEADS):
            out[v, j] = np.block([[per_head[4 * j], per_head[4 * j + 1]],
                                  [per_head[4 * j + 2], per_head[4 * j + 3]]])
    return jnp.asarray(out, F32)


def _slab_gates(w):
    nb, c, _ = w.shape
    per = LANES // c
    w = w.reshape(nb // per, per, c, c)
    eye = jnp.eye(per, dtype=w.dtype)
    return jnp.einsum('spcd,pq->spcqd', w, eye).reshape(nb // per, LANES, LANES).astype(BF16)


def kernel(x, meta_tokens, w_in, conv_w, conv_b, lru_wa, lru_ba, lru_wx, lru_bx, lru_lambda,
           attn_sinks, g_attn, g_lru, w_out, ln1_g, ln1_b, w_group, b_group, w_router,
           b_router, w_gate, w_up, w_down, ln2_g, ln2_b):
    bsz, seq, d = x.shape
    nbx = seq // BLOCK
    n_tok = bsz * seq
    x2d = x.reshape(n_tok, d)
    row = lambda v: v.reshape(1, -1).astype(F32)

    q_scale = jnp.concatenate([jnp.full((ATTN_WIDTH,), LOG2E * HEAD_DIM ** -0.5, F32),
                               jnp.ones((IN_COLS - ATTN_WIDTH,), F32)])
    w_in_b = (w_in[0] * q_scale).astype(BF16)
    meta_blk = jnp.concatenate([jnp.zeros((PAD_FRONT, d), F32), meta_tokens.astype(F32)], axis=0)
    q, kv, xr, yr = _in_proj(x2d, w_in_b, PROJ_ROWS)
    qm, kvm, xrm, yrm = _in_proj(meta_blk, w_in_b, BLOCK)
    shp = lambda a: a.reshape(bsz, seq, a.shape[-1])

    attn_n = _attention(attn_sinks[0].astype(F32) * LOG2E, shp(q), shp(kv), kvm, _alibi_bias(),
                        row(g_attn[0]), bsz, nbx)
    lru_n = _rglru(shp(xr), shp(yr), xrm, yrm, conv_w[0].astype(F32), row(conv_b[0]),
                   _slab_gates(0.5 * lru_wa[0]), _slab_gates(0.5 * lru_wx[0]),
                   0.5 * row(lru_ba[0]), 0.5 * row(lru_bx[0]), row(lru_lambda[0]),
                   row(g_lru[0]), bsz, nbx)

    w_out_b = w_out[0].astype(BF16)
    gpad = SUBLANES - N_GROUPS
    w_rt = jnp.concatenate(
        [w_group[0].T, jnp.zeros((gpad, d), F32),
         jnp.transpose(w_router[0], (0, 2, 1)).reshape(N_EXPERTS, d)], axis=0).astype(F32)
    w_rt_hi = w_rt.astype(BF16)
    w_rt_lo = (w_rt - w_rt_hi.astype(F32)).astype(BF16)
    b_rt = jnp.concatenate([b_group[0], jnp.zeros((gpad,), F32),
                            b_router[0].reshape(-1)]).astype(F32).reshape(ROUTER_ROWS, 1)
    h1, hp, info, cnt = _out_proj(
        attn_n.reshape(n_tok, ATTN_WIDTH), lru_n.reshape(n_tok, LRU_WIDTH), x2d,
        w_out_b[:ATTN_WIDTH], w_out_b[ATTN_WIDTH:], row(ln1_g[0]), row(ln1_b[0]),
        w_rt_hi, w_rt_lo, b_rt, OUT_PROJ_ROWS)

    dest = _route(info, cnt, ROUTE_ROWS)
    n_slots = n_tok * TOP_K
    n_blocks = n_slots // MOE_BLOCK + N_EXPERTS
    cap = n_blocks * MOE_BLOCK
    nblk = (cnt[:, 0].astype(I32) + MOE_BLOCK - 1) // MOE_BLOCK
    bends = jnp.cumsum(nblk)
    bstart = (bends - nblk).astype(I32)
    nused = bends[-1:].astype(I32)
    windows = lambda v: v.reshape(SC_WORKERS, -1, SC_WINDOW)
    d0 = dest[0]
    d1 = dest[1]

    xs = _sc_scatter_rows(hp, windows(d0), windows(d1), cap)
    yb = _experts(bstart, nblk.astype(I32), nused, xs, w_gate[0], w_up[0], w_down[0])
    out = h1
    part_len = n_tok // COMBINE_PARTS
    for part in range(COMBINE_PARTS):
        tok = slice(part * part_len, (part + 1) * part_len)
        ys = _sc_gather_rows(yb, windows(jnp.concatenate([d0[tok], d1[tok]])))
        out = _combine(ys, out, info, row(ln2_g[0]), row(ln2_b[0]), COMBINE_ROWS,
                       part, COMBINE_PARTS)
    return out.reshape(bsz, seq, d)
```

```python
import jax
import jax.numpy as jnp
import numpy as np
from jax import lax
from jax.experimental import pallas as pl
from jax.experimental.pallas import tpu as pltpu
from jax.experimental.pallas import tpu_sc as plsc

F32 = jnp.float32
BF16 = jnp.bfloat16
U32 = jnp.uint32
I32 = jnp.int32

D_MODEL = 1024
N_META = 16
BLOCK = 128
PAD_FRONT = BLOCK - N_META
HEAD_DIM = 64
ATTN_WIDTH = 512
LRU_WIDTH = 512
N_Q_HEADS = 8
N_KV_HEADS = 2
KV_WIDTH = N_KV_HEADS * HEAD_DIM
LRU_BLOCKS = 8
CONV_W = 4
LRU_C = 8.0
IN_COLS = ATTN_WIDTH + 2 * KV_WIDTH + 2 * LRU_WIDTH
N_GROUPS = 4
EXPERTS_PER_GROUP = 8
N_EXPERTS = N_GROUPS * EXPERTS_PER_GROUP
TOP_K = 2
D_FF = 512
MOE_BLOCK = 128
ALPHA = 2.0 ** 0.25
EPS = 1e-5
NEG = -1e30
LOG2E = float(np.log2(np.e))
LANES = 128
SUBLANES = 8
PACKED = D_MODEL // 2

PROJ_ROWS = 1024
OUT_PROJ_ROWS = 1024
ROUTE_ROWS = 512
COMBINE_ROWS = 1024
COMBINE_PARTS = 4
X_GROUP = 8
X_AHEAD = 8
X_RING = X_AHEAD + X_GROUP
Y_RING = 2 * X_GROUP
VMEM_LIMIT = 48 * 1024 * 1024

SC_CORES = 2
SC_SUBCORES = 16
SC_WORKERS = SC_CORES * SC_SUBCORES
SC_WINDOW = 64


def _cparams(n_axes):
    return pltpu.CompilerParams(
        dimension_semantics=("arbitrary",) * n_axes, vmem_limit_bytes=VMEM_LIMIT)


def _in_proj_kernel(x_ref, w_ref, q_ref, kv_ref, xr_ref, yr_ref):
    proj = jnp.dot(x_ref[...].astype(BF16), w_ref[...], preferred_element_type=F32)
    o = 0
    for ref, width in ((q_ref, ATTN_WIDTH), (kv_ref, 2 * KV_WIDTH),
                       (xr_ref, LRU_WIDTH), (yr_ref, LRU_WIDTH)):
        ref[...] = proj[:, o:o + width].astype(ref.dtype)
        o += width


def _in_proj(x2d, w_bf16, rows):
    n = x2d.shape[0]
    widths = (ATTN_WIDTH, 2 * KV_WIDTH, LRU_WIDTH, LRU_WIDTH)
    return pl.pallas_call(
        _in_proj_kernel,
        grid=(n // rows,),
        in_specs=[pl.BlockSpec((rows, D_MODEL), lambda i: (i, 0)),
                  pl.BlockSpec((D_MODEL, IN_COLS), lambda i: (0, 0))],
        out_specs=[pl.BlockSpec((rows, w), lambda i: (i, 0)) for w in widths],
        out_shape=[jax.ShapeDtypeStruct((n, w), BF16) for w in widths],
        compiler_params=_cparams(1),
        name="in_proj",
    )(x2d, w_bf16)


def _attn_kernel(sinks_ref, q_ref, kv_ref, kvm_ref, bias_ref, g_ref, o_ref,
                 klo, khi, vlo, vhi):
    nbx = q_ref.shape[0] // BLOCK
    lo_lanes = lax.broadcasted_iota(I32, (BLOCK // 2, LANES), 1) < HEAD_DIM

    def layout_block(n, blk):
        rows = pl.ds(pl.multiple_of(n * BLOCK, BLOCK), BLOCK)
        as_bf16 = lambda words: pltpu.bitcast(words, BF16)
        for src, dst_lo, dst_hi in ((blk[:, :KV_WIDTH], klo, khi), (blk[:, KV_WIDTH:], vlo, vhi)):
            w = pltpu.bitcast(src, U32)
            r = pltpu.roll(w, HEAD_DIM, axis=1)
            zero = jnp.zeros_like(w)
            dst_lo[0, rows, :] = as_bf16(jnp.where(lo_lanes, w, zero))
            dst_hi[0, rows, :] = as_bf16(jnp.where(lo_lanes, zero, r))
            dst_lo[1, rows, :] = as_bf16(jnp.where(lo_lanes, r, zero))
            dst_hi[1, rows, :] = as_bf16(jnp.where(lo_lanes, zero, w))

    layout_block(0, kvm_ref[...])

    def layout_body(n, carry):
        layout_block(n + 1, kv_ref[pl.ds(pl.multiple_of(n * BLOCK, BLOCK), BLOCK), :])
        return carry

    lax.fori_loop(0, nbx, layout_body, 0, unroll=4)

    ones_lo = jnp.where(lax.broadcasted_iota(I32, (2 * BLOCK, LANES), 1) < HEAD_DIM,
                        1.0, 0.0).astype(BF16)
    ones_hi = (1.0 - ones_lo.astype(F32)).astype(BF16)
    top_rows = lax.broadcasted_iota(I32, (2 * BLOCK, 1), 0) < BLOCK
    lo_half = lax.broadcasted_iota(I32, (2 * BLOCK, LANES), 1) < HEAD_DIM

    def block(i, carry):
        q_rows = pl.ds(pl.multiple_of(i * BLOCK, BLOCK), BLOCK)
        win = pl.ds(pl.multiple_of(i * BLOCK, BLOCK), 2 * BLOCK)
        q = q_ref[q_rows, :]
        first = jnp.minimum(i, 1)
        outs = []
        for j in range(N_KV_HEADS):
            q2 = jnp.concatenate([q[:, (2 * j) * LANES:(2 * j + 1) * LANES],
                                  q[:, (2 * j + 1) * LANES:(2 * j + 2) * LANES]], axis=0)
            kc = jnp.concatenate([klo[j, win, :], khi[j, win, :]], axis=0)
            s = lax.dot_general(q2, kc, (((1,), (1,)), ((), ())), preferred_element_type=F32)
            s = s + bias_ref[first, j]
            ps, es = [], []
            for c in range(2):
                sink = jnp.where(top_rows, sinks_ref[4 * j + c], sinks_ref[4 * j + 2 + c])
                sc = s[:, c * 2 * BLOCK:(c + 1) * 2 * BLOCK]
                m = jnp.maximum(jnp.max(sc, axis=1, keepdims=True), sink)
                ps.append(jnp.exp2(sc - m).astype(BF16))
                es.append(jnp.exp2(sink - m))
            v_lo = jnp.concatenate([vlo[j, win, :], ones_lo], axis=1)
            v_hi = jnp.concatenate([vhi[j, win, :], ones_hi], axis=1)
            r = jnp.dot(jnp.concatenate(ps, axis=1), jnp.concatenate([v_lo, v_hi], axis=0),
                        preferred_element_type=F32)
            den = r[:, LANES:] + jnp.where(lo_half, es[0], es[1])
            o2 = r[:, :LANES] * (1.0 / den)
            outs += [o2[:BLOCK], o2[BLOCK:]]
        out = jnp.concatenate(outs, axis=1)
        ms = jnp.mean(out * out, axis=1, keepdims=True)
        o_ref[q_rows, :] = (out * lax.rsqrt(ms + EPS) * g_ref[...]).astype(o_ref.dtype)
        return carry

    lax.fori_loop(0, nbx, block, 0, unroll=8)


def _attention(sinks, q, kv, kvm, bias, g_attn, bsz, nbx):
    seq = nbx * BLOCK
    const2 = lambda b: (0, 0)
    kv_scratch = pltpu.VMEM((N_KV_HEADS, seq + BLOCK, LANES), BF16)
    return pl.pallas_call(
        _attn_kernel,
        grid=(bsz,),
        in_specs=[pl.BlockSpec(memory_space=pltpu.SMEM),
                  pl.BlockSpec((None, seq, ATTN_WIDTH), lambda b: (b, 0, 0)),
                  pl.BlockSpec((None, seq, 2 * KV_WIDTH), lambda b: (b, 0, 0)),
                  pl.BlockSpec((BLOCK, 2 * KV_WIDTH), const2),
                  pl.BlockSpec((2, N_KV_HEADS, 2 * BLOCK, 4 * BLOCK), lambda b: (0, 0, 0, 0)),
                  pl.BlockSpec((1, ATTN_WIDTH), const2)],
        out_specs=pl.BlockSpec((None, seq, ATTN_WIDTH), lambda b: (b, 0, 0)),
        out_shape=jax.ShapeDtypeStruct((bsz, seq, ATTN_WIDTH), BF16),
        scratch_shapes=[kv_scratch, kv_scratch, kv_scratch, kv_scratch],
        compiler_params=_cparams(1),
        name="attention",
    )(sinks, q, kv, kvm, bias, g_attn)


def _sigmoid(v):
    return 0.5 * jnp.tanh(0.5 * v) + 0.5


def _gelu_tanh(y):
    c = float(np.sqrt(2.0 / np.pi))
    half = 0.5 * y
    return half + half * jnp.tanh(y * (c + (c * 0.044715) * (y * y)))


LRU_CHUNK = 44
LRU_SEG = SUBLANES * LRU_CHUNK
LRU_SLABS = LRU_WIDTH // LANES


def _lru_kernel(xr_ref, yr_ref, xrm_ref, yrm_ref, cw_ref, cb_ref, wa_ref, wx_ref, ba_ref,
                bx_ref, lam_ref, g_ref, o_ref, x_st, y_st, o_st, s_st, xtail, hcar):
    seq = xr_ref.shape[0]
    n_seg = (seq + BLOCK) // LRU_SEG
    xtail[...] = jnp.zeros_like(xtail)
    hcar[...] = jnp.zeros_like(hcar)
    lam = lam_ref[...]
    softplus_neg = jnp.maximum(-lam, 0.0) + jnp.log(1.0 + jnp.exp(-jnp.abs(lam)))
    sub = lax.broadcasted_iota(jnp.int32, (SUBLANES, LANES), 0)

    def strided(j):
        return pl.ds(j, SUBLANES, stride=LRU_CHUNK)

    def piece(v, j):
        return v[j * SUBLANES:(j + 1) * SUBLANES, :]

    def segment(k, first):
        if first:
            head = LRU_SEG - BLOCK
            x_nat = jnp.concatenate([xrm_ref[...], xr_ref[0:head, :]], axis=0).astype(F32)
            y_nat = jnp.concatenate([yrm_ref[...], yr_ref[0:head, :]], axis=0).astype(F32)
        else:
            rows = pl.ds(pl.multiple_of(k * LRU_SEG - BLOCK, 2 * SUBLANES), LRU_SEG)
            x_nat = xr_ref[rows, :].astype(F32)
            y_nat = yr_ref[rows, :].astype(F32)
        for c in range(LRU_SLABS):
            x_st[c] = x_nat[:, c * LANES:(c + 1) * LANES]
            y_st[c] = y_nat[:, c * LANES:(c + 1) * LANES]
        first_row = k * LRU_SEG + LRU_CHUNK * sub
        sumsq = [jnp.zeros((SUBLANES, LANES), F32) for _ in range(LRU_CHUNK)]

        for c in range(LRU_SLABS):
            lanes = slice(c * LANES, (c + 1) * LANES)
            x = [x_st[c, strided(j), :] for j in range(LRU_CHUNK)]
            before = []
            for d in range(1, CONV_W):
                from_prev_chunk = pltpu.roll(x[LRU_CHUNK - d], 1, axis=0)
                before.append(jnp.where(sub == 0, xtail[d - 1:d, lanes], from_prev_chunk))
            for d in range(1, CONV_W):
                xtail[d - 1:d, lanes] = x[LRU_CHUNK - d][SUBLANES - 1:SUBLANES, :]

            def x_at(j):
                return x[j] if j >= 0 else before[-j - 1]

            taps = [cw_ref[t:t + 1, lanes] for t in range(CONV_W)]
            bias = cb_ref[:, lanes]
            xc = jnp.concatenate(
                [bias + sum(taps[t] * x_at(j - (CONV_W - 1) + t) for t in range(CONV_W))
                 for j in range(LRU_CHUNK)], axis=0)
            xcb = xc.astype(BF16)
            both = jnp.dot(xcb, jnp.concatenate([wa_ref[c], wx_ref[c]], axis=1),
                           preferred_element_type=F32)
            tr = jnp.tanh(both[:, :LANES] + ba_ref[:, lanes])
            ti = jnp.tanh(both[:, LANES:] + bx_ref[:, lanes])
            log_a_half = (-0.5 * LRU_C) * softplus_neg[:, lanes]
            a = jnp.exp(log_a_half * tr + log_a_half)
            half_xc = 0.5 * xc
            gated_x = half_xc * ti + half_xc
            z = 1.0 - a * a
            u = jnp.where(z > 0.0, z * lax.rsqrt(z), 0.0) * gated_x

            h = jnp.zeros((SUBLANES, LANES), F32)
            p = jnp.ones((SUBLANES, LANES), F32)
            hs, ps = [], []
            for j in range(LRU_CHUNK):
                aj = piece(a, j)
                uj = piece(u, j)
                if first:
                    uj = jnp.where(first_row + j >= PAD_FRONT, uj, 0.0)
                h = aj * h + uj
                p = aj * p
                hs.append(h)
                ps.append(p)
            entry = [hcar[:, lanes]]
            for s in range(SUBLANES):
                entry.append(h[s:s + 1, :] + p[s:s + 1, :] * entry[s])
            hcar[:, lanes] = entry[SUBLANES]
            entry_rows = jnp.concatenate(entry[:SUBLANES], axis=0)

            for j in range(LRU_CHUNK):
                state = hs[j] + ps[j] * entry_rows
                out = state * _gelu_tanh(y_st[c, strided(j), :])
                sumsq[j] = sumsq[j] + out * out
                o_st[c, strided(j), :] = out

        for j in range(LRU_CHUNK):
            ms = jnp.sum(sumsq[j], axis=1, keepdims=True) * (1.0 / LRU_WIDTH)
            s_st[strided(j), :] = jnp.broadcast_to(lax.rsqrt(ms + EPS), (SUBLANES, LANES))
        scale = s_st[...]
        for c in range(LRU_SLABS):
            lanes = slice(c * LANES, (c + 1) * LANES)
            normed = (o_st[c] * scale * g_ref[:, lanes]).astype(o_ref.dtype)
            if first:
                o_ref[0:LRU_SEG - BLOCK, lanes] = normed[BLOCK:, :]
            else:
                o_ref[rows, lanes] = normed

    assert BLOCK <= LRU_SEG
    segment(0, True)

    def later_segment(k, carry):
        segment(k, False)
        return carry

    lax.fori_loop(1, n_seg, later_segment, 0)


def _rglru(xr, yr, xrm, yrm, cw, cb, wa, wx, ba, bx, lam, g_lru, bsz, nbx):
    seq = nbx * BLOCK
    assert (seq + BLOCK) % LRU_SEG == 0
    main = pl.BlockSpec((None, seq, LRU_WIDTH), lambda b: (b, 0, 0))
    const2 = lambda b: (0, 0)
    row_spec = pl.BlockSpec((1, LRU_WIDTH), const2)
    gate_spec = pl.BlockSpec((LRU_SLABS, LANES, LANES), lambda b: (0, 0, 0))
    slabs = pltpu.VMEM((LRU_SLABS, LRU_SEG, LANES), F32)
    return pl.pallas_call(
        _lru_kernel,
        grid=(bsz,),
        in_specs=[main, main,
                  pl.BlockSpec((BLOCK, LRU_WIDTH), const2),
                  pl.BlockSpec((BLOCK, LRU_WIDTH), const2),
                  pl.BlockSpec((CONV_W, LRU_WIDTH), const2),
                  row_spec, gate_spec, gate_spec,
                  row_spec, row_spec, row_spec, row_spec],
        out_specs=main,
        out_shape=jax.ShapeDtypeStruct((bsz, seq, LRU_WIDTH), BF16),
        scratch_shapes=[slabs, slabs, slabs,
                        pltpu.VMEM((LRU_SEG, LANES), F32),
                        pltpu.VMEM((SUBLANES, LRU_WIDTH), F32),
                        pltpu.VMEM((1, LRU_WIDTH), F32)],
        compiler_params=_cparams(1),
        name="rglru",
    )(xr, yr, xrm, yrm, cw, cb, wa, wx, ba, bx, lam, g_lru)


def _pack_rows(v):
    bits = lax.bitcast_convert_type(v.astype(BF16).astype(F32), U32)
    return (bits[:, :PACKED] >> 16) | (bits[:, PACKED:] & jnp.uint32(0xFFFF0000))


def _unpack_rows(w):
    lo = lax.bitcast_convert_type(w << 16, F32)
    hi = lax.bitcast_convert_type(w & jnp.uint32(0xFFFF0000), F32)
    return lo, hi


def _layer_norm(z, g, b):
    mu = jnp.mean(z, axis=1, keepdims=True)
    zc = z - mu
    var = jnp.mean(zc * zc, axis=1, keepdims=True)
    return zc * lax.rsqrt(var + EPS) * g + b


def _out_proj_kernel(a_ref, l_ref, x_ref, wa_ref, wl_ref, g_ref, b_ref, wrt_hi_ref, wrt_lo_ref,
                     brt_ref, h_ref, hp_ref, info_ref, cnt_ref):
    @pl.when(pl.program_id(0) == 0)
    def _():
        cnt_ref[...] = jnp.zeros_like(cnt_ref)

    _out_proj_rows(a_ref[...], l_ref[...], x_ref[...], wa_ref, wl_ref, g_ref, b_ref,
                   wrt_hi_ref, wrt_lo_ref, brt_ref, h_ref, hp_ref, info_ref, cnt_ref)


def _out_proj_rows(a, l, x, wa_ref, wl_ref, g_ref, b_ref, wrt_hi_ref, wrt_lo_ref, brt_ref,
                   h_ref, hp_ref, info_ref, cnt_ref):
    mix = jnp.dot(jnp.concatenate([a, l], axis=1),
                  jnp.concatenate([wa_ref[...], wl_ref[...]], axis=0),
                  preferred_element_type=F32)
    h = _layer_norm(ALPHA * x + mix, g_ref[...], b_ref[...])
    h_ref[...] = h
    hp_ref[...] = _pack_rows(h)

    h_hi = h.astype(BF16)
    h_lo = (h - h_hi.astype(F32)).astype(BF16)
    nt = (((1,), (1,)), ((), ()))
    w_both = jnp.concatenate([wrt_hi_ref[...], wrt_lo_ref[...]], axis=0)
    both = lax.dot_general(w_both, h_hi, nt, preferred_element_type=F32)
    lg = (both[:ROUTER_ROWS] + both[ROUTER_ROWS:]
          + lax.dot_general(wrt_hi_ref[...], h_lo, nt, preferred_element_type=F32)) + brt_ref[...]
    tile_shape = (SUBLANES, h.shape[0])
    sub = lax.broadcasted_iota(I32, tile_shape, 0)
    ninf = -jnp.inf
    t0 = lg[0:SUBLANES]
    gl = jnp.where(sub < N_GROUPS, t0, ninf)
    gmax = jnp.max(gl, axis=0, keepdims=True)
    g_idx = jnp.min(jnp.where(gl == gmax, sub, SUBLANES), axis=0, keepdims=True)
    g_w = 1.0 / jnp.sum(jnp.where(sub < N_GROUPS, jnp.exp(t0 - gmax), 0.0),
                        axis=0, keepdims=True)
    el = lg[SUBLANES:2 * SUBLANES]
    for g in range(1, N_GROUPS):
        el = jnp.where(g_idx == g, lg[(g + 1) * SUBLANES:(g + 2) * SUBLANES], el)
    v1 = jnp.max(el, axis=0, keepdims=True)
    i1 = jnp.min(jnp.where(el == v1, sub, SUBLANES), axis=0, keepdims=True)
    el2 = jnp.where(sub == i1, ninf, el)
    v2 = jnp.max(el2, axis=0, keepdims=True)
    i2 = jnp.min(jnp.where(el2 == v2, sub, SUBLANES), axis=0, keepdims=True)
    t = jnp.exp(v2 - v1)
    w1 = 1.0 / (1.0 + t)
    w2 = t * w1
    e_base = g_idx * EXPERTS_PER_GROUP
    e1 = e_base + i1
    e2 = e_base + i2
    info_ref[...] = jnp.where(sub == 0, e1.astype(F32),
                              jnp.where(sub == 1, e2.astype(F32),
                                        jnp.where(sub == 2, g_w * w1,
                                                  jnp.where(sub == 3, g_w * w2, 0.0))))
    expert = lax.broadcasted_iota(I32, (N_EXPERTS, h.shape[0]), 0)
    chosen = (expert == e1).astype(F32) + (expert == e2).astype(F32)
    cnt_ref[...] += jnp.sum(chosen, axis=1, keepdims=True)


ROUTER_ROWS = (N_GROUPS + 1) * SUBLANES


def _out_proj(attn_n, lru_n, x2d, wo_a, wo_l, ln_g, ln_b, w_rt_hi, w_rt_lo, b_rt, rows):
    n = x2d.shape[0]
    const = lambda i: (0, 0)
    tile = lambda w: pl.BlockSpec((rows, w), lambda i: (i, 0))
    return pl.pallas_call(
        _out_proj_kernel,
        grid=(n // rows,),
        in_specs=[tile(ATTN_WIDTH), tile(LRU_WIDTH), tile(D_MODEL),
                  pl.BlockSpec((ATTN_WIDTH, D_MODEL), const),
                  pl.BlockSpec((LRU_WIDTH, D_MODEL), const),
                  pl.BlockSpec((1, D_MODEL), const),
                  pl.BlockSpec((1, D_MODEL), const),
                  pl.BlockSpec((ROUTER_ROWS, D_MODEL), const),
                  pl.BlockSpec((ROUTER_ROWS, D_MODEL), const),
                  pl.BlockSpec((ROUTER_ROWS, 1), const)],
        out_specs=[tile(D_MODEL), tile(PACKED),
                   pl.BlockSpec((SUBLANES, rows), lambda i: (0, i)),
                   pl.BlockSpec((N_EXPERTS, LANES), const)],
        out_shape=[jax.ShapeDtypeStruct((n, D_MODEL), F32),
                   jax.ShapeDtypeStruct((n, PACKED), U32),
                   jax.ShapeDtypeStruct((SUBLANES, n), F32),
                   jax.ShapeDtypeStruct((N_EXPERTS, LANES), F32)],
        compiler_params=_cparams(1),
        name="out_proj",
    )(attn_n, lru_n, x2d, wo_a, wo_l, ln_g, ln_b, w_rt_hi, w_rt_lo, b_rt)


def _route_kernel(info_ref, cnt_ref, tri_ref, dest_ref, carry, pstart):
    t = pl.program_id(0)
    info = info_ref[...]
    shape = (N_EXPERTS, info.shape[1])
    expert = lax.broadcasted_iota(I32, shape, 0)
    oh1 = (expert == info[0:1, :].astype(I32)).astype(F32)
    oh2 = (expert == info[1:2, :].astype(I32)).astype(F32)
    both = oh1 + oh2

    @pl.when(t == 0)
    def _():
        c = cnt_ref[...].astype(I32)
        padded = ((c + (MOE_BLOCK - 1)) // MOE_BLOCK) * MOE_BLOCK
        e = lax.broadcasted_iota(I32, (N_EXPERTS, LANES), 0)
        scan = padded
        for d in (1, 2, 4, 8, 16):
            scan = scan + jnp.where(e >= d, pltpu.roll(scan, d, axis=0), 0)
        pstart[...] = (scan - padded)[:, 0:1].astype(F32)
        carry[...] = jnp.zeros_like(carry)

    before = jnp.dot(both.astype(BF16), tri_ref[...], preferred_element_type=F32)
    row_of = before + (carry[...] + pstart[...])
    r1 = jnp.sum(oh1 * row_of, axis=0, keepdims=True)
    r2 = jnp.sum(oh2 * row_of, axis=0, keepdims=True)
    sub = lax.broadcasted_iota(I32, dest_ref.shape, 0)
    dest_ref[...] = jnp.where(sub == 0, r1, jnp.where(sub == 1, r2, 0.0)).astype(I32)
    carry[...] += jnp.sum(both, axis=1, keepdims=True)


def _route(info_t, cnt, cols):
    n = info_t.shape[1]
    tri = jnp.asarray(np.triu(np.ones((cols, cols), np.float32), 1), BF16)
    return pl.pallas_call(
        _route_kernel,
        grid=(n // cols,),
        in_specs=[pl.BlockSpec((SUBLANES, cols), lambda t: (0, t)),
                  pl.BlockSpec((N_EXPERTS, LANES), lambda t: (0, 0)),
                  pl.BlockSpec((cols, cols), lambda t: (0, 0))],
        out_specs=pl.BlockSpec((SUBLANES, cols), lambda t: (0, t)),
        out_shape=jax.ShapeDtypeStruct((SUBLANES, n), I32),
        scratch_shapes=[pltpu.VMEM((N_EXPERTS, 1), F32), pltpu.VMEM((N_EXPERTS, 1), F32)],
        compiler_params=_cparams(1),
        name="route",
    )(info_t, cnt, tri)


def _sc_mesh():
    return plsc.VectorSubcoreMesh(core_axis_name="core", subcore_axis_name="subcore")


def _sc_worker_id():
    return lax.axis_index("subcore") * SC_CORES + lax.axis_index("core")


def _sc_scatter_rows(rows, d0, d1, cap):
    n, width = rows.shape
    per_worker = n // SC_WORKERS
    n_win = per_worker // SC_WINDOW

    def body(x_hbm, d0_hbm, d1_hbm, o_hbm, i0_v, i1_v, rows_v, rsem, sem0, sem1):
        wid = _sc_worker_id()
        pltpu.sync_copy(d0_hbm.at[wid], i0_v)
        pltpu.sync_copy(d1_hbm.at[wid], i1_v)

        def read(j):
            src = x_hbm.at[pl.ds(wid * per_worker + j * SC_WINDOW, SC_WINDOW)]
            return pltpu.make_async_copy(src, rows_v.at[j % 2], rsem.at[j % 2])

        def scatters(j):
            return (pltpu.make_async_copy(rows_v.at[j % 2], o_hbm.at[i0_v.at[j]], sem0.at[j % 2]),
                    pltpu.make_async_copy(rows_v.at[j % 2], o_hbm.at[i1_v.at[j]], sem1.at[j % 2]))

        read(0).start()
        for j in range(n_win):
            if j + 1 < n_win:
                if j >= 1:
                    for cp in scatters(j - 1):
                        cp.wait()
                read(j + 1).start()
            read(j).wait()
            for cp in scatters(j):
                cp.start()
        for j in range(max(n_win - 2, 0), n_win):
            for cp in scatters(j):
                cp.wait()

    return pl.kernel(
        body,
        out_type=jax.ShapeDtypeStruct((cap, width), rows.dtype),
        mesh=_sc_mesh(),
        scratch_types=[pltpu.VMEM((n_win, SC_WINDOW), I32), pltpu.VMEM((n_win, SC_WINDOW), I32),
                       pltpu.VMEM((2, SC_WINDOW, width), rows.dtype),
                       pltpu.SemaphoreType.DMA((2,)), pltpu.SemaphoreType.DMA((2,)),
                       pltpu.SemaphoreType.DMA((2,))],
        name="dispatch",
    )(rows, d0, d1)


def _sc_gather_rows(table, idx):
    width = table.shape[1]
    n_win = idx.shape[1]
    per_worker = n_win * SC_WINDOW

    def body(y_hbm, i_hbm, o_hbm, i_v, rows_v, gsem, wsem):
        wid = _sc_worker_id()
        pltpu.sync_copy(i_hbm.at[wid], i_v)

        def gather(j):
            return pltpu.make_async_copy(y_hbm.at[i_v.at[j]], rows_v.at[j % 2], gsem.at[j % 2])

        def write(j):
            dst = o_hbm.at[pl.ds(wid * per_worker + j * SC_WINDOW, SC_WINDOW)]
            return pltpu.make_async_copy(rows_v.at[j % 2], dst, wsem.at[j % 2])

        gather(0).start()
        for j in range(n_win):
            if j + 1 < n_win:
                if j >= 1:
                    write(j - 1).wait()
                gather(j + 1).start()
            gather(j).wait()
            write(j).start()
        for j in range(max(n_win - 2, 0), n_win):
            write(j).wait()

    return pl.kernel(
        body,
        out_type=jax.ShapeDtypeStruct((SC_WORKERS * per_worker, width), table.dtype),
        mesh=_sc_mesh(),
        scratch_types=[pltpu.VMEM((n_win, SC_WINDOW), I32),
                       pltpu.VMEM((2, SC_WINDOW, width), table.dtype),
                       pltpu.SemaphoreType.DMA((2,)), pltpu.SemaphoreType.DMA((2,))],
        name="collect",
    )(table, idx)


def _expert_kernel(bstart_ref, nblk_ref, nused_ref, xs_hbm, wg_ref, wu_ref, wd_ref, yb_hbm,
                   xbuf, ybuf, zbuf, xsem, ysem, zsem, wg_b, wu_b, wd_b):
    e = pl.program_id(0)
    nused = nused_ref[0]
    n_blocks = yb_hbm.shape[0] // MOE_BLOCK

    def rows(b):
        return pl.ds(pl.multiple_of(b * MOE_BLOCK, MOE_BLOCK), MOE_BLOCK)

    def x_copy(b):
        slot = b % X_RING
        return pltpu.make_async_copy(xs_hbm.at[rows(b)], xbuf.at[slot], xsem.at[slot])

    def y_copy(b):
        slot = b % Y_RING
        return pltpu.make_async_copy(ybuf.at[slot], yb_hbm.at[rows(b)], ysem.at[slot])

    @pl.when(e == 0)
    def _():
        for b in range(X_AHEAD):
            @pl.when(b < nused)
            def _():
                x_copy(b).start()

    wg_b[...] = wg_ref[...].astype(BF16)
    wu_b[...] = wu_ref[...].astype(BF16)
    wd_b[...] = wd_ref[...].astype(BF16)

    def run_blocks(b, count):
        for k in range(count):
            x_copy(b + k).wait()
        for k in range(count):
            nxt = b + X_AHEAD + k

            @pl.when(nxt < nused)
            def _():
                x_copy(nxt).start()

            @pl.when(b + k >= Y_RING)
            def _():
                y_copy(b + k - Y_RING).wait()

        words = jnp.concatenate([xbuf[(b + k) % X_RING] for k in range(count)], axis=0)
        lo, hi = _unpack_rows(words)
        x = jnp.concatenate([lo.astype(BF16), hi.astype(BF16)], axis=1)
        g = jnp.dot(x, wg_b[...], preferred_element_type=F32)
        u = jnp.dot(x, wu_b[...], preferred_element_type=F32)
        mid = (g * _sigmoid(g) * u).astype(BF16)
        y = _pack_rows(jnp.dot(mid, wd_b[...], preferred_element_type=F32))
        for k in range(count):
            ybuf[(b + k) % Y_RING] = y[k * MOE_BLOCK:(k + 1) * MOE_BLOCK]
            y_copy(b + k).start()

    b0 = bstart_ref[e]
    nb = nblk_ref[e]

    def group(i, carry):
        run_blocks(b0 + X_GROUP * i, X_GROUP)
        return carry

    lax.fori_loop(0, nb // X_GROUP, group, 0)
    done = nb - nb % X_GROUP
    size = X_GROUP // 2
    while size >= 1:
        @pl.when((nb // size) % 2 == 1)
        def _(size=size, done=done):
            run_blocks(b0 + done, size)

        done = done + (nb // size) % 2 * size
        size //= 2

    @pl.when(e == pl.num_programs(0) - 1)
    def _():
        for back in range(Y_RING, 0, -1):
            @pl.when(nused >= back)
            def _():
                y_copy(nused - back).wait()

        zbuf[...] = jnp.zeros_like(zbuf)

        def z_copy(b):
            return pltpu.make_async_copy(zbuf, yb_hbm.at[rows(b)], zsem.at[0])

        def z_start(b, carry):
            z_copy(b).start()
            return carry

        def z_wait(b, carry):
            z_copy(b).wait()
            return carry

        lax.fori_loop(nused, n_blocks, z_start, 0)
        lax.fori_loop(nused, n_blocks, z_wait, 0)


def _experts(bstart, nblk, nused, xs, w_gate, w_up, w_down):
    cap = xs.shape[0]
    w_idx = lambda e, bs, nb, nu: (e, 0, 0)
    grid_spec = pltpu.PrefetchScalarGridSpec(
        num_scalar_prefetch=3,
        grid=(N_EXPERTS,),
        in_specs=[pl.BlockSpec(memory_space=pl.ANY),
                  pl.BlockSpec((None, D_MODEL, D_FF), w_idx),
                  pl.BlockSpec((None, D_MODEL, D_FF), w_idx),
                  pl.BlockSpec((None, D_FF, D_MODEL), w_idx)],
        out_specs=pl.BlockSpec(memory_space=pl.ANY),
        scratch_shapes=[pltpu.VMEM((X_RING, MOE_BLOCK, PACKED), U32),
                        pltpu.VMEM((Y_RING, MOE_BLOCK, PACKED), U32),
                        pltpu.VMEM((MOE_BLOCK, PACKED), U32),
                        pltpu.SemaphoreType.DMA((X_RING,)),
                        pltpu.SemaphoreType.DMA((Y_RING,)),
                        pltpu.SemaphoreType.DMA((1,)),
                        pltpu.VMEM((D_MODEL, D_FF), BF16),
                        pltpu.VMEM((D_MODEL, D_FF), BF16),
                        pltpu.VMEM((D_FF, D_MODEL), BF16)])
    return pl.pallas_call(
        _expert_kernel,
        grid_spec=grid_spec,
        out_shape=jax.ShapeDtypeStruct((cap, PACKED), U32),
        compiler_params=_cparams(1),
        name="experts",
    )(bstart, nblk, nused, xs, w_gate, w_up, w_down)


def _combine_kernel(y0_ref, y1_ref, h_ref, info_ref, g_ref, b_ref, o_ref):
    info = info_ref[...].T
    g0 = info[:, 2:3]
    g1 = info[:, 3:4]
    lo0, hi0 = _unpack_rows(y0_ref[...])
    lo1, hi1 = _unpack_rows(y1_ref[...])
    y = jnp.concatenate([g0 * lo0 + g1 * lo1, g0 * hi0 + g1 * hi1], axis=1)
    o_ref[...] = _layer_norm(ALPHA * h_ref[...] + y, g_ref[...], b_ref[...])


def _combine(ys, h, info, ln_g, ln_b, rows, part, n_parts):
    n = h.shape[0]
    steps = n // n_parts // rows
    off = part * steps
    const = lambda i: (0, 0)
    return pl.pallas_call(
        _combine_kernel,
        grid=(steps,),
        in_specs=[pl.BlockSpec((rows, PACKED), lambda i: (i, 0)),
                  pl.BlockSpec((rows, PACKED), lambda i: (i + steps, 0)),
                  pl.BlockSpec((rows, D_MODEL), lambda i: (i + off, 0)),
                  pl.BlockSpec((SUBLANES, rows), lambda i: (0, i + off)),
                  pl.BlockSpec((1, D_MODEL), const),
                  pl.BlockSpec((1, D_MODEL), const)],
        out_specs=pl.BlockSpec((rows, D_MODEL), lambda i: (i + off, 0)),
        out_shape=jax.ShapeDtypeStruct((n, D_MODEL), F32),
        input_output_aliases={2: 0},
        compiler_params=_cparams(1),
        name="combine",
    )(ys, ys, h, info, ln_g, ln_b)


def _alibi_bias():
    qi = np.arange(BLOCK)[:, None]
    kj = np.arange(2 * BLOCK)[None, :]
    dist = qi - kj + BLOCK
    band = (dist >= 0) & (dist < BLOCK)
    slopes = np.exp2(-8.0 * np.arange(1, N_Q_HEADS + 1, dtype=np.float32) / N_Q_HEADS)
    bias = np.where(band[None], -slopes[:, None, None] * dist[None].astype(np.float32), NEG)
    bias = bias * LOG2E
    first = np.where((kj >= PAD_FRONT)[None], bias, NEG)
    out = np.empty((2, N_KV_HEADS, 2 * BLOCK, 4 * BLOCK), np.float32)
    for v, per_head in enumerate((first, bias)):
        for j in range(N_KV_HEADS):
            out[v, j] = np.block([[per_head[4 * j], per_head[4 * j + 1]],
                                  [per_head[4 * j + 2], per_head[4 * j + 3]]])
    return jnp.asarray(out, F32)


def _slab_gates(w):
    nb, c, _ = w.shape
    per = LANES // c
    w = w.reshape(nb // per, per, c, c)
    eye = jnp.eye(per, dtype=w.dtype)
    return jnp.einsum('spcd,pq->spcqd', w, eye).reshape(nb // per, LANES, LANES).astype(BF16)


def kernel(x, meta_tokens, w_in, conv_w, conv_b, lru_wa, lru_ba, lru_wx, lru_bx, lru_lambda,
           attn_sinks, g_attn, g_lru, w_out, ln1_g, ln1_b, w_group, b_group, w_router,
           b_router, w_gate, w_up, w_down, ln2_g, ln2_b):
    bsz, seq, d = x.shape
    nbx = seq // BLOCK
    n_tok = bsz * seq
    x2d = x.reshape(n_tok, d)
    row = lambda v: v.reshape(1, -1).astype(F32)

    q_scale = jnp.concatenate([jnp.full((ATTN_WIDTH,), LOG2E * HEAD_DIM ** -0.5, F32),
                               jnp.ones((IN_COLS - ATTN_WIDTH,), F32)])
    w_in_b = (w_in[0] * q_scale).astype(BF16)
    meta_blk = jnp.concatenate([jnp.zeros((PAD_FRONT, d), F32), meta_tokens.astype(F32)], axis=0)
    q, kv, xr, yr = _in_proj(x2d, w_in_b, PROJ_ROWS)
    qm, kvm, xrm, yrm = _in_proj(meta_blk, w_in_b, BLOCK)
    shp = lambda a: a.reshape(bsz, seq, a.shape[-1])

    attn_n = _attention(attn_sinks[0].astype(F32) * LOG2E, shp(q), shp(kv), kvm, _alibi_bias(),
                        row(g_attn[0]), bsz, nbx)
    lru_n = _rglru(shp(xr), shp(yr), xrm, yrm, conv_w[0].astype(F32), row(conv_b[0]),
                   _slab_gates(0.5 * lru_wa[0]), _slab_gates(0.5 * lru_wx[0]),
                   0.5 * row(lru_ba[0]), 0.5 * row(lru_bx[0]), row(lru_lambda[0]),
                   row(g_lru[0]), bsz, nbx)

    w_out_b = w_out[0].astype(BF16)
    gpad = SUBLANES - N_GROUPS
    w_rt = jnp.concatenate(
        [w_group[0].T, jnp.zeros((gpad, d), F32),
         jnp.transpose(w_router[0], (0, 2, 1)).reshape(N_EXPERTS, d)], axis=0).astype(F32)
    w_rt_hi = w_rt.astype(BF16)
    w_rt_lo = (w_rt - w_rt_hi.astype(F32)).astype(BF16)
    b_rt = jnp.concatenate([b_group[0], jnp.zeros((gpad,), F32),
                            b_router[0].reshape(-1)]).astype(F32).reshape(ROUTER_ROWS, 1)
    h1, hp, info, cnt = _out_proj(
        attn_n.reshape(n_tok, ATTN_WIDTH), lru_n.reshape(n_tok, LRU_WIDTH), x2d,
        w_out_b[:ATTN_WIDTH], w_out_b[ATTN_WIDTH:], row(ln1_g[0]), row(ln1_b[0]),
        w_rt_hi, w_rt_lo, b_rt, OUT_PROJ_ROWS)

    dest = _route(info, cnt, ROUTE_ROWS)
    n_slots = n_tok * TOP_K
    n_blocks = n_slots // MOE_BLOCK + N_EXPERTS
    cap = n_blocks * MOE_BLOCK
    nblk = (cnt[:, 0].astype(I32) + MOE_BLOCK - 1) // MOE_BLOCK
    bends = jnp.cumsum(nblk)
    bstart = (bends - nblk).astype(I32)
    nused = bends[-1:].astype(I32)
    windows = lambda v: v.reshape(SC_WORKERS, -1, SC_WINDOW)
    d0 = dest[0]
    d1 = dest[1]

    xs = _sc_scatter_rows(hp, windows(d0), windows(d1), cap)
    yb = _experts(bstart, nblk.astype(I32), nused, xs, w_gate[0], w_up[0], w_down[0])
    out = h1
    part_len = n_tok // COMBINE_PARTS
    for part in range(COMBINE_PARTS):
        tok = slice(part * part_len, (part + 1) * part_len)
        ys = _sc_gather_rows(yb, windows(jnp.concatenate([d0[tok], d1[tok]])))
        out = _combine(ys, out, info, row(ln2_g[0]), row(ln2_b[0]), COMBINE_ROWS,
                       part, COMBINE_PARTS)
    return out.reshape(bsz, seq, d)
```

```python
import jax
import jax.numpy as jnp
import numpy as np
from jax import lax
from jax.experimental import pallas as pl
from jax.experimental.pallas import tpu as pltpu
from jax.experimental.pallas import tpu_sc as plsc

F32 = jnp.float32
BF16 = jnp.bfloat16
U32 = jnp.uint32
I32 = jnp.int32

D_MODEL = 1024
N_META = 16
BLOCK = 128
PAD_FRONT = BLOCK - N_META
HEAD_DIM = 64
ATTN_WIDTH = 512
LRU_WIDTH = 512
N_Q_HEADS = 8
N_KV_HEADS = 2
KV_WIDTH = N_KV_HEADS * HEAD_DIM
LRU_BLOCKS = 8
CONV_W = 4
LRU_C = 8.0
IN_COLS = ATTN_WIDTH + 2 * KV_WIDTH + 2 * LRU_WIDTH
N_GROUPS = 4
EXPERTS_PER_GROUP = 8
N_EXPERTS = N_GROUPS * EXPERTS_PER_GROUP
TOP_K = 2
D_FF = 512
MOE_BLOCK = 256
ALPHA = 2.0 ** 0.25
EPS = 1e-5
NEG = -1e30
LOG2E = float(np.log2(np.e))
LANES = 128
SUBLANES = 8
PACKED = D_MODEL // 2

PROJ_ROWS = 1024
OUT_PROJ_ROWS = 1024
ROUTE_ROWS = 512
COMBINE_ROWS = 1024
COMBINE_SHARES = (1, 1, 2, 4)
X_GROUP = 4
X_AHEAD = 4
X_RING = X_AHEAD + X_GROUP
Y_RING = 2 * X_GROUP
VMEM_LIMIT = 48 * 1024 * 1024

SC_CORES = 2
SC_SUBCORES = 16
SC_WORKERS = SC_CORES * SC_SUBCORES
SC_WINDOW = 64


def _cparams(n_axes):
    return pltpu.CompilerParams(
        dimension_semantics=("arbitrary",) * n_axes, vmem_limit_bytes=VMEM_LIMIT)


def _in_proj_kernel(x_ref, w_ref, q_ref, kv_ref, xr_ref, yr_ref):
    proj = jnp.dot(x_ref[...].astype(BF16), w_ref[...], preferred_element_type=F32)
    o = 0
    for ref, width in ((q_ref, ATTN_WIDTH), (kv_ref, 2 * KV_WIDTH),
                       (xr_ref, LRU_WIDTH), (yr_ref, LRU_WIDTH)):
        ref[...] = proj[:, o:o + width].astype(ref.dtype)
        o += width


def _in_proj(x2d, w_bf16, rows):
    n = x2d.shape[0]
    widths = (ATTN_WIDTH, 2 * KV_WIDTH, LRU_WIDTH, LRU_WIDTH)
    return pl.pallas_call(
        _in_proj_kernel,
        grid=(n // rows,),
        in_specs=[pl.BlockSpec((rows, D_MODEL), lambda i: (i, 0)),
                  pl.BlockSpec((D_MODEL, IN_COLS), lambda i: (0, 0))],
        out_specs=[pl.BlockSpec((rows, w), lambda i: (i, 0)) for w in widths],
        out_shape=[jax.ShapeDtypeStruct((n, w), BF16) for w in widths],
        compiler_params=_cparams(1),
        name="in_proj",
    )(x2d, w_bf16)


def _attn_kernel(sinks_ref, q_ref, kv_ref, kvm_ref, bias_ref, g_ref, o_ref,
                 klo, khi, vlo, vhi):
    nbx = q_ref.shape[0] // BLOCK
    lo_lanes = lax.broadcasted_iota(I32, (BLOCK // 2, LANES), 1) < HEAD_DIM

    def layout_block(n, blk):
        rows = pl.ds(pl.multiple_of(n * BLOCK, BLOCK), BLOCK)
        as_bf16 = lambda words: pltpu.bitcast(words, BF16)
        for src, dst_lo, dst_hi in ((blk[:, :KV_WIDTH], klo, khi), (blk[:, KV_WIDTH:], vlo, vhi)):
            w = pltpu.bitcast(src, U32)
            r = pltpu.roll(w, HEAD_DIM, axis=1)
            zero = jnp.zeros_like(w)
            dst_lo[0, rows, :] = as_bf16(jnp.where(lo_lanes, w, zero))
            dst_hi[0, rows, :] = as_bf16(jnp.where(lo_lanes, zero, r))
            dst_lo[1, rows, :] = as_bf16(jnp.where(lo_lanes, r, zero))
            dst_hi[1, rows, :] = as_bf16(jnp.where(lo_lanes, zero, w))

    layout_block(0, kvm_ref[...])

    def layout_body(n, carry):
        layout_block(n + 1, kv_ref[pl.ds(pl.multiple_of(n * BLOCK, BLOCK), BLOCK), :])
        return carry

    lax.fori_loop(0, nbx, layout_body, 0, unroll=4)

    ones_lo = jnp.where(lax.broadcasted_iota(I32, (2 * BLOCK, LANES), 1) < HEAD_DIM,
                        1.0, 0.0).astype(BF16)
    ones_hi = (1.0 - ones_lo.astype(F32)).astype(BF16)
    top_rows = lax.broadcasted_iota(I32, (2 * BLOCK, 1), 0) < BLOCK
    lo_half = lax.broadcasted_iota(I32, (2 * BLOCK, LANES), 1) < HEAD_DIM

    def block(i, carry):
        q_rows = pl.ds(pl.multiple_of(i * BLOCK, BLOCK), BLOCK)
        win = pl.ds(pl.multiple_of(i * BLOCK, BLOCK), 2 * BLOCK)
        q = q_ref[q_rows, :]
        first = jnp.minimum(i, 1)
        outs = []
        for j in range(N_KV_HEADS):
            q2 = jnp.concatenate([q[:, (2 * j) * LANES:(2 * j + 1) * LANES],
                                  q[:, (2 * j + 1) * LANES:(2 * j + 2) * LANES]], axis=0)
            kc = jnp.concatenate([klo[j, win, :], khi[j, win, :]], axis=0)
            s = lax.dot_general(q2, kc, (((1,), (1,)), ((), ())), preferred_element_type=F32)
            s = s + bias_ref[first, j]
            ps, es = [], []
            for c in range(2):
                sink = jnp.where(top_rows, sinks_ref[4 * j + c], sinks_ref[4 * j + 2 + c])
                sc = s[:, c * 2 * BLOCK:(c + 1) * 2 * BLOCK]
                m = jnp.maximum(jnp.max(sc, axis=1, keepdims=True), sink)
                ps.append(jnp.exp2(sc - m).astype(BF16))
                es.append(jnp.exp2(sink - m))
            v_lo = jnp.concatenate([vlo[j, win, :], ones_lo], axis=1)
            v_hi = jnp.concatenate([vhi[j, win, :], ones_hi], axis=1)
            r = jnp.dot(jnp.concatenate(ps, axis=1), jnp.concatenate([v_lo, v_hi], axis=0),
                        preferred_element_type=F32)
            den = r[:, LANES:] + jnp.where(lo_half, es[0], es[1])
            o2 = r[:, :LANES] * (1.0 / den)
            outs += [o2[:BLOCK], o2[BLOCK:]]
        out = jnp.concatenate(outs, axis=1)
        ms = jnp.mean(out * out, axis=1, keepdims=True)
        o_ref[q_rows, :] = (out * lax.rsqrt(ms + EPS) * g_ref[...]).astype(o_ref.dtype)
        return carry

    lax.fori_loop(0, nbx, block, 0, unroll=8)


def _attention(sinks, q, kv, kvm, bias, g_attn, bsz, nbx):
    seq = nbx * BLOCK
    const2 = lambda b: (0, 0)
    kv_scratch = pltpu.VMEM((N_KV_HEADS, seq + BLOCK, LANES), BF16)
    return pl.pallas_call(
        _attn_kernel,
        grid=(bsz,),
        in_specs=[pl.BlockSpec(memory_space=pltpu.SMEM),
                  pl.BlockSpec((None, seq, ATTN_WIDTH), lambda b: (b, 0, 0)),
                  pl.BlockSpec((None, seq, 2 * KV_WIDTH), lambda b: (b, 0, 0)),
                  pl.BlockSpec((BLOCK, 2 * KV_WIDTH), const2),
                  pl.BlockSpec((2, N_KV_HEADS, 2 * BLOCK, 4 * BLOCK), lambda b: (0, 0, 0, 0)),
                  pl.BlockSpec((1, ATTN_WIDTH), const2)],
        out_specs=pl.BlockSpec((None, seq, ATTN_WIDTH), lambda b: (b, 0, 0)),
        out_shape=jax.ShapeDtypeStruct((bsz, seq, ATTN_WIDTH), BF16),
        scratch_shapes=[kv_scratch, kv_scratch, kv_scratch, kv_scratch],
        compiler_params=_cparams(1),
        name="attention",
    )(sinks, q, kv, kvm, bias, g_attn)


def _sigmoid(v):
    return 0.5 * jnp.tanh(0.5 * v) + 0.5


def _gelu_tanh(y):
    c = float(np.sqrt(2.0 / np.pi))
    half = 0.5 * y
    return half + half * jnp.tanh(y * (c + (c * 0.044715) * (y * y)))


LRU_CHUNK = 44
LRU_SEG = SUBLANES * LRU_CHUNK
LRU_SLABS = LRU_WIDTH // LANES


def _lru_kernel(xr_ref, yr_ref, xrm_ref, yrm_ref, cw_ref, cb_ref, wa_ref, wx_ref, ba_ref,
                bx_ref, lam_ref, g_ref, o_ref, x_st, y_st, o_st, s_st, xtail, hcar):
    seq = xr_ref.shape[0]
    n_seg = (seq + BLOCK) // LRU_SEG
    xtail[...] = jnp.zeros_like(xtail)
    hcar[...] = jnp.zeros_like(hcar)
    lam = lam_ref[...]
    softplus_neg = jnp.maximum(-lam, 0.0) + jnp.log(1.0 + jnp.exp(-jnp.abs(lam)))
    sub = lax.broadcasted_iota(jnp.int32, (SUBLANES, LANES), 0)

    def strided(j):
        return pl.ds(j, SUBLANES, stride=LRU_CHUNK)

    def piece(v, j):
        return v[j * SUBLANES:(j + 1) * SUBLANES, :]

    def segment(k, first):
        if first:
            head = LRU_SEG - BLOCK
            x_nat = jnp.concatenate([xrm_ref[...], xr_ref[0:head, :]], axis=0).astype(F32)
            y_nat = jnp.concatenate([yrm_ref[...], yr_ref[0:head, :]], axis=0).astype(F32)
        else:
            rows = pl.ds(pl.multiple_of(k * LRU_SEG - BLOCK, 2 * SUBLANES), LRU_SEG)
            x_nat = xr_ref[rows, :].astype(F32)
            y_nat = yr_ref[rows, :].astype(F32)
        for c in range(LRU_SLABS):
            x_st[c] = x_nat[:, c * LANES:(c + 1) * LANES]
            y_st[c] = y_nat[:, c * LANES:(c + 1) * LANES]
        first_row = k * LRU_SEG + LRU_CHUNK * sub
        sumsq = [jnp.zeros((SUBLANES, LANES), F32) for _ in range(LRU_CHUNK)]

        for c in range(LRU_SLABS):
            lanes = slice(c * LANES, (c + 1) * LANES)
            x = [x_st[c, strided(j), :] for j in range(LRU_CHUNK)]
            before = []
            for d in range(1, CONV_W):
                from_prev_chunk = pltpu.roll(x[LRU_CHUNK - d], 1, axis=0)
                before.append(jnp.where(sub == 0, xtail[d - 1:d, lanes], from_prev_chunk))
            for d in range(1, CONV_W):
                xtail[d - 1:d, lanes] = x[LRU_CHUNK - d][SUBLANES - 1:SUBLANES, :]

            def x_at(j):
                return x[j] if j >= 0 else before[-j - 1]

            taps = [cw_ref[t:t + 1, lanes] for t in range(CONV_W)]
            bias = cb_ref[:, lanes]
            xc = jnp.concatenate(
                [bias + sum(taps[t] * x_at(j - (CONV_W - 1) + t) for t in range(CONV_W))
                 for j in range(LRU_CHUNK)], axis=0)
            xcb = xc.astype(BF16)
            both = jnp.dot(xcb, jnp.concatenate([wa_ref[c], wx_ref[c]], axis=1),
                           preferred_element_type=F32)
            tr = jnp.tanh(both[:, :LANES] + ba_ref[:, lanes])
            ti = jnp.tanh(both[:, LANES:] + bx_ref[:, lanes])
            log_a_half = (-0.5 * LRU_C) * softplus_neg[:, lanes]
            a = jnp.exp(log_a_half * tr + log_a_half)
            half_xc = 0.5 * xc
            gated_x = half_xc * ti + half_xc
            z = 1.0 - a * a
            u = jnp.where(z > 0.0, z * lax.rsqrt(z), 0.0) * gated_x

            h = jnp.zeros((SUBLANES, LANES), F32)
            p = jnp.ones((SUBLANES, LANES), F32)
            hs, ps = [], []
            for j in range(LRU_CHUNK):
                aj = piece(a, j)
                uj = piece(u, j)
                if first:
                    uj = jnp.where(first_row + j >= PAD_FRONT, uj, 0.0)
                h = aj * h + uj
                p = aj * p
                hs.append(h)
                ps.append(p)
            entry = [hcar[:, lanes]]
            for s in range(SUBLANES):
                entry.append(h[s:s + 1, :] + p[s:s + 1, :] * entry[s])
            hcar[:, lanes] = entry[SUBLANES]
            entry_rows = jnp.concatenate(entry[:SUBLANES], axis=0)

            for j in range(LRU_CHUNK):
                state = hs[j] + ps[j] * entry_rows
                out = state * _gelu_tanh(y_st[c, strided(j), :])
                sumsq[j] = sumsq[j] + out * out
                o_st[c, strided(j), :] = out

        for j in range(LRU_CHUNK):
            ms = jnp.sum(sumsq[j], axis=1, keepdims=True) * (1.0 / LRU_WIDTH)
            s_st[strided(j), :] = jnp.broadcast_to(lax.rsqrt(ms + EPS), (SUBLANES, LANES))
        scale = s_st[...]
        for c in range(LRU_SLABS):
            lanes = slice(c * LANES, (c + 1) * LANES)
            normed = (o_st[c] * scale * g_ref[:, lanes]).astype(o_ref.dtype)
            if first:
                o_ref[0:LRU_SEG - BLOCK, lanes] = normed[BLOCK:, :]
            else:
                o_ref[rows, lanes] = normed

    assert BLOCK <= LRU_SEG
    segment(0, True)

    def later_segment(k, carry):
        segment(k, False)
        return carry

    lax.fori_loop(1, n_seg, later_segment, 0)


def _rglru(xr, yr, xrm, yrm, cw, cb, wa, wx, ba, bx, lam, g_lru, bsz, nbx):
    seq = nbx * BLOCK
    assert (seq + BLOCK) % LRU_SEG == 0
    main = pl.BlockSpec((None, seq, LRU_WIDTH), lambda b: (b, 0, 0))
    const2 = lambda b: (0, 0)
    row_spec = pl.BlockSpec((1, LRU_WIDTH), const2)
    gate_spec = pl.BlockSpec((LRU_SLABS, LANES, LANES), lambda b: (0, 0, 0))
    slabs = pltpu.VMEM((LRU_SLABS, LRU_SEG, LANES), F32)
    return pl.pallas_call(
        _lru_kernel,
        grid=(bsz,),
        in_specs=[main, main,
                  pl.BlockSpec((BLOCK, LRU_WIDTH), const2),
                  pl.BlockSpec((BLOCK, LRU_WIDTH), const2),
                  pl.BlockSpec((CONV_W, LRU_WIDTH), const2),
                  row_spec, gate_spec, gate_spec,
                  row_spec, row_spec, row_spec, row_spec],
        out_specs=main,
        out_shape=jax.ShapeDtypeStruct((bsz, seq, LRU_WIDTH), BF16),
        scratch_shapes=[slabs, slabs, slabs,
                        pltpu.VMEM((LRU_SEG, LANES), F32),
                        pltpu.VMEM((SUBLANES, LRU_WIDTH), F32),
                        pltpu.VMEM((1, LRU_WIDTH), F32)],
        compiler_params=_cparams(1),
        name="rglru",
    )(xr, yr, xrm, yrm, cw, cb, wa, wx, ba, bx, lam, g_lru)


def _pack_rows(v):
    bits = lax.bitcast_convert_type(v.astype(BF16).astype(F32), U32)
    return (bits[:, :PACKED] >> 16) | (bits[:, PACKED:] & jnp.uint32(0xFFFF0000))


def _unpack_rows(w):
    lo = lax.bitcast_convert_type(w << 16, F32)
    hi = lax.bitcast_convert_type(w & jnp.uint32(0xFFFF0000), F32)
    return lo, hi


def _layer_norm(z, g, b):
    mu = jnp.mean(z, axis=1, keepdims=True)
    zc = z - mu
    var = jnp.mean(zc * zc, axis=1, keepdims=True)
    return zc * lax.rsqrt(var + EPS) * g + b


def _out_proj_kernel(a_ref, l_ref, x_ref, wa_ref, wl_ref, g_ref, b_ref, wrt_hi_ref, wrt_lo_ref,
                     brt_ref, h_ref, hp_ref, info_ref, cnt_ref):
    @pl.when(pl.program_id(0) == 0)
    def _():
        cnt_ref[...] = jnp.zeros_like(cnt_ref)

    _out_proj_rows(a_ref[...], l_ref[...], x_ref[...], wa_ref, wl_ref, g_ref, b_ref,
                   wrt_hi_ref, wrt_lo_ref, brt_ref, h_ref, hp_ref, info_ref, cnt_ref)


def _out_proj_rows(a, l, x, wa_ref, wl_ref, g_ref, b_ref, wrt_hi_ref, wrt_lo_ref, brt_ref,
                   h_ref, hp_ref, info_ref, cnt_ref):
    mix = jnp.dot(jnp.concatenate([a, l], axis=1),
                  jnp.concatenate([wa_ref[...], wl_ref[...]], axis=0),
                  preferred_element_type=F32)
    h = _layer_norm(ALPHA * x + mix, g_ref[...], b_ref[...])
    h_ref[...] = h
    hp_ref[...] = _pack_rows(h)

    h_hi = h.astype(BF16)
    h_lo = (h - h_hi.astype(F32)).astype(BF16)
    nt = (((1,), (1,)), ((), ()))
    w_both = jnp.concatenate([wrt_hi_ref[...], wrt_lo_ref[...]], axis=0)
    both = lax.dot_general(w_both, h_hi, nt, preferred_element_type=F32)
    lg = (both[:ROUTER_ROWS] + both[ROUTER_ROWS:]
          + lax.dot_general(wrt_hi_ref[...], h_lo, nt, preferred_element_type=F32)) + brt_ref[...]
    tile_shape = (SUBLANES, h.shape[0])
    sub = lax.broadcasted_iota(I32, tile_shape, 0)
    ninf = -jnp.inf
    t0 = lg[0:SUBLANES]
    gl = jnp.where(sub < N_GROUPS, t0, ninf)
    gmax = jnp.max(gl, axis=0, keepdims=True)
    g_idx = jnp.min(jnp.where(gl == gmax, sub, SUBLANES), axis=0, keepdims=True)
    g_w = 1.0 / jnp.sum(jnp.where(sub < N_GROUPS, jnp.exp(t0 - gmax), 0.0),
                        axis=0, keepdims=True)
    el = lg[SUBLANES:2 * SUBLANES]
    for g in range(1, N_GROUPS):
        el = jnp.where(g_idx == g, lg[(g + 1) * SUBLANES:(g + 2) * SUBLANES], el)
    v1 = jnp.max(el, axis=0, keepdims=True)
    i1 = jnp.min(jnp.where(el == v1, sub, SUBLANES), axis=0, keepdims=True)
    el2 = jnp.where(sub == i1, ninf, el)
    v2 = jnp.max(el2, axis=0, keepdims=True)
    i2 = jnp.min(jnp.where(el2 == v2, sub, SUBLANES), axis=0, keepdims=True)
    t = jnp.exp(v2 - v1)
    w1 = 1.0 / (1.0 + t)
    w2 = t * w1
    e_base = g_idx * EXPERTS_PER_GROUP
    e1 = e_base + i1
    e2 = e_base + i2
    info_ref[...] = jnp.where(sub == 0, e1.astype(F32),
                              jnp.where(sub == 1, e2.astype(F32),
                                        jnp.where(sub == 2, g_w * w1,
                                                  jnp.where(sub == 3, g_w * w2, 0.0))))
    expert = lax.broadcasted_iota(I32, (N_EXPERTS, h.shape[0]), 0)
    chosen = (expert == e1).astype(F32) + (expert == e2).astype(F32)
    cnt_ref[...] += jnp.sum(chosen, axis=1, keepdims=True)


ROUTER_ROWS = (N_GROUPS + 1) * SUBLANES


def _out_proj(attn_n, lru_n, x2d, wo_a, wo_l, ln_g, ln_b, w_rt_hi, w_rt_lo, b_rt, rows):
    n = x2d.shape[0]
    const = lambda i: (0, 0)
    tile = lambda w: pl.BlockSpec((rows, w), lambda i: (i, 0))
    return pl.pallas_call(
        _out_proj_kernel,
        grid=(n // rows,),
        in_specs=[tile(ATTN_WIDTH), tile(LRU_WIDTH), tile(D_MODEL),
                  pl.BlockSpec((ATTN_WIDTH, D_MODEL), const),
                  pl.BlockSpec((LRU_WIDTH, D_MODEL), const),
                  pl.BlockSpec((1, D_MODEL), const),
                  pl.BlockSpec((1, D_MODEL), const),
                  pl.BlockSpec((ROUTER_ROWS, D_MODEL), const),
                  pl.BlockSpec((ROUTER_ROWS, D_MODEL), const),
                  pl.BlockSpec((ROUTER_ROWS, 1), const)],
        out_specs=[tile(D_MODEL), tile(PACKED),
                   pl.BlockSpec((SUBLANES, rows), lambda i: (0, i)),
                   pl.BlockSpec((N_EXPERTS, LANES), const)],
        out_shape=[jax.ShapeDtypeStruct((n, D_MODEL), F32),
                   jax.ShapeDtypeStruct((n, PACKED), U32),
                   jax.ShapeDtypeStruct((SUBLANES, n), F32),
                   jax.ShapeDtypeStruct((N_EXPERTS, LANES), F32)],
        compiler_params=_cparams(1),
        name="out_proj",
    )(attn_n, lru_n, x2d, wo_a, wo_l, ln_g, ln_b, w_rt_hi, w_rt_lo, b_rt)


def _route_kernel(info_ref, cnt_ref, tri_ref, dest_ref, carry, pstart):
    t = pl.program_id(0)
    info = info_ref[...]
    shape = (N_EXPERTS, info.shape[1])
    expert = lax.broadcasted_iota(I32, shape, 0)
    oh1 = (expert == info[0:1, :].astype(I32)).astype(F32)
    oh2 = (expert == info[1:2, :].astype(I32)).astype(F32)
    both = oh1 + oh2

    @pl.when(t == 0)
    def _():
        c = cnt_ref[...].astype(I32)
        padded = ((c + (MOE_BLOCK - 1)) // MOE_BLOCK) * MOE_BLOCK
        e = lax.broadcasted_iota(I32, (N_EXPERTS, LANES), 0)
        scan = padded
        for d in (1, 2, 4, 8, 16):
            scan = scan + jnp.where(e >= d, pltpu.roll(scan, d, axis=0), 0)
        pstart[...] = (scan - padded)[:, 0:1].astype(F32)
        carry[...] = jnp.zeros_like(carry)

    before = jnp.dot(both.astype(BF16), tri_ref[...], preferred_element_type=F32)
    row_of = before + (carry[...] + pstart[...])
    r1 = jnp.sum(oh1 * row_of, axis=0, keepdims=True)
    r2 = jnp.sum(oh2 * row_of, axis=0, keepdims=True)
    sub = lax.broadcasted_iota(I32, dest_ref.shape, 0)
    dest_ref[...] = jnp.where(sub == 0, r1, jnp.where(sub == 1, r2, 0.0)).astype(I32)
    carry[...] += jnp.sum(both, axis=1, keepdims=True)


def _route(info_t, cnt, cols):
    n = info_t.shape[1]
    tri = jnp.asarray(np.triu(np.ones((cols, cols), np.float32), 1), BF16)
    return pl.pallas_call(
        _route_kernel,
        grid=(n // cols,),
        in_specs=[pl.BlockSpec((SUBLANES, cols), lambda t: (0, t)),
                  pl.BlockSpec((N_EXPERTS, LANES), lambda t: (0, 0)),
                  pl.BlockSpec((cols, cols), lambda t: (0, 0))],
        out_specs=pl.BlockSpec((SUBLANES, cols), lambda t: (0, t)),
        out_shape=jax.ShapeDtypeStruct((SUBLANES, n), I32),
        scratch_shapes=[pltpu.VMEM((N_EXPERTS, 1), F32), pltpu.VMEM((N_EXPERTS, 1), F32)],
        compiler_params=_cparams(1),
        name="route",
    )(info_t, cnt, tri)


def _sc_mesh():
    return plsc.VectorSubcoreMesh(core_axis_name="core", subcore_axis_name="subcore")


def _sc_worker_id():
    return lax.axis_index("subcore") * SC_CORES + lax.axis_index("core")


def _sc_scatter_rows(rows, d0, d1, cap):
    n, width = rows.shape
    per_worker = n // SC_WORKERS
    n_win = per_worker // SC_WINDOW

    def body(x_hbm, d0_hbm, d1_hbm, o_hbm, i0_v, i1_v, rows_v, rsem, sem0, sem1):
        wid = _sc_worker_id()
        pltpu.sync_copy(d0_hbm.at[wid], i0_v)
        pltpu.sync_copy(d1_hbm.at[wid], i1_v)

        def read(j):
            src = x_hbm.at[pl.ds(wid * per_worker + j * SC_WINDOW, SC_WINDOW)]
            return pltpu.make_async_copy(src, rows_v.at[j % 2], rsem.at[j % 2])

        def scatters(j):
            return (pltpu.make_async_copy(rows_v.at[j % 2], o_hbm.at[i0_v.at[j]], sem0.at[j % 2]),
                    pltpu.make_async_copy(rows_v.at[j % 2], o_hbm.at[i1_v.at[j]], sem1.at[j % 2]))

        read(0).start()
        for j in range(n_win):
            if j + 1 < n_win:
                if j >= 1:
                    for cp in scatters(j - 1):
                        cp.wait()
                read(j + 1).start()
            read(j).wait()
            for cp in scatters(j):
                cp.start()
        for j in range(max(n_win - 2, 0), n_win):
            for cp in scatters(j):
                cp.wait()

    return pl.kernel(
        body,
        out_type=jax.ShapeDtypeStruct((cap, width), rows.dtype),
        mesh=_sc_mesh(),
        scratch_types=[pltpu.VMEM((n_win, SC_WINDOW), I32), pltpu.VMEM((n_win, SC_WINDOW), I32),
                       pltpu.VMEM((2, SC_WINDOW, width), rows.dtype),
                       pltpu.SemaphoreType.DMA((2,)), pltpu.SemaphoreType.DMA((2,)),
                       pltpu.SemaphoreType.DMA((2,))],
        name="dispatch",
    )(rows, d0, d1)


def _sc_gather_rows(table, idx):
    width = table.shape[1]
    n_win = idx.shape[1]
    per_worker = n_win * SC_WINDOW

    def body(y_hbm, i_hbm, o_hbm, i_v, rows_v, gsem, wsem):
        wid = _sc_worker_id()
        pltpu.sync_copy(i_hbm.at[wid], i_v)

        def gather(j):
            return pltpu.make_async_copy(y_hbm.at[i_v.at[j]], rows_v.at[j % 2], gsem.at[j % 2])

        def write(j):
            dst = o_hbm.at[pl.ds(wid * per_worker + j * SC_WINDOW, SC_WINDOW)]
            return pltpu.make_async_copy(rows_v.at[j % 2], dst, wsem.at[j % 2])

        gather(0).start()
        for j in range(n_win):
            if j + 1 < n_win:
                if j >= 1:
                    write(j - 1).wait()
                gather(j + 1).start()
            gather(j).wait()
            write(j).start()
        for j in range(max(n_win - 2, 0), n_win):
            write(j).wait()

    return pl.kernel(
        body,
        out_type=jax.ShapeDtypeStruct((SC_WORKERS * per_worker, width), table.dtype),
        mesh=_sc_mesh(),
        scratch_types=[pltpu.VMEM((n_win, SC_WINDOW), I32),
                       pltpu.VMEM((2, SC_WINDOW, width), table.dtype),
                       pltpu.SemaphoreType.DMA((2,)), pltpu.SemaphoreType.DMA((2,))],
        name="collect",
    )(table, idx)


def _expert_kernel(bstart_ref, nblk_ref, nused_ref, xs_hbm, wg_ref, wu_ref, wd_ref, yb_hbm,
                   xbuf, ybuf, zbuf, xsem, ysem, zsem, wg_b, wu_b, wd_b):
    e = pl.program_id(0)
    nused = nused_ref[0]
    n_blocks = yb_hbm.shape[0] // MOE_BLOCK

    def rows(b):
        return pl.ds(pl.multiple_of(b * MOE_BLOCK, MOE_BLOCK), MOE_BLOCK)

    def x_copy(b):
        slot = b % X_RING
        return pltpu.make_async_copy(xs_hbm.at[rows(b)], xbuf.at[slot], xsem.at[slot])

    def y_copy(b):
        slot = b % Y_RING
        return pltpu.make_async_copy(ybuf.at[slot], yb_hbm.at[rows(b)], ysem.at[slot])

    @pl.when(e == 0)
    def _():
        for b in range(X_AHEAD):
            @pl.when(b < nused)
            def _():
                x_copy(b).start()

    wg_b[...] = wg_ref[...].astype(BF16)
    wu_b[...] = wu_ref[...].astype(BF16)
    wd_b[...] = wd_ref[...].astype(BF16)

    def run_blocks(b, count):
        for k in range(count):
            x_copy(b + k).wait()
        for k in range(count):
            nxt = b + X_AHEAD + k

            @pl.when(nxt < nused)
            def _():
                x_copy(nxt).start()

            @pl.when(b + k >= Y_RING)
            def _():
                y_copy(b + k - Y_RING).wait()

        words = jnp.concatenate([xbuf[(b + k) % X_RING] for k in range(count)], axis=0)
        lo, hi = _unpack_rows(words)
        x = jnp.concatenate([lo.astype(BF16), hi.astype(BF16)], axis=1)
        g = jnp.dot(x, wg_b[...], preferred_element_type=F32)
        u = jnp.dot(x, wu_b[...], preferred_element_type=F32)
        mid = (g * _sigmoid(g) * u).astype(BF16)
        y = _pack_rows(jnp.dot(mid, wd_b[...], preferred_element_type=F32))
        for k in range(count):
            ybuf[(b + k) % Y_RING] = y[k * MOE_BLOCK:(k + 1) * MOE_BLOCK]
            y_copy(b + k).start()

    b0 = bstart_ref[e]
    nb = nblk_ref[e]

    def group(i, carry):
        run_blocks(b0 + X_GROUP * i, X_GROUP)
        return carry

    lax.fori_loop(0, nb // X_GROUP, group, 0)
    done = nb - nb % X_GROUP
    size = X_GROUP // 2
    while size >= 1:
        @pl.when((nb // size) % 2 == 1)
        def _(size=size, done=done):
            run_blocks(b0 + done, size)

        done = done + (nb // size) % 2 * size
        size //= 2

    @pl.when(e == pl.num_programs(0) - 1)
    def _():
        for back in range(Y_RING, 0, -1):
            @pl.when(nused >= back)
            def _():
                y_copy(nused - back).wait()

        zbuf[...] = jnp.zeros_like(zbuf)

        def z_copy(b):
            return pltpu.make_async_copy(zbuf, yb_hbm.at[rows(b)], zsem.at[0])

        def z_start(b, carry):
            z_copy(b).start()
            return carry

        def z_wait(b, carry):
            z_copy(b).wait()
            return carry

        lax.fori_loop(nused, n_blocks, z_start, 0)
        lax.fori_loop(nused, n_blocks, z_wait, 0)


def _experts(bstart, nblk, nused, xs, w_gate, w_up, w_down):
    cap = xs.shape[0]
    w_idx = lambda e, bs, nb, nu: (e, 0, 0)
    grid_spec = pltpu.PrefetchScalarGridSpec(
        num_scalar_prefetch=3,
        grid=(N_EXPERTS,),
        in_specs=[pl.BlockSpec(memory_space=pl.ANY),
                  pl.BlockSpec((None, D_MODEL, D_FF), w_idx),
                  pl.BlockSpec((None, D_MODEL, D_FF), w_idx),
                  pl.BlockSpec((None, D_FF, D_MODEL), w_idx)],
        out_specs=pl.BlockSpec(memory_space=pl.ANY),
        scratch_shapes=[pltpu.VMEM((X_RING, MOE_BLOCK, PACKED), U32),
                        pltpu.VMEM((Y_RING, MOE_BLOCK, PACKED), U32),
                        pltpu.VMEM((MOE_BLOCK, PACKED), U32),
                        pltpu.SemaphoreType.DMA((X_RING,)),
                        pltpu.SemaphoreType.DMA((Y_RING,)),
                        pltpu.SemaphoreType.DMA((1,)),
                        pltpu.VMEM((D_MODEL, D_FF), BF16),
                        pltpu.VMEM((D_MODEL, D_FF), BF16),
                        pltpu.VMEM((D_FF, D_MODEL), BF16)])
    return pl.pallas_call(
        _expert_kernel,
        grid_spec=grid_spec,
        out_shape=jax.ShapeDtypeStruct((cap, PACKED), U32),
        compiler_params=_cparams(1),
        name="experts",
    )(bstart, nblk, nused, xs, w_gate, w_up, w_down)


def _combine_kernel(y0_ref, y1_ref, h_ref, info_ref, g_ref, b_ref, o_ref):
    info = info_ref[...].T
    g0 = info[:, 2:3]
    g1 = info[:, 3:4]
    lo0, hi0 = _unpack_rows(y0_ref[...])
    lo1, hi1 = _unpack_rows(y1_ref[...])
    y = jnp.concatenate([g0 * lo0 + g1 * lo1, g0 * hi0 + g1 * hi1], axis=1)
    o_ref[...] = _layer_norm(ALPHA * h_ref[...] + y, g_ref[...], b_ref[...])


def _combine(ys, h, info, ln_g, ln_b, rows, start, length):
    n = h.shape[0]
    steps = length // rows
    off = start // rows
    const = lambda i: (0, 0)
    return pl.pallas_call(
        _combine_kernel,
        grid=(steps,),
        in_specs=[pl.BlockSpec((rows, PACKED), lambda i: (i, 0)),
                  pl.BlockSpec((rows, PACKED), lambda i: (i + steps, 0)),
                  pl.BlockSpec((rows, D_MODEL), lambda i: (i + off, 0)),
                  pl.BlockSpec((SUBLANES, rows), lambda i: (0, i + off)),
                  pl.BlockSpec((1, D_MODEL), const),
                  pl.BlockSpec((1, D_MODEL), const)],
        out_specs=pl.BlockSpec((rows, D_MODEL), lambda i: (i + off, 0)),
        out_shape=jax.ShapeDtypeStruct((n, D_MODEL), F32),
        input_output_aliases={2: 0},
        compiler_params=_cparams(1),
        name="combine",
    )(ys, ys, h, info, ln_g, ln_b)


def _alibi_bias():
    qi = np.arange(BLOCK)[:, None]
    kj = np.arange(2 * BLOCK)[None, :]
    dist = qi - kj + BLOCK
    band = (dist >= 0) & (dist < BLOCK)
    slopes = np.exp2(-8.0 * np.arange(1, N_Q_HEADS + 1, dtype=np.float32) / N_Q_HEADS)
    bias = np.where(band[None], -slopes[:, None, None] * dist[None].astype(np.float32), NEG)
    bias = bias * LOG2E
    first = np.where((kj >= PAD_FRONT)[None], bias, NEG)
    out = np.empty((2, N_KV_HEADS, 2 * BLOCK, 4 * BLOCK), np.float32)
    for v, per_head in enumerate((first, bias)):
        for j in range(N_KV_HEADS):
            out[v, j] = np.block([[per_head[4 * j], per_head[4 * j + 1]],
                                  [per_head[4 * j + 2], per_head[4 * j + 3]]])
    return jnp.asarray(out, F32)


def _slab_gates(w):
    nb, c, _ = w.shape
    per = LANES // c
    w = w.reshape(nb // per, per, c, c)
    eye = jnp.eye(per, dtype=w.dtype)
    return jnp.einsum('spcd,pq->spcqd', w, eye).reshape(nb // per, LANES, LANES).astype(BF16)


def kernel(x, meta_tokens, w_in, conv_w, conv_b, lru_wa, lru_ba, lru_wx, lru_bx, lru_lambda,
           attn_sinks, g_attn, g_lru, w_out, ln1_g, ln1_b, w_group, b_group, w_router,
           b_router, w_gate, w_up, w_down, ln2_g, ln2_b):
    bsz, seq, d = x.shape
    nbx = seq // BLOCK
    n_tok = bsz * seq
    x2d = x.reshape(n_tok, d)
    row = lambda v: v.reshape(1, -1).astype(F32)

    q_scale = jnp.concatenate([jnp.full((ATTN_WIDTH,), LOG2E * HEAD_DIM ** -0.5, F32),
                               jnp.ones((IN_COLS - ATTN_WIDTH,), F32)])
    w_in_b = (w_in[0] * q_scale).astype(BF16)
    meta_blk = jnp.concatenate([jnp.zeros((PAD_FRONT, d), F32), meta_tokens.astype(F32)], axis=0)
    q, kv, xr, yr = _in_proj(x2d, w_in_b, PROJ_ROWS)
    qm, kvm, xrm, yrm = _in_proj(meta_blk, w_in_b, BLOCK)
    shp = lambda a: a.reshape(bsz, seq, a.shape[-1])

    attn_n = _attention(attn_sinks[0].astype(F32) * LOG2E, shp(q), shp(kv), kvm, _alibi_bias(),
                        row(g_attn[0]), bsz, nbx)
    lru_n = _rglru(shp(xr), shp(yr), xrm, yrm, conv_w[0].astype(F32), row(conv_b[0]),
                   _slab_gates(0.5 * lru_wa[0]), _slab_gates(0.5 * lru_wx[0]),
                   0.5 * row(lru_ba[0]), 0.5 * row(lru_bx[0]), row(lru_lambda[0]),
                   row(g_lru[0]), bsz, nbx)

    w_out_b = w_out[0].astype(BF16)
    gpad = SUBLANES - N_GROUPS
    w_rt = jnp.concatenate(
        [w_group[0].T, jnp.zeros((gpad, d), F32),
         jnp.transpose(w_router[0], (0, 2, 1)).reshape(N_EXPERTS, d)], axis=0).astype(F32)
    w_rt_hi = w_rt.astype(BF16)
    w_rt_lo = (w_rt - w_rt_hi.astype(F32)).astype(BF16)
    b_rt = jnp.concatenate([b_group[0], jnp.zeros((gpad,), F32),
                            b_router[0].reshape(-1)]).astype(F32).reshape(ROUTER_ROWS, 1)
    h1, hp, info, cnt = _out_proj(
        attn_n.reshape(n_tok, ATTN_WIDTH), lru_n.reshape(n_tok, LRU_WIDTH), x2d,
        w_out_b[:ATTN_WIDTH], w_out_b[ATTN_WIDTH:], row(ln1_g[0]), row(ln1_b[0]),
        w_rt_hi, w_rt_lo, b_rt, OUT_PROJ_ROWS)

    dest = _route(info, cnt, ROUTE_ROWS)
    n_slots = n_tok * TOP_K
    n_blocks = n_slots // MOE_BLOCK + N_EXPERTS
    cap = n_blocks * MOE_BLOCK
    nblk = (cnt[:, 0].astype(I32) + MOE_BLOCK - 1) // MOE_BLOCK
    bends = jnp.cumsum(nblk)
    bstart = (bends - nblk).astype(I32)
    nused = bends[-1:].astype(I32)
    windows = lambda v: v.reshape(SC_WORKERS, -1, SC_WINDOW)
    d0 = dest[0]
    d1 = dest[1]

    xs = _sc_scatter_rows(hp, windows(d0), windows(d1), cap)
    yb = _experts(bstart, nblk.astype(I32), nused, xs, w_gate[0], w_up[0], w_down[0])
    out = h1
    start = 0
    for share in COMBINE_SHARES:
        length = n_tok * share // sum(COMBINE_SHARES)
        tok = slice(start, start + length)
        ys = _sc_gather_rows(yb, windows(jnp.concatenate([d0[tok], d1[tok]])))
        out = _combine(ys, out, info, row(ln2_g[0]), row(ln2_b[0]), COMBINE_ROWS, start, length)
        start += length
    return out.reshape(bsz, seq, d)
```

```python
import jax
import jax.numpy as jnp
import numpy as np
from jax import lax
from jax.experimental import pallas as pl
from jax.experimental.pallas import tpu as pltpu
from jax.experimental.pallas import tpu_sc as plsc

F32 = jnp.float32
BF16 = jnp.bfloat16
U32 = jnp.uint32
I32 = jnp.int32

D_MODEL = 1024
N_META = 16
BLOCK = 128
PAD_FRONT = BLOCK - N_META
HEAD_DIM = 64
ATTN_WIDTH = 512
LRU_WIDTH = 512
N_Q_HEADS = 8
N_KV_HEADS = 2
KV_WIDTH = N_KV_HEADS * HEAD_DIM
LRU_BLOCKS = 8
CONV_W = 4
LRU_C = 8.0
IN_COLS = ATTN_WIDTH + 2 * KV_WIDTH + 2 * LRU_WIDTH
N_GROUPS = 4
EXPERTS_PER_GROUP = 8
N_EXPERTS = N_GROUPS * EXPERTS_PER_GROUP
TOP_K = 2
D_FF = 512
MOE_BLOCK = 256
ALPHA = 2.0 ** 0.25
EPS = 1e-5
NEG = -1e30
LOG2E = float(np.log2(np.e))
LANES = 128
SUBLANES = 8
PACKED = D_MODEL // 2

PROJ_ROWS = 1024
OUT_PROJ_ROWS = 1024
ROUTE_ROWS = 512
COMBINE_ROWS = 1024
COMBINE_PARTS = 4
X_GROUP = 4
X_AHEAD = 4
X_RING = X_AHEAD + X_GROUP
Y_RING = 2 * X_GROUP
VMEM_LIMIT = 48 * 1024 * 1024

SC_CORES = 2
SC_SUBCORES = 16
SC_WORKERS = SC_CORES * SC_SUBCORES
SC_WINDOW = 64


def _cparams(n_axes):
    return pltpu.CompilerParams(
        dimension_semantics=("arbitrary",) * n_axes, vmem_limit_bytes=VMEM_LIMIT)


def _in_proj_kernel(x_ref, w_ref, q_ref, kv_ref, xr_ref, yr_ref):
    proj = jnp.dot(x_ref[...].astype(BF16), w_ref[...], preferred_element_type=F32)
    o = 0
    for ref, width in ((q_ref, ATTN_WIDTH), (kv_ref, 2 * KV_WIDTH),
                       (xr_ref, LRU_WIDTH), (yr_ref, LRU_WIDTH)):
        ref[...] = proj[:, o:o + width].astype(ref.dtype)
        o += width


def _in_proj(x2d, w_bf16, rows):
    n = x2d.shape[0]
    widths = (ATTN_WIDTH, 2 * KV_WIDTH, LRU_WIDTH, LRU_WIDTH)
    return pl.pallas_call(
        _in_proj_kernel,
        grid=(n // rows,),
        in_specs=[pl.BlockSpec((rows, D_MODEL), lambda i: (i, 0)),
                  pl.BlockSpec((D_MODEL, IN_COLS), lambda i: (0, 0))],
        out_specs=[pl.BlockSpec((rows, w), lambda i: (i, 0)) for w in widths],
        out_shape=[jax.ShapeDtypeStruct((n, w), BF16) for w in widths],
        compiler_params=_cparams(1),
        name="in_proj",
    )(x2d, w_bf16)


def _attn_kernel(sinks_ref, q_ref, kv_ref, kvm_ref, bias_ref, g_ref, o_ref,
                 klo, khi, vlo, vhi):
    nbx = q_ref.shape[0] // BLOCK
    lo_lanes = lax.broadcasted_iota(I32, (BLOCK // 2, LANES), 1) < HEAD_DIM

    def layout_block(n, blk):
        rows = pl.ds(pl.multiple_of(n * BLOCK, BLOCK), BLOCK)
        as_bf16 = lambda words: pltpu.bitcast(words, BF16)
        for src, dst_lo, dst_hi in ((blk[:, :KV_WIDTH], klo, khi), (blk[:, KV_WIDTH:], vlo, vhi)):
            w = pltpu.bitcast(src, U32)
            r = pltpu.roll(w, HEAD_DIM, axis=1)
            zero = jnp.zeros_like(w)
            dst_lo[0, rows, :] = as_bf16(jnp.where(lo_lanes, w, zero))
            dst_hi[0, rows, :] = as_bf16(jnp.where(lo_lanes, zero, r))
            dst_lo[1, rows, :] = as_bf16(jnp.where(lo_lanes, r, zero))
            dst_hi[1, rows, :] = as_bf16(jnp.where(lo_lanes, zero, w))

    layout_block(0, kvm_ref[...])

    def layout_body(n, carry):
        layout_block(n + 1, kv_ref[pl.ds(pl.multiple_of(n * BLOCK, BLOCK), BLOCK), :])
        return carry

    lax.fori_loop(0, nbx, layout_body, 0, unroll=4)

    ones_lo = jnp.where(lax.broadcasted_iota(I32, (2 * BLOCK, LANES), 1) < HEAD_DIM,
                        1.0, 0.0).astype(BF16)
    ones_hi = (1.0 - ones_lo.astype(F32)).astype(BF16)
    top_rows = lax.broadcasted_iota(I32, (2 * BLOCK, 1), 0) < BLOCK
    lo_half = lax.broadcasted_iota(I32, (2 * BLOCK, LANES), 1) < HEAD_DIM

    def block(i, carry):
        q_rows = pl.ds(pl.multiple_of(i * BLOCK, BLOCK), BLOCK)
        win = pl.ds(pl.multiple_of(i * BLOCK, BLOCK), 2 * BLOCK)
        q = q_ref[q_rows, :]
        first = jnp.minimum(i, 1)
        outs = []
        for j in range(N_KV_HEADS):
            q2 = jnp.concatenate([q[:, (2 * j) * LANES:(2 * j + 1) * LANES],
                                  q[:, (2 * j + 1) * LANES:(2 * j + 2) * LANES]], axis=0)
            kc = jnp.concatenate([klo[j, win, :], khi[j, win, :]], axis=0)
            s = lax.dot_general(q2, kc, (((1,), (1,)), ((), ())), preferred_element_type=F32)
            s = s + bias_ref[first, j]
            ps, es = [], []
            for c in range(2):
                sink = jnp.where(top_rows, sinks_ref[4 * j + c], sinks_ref[4 * j + 2 + c])
                sc = s[:, c * 2 * BLOCK:(c + 1) * 2 * BLOCK]
                m = jnp.maximum(jnp.max(sc, axis=1, keepdims=True), sink)
                ps.append(jnp.exp2(sc - m).astype(BF16))
                es.append(jnp.exp2(sink - m))
            v_lo = jnp.concatenate([vlo[j, win, :], ones_lo], axis=1)
            v_hi = jnp.concatenate([vhi[j, win, :], ones_hi], axis=1)
            r = jnp.dot(jnp.concatenate(ps, axis=1), jnp.concatenate([v_lo, v_hi], axis=0),
                        preferred_element_type=F32)
            den = r[:, LANES:] + jnp.where(lo_half, es[0], es[1])
            o2 = r[:, :LANES] * (1.0 / den)
            outs += [o2[:BLOCK], o2[BLOCK:]]
        out = jnp.concatenate(outs, axis=1)
        ms = jnp.mean(out * out, axis=1, keepdims=True)
        o_ref[q_rows, :] = (out * lax.rsqrt(ms + EPS) * g_ref[...]).astype(o_ref.dtype)
        return carry

    lax.fori_loop(0, nbx, block, 0, unroll=8)


def _attention(sinks, q, kv, kvm, bias, g_attn, bsz, nbx):
    seq = nbx * BLOCK
    const2 = lambda b: (0, 0)
    kv_scratch = pltpu.VMEM((N_KV_HEADS, seq + BLOCK, LANES), BF16)
    return pl.pallas_call(
        _attn_kernel,
        grid=(bsz,),
        in_specs=[pl.BlockSpec(memory_space=pltpu.SMEM),
                  pl.BlockSpec((None, seq, ATTN_WIDTH), lambda b: (b, 0, 0)),
                  pl.BlockSpec((None, seq, 2 * KV_WIDTH), lambda b: (b, 0, 0)),
                  pl.BlockSpec((BLOCK, 2 * KV_WIDTH), const2),
                  pl.BlockSpec((2, N_KV_HEADS, 2 * BLOCK, 4 * BLOCK), lambda b: (0, 0, 0, 0)),
                  pl.BlockSpec((1, ATTN_WIDTH), const2)],
        out_specs=pl.BlockSpec((None, seq, ATTN_WIDTH), lambda b: (b, 0, 0)),
        out_shape=jax.ShapeDtypeStruct((bsz, seq, ATTN_WIDTH), BF16),
        scratch_shapes=[kv_scratch, kv_scratch, kv_scratch, kv_scratch],
        compiler_params=_cparams(1),
        name="attention",
    )(sinks, q, kv, kvm, bias, g_attn)


def _sigmoid(v):
    return 0.5 * jnp.tanh(0.5 * v) + 0.5


def _gelu_tanh(y):
    c = float(np.sqrt(2.0 / np.pi))
    half = 0.5 * y
    return half + half * jnp.tanh(y * (c + (c * 0.044715) * (y * y)))


LRU_CHUNK = 44
LRU_SEG = SUBLANES * LRU_CHUNK
LRU_SLABS = LRU_WIDTH // LANES


def _lru_kernel(xr_ref, yr_ref, xrm_ref, yrm_ref, cw_ref, cb_ref, wa_ref, wx_ref, ba_ref,
                bx_ref, lam_ref, g_ref, o_ref, x_st, y_st, o_st, s_st, xtail, hcar):
    seq = xr_ref.shape[0]
    n_seg = (seq + BLOCK) // LRU_SEG
    xtail[...] = jnp.zeros_like(xtail)
    hcar[...] = jnp.zeros_like(hcar)
    lam = lam_ref[...]
    softplus_neg = jnp.maximum(-lam, 0.0) + jnp.log(1.0 + jnp.exp(-jnp.abs(lam)))
    sub = lax.broadcasted_iota(jnp.int32, (SUBLANES, LANES), 0)

    def strided(j):
        return pl.ds(j, SUBLANES, stride=LRU_CHUNK)

    def piece(v, j):
        return v[j * SUBLANES:(j + 1) * SUBLANES, :]

    def segment(k, first):
        if first:
            head = LRU_SEG - BLOCK
            x_nat = jnp.concatenate([xrm_ref[...], xr_ref[0:head, :]], axis=0).astype(F32)
            y_nat = jnp.concatenate([yrm_ref[...], yr_ref[0:head, :]], axis=0).astype(F32)
        else:
            rows = pl.ds(pl.multiple_of(k * LRU_SEG - BLOCK, 2 * SUBLANES), LRU_SEG)
            x_nat = xr_ref[rows, :].astype(F32)
            y_nat = yr_ref[rows, :].astype(F32)
        for c in range(LRU_SLABS):
            x_st[c] = x_nat[:, c * LANES:(c + 1) * LANES]
            y_st[c] = y_nat[:, c * LANES:(c + 1) * LANES]
        first_row = k * LRU_SEG + LRU_CHUNK * sub
        sumsq = [jnp.zeros((SUBLANES, LANES), F32) for _ in range(LRU_CHUNK)]

        for c in range(LRU_SLABS):
            lanes = slice(c * LANES, (c + 1) * LANES)
            x = [x_st[c, strided(j), :] for j in range(LRU_CHUNK)]
            before = []
            for d in range(1, CONV_W):
                from_prev_chunk = pltpu.roll(x[LRU_CHUNK - d], 1, axis=0)
                before.append(jnp.where(sub == 0, xtail[d - 1:d, lanes], from_prev_chunk))
            for d in range(1, CONV_W):
                xtail[d - 1:d, lanes] = x[LRU_CHUNK - d][SUBLANES - 1:SUBLANES, :]

            def x_at(j):
                return x[j] if j >= 0 else before[-j - 1]

            taps = [cw_ref[t:t + 1, lanes] for t in range(CONV_W)]
            bias = cb_ref[:, lanes]
            xc = jnp.concatenate(
                [bias + sum(taps[t] * x_at(j - (CONV_W - 1) + t) for t in range(CONV_W))
                 for j in range(LRU_CHUNK)], axis=0)
            xcb = xc.astype(BF16)
            both = jnp.dot(xcb, jnp.concatenate([wa_ref[c], wx_ref[c]], axis=1),
                           preferred_element_type=F32)
            tr = jnp.tanh(both[:, :LANES] + ba_ref[:, lanes])
            ti = jnp.tanh(both[:, LANES:] + bx_ref[:, lanes])
            log_a_half = (-0.5 * LRU_C) * softplus_neg[:, lanes]
            a = jnp.exp(log_a_half * tr + log_a_half)
            half_xc = 0.5 * xc
            gated_x = half_xc * ti + half_xc
            z = 1.0 - a * a
            u = jnp.where(z > 0.0, z * lax.rsqrt(z), 0.0) * gated_x

            h = jnp.zeros((SUBLANES, LANES), F32)
            p = jnp.ones((SUBLANES, LANES), F32)
            hs, ps = [], []
            for j in range(LRU_CHUNK):
                aj = piece(a, j)
                uj = piece(u, j)
                if first:
                    uj = jnp.where(first_row + j >= PAD_FRONT, uj, 0.0)
                h = aj * h + uj
                p = aj * p
                hs.append(h)
                ps.append(p)
            entry = [hcar[:, lanes]]
            for s in range(SUBLANES):
                entry.append(h[s:s + 1, :] + p[s:s + 1, :] * entry[s])
            hcar[:, lanes] = entry[SUBLANES]
            entry_rows = jnp.concatenate(entry[:SUBLANES], axis=0)

            for j in range(LRU_CHUNK):
                state = hs[j] + ps[j] * entry_rows
                out = state * _gelu_tanh(y_st[c, strided(j), :])
                sumsq[j] = sumsq[j] + out * out
                o_st[c, strided(j), :] = out

        for j in range(LRU_CHUNK):
            ms = jnp.sum(sumsq[j], axis=1, keepdims=True) * (1.0 / LRU_WIDTH)
            s_st[strided(j), :] = jnp.broadcast_to(lax.rsqrt(ms + EPS), (SUBLANES, LANES))
        scale = s_st[...]
        for c in range(LRU_SLABS):
            lanes = slice(c * LANES, (c + 1) * LANES)
            normed = (o_st[c] * scale * g_ref[:, lanes]).astype(o_ref.dtype)
            if first:
                o_ref[0:LRU_SEG - BLOCK, lanes] = normed[BLOCK:, :]
            else:
                o_ref[rows, lanes] = normed

    assert BLOCK <= LRU_SEG
    segment(0, True)

    def later_segment(k, carry):
        segment(k, False)
        return carry

    lax.fori_loop(1, n_seg, later_segment, 0)


def _rglru(xr, yr, xrm, yrm, cw, cb, wa, wx, ba, bx, lam, g_lru, bsz, nbx):
    seq = nbx * BLOCK
    assert (seq + BLOCK) % LRU_SEG == 0
    main = pl.BlockSpec((None, seq, LRU_WIDTH), lambda b: (b, 0, 0))
    const2 = lambda b: (0, 0)
    row_spec = pl.BlockSpec((1, LRU_WIDTH), const2)
    gate_spec = pl.BlockSpec((LRU_SLABS, LANES, LANES), lambda b: (0, 0, 0))
    slabs = pltpu.VMEM((LRU_SLABS, LRU_SEG, LANES), F32)
    return pl.pallas_call(
        _lru_kernel,
        grid=(bsz,),
        in_specs=[main, main,
                  pl.BlockSpec((BLOCK, LRU_WIDTH), const2),
                  pl.BlockSpec((BLOCK, LRU_WIDTH), const2),
                  pl.BlockSpec((CONV_W, LRU_WIDTH), const2),
                  row_spec, gate_spec, gate_spec,
                  row_spec, row_spec, row_spec, row_spec],
        out_specs=main,
        out_shape=jax.ShapeDtypeStruct((bsz, seq, LRU_WIDTH), BF16),
        scratch_shapes=[slabs, slabs, slabs,
                        pltpu.VMEM((LRU_SEG, LANES), F32),
                        pltpu.VMEM((SUBLANES, LRU_WIDTH), F32),
                        pltpu.VMEM((1, LRU_WIDTH), F32)],
        compiler_params=_cparams(1),
        name="rglru",
    )(xr, yr, xrm, yrm, cw, cb, wa, wx, ba, bx, lam, g_lru)


def _pack_rows(v):
    bits = lax.bitcast_convert_type(v.astype(BF16).astype(F32), U32)
    return (bits[:, :PACKED] >> 16) | (bits[:, PACKED:] & jnp.uint32(0xFFFF0000))


def _unpack_rows(w):
    lo = lax.bitcast_convert_type(w << 16, F32)
    hi = lax.bitcast_convert_type(w & jnp.uint32(0xFFFF0000), F32)
    return lo, hi


def _layer_norm(z, g, b):
    mu = jnp.mean(z, axis=1, keepdims=True)
    zc = z - mu
    var = jnp.mean(zc * zc, axis=1, keepdims=True)
    return zc * lax.rsqrt(var + EPS) * g + b


def _out_proj_kernel(a_ref, l_ref, x_ref, w_ref, g_ref, b_ref, wrt_ref, brt_ref,
                     h_ref, hp_ref, info_ref, cnt_ref, w_b, wrt_b):
    @pl.when(pl.program_id(0) == 0)
    def _():
        cnt_ref[...] = jnp.zeros_like(cnt_ref)
        w_b[...] = w_ref[...].astype(BF16)
        hi = wrt_ref[...].astype(BF16)
        wrt_b[0:ROUTER_ROWS, :] = hi
        wrt_b[ROUTER_ROWS:, :] = (wrt_ref[...] - hi.astype(F32)).astype(BF16)

    mix = jnp.dot(jnp.concatenate([a_ref[...], l_ref[...]], axis=1), w_b[...],
                  preferred_element_type=F32)
    h = _layer_norm(ALPHA * x_ref[...] + mix, g_ref[...], b_ref[...])
    h_ref[...] = h
    hp_ref[...] = _pack_rows(h)

    h_hi = h.astype(BF16)
    h_lo = (h - h_hi.astype(F32)).astype(BF16)
    nt = (((1,), (1,)), ((), ()))
    both = lax.dot_general(wrt_b[...], h_hi, nt, preferred_element_type=F32)
    lg = (both[:ROUTER_ROWS] + both[ROUTER_ROWS:]
          + lax.dot_general(wrt_b[0:ROUTER_ROWS, :], h_lo, nt, preferred_element_type=F32)
          ) + brt_ref[...]
    tile_shape = (SUBLANES, h.shape[0])
    sub = lax.broadcasted_iota(I32, tile_shape, 0)
    ninf = -jnp.inf
    t0 = lg[0:SUBLANES]
    gl = jnp.where(sub < N_GROUPS, t0, ninf)
    gmax = jnp.max(gl, axis=0, keepdims=True)
    g_idx = jnp.min(jnp.where(gl == gmax, sub, SUBLANES), axis=0, keepdims=True)
    g_w = 1.0 / jnp.sum(jnp.where(sub < N_GROUPS, jnp.exp(t0 - gmax), 0.0),
                        axis=0, keepdims=True)
    el = lg[SUBLANES:2 * SUBLANES]
    for g in range(1, N_GROUPS):
        el = jnp.where(g_idx == g, lg[(g + 1) * SUBLANES:(g + 2) * SUBLANES], el)
    v1 = jnp.max(el, axis=0, keepdims=True)
    i1 = jnp.min(jnp.where(el == v1, sub, SUBLANES), axis=0, keepdims=True)
    el2 = jnp.where(sub == i1, ninf, el)
    v2 = jnp.max(el2, axis=0, keepdims=True)
    i2 = jnp.min(jnp.where(el2 == v2, sub, SUBLANES), axis=0, keepdims=True)
    t = jnp.exp(v2 - v1)
    w1 = 1.0 / (1.0 + t)
    w2 = t * w1
    e_base = g_idx * EXPERTS_PER_GROUP
    e1 = e_base + i1
    e2 = e_base + i2
    info_ref[...] = jnp.where(sub == 0, e1.astype(F32),
                              jnp.where(sub == 1, e2.astype(F32),
                                        jnp.where(sub == 2, g_w * w1,
                                                  jnp.where(sub == 3, g_w * w2, 0.0))))
    expert = lax.broadcasted_iota(I32, (N_EXPERTS, h.shape[0]), 0)
    chosen = (expert == e1).astype(F32) + (expert == e2).astype(F32)
    cnt_ref[...] += jnp.sum(chosen, axis=1, keepdims=True)


ROUTER_ROWS = -(-(N_GROUPS + 1) * SUBLANES // (2 * SUBLANES)) * (2 * SUBLANES)


def _out_proj(attn_n, lru_n, x2d, w_out, ln_g, ln_b, w_rt, b_rt, rows):
    n = x2d.shape[0]
    const = lambda i: (0, 0)
    tile = lambda w: pl.BlockSpec((rows, w), lambda i: (i, 0))
    return pl.pallas_call(
        _out_proj_kernel,
        grid=(n // rows,),
        in_specs=[tile(ATTN_WIDTH), tile(LRU_WIDTH), tile(D_MODEL),
                  pl.BlockSpec((D_MODEL, D_MODEL), const),
                  pl.BlockSpec((1, D_MODEL), const),
                  pl.BlockSpec((1, D_MODEL), const),
                  pl.BlockSpec((ROUTER_ROWS, D_MODEL), const),
                  pl.BlockSpec((ROUTER_ROWS, 1), const)],
        out_specs=[tile(D_MODEL), tile(PACKED),
                   pl.BlockSpec((SUBLANES, rows), lambda i: (0, i)),
                   pl.BlockSpec((N_EXPERTS, LANES), const)],
        out_shape=[jax.ShapeDtypeStruct((n, D_MODEL), F32),
                   jax.ShapeDtypeStruct((n, PACKED), U32),
                   jax.ShapeDtypeStruct((SUBLANES, n), F32),
                   jax.ShapeDtypeStruct((N_EXPERTS, LANES), F32)],
        scratch_shapes=[pltpu.VMEM((D_MODEL, D_MODEL), BF16),
                        pltpu.VMEM((2 * ROUTER_ROWS, D_MODEL), BF16)],
        compiler_params=_cparams(1),
        name="out_proj",
    )(attn_n, lru_n, x2d, w_out, ln_g, ln_b, w_rt, b_rt)


def _route_kernel(info_ref, cnt_ref, tri_ref, d0_ref, d1_ref, carry, pstart):
    t = pl.program_id(0)
    info = info_ref[...]
    shape = (N_EXPERTS, info.shape[1])
    expert = lax.broadcasted_iota(I32, shape, 0)
    oh1 = (expert == info[0:1, :].astype(I32)).astype(F32)
    oh2 = (expert == info[1:2, :].astype(I32)).astype(F32)
    both = oh1 + oh2

    @pl.when(t == 0)
    def _():
        c = cnt_ref[...].astype(I32)
        padded = ((c + (MOE_BLOCK - 1)) // MOE_BLOCK) * MOE_BLOCK
        e = lax.broadcasted_iota(I32, (N_EXPERTS, LANES), 0)
        scan = padded
        for d in (1, 2, 4, 8, 16):
            scan = scan + jnp.where(e >= d, pltpu.roll(scan, d, axis=0), 0)
        pstart[...] = (scan - padded)[:, 0:1].astype(F32)
        carry[...] = jnp.zeros_like(carry)

    before = jnp.dot(both.astype(BF16), tri_ref[...], preferred_element_type=F32)
    row_of = before + (carry[...] + pstart[...])
    r1 = jnp.sum(oh1 * row_of, axis=0, keepdims=True)
    r2 = jnp.sum(oh2 * row_of, axis=0, keepdims=True)
    d0_ref[...] = r1.astype(I32)
    d1_ref[...] = r2.astype(I32)
    carry[...] += jnp.sum(both, axis=1, keepdims=True)


def _route(info_t, cnt, cols):
    n = info_t.shape[1]
    tri = jnp.asarray(np.triu(np.ones((cols, cols), np.float32), 1), BF16)
    return pl.pallas_call(
        _route_kernel,
        grid=(n // cols,),
        in_specs=[pl.BlockSpec((SUBLANES, cols), lambda t: (0, t)),
                  pl.BlockSpec((N_EXPERTS, LANES), lambda t: (0, 0)),
                  pl.BlockSpec((cols, cols), lambda t: (0, 0))],
        out_specs=[pl.BlockSpec((1, cols), lambda t: (0, t)),
                   pl.BlockSpec((1, cols), lambda t: (0, t))],
        out_shape=[jax.ShapeDtypeStruct((1, n), I32), jax.ShapeDtypeStruct((1, n), I32)],
        scratch_shapes=[pltpu.VMEM((N_EXPERTS, 1), F32), pltpu.VMEM((N_EXPERTS, 1), F32)],
        compiler_params=_cparams(1),
        name="route",
    )(info_t, cnt, tri)


def _sc_mesh():
    return plsc.VectorSubcoreMesh(core_axis_name="core", subcore_axis_name="subcore")


def _sc_worker_id():
    return lax.axis_index("subcore") * SC_CORES + lax.axis_index("core")


def _sc_scatter_rows(rows, d0, d1, cap):
    n, width = rows.shape
    per_worker = n // SC_WORKERS
    n_win = per_worker // SC_WINDOW

    def body(x_hbm, d0_hbm, d1_hbm, o_hbm, i0_v, i1_v, rows_v, rsem, sem0, sem1):
        wid = _sc_worker_id()
        pltpu.sync_copy(d0_hbm.at[wid], i0_v)
        pltpu.sync_copy(d1_hbm.at[wid], i1_v)

        def read(j):
            src = x_hbm.at[pl.ds(wid * per_worker + j * SC_WINDOW, SC_WINDOW)]
            return pltpu.make_async_copy(src, rows_v.at[j % 2], rsem.at[j % 2])

        def scatters(j):
            return (pltpu.make_async_copy(rows_v.at[j % 2], o_hbm.at[i0_v.at[j]], sem0.at[j % 2]),
                    pltpu.make_async_copy(rows_v.at[j % 2], o_hbm.at[i1_v.at[j]], sem1.at[j % 2]))

        read(0).start()
        for j in range(n_win):
            if j + 1 < n_win:
                if j >= 1:
                    for cp in scatters(j - 1):
                        cp.wait()
                read(j + 1).start()
            read(j).wait()
            for cp in scatters(j):
                cp.start()
        for j in range(max(n_win - 2, 0), n_win):
            for cp in scatters(j):
                cp.wait()

    return pl.kernel(
        body,
        out_type=jax.ShapeDtypeStruct((cap, width), rows.dtype),
        mesh=_sc_mesh(),
        scratch_types=[pltpu.VMEM((n_win, SC_WINDOW), I32), pltpu.VMEM((n_win, SC_WINDOW), I32),
                       pltpu.VMEM((2, SC_WINDOW, width), rows.dtype),
                       pltpu.SemaphoreType.DMA((2,)), pltpu.SemaphoreType.DMA((2,)),
                       pltpu.SemaphoreType.DMA((2,))],
        name="dispatch",
    )(rows, d0, d1)


def _sc_gather_rows(table, idx):
    width = table.shape[1]
    n_win = idx.shape[1]
    per_worker = n_win * SC_WINDOW

    def body(y_hbm, i_hbm, o_hbm, i_v, rows_v, gsem, wsem):
        wid = _sc_worker_id()
        pltpu.sync_copy(i_hbm.at[wid], i_v)

        def gather(j):
            return pltpu.make_async_copy(y_hbm.at[i_v.at[j]], rows_v.at[j % 2], gsem.at[j % 2])

        def write(j):
            dst = o_hbm.at[pl.ds(wid * per_worker + j * SC_WINDOW, SC_WINDOW)]
            return pltpu.make_async_copy(rows_v.at[j % 2], dst, wsem.at[j % 2])

        gather(0).start()
        for j in range(n_win):
            if j + 1 < n_win:
                if j >= 1:
                    write(j - 1).wait()
                gather(j + 1).start()
            gather(j).wait()
            write(j).start()
        for j in range(max(n_win - 2, 0), n_win):
            write(j).wait()

    return pl.kernel(
        body,
        out_type=jax.ShapeDtypeStruct((SC_WORKERS * per_worker, width), table.dtype),
        mesh=_sc_mesh(),
        scratch_types=[pltpu.VMEM((n_win, SC_WINDOW), I32),
                       pltpu.VMEM((2, SC_WINDOW, width), table.dtype),
                       pltpu.SemaphoreType.DMA((2,)), pltpu.SemaphoreType.DMA((2,))],
        name="collect",
    )(table, idx)


def _expert_kernel(bstart_ref, nblk_ref, nused_ref, xs_hbm, wg_ref, wu_ref, wd_ref, yb_hbm,
                   xbuf, ybuf, zbuf, xsem, ysem, zsem, wg_b, wu_b, wd_b):
    e = pl.program_id(0)
    nused = nused_ref[0]
    n_blocks = yb_hbm.shape[0] // MOE_BLOCK

    def rows(b):
        return pl.ds(pl.multiple_of(b * MOE_BLOCK, MOE_BLOCK), MOE_BLOCK)

    def x_copy(b):
        slot = b % X_RING
        return pltpu.make_async_copy(xs_hbm.at[rows(b)], xbuf.at[slot], xsem.at[slot])

    def y_copy(b):
        slot = b % Y_RING
        return pltpu.make_async_copy(ybuf.at[slot], yb_hbm.at[rows(b)], ysem.at[slot])

    @pl.when(e == 0)
    def _():
        for b in range(X_AHEAD):
            @pl.when(b < nused)
            def _():
                x_copy(b).start()

    wg_b[...] = wg_ref[...].astype(BF16)
    wu_b[...] = wu_ref[...].astype(BF16)
    wd_b[...] = wd_ref[...].astype(BF16)

    def run_blocks(b, count):
        for k in range(count):
            x_copy(b + k).wait()
        for k in range(count):
            nxt = b + X_AHEAD + k

            @pl.when(nxt < nused)
            def _():
                x_copy(nxt).start()

            @pl.when(b + k >= Y_RING)
            def _():
                y_copy(b + k - Y_RING).wait()

        words = jnp.concatenate([xbuf[(b + k) % X_RING] for k in range(count)], axis=0)
        lo, hi = _unpack_rows(words)
        x = jnp.concatenate([lo.astype(BF16), hi.astype(BF16)], axis=1)
        g = jnp.dot(x, wg_b[...], preferred_element_type=F32)
        u = jnp.dot(x, wu_b[...], preferred_element_type=F32)
        mid = (g * _sigmoid(g) * u).astype(BF16)
        y = _pack_rows(jnp.dot(mid, wd_b[...], preferred_element_type=F32))
        for k in range(count):
            ybuf[(b + k) % Y_RING] = y[k * MOE_BLOCK:(k + 1) * MOE_BLOCK]
            y_copy(b + k).start()

    b0 = bstart_ref[e]
    nb = nblk_ref[e]

    def group(i, carry):
        run_blocks(b0 + X_GROUP * i, X_GROUP)
        return carry

    lax.fori_loop(0, nb // X_GROUP, group, 0)
    done = nb - nb % X_GROUP
    size = X_GROUP // 2
    while size >= 1:
        @pl.when((nb // size) % 2 == 1)
        def _(size=size, done=done):
            run_blocks(b0 + done, size)

        done = done + (nb // size) % 2 * size
        size //= 2

    @pl.when(e == pl.num_programs(0) - 1)
    def _():
        for back in range(Y_RING, 0, -1):
            @pl.when(nused >= back)
            def _():
                y_copy(nused - back).wait()

        zbuf[...] = jnp.zeros_like(zbuf)

        def z_copy(b):
            return pltpu.make_async_copy(zbuf, yb_hbm.at[rows(b)], zsem.at[0])

        def z_start(b, carry):
            z_copy(b).start()
            return carry

        def z_wait(b, carry):
            z_copy(b).wait()
            return carry

        lax.fori_loop(nused, n_blocks, z_start, 0)
        lax.fori_loop(nused, n_blocks, z_wait, 0)


def _experts(bstart, nblk, nused, xs, w_gate, w_up, w_down):
    cap = xs.shape[0]
    w_idx = lambda e, bs, nb, nu: (e, 0, 0)
    grid_spec = pltpu.PrefetchScalarGridSpec(
        num_scalar_prefetch=3,
        grid=(N_EXPERTS,),
        in_specs=[pl.BlockSpec(memory_space=pl.ANY),
                  pl.BlockSpec((None, D_MODEL, D_FF), w_idx),
                  pl.BlockSpec((None, D_MODEL, D_FF), w_idx),
                  pl.BlockSpec((None, D_FF, D_MODEL), w_idx)],
        out_specs=pl.BlockSpec(memory_space=pl.ANY),
        scratch_shapes=[pltpu.VMEM((X_RING, MOE_BLOCK, PACKED), U32),
                        pltpu.VMEM((Y_RING, MOE_BLOCK, PACKED), U32),
                        pltpu.VMEM((MOE_BLOCK, PACKED), U32),
                        pltpu.SemaphoreType.DMA((X_RING,)),
                        pltpu.SemaphoreType.DMA((Y_RING,)),
                        pltpu.SemaphoreType.DMA((1,)),
                        pltpu.VMEM((D_MODEL, D_FF), BF16),
                        pltpu.VMEM((D_MODEL, D_FF), BF16),
                        pltpu.VMEM((D_FF, D_MODEL), BF16)])
    return pl.pallas_call(
        _expert_kernel,
        grid_spec=grid_spec,
        out_shape=jax.ShapeDtypeStruct((cap, PACKED), U32),
        compiler_params=_cparams(1),
        name="experts",
    )(bstart, nblk, nused, xs, w_gate, w_up, w_down)


def _combine_kernel(y0_ref, y1_ref, h_ref, info_ref, g_ref, b_ref, o_ref):
    info = info_ref[...].T
    g0 = info[:, 2:3]
    g1 = info[:, 3:4]
    lo0, hi0 = _unpack_rows(y0_ref[...])
    lo1, hi1 = _unpack_rows(y1_ref[...])
    y = jnp.concatenate([g0 * lo0 + g1 * lo1, g0 * hi0 + g1 * hi1], axis=1)
    o_ref[...] = _layer_norm(ALPHA * h_ref[...] + y, g_ref[...], b_ref[...])


def _combine(ys, h, info, ln_g, ln_b, rows, part, n_parts):
    n = h.shape[0]
    steps = n // n_parts // rows
    off = part * steps
    const = lambda i: (0, 0)
    return pl.pallas_call(
        _combine_kernel,
        grid=(steps,),
        in_specs=[pl.BlockSpec((rows, PACKED), lambda i: (i, 0)),
                  pl.BlockSpec((rows, PACKED), lambda i: (i + steps, 0)),
                  pl.BlockSpec((rows, D_MODEL), lambda i: (i + off, 0)),
                  pl.BlockSpec((SUBLANES, rows), lambda i: (0, i + off)),
                  pl.BlockSpec((1, D_MODEL), const),
                  pl.BlockSpec((1, D_MODEL), const)],
        out_specs=pl.BlockSpec((rows, D_MODEL), lambda i: (i + off, 0)),
        out_shape=jax.ShapeDtypeStruct((n, D_MODEL), F32),
        input_output_aliases={2: 0},
        compiler_params=_cparams(1),
        name="combine",
    )(ys, ys, h, info, ln_g, ln_b)


def _alibi_bias():
    qi = np.arange(BLOCK)[:, None]
    kj = np.arange(2 * BLOCK)[None, :]
    dist = qi - kj + BLOCK
    band = (dist >= 0) & (dist < BLOCK)
    slopes = np.exp2(-8.0 * np.arange(1, N_Q_HEADS + 1, dtype=np.float32) / N_Q_HEADS)
    bias = np.where(band[None], -slopes[:, None, None] * dist[None].astype(np.float32), NEG)
    bias = bias * LOG2E
    first = np.where((kj >= PAD_FRONT)[None], bias, NEG)
    out = np.empty((2, N_KV_HEADS, 2 * BLOCK, 4 * BLOCK), np.float32)
    for v, per_head in enumerate((first, bias)):
        for j in range(N_KV_HEADS):
            out[v, j] = np.block([[per_head[4 * j], per_head[4 * j + 1]],
                                  [per_head[4 * j + 2], per_head[4 * j + 3]]])
    return jnp.asarray(out, F32)


def _slab_gates(w):
    nb, c, _ = w.shape
    per = LANES // c
    w = w.reshape(nb // per, per, c, c)
    eye = jnp.eye(per, dtype=w.dtype)
    return jnp.einsum('spcd,pq->spcqd', w, eye).reshape(nb // per, LANES, LANES).astype(BF16)


def kernel(x, meta_tokens, w_in, conv_w, conv_b, lru_wa, lru_ba, lru_wx, lru_bx, lru_lambda,
           attn_sinks, g_attn, g_lru, w_out, ln1_g, ln1_b, w_group, b_group, w_router,
           b_router, w_gate, w_up, w_down, ln2_g, ln2_b):
    bsz, seq, d = x.shape
    nbx = seq // BLOCK
    n_tok = bsz * seq
    x2d = x.reshape(n_tok, d)
    row = lambda v: v.reshape(1, -1).astype(F32)

    q_scale = jnp.concatenate([jnp.full((ATTN_WIDTH,), LOG2E * HEAD_DIM ** -0.5, F32),
                               jnp.ones((IN_COLS - ATTN_WIDTH,), F32)])
    w_in_b = (w_in[0] * q_scale).astype(BF16)
    meta_blk = jnp.concatenate([jnp.zeros((PAD_FRONT, d), F32), meta_tokens.astype(F32)], axis=0)
    q, kv, xr, yr = _in_proj(x2d, w_in_b, PROJ_ROWS)
    qm, kvm, xrm, yrm = _in_proj(meta_blk, w_in_b, BLOCK)
    shp = lambda a: a.reshape(bsz, seq, a.shape[-1])

    attn_n = _attention(attn_sinks[0].astype(F32) * LOG2E, shp(q), shp(kv), kvm, _alibi_bias(),
                        row(g_attn[0]), bsz, nbx)
    lru_n = _rglru(shp(xr), shp(yr), xrm, yrm, conv_w[0].astype(F32), row(conv_b[0]),
                   _slab_gates(0.5 * lru_wa[0]), _slab_gates(0.5 * lru_wx[0]),
                   0.5 * row(lru_ba[0]), 0.5 * row(lru_bx[0]), row(lru_lambda[0]),
                   row(g_lru[0]), bsz, nbx)

    gpad = SUBLANES - N_GROUPS
    tail = ROUTER_ROWS - SUBLANES - N_EXPERTS
    w_rt = jnp.concatenate(
        [w_group[0].T, jnp.zeros((gpad, d), F32),
         jnp.transpose(w_router[0], (0, 2, 1)).reshape(N_EXPERTS, d),
         jnp.zeros((tail, d), F32)], axis=0).astype(F32)
    b_rt = jnp.concatenate([b_group[0], jnp.zeros((gpad,), F32), b_router[0].reshape(-1),
                            jnp.zeros((tail,), F32)]).astype(F32).reshape(ROUTER_ROWS, 1)
    h1, hp, info, cnt = _out_proj(
        attn_n.reshape(n_tok, ATTN_WIDTH), lru_n.reshape(n_tok, LRU_WIDTH), x2d,
        w_out[0].astype(F32), row(ln1_g[0]), row(ln1_b[0]), w_rt, b_rt, OUT_PROJ_ROWS)

    d0, d1 = (v.reshape(n_tok) for v in _route(info, cnt, ROUTE_ROWS))
    n_slots = n_tok * TOP_K
    n_blocks = n_slots // MOE_BLOCK + N_EXPERTS
    cap = n_blocks * MOE_BLOCK
    nblk = (cnt[:, 0].astype(I32) + MOE_BLOCK - 1) // MOE_BLOCK
    bends = jnp.cumsum(nblk)
    bstart = (bends - nblk).astype(I32)
    nused = bends[-1:].astype(I32)
    windows = lambda v: v.reshape(SC_WORKERS, -1, SC_WINDOW)

    xs = _sc_scatter_rows(hp, windows(d0), windows(d1), cap)
    yb = _experts(bstart, nblk.astype(I32), nused, xs, w_gate[0], w_up[0], w_down[0])
    out = h1
    part_len = n_tok // COMBINE_PARTS
    for part in range(COMBINE_PARTS):
        tok = slice(part * part_len, (part + 1) * part_len)
        ys = _sc_gather_rows(yb, windows(jnp.concatenate([d0[tok], d1[tok]])))
        out = _combine(ys, out, info, row(ln2_g[0]), row(ln2_b[0]), COMBINE_ROWS,
                       part, COMBINE_PARTS)
    return out.reshape(bsz, seq, d)
```

```python
import jax
import jax.numpy as jnp
import numpy as np
from jax import lax
from jax.experimental import pallas as pl
from jax.experimental.pallas import tpu as pltpu
from jax.experimental.pallas import tpu_sc as plsc

F32 = jnp.float32
BF16 = jnp.bfloat16
U32 = jnp.uint32
I32 = jnp.int32

D_MODEL = 1024
N_META = 16
BLOCK = 128
PAD_FRONT = BLOCK - N_META
HEAD_DIM = 64
ATTN_WIDTH = 512
LRU_WIDTH = 512
N_Q_HEADS = 8
N_KV_HEADS = 2
KV_WIDTH = N_KV_HEADS * HEAD_DIM
LRU_BLOCKS = 8
CONV_W = 4
LRU_C = 8.0
IN_COLS = ATTN_WIDTH + 2 * KV_WIDTH + 2 * LRU_WIDTH
N_GROUPS = 4
EXPERTS_PER_GROUP = 8
N_EXPERTS = N_GROUPS * EXPERTS_PER_GROUP
TOP_K = 2
D_FF = 512
MOE_BLOCK = 256
ALPHA = 2.0 ** 0.25
EPS = 1e-5
NEG = -1e30
LOG2E = float(np.log2(np.e))
LANES = 128
SUBLANES = 8
PACKED = D_MODEL // 2

PROJ_ROWS = 1024
OUT_PROJ_ROWS = 1024
ROUTE_ROWS = 512
COMBINE_ROWS = 1024
COMBINE_PARTS = 4
X_GROUP = 4
X_AHEAD = 4
X_RING = X_AHEAD + X_GROUP
Y_RING = 2 * X_GROUP
VMEM_LIMIT = 48 * 1024 * 1024

SC_CORES = 2
SC_SUBCORES = 16
SC_WORKERS = SC_CORES * SC_SUBCORES
SC_WINDOW = 64


def _cparams(n_axes):
    return pltpu.CompilerParams(
        dimension_semantics=("arbitrary",) * n_axes, vmem_limit_bytes=VMEM_LIMIT)


IN_SPLITS = (ATTN_WIDTH, 2 * KV_WIDTH, LRU_WIDTH, LRU_WIDTH)


def _in_proj_kernel(x_ref, meta_ref, w_ref, scale_ref, q_ref, kv_ref, xr_ref, yr_ref,
                    kvm_ref, xrm_ref, yrm_ref, w_b):
    i = pl.program_id(0)
    tiles = pl.num_programs(0) - 1

    @pl.when(i == 0)
    def _():
        w_b[...] = (w_ref[...] * scale_ref[...]).astype(BF16)

    def project(rows_ref, outs, first_col):
        proj = jnp.dot(rows_ref[...].astype(BF16), w_b[:, first_col:],
                       preferred_element_type=F32)
        o = 0
        for ref in outs:
            ref[...] = proj[:, o:o + ref.shape[1]].astype(ref.dtype)
            o += ref.shape[1]

    @pl.when(i < tiles)
    def _():
        project(x_ref, (q_ref, kv_ref, xr_ref, yr_ref), 0)

    @pl.when(i == tiles)
    def _():
        project(meta_ref, (kvm_ref, xrm_ref, yrm_ref), ATTN_WIDTH)


def _in_proj(x2d, meta_blk, w, col_scale, rows):
    n = x2d.shape[0]
    tiles = n // rows
    tile = lambda w_: pl.BlockSpec((rows, w_), lambda i: (jnp.minimum(i, tiles - 1), 0))
    const = lambda i: (0, 0)
    meta_rows = meta_blk.shape[0]
    return pl.pallas_call(
        _in_proj_kernel,
        grid=(tiles + 1,),
        in_specs=[tile(D_MODEL),
                  pl.BlockSpec((meta_rows, D_MODEL), const),
                  pl.BlockSpec((D_MODEL, IN_COLS), const),
                  pl.BlockSpec((1, IN_COLS), const)],
        out_specs=[tile(w_) for w_ in IN_SPLITS]
        + [pl.BlockSpec((meta_rows, w_), const) for w_ in IN_SPLITS[1:]],
        out_shape=[jax.ShapeDtypeStruct((n, w_), BF16) for w_ in IN_SPLITS]
        + [jax.ShapeDtypeStruct((meta_rows, w_), BF16) for w_ in IN_SPLITS[1:]],
        scratch_shapes=[pltpu.VMEM((D_MODEL, IN_COLS), BF16)],
        compiler_params=_cparams(1),
        name="in_proj",
    )(x2d, meta_blk, w, col_scale)


def _attn_kernel(sinks_ref, q_ref, kv_ref, kvm_ref, bias_ref, g_ref, o_ref,
                 klo, khi, vlo, vhi):
    nbx = q_ref.shape[0] // BLOCK
    lo_lanes = lax.broadcasted_iota(I32, (BLOCK // 2, LANES), 1) < HEAD_DIM

    def layout_block(n, blk):
        rows = pl.ds(pl.multiple_of(n * BLOCK, BLOCK), BLOCK)
        as_bf16 = lambda words: pltpu.bitcast(words, BF16)
        for src, dst_lo, dst_hi in ((blk[:, :KV_WIDTH], klo, khi), (blk[:, KV_WIDTH:], vlo, vhi)):
            w = pltpu.bitcast(src, U32)
            r = pltpu.roll(w, HEAD_DIM, axis=1)
            zero = jnp.zeros_like(w)
            dst_lo[0, rows, :] = as_bf16(jnp.where(lo_lanes, w, zero))
            dst_hi[0, rows, :] = as_bf16(jnp.where(lo_lanes, zero, r))
            dst_lo[1, rows, :] = as_bf16(jnp.where(lo_lanes, r, zero))
            dst_hi[1, rows, :] = as_bf16(jnp.where(lo_lanes, zero, w))

    layout_block(0, kvm_ref[...])

    def layout_body(n, carry):
        layout_block(n + 1, kv_ref[pl.ds(pl.multiple_of(n * BLOCK, BLOCK), BLOCK), :])
        return carry

    lax.fori_loop(0, nbx, layout_body, 0, unroll=4)

    ones_lo = jnp.where(lax.broadcasted_iota(I32, (2 * BLOCK, LANES), 1) < HEAD_DIM,
                        1.0, 0.0).astype(BF16)
    ones_hi = (1.0 - ones_lo.astype(F32)).astype(BF16)
    top_rows = lax.broadcasted_iota(I32, (2 * BLOCK, 1), 0) < BLOCK
    lo_half = lax.broadcasted_iota(I32, (2 * BLOCK, LANES), 1) < HEAD_DIM

    def block(i, carry):
        q_rows = pl.ds(pl.multiple_of(i * BLOCK, BLOCK), BLOCK)
        win = pl.ds(pl.multiple_of(i * BLOCK, BLOCK), 2 * BLOCK)
        q = q_ref[q_rows, :]
        first = jnp.minimum(i, 1)
        outs = []
        for j in range(N_KV_HEADS):
            q2 = jnp.concatenate([q[:, (2 * j) * LANES:(2 * j + 1) * LANES],
                                  q[:, (2 * j + 1) * LANES:(2 * j + 2) * LANES]], axis=0)
            kc = jnp.concatenate([klo[j, win, :], khi[j, win, :]], axis=0)
            s = lax.dot_general(q2, kc, (((1,), (1,)), ((), ())), preferred_element_type=F32)
            s = s + bias_ref[first, j]
            ps, es = [], []
            for c in range(2):
                sink = jnp.where(top_rows, sinks_ref[4 * j + c], sinks_ref[4 * j + 2 + c])
                sc = s[:, c * 2 * BLOCK:(c + 1) * 2 * BLOCK]
                m = jnp.maximum(jnp.max(sc, axis=1, keepdims=True), sink)
                ps.append(jnp.exp2(sc - m).astype(BF16))
                es.append(jnp.exp2(sink - m))
            v_lo = jnp.concatenate([vlo[j, win, :], ones_lo], axis=1)
            v_hi = jnp.concatenate([vhi[j, win, :], ones_hi], axis=1)
            r = jnp.dot(jnp.concatenate(ps, axis=1), jnp.concatenate([v_lo, v_hi], axis=0),
                        preferred_element_type=F32)
            den = r[:, LANES:] + jnp.where(lo_half, es[0], es[1])
            o2 = r[:, :LANES] * (1.0 / den)
            outs += [o2[:BLOCK], o2[BLOCK:]]
        out = jnp.concatenate(outs, axis=1)
        ms = jnp.mean(out * out, axis=1, keepdims=True)
        o_ref[q_rows, :] = (out * lax.rsqrt(ms + EPS) * g_ref[...]).astype(o_ref.dtype)
        return carry

    lax.fori_loop(0, nbx, block, 0, unroll=8)


def _attention(sinks, q, kv, kvm, bias, g_attn, bsz, nbx):
    seq = nbx * BLOCK
    const2 = lambda b: (0, 0)
    kv_scratch = pltpu.VMEM((N_KV_HEADS, seq + BLOCK, LANES), BF16)
    return pl.pallas_call(
        _attn_kernel,
        grid=(bsz,),
        in_specs=[pl.BlockSpec(memory_space=pltpu.SMEM),
                  pl.BlockSpec((None, seq, ATTN_WIDTH), lambda b: (b, 0, 0)),
                  pl.BlockSpec((None, seq, 2 * KV_WIDTH), lambda b: (b, 0, 0)),
                  pl.BlockSpec((BLOCK, 2 * KV_WIDTH), const2),
                  pl.BlockSpec((2, N_KV_HEADS, 2 * BLOCK, 4 * BLOCK), lambda b: (0, 0, 0, 0)),
                  pl.BlockSpec((1, ATTN_WIDTH), const2)],
        out_specs=pl.BlockSpec((None, seq, ATTN_WIDTH), lambda b: (b, 0, 0)),
        out_shape=jax.ShapeDtypeStruct((bsz, seq, ATTN_WIDTH), BF16),
        scratch_shapes=[kv_scratch, kv_scratch, kv_scratch, kv_scratch],
        compiler_params=_cparams(1),
        name="attention",
    )(sinks, q, kv, kvm, bias, g_attn)


def _sigmoid(v):
    return 0.5 * jnp.tanh(0.5 * v) + 0.5


def _gelu_tanh(y):
    c = float(np.sqrt(2.0 / np.pi))
    half = 0.5 * y
    return half + half * jnp.tanh(y * (c + (c * 0.044715) * (y * y)))


LRU_CHUNK = 44
LRU_SEG = SUBLANES * LRU_CHUNK
LRU_SLABS = LRU_WIDTH // LANES


def _lru_kernel(xr_ref, yr_ref, xrm_ref, yrm_ref, cw_ref, cb_ref, wa_ref, wx_ref, ba_ref,
                bx_ref, lam_ref, g_ref, o_ref, x_st, y_st, o_st, s_st, xtail, hcar):
    seq = xr_ref.shape[0]
    n_seg = (seq + BLOCK) // LRU_SEG
    xtail[...] = jnp.zeros_like(xtail)
    hcar[...] = jnp.zeros_like(hcar)
    lam = lam_ref[...]
    softplus_neg = jnp.maximum(-lam, 0.0) + jnp.log(1.0 + jnp.exp(-jnp.abs(lam)))
    sub = lax.broadcasted_iota(jnp.int32, (SUBLANES, LANES), 0)

    def strided(j):
        return pl.ds(j, SUBLANES, stride=LRU_CHUNK)

    def piece(v, j):
        return v[j * SUBLANES:(j + 1) * SUBLANES, :]

    def segment(k, first):
        if first:
            head = LRU_SEG - BLOCK
            x_nat = jnp.concatenate([xrm_ref[...], xr_ref[0:head, :]], axis=0).astype(F32)
            y_nat = jnp.concatenate([yrm_ref[...], yr_ref[0:head, :]], axis=0).astype(F32)
        else:
            rows = pl.ds(pl.multiple_of(k * LRU_SEG - BLOCK, 2 * SUBLANES), LRU_SEG)
            x_nat = xr_ref[rows, :].astype(F32)
            y_nat = yr_ref[rows, :].astype(F32)
        for c in range(LRU_SLABS):
            x_st[c] = x_nat[:, c * LANES:(c + 1) * LANES]
            y_st[c] = y_nat[:, c * LANES:(c + 1) * LANES]
        first_row = k * LRU_SEG + LRU_CHUNK * sub
        sumsq = [jnp.zeros((SUBLANES, LANES), F32) for _ in range(LRU_CHUNK)]

        for c in range(LRU_SLABS):
            lanes = slice(c * LANES, (c + 1) * LANES)
            x = [x_st[c, strided(j), :] for j in range(LRU_CHUNK)]
            before = []
            for d in range(1, CONV_W):
                from_prev_chunk = pltpu.roll(x[LRU_CHUNK - d], 1, axis=0)
                before.append(jnp.where(sub == 0, xtail[d - 1:d, lanes], from_prev_chunk))
            for d in range(1, CONV_W):
                xtail[d - 1:d, lanes] = x[LRU_CHUNK - d][SUBLANES - 1:SUBLANES, :]

            def x_at(j):
                return x[j] if j >= 0 else before[-j - 1]

            taps = [cw_ref[t:t + 1, lanes] for t in range(CONV_W)]
            bias = cb_ref[:, lanes]
            xc = jnp.concatenate(
                [bias + sum(taps[t] * x_at(j - (CONV_W - 1) + t) for t in range(CONV_W))
                 for j in range(LRU_CHUNK)], axis=0)
            xcb = xc.astype(BF16)
            both = jnp.dot(xcb, jnp.concatenate([wa_ref[c], wx_ref[c]], axis=1),
                           preferred_element_type=F32)
            tr = jnp.tanh(both[:, :LANES] + ba_ref[:, lanes])
            ti = jnp.tanh(both[:, LANES:] + bx_ref[:, lanes])
            log_a_half = (-0.5 * LRU_C) * softplus_neg[:, lanes]
            a = jnp.exp(log_a_half * tr + log_a_half)
            half_xc = 0.5 * xc
            gated_x = half_xc * ti + half_xc
            z = 1.0 - a * a
            u = jnp.where(z > 0.0, z * lax.rsqrt(z), 0.0) * gated_x

            h = jnp.zeros((SUBLANES, LANES), F32)
            p = jnp.ones((SUBLANES, LANES), F32)
            hs, ps = [], []
            for j in range(LRU_CHUNK):
                aj = piece(a, j)
                uj = piece(u, j)
                if first:
                    uj = jnp.where(first_row + j >= PAD_FRONT, uj, 0.0)
                h = aj * h + uj
                p = aj * p
                hs.append(h)
                ps.append(p)
            entry = [hcar[:, lanes]]
            for s in range(SUBLANES):
                entry.append(h[s:s + 1, :] + p[s:s + 1, :] * entry[s])
            hcar[:, lanes] = entry[SUBLANES]
            entry_rows = jnp.concatenate(entry[:SUBLANES], axis=0)

            for j in range(LRU_CHUNK):
                state = hs[j] + ps[j] * entry_rows
                out = state * _gelu_tanh(y_st[c, strided(j), :])
                sumsq[j] = sumsq[j] + out * out
                o_st[c, strided(j), :] = out

        for j in range(LRU_CHUNK):
            ms = jnp.sum(sumsq[j], axis=1, keepdims=True) * (1.0 / LRU_WIDTH)
            s_st[strided(j), :] = jnp.broadcast_to(lax.rsqrt(ms + EPS), (SUBLANES, LANES))
        scale = s_st[...]
        for c in range(LRU_SLABS):
            lanes = slice(c * LANES, (c + 1) * LANES)
            normed = (o_st[c] * scale * g_ref[:, lanes]).astype(o_ref.dtype)
            if first:
                o_ref[0:LRU_SEG - BLOCK, lanes] = normed[BLOCK:, :]
            else:
                o_ref[rows, lanes] = normed

    assert BLOCK <= LRU_SEG
    segment(0, True)

    def later_segment(k, carry):
        segment(k, False)
        return carry

    lax.fori_loop(1, n_seg, later_segment, 0)


def _rglru(xr, yr, xrm, yrm, cw, cb, wa, wx, ba, bx, lam, g_lru, bsz, nbx):
    seq = nbx * BLOCK
    assert (seq + BLOCK) % LRU_SEG == 0
    main = pl.BlockSpec((None, seq, LRU_WIDTH), lambda b: (b, 0, 0))
    const2 = lambda b: (0, 0)
    row_spec = pl.BlockSpec((1, LRU_WIDTH), const2)
    gate_spec = pl.BlockSpec((LRU_SLABS, LANES, LANES), lambda b: (0, 0, 0))
    slabs = pltpu.VMEM((LRU_SLABS, LRU_SEG, LANES), F32)
    return pl.pallas_call(
        _lru_kernel,
        grid=(bsz,),
        in_specs=[main, main,
                  pl.BlockSpec((BLOCK, LRU_WIDTH), const2),
                  pl.BlockSpec((BLOCK, LRU_WIDTH), const2),
                  pl.BlockSpec((CONV_W, LRU_WIDTH), const2),
                  row_spec, gate_spec, gate_spec,
                  row_spec, row_spec, row_spec, row_spec],
        out_specs=main,
        out_shape=jax.ShapeDtypeStruct((bsz, seq, LRU_WIDTH), BF16),
        scratch_shapes=[slabs, slabs, slabs,
                        pltpu.VMEM((LRU_SEG, LANES), F32),
                        pltpu.VMEM((SUBLANES, LRU_WIDTH), F32),
                        pltpu.VMEM((1, LRU_WIDTH), F32)],
        compiler_params=_cparams(1),
        name="rglru",
    )(xr, yr, xrm, yrm, cw, cb, wa, wx, ba, bx, lam, g_lru)


def _pack_rows(v):
    bits = lax.bitcast_convert_type(v.astype(BF16).astype(F32), U32)
    return (bits[:, :PACKED] >> 16) | (bits[:, PACKED:] & jnp.uint32(0xFFFF0000))


def _unpack_rows(w):
    lo = lax.bitcast_convert_type(w << 16, F32)
    hi = lax.bitcast_convert_type(w & jnp.uint32(0xFFFF0000), F32)
    return lo, hi


def _layer_norm(z, g, b):
    mu = jnp.mean(z, axis=1, keepdims=True)
    zc = z - mu
    var = jnp.mean(zc * zc, axis=1, keepdims=True)
    return zc * lax.rsqrt(var + EPS) * g + b


def _out_proj_kernel(a_ref, l_ref, x_ref, w_ref, g_ref, b_ref, wrt_ref, brt_ref,
                     h_ref, hp_ref, info_ref, cnt_ref, w_b, wrt_b):
    @pl.when(pl.program_id(0) == 0)
    def _():
        cnt_ref[...] = jnp.zeros_like(cnt_ref)
        w_b[...] = w_ref[...].astype(BF16)
        hi = wrt_ref[...].astype(BF16)
        wrt_b[0:ROUTER_ROWS, :] = hi
        wrt_b[ROUTER_ROWS:, :] = (wrt_ref[...] - hi.astype(F32)).astype(BF16)

    mix = jnp.dot(jnp.concatenate([a_ref[...], l_ref[...]], axis=1), w_b[...],
                  preferred_element_type=F32)
    h = _layer_norm(ALPHA * x_ref[...] + mix, g_ref[...], b_ref[...])
    h_ref[...] = h
    hp_ref[...] = _pack_rows(h)

    h_hi = h.astype(BF16)
    h_lo = (h - h_hi.astype(F32)).astype(BF16)
    nt = (((1,), (1,)), ((), ()))
    both = lax.dot_general(wrt_b[...], h_hi, nt, preferred_element_type=F32)
    lg = (both[:ROUTER_ROWS] + both[ROUTER_ROWS:]
          + lax.dot_general(wrt_b[0:ROUTER_ROWS, :], h_lo, nt, preferred_element_type=F32)
          ) + brt_ref[...]
    tile_shape = (SUBLANES, h.shape[0])
    sub = lax.broadcasted_iota(I32, tile_shape, 0)
    ninf = -jnp.inf
    t0 = lg[0:SUBLANES]
    gl = jnp.where(sub < N_GROUPS, t0, ninf)
    gmax = jnp.max(gl, axis=0, keepdims=True)
    g_idx = jnp.min(jnp.where(gl == gmax, sub, SUBLANES), axis=0, keepdims=True)
    g_w = 1.0 / jnp.sum(jnp.where(sub < N_GROUPS, jnp.exp(t0 - gmax), 0.0),
                        axis=0, keepdims=True)
    el = lg[SUBLANES:2 * SUBLANES]
    for g in range(1, N_GROUPS):
        el = jnp.where(g_idx == g, lg[(g + 1) * SUBLANES:(g + 2) * SUBLANES], el)
    v1 = jnp.max(el, axis=0, keepdims=True)
    i1 = jnp.min(jnp.where(el == v1, sub, SUBLANES), axis=0, keepdims=True)
    el2 = jnp.where(sub == i1, ninf, el)
    v2 = jnp.max(el2, axis=0, keepdims=True)
    i2 = jnp.min(jnp.where(el2 == v2, sub, SUBLANES), axis=0, keepdims=True)
    t = jnp.exp(v2 - v1)
    w1 = 1.0 / (1.0 + t)
    w2 = t * w1
    e_base = g_idx * EXPERTS_PER_GROUP
    e1 = e_base + i1
    e2 = e_base + i2
    info_ref[...] = jnp.where(sub == 0, e1.astype(F32),
                              jnp.where(sub == 1, e2.astype(F32),
                                        jnp.where(sub == 2, g_w * w1,
                                                  jnp.where(sub == 3, g_w * w2, 0.0))))
    expert = lax.broadcasted_iota(I32, (N_EXPERTS, h.shape[0]), 0)
    chosen = (expert == e1).astype(F32) + (expert == e2).astype(F32)
    cnt_ref[...] += jnp.sum(chosen, axis=1, keepdims=True)


ROUTER_ROWS = -(-(N_GROUPS + 1) * SUBLANES // (2 * SUBLANES)) * (2 * SUBLANES)


def _out_proj(attn_n, lru_n, x2d, w_out, ln_g, ln_b, w_rt, b_rt, rows):
    n = x2d.shape[0]
    const = lambda i: (0, 0)
    tile = lambda w: pl.BlockSpec((rows, w), lambda i: (i, 0))
    return pl.pallas_call(
        _out_proj_kernel,
        grid=(n // rows,),
        in_specs=[tile(ATTN_WIDTH), tile(LRU_WIDTH), tile(D_MODEL),
                  pl.BlockSpec((D_MODEL, D_MODEL), const),
                  pl.BlockSpec((1, D_MODEL), const),
                  pl.BlockSpec((1, D_MODEL), const),
                  pl.BlockSpec((ROUTER_ROWS, D_MODEL), const),
                  pl.BlockSpec((ROUTER_ROWS, 1), const)],
        out_specs=[tile(D_MODEL), tile(PACKED),
                   pl.BlockSpec((SUBLANES, rows), lambda i: (0, i)),
                   pl.BlockSpec((N_EXPERTS, LANES), const)],
        out_shape=[jax.ShapeDtypeStruct((n, D_MODEL), F32),
                   jax.ShapeDtypeStruct((n, PACKED), U32),
                   jax.ShapeDtypeStruct((SUBLANES, n), F32),
                   jax.ShapeDtypeStruct((N_EXPERTS, LANES), F32)],
        scratch_shapes=[pltpu.VMEM((D_MODEL, D_MODEL), BF16),
                        pltpu.VMEM((2 * ROUTER_ROWS, D_MODEL), BF16)],
        compiler_params=_cparams(1),
        name="out_proj",
    )(attn_n, lru_n, x2d, w_out, ln_g, ln_b, w_rt, b_rt)


def _route_kernel(info_ref, cnt_ref, tri_ref, d0_ref, d1_ref, carry, pstart):
    t = pl.program_id(0)
    info = info_ref[...]
    shape = (N_EXPERTS, info.shape[1])
    expert = lax.broadcasted_iota(I32, shape, 0)
    oh1 = (expert == info[0:1, :].astype(I32)).astype(F32)
    oh2 = (expert == info[1:2, :].astype(I32)).astype(F32)
    both = oh1 + oh2

    @pl.when(t == 0)
    def _():
        c = cnt_ref[...].astype(I32)
        padded = ((c + (MOE_BLOCK - 1)) // MOE_BLOCK) * MOE_BLOCK
        e = lax.broadcasted_iota(I32, (N_EXPERTS, LANES), 0)
        scan = padded
        for d in (1, 2, 4, 8, 16):
            scan = scan + jnp.where(e >= d, pltpu.roll(scan, d, axis=0), 0)
        pstart[...] = (scan - padded)[:, 0:1].astype(F32)
        carry[...] = jnp.zeros_like(carry)

    before = jnp.dot(both.astype(BF16), tri_ref[...], preferred_element_type=F32)
    row_of = before + (carry[...] + pstart[...])
    r1 = jnp.sum(oh1 * row_of, axis=0, keepdims=True)
    r2 = jnp.sum(oh2 * row_of, axis=0, keepdims=True)
    d0_ref[...] = r1.astype(I32)
    d1_ref[...] = r2.astype(I32)
    carry[...] += jnp.sum(both, axis=1, keepdims=True)


def _route(info_t, cnt, cols):
    n = info_t.shape[1]
    tri = jnp.asarray(np.triu(np.ones((cols, cols), np.float32), 1), BF16)
    return pl.pallas_call(
        _route_kernel,
        grid=(n // cols,),
        in_specs=[pl.BlockSpec((SUBLANES, cols), lambda t: (0, t)),
                  pl.BlockSpec((N_EXPERTS, LANES), lambda t: (0, 0)),
                  pl.BlockSpec((cols, cols), lambda t: (0, 0))],
        out_specs=[pl.BlockSpec((1, cols), lambda t: (0, t)),
                   pl.BlockSpec((1, cols), lambda t: (0, t))],
        out_shape=[jax.ShapeDtypeStruct((1, n), I32), jax.ShapeDtypeStruct((1, n), I32)],
        scratch_shapes=[pltpu.VMEM((N_EXPERTS, 1), F32), pltpu.VMEM((N_EXPERTS, 1), F32)],
        compiler_params=_cparams(1),
        name="route",
    )(info_t, cnt, tri)


def _sc_mesh():
    return plsc.VectorSubcoreMesh(core_axis_name="core", subcore_axis_name="subcore")


def _sc_worker_id():
    return lax.axis_index("subcore") * SC_CORES + lax.axis_index("core")


def _sc_scatter_rows(rows, d0, d1, cap):
    n, width = rows.shape
    per_worker = n // SC_WORKERS
    n_win = per_worker // SC_WINDOW

    def body(x_hbm, d0_hbm, d1_hbm, o_hbm, i0_v, i1_v, rows_v, rsem, sem0, sem1):
        wid = _sc_worker_id()
        pltpu.sync_copy(d0_hbm.at[wid], i0_v)
        pltpu.sync_copy(d1_hbm.at[wid], i1_v)

        def read(j):
            src = x_hbm.at[pl.ds(wid * per_worker + j * SC_WINDOW, SC_WINDOW)]
            return pltpu.make_async_copy(src, rows_v.at[j % 2], rsem.at[j % 2])

        def scatters(j):
            return (pltpu.make_async_copy(rows_v.at[j % 2], o_hbm.at[i0_v.at[j]], sem0.at[j % 2]),
                    pltpu.make_async_copy(rows_v.at[j % 2], o_hbm.at[i1_v.at[j]], sem1.at[j % 2]))

        read(0).start()
        for j in range(n_win):
            if j + 1 < n_win:
                if j >= 1:
                    for cp in scatters(j - 1):
                        cp.wait()
                read(j + 1).start()
            read(j).wait()
            for cp in scatters(j):
                cp.start()
        for j in range(max(n_win - 2, 0), n_win):
            for cp in scatters(j):
                cp.wait()

    return pl.kernel(
        body,
        out_type=jax.ShapeDtypeStruct((cap, width), rows.dtype),
        mesh=_sc_mesh(),
        scratch_types=[pltpu.VMEM((n_win, SC_WINDOW), I32), pltpu.VMEM((n_win, SC_WINDOW), I32),
                       pltpu.VMEM((2, SC_WINDOW, width), rows.dtype),
                       pltpu.SemaphoreType.DMA((2,)), pltpu.SemaphoreType.DMA((2,)),
                       pltpu.SemaphoreType.DMA((2,))],
        name="dispatch",
    )(rows, d0, d1)


def _sc_gather_rows(table, idx):
    width = table.shape[1]
    n_win = idx.shape[1]
    per_worker = n_win * SC_WINDOW

    def body(y_hbm, i_hbm, o_hbm, i_v, rows_v, gsem, wsem):
        wid = _sc_worker_id()
        pltpu.sync_copy(i_hbm.at[wid], i_v)

        def gather(j):
            return pltpu.make_async_copy(y_hbm.at[i_v.at[j]], rows_v.at[j % 2], gsem.at[j % 2])

        def write(j):
            dst = o_hbm.at[pl.ds(wid * per_worker + j * SC_WINDOW, SC_WINDOW)]
            return pltpu.make_async_copy(rows_v.at[j % 2], dst, wsem.at[j % 2])

        gather(0).start()
        for j in range(n_win):
            if j + 1 < n_win:
                if j >= 1:
                    write(j - 1).wait()
                gather(j + 1).start()
            gather(j).wait()
            write(j).start()
        for j in range(max(n_win - 2, 0), n_win):
            write(j).wait()

    return pl.kernel(
        body,
        out_type=jax.ShapeDtypeStruct((SC_WORKERS * per_worker, width), table.dtype),
        mesh=_sc_mesh(),
        scratch_types=[pltpu.VMEM((n_win, SC_WINDOW), I32),
                       pltpu.VMEM((2, SC_WINDOW, width), table.dtype),
                       pltpu.SemaphoreType.DMA((2,)), pltpu.SemaphoreType.DMA((2,))],
        name="collect",
    )(table, idx)


def _expert_kernel(bstart_ref, nblk_ref, nused_ref, xs_hbm, wg_ref, wu_ref, wd_ref, yb_hbm,
                   xbuf, ybuf, zbuf, xsem, ysem, zsem, wg_b, wu_b, wd_b):
    e = pl.program_id(0)
    nused = nused_ref[0]
    n_blocks = yb_hbm.shape[0] // MOE_BLOCK

    def rows(b):
        return pl.ds(pl.multiple_of(b * MOE_BLOCK, MOE_BLOCK), MOE_BLOCK)

    def x_copy(b):
        slot = b % X_RING
        return pltpu.make_async_copy(xs_hbm.at[rows(b)], xbuf.at[slot], xsem.at[slot])

    def y_copy(b):
        slot = b % Y_RING
        return pltpu.make_async_copy(ybuf.at[slot], yb_hbm.at[rows(b)], ysem.at[slot])

    @pl.when(e == 0)
    def _():
        for b in range(X_AHEAD):
            @pl.when(b < nused)
            def _():
                x_copy(b).start()

    wg_b[...] = wg_ref[...].astype(BF16)
    wu_b[...] = wu_ref[...].astype(BF16)
    wd_b[...] = wd_ref[...].astype(BF16)

    def run_blocks(b, count):
        for k in range(count):
            x_copy(b + k).wait()
        for k in range(count):
            nxt = b + X_AHEAD + k

            @pl.when(nxt < nused)
            def _():
                x_copy(nxt).start()

            @pl.when(b + k >= Y_RING)
            def _():
                y_copy(b + k - Y_RING).wait()

        words = jnp.concatenate([xbuf[(b + k) % X_RING] for k in range(count)], axis=0)
        lo, hi = _unpack_rows(words)
        x = jnp.concatenate([lo.astype(BF16), hi.astype(BF16)], axis=1)
        g = jnp.dot(x, wg_b[...], preferred_element_type=F32)
        u = jnp.dot(x, wu_b[...], preferred_element_type=F32)
        mid = (g * _sigmoid(g) * u).astype(BF16)
        y = _pack_rows(jnp.dot(mid, wd_b[...], preferred_element_type=F32))
        for k in range(count):
            ybuf[(b + k) % Y_RING] = y[k * MOE_BLOCK:(k + 1) * MOE_BLOCK]
            y_copy(b + k).start()

    b0 = bstart_ref[e]
    nb = nblk_ref[e]

    def group(i, carry):
        run_blocks(b0 + X_GROUP * i, X_GROUP)
        return carry

    lax.fori_loop(0, nb // X_GROUP, group, 0)
    done = nb - nb % X_GROUP
    size = X_GROUP // 2
    while size >= 1:
        @pl.when((nb // size) % 2 == 1)
        def _(size=size, done=done):
            run_blocks(b0 + done, size)

        done = done + (nb // size) % 2 * size
        size //= 2

    @pl.when(e == pl.num_programs(0) - 1)
    def _():
        for back in range(Y_RING, 0, -1):
            @pl.when(nused >= back)
            def _():
                y_copy(nused - back).wait()

        zbuf[...] = jnp.zeros_like(zbuf)

        def z_copy(b):
            return pltpu.make_async_copy(zbuf, yb_hbm.at[rows(b)], zsem.at[0])

        def z_start(b, carry):
            z_copy(b).start()
            return carry

        def z_wait(b, carry):
            z_copy(b).wait()
            return carry

        lax.fori_loop(nused, n_blocks, z_start, 0)
        lax.fori_loop(nused, n_blocks, z_wait, 0)


def _experts(bstart, nblk, nused, xs, w_gate, w_up, w_down):
    cap = xs.shape[0]
    w_idx = lambda e, bs, nb, nu: (e, 0, 0)
    grid_spec = pltpu.PrefetchScalarGridSpec(
        num_scalar_prefetch=3,
        grid=(N_EXPERTS,),
        in_specs=[pl.BlockSpec(memory_space=pl.ANY),
                  pl.BlockSpec((None, D_MODEL, D_FF), w_idx),
                  pl.BlockSpec((None, D_MODEL, D_FF), w_idx),
                  pl.BlockSpec((None, D_FF, D_MODEL), w_idx)],
        out_specs=pl.BlockSpec(memory_space=pl.ANY),
        scratch_shapes=[pltpu.VMEM((X_RING, MOE_BLOCK, PACKED), U32),
                        pltpu.VMEM((Y_RING, MOE_BLOCK, PACKED), U32),
                        pltpu.VMEM((MOE_BLOCK, PACKED), U32),
                        pltpu.SemaphoreType.DMA((X_RING,)),
                        pltpu.SemaphoreType.DMA((Y_RING,)),
                        pltpu.SemaphoreType.DMA((1,)),
                        pltpu.VMEM((D_MODEL, D_FF), BF16),
                        pltpu.VMEM((D_MODEL, D_FF), BF16),
                        pltpu.VMEM((D_FF, D_MODEL), BF16)])
    return pl.pallas_call(
        _expert_kernel,
        grid_spec=grid_spec,
        out_shape=jax.ShapeDtypeStruct((cap, PACKED), U32),
        compiler_params=_cparams(1),
        name="experts",
    )(bstart, nblk, nused, xs, w_gate, w_up, w_down)


def _combine_kernel(y0_ref, y1_ref, h_ref, info_ref, g_ref, b_ref, o_ref):
    info = info_ref[...].T
    g0 = info[:, 2:3]
    g1 = info[:, 3:4]
    lo0, hi0 = _unpack_rows(y0_ref[...])
    lo1, hi1 = _unpack_rows(y1_ref[...])
    y = jnp.concatenate([g0 * lo0 + g1 * lo1, g0 * hi0 + g1 * hi1], axis=1)
    o_ref[...] = _layer_norm(ALPHA * h_ref[...] + y, g_ref[...], b_ref[...])


def _combine(ys, h, info, ln_g, ln_b, rows, part, n_parts):
    n = h.shape[0]
    steps = n // n_parts // rows
    off = part * steps
    const = lambda i: (0, 0)
    return pl.pallas_call(
        _combine_kernel,
        grid=(steps,),
        in_specs=[pl.BlockSpec((rows, PACKED), lambda i: (i, 0)),
                  pl.BlockSpec((rows, PACKED), lambda i: (i + steps, 0)),
                  pl.BlockSpec((rows, D_MODEL), lambda i: (i + off, 0)),
                  pl.BlockSpec((SUBLANES, rows), lambda i: (0, i + off)),
                  pl.BlockSpec((1, D_MODEL), const),
                  pl.BlockSpec((1, D_MODEL), const)],
        out_specs=pl.BlockSpec((rows, D_MODEL), lambda i: (i + off, 0)),
        out_shape=jax.ShapeDtypeStruct((n, D_MODEL), F32),
        input_output_aliases={2: 0},
        compiler_params=_cparams(1),
        name="combine",
    )(ys, ys, h, info, ln_g, ln_b)


def _alibi_bias():
    qi = np.arange(BLOCK)[:, None]
    kj = np.arange(2 * BLOCK)[None, :]
    dist = qi - kj + BLOCK
    band = (dist >= 0) & (dist < BLOCK)
    slopes = np.exp2(-8.0 * np.arange(1, N_Q_HEADS + 1, dtype=np.float32) / N_Q_HEADS)
    bias = np.where(band[None], -slopes[:, None, None] * dist[None].astype(np.float32), NEG)
    bias = bias * LOG2E
    first = np.where((kj >= PAD_FRONT)[None], bias, NEG)
    out = np.empty((2, N_KV_HEADS, 2 * BLOCK, 4 * BLOCK), np.float32)
    for v, per_head in enumerate((first, bias)):
        for j in range(N_KV_HEADS):
            out[v, j] = np.block([[per_head[4 * j], per_head[4 * j + 1]],
                                  [per_head[4 * j + 2], per_head[4 * j + 3]]])
    return jnp.asarray(out, F32)


def _slab_gates(w):
    nb, c, _ = w.shape
    per = LANES // c
    w = w.reshape(nb // per, per, c, c)
    eye = jnp.eye(per, dtype=w.dtype)
    return jnp.einsum('spcd,pq->spcqd', w, eye).reshape(nb // per, LANES, LANES).astype(BF16)


def kernel(x, meta_tokens, w_in, conv_w, conv_b, lru_wa, lru_ba, lru_wx, lru_bx, lru_lambda,
           attn_sinks, g_attn, g_lru, w_out, ln1_g, ln1_b, w_group, b_group, w_router,
           b_router, w_gate, w_up, w_down, ln2_g, ln2_b):
    bsz, seq, d = x.shape
    nbx = seq // BLOCK
    n_tok = bsz * seq
    x2d = x.reshape(n_tok, d)
    row = lambda v: v.reshape(1, -1).astype(F32)

    q_scale = jnp.concatenate([jnp.full((ATTN_WIDTH,), LOG2E * HEAD_DIM ** -0.5, F32),
                               jnp.ones((IN_COLS - ATTN_WIDTH,), F32)]).reshape(1, IN_COLS)
    meta_blk = jnp.concatenate([jnp.zeros((PAD_FRONT, d), F32), meta_tokens.astype(F32)], axis=0)
    q, kv, xr, yr, kvm, xrm, yrm = _in_proj(x2d, meta_blk, w_in[0].astype(F32), q_scale,
                                            PROJ_ROWS)
    shp = lambda a: a.reshape(bsz, seq, a.shape[-1])

    attn_n = _attention(attn_sinks[0].astype(F32) * LOG2E, shp(q), shp(kv), kvm, _alibi_bias(),
                        row(g_attn[0]), bsz, nbx)
    lru_n = _rglru(shp(xr), shp(yr), xrm, yrm, conv_w[0].astype(F32), row(conv_b[0]),
                   _slab_gates(0.5 * lru_wa[0]), _slab_gates(0.5 * lru_wx[0]),
                   0.5 * row(lru_ba[0]), 0.5 * row(lru_bx[0]), row(lru_lambda[0]),
                   row(g_lru[0]), bsz, nbx)

    gpad = SUBLANES - N_GROUPS
    tail = ROUTER_ROWS - SUBLANES - N_EXPERTS
    w_rt = jnp.concatenate(
        [w_group[0].T, jnp.zeros((gpad, d), F32),
         jnp.transpose(w_router[0], (0, 2, 1)).reshape(N_EXPERTS, d),
         jnp.zeros((tail, d), F32)], axis=0).astype(F32)
    b_rt = jnp.concatenate([b_group[0], jnp.zeros((gpad,), F32), b_router[0].reshape(-1),
                            jnp.zeros((tail,), F32)]).astype(F32).reshape(ROUTER_ROWS, 1)
    h1, hp, info, cnt = _out_proj(
        attn_n.reshape(n_tok, ATTN_WIDTH), lru_n.reshape(n_tok, LRU_WIDTH), x2d,
        w_out[0].astype(F32), row(ln1_g[0]), row(ln1_b[0]), w_rt, b_rt, OUT_PROJ_ROWS)

    d0, d1 = (v.reshape(n_tok) for v in _route(info, cnt, ROUTE_ROWS))
    n_slots = n_tok * TOP_K
    n_blocks = n_slots // MOE_BLOCK + N_EXPERTS
    cap = n_blocks * MOE_BLOCK
    nblk = (cnt[:, 0].astype(I32) + MOE_BLOCK - 1) // MOE_BLOCK
    bends = jnp.cumsum(nblk)
    bstart = (bends - nblk).astype(I32)
    nused = bends[-1:].astype(I32)
    windows = lambda v: v.reshape(SC_WORKERS, -1, SC_WINDOW)

    xs = _sc_scatter_rows(hp, windows(d0), windows(d1), cap)
    yb = _experts(bstart, nblk.astype(I32), nused, xs, w_gate[0], w_up[0], w_down[0])
    out = h1
    part_len = n_tok // COMBINE_PARTS
    for part in range(COMBINE_PARTS):
        tok = slice(part * part_len, (part + 1) * part_len)
        ys = _sc_gather_rows(yb, windows(jnp.concatenate([d0[tok], d1[tok]])))
        out = _combine(ys, out, info, row(ln2_g[0]), row(ln2_b[0]), COMBINE_ROWS,
                       part, COMBINE_PARTS)
    return out.reshape(bsz, seq, d)
```

```python
import jax
import jax.numpy as jnp
import numpy as np
from jax import lax
from jax.experimental import pallas as pl
from jax.experimental.pallas import tpu as pltpu
from jax.experimental.pallas import tpu_sc as plsc

F32 = jnp.float32
BF16 = jnp.bfloat16
U32 = jnp.uint32
I32 = jnp.int32

D_MODEL = 1024
N_META = 16
BLOCK = 128
PAD_FRONT = BLOCK - N_META
HEAD_DIM = 64
ATTN_WIDTH = 512
LRU_WIDTH = 512
N_Q_HEADS = 8
N_KV_HEADS = 2
KV_WIDTH = N_KV_HEADS * HEAD_DIM
LRU_BLOCKS = 8
CONV_W = 4
LRU_C = 8.0
IN_COLS = ATTN_WIDTH + 2 * KV_WIDTH + 2 * LRU_WIDTH
N_GROUPS = 4
EXPERTS_PER_GROUP = 8
N_EXPERTS = N_GROUPS * EXPERTS_PER_GROUP
TOP_K = 2
D_FF = 512
MOE_BLOCK = 256
ALPHA = 2.0 ** 0.25
EPS = 1e-5
NEG = -1e30
LOG2E = float(np.log2(np.e))
LANES = 128
SUBLANES = 8
PACKED = D_MODEL // 2

PROJ_ROWS = 1024
OUT_PROJ_ROWS = 1024
ROUTE_ROWS = 512
COMBINE_ROWS = 1024
COMBINE_PARTS = 4
X_GROUP = 4
X_AHEAD = 4
X_RING = X_AHEAD + X_GROUP
Y_RING = 2 * X_GROUP
VMEM_LIMIT = 48 * 1024 * 1024

SC_CORES = 2
SC_SUBCORES = 16
SC_WORKERS = SC_CORES * SC_SUBCORES
SC_WINDOW = 64


def _cparams(n_axes):
    return pltpu.CompilerParams(
        dimension_semantics=("arbitrary",) * n_axes, vmem_limit_bytes=VMEM_LIMIT)


def _in_proj_kernel(x_ref, w_ref, q_ref, kv_ref, xr_ref, yr_ref):
    proj = jnp.dot(x_ref[...].astype(BF16), w_ref[...], preferred_element_type=F32)
    o = 0
    for ref, width in ((q_ref, ATTN_WIDTH), (kv_ref, 2 * KV_WIDTH),
                       (xr_ref, LRU_WIDTH), (yr_ref, LRU_WIDTH)):
        ref[...] = proj[:, o:o + width].astype(ref.dtype)
        o += width


def _in_proj(x2d, w_bf16, rows):
    n = x2d.shape[0]
    widths = (ATTN_WIDTH, 2 * KV_WIDTH, LRU_WIDTH, LRU_WIDTH)
    return pl.pallas_call(
        _in_proj_kernel,
        grid=(n // rows,),
        in_specs=[pl.BlockSpec((rows, D_MODEL), lambda i: (i, 0)),
                  pl.BlockSpec((D_MODEL, IN_COLS), lambda i: (0, 0))],
        out_specs=[pl.BlockSpec((rows, w), lambda i: (i, 0)) for w in widths],
        out_shape=[jax.ShapeDtypeStruct((n, w), BF16) for w in widths],
        compiler_params=_cparams(1),
        name="in_proj",
    )(x2d, w_bf16)


def _attn_kernel(sinks_ref, q_ref, kv_ref, kvm_ref, bias_ref, g_ref, o_ref,
                 klo, khi, vlo, vhi):
    nbx = q_ref.shape[0] // BLOCK
    lo_lanes = lax.broadcasted_iota(I32, (BLOCK // 2, LANES), 1) < HEAD_DIM

    def layout_block(n, blk):
        rows = pl.ds(pl.multiple_of(n * BLOCK, BLOCK), BLOCK)
        as_bf16 = lambda words: pltpu.bitcast(words, BF16)
        for src, dst_lo, dst_hi in ((blk[:, :KV_WIDTH], klo, khi), (blk[:, KV_WIDTH:], vlo, vhi)):
            w = pltpu.bitcast(src, U32)
            r = pltpu.roll(w, HEAD_DIM, axis=1)
            zero = jnp.zeros_like(w)
            dst_lo[0, rows, :] = as_bf16(jnp.where(lo_lanes, w, zero))
            dst_hi[0, rows, :] = as_bf16(jnp.where(lo_lanes, zero, r))
            dst_lo[1, rows, :] = as_bf16(jnp.where(lo_lanes, r, zero))
            dst_hi[1, rows, :] = as_bf16(jnp.where(lo_lanes, zero, w))

    layout_block(0, kvm_ref[...])

    def layout_body(n, carry):
        layout_block(n + 1, kv_ref[pl.ds(pl.multiple_of(n * BLOCK, BLOCK), BLOCK), :])
        return carry

    lax.fori_loop(0, nbx, layout_body, 0, unroll=4)

    ones_lo = jnp.where(lax.broadcasted_iota(I32, (2 * BLOCK, LANES), 1) < HEAD_DIM,
                        1.0, 0.0).astype(BF16)
    ones_hi = (1.0 - ones_lo.astype(F32)).astype(BF16)
    top_rows = lax.broadcasted_iota(I32, (2 * BLOCK, 1), 0) < BLOCK
    lo_half = lax.broadcasted_iota(I32, (2 * BLOCK, LANES), 1) < HEAD_DIM

    def block(i, carry):
        q_rows = pl.ds(pl.multiple_of(i * BLOCK, BLOCK), BLOCK)
        win = pl.ds(pl.multiple_of(i * BLOCK, BLOCK), 2 * BLOCK)
        q = q_ref[q_rows, :]
        first = jnp.minimum(i, 1)
        outs = []
        for j in range(N_KV_HEADS):
            q2 = jnp.concatenate([q[:, (2 * j) * LANES:(2 * j + 1) * LANES],
                                  q[:, (2 * j + 1) * LANES:(2 * j + 2) * LANES]], axis=0)
            kc = jnp.concatenate([klo[j, win, :], khi[j, win, :]], axis=0)
            s = lax.dot_general(q2, kc, (((1,), (1,)), ((), ())), preferred_element_type=F32)
            s = s + bias_ref[first, j]
            ps, es = [], []
            for c in range(2):
                sink = jnp.where(top_rows, sinks_ref[4 * j + c], sinks_ref[4 * j + 2 + c])
                sc = s[:, c * 2 * BLOCK:(c + 1) * 2 * BLOCK]
                m = jnp.maximum(jnp.max(sc, axis=1, keepdims=True), sink)
                ps.append(jnp.exp2(sc - m).astype(BF16))
                es.append(jnp.exp2(sink - m))
            v_lo = jnp.concatenate([vlo[j, win, :], ones_lo], axis=1)
            v_hi = jnp.concatenate([vhi[j, win, :], ones_hi], axis=1)
            r = jnp.dot(jnp.concatenate(ps, axis=1), jnp.concatenate([v_lo, v_hi], axis=0),
                        preferred_element_type=F32)
            den = r[:, LANES:] + jnp.where(lo_half, es[0], es[1])
            o2 = r[:, :LANES] * (1.0 / den)
            outs += [o2[:BLOCK], o2[BLOCK:]]
        out = jnp.concatenate(outs, axis=1)
        ms = jnp.mean(out * out, axis=1, keepdims=True)
        o_ref[q_rows, :] = (out * lax.rsqrt(ms + EPS) * g_ref[...]).astype(o_ref.dtype)
        return carry

    lax.fori_loop(0, nbx, block, 0, unroll=8)


def _attention(sinks, q, kv, kvm, bias, g_attn, bsz, nbx):
    seq = nbx * BLOCK
    const2 = lambda b: (0, 0)
    kv_scratch = pltpu.VMEM((N_KV_HEADS, seq + BLOCK, LANES), BF16)
    return pl.pallas_call(
        _attn_kernel,
        grid=(bsz,),
        in_specs=[pl.BlockSpec(memory_space=pltpu.SMEM),
                  pl.BlockSpec((None, seq, ATTN_WIDTH), lambda b: (b, 0, 0)),
                  pl.BlockSpec((None, seq, 2 * KV_WIDTH), lambda b: (b, 0, 0)),
                  pl.BlockSpec((BLOCK, 2 * KV_WIDTH), const2),
                  pl.BlockSpec((2, N_KV_HEADS, 2 * BLOCK, 4 * BLOCK), lambda b: (0, 0, 0, 0)),
                  pl.BlockSpec((1, ATTN_WIDTH), const2)],
        out_specs=pl.BlockSpec((None, seq, ATTN_WIDTH), lambda b: (b, 0, 0)),
        out_shape=jax.ShapeDtypeStruct((bsz, seq, ATTN_WIDTH), BF16),
        scratch_shapes=[kv_scratch, kv_scratch, kv_scratch, kv_scratch],
        compiler_params=_cparams(1),
        name="attention",
    )(sinks, q, kv, kvm, bias, g_attn)


def _sigmoid(v):
    return 0.5 * jnp.tanh(0.5 * v) + 0.5


def _gelu_tanh(y):
    c = float(np.sqrt(2.0 / np.pi))
    half = 0.5 * y
    return half + half * jnp.tanh(y * (c + (c * 0.044715) * (y * y)))


LRU_CHUNK = 44
LRU_SEG = SUBLANES * LRU_CHUNK
LRU_SLABS = LRU_WIDTH // LANES


def _lru_kernel(xr_ref, yr_ref, xrm_ref, yrm_ref, cw_ref, cb_ref, wa_ref, wx_ref, ba_ref,
                bx_ref, lam_ref, g_ref, o_ref, x_st, y_st, o_st, s_st, xtail, hcar):
    seq = xr_ref.shape[0]
    n_seg = (seq + BLOCK) // LRU_SEG
    xtail[...] = jnp.zeros_like(xtail)
    hcar[...] = jnp.zeros_like(hcar)
    lam = lam_ref[...]
    softplus_neg = jnp.maximum(-lam, 0.0) + jnp.log(1.0 + jnp.exp(-jnp.abs(lam)))
    sub = lax.broadcasted_iota(jnp.int32, (SUBLANES, LANES), 0)

    def strided(j):
        return pl.ds(j, SUBLANES, stride=LRU_CHUNK)

    def piece(v, j):
        return v[j * SUBLANES:(j + 1) * SUBLANES, :]

    def segment(k, first):
        if first:
            head = LRU_SEG - BLOCK
            x_nat = jnp.concatenate([xrm_ref[...], xr_ref[0:head, :]], axis=0).astype(F32)
            y_nat = jnp.concatenate([yrm_ref[...], yr_ref[0:head, :]], axis=0).astype(F32)
        else:
            rows = pl.ds(pl.multiple_of(k * LRU_SEG - BLOCK, 2 * SUBLANES), LRU_SEG)
            x_nat = xr_ref[rows, :].astype(F32)
            y_nat = yr_ref[rows, :].astype(F32)
        for c in range(LRU_SLABS):
            x_st[c] = x_nat[:, c * LANES:(c + 1) * LANES]
            y_st[c] = y_nat[:, c * LANES:(c + 1) * LANES]
        first_row = k * LRU_SEG + LRU_CHUNK * sub
        sumsq = [jnp.zeros((SUBLANES, LANES), F32) for _ in range(LRU_CHUNK)]

        for c in range(LRU_SLABS):
            lanes = slice(c * LANES, (c + 1) * LANES)
            x = [x_st[c, strided(j), :] for j in range(LRU_CHUNK)]
            before = []
            for d in range(1, CONV_W):
                from_prev_chunk = pltpu.roll(x[LRU_CHUNK - d], 1, axis=0)
                before.append(jnp.where(sub == 0, xtail[d - 1:d, lanes], from_prev_chunk))
            for d in range(1, CONV_W):
                xtail[d - 1:d, lanes] = x[LRU_CHUNK - d][SUBLANES - 1:SUBLANES, :]

            def x_at(j):
                return x[j] if j >= 0 else before[-j - 1]

            taps = [cw_ref[t:t + 1, lanes] for t in range(CONV_W)]
            bias = cb_ref[:, lanes]
            xc = jnp.concatenate(
                [bias + sum(taps[t] * x_at(j - (CONV_W - 1) + t) for t in range(CONV_W))
                 for j in range(LRU_CHUNK)], axis=0)
            xcb = xc.astype(BF16)
            both = jnp.dot(xcb, jnp.concatenate([wa_ref[c], wx_ref[c]], axis=1),
                           preferred_element_type=F32)
            tr = jnp.tanh(both[:, :LANES] + ba_ref[:, lanes])
            ti = jnp.tanh(both[:, LANES:] + bx_ref[:, lanes])
            log_a_half = (-0.5 * LRU_C) * softplus_neg[:, lanes]
            a = jnp.exp(log_a_half * tr + log_a_half)
            half_xc = 0.5 * xc
            gated_x = half_xc * ti + half_xc
            z = 1.0 - a * a
            u = jnp.where(z > 0.0, z * lax.rsqrt(z), 0.0) * gated_x

            h = jnp.zeros((SUBLANES, LANES), F32)
            p = jnp.ones((SUBLANES, LANES), F32)
            hs, ps = [], []
            for j in range(LRU_CHUNK):
                aj = piece(a, j)
                uj = piece(u, j)
                if first:
                    uj = jnp.where(first_row + j >= PAD_FRONT, uj, 0.0)
                h = aj * h + uj
                p = aj * p
                hs.append(h)
                ps.append(p)
            entry = [hcar[:, lanes]]
            for s in range(SUBLANES):
                entry.append(h[s:s + 1, :] + p[s:s + 1, :] * entry[s])
            hcar[:, lanes] = entry[SUBLANES]
            entry_rows = jnp.concatenate(entry[:SUBLANES], axis=0)

            for j in range(LRU_CHUNK):
                state = hs[j] + ps[j] * entry_rows
                out = state * _gelu_tanh(y_st[c, strided(j), :])
                sumsq[j] = sumsq[j] + out * out
                o_st[c, strided(j), :] = out

        for j in range(LRU_CHUNK):
            ms = jnp.sum(sumsq[j], axis=1, keepdims=True) * (1.0 / LRU_WIDTH)
            s_st[strided(j), :] = jnp.broadcast_to(lax.rsqrt(ms + EPS), (SUBLANES, LANES))
        scale = s_st[...]
        for c in range(LRU_SLABS):
            lanes = slice(c * LANES, (c + 1) * LANES)
            normed = (o_st[c] * scale * g_ref[:, lanes]).astype(o_ref.dtype)
            if first:
                o_ref[0:LRU_SEG - BLOCK, lanes] = normed[BLOCK:, :]
            else:
                o_ref[rows, lanes] = normed

    assert BLOCK <= LRU_SEG
    segment(0, True)

    def later_segment(k, carry):
        segment(k, False)
        return carry

    lax.fori_loop(1, n_seg, later_segment, 0)


def _rglru(xr, yr, xrm, yrm, cw, cb, wa, wx, ba, bx, lam, g_lru, bsz, nbx):
    seq = nbx * BLOCK
    assert (seq + BLOCK) % LRU_SEG == 0
    main = pl.BlockSpec((None, seq, LRU_WIDTH), lambda b: (b, 0, 0))
    const2 = lambda b: (0, 0)
    row_spec = pl.BlockSpec((1, LRU_WIDTH), const2)
    gate_spec = pl.BlockSpec((LRU_SLABS, LANES, LANES), lambda b: (0, 0, 0))
    slabs = pltpu.VMEM((LRU_SLABS, LRU_SEG, LANES), F32)
    return pl.pallas_call(
        _lru_kernel,
        grid=(bsz,),
        in_specs=[main, main,
                  pl.BlockSpec((BLOCK, LRU_WIDTH), const2),
                  pl.BlockSpec((BLOCK, LRU_WIDTH), const2),
                  pl.BlockSpec((CONV_W, LRU_WIDTH), const2),
                  row_spec, gate_spec, gate_spec,
                  row_spec, row_spec, row_spec, row_spec],
        out_specs=main,
        out_shape=jax.ShapeDtypeStruct((bsz, seq, LRU_WIDTH), BF16),
        scratch_shapes=[slabs, slabs, slabs,
                        pltpu.VMEM((LRU_SEG, LANES), F32),
                        pltpu.VMEM((SUBLANES, LRU_WIDTH), F32),
                        pltpu.VMEM((1, LRU_WIDTH), F32)],
        compiler_params=_cparams(1),
        name="rglru",
    )(xr, yr, xrm, yrm, cw, cb, wa, wx, ba, bx, lam, g_lru)


def _pack_rows(v):
    bits = lax.bitcast_convert_type(v.astype(BF16).astype(F32), U32)
    return (bits[:, :PACKED] >> 16) | (bits[:, PACKED:] & jnp.uint32(0xFFFF0000))


def _unpack_rows(w):
    lo = lax.bitcast_convert_type(w << 16, F32)
    hi = lax.bitcast_convert_type(w & jnp.uint32(0xFFFF0000), F32)
    return lo, hi


def _layer_norm(z, g, b):
    mu = jnp.mean(z, axis=1, keepdims=True)
    zc = z - mu
    var = jnp.mean(zc * zc, axis=1, keepdims=True)
    return zc * lax.rsqrt(var + EPS) * g + b


def _out_proj_kernel(a_ref, l_ref, x_ref, w_ref, g_ref, b_ref, wrt_ref, brt_ref,
                     h_ref, hp_ref, info_ref, cnt_ref, w_b, wrt_b):
    @pl.when(pl.program_id(0) == 0)
    def _():
        cnt_ref[...] = jnp.zeros_like(cnt_ref)
        w_b[...] = w_ref[...].astype(BF16)
        hi = wrt_ref[...].astype(BF16)
        wrt_b[0:ROUTER_ROWS, :] = hi
        wrt_b[ROUTER_ROWS:, :] = (wrt_ref[...] - hi.astype(F32)).astype(BF16)

    mix = jnp.dot(jnp.concatenate([a_ref[...], l_ref[...]], axis=1), w_b[...],
                  preferred_element_type=F32)
    h = _layer_norm(ALPHA * x_ref[...] + mix, g_ref[...], b_ref[...])
    h_ref[...] = h
    hp_ref[...] = _pack_rows(h)

    h_hi = h.astype(BF16)
    h_lo = (h - h_hi.astype(F32)).astype(BF16)
    nt = (((1,), (1,)), ((), ()))
    both = lax.dot_general(wrt_b[...], h_hi, nt, preferred_element_type=F32)
    lg = (both[:ROUTER_ROWS] + both[ROUTER_ROWS:]
          + lax.dot_general(wrt_b[0:ROUTER_ROWS, :], h_lo, nt, preferred_element_type=F32)
          ) + brt_ref[...]
    tile_shape = (SUBLANES, h.shape[0])
    sub = lax.broadcasted_iota(I32, tile_shape, 0)
    ninf = -jnp.inf
    t0 = lg[0:SUBLANES]
    gl = jnp.where(sub < N_GROUPS, t0, ninf)
    gmax = jnp.max(gl, axis=0, keepdims=True)
    g_idx = jnp.min(jnp.where(gl == gmax, sub, SUBLANES), axis=0, keepdims=True)
    g_w = 1.0 / jnp.sum(jnp.where(sub < N_GROUPS, jnp.exp(t0 - gmax), 0.0),
                        axis=0, keepdims=True)
    el = lg[SUBLANES:2 * SUBLANES]
    for g in range(1, N_GROUPS):
        el = jnp.where(g_idx == g, lg[(g + 1) * SUBLANES:(g + 2) * SUBLANES], el)
    v1 = jnp.max(el, axis=0, keepdims=True)
    i1 = jnp.min(jnp.where(el == v1, sub, SUBLANES), axis=0, keepdims=True)
    el2 = jnp.where(sub == i1, ninf, el)
    v2 = jnp.max(el2, axis=0, keepdims=True)
    i2 = jnp.min(jnp.where(el2 == v2, sub, SUBLANES), axis=0, keepdims=True)
    t = jnp.exp(v2 - v1)
    w1 = 1.0 / (1.0 + t)
    w2 = t * w1
    e_base = g_idx * EXPERTS_PER_GROUP
    e1 = e_base + i1
    e2 = e_base + i2
    info_ref[...] = jnp.where(sub == 0, e1.astype(F32),
                              jnp.where(sub == 1, e2.astype(F32),
                                        jnp.where(sub == 2, g_w * w1,
                                                  jnp.where(sub == 3, g_w * w2, 0.0))))
    expert = lax.broadcasted_iota(I32, (N_EXPERTS, h.shape[0]), 0)
    chosen = (expert == e1).astype(F32) + (expert == e2).astype(F32)
    cnt_ref[...] += jnp.sum(chosen, axis=1, keepdims=True)


ROUTER_ROWS = -(-(N_GROUPS + 1) * SUBLANES // (2 * SUBLANES)) * (2 * SUBLANES)


def _out_proj(attn_n, lru_n, x2d, w_out, ln_g, ln_b, w_rt, b_rt, rows):
    n = x2d.shape[0]
    const = lambda i: (0, 0)
    tile = lambda w: pl.BlockSpec((rows, w), lambda i: (i, 0))
    return pl.pallas_call(
        _out_proj_kernel,
        grid=(n // rows,),
        in_specs=[tile(ATTN_WIDTH), tile(LRU_WIDTH), tile(D_MODEL),
                  pl.BlockSpec((D_MODEL, D_MODEL), const),
                  pl.BlockSpec((1, D_MODEL), const),
                  pl.BlockSpec((1, D_MODEL), const),
                  pl.BlockSpec((ROUTER_ROWS, D_MODEL), const),
                  pl.BlockSpec((ROUTER_ROWS, 1), const)],
        out_specs=[tile(D_MODEL), tile(PACKED),
                   pl.BlockSpec((SUBLANES, rows), lambda i: (0, i)),
                   pl.BlockSpec((N_EXPERTS, LANES), const)],
        out_shape=[jax.ShapeDtypeStruct((n, D_MODEL), F32),
                   jax.ShapeDtypeStruct((n, PACKED), U32),
                   jax.ShapeDtypeStruct((SUBLANES, n), F32),
                   jax.ShapeDtypeStruct((N_EXPERTS, LANES), F32)],
        scratch_shapes=[pltpu.VMEM((D_MODEL, D_MODEL), BF16),
                        pltpu.VMEM((2 * ROUTER_ROWS, D_MODEL), BF16)],
        compiler_params=_cparams(1),
        name="out_proj",
    )(attn_n, lru_n, x2d, w_out, ln_g, ln_b, w_rt, b_rt)


def _route_kernel(info_ref, cnt_ref, tri_ref, dest_ref, carry, pstart):
    t = pl.program_id(0)
    info = info_ref[...]
    shape = (N_EXPERTS, info.shape[1])
    expert = lax.broadcasted_iota(I32, shape, 0)
    oh1 = (expert == info[0:1, :].astype(I32)).astype(F32)
    oh2 = (expert == info[1:2, :].astype(I32)).astype(F32)
    both = oh1 + oh2

    @pl.when(t == 0)
    def _():
        c = cnt_ref[...].astype(I32)
        padded = ((c + (MOE_BLOCK - 1)) // MOE_BLOCK) * MOE_BLOCK
        e = lax.broadcasted_iota(I32, (N_EXPERTS, LANES), 0)
        scan = padded
        for d in (1, 2, 4, 8, 16):
            scan = scan + jnp.where(e >= d, pltpu.roll(scan, d, axis=0), 0)
        pstart[...] = (scan - padded)[:, 0:1].astype(F32)
        carry[...] = jnp.zeros_like(carry)

    before = jnp.dot(both.astype(BF16), tri_ref[...], preferred_element_type=F32)
    row_of = before + (carry[...] + pstart[...])
    r1 = jnp.sum(oh1 * row_of, axis=0, keepdims=True)
    r2 = jnp.sum(oh2 * row_of, axis=0, keepdims=True)
    dest_ref[...] = jnp.concatenate([r1, r2], axis=0).astype(I32)
    carry[...] += jnp.sum(both, axis=1, keepdims=True)


def _route(info_t, cnt, cols):
    n = info_t.shape[1]
    tri = jnp.asarray(np.triu(np.ones((cols, cols), np.float32), 1), BF16)
    return pl.pallas_call(
        _route_kernel,
        grid=(n // cols,),
        in_specs=[pl.BlockSpec((SUBLANES, cols), lambda t: (0, t)),
                  pl.BlockSpec((N_EXPERTS, LANES), lambda t: (0, 0)),
                  pl.BlockSpec((cols, cols), lambda t: (0, 0))],
        out_specs=pl.BlockSpec((TOP_K, cols), lambda t: (0, t)),
        out_shape=jax.ShapeDtypeStruct((TOP_K, n), I32),
        scratch_shapes=[pltpu.VMEM((N_EXPERTS, 1), F32), pltpu.VMEM((N_EXPERTS, 1), F32)],
        compiler_params=_cparams(1),
        name="route",
    )(info_t, cnt, tri)


def _sc_mesh():
    return plsc.VectorSubcoreMesh(core_axis_name="core", subcore_axis_name="subcore")


def _sc_worker_id():
    return lax.axis_index("subcore") * SC_CORES + lax.axis_index("core")


def _sc_scatter_rows(rows, dest, cap):
    n, width = rows.shape
    per_worker = n // SC_WORKERS
    n_win = per_worker // SC_WINDOW

    def body(x_hbm, d_hbm, o_hbm, i0_v, i1_v, rows_v, rsem, sem0, sem1):
        wid = _sc_worker_id()
        for j in range(n_win):
            win = pl.ds(wid * per_worker + j * SC_WINDOW, SC_WINDOW)
            pltpu.sync_copy(d_hbm.at[0, win], i0_v.at[j])
            pltpu.sync_copy(d_hbm.at[1, win], i1_v.at[j])

        def read(j):
            src = x_hbm.at[pl.ds(wid * per_worker + j * SC_WINDOW, SC_WINDOW)]
            return pltpu.make_async_copy(src, rows_v.at[j % 2], rsem.at[j % 2])

        def scatters(j):
            return (pltpu.make_async_copy(rows_v.at[j % 2], o_hbm.at[i0_v.at[j]], sem0.at[j % 2]),
                    pltpu.make_async_copy(rows_v.at[j % 2], o_hbm.at[i1_v.at[j]], sem1.at[j % 2]))

        read(0).start()
        for j in range(n_win):
            if j + 1 < n_win:
                if j >= 1:
                    for cp in scatters(j - 1):
                        cp.wait()
                read(j + 1).start()
            read(j).wait()
            for cp in scatters(j):
                cp.start()
        for j in range(max(n_win - 2, 0), n_win):
            for cp in scatters(j):
                cp.wait()

    return pl.kernel(
        body,
        out_type=jax.ShapeDtypeStruct((cap, width), rows.dtype),
        mesh=_sc_mesh(),
        scratch_types=[pltpu.VMEM((n_win, SC_WINDOW), I32), pltpu.VMEM((n_win, SC_WINDOW), I32),
                       pltpu.VMEM((2, SC_WINDOW, width), rows.dtype),
                       pltpu.SemaphoreType.DMA((2,)), pltpu.SemaphoreType.DMA((2,)),
                       pltpu.SemaphoreType.DMA((2,))],
        name="dispatch",
    )(rows, dest)


def _sc_gather_rows(table, dest, start, length):
    width = table.shape[1]
    per_worker = TOP_K * length // SC_WORKERS
    n_win = per_worker // SC_WINDOW
    workers_per_choice = SC_WORKERS // TOP_K

    def body(y_hbm, d_hbm, o_hbm, i_v, rows_v, gsem, wsem):
        wid = _sc_worker_id()
        choice = wid // workers_per_choice
        first = start + (wid % workers_per_choice) * per_worker
        for j in range(n_win):
            pltpu.sync_copy(d_hbm.at[choice, pl.ds(first + j * SC_WINDOW, SC_WINDOW)], i_v.at[j])

        def gather(j):
            return pltpu.make_async_copy(y_hbm.at[i_v.at[j]], rows_v.at[j % 2], gsem.at[j % 2])

        def write(j):
            dst = o_hbm.at[pl.ds(wid * per_worker + j * SC_WINDOW, SC_WINDOW)]
            return pltpu.make_async_copy(rows_v.at[j % 2], dst, wsem.at[j % 2])

        gather(0).start()
        for j in range(n_win):
            if j + 1 < n_win:
                if j >= 1:
                    write(j - 1).wait()
                gather(j + 1).start()
            gather(j).wait()
            write(j).start()
        for j in range(max(n_win - 2, 0), n_win):
            write(j).wait()

    return pl.kernel(
        body,
        out_type=jax.ShapeDtypeStruct((SC_WORKERS * per_worker, width), table.dtype),
        mesh=_sc_mesh(),
        scratch_types=[pltpu.VMEM((n_win, SC_WINDOW), I32),
                       pltpu.VMEM((2, SC_WINDOW, width), table.dtype),
                       pltpu.SemaphoreType.DMA((2,)), pltpu.SemaphoreType.DMA((2,))],
        name="collect",
    )(table, dest)


def _expert_kernel(bstart_ref, nblk_ref, nused_ref, xs_hbm, wg_ref, wu_ref, wd_ref, yb_hbm,
                   xbuf, ybuf, zbuf, xsem, ysem, zsem, wg_b, wu_b, wd_b):
    e = pl.program_id(0)
    nused = nused_ref[0]
    n_blocks = yb_hbm.shape[0] // MOE_BLOCK

    def rows(b):
        return pl.ds(pl.multiple_of(b * MOE_BLOCK, MOE_BLOCK), MOE_BLOCK)

    def x_copy(b):
        slot = b % X_RING
        return pltpu.make_async_copy(xs_hbm.at[rows(b)], xbuf.at[slot], xsem.at[slot])

    def y_copy(b):
        slot = b % Y_RING
        return pltpu.make_async_copy(ybuf.at[slot], yb_hbm.at[rows(b)], ysem.at[slot])

    @pl.when(e == 0)
    def _():
        for b in range(X_AHEAD):
            @pl.when(b < nused)
            def _():
                x_copy(b).start()

    wg_b[...] = wg_ref[...].astype(BF16)
    wu_b[...] = wu_ref[...].astype(BF16)
    wd_b[...] = wd_ref[...].astype(BF16)

    def run_blocks(b, count):
        for k in range(count):
            x_copy(b + k).wait()
        for k in range(count):
            nxt = b + X_AHEAD + k

            @pl.when(nxt < nused)
            def _():
                x_copy(nxt).start()

            @pl.when(b + k >= Y_RING)
            def _():
                y_copy(b + k - Y_RING).wait()

        words = jnp.concatenate([xbuf[(b + k) % X_RING] for k in range(count)], axis=0)
        lo, hi = _unpack_rows(words)
        x = jnp.concatenate([lo.astype(BF16), hi.astype(BF16)], axis=1)
        g = jnp.dot(x, wg_b[...], preferred_element_type=F32)
        u = jnp.dot(x, wu_b[...], preferred_element_type=F32)
        mid = (g * _sigmoid(g) * u).astype(BF16)
        y = _pack_rows(jnp.dot(mid, wd_b[...], preferred_element_type=F32))
        for k in range(count):
            ybuf[(b + k) % Y_RING] = y[k * MOE_BLOCK:(k + 1) * MOE_BLOCK]
            y_copy(b + k).start()

    b0 = bstart_ref[e]
    nb = nblk_ref[e]

    def group(i, carry):
        run_blocks(b0 + X_GROUP * i, X_GROUP)
        return carry

    lax.fori_loop(0, nb // X_GROUP, group, 0)
    done = nb - nb % X_GROUP
    size = X_GROUP // 2
    while size >= 1:
        @pl.when((nb // size) % 2 == 1)
        def _(size=size, done=done):
            run_blocks(b0 + done, size)

        done = done + (nb // size) % 2 * size
        size //= 2

    @pl.when(e == pl.num_programs(0) - 1)
    def _():
        for back in range(Y_RING, 0, -1):
            @pl.when(nused >= back)
            def _():
                y_copy(nused - back).wait()

        zbuf[...] = jnp.zeros_like(zbuf)

        def z_copy(b):
            return pltpu.make_async_copy(zbuf, yb_hbm.at[rows(b)], zsem.at[0])

        def z_start(b, carry):
            z_copy(b).start()
            return carry

        def z_wait(b, carry):
            z_copy(b).wait()
            return carry

        lax.fori_loop(nused, n_blocks, z_start, 0)
        lax.fori_loop(nused, n_blocks, z_wait, 0)


def _experts(bstart, nblk, nused, xs, w_gate, w_up, w_down):
    cap = xs.shape[0]
    w_idx = lambda e, bs, nb, nu: (e, 0, 0)
    grid_spec = pltpu.PrefetchScalarGridSpec(
        num_scalar_prefetch=3,
        grid=(N_EXPERTS,),
        in_specs=[pl.BlockSpec(memory_space=pl.ANY),
                  pl.BlockSpec((None, D_MODEL, D_FF), w_idx),
                  pl.BlockSpec((None, D_MODEL, D_FF), w_idx),
                  pl.BlockSpec((None, D_FF, D_MODEL), w_idx)],
        out_specs=pl.BlockSpec(memory_space=pl.ANY),
        scratch_shapes=[pltpu.VMEM((X_RING, MOE_BLOCK, PACKED), U32),
                        pltpu.VMEM((Y_RING, MOE_BLOCK, PACKED), U32),
                        pltpu.VMEM((MOE_BLOCK, PACKED), U32),
                        pltpu.SemaphoreType.DMA((X_RING,)),
                        pltpu.SemaphoreType.DMA((Y_RING,)),
                        pltpu.SemaphoreType.DMA((1,)),
                        pltpu.VMEM((D_MODEL, D_FF), BF16),
                        pltpu.VMEM((D_MODEL, D_FF), BF16),
                        pltpu.VMEM((D_FF, D_MODEL), BF16)])
    return pl.pallas_call(
        _expert_kernel,
        grid_spec=grid_spec,
        out_shape=jax.ShapeDtypeStruct((cap, PACKED), U32),
        compiler_params=_cparams(1),
        name="experts",
    )(bstart, nblk, nused, xs, w_gate, w_up, w_down)


def _combine_kernel(y0_ref, y1_ref, h_ref, info_ref, g_ref, b_ref, o_ref):
    info = info_ref[...].T
    g0 = info[:, 2:3]
    g1 = info[:, 3:4]
    lo0, hi0 = _unpack_rows(y0_ref[...])
    lo1, hi1 = _unpack_rows(y1_ref[...])
    y = jnp.concatenate([g0 * lo0 + g1 * lo1, g0 * hi0 + g1 * hi1], axis=1)
    o_ref[...] = _layer_norm(ALPHA * h_ref[...] + y, g_ref[...], b_ref[...])


def _combine(ys, h, info, ln_g, ln_b, rows, part, n_parts):
    n = h.shape[0]
    steps = n // n_parts // rows
    off = part * steps
    const = lambda i: (0, 0)
    return pl.pallas_call(
        _combine_kernel,
        grid=(steps,),
        in_specs=[pl.BlockSpec((rows, PACKED), lambda i: (i, 0)),
                  pl.BlockSpec((rows, PACKED), lambda i: (i + steps, 0)),
                  pl.BlockSpec((rows, D_MODEL), lambda i: (i + off, 0)),
                  pl.BlockSpec((SUBLANES, rows), lambda i: (0, i + off)),
                  pl.BlockSpec((1, D_MODEL), const),
                  pl.BlockSpec((1, D_MODEL), const)],
        out_specs=pl.BlockSpec((rows, D_MODEL), lambda i: (i + off, 0)),
        out_shape=jax.ShapeDtypeStruct((n, D_MODEL), F32),
        input_output_aliases={2: 0},
        compiler_params=_cparams(1),
        name="combine",
    )(ys, ys, h, info, ln_g, ln_b)


def _alibi_bias():
    qi = np.arange(BLOCK)[:, None]
    kj = np.arange(2 * BLOCK)[None, :]
    dist = qi - kj + BLOCK
    band = (dist >= 0) & (dist < BLOCK)
    slopes = np.exp2(-8.0 * np.arange(1, N_Q_HEADS + 1, dtype=np.float32) / N_Q_HEADS)
    bias = np.where(band[None], -slopes[:, None, None] * dist[None].astype(np.float32), NEG)
    bias = bias * LOG2E
    first = np.where((kj >= PAD_FRONT)[None], bias, NEG)
    out = np.empty((2, N_KV_HEADS, 2 * BLOCK, 4 * BLOCK), np.float32)
    for v, per_head in enumerate((first, bias)):
        for j in range(N_KV_HEADS):
            out[v, j] = np.block([[per_head[4 * j], per_head[4 * j + 1]],
                                  [per_head[4 * j + 2], per_head[4 * j + 3]]])
    return jnp.asarray(out, F32)


def _slab_gates(w):
    nb, c, _ = w.shape
    per = LANES // c
    w = w.reshape(nb // per, per, c, c)
    eye = jnp.eye(per, dtype=w.dtype)
    return jnp.einsum('spcd,pq->spcqd', w, eye).reshape(nb // per, LANES, LANES).astype(BF16)


def kernel(x, meta_tokens, w_in, conv_w, conv_b, lru_wa, lru_ba, lru_wx, lru_bx, lru_lambda,
           attn_sinks, g_attn, g_lru, w_out, ln1_g, ln1_b, w_group, b_group, w_router,
           b_router, w_gate, w_up, w_down, ln2_g, ln2_b):
    bsz, seq, d = x.shape
    nbx = seq // BLOCK
    n_tok = bsz * seq
    x2d = x.reshape(n_tok, d)
    row = lambda v: v.reshape(1, -1).astype(F32)

    q_scale = jnp.concatenate([jnp.full((ATTN_WIDTH,), LOG2E * HEAD_DIM ** -0.5, F32),
                               jnp.ones((IN_COLS - ATTN_WIDTH,), F32)])
    w_in_b = (w_in[0] * q_scale).astype(BF16)
    meta_blk = jnp.concatenate([jnp.zeros((PAD_FRONT, d), F32), meta_tokens.astype(F32)], axis=0)
    q, kv, xr, yr = _in_proj(x2d, w_in_b, PROJ_ROWS)
    qm, kvm, xrm, yrm = _in_proj(meta_blk, w_in_b, BLOCK)
    shp = lambda a: a.reshape(bsz, seq, a.shape[-1])

    attn_n = _attention(attn_sinks[0].astype(F32) * LOG2E, shp(q), shp(kv), kvm, _alibi_bias(),
                        row(g_attn[0]), bsz, nbx)
    lru_n = _rglru(shp(xr), shp(yr), xrm, yrm, conv_w[0].astype(F32), row(conv_b[0]),
                   _slab_gates(0.5 * lru_wa[0]), _slab_gates(0.5 * lru_wx[0]),
                   0.5 * row(lru_ba[0]), 0.5 * row(lru_bx[0]), row(lru_lambda[0]),
                   row(g_lru[0]), bsz, nbx)

    gpad = SUBLANES - N_GROUPS
    tail = ROUTER_ROWS - SUBLANES - N_EXPERTS
    w_rt = jnp.concatenate(
        [w_group[0].T, jnp.zeros((gpad, d), F32),
         jnp.transpose(w_router[0], (0, 2, 1)).reshape(N_EXPERTS, d),
         jnp.zeros((tail, d), F32)], axis=0).astype(F32)
    b_rt = jnp.concatenate([b_group[0], jnp.zeros((gpad,), F32), b_router[0].reshape(-1),
                            jnp.zeros((tail,), F32)]).astype(F32).reshape(ROUTER_ROWS, 1)
    h1, hp, info, cnt = _out_proj(
        attn_n.reshape(n_tok, ATTN_WIDTH), lru_n.reshape(n_tok, LRU_WIDTH), x2d,
        w_out[0].astype(F32), row(ln1_g[0]), row(ln1_b[0]), w_rt, b_rt, OUT_PROJ_ROWS)

    dest = _route(info, cnt, ROUTE_ROWS)
    n_slots = n_tok * TOP_K
    n_blocks = n_slots // MOE_BLOCK + N_EXPERTS
    cap = n_blocks * MOE_BLOCK
    nblk = (cnt[:, 0].astype(I32) + MOE_BLOCK - 1) // MOE_BLOCK
    bends = jnp.cumsum(nblk)
    bstart = (bends - nblk).astype(I32)
    nused = bends[-1:].astype(I32)

    xs = _sc_scatter_rows(hp, dest, cap)
    yb = _experts(bstart, nblk.astype(I32), nused, xs, w_gate[0], w_up[0], w_down[0])
    out = h1
    part_len = n_tok // COMBINE_PARTS
    for part in range(COMBINE_PARTS):
        ys = _sc_gather_rows(yb, dest, part * part_len, part_len)
        out = _combine(ys, out, info, row(ln2_g[0]), row(ln2_b[0]), COMBINE_ROWS,
                       part, COMBINE_PARTS)
    return out.reshape(bsz, seq, d)
```

```python
import jax
import jax.numpy as jnp
import numpy as np
from jax import lax
from jax.experimental import pallas as pl
from jax.experimental.pallas import tpu as pltpu
from jax.experimental.pallas import tpu_sc as plsc

F32 = jnp.float32
BF16 = jnp.bfloat16
U32 = jnp.uint32
I32 = jnp.int32

D_MODEL = 1024
N_META = 16
BLOCK = 128
PAD_FRONT = BLOCK - N_META
HEAD_DIM = 64
ATTN_WIDTH = 512
LRU_WIDTH = 512
N_Q_HEADS = 8
N_KV_HEADS = 2
KV_WIDTH = N_KV_HEADS * HEAD_DIM
LRU_BLOCKS = 8
CONV_W = 4
LRU_C = 8.0
IN_COLS = ATTN_WIDTH + 2 * KV_WIDTH + 2 * LRU_WIDTH
N_GROUPS = 4
EXPERTS_PER_GROUP = 8
N_EXPERTS = N_GROUPS * EXPERTS_PER_GROUP
TOP_K = 2
D_FF = 512
MOE_BLOCK = 256
ALPHA = 2.0 ** 0.25
EPS = 1e-5
NEG = -1e30
LOG2E = float(np.log2(np.e))
LANES = 128
SUBLANES = 8
PACKED = D_MODEL // 2

PROJ_ROWS = 1024
OUT_PROJ_ROWS = 1024
ROUTE_ROWS = 512
COMBINE_ROWS = 1024
COMBINE_PARTS = 4
X_GROUP = 4
X_AHEAD = 4
X_RING = X_AHEAD + X_GROUP
Y_RING = 2 * X_GROUP
VMEM_LIMIT = 48 * 1024 * 1024

SC_CORES = 2
SC_SUBCORES = 16
SC_WORKERS = SC_CORES * SC_SUBCORES
SC_WINDOW = 64


def _cparams(n_axes):
    return pltpu.CompilerParams(
        dimension_semantics=("arbitrary",) * n_axes, vmem_limit_bytes=VMEM_LIMIT)


def _in_proj_kernel(x_ref, w_ref, q_ref, kv_ref, xr_ref, yr_ref):
    proj = jnp.dot(x_ref[...].astype(BF16), w_ref[...], preferred_element_type=F32)
    o = 0
    for ref, width in ((q_ref, ATTN_WIDTH), (kv_ref, 2 * KV_WIDTH),
                       (xr_ref, LRU_WIDTH), (yr_ref, LRU_WIDTH)):
        ref[...] = proj[:, o:o + width].astype(ref.dtype)
        o += width


def _in_proj(x2d, w_bf16, rows):
    n = x2d.shape[0]
    widths = (ATTN_WIDTH, 2 * KV_WIDTH, LRU_WIDTH, LRU_WIDTH)
    return pl.pallas_call(
        _in_proj_kernel,
        grid=(n // rows,),
        in_specs=[pl.BlockSpec((rows, D_MODEL), lambda i: (i, 0)),
                  pl.BlockSpec((D_MODEL, IN_COLS), lambda i: (0, 0))],
        out_specs=[pl.BlockSpec((rows, w), lambda i: (i, 0)) for w in widths],
        out_shape=[jax.ShapeDtypeStruct((n, w), BF16) for w in widths],
        compiler_params=_cparams(1),
        name="in_proj",
    )(x2d, w_bf16)


def _attn_kernel(sinks_ref, q_ref, kv_ref, kvm_ref, bias_ref, g_ref, o_ref,
                 klo, khi, vlo, vhi):
    nbx = q_ref.shape[0] // BLOCK
    lo_lanes = lax.broadcasted_iota(I32, (BLOCK // 2, LANES), 1) < HEAD_DIM

    def layout_block(n, blk):
        rows = pl.ds(pl.multiple_of(n * BLOCK, BLOCK), BLOCK)
        as_bf16 = lambda words: pltpu.bitcast(words, BF16)
        for src, dst_lo, dst_hi in ((blk[:, :KV_WIDTH], klo, khi), (blk[:, KV_WIDTH:], vlo, vhi)):
            w = pltpu.bitcast(src, U32)
            r = pltpu.roll(w, HEAD_DIM, axis=1)
            zero = jnp.zeros_like(w)
            dst_lo[0, rows, :] = as_bf16(jnp.where(lo_lanes, w, zero))
            dst_hi[0, rows, :] = as_bf16(jnp.where(lo_lanes, zero, r))
            dst_lo[1, rows, :] = as_bf16(jnp.where(lo_lanes, r, zero))
            dst_hi[1, rows, :] = as_bf16(jnp.where(lo_lanes, zero, w))

    layout_block(0, kvm_ref[...])

    def layout_body(n, carry):
        layout_block(n + 1, kv_ref[pl.ds(pl.multiple_of(n * BLOCK, BLOCK), BLOCK), :])
        return carry

    lax.fori_loop(0, nbx, layout_body, 0, unroll=4)

    ones_lo = jnp.where(lax.broadcasted_iota(I32, (2 * BLOCK, LANES), 1) < HEAD_DIM,
                        1.0, 0.0).astype(BF16)
    ones_hi = (1.0 - ones_lo.astype(F32)).astype(BF16)
    top_rows = lax.broadcasted_iota(I32, (2 * BLOCK, 1), 0) < BLOCK
    lo_half = lax.broadcasted_iota(I32, (2 * BLOCK, LANES), 1) < HEAD_DIM

    def block(i, carry):
        q_rows = pl.ds(pl.multiple_of(i * BLOCK, BLOCK), BLOCK)
        win = pl.ds(pl.multiple_of(i * BLOCK, BLOCK), 2 * BLOCK)
        q = q_ref[q_rows, :]
        first = jnp.minimum(i, 1)
        outs = []
        for j in range(N_KV_HEADS):
            q2 = jnp.concatenate([q[:, (2 * j) * LANES:(2 * j + 1) * LANES],
                                  q[:, (2 * j + 1) * LANES:(2 * j + 2) * LANES]], axis=0)
            kc = jnp.concatenate([klo[j, win, :], khi[j, win, :]], axis=0)
            s = lax.dot_general(q2, kc, (((1,), (1,)), ((), ())), preferred_element_type=F32)
            s = s + bias_ref[first, j]
            ps, es = [], []
            for c in range(2):
                sink = jnp.where(top_rows, sinks_ref[4 * j + c], sinks_ref[4 * j + 2 + c])
                sc = s[:, c * 2 * BLOCK:(c + 1) * 2 * BLOCK]
                m = jnp.maximum(jnp.max(sc, axis=1, keepdims=True), sink)
                ps.append(jnp.exp2(sc - m).astype(BF16))
                es.append(jnp.exp2(sink - m))
            v_lo = jnp.concatenate([vlo[j, win, :], ones_lo], axis=1)
            v_hi = jnp.concatenate([vhi[j, win, :], ones_hi], axis=1)
            r = jnp.dot(jnp.concatenate(ps, axis=1), jnp.concatenate([v_lo, v_hi], axis=0),
                        preferred_element_type=F32)
            den = r[:, LANES:] + jnp.where(lo_half, es[0], es[1])
            o2 = r[:, :LANES] * (1.0 / den)
            outs += [o2[:BLOCK], o2[BLOCK:]]
        out = jnp.concatenate(outs, axis=1)
        ms = jnp.mean(out * out, axis=1, keepdims=True)
        o_ref[q_rows, :] = (out * lax.rsqrt(ms + EPS) * g_ref[...]).astype(o_ref.dtype)
        return carry

    lax.fori_loop(0, nbx, block, 0, unroll=8)


def _attention(sinks, q, kv, kvm, bias, g_attn, bsz, nbx):
    seq = nbx * BLOCK
    const2 = lambda b: (0, 0)
    kv_scratch = pltpu.VMEM((N_KV_HEADS, seq + BLOCK, LANES), BF16)
    return pl.pallas_call(
        _attn_kernel,
        grid=(bsz,),
        in_specs=[pl.BlockSpec(memory_space=pltpu.SMEM),
                  pl.BlockSpec((None, seq, ATTN_WIDTH), lambda b: (b, 0, 0)),
                  pl.BlockSpec((None, seq, 2 * KV_WIDTH), lambda b: (b, 0, 0)),
                  pl.BlockSpec((BLOCK, 2 * KV_WIDTH), const2),
                  pl.BlockSpec((2, N_KV_HEADS, 2 * BLOCK, 4 * BLOCK), lambda b: (0, 0, 0, 0)),
                  pl.BlockSpec((1, ATTN_WIDTH), const2)],
        out_specs=pl.BlockSpec((None, seq, ATTN_WIDTH), lambda b: (b, 0, 0)),
        out_shape=jax.ShapeDtypeStruct((bsz, seq, ATTN_WIDTH), BF16),
        scratch_shapes=[kv_scratch, kv_scratch, kv_scratch, kv_scratch],
        compiler_params=_cparams(1),
        name="attention",
    )(sinks, q, kv, kvm, bias, g_attn)


def _sigmoid(v):
    return 0.5 * jnp.tanh(0.5 * v) + 0.5


def _gelu_tanh(y):
    c = float(np.sqrt(2.0 / np.pi))
    half = 0.5 * y
    return half + half * jnp.tanh(y * (c + (c * 0.044715) * (y * y)))


LRU_CHUNK = 44
LRU_SEG = SUBLANES * LRU_CHUNK
LRU_SLABS = LRU_WIDTH // LANES


def _lru_kernel(xr_ref, yr_ref, xrm_ref, yrm_ref, cw_ref, cb_ref, wa_ref, wx_ref, ba_ref,
                bx_ref, lam_ref, g_ref, o_ref, x_st, y_st, o_st, s_st, xtail, hcar):
    seq = xr_ref.shape[0]
    n_seg = (seq + BLOCK) // LRU_SEG
    xtail[...] = jnp.zeros_like(xtail)
    hcar[...] = jnp.zeros_like(hcar)
    lam = lam_ref[...]
    softplus_neg = jnp.maximum(-lam, 0.0) + jnp.log(1.0 + jnp.exp(-jnp.abs(lam)))
    sub = lax.broadcasted_iota(jnp.int32, (SUBLANES, LANES), 0)

    def strided(j):
        return pl.ds(j, SUBLANES, stride=LRU_CHUNK)

    def piece(v, j):
        return v[j * SUBLANES:(j + 1) * SUBLANES, :]

    def segment(k, first):
        if first:
            head = LRU_SEG - BLOCK
            x_nat = jnp.concatenate([xrm_ref[...], xr_ref[0:head, :]], axis=0).astype(F32)
            y_nat = jnp.concatenate([yrm_ref[...], yr_ref[0:head, :]], axis=0).astype(F32)
        else:
            rows = pl.ds(pl.multiple_of(k * LRU_SEG - BLOCK, 2 * SUBLANES), LRU_SEG)
            x_nat = xr_ref[rows, :].astype(F32)
            y_nat = yr_ref[rows, :].astype(F32)
        for c in range(LRU_SLABS):
            x_st[c] = x_nat[:, c * LANES:(c + 1) * LANES]
            y_st[c] = y_nat[:, c * LANES:(c + 1) * LANES]
        first_row = k * LRU_SEG + LRU_CHUNK * sub
        sumsq = [jnp.zeros((SUBLANES, LANES), F32) for _ in range(LRU_CHUNK)]

        for c in range(LRU_SLABS):
            lanes = slice(c * LANES, (c + 1) * LANES)
            x = [x_st[c, strided(j), :] for j in range(LRU_CHUNK)]
            before = []
            for d in range(1, CONV_W):
                from_prev_chunk = pltpu.roll(x[LRU_CHUNK - d], 1, axis=0)
                before.append(jnp.where(sub == 0, xtail[d - 1:d, lanes], from_prev_chunk))
            for d in range(1, CONV_W):
                xtail[d - 1:d, lanes] = x[LRU_CHUNK - d][SUBLANES - 1:SUBLANES, :]

            def x_at(j):
                return x[j] if j >= 0 else before[-j - 1]

            taps = [cw_ref[t:t + 1, lanes] for t in range(CONV_W)]
            bias = cb_ref[:, lanes]
            xc = jnp.concatenate(
                [bias + sum(taps[t] * x_at(j - (CONV_W - 1) + t) for t in range(CONV_W))
                 for j in range(LRU_CHUNK)], axis=0)
            xcb = xc.astype(BF16)
            both = jnp.dot(xcb, jnp.concatenate([wa_ref[c], wx_ref[c]], axis=1),
                           preferred_element_type=F32)
            tr = jnp.tanh(both[:, :LANES] + ba_ref[:, lanes])
            ti = jnp.tanh(both[:, LANES:] + bx_ref[:, lanes])
            log_a_half = (-0.5 * LRU_C) * softplus_neg[:, lanes]
            a = jnp.exp(log_a_half * tr + log_a_half)
            half_xc = 0.5 * xc
            gated_x = half_xc * ti + half_xc
            z = 1.0 - a * a
            u = jnp.where(z > 0.0, z * lax.rsqrt(z), 0.0) * gated_x

            h = jnp.zeros((SUBLANES, LANES), F32)
            p = jnp.ones((SUBLANES, LANES), F32)
            hs, ps = [], []
            for j in range(LRU_CHUNK):
                aj = piece(a, j)
                uj = piece(u, j)
                if first:
                    uj = jnp.where(first_row + j >= PAD_FRONT, uj, 0.0)
                h = aj * h + uj
                p = aj * p
                hs.append(h)
                ps.append(p)
            entry = [hcar[:, lanes]]
            for s in range(SUBLANES):
                entry.append(h[s:s + 1, :] + p[s:s + 1, :] * entry[s])
            hcar[:, lanes] = entry[SUBLANES]
            entry_rows = jnp.concatenate(entry[:SUBLANES], axis=0)

            for j in range(LRU_CHUNK):
                state = hs[j] + ps[j] * entry_rows
                out = state * _gelu_tanh(y_st[c, strided(j), :])
                sumsq[j] = sumsq[j] + out * out
                o_st[c, strided(j), :] = out

        for j in range(LRU_CHUNK):
            ms = jnp.sum(sumsq[j], axis=1, keepdims=True) * (1.0 / LRU_WIDTH)
            s_st[strided(j), :] = jnp.broadcast_to(lax.rsqrt(ms + EPS), (SUBLANES, LANES))
        scale = s_st[...]
        for c in range(LRU_SLABS):
            lanes = slice(c * LANES, (c + 1) * LANES)
            normed = (o_st[c] * scale * g_ref[:, lanes]).astype(o_ref.dtype)
            if first:
                o_ref[0:LRU_SEG - BLOCK, lanes] = normed[BLOCK:, :]
            else:
                o_ref[rows, lanes] = normed

    assert BLOCK <= LRU_SEG
    segment(0, True)

    def later_segment(k, carry):
        segment(k, False)
        return carry

    lax.fori_loop(1, n_seg, later_segment, 0)


def _rglru(xr, yr, xrm, yrm, cw, cb, wa, wx, ba, bx, lam, g_lru, bsz, nbx):
    seq = nbx * BLOCK
    assert (seq + BLOCK) % LRU_SEG == 0
    main = pl.BlockSpec((None, seq, LRU_WIDTH), lambda b: (b, 0, 0))
    const2 = lambda b: (0, 0)
    row_spec = pl.BlockSpec((1, LRU_WIDTH), const2)
    gate_spec = pl.BlockSpec((LRU_SLABS, LANES, LANES), lambda b: (0, 0, 0))
    slabs = pltpu.VMEM((LRU_SLABS, LRU_SEG, LANES), F32)
    return pl.pallas_call(
        _lru_kernel,
        grid=(bsz,),
        in_specs=[main, main,
                  pl.BlockSpec((BLOCK, LRU_WIDTH), const2),
                  pl.BlockSpec((BLOCK, LRU_WIDTH), const2),
                  pl.BlockSpec((CONV_W, LRU_WIDTH), const2),
                  row_spec, gate_spec, gate_spec,
                  row_spec, row_spec, row_spec, row_spec],
        out_specs=main,
        out_shape=jax.ShapeDtypeStruct((bsz, seq, LRU_WIDTH), BF16),
        scratch_shapes=[slabs, slabs, slabs,
                        pltpu.VMEM((LRU_SEG, LANES), F32),
                        pltpu.VMEM((SUBLANES, LRU_WIDTH), F32),
                        pltpu.VMEM((1, LRU_WIDTH), F32)],
        compiler_params=_cparams(1),
        name="rglru",
    )(xr, yr, xrm, yrm, cw, cb, wa, wx, ba, bx, lam, g_lru)


def _pack_rows(v):
    bits = lax.bitcast_convert_type(v.astype(BF16).astype(F32), U32)
    return (bits[:, :PACKED] >> 16) | (bits[:, PACKED:] & jnp.uint32(0xFFFF0000))


def _unpack_rows(w):
    lo = lax.bitcast_convert_type(w << 16, F32)
    hi = lax.bitcast_convert_type(w & jnp.uint32(0xFFFF0000), F32)
    return lo, hi


def _layer_norm(z, g, b):
    mu = jnp.mean(z, axis=1, keepdims=True)
    zc = z - mu
    var = jnp.mean(zc * zc, axis=1, keepdims=True)
    return zc * lax.rsqrt(var + EPS) * g + b


def _out_proj_kernel(a_ref, l_ref, x_ref, w_ref, g_ref, b_ref, wrt_ref, brt_ref,
                     h_ref, hp_ref, info_ref, cnt_ref, w_b, wrt_b):
    @pl.when(pl.program_id(0) == 0)
    def _():
        cnt_ref[...] = jnp.zeros_like(cnt_ref)
        w_b[...] = w_ref[...].astype(BF16)
        hi = wrt_ref[...].astype(BF16)
        wrt_b[0:ROUTER_ROWS, :] = hi
        wrt_b[ROUTER_ROWS:, :] = (wrt_ref[...] - hi.astype(F32)).astype(BF16)

    mix = jnp.dot(jnp.concatenate([a_ref[...], l_ref[...]], axis=1), w_b[...],
                  preferred_element_type=F32)
    h = _layer_norm(ALPHA * x_ref[...] + mix, g_ref[...], b_ref[...])
    h_ref[...] = h
    hp_ref[...] = _pack_rows(h)

    h_hi = h.astype(BF16)
    h_lo = (h - h_hi.astype(F32)).astype(BF16)
    nt = (((1,), (1,)), ((), ()))
    both = lax.dot_general(wrt_b[...], h_hi, nt, preferred_element_type=F32)
    lg = (both[:ROUTER_ROWS] + both[ROUTER_ROWS:]
          + lax.dot_general(wrt_b[0:ROUTER_ROWS, :], h_lo, nt, preferred_element_type=F32)
          ) + brt_ref[...]
    tile_shape = (SUBLANES, h.shape[0])
    sub = lax.broadcasted_iota(I32, tile_shape, 0)
    ninf = -jnp.inf
    t0 = lg[0:SUBLANES]
    gl = jnp.where(sub < N_GROUPS, t0, ninf)
    gmax = jnp.max(gl, axis=0, keepdims=True)
    g_idx = jnp.min(jnp.where(gl == gmax, sub, SUBLANES), axis=0, keepdims=True)
    g_w = 1.0 / jnp.sum(jnp.where(sub < N_GROUPS, jnp.exp(t0 - gmax), 0.0),
                        axis=0, keepdims=True)
    el = lg[SUBLANES:2 * SUBLANES]
    for g in range(1, N_GROUPS):
        el = jnp.where(g_idx == g, lg[(g + 1) * SUBLANES:(g + 2) * SUBLANES], el)
    v1 = jnp.max(el, axis=0, keepdims=True)
    i1 = jnp.min(jnp.where(el == v1, sub, SUBLANES), axis=0, keepdims=True)
    el2 = jnp.where(sub == i1, ninf, el)
    v2 = jnp.max(el2, axis=0, keepdims=True)
    i2 = jnp.min(jnp.where(el2 == v2, sub, SUBLANES), axis=0, keepdims=True)
    t = jnp.exp(v2 - v1)
    w1 = 1.0 / (1.0 + t)
    w2 = t * w1
    e_base = g_idx * EXPERTS_PER_GROUP
    e1 = e_base + i1
    e2 = e_base + i2
    info_ref[...] = jnp.where(sub == 0, e1.astype(F32),
                              jnp.where(sub == 1, e2.astype(F32),
                                        jnp.where(sub == 2, g_w * w1,
                                                  jnp.where(sub == 3, g_w * w2, 0.0))))
    expert = lax.broadcasted_iota(I32, (N_EXPERTS, h.shape[0]), 0)
    chosen = (expert == e1).astype(F32) + (expert == e2).astype(F32)
    cnt_ref[...] += jnp.sum(chosen, axis=1, keepdims=True)


ROUTER_ROWS = -(-(N_GROUPS + 1) * SUBLANES // (2 * SUBLANES)) * (2 * SUBLANES)


def _out_proj(attn_n, lru_n, x2d, w_out, ln_g, ln_b, w_rt, b_rt, rows):
    n = x2d.shape[0]
    const = lambda i: (0, 0)
    tile = lambda w: pl.BlockSpec((rows, w), lambda i: (i, 0))
    return pl.pallas_call(
        _out_proj_kernel,
        grid=(n // rows,),
        in_specs=[tile(ATTN_WIDTH), tile(LRU_WIDTH), tile(D_MODEL),
                  pl.BlockSpec((D_MODEL, D_MODEL), const),
                  pl.BlockSpec((1, D_MODEL), const),
                  pl.BlockSpec((1, D_MODEL), const),
                  pl.BlockSpec((ROUTER_ROWS, D_MODEL), const),
                  pl.BlockSpec((ROUTER_ROWS, 1), const)],
        out_specs=[tile(D_MODEL), tile(PACKED),
                   pl.BlockSpec((SUBLANES, rows), lambda i: (0, i)),
                   pl.BlockSpec((N_EXPERTS, LANES), const)],
        out_shape=[jax.ShapeDtypeStruct((n, D_MODEL), F32),
                   jax.ShapeDtypeStruct((n, PACKED), U32),
                   jax.ShapeDtypeStruct((SUBLANES, n), F32),
                   jax.ShapeDtypeStruct((N_EXPERTS, LANES), F32)],
        scratch_shapes=[pltpu.VMEM((D_MODEL, D_MODEL), BF16),
                        pltpu.VMEM((2 * ROUTER_ROWS, D_MODEL), BF16)],
        compiler_params=_cparams(1),
        name="out_proj",
    )(attn_n, lru_n, x2d, w_out, ln_g, ln_b, w_rt, b_rt)


def _route_kernel(info_ref, cnt_ref, tri_ref, dest_ref, carry, pstart):
    t = pl.program_id(0)
    info = info_ref[...]
    shape = (N_EXPERTS, info.shape[1])
    expert = lax.broadcasted_iota(I32, shape, 0)
    oh1 = (expert == info[0:1, :].astype(I32)).astype(F32)
    oh2 = (expert == info[1:2, :].astype(I32)).astype(F32)
    both = oh1 + oh2

    @pl.when(t == 0)
    def _():
        c = cnt_ref[...].astype(I32)
        padded = ((c + (MOE_BLOCK - 1)) // MOE_BLOCK) * MOE_BLOCK
        e = lax.broadcasted_iota(I32, (N_EXPERTS, LANES), 0)
        scan = padded
        for d in (1, 2, 4, 8, 16):
            scan = scan + jnp.where(e >= d, pltpu.roll(scan, d, axis=0), 0)
        pstart[...] = (scan - padded)[:, 0:1].astype(F32)
        carry[...] = jnp.zeros_like(carry)

    before = jnp.dot(both.astype(BF16), tri_ref[...], preferred_element_type=F32)
    row_of = before + (carry[...] + pstart[...])
    r1 = jnp.sum(oh1 * row_of, axis=0, keepdims=True)
    r2 = jnp.sum(oh2 * row_of, axis=0, keepdims=True)
    dest_ref[...] = jnp.concatenate([r1, r2], axis=0).astype(I32)
    carry[...] += jnp.sum(both, axis=1, keepdims=True)


def _route(info_t, cnt, cols):
    n = info_t.shape[1]
    tri = jnp.asarray(np.triu(np.ones((cols, cols), np.float32), 1), BF16)
    return pl.pallas_call(
        _route_kernel,
        grid=(n // cols,),
        in_specs=[pl.BlockSpec((SUBLANES, cols), lambda t: (0, t)),
                  pl.BlockSpec((N_EXPERTS, LANES), lambda t: (0, 0)),
                  pl.BlockSpec((cols, cols), lambda t: (0, 0))],
        out_specs=pl.BlockSpec((TOP_K, cols), lambda t: (0, t)),
        out_shape=jax.ShapeDtypeStruct((TOP_K, n), I32),
        scratch_shapes=[pltpu.VMEM((N_EXPERTS, 1), F32), pltpu.VMEM((N_EXPERTS, 1), F32)],
        compiler_params=_cparams(1),
        name="route",
    )(info_t, cnt, tri)


def _sc_mesh():
    return plsc.VectorSubcoreMesh(core_axis_name="core", subcore_axis_name="subcore")


def _sc_worker_id():
    return lax.axis_index("subcore") * SC_CORES + lax.axis_index("core")


def _sc_scatter_rows(rows, dest, cap):
    n, width = rows.shape
    per_worker = n // SC_WORKERS
    n_win = per_worker // SC_WINDOW

    def body(x_hbm, d_hbm, o_hbm, i0_v, i1_v, rows_v, isem, rsem, sem0, sem1):
        wid = _sc_worker_id()

        def indices(j):
            win = pl.ds(wid * per_worker + j * SC_WINDOW, SC_WINDOW)
            return (pltpu.make_async_copy(d_hbm.at[0, win], i0_v.at[j], isem.at[0, j]),
                    pltpu.make_async_copy(d_hbm.at[1, win], i1_v.at[j], isem.at[1, j]))

        def read(j):
            src = x_hbm.at[pl.ds(wid * per_worker + j * SC_WINDOW, SC_WINDOW)]
            return pltpu.make_async_copy(src, rows_v.at[j % 2], rsem.at[j % 2])

        def scatters(j):
            return (pltpu.make_async_copy(rows_v.at[j % 2], o_hbm.at[i0_v.at[j]], sem0.at[j % 2]),
                    pltpu.make_async_copy(rows_v.at[j % 2], o_hbm.at[i1_v.at[j]], sem1.at[j % 2]))

        read(0).start()
        for j in range(n_win):
            for cp in indices(j):
                cp.start()
        for j in range(n_win):
            for cp in indices(j):
                cp.wait()
        for j in range(n_win):
            if j + 1 < n_win:
                if j >= 1:
                    for cp in scatters(j - 1):
                        cp.wait()
                read(j + 1).start()
            read(j).wait()
            for cp in scatters(j):
                cp.start()
        for j in range(max(n_win - 2, 0), n_win):
            for cp in scatters(j):
                cp.wait()

    return pl.kernel(
        body,
        out_type=jax.ShapeDtypeStruct((cap, width), rows.dtype),
        mesh=_sc_mesh(),
        scratch_types=[pltpu.VMEM((n_win, SC_WINDOW), I32), pltpu.VMEM((n_win, SC_WINDOW), I32),
                       pltpu.VMEM((2, SC_WINDOW, width), rows.dtype),
                       pltpu.SemaphoreType.DMA((TOP_K, n_win)),
                       pltpu.SemaphoreType.DMA((2,)), pltpu.SemaphoreType.DMA((2,)),
                       pltpu.SemaphoreType.DMA((2,))],
        name="dispatch",
    )(rows, dest)


def _sc_gather_rows(table, dest, start, length):
    width = table.shape[1]
    per_worker = TOP_K * length // SC_WORKERS
    n_win = per_worker // SC_WINDOW
    workers_per_choice = SC_WORKERS // TOP_K

    def body(y_hbm, d_hbm, o_hbm, i_v, rows_v, isem, gsem, wsem):
        wid = _sc_worker_id()
        choice = wid // workers_per_choice
        first = start + (wid % workers_per_choice) * per_worker

        def indices(j):
            win = pl.ds(first + j * SC_WINDOW, SC_WINDOW)
            return pltpu.make_async_copy(d_hbm.at[choice, win], i_v.at[j], isem.at[j])

        for j in range(n_win):
            indices(j).start()
        for j in range(n_win):
            indices(j).wait()

        def gather(j):
            return pltpu.make_async_copy(y_hbm.at[i_v.at[j]], rows_v.at[j % 2], gsem.at[j % 2])

        def write(j):
            dst = o_hbm.at[pl.ds(wid * per_worker + j * SC_WINDOW, SC_WINDOW)]
            return pltpu.make_async_copy(rows_v.at[j % 2], dst, wsem.at[j % 2])

        gather(0).start()
        for j in range(n_win):
            if j + 1 < n_win:
                if j >= 1:
                    write(j - 1).wait()
                gather(j + 1).start()
            gather(j).wait()
            write(j).start()
        for j in range(max(n_win - 2, 0), n_win):
            write(j).wait()

    return pl.kernel(
        body,
        out_type=jax.ShapeDtypeStruct((SC_WORKERS * per_worker, width), table.dtype),
        mesh=_sc_mesh(),
        scratch_types=[pltpu.VMEM((n_win, SC_WINDOW), I32),
                       pltpu.VMEM((2, SC_WINDOW, width), table.dtype),
                       pltpu.SemaphoreType.DMA((n_win,)),
                       pltpu.SemaphoreType.DMA((2,)), pltpu.SemaphoreType.DMA((2,))],
        name="collect",
    )(table, dest)


def _expert_kernel(bstart_ref, nblk_ref, nused_ref, xs_hbm, wg_ref, wu_ref, wd_ref, yb_hbm,
                   xbuf, ybuf, zbuf, xsem, ysem, zsem, wg_b, wu_b, wd_b):
    e = pl.program_id(0)
    nused = nused_ref[0]
    n_blocks = yb_hbm.shape[0] // MOE_BLOCK

    def rows(b):
        return pl.ds(pl.multiple_of(b * MOE_BLOCK, MOE_BLOCK), MOE_BLOCK)

    def x_copy(b):
        slot = b % X_RING
        return pltpu.make_async_copy(xs_hbm.at[rows(b)], xbuf.at[slot], xsem.at[slot])

    def y_copy(b):
        slot = b % Y_RING
        return pltpu.make_async_copy(ybuf.at[slot], yb_hbm.at[rows(b)], ysem.at[slot])

    @pl.when(e == 0)
    def _():
        for b in range(X_AHEAD):
            @pl.when(b < nused)
            def _():
                x_copy(b).start()

    wg_b[...] = wg_ref[...].astype(BF16)
    wu_b[...] = wu_ref[...].astype(BF16)
    wd_b[...] = wd_ref[...].astype(BF16)

    def run_blocks(b, count):
        for k in range(count):
            x_copy(b + k).wait()
        for k in range(count):
            nxt = b + X_AHEAD + k

            @pl.when(nxt < nused)
            def _():
                x_copy(nxt).start()

            @pl.when(b + k >= Y_RING)
            def _():
                y_copy(b + k - Y_RING).wait()

        words = jnp.concatenate([xbuf[(b + k) % X_RING] for k in range(count)], axis=0)
        lo, hi = _unpack_rows(words)
        x = jnp.concatenate([lo.astype(BF16), hi.astype(BF16)], axis=1)
        g = jnp.dot(x, wg_b[...], preferred_element_type=F32)
        u = jnp.dot(x, wu_b[...], preferred_element_type=F32)
        mid = (g * _sigmoid(g) * u).astype(BF16)
        y = _pack_rows(jnp.dot(mid, wd_b[...], preferred_element_type=F32))
        for k in range(count):
            ybuf[(b + k) % Y_RING] = y[k * MOE_BLOCK:(k + 1) * MOE_BLOCK]
            y_copy(b + k).start()

    b0 = bstart_ref[e]
    nb = nblk_ref[e]

    def group(i, carry):
        run_blocks(b0 + X_GROUP * i, X_GROUP)
        return carry

    lax.fori_loop(0, nb // X_GROUP, group, 0)
    done = nb - nb % X_GROUP
    size = X_GROUP // 2
    while size >= 1:
        @pl.when((nb // size) % 2 == 1)
        def _(size=size, done=done):
            run_blocks(b0 + done, size)

        done = done + (nb // size) % 2 * size
        size //= 2

    @pl.when(e == pl.num_programs(0) - 1)
    def _():
        for back in range(Y_RING, 0, -1):
            @pl.when(nused >= back)
            def _():
                y_copy(nused - back).wait()

        zbuf[...] = jnp.zeros_like(zbuf)

        def z_copy(b):
            return pltpu.make_async_copy(zbuf, yb_hbm.at[rows(b)], zsem.at[0])

        def z_start(b, carry):
            z_copy(b).start()
            return carry

        def z_wait(b, carry):
            z_copy(b).wait()
            return carry

        lax.fori_loop(nused, n_blocks, z_start, 0)
        lax.fori_loop(nused, n_blocks, z_wait, 0)


def _experts(bstart, nblk, nused, xs, w_gate, w_up, w_down):
    cap = xs.shape[0]
    w_idx = lambda e, bs, nb, nu: (e, 0, 0)
    grid_spec = pltpu.PrefetchScalarGridSpec(
        num_scalar_prefetch=3,
        grid=(N_EXPERTS,),
        in_specs=[pl.BlockSpec(memory_space=pl.ANY),
                  pl.BlockSpec((None, D_MODEL, D_FF), w_idx),
                  pl.BlockSpec((None, D_MODEL, D_FF), w_idx),
                  pl.BlockSpec((None, D_FF, D_MODEL), w_idx)],
        out_specs=pl.BlockSpec(memory_space=pl.ANY),
        scratch_shapes=[pltpu.VMEM((X_RING, MOE_BLOCK, PACKED), U32),
                        pltpu.VMEM((Y_RING, MOE_BLOCK, PACKED), U32),
                        pltpu.VMEM((MOE_BLOCK, PACKED), U32),
                        pltpu.SemaphoreType.DMA((X_RING,)),
                        pltpu.SemaphoreType.DMA((Y_RING,)),
                        pltpu.SemaphoreType.DMA((1,)),
                        pltpu.VMEM((D_MODEL, D_FF), BF16),
                        pltpu.VMEM((D_MODEL, D_FF), BF16),
                        pltpu.VMEM((D_FF, D_MODEL), BF16)])
    return pl.pallas_call(
        _expert_kernel,
        grid_spec=grid_spec,
        out_shape=jax.ShapeDtypeStruct((cap, PACKED), U32),
        compiler_params=_cparams(1),
        name="experts",
    )(bstart, nblk, nused, xs, w_gate, w_up, w_down)


def _combine_kernel(y0_ref, y1_ref, h_ref, info_ref, g_ref, b_ref, o_ref):
    info = info_ref[...].T
    g0 = info[:, 2:3]
    g1 = info[:, 3:4]
    lo0, hi0 = _unpack_rows(y0_ref[...])
    lo1, hi1 = _unpack_rows(y1_ref[...])
    y = jnp.concatenate([g0 * lo0 + g1 * lo1, g0 * hi0 + g1 * hi1], axis=1)
    o_ref[...] = _layer_norm(ALPHA * h_ref[...] + y, g_ref[...], b_ref[...])


def _combine(ys, h, info, ln_g, ln_b, rows, part, n_parts):
    n = h.shape[0]
    steps = n // n_parts // rows
    off = part * steps
    const = lambda i: (0, 0)
    return pl.pallas_call(
        _combine_kernel,
        grid=(steps,),
        in_specs=[pl.BlockSpec((rows, PACKED), lambda i: (i, 0)),
                  pl.BlockSpec((rows, PACKED), lambda i: (i + steps, 0)),
                  pl.BlockSpec((rows, D_MODEL), lambda i: (i + off, 0)),
                  pl.BlockSpec((SUBLANES, rows), lambda i: (0, i + off)),
                  pl.BlockSpec((1, D_MODEL), const),
                  pl.BlockSpec((1, D_MODEL), const)],
        out_specs=pl.BlockSpec((rows, D_MODEL), lambda i: (i + off, 0)),
        out_shape=jax.ShapeDtypeStruct((n, D_MODEL), F32),
        input_output_aliases={2: 0},
        compiler_params=_cparams(1),
        name="combine",
    )(ys, ys, h, info, ln_g, ln_b)


def _alibi_bias():
    qi = np.arange(BLOCK)[:, None]
    kj = np.arange(2 * BLOCK)[None, :]
    dist = qi - kj + BLOCK
    band = (dist >= 0) & (dist < BLOCK)
    slopes = np.exp2(-8.0 * np.arange(1, N_Q_HEADS + 1, dtype=np.float32) / N_Q_HEADS)
    bias = np.where(band[None], -slopes[:, None, None] * dist[None].astype(np.float32), NEG)
    bias = bias * LOG2E
    first = np.where((kj >= PAD_FRONT)[None], bias, NEG)
    out = np.empty((2, N_KV_HEADS, 2 * BLOCK, 4 * BLOCK), np.float32)
    for v, per_head in enumerate((first, bias)):
        for j in range(N_KV_HEADS):
            out[v, j] = np.block([[per_head[4 * j], per_head[4 * j + 1]],
                                  [per_head[4 * j + 2], per_head[4 * j + 3]]])
    return jnp.asarray(out, F32)


def _slab_gates(w):
    nb, c, _ = w.shape
    per = LANES // c
    w = w.reshape(nb // per, per, c, c)
    eye = jnp.eye(per, dtype=w.dtype)
    return jnp.einsum('spcd,pq->spcqd', w, eye).reshape(nb // per, LANES, LANES).astype(BF16)


def kernel(x, meta_tokens, w_in, conv_w, conv_b, lru_wa, lru_ba, lru_wx, lru_bx, lru_lambda,
           attn_sinks, g_attn, g_lru, w_out, ln1_g, ln1_b, w_group, b_group, w_router,
           b_router, w_gate, w_up, w_down, ln2_g, ln2_b):
    bsz, seq, d = x.shape
    nbx = seq // BLOCK
    n_tok = bsz * seq
    x2d = x.reshape(n_tok, d)
    row = lambda v: v.reshape(1, -1).astype(F32)

    q_scale = jnp.concatenate([jnp.full((ATTN_WIDTH,), LOG2E * HEAD_DIM ** -0.5, F32),
                               jnp.ones((IN_COLS - ATTN_WIDTH,), F32)])
    w_in_b = (w_in[0] * q_scale).astype(BF16)
    meta_blk = jnp.concatenate([jnp.zeros((PAD_FRONT, d), F32), meta_tokens.astype(F32)], axis=0)
    q, kv, xr, yr = _in_proj(x2d, w_in_b, PROJ_ROWS)
    qm, kvm, xrm, yrm = _in_proj(meta_blk, w_in_b, BLOCK)
    shp = lambda a: a.reshape(bsz, seq, a.shape[-1])

    attn_n = _attention(attn_sinks[0].astype(F32) * LOG2E, shp(q), shp(kv), kvm, _alibi_bias(),
                        row(g_attn[0]), bsz, nbx)
    lru_n = _rglru(shp(xr), shp(yr), xrm, yrm, conv_w[0].astype(F32), row(conv_b[0]),
                   _slab_gates(0.5 * lru_wa[0]), _slab_gates(0.5 * lru_wx[0]),
                   0.5 * row(lru_ba[0]), 0.5 * row(lru_bx[0]), row(lru_lambda[0]),
                   row(g_lru[0]), bsz, nbx)

    gpad = SUBLANES - N_GROUPS
    tail = ROUTER_ROWS - SUBLANES - N_EXPERTS
    w_rt = jnp.concatenate(
        [w_group[0].T, jnp.zeros((gpad, d), F32),
         jnp.transpose(w_router[0], (0, 2, 1)).reshape(N_EXPERTS, d),
         jnp.zeros((tail, d), F32)], axis=0).astype(F32)
    b_rt = jnp.concatenate([b_group[0], jnp.zeros((gpad,), F32), b_router[0].reshape(-1),
                            jnp.zeros((tail,), F32)]).astype(F32).reshape(ROUTER_ROWS, 1)
    h1, hp, info, cnt = _out_proj(
        attn_n.reshape(n_tok, ATTN_WIDTH), lru_n.reshape(n_tok, LRU_WIDTH), x2d,
        w_out[0].astype(F32), row(ln1_g[0]), row(ln1_b[0]), w_rt, b_rt, OUT_PROJ_ROWS)

    dest = _route(info, cnt, ROUTE_ROWS)
    n_slots = n_tok * TOP_K
    n_blocks = n_slots // MOE_BLOCK + N_EXPERTS
    cap = n_blocks * MOE_BLOCK
    nblk = (cnt[:, 0].astype(I32) + MOE_BLOCK - 1) // MOE_BLOCK
    bends = jnp.cumsum(nblk)
    bstart = (bends - nblk).astype(I32)
    nused = bends[-1:].astype(I32)

    xs = _sc_scatter_rows(hp, dest, cap)
    yb = _experts(bstart, nblk.astype(I32), nused, xs, w_gate[0], w_up[0], w_down[0])
    out = h1
    part_len = n_tok // COMBINE_PARTS
    for part in range(COMBINE_PARTS):
        ys = _sc_gather_rows(yb, dest, part * part_len, part_len)
        out = _combine(ys, out, info, row(ln2_g[0]), row(ln2_b[0]), COMBINE_ROWS,
                       part, COMBINE_PARTS)
    return out.reshape(bsz, seq, d)
```

```python
import jax
import jax.numpy as jnp
import numpy as np
from jax import lax
from jax.experimental import pallas as pl
from jax.experimental.pallas import tpu as pltpu
from jax.experimental.pallas import tpu_sc as plsc

F32 = jnp.float32
BF16 = jnp.bfloat16
U32 = jnp.uint32
I32 = jnp.int32

D_MODEL = 1024
N_META = 16
BLOCK = 128
PAD_FRONT = BLOCK - N_META
HEAD_DIM = 64
ATTN_WIDTH = 512
LRU_WIDTH = 512
N_Q_HEADS = 8
N_KV_HEADS = 2
KV_WIDTH = N_KV_HEADS * HEAD_DIM
LRU_BLOCKS = 8
CONV_W = 4
LRU_C = 8.0
IN_COLS = ATTN_WIDTH + 2 * KV_WIDTH + 2 * LRU_WIDTH
N_GROUPS = 4
EXPERTS_PER_GROUP = 8
N_EXPERTS = N_GROUPS * EXPERTS_PER_GROUP
TOP_K = 2
D_FF = 512
MOE_BLOCK = 256
ALPHA = 2.0 ** 0.25
EPS = 1e-5
NEG = -1e30
LOG2E = float(np.log2(np.e))
LANES = 128
SUBLANES = 8
PACKED = D_MODEL // 2

PROJ_ROWS = 1024
OUT_PROJ_ROWS = 1024
ROUTE_ROWS = 512
COMBINE_ROWS = 1024
COMBINE_PARTS = 4
X_GROUP = 4
X_AHEAD = 4
X_RING = X_AHEAD + X_GROUP
Y_RING = 2 * X_GROUP
VMEM_LIMIT = 48 * 1024 * 1024

SC_CORES = 2
SC_SUBCORES = 16
SC_WORKERS = SC_CORES * SC_SUBCORES
SC_WINDOW = 64


def _cparams(n_axes):
    return pltpu.CompilerParams(
        dimension_semantics=("arbitrary",) * n_axes, vmem_limit_bytes=VMEM_LIMIT)


def _in_proj_kernel(x_ref, w_ref, q_ref, kv_ref, xr_ref, yr_ref):
    proj = jnp.dot(x_ref[...].astype(BF16), w_ref[...], preferred_element_type=F32)
    o = 0
    for ref, width in ((q_ref, ATTN_WIDTH), (kv_ref, 2 * KV_WIDTH),
                       (xr_ref, LRU_WIDTH), (yr_ref, LRU_WIDTH)):
        ref[...] = proj[:, o:o + width].astype(ref.dtype)
        o += width


def _in_proj(x2d, w_bf16, rows):
    n = x2d.shape[0]
    widths = (ATTN_WIDTH, 2 * KV_WIDTH, LRU_WIDTH, LRU_WIDTH)
    return pl.pallas_call(
        _in_proj_kernel,
        grid=(n // rows,),
        in_specs=[pl.BlockSpec((rows, D_MODEL), lambda i: (i, 0)),
                  pl.BlockSpec((D_MODEL, IN_COLS), lambda i: (0, 0))],
        out_specs=[pl.BlockSpec((rows, w), lambda i: (i, 0)) for w in widths],
        out_shape=[jax.ShapeDtypeStruct((n, w), BF16) for w in widths],
        compiler_params=_cparams(1),
        name="in_proj",
    )(x2d, w_bf16)


def _attn_kernel(sinks_ref, q_ref, kv_ref, kvm_ref, bias_ref, g_ref, o_ref,
                 klo, khi, vlo, vhi):
    nbx = q_ref.shape[0] // BLOCK
    lo_lanes = lax.broadcasted_iota(I32, (BLOCK // 2, LANES), 1) < HEAD_DIM

    def layout_block(n, blk):
        rows = pl.ds(pl.multiple_of(n * BLOCK, BLOCK), BLOCK)
        as_bf16 = lambda words: pltpu.bitcast(words, BF16)
        for src, dst_lo, dst_hi in ((blk[:, :KV_WIDTH], klo, khi), (blk[:, KV_WIDTH:], vlo, vhi)):
            w = pltpu.bitcast(src, U32)
            r = pltpu.roll(w, HEAD_DIM, axis=1)
            zero = jnp.zeros_like(w)
            dst_lo[0, rows, :] = as_bf16(jnp.where(lo_lanes, w, zero))
            dst_hi[0, rows, :] = as_bf16(jnp.where(lo_lanes, zero, r))
            dst_lo[1, rows, :] = as_bf16(jnp.where(lo_lanes, r, zero))
            dst_hi[1, rows, :] = as_bf16(jnp.where(lo_lanes, zero, w))

    layout_block(0, kvm_ref[...])

    def layout_body(n, carry):
        layout_block(n + 1, kv_ref[pl.ds(pl.multiple_of(n * BLOCK, BLOCK), BLOCK), :])
        return carry

    lax.fori_loop(0, nbx, layout_body, 0, unroll=4)

    ones_lo = jnp.where(lax.broadcasted_iota(I32, (2 * BLOCK, LANES), 1) < HEAD_DIM,
                        1.0, 0.0).astype(BF16)
    ones_hi = (1.0 - ones_lo.astype(F32)).astype(BF16)
    top_rows = lax.broadcasted_iota(I32, (2 * BLOCK, 1), 0) < BLOCK
    lo_half = lax.broadcasted_iota(I32, (2 * BLOCK, LANES), 1) < HEAD_DIM
    sinks = [sinks_ref[h] * LOG2E for h in range(N_Q_HEADS)]

    def block(i, carry):
        q_rows = pl.ds(pl.multiple_of(i * BLOCK, BLOCK), BLOCK)
        win = pl.ds(pl.multiple_of(i * BLOCK, BLOCK), 2 * BLOCK)
        q = q_ref[q_rows, :]
        first = jnp.minimum(i, 1)
        outs = []
        for j in range(N_KV_HEADS):
            q2 = jnp.concatenate([q[:, (2 * j) * LANES:(2 * j + 1) * LANES],
                                  q[:, (2 * j + 1) * LANES:(2 * j + 2) * LANES]], axis=0)
            kc = jnp.concatenate([klo[j, win, :], khi[j, win, :]], axis=0)
            s = lax.dot_general(q2, kc, (((1,), (1,)), ((), ())), preferred_element_type=F32)
            s = s + bias_ref[first, j]
            ps, es = [], []
            for c in range(2):
                sink = jnp.where(top_rows, sinks[4 * j + c], sinks[4 * j + 2 + c])
                sc = s[:, c * 2 * BLOCK:(c + 1) * 2 * BLOCK]
                m = jnp.maximum(jnp.max(sc, axis=1, keepdims=True), sink)
                ps.append(jnp.exp2(sc - m).astype(BF16))
                es.append(jnp.exp2(sink - m))
            v_lo = jnp.concatenate([vlo[j, win, :], ones_lo], axis=1)
            v_hi = jnp.concatenate([vhi[j, win, :], ones_hi], axis=1)
            r = jnp.dot(jnp.concatenate(ps, axis=1), jnp.concatenate([v_lo, v_hi], axis=0),
                        preferred_element_type=F32)
            den = r[:, LANES:] + jnp.where(lo_half, es[0], es[1])
            o2 = r[:, :LANES] * (1.0 / den)
            outs += [o2[:BLOCK], o2[BLOCK:]]
        out = jnp.concatenate(outs, axis=1)
        ms = jnp.mean(out * out, axis=1, keepdims=True)
        o_ref[q_rows, :] = (out * lax.rsqrt(ms + EPS) * g_ref[...]).astype(o_ref.dtype)
        return carry

    lax.fori_loop(0, nbx, block, 0, unroll=8)


def _attention(sinks, q, kv, kvm, bias, g_attn, bsz, nbx):
    seq = nbx * BLOCK
    const2 = lambda b: (0, 0)
    kv_scratch = pltpu.VMEM((N_KV_HEADS, seq + BLOCK, LANES), BF16)
    return pl.pallas_call(
        _attn_kernel,
        grid=(bsz,),
        in_specs=[pl.BlockSpec(memory_space=pltpu.SMEM),
                  pl.BlockSpec((None, seq, ATTN_WIDTH), lambda b: (b, 0, 0)),
                  pl.BlockSpec((None, seq, 2 * KV_WIDTH), lambda b: (b, 0, 0)),
                  pl.BlockSpec((BLOCK, 2 * KV_WIDTH), const2),
                  pl.BlockSpec((2, N_KV_HEADS, 2 * BLOCK, 4 * BLOCK), lambda b: (0, 0, 0, 0)),
                  pl.BlockSpec((1, ATTN_WIDTH), const2)],
        out_specs=pl.BlockSpec((None, seq, ATTN_WIDTH), lambda b: (b, 0, 0)),
        out_shape=jax.ShapeDtypeStruct((bsz, seq, ATTN_WIDTH), BF16),
        scratch_shapes=[kv_scratch, kv_scratch, kv_scratch, kv_scratch],
        compiler_params=_cparams(1),
        name="attention",
    )(sinks, q, kv, kvm, bias, g_attn)


def _sigmoid(v):
    return 0.5 * jnp.tanh(0.5 * v) + 0.5


def _gelu_tanh(y):
    c = float(np.sqrt(2.0 / np.pi))
    half = 0.5 * y
    return half + half * jnp.tanh(y * (c + (c * 0.044715) * (y * y)))


LRU_CHUNK = 44
LRU_SEG = SUBLANES * LRU_CHUNK
LRU_SLABS = LRU_WIDTH // LANES


def _lru_kernel(xr_ref, yr_ref, xrm_ref, yrm_ref, cw_ref, cb_ref, wg_ref, ba_ref,
                bx_ref, lam_ref, g_ref, o_ref, x_st, y_st, o_st, s_st, xtail, hcar):
    seq = xr_ref.shape[0]
    n_seg = (seq + BLOCK) // LRU_SEG
    xtail[...] = jnp.zeros_like(xtail)
    hcar[...] = jnp.zeros_like(hcar)
    lam = lam_ref[...]
    softplus_neg = jnp.maximum(-lam, 0.0) + jnp.log(1.0 + jnp.exp(-jnp.abs(lam)))
    sub = lax.broadcasted_iota(jnp.int32, (SUBLANES, LANES), 0)

    def strided(j):
        return pl.ds(j, SUBLANES, stride=LRU_CHUNK)

    def piece(v, j):
        return v[j * SUBLANES:(j + 1) * SUBLANES, :]

    def segment(k, first):
        if first:
            head = LRU_SEG - BLOCK
            x_nat = jnp.concatenate([xrm_ref[...], xr_ref[0:head, :]], axis=0).astype(F32)
            y_nat = jnp.concatenate([yrm_ref[...], yr_ref[0:head, :]], axis=0).astype(F32)
        else:
            rows = pl.ds(pl.multiple_of(k * LRU_SEG - BLOCK, 2 * SUBLANES), LRU_SEG)
            x_nat = xr_ref[rows, :].astype(F32)
            y_nat = yr_ref[rows, :].astype(F32)
        for c in range(LRU_SLABS):
            x_st[c] = x_nat[:, c * LANES:(c + 1) * LANES]
            y_st[c] = y_nat[:, c * LANES:(c + 1) * LANES]
        first_row = k * LRU_SEG + LRU_CHUNK * sub
        sumsq = [jnp.zeros((SUBLANES, LANES), F32) for _ in range(LRU_CHUNK)]

        for c in range(LRU_SLABS):
            lanes = slice(c * LANES, (c + 1) * LANES)
            x = [x_st[c, strided(j), :] for j in range(LRU_CHUNK)]
            before = []
            for d in range(1, CONV_W):
                from_prev_chunk = pltpu.roll(x[LRU_CHUNK - d], 1, axis=0)
                before.append(jnp.where(sub == 0, xtail[d - 1:d, lanes], from_prev_chunk))
            for d in range(1, CONV_W):
                xtail[d - 1:d, lanes] = x[LRU_CHUNK - d][SUBLANES - 1:SUBLANES, :]

            def x_at(j):
                return x[j] if j >= 0 else before[-j - 1]

            taps = [cw_ref[t:t + 1, lanes] for t in range(CONV_W)]
            bias = cb_ref[:, lanes]
            xc = jnp.concatenate(
                [bias + sum(taps[t] * x_at(j - (CONV_W - 1) + t) for t in range(CONV_W))
                 for j in range(LRU_CHUNK)], axis=0)
            xcb = xc.astype(BF16)
            both = jnp.dot(xcb, wg_ref[c], preferred_element_type=F32)
            tr = jnp.tanh(both[:, :LANES] + 0.5 * ba_ref[:, lanes])
            ti = jnp.tanh(both[:, LANES:] + 0.5 * bx_ref[:, lanes])
            log_a_half = (-0.5 * LRU_C) * softplus_neg[:, lanes]
            a = jnp.exp(log_a_half * tr + log_a_half)
            half_xc = 0.5 * xc
            gated_x = half_xc * ti + half_xc
            z = 1.0 - a * a
            u = jnp.where(z > 0.0, z * lax.rsqrt(z), 0.0) * gated_x

            h = jnp.zeros((SUBLANES, LANES), F32)
            p = jnp.ones((SUBLANES, LANES), F32)
            hs, ps = [], []
            for j in range(LRU_CHUNK):
                aj = piece(a, j)
                uj = piece(u, j)
                if first:
                    uj = jnp.where(first_row + j >= PAD_FRONT, uj, 0.0)
                h = aj * h + uj
                p = aj * p
                hs.append(h)
                ps.append(p)
            entry = [hcar[:, lanes]]
            for s in range(SUBLANES):
                entry.append(h[s:s + 1, :] + p[s:s + 1, :] * entry[s])
            hcar[:, lanes] = entry[SUBLANES]
            entry_rows = jnp.concatenate(entry[:SUBLANES], axis=0)

            for j in range(LRU_CHUNK):
                state = hs[j] + ps[j] * entry_rows
                out = state * _gelu_tanh(y_st[c, strided(j), :])
                sumsq[j] = sumsq[j] + out * out
                o_st[c, strided(j), :] = out

        for j in range(LRU_CHUNK):
            ms = jnp.sum(sumsq[j], axis=1, keepdims=True) * (1.0 / LRU_WIDTH)
            s_st[strided(j), :] = jnp.broadcast_to(lax.rsqrt(ms + EPS), (SUBLANES, LANES))
        scale = s_st[...]
        for c in range(LRU_SLABS):
            lanes = slice(c * LANES, (c + 1) * LANES)
            normed = (o_st[c] * scale * g_ref[:, lanes]).astype(o_ref.dtype)
            if first:
                o_ref[0:LRU_SEG - BLOCK, lanes] = normed[BLOCK:, :]
            else:
                o_ref[rows, lanes] = normed

    assert BLOCK <= LRU_SEG
    segment(0, True)

    def later_segment(k, carry):
        segment(k, False)
        return carry

    lax.fori_loop(1, n_seg, later_segment, 0)


def _rglru(xr, yr, xrm, yrm, cw, cb, w_gates, ba, bx, lam, g_lru, bsz, nbx):
    seq = nbx * BLOCK
    assert (seq + BLOCK) % LRU_SEG == 0
    main = pl.BlockSpec((None, seq, LRU_WIDTH), lambda b: (b, 0, 0))
    const2 = lambda b: (0, 0)
    row_spec = pl.BlockSpec((1, LRU_WIDTH), const2)
    gate_spec = pl.BlockSpec((LRU_SLABS, LANES, 2 * LANES), lambda b: (0, 0, 0))
    slabs = pltpu.VMEM((LRU_SLABS, LRU_SEG, LANES), F32)
    return pl.pallas_call(
        _lru_kernel,
        grid=(bsz,),
        in_specs=[main, main,
                  pl.BlockSpec((BLOCK, LRU_WIDTH), const2),
                  pl.BlockSpec((BLOCK, LRU_WIDTH), const2),
                  pl.BlockSpec((CONV_W, LRU_WIDTH), const2),
                  row_spec, gate_spec,
                  row_spec, row_spec, row_spec, row_spec],
        out_specs=main,
        out_shape=jax.ShapeDtypeStruct((bsz, seq, LRU_WIDTH), BF16),
        scratch_shapes=[slabs, slabs, slabs,
                        pltpu.VMEM((LRU_SEG, LANES), F32),
                        pltpu.VMEM((SUBLANES, LRU_WIDTH), F32),
                        pltpu.VMEM((1, LRU_WIDTH), F32)],
        compiler_params=_cparams(1),
        name="rglru",
    )(xr, yr, xrm, yrm, cw, cb, w_gates, ba, bx, lam, g_lru)


def _pack_rows(v):
    bits = lax.bitcast_convert_type(v.astype(BF16).astype(F32), U32)
    return (bits[:, :PACKED] >> 16) | (bits[:, PACKED:] & jnp.uint32(0xFFFF0000))


def _unpack_rows(w):
    lo = lax.bitcast_convert_type(w << 16, F32)
    hi = lax.bitcast_convert_type(w & jnp.uint32(0xFFFF0000), F32)
    return lo, hi


def _layer_norm(z, g, b):
    mu = jnp.mean(z, axis=1, keepdims=True)
    zc = z - mu
    var = jnp.mean(zc * zc, axis=1, keepdims=True)
    return zc * lax.rsqrt(var + EPS) * g + b


def _out_proj_kernel(a_ref, l_ref, x_ref, w_ref, g_ref, b_ref, wrt_ref, brt_ref,
                     h_ref, hp_ref, info_ref, cnt_ref, w_b, wrt_b):
    @pl.when(pl.program_id(0) == 0)
    def _():
        cnt_ref[...] = jnp.zeros_like(cnt_ref)
        w_b[...] = w_ref[...].astype(BF16)
        hi = wrt_ref[...].astype(BF16)
        wrt_b[0:ROUTER_ROWS, :] = hi
        wrt_b[ROUTER_ROWS:, :] = (wrt_ref[...] - hi.astype(F32)).astype(BF16)

    mix = jnp.dot(jnp.concatenate([a_ref[...], l_ref[...]], axis=1), w_b[...],
                  preferred_element_type=F32)
    h = _layer_norm(ALPHA * x_ref[...] + mix, g_ref[...], b_ref[...])
    h_ref[...] = h
    hp_ref[...] = _pack_rows(h)

    h_hi = h.astype(BF16)
    h_lo = (h - h_hi.astype(F32)).astype(BF16)
    nt = (((1,), (1,)), ((), ()))
    both = lax.dot_general(wrt_b[...], h_hi, nt, preferred_element_type=F32)
    lg = (both[:ROUTER_ROWS] + both[ROUTER_ROWS:]
          + lax.dot_general(wrt_b[0:ROUTER_ROWS, :], h_lo, nt, preferred_element_type=F32)
          ) + brt_ref[...]
    tile_shape = (SUBLANES, h.shape[0])
    sub = lax.broadcasted_iota(I32, tile_shape, 0)
    ninf = -jnp.inf
    t0 = lg[0:SUBLANES]
    gl = jnp.where(sub < N_GROUPS, t0, ninf)
    gmax = jnp.max(gl, axis=0, keepdims=True)
    g_idx = jnp.min(jnp.where(gl == gmax, sub, SUBLANES), axis=0, keepdims=True)
    g_w = 1.0 / jnp.sum(jnp.where(sub < N_GROUPS, jnp.exp(t0 - gmax), 0.0),
                        axis=0, keepdims=True)
    el = lg[SUBLANES:2 * SUBLANES]
    for g in range(1, N_GROUPS):
        el = jnp.where(g_idx == g, lg[(g + 1) * SUBLANES:(g + 2) * SUBLANES], el)
    v1 = jnp.max(el, axis=0, keepdims=True)
    i1 = jnp.min(jnp.where(el == v1, sub, SUBLANES), axis=0, keepdims=True)
    el2 = jnp.where(sub == i1, ninf, el)
    v2 = jnp.max(el2, axis=0, keepdims=True)
    i2 = jnp.min(jnp.where(el2 == v2, sub, SUBLANES), axis=0, keepdims=True)
    t = jnp.exp(v2 - v1)
    w1 = 1.0 / (1.0 + t)
    w2 = t * w1
    e_base = g_idx * EXPERTS_PER_GROUP
    e1 = e_base + i1
    e2 = e_base + i2
    info_ref[...] = jnp.where(sub == 0, e1.astype(F32),
                              jnp.where(sub == 1, e2.astype(F32),
                                        jnp.where(sub == 2, g_w * w1,
                                                  jnp.where(sub == 3, g_w * w2, 0.0))))
    expert = lax.broadcasted_iota(I32, (N_EXPERTS, h.shape[0]), 0)
    chosen = (expert == e1).astype(F32) + (expert == e2).astype(F32)
    cnt_ref[...] += jnp.sum(chosen, axis=1, keepdims=True)


ROUTER_ROWS = -(-(N_GROUPS + 1) * SUBLANES // (2 * SUBLANES)) * (2 * SUBLANES)


def _out_proj(attn_n, lru_n, x2d, w_out, ln_g, ln_b, w_rt, b_rt, rows):
    n = x2d.shape[0]
    const = lambda i: (0, 0)
    tile = lambda w: pl.BlockSpec((rows, w), lambda i: (i, 0))
    return pl.pallas_call(
        _out_proj_kernel,
        grid=(n // rows,),
        in_specs=[tile(ATTN_WIDTH), tile(LRU_WIDTH), tile(D_MODEL),
                  pl.BlockSpec((D_MODEL, D_MODEL), const),
                  pl.BlockSpec((1, D_MODEL), const),
                  pl.BlockSpec((1, D_MODEL), const),
                  pl.BlockSpec((ROUTER_ROWS, D_MODEL), const),
                  pl.BlockSpec((ROUTER_ROWS, 1), const)],
        out_specs=[tile(D_MODEL), tile(PACKED),
                   pl.BlockSpec((SUBLANES, rows), lambda i: (0, i)),
                   pl.BlockSpec((N_EXPERTS, LANES), const)],
        out_shape=[jax.ShapeDtypeStruct((n, D_MODEL), F32),
                   jax.ShapeDtypeStruct((n, PACKED), U32),
                   jax.ShapeDtypeStruct((SUBLANES, n), F32),
                   jax.ShapeDtypeStruct((N_EXPERTS, LANES), F32)],
        scratch_shapes=[pltpu.VMEM((D_MODEL, D_MODEL), BF16),
                        pltpu.VMEM((2 * ROUTER_ROWS, D_MODEL), BF16)],
        compiler_params=_cparams(1),
        name="out_proj",
    )(attn_n, lru_n, x2d, w_out, ln_g, ln_b, w_rt, b_rt)


def _route_kernel(info_ref, cnt_ref, tri_ref, dest_ref, carry, pstart):
    t = pl.program_id(0)
    info = info_ref[...]
    shape = (N_EXPERTS, info.shape[1])
    expert = lax.broadcasted_iota(I32, shape, 0)
    oh1 = (expert == info[0:1, :].astype(I32)).astype(F32)
    oh2 = (expert == info[1:2, :].astype(I32)).astype(F32)
    both = oh1 + oh2

    @pl.when(t == 0)
    def _():
        c = cnt_ref[...].astype(I32)
        padded = ((c + (MOE_BLOCK - 1)) // MOE_BLOCK) * MOE_BLOCK
        e = lax.broadcasted_iota(I32, (N_EXPERTS, LANES), 0)
        scan = padded
        for d in (1, 2, 4, 8, 16):
            scan = scan + jnp.where(e >= d, pltpu.roll(scan, d, axis=0), 0)
        pstart[...] = (scan - padded)[:, 0:1].astype(F32)
        carry[...] = jnp.zeros_like(carry)

    before = jnp.dot(both.astype(BF16), tri_ref[...], preferred_element_type=F32)
    row_of = before + (carry[...] + pstart[...])
    r1 = jnp.sum(oh1 * row_of, axis=0, keepdims=True)
    r2 = jnp.sum(oh2 * row_of, axis=0, keepdims=True)
    dest_ref[...] = jnp.concatenate([r1, r2], axis=0).astype(I32)
    carry[...] += jnp.sum(both, axis=1, keepdims=True)


def _route(info_t, cnt, cols):
    n = info_t.shape[1]
    tri = jnp.asarray(np.triu(np.ones((cols, cols), np.float32), 1), BF16)
    return pl.pallas_call(
        _route_kernel,
        grid=(n // cols,),
        in_specs=[pl.BlockSpec((SUBLANES, cols), lambda t: (0, t)),
                  pl.BlockSpec((N_EXPERTS, LANES), lambda t: (0, 0)),
                  pl.BlockSpec((cols, cols), lambda t: (0, 0))],
        out_specs=pl.BlockSpec((TOP_K, cols), lambda t: (0, t)),
        out_shape=jax.ShapeDtypeStruct((TOP_K, n), I32),
        scratch_shapes=[pltpu.VMEM((N_EXPERTS, 1), F32), pltpu.VMEM((N_EXPERTS, 1), F32)],
        compiler_params=_cparams(1),
        name="route",
    )(info_t, cnt, tri)


def _sc_mesh():
    return plsc.VectorSubcoreMesh(core_axis_name="core", subcore_axis_name="subcore")


def _sc_worker_id():
    return lax.axis_index("subcore") * SC_CORES + lax.axis_index("core")


def _sc_scatter_rows(rows, dest, cap):
    n, width = rows.shape
    per_worker = n // SC_WORKERS
    n_win = per_worker // SC_WINDOW

    def body(x_hbm, d_hbm, o_hbm, i0_v, i1_v, rows_v, isem, rsem, sem0, sem1):
        wid = _sc_worker_id()

        def indices(j):
            win = pl.ds(wid * per_worker + j * SC_WINDOW, SC_WINDOW)
            return (pltpu.make_async_copy(d_hbm.at[0, win], i0_v.at[j], isem.at[0, j]),
                    pltpu.make_async_copy(d_hbm.at[1, win], i1_v.at[j], isem.at[1, j]))

        def read(j):
            src = x_hbm.at[pl.ds(wid * per_worker + j * SC_WINDOW, SC_WINDOW)]
            return pltpu.make_async_copy(src, rows_v.at[j % 2], rsem.at[j % 2])

        def scatters(j):
            return (pltpu.make_async_copy(rows_v.at[j % 2], o_hbm.at[i0_v.at[j]], sem0.at[j % 2]),
                    pltpu.make_async_copy(rows_v.at[j % 2], o_hbm.at[i1_v.at[j]], sem1.at[j % 2]))

        read(0).start()
        for j in range(n_win):
            for cp in indices(j):
                cp.start()
        for j in range(n_win):
            for cp in indices(j):
                cp.wait()
        for j in range(n_win):
            if j + 1 < n_win:
                if j >= 1:
                    for cp in scatters(j - 1):
                        cp.wait()
                read(j + 1).start()
            read(j).wait()
            for cp in scatters(j):
                cp.start()
        for j in range(max(n_win - 2, 0), n_win):
            for cp in scatters(j):
                cp.wait()

    return pl.kernel(
        body,
        out_type=jax.ShapeDtypeStruct((cap, width), rows.dtype),
        mesh=_sc_mesh(),
        scratch_types=[pltpu.VMEM((n_win, SC_WINDOW), I32), pltpu.VMEM((n_win, SC_WINDOW), I32),
                       pltpu.VMEM((2, SC_WINDOW, width), rows.dtype),
                       pltpu.SemaphoreType.DMA((TOP_K, n_win)),
                       pltpu.SemaphoreType.DMA((2,)), pltpu.SemaphoreType.DMA((2,)),
                       pltpu.SemaphoreType.DMA((2,))],
        name="dispatch",
    )(rows, dest)


def _sc_gather_rows(table, dest, start, length):
    width = table.shape[1]
    per_worker = TOP_K * length // SC_WORKERS
    n_win = per_worker // SC_WINDOW
    workers_per_choice = SC_WORKERS // TOP_K

    def body(y_hbm, d_hbm, o_hbm, i_v, rows_v, isem, gsem, wsem):
        wid = _sc_worker_id()
        choice = wid // workers_per_choice
        first = start + (wid % workers_per_choice) * per_worker

        def indices(j):
            win = pl.ds(first + j * SC_WINDOW, SC_WINDOW)
            return pltpu.make_async_copy(d_hbm.at[choice, win], i_v.at[j], isem.at[j])

        for j in range(n_win):
            indices(j).start()
        for j in range(n_win):
            indices(j).wait()

        def gather(j):
            return pltpu.make_async_copy(y_hbm.at[i_v.at[j]], rows_v.at[j % 2], gsem.at[j % 2])

        def write(j):
            dst = o_hbm.at[pl.ds(wid * per_worker + j * SC_WINDOW, SC_WINDOW)]
            return pltpu.make_async_copy(rows_v.at[j % 2], dst, wsem.at[j % 2])

        gather(0).start()
        for j in range(n_win):
            if j + 1 < n_win:
                if j >= 1:
                    write(j - 1).wait()
                gather(j + 1).start()
            gather(j).wait()
            write(j).start()
        for j in range(max(n_win - 2, 0), n_win):
            write(j).wait()

    return pl.kernel(
        body,
        out_type=jax.ShapeDtypeStruct((SC_WORKERS * per_worker, width), table.dtype),
        mesh=_sc_mesh(),
        scratch_types=[pltpu.VMEM((n_win, SC_WINDOW), I32),
                       pltpu.VMEM((2, SC_WINDOW, width), table.dtype),
                       pltpu.SemaphoreType.DMA((n_win,)),
                       pltpu.SemaphoreType.DMA((2,)), pltpu.SemaphoreType.DMA((2,))],
        name="collect",
    )(table, dest)


def _expert_kernel(bstart_ref, nblk_ref, nused_ref, xs_hbm, wg_ref, wu_ref, wd_ref, yb_hbm,
                   xbuf, ybuf, zbuf, xsem, ysem, zsem, wg_b, wu_b, wd_b):
    e = pl.program_id(0)
    nused = nused_ref[0]
    n_blocks = yb_hbm.shape[0] // MOE_BLOCK

    def rows(b):
        return pl.ds(pl.multiple_of(b * MOE_BLOCK, MOE_BLOCK), MOE_BLOCK)

    def x_copy(b):
        slot = b % X_RING
        return pltpu.make_async_copy(xs_hbm.at[rows(b)], xbuf.at[slot], xsem.at[slot])

    def y_copy(b):
        slot = b % Y_RING
        return pltpu.make_async_copy(ybuf.at[slot], yb_hbm.at[rows(b)], ysem.at[slot])

    @pl.when(e == 0)
    def _():
        for b in range(X_AHEAD):
            @pl.when(b < nused)
            def _():
                x_copy(b).start()

    wg_b[...] = wg_ref[...].astype(BF16)
    wu_b[...] = wu_ref[...].astype(BF16)
    wd_b[...] = wd_ref[...].astype(BF16)

    def run_blocks(b, count):
        for k in range(count):
            x_copy(b + k).wait()
        for k in range(count):
            nxt = b + X_AHEAD + k

            @pl.when(nxt < nused)
            def _():
                x_copy(nxt).start()

            @pl.when(b + k >= Y_RING)
            def _():
                y_copy(b + k - Y_RING).wait()

        words = jnp.concatenate([xbuf[(b + k) % X_RING] for k in range(count)], axis=0)
        lo, hi = _unpack_rows(words)
        x = jnp.concatenate([lo.astype(BF16), hi.astype(BF16)], axis=1)
        g = jnp.dot(x, wg_b[...], preferred_element_type=F32)
        u = jnp.dot(x, wu_b[...], preferred_element_type=F32)
        mid = (g * _sigmoid(g) * u).astype(BF16)
        y = _pack_rows(jnp.dot(mid, wd_b[...], preferred_element_type=F32))
        for k in range(count):
            ybuf[(b + k) % Y_RING] = y[k * MOE_BLOCK:(k + 1) * MOE_BLOCK]
            y_copy(b + k).start()

    b0 = bstart_ref[e]
    nb = nblk_ref[e]

    def group(i, carry):
        run_blocks(b0 + X_GROUP * i, X_GROUP)
        return carry

    lax.fori_loop(0, nb // X_GROUP, group, 0)
    done = nb - nb % X_GROUP
    size = X_GROUP // 2
    while size >= 1:
        @pl.when((nb // size) % 2 == 1)
        def _(size=size, done=done):
            run_blocks(b0 + done, size)

        done = done + (nb // size) % 2 * size
        size //= 2

    @pl.when(e == pl.num_programs(0) - 1)
    def _():
        for back in range(Y_RING, 0, -1):
            @pl.when(nused >= back)
            def _():
                y_copy(nused - back).wait()

        zbuf[...] = jnp.zeros_like(zbuf)

        def z_copy(b):
            return pltpu.make_async_copy(zbuf, yb_hbm.at[rows(b)], zsem.at[0])

        def z_start(b, carry):
            z_copy(b).start()
            return carry

        def z_wait(b, carry):
            z_copy(b).wait()
            return carry

        lax.fori_loop(nused, n_blocks, z_start, 0)
        lax.fori_loop(nused, n_blocks, z_wait, 0)


def _experts(bstart, nblk, nused, xs, w_gate, w_up, w_down):
    cap = xs.shape[0]
    w_idx = lambda e, bs, nb, nu: (e, 0, 0)
    grid_spec = pltpu.PrefetchScalarGridSpec(
        num_scalar_prefetch=3,
        grid=(N_EXPERTS,),
        in_specs=[pl.BlockSpec(memory_space=pl.ANY),
                  pl.BlockSpec((None, D_MODEL, D_FF), w_idx),
                  pl.BlockSpec((None, D_MODEL, D_FF), w_idx),
                  pl.BlockSpec((None, D_FF, D_MODEL), w_idx)],
        out_specs=pl.BlockSpec(memory_space=pl.ANY),
        scratch_shapes=[pltpu.VMEM((X_RING, MOE_BLOCK, PACKED), U32),
                        pltpu.VMEM((Y_RING, MOE_BLOCK, PACKED), U32),
                        pltpu.VMEM((MOE_BLOCK, PACKED), U32),
                        pltpu.SemaphoreType.DMA((X_RING,)),
                        pltpu.SemaphoreType.DMA((Y_RING,)),
                        pltpu.SemaphoreType.DMA((1,)),
                        pltpu.VMEM((D_MODEL, D_FF), BF16),
                        pltpu.VMEM((D_MODEL, D_FF), BF16),
                        pltpu.VMEM((D_FF, D_MODEL), BF16)])
    return pl.pallas_call(
        _expert_kernel,
        grid_spec=grid_spec,
        out_shape=jax.ShapeDtypeStruct((cap, PACKED), U32),
        compiler_params=_cparams(1),
        name="experts",
    )(bstart, nblk, nused, xs, w_gate, w_up, w_down)


def _combine_kernel(y0_ref, y1_ref, h_ref, info_ref, g_ref, b_ref, o_ref):
    info = info_ref[...].T
    g0 = info[:, 2:3]
    g1 = info[:, 3:4]
    lo0, hi0 = _unpack_rows(y0_ref[...])
    lo1, hi1 = _unpack_rows(y1_ref[...])
    y = jnp.concatenate([g0 * lo0 + g1 * lo1, g0 * hi0 + g1 * hi1], axis=1)
    o_ref[...] = _layer_norm(ALPHA * h_ref[...] + y, g_ref[...], b_ref[...])


def _combine(ys, h, info, ln_g, ln_b, rows, part, n_parts):
    n = h.shape[0]
    steps = n // n_parts // rows
    off = part * steps
    const = lambda i: (0, 0)
    return pl.pallas_call(
        _combine_kernel,
        grid=(steps,),
        in_specs=[pl.BlockSpec((rows, PACKED), lambda i: (i, 0)),
                  pl.BlockSpec((rows, PACKED), lambda i: (i + steps, 0)),
                  pl.BlockSpec((rows, D_MODEL), lambda i: (i + off, 0)),
                  pl.BlockSpec((SUBLANES, rows), lambda i: (0, i + off)),
                  pl.BlockSpec((1, D_MODEL), const),
                  pl.BlockSpec((1, D_MODEL), const)],
        out_specs=pl.BlockSpec((rows, D_MODEL), lambda i: (i + off, 0)),
        out_shape=jax.ShapeDtypeStruct((n, D_MODEL), F32),
        input_output_aliases={2: 0},
        compiler_params=_cparams(1),
        name="combine",
    )(ys, ys, h, info, ln_g, ln_b)


def _alibi_bias():
    qi = np.arange(BLOCK)[:, None]
    kj = np.arange(2 * BLOCK)[None, :]
    dist = qi - kj + BLOCK
    band = (dist >= 0) & (dist < BLOCK)
    slopes = np.exp2(-8.0 * np.arange(1, N_Q_HEADS + 1, dtype=np.float32) / N_Q_HEADS)
    bias = np.where(band[None], -slopes[:, None, None] * dist[None].astype(np.float32), NEG)
    bias = bias * LOG2E
    first = np.where((kj >= PAD_FRONT)[None], bias, NEG)
    out = np.empty((2, N_KV_HEADS, 2 * BLOCK, 4 * BLOCK), np.float32)
    for v, per_head in enumerate((first, bias)):
        for j in range(N_KV_HEADS):
            out[v, j] = np.block([[per_head[4 * j], per_head[4 * j + 1]],
                                  [per_head[4 * j + 2], per_head[4 * j + 3]]])
    return jnp.asarray(out, F32)


def _slab_gates(wa, wx):
    nb, c, _ = wa.shape
    per = LANES // c
    w = (0.5 * jnp.stack([wa, wx])).reshape(2, nb // per, per, c, c)
    eye = jnp.eye(per, dtype=w.dtype)
    slabs = jnp.einsum('gspcd,pq->spcgqd', w, eye)
    return slabs.reshape(nb // per, LANES, 2 * LANES).astype(BF16)


def kernel(x, meta_tokens, w_in, conv_w, conv_b, lru_wa, lru_ba, lru_wx, lru_bx, lru_lambda,
           attn_sinks, g_attn, g_lru, w_out, ln1_g, ln1_b, w_group, b_group, w_router,
           b_router, w_gate, w_up, w_down, ln2_g, ln2_b):
    bsz, seq, d = x.shape
    nbx = seq // BLOCK
    n_tok = bsz * seq
    x2d = x.reshape(n_tok, d)
    row = lambda v: v.reshape(1, -1).astype(F32)

    q_scale = jnp.concatenate([jnp.full((ATTN_WIDTH,), LOG2E * HEAD_DIM ** -0.5, F32),
                               jnp.ones((IN_COLS - ATTN_WIDTH,), F32)])
    w_in_b = (w_in[0] * q_scale).astype(BF16)
    meta_blk = jnp.concatenate([jnp.zeros((PAD_FRONT, d), F32), meta_tokens.astype(F32)], axis=0)
    q, kv, xr, yr = _in_proj(x2d, w_in_b, PROJ_ROWS)
    qm, kvm, xrm, yrm = _in_proj(meta_blk, w_in_b, BLOCK)
    shp = lambda a: a.reshape(bsz, seq, a.shape[-1])

    attn_n = _attention(attn_sinks[0].astype(F32), shp(q), shp(kv), kvm, _alibi_bias(),
                        row(g_attn[0]), bsz, nbx)
    lru_n = _rglru(shp(xr), shp(yr), xrm, yrm, conv_w[0].astype(F32), row(conv_b[0]),
                   _slab_gates(lru_wa[0].astype(F32), lru_wx[0].astype(F32)),
                   row(lru_ba[0]), row(lru_bx[0]), row(lru_lambda[0]),
                   row(g_lru[0]), bsz, nbx)

    gpad = SUBLANES - N_GROUPS
    tail = ROUTER_ROWS - SUBLANES - N_EXPERTS
    w_rt = jnp.concatenate(
        [w_group[0].T, jnp.zeros((gpad, d), F32),
         jnp.transpose(w_router[0], (0, 2, 1)).reshape(N_EXPERTS, d),
         jnp.zeros((tail, d), F32)], axis=0).astype(F32)
    b_rt = jnp.concatenate([b_group[0], jnp.zeros((gpad,), F32), b_router[0].reshape(-1),
                            jnp.zeros((tail,), F32)]).astype(F32).reshape(ROUTER_ROWS, 1)
    h1, hp, info, cnt = _out_proj(
        attn_n.reshape(n_tok, ATTN_WIDTH), lru_n.reshape(n_tok, LRU_WIDTH), x2d,
        w_out[0].astype(F32), row(ln1_g[0]), row(ln1_b[0]), w_rt, b_rt, OUT_PROJ_ROWS)

    dest = _route(info, cnt, ROUTE_ROWS)
    n_slots = n_tok * TOP_K
    n_blocks = n_slots // MOE_BLOCK + N_EXPERTS
    cap = n_blocks * MOE_BLOCK
    nblk = (cnt[:, 0].astype(I32) + MOE_BLOCK - 1) // MOE_BLOCK
    bends = jnp.cumsum(nblk)
    bstart = (bends - nblk).astype(I32)
    nused = bends[-1:].astype(I32)

    xs = _sc_scatter_rows(hp, dest, cap)
    yb = _experts(bstart, nblk.astype(I32), nused, xs, w_gate[0], w_up[0], w_down[0])
    out = h1
    part_len = n_tok // COMBINE_PARTS
    for part in range(COMBINE_PARTS):
        ys = _sc_gather_rows(yb, dest, part * part_len, part_len)
        out = _combine(ys, out, info, row(ln2_g[0]), row(ln2_b[0]), COMBINE_ROWS,
                       part, COMBINE_PARTS)
    return out.reshape(bsz, seq, d)
```

```python
import jax
import jax.numpy as jnp
import numpy as np
from jax import lax
from jax.experimental import pallas as pl
from jax.experimental.pallas import tpu as pltpu
from jax.experimental.pallas import tpu_sc as plsc

F32 = jnp.float32
BF16 = jnp.bfloat16
U32 = jnp.uint32
I32 = jnp.int32

D_MODEL = 1024
N_META = 16
BLOCK = 128
PAD_FRONT = BLOCK - N_META
HEAD_DIM = 64
ATTN_WIDTH = 512
LRU_WIDTH = 512
N_Q_HEADS = 8
N_KV_HEADS = 2
KV_WIDTH = N_KV_HEADS * HEAD_DIM
LRU_BLOCKS = 8
CONV_W = 4
LRU_C = 8.0
IN_COLS = ATTN_WIDTH + 2 * KV_WIDTH + 2 * LRU_WIDTH
N_GROUPS = 4
EXPERTS_PER_GROUP = 8
N_EXPERTS = N_GROUPS * EXPERTS_PER_GROUP
TOP_K = 2
D_FF = 512
MOE_BLOCK = 256
ALPHA = 2.0 ** 0.25
EPS = 1e-5
NEG = -1e30
LOG2E = float(np.log2(np.e))
LANES = 128
SUBLANES = 8
PACKED = D_MODEL // 2

PROJ_ROWS = 1024
OUT_PROJ_ROWS = 1024
ROUTE_ROWS = 512
COMBINE_ROWS = 1024
COMBINE_PARTS = 4
X_GROUP = 4
X_AHEAD = 4
X_RING = X_AHEAD + X_GROUP
Y_RING = 2 * X_GROUP
VMEM_LIMIT = 48 * 1024 * 1024

SC_CORES = 2
SC_SUBCORES = 16
SC_WORKERS = SC_CORES * SC_SUBCORES
SC_WINDOW = 64


def _cparams(n_axes):
    return pltpu.CompilerParams(
        dimension_semantics=("arbitrary",) * n_axes, vmem_limit_bytes=VMEM_LIMIT)


def _in_proj_kernel(x_ref, w_ref, q_ref, kv_ref, xr_ref, yr_ref):
    proj = jnp.dot(x_ref[...].astype(BF16), w_ref[...], preferred_element_type=F32)
    front = q_ref.shape[0] - x_ref.shape[0]
    o = 0
    for ref, width in ((q_ref, ATTN_WIDTH), (kv_ref, 2 * KV_WIDTH),
                       (xr_ref, LRU_WIDTH), (yr_ref, LRU_WIDTH)):
        vals = proj[:, o:o + width].astype(ref.dtype)
        if front:
            vals = jnp.concatenate([jnp.zeros((front, width), ref.dtype), vals], axis=0)
        ref[...] = vals
        o += width


def _in_proj(x2d, w_bf16, rows, front=0):
    steps = x2d.shape[0] // rows
    widths = (ATTN_WIDTH, 2 * KV_WIDTH, LRU_WIDTH, LRU_WIDTH)
    return pl.pallas_call(
        _in_proj_kernel,
        grid=(steps,),
        in_specs=[pl.BlockSpec((rows, D_MODEL), lambda i: (i, 0)),
                  pl.BlockSpec((D_MODEL, IN_COLS), lambda i: (0, 0))],
        out_specs=[pl.BlockSpec((front + rows, w), lambda i: (i, 0)) for w in widths],
        out_shape=[jax.ShapeDtypeStruct((steps * (front + rows), w), BF16) for w in widths],
        compiler_params=_cparams(1),
        name="in_proj",
    )(x2d, w_bf16)


def _attn_kernel(sinks_ref, q_ref, kv_ref, kvm_ref, bias_ref, g_ref, o_ref,
                 klo, khi, vlo, vhi):
    nbx = q_ref.shape[0] // BLOCK
    lo_lanes = lax.broadcasted_iota(I32, (BLOCK // 2, LANES), 1) < HEAD_DIM

    def layout_block(n, blk):
        rows = pl.ds(pl.multiple_of(n * BLOCK, BLOCK), BLOCK)
        as_bf16 = lambda words: pltpu.bitcast(words, BF16)
        for src, dst_lo, dst_hi in ((blk[:, :KV_WIDTH], klo, khi), (blk[:, KV_WIDTH:], vlo, vhi)):
            w = pltpu.bitcast(src, U32)
            r = pltpu.roll(w, HEAD_DIM, axis=1)
            zero = jnp.zeros_like(w)
            dst_lo[0, rows, :] = as_bf16(jnp.where(lo_lanes, w, zero))
            dst_hi[0, rows, :] = as_bf16(jnp.where(lo_lanes, zero, r))
            dst_lo[1, rows, :] = as_bf16(jnp.where(lo_lanes, r, zero))
            dst_hi[1, rows, :] = as_bf16(jnp.where(lo_lanes, zero, w))

    layout_block(0, kvm_ref[...])

    def layout_body(n, carry):
        layout_block(n + 1, kv_ref[pl.ds(pl.multiple_of(n * BLOCK, BLOCK), BLOCK), :])
        return carry

    lax.fori_loop(0, nbx, layout_body, 0, unroll=4)

    ones_lo = jnp.where(lax.broadcasted_iota(I32, (2 * BLOCK, LANES), 1) < HEAD_DIM,
                        1.0, 0.0).astype(BF16)
    ones_hi = (1.0 - ones_lo.astype(F32)).astype(BF16)
    top_rows = lax.broadcasted_iota(I32, (2 * BLOCK, 1), 0) < BLOCK
    lo_half = lax.broadcasted_iota(I32, (2 * BLOCK, LANES), 1) < HEAD_DIM
    sinks = [sinks_ref[h] * LOG2E for h in range(N_Q_HEADS)]

    def block(i, carry):
        q_rows = pl.ds(pl.multiple_of(i * BLOCK, BLOCK), BLOCK)
        win = pl.ds(pl.multiple_of(i * BLOCK, BLOCK), 2 * BLOCK)
        q = q_ref[q_rows, :]
        first = jnp.minimum(i, 1)
        outs = []
        for j in range(N_KV_HEADS):
            q2 = jnp.concatenate([q[:, (2 * j) * LANES:(2 * j + 1) * LANES],
                                  q[:, (2 * j + 1) * LANES:(2 * j + 2) * LANES]], axis=0)
            kc = jnp.concatenate([klo[j, win, :], khi[j, win, :]], axis=0)
            s = lax.dot_general(q2, kc, (((1,), (1,)), ((), ())), preferred_element_type=F32)
            s = s + bias_ref[first, j]
            ps, es = [], []
            for c in range(2):
                sink = jnp.where(top_rows, sinks[4 * j + c], sinks[4 * j + 2 + c])
                sc = s[:, c * 2 * BLOCK:(c + 1) * 2 * BLOCK]
                m = jnp.maximum(jnp.max(sc, axis=1, keepdims=True), sink)
                ps.append(jnp.exp2(sc - m).astype(BF16))
                es.append(jnp.exp2(sink - m))
            v_lo = jnp.concatenate([vlo[j, win, :], ones_lo], axis=1)
            v_hi = jnp.concatenate([vhi[j, win, :], ones_hi], axis=1)
            r = jnp.dot(jnp.concatenate(ps, axis=1), jnp.concatenate([v_lo, v_hi], axis=0),
                        preferred_element_type=F32)
            den = r[:, LANES:] + jnp.where(lo_half, es[0], es[1])
            o2 = r[:, :LANES] * (1.0 / den)
            outs += [o2[:BLOCK], o2[BLOCK:]]
        out = jnp.concatenate(outs, axis=1)
        ms = jnp.mean(out * out, axis=1, keepdims=True)
        o_ref[q_rows, :] = (out * lax.rsqrt(ms + EPS) * g_ref[...]).astype(o_ref.dtype)
        return carry

    lax.fori_loop(0, nbx, block, 0, unroll=8)


def _attention(sinks, q, kv, kvm, bias, g_attn, bsz, nbx):
    seq = nbx * BLOCK
    const2 = lambda b: (0, 0)
    kv_scratch = pltpu.VMEM((N_KV_HEADS, seq + BLOCK, LANES), BF16)
    return pl.pallas_call(
        _attn_kernel,
        grid=(bsz,),
        in_specs=[pl.BlockSpec(memory_space=pltpu.SMEM),
                  pl.BlockSpec((None, seq, ATTN_WIDTH), lambda b: (b, 0, 0)),
                  pl.BlockSpec((None, seq, 2 * KV_WIDTH), lambda b: (b, 0, 0)),
                  pl.BlockSpec((BLOCK, 2 * KV_WIDTH), const2),
                  pl.BlockSpec((2, N_KV_HEADS, 2 * BLOCK, 4 * BLOCK), lambda b: (0, 0, 0, 0)),
                  pl.BlockSpec((1, ATTN_WIDTH), const2)],
        out_specs=pl.BlockSpec((None, seq, ATTN_WIDTH), lambda b: (b, 0, 0)),
        out_shape=jax.ShapeDtypeStruct((bsz, seq, ATTN_WIDTH), BF16),
        scratch_shapes=[kv_scratch, kv_scratch, kv_scratch, kv_scratch],
        compiler_params=_cparams(1),
        name="attention",
    )(sinks, q, kv, kvm, bias, g_attn)


def _sigmoid(v):
    return 0.5 * jnp.tanh(0.5 * v) + 0.5


def _gelu_tanh(y):
    c = float(np.sqrt(2.0 / np.pi))
    half = 0.5 * y
    return half + half * jnp.tanh(y * (c + (c * 0.044715) * (y * y)))


LRU_CHUNK = 44
LRU_SEG = SUBLANES * LRU_CHUNK
LRU_SLABS = LRU_WIDTH // LANES


def _lru_kernel(xr_ref, yr_ref, xrm_ref, yrm_ref, cw_ref, cb_ref, wa_ref, wx_ref, ba_ref,
                bx_ref, lam_ref, g_ref, o_ref, x_st, y_st, o_st, s_st, xtail, hcar, wg_ref):
    seq = xr_ref.shape[0]
    n_seg = (seq + BLOCK) // LRU_SEG

    @pl.when(pl.program_id(0) == 0)
    def _():
        c = wa_ref.shape[1]
        per = LANES // c
        zero = jnp.zeros((c, c), F32)
        for s in range(LRU_SLABS):
            rows = [jnp.concatenate([0.5 * w_ref[s * per + p] if q == p else zero
                                     for w_ref in (wa_ref, wx_ref) for q in range(per)], axis=1)
                    for p in range(per)]
            wg_ref[s] = jnp.concatenate(rows, axis=0).astype(BF16)

    xtail[...] = jnp.zeros_like(xtail)
    hcar[...] = jnp.zeros_like(hcar)
    lam = lam_ref[...]
    softplus_neg = jnp.maximum(-lam, 0.0) + jnp.log(1.0 + jnp.exp(-jnp.abs(lam)))
    sub = lax.broadcasted_iota(jnp.int32, (SUBLANES, LANES), 0)

    def half_bias(b_ref):
        per = LANES // b_ref.shape[1]
        return [0.5 * jnp.concatenate([b_ref[s * per + p:s * per + p + 1, :] for p in range(per)],
                                      axis=1) for s in range(LRU_SLABS)]

    half_ba, half_bx = half_bias(ba_ref), half_bias(bx_ref)

    def strided(j):
        return pl.ds(j, SUBLANES, stride=LRU_CHUNK)

    def piece(v, j):
        return v[j * SUBLANES:(j + 1) * SUBLANES, :]

    def segment(k, first):
        if first:
            head = LRU_SEG - BLOCK
            x_nat = jnp.concatenate([xrm_ref[...], xr_ref[0:head, :]], axis=0).astype(F32)
            y_nat = jnp.concatenate([yrm_ref[...], yr_ref[0:head, :]], axis=0).astype(F32)
        else:
            rows = pl.ds(pl.multiple_of(k * LRU_SEG - BLOCK, 2 * SUBLANES), LRU_SEG)
            x_nat = xr_ref[rows, :].astype(F32)
            y_nat = yr_ref[rows, :].astype(F32)
        for c in range(LRU_SLABS):
            x_st[c] = x_nat[:, c * LANES:(c + 1) * LANES]
            y_st[c] = y_nat[:, c * LANES:(c + 1) * LANES]
        first_row = k * LRU_SEG + LRU_CHUNK * sub
        sumsq = [jnp.zeros((SUBLANES, LANES), F32) for _ in range(LRU_CHUNK)]

        for c in range(LRU_SLABS):
            lanes = slice(c * LANES, (c + 1) * LANES)
            x = [x_st[c, strided(j), :] for j in range(LRU_CHUNK)]
            before = []
            for d in range(1, CONV_W):
                from_prev_chunk = pltpu.roll(x[LRU_CHUNK - d], 1, axis=0)
                before.append(jnp.where(sub == 0, xtail[d - 1:d, lanes], from_prev_chunk))
            for d in range(1, CONV_W):
                xtail[d - 1:d, lanes] = x[LRU_CHUNK - d][SUBLANES - 1:SUBLANES, :]

            def x_at(j):
                return x[j] if j >= 0 else before[-j - 1]

            taps = [cw_ref[t:t + 1, lanes] for t in range(CONV_W)]
            bias = cb_ref[:, lanes]
            xc = jnp.concatenate(
                [bias + sum(taps[t] * x_at(j - (CONV_W - 1) + t) for t in range(CONV_W))
                 for j in range(LRU_CHUNK)], axis=0)
            xcb = xc.astype(BF16)
            both = jnp.dot(xcb, wg_ref[c], preferred_element_type=F32)
            tr = jnp.tanh(both[:, :LANES] + half_ba[c])
            ti = jnp.tanh(both[:, LANES:] + half_bx[c])
            log_a_half = (-0.5 * LRU_C) * softplus_neg[:, lanes]
            a = jnp.exp(log_a_half * tr + log_a_half)
            half_xc = 0.5 * xc
            gated_x = half_xc * ti + half_xc
            z = 1.0 - a * a
            u = jnp.where(z > 0.0, z * lax.rsqrt(z), 0.0) * gated_x

            h = jnp.zeros((SUBLANES, LANES), F32)
            p = jnp.ones((SUBLANES, LANES), F32)
            hs, ps = [], []
            for j in range(LRU_CHUNK):
                aj = piece(a, j)
                uj = piece(u, j)
                if first:
                    uj = jnp.where(first_row + j >= PAD_FRONT, uj, 0.0)
                h = aj * h + uj
                p = aj * p
                hs.append(h)
                ps.append(p)
            entry = [hcar[:, lanes]]
            for s in range(SUBLANES):
                entry.append(h[s:s + 1, :] + p[s:s + 1, :] * entry[s])
            hcar[:, lanes] = entry[SUBLANES]
            entry_rows = jnp.concatenate(entry[:SUBLANES], axis=0)

            for j in range(LRU_CHUNK):
                state = hs[j] + ps[j] * entry_rows
                out = state * _gelu_tanh(y_st[c, strided(j), :])
                sumsq[j] = sumsq[j] + out * out
                o_st[c, strided(j), :] = out

        for j in range(LRU_CHUNK):
            ms = jnp.sum(sumsq[j], axis=1, keepdims=True) * (1.0 / LRU_WIDTH)
            s_st[strided(j), :] = jnp.broadcast_to(lax.rsqrt(ms + EPS), (SUBLANES, LANES))
        scale = s_st[...]
        for c in range(LRU_SLABS):
            lanes = slice(c * LANES, (c + 1) * LANES)
            normed = (o_st[c] * scale * g_ref[:, lanes]).astype(o_ref.dtype)
            if first:
                o_ref[0:LRU_SEG - BLOCK, lanes] = normed[BLOCK:, :]
            else:
                o_ref[rows, lanes] = normed

    assert BLOCK <= LRU_SEG
    segment(0, True)

    def later_segment(k, carry):
        segment(k, False)
        return carry

    lax.fori_loop(1, n_seg, later_segment, 0)


def _rglru(xr, yr, xrm, yrm, cw, cb, wa, wx, ba, bx, lam, g_lru, bsz, nbx):
    seq = nbx * BLOCK
    assert (seq + BLOCK) % LRU_SEG == 0
    main = pl.BlockSpec((None, seq, LRU_WIDTH), lambda b: (b, 0, 0))
    const2 = lambda b: (0, 0)
    row_spec = pl.BlockSpec((1, LRU_WIDTH), const2)
    gate_spec = pl.BlockSpec((None,) + wa.shape[1:], lambda b: (0, 0, 0, 0))
    bias_spec = pl.BlockSpec((None,) + ba.shape[1:], lambda b: (0, 0, 0))
    slabs = pltpu.VMEM((LRU_SLABS, LRU_SEG, LANES), F32)
    return pl.pallas_call(
        _lru_kernel,
        grid=(bsz,),
        in_specs=[main, main,
                  pl.BlockSpec((BLOCK, LRU_WIDTH), const2),
                  pl.BlockSpec((BLOCK, LRU_WIDTH), const2),
                  pl.BlockSpec((CONV_W, LRU_WIDTH), const2),
                  row_spec, gate_spec, gate_spec, bias_spec, bias_spec,
                  row_spec, row_spec],
        out_specs=main,
        out_shape=jax.ShapeDtypeStruct((bsz, seq, LRU_WIDTH), BF16),
        scratch_shapes=[slabs, slabs, slabs,
                        pltpu.VMEM((LRU_SEG, LANES), F32),
                        pltpu.VMEM((SUBLANES, LRU_WIDTH), F32),
                        pltpu.VMEM((1, LRU_WIDTH), F32),
                        pltpu.VMEM((LRU_SLABS, LANES, 2 * LANES), BF16)],
        compiler_params=_cparams(1),
        name="rglru",
    )(xr, yr, xrm, yrm, cw, cb, wa, wx, ba, bx, lam, g_lru)


def _pack_rows(v):
    bits = lax.bitcast_convert_type(v.astype(BF16).astype(F32), U32)
    return (bits[:, :PACKED] >> 16) | (bits[:, PACKED:] & jnp.uint32(0xFFFF0000))


def _unpack_rows(w):
    lo = lax.bitcast_convert_type(w << 16, F32)
    hi = lax.bitcast_convert_type(w & jnp.uint32(0xFFFF0000), F32)
    return lo, hi


def _layer_norm(z, g, b):
    mu = jnp.mean(z, axis=1, keepdims=True)
    zc = z - mu
    var = jnp.mean(zc * zc, axis=1, keepdims=True)
    return zc * lax.rsqrt(var + EPS) * g + b


def _out_proj_kernel(a_ref, l_ref, x_ref, w_ref, g_ref, b_ref, wrt_ref, brt_ref,
                     h_ref, hp_ref, info_ref, cnt_ref, w_b, wrt_b):
    @pl.when(pl.program_id(0) == 0)
    def _():
        cnt_ref[...] = jnp.zeros_like(cnt_ref)
        w_b[...] = w_ref[...].astype(BF16)
        hi = wrt_ref[...].astype(BF16)
        wrt_b[0:ROUTER_ROWS, :] = hi
        wrt_b[ROUTER_ROWS:, :] = (wrt_ref[...] - hi.astype(F32)).astype(BF16)

    mix = jnp.dot(jnp.concatenate([a_ref[...], l_ref[...]], axis=1), w_b[...],
                  preferred_element_type=F32)
    h = _layer_norm(ALPHA * x_ref[...] + mix, g_ref[...], b_ref[...])
    h_ref[...] = h
    hp_ref[...] = _pack_rows(h)

    h_hi = h.astype(BF16)
    h_lo = (h - h_hi.astype(F32)).astype(BF16)
    nt = (((1,), (1,)), ((), ()))
    both = lax.dot_general(wrt_b[...], h_hi, nt, preferred_element_type=F32)
    lg = (both[:ROUTER_ROWS] + both[ROUTER_ROWS:]
          + lax.dot_general(wrt_b[0:ROUTER_ROWS, :], h_lo, nt, preferred_element_type=F32)
          ) + brt_ref[...]
    tile_shape = (SUBLANES, h.shape[0])
    sub = lax.broadcasted_iota(I32, tile_shape, 0)
    ninf = -jnp.inf
    t0 = lg[0:SUBLANES]
    gl = jnp.where(sub < N_GROUPS, t0, ninf)
    gmax = jnp.max(gl, axis=0, keepdims=True)
    g_idx = jnp.min(jnp.where(gl == gmax, sub, SUBLANES), axis=0, keepdims=True)
    g_w = 1.0 / jnp.sum(jnp.where(sub < N_GROUPS, jnp.exp(t0 - gmax), 0.0),
                        axis=0, keepdims=True)
    el = lg[SUBLANES:2 * SUBLANES]
    for g in range(1, N_GROUPS):
        el = jnp.where(g_idx == g, lg[(g + 1) * SUBLANES:(g + 2) * SUBLANES], el)
    v1 = jnp.max(el, axis=0, keepdims=True)
    i1 = jnp.min(jnp.where(el == v1, sub, SUBLANES), axis=0, keepdims=True)
    el2 = jnp.where(sub == i1, ninf, el)
    v2 = jnp.max(el2, axis=0, keepdims=True)
    i2 = jnp.min(jnp.where(el2 == v2, sub, SUBLANES), axis=0, keepdims=True)
    t = jnp.exp(v2 - v1)
    w1 = 1.0 / (1.0 + t)
    w2 = t * w1
    e_base = g_idx * EXPERTS_PER_GROUP
    e1 = e_base + i1
    e2 = e_base + i2
    info_ref[...] = jnp.where(sub == 0, e1.astype(F32),
                              jnp.where(sub == 1, e2.astype(F32),
                                        jnp.where(sub == 2, g_w * w1,
                                                  jnp.where(sub == 3, g_w * w2, 0.0))))
    expert = lax.broadcasted_iota(I32, (N_EXPERTS, h.shape[0]), 0)
    chosen = (expert == e1).astype(F32) + (expert == e2).astype(F32)
    cnt_ref[...] += jnp.sum(chosen, axis=1, keepdims=True)


ROUTER_ROWS = -(-(N_GROUPS + 1) * SUBLANES // (2 * SUBLANES)) * (2 * SUBLANES)


def _out_proj(attn_n, lru_n, x2d, w_out, ln_g, ln_b, w_rt, b_rt, rows):
    n = x2d.shape[0]
    const = lambda i: (0, 0)
    tile = lambda w: pl.BlockSpec((rows, w), lambda i: (i, 0))
    return pl.pallas_call(
        _out_proj_kernel,
        grid=(n // rows,),
        in_specs=[tile(ATTN_WIDTH), tile(LRU_WIDTH), tile(D_MODEL),
                  pl.BlockSpec((D_MODEL, D_MODEL), const),
                  pl.BlockSpec((1, D_MODEL), const),
                  pl.BlockSpec((1, D_MODEL), const),
                  pl.BlockSpec((ROUTER_ROWS, D_MODEL), const),
                  pl.BlockSpec((ROUTER_ROWS, 1), const)],
        out_specs=[tile(D_MODEL), tile(PACKED),
                   pl.BlockSpec((SUBLANES, rows), lambda i: (0, i)),
                   pl.BlockSpec((N_EXPERTS, LANES), const)],
        out_shape=[jax.ShapeDtypeStruct((n, D_MODEL), F32),
                   jax.ShapeDtypeStruct((n, PACKED), U32),
                   jax.ShapeDtypeStruct((SUBLANES, n), F32),
                   jax.ShapeDtypeStruct((N_EXPERTS, LANES), F32)],
        scratch_shapes=[pltpu.VMEM((D_MODEL, D_MODEL), BF16),
                        pltpu.VMEM((2 * ROUTER_ROWS, D_MODEL), BF16)],
        compiler_params=_cparams(1),
        name="out_proj",
    )(attn_n, lru_n, x2d, w_out, ln_g, ln_b, w_rt, b_rt)


def _route_kernel(info_ref, cnt_ref, tri_ref, dest_ref, carry, pstart):
    t = pl.program_id(0)
    info = info_ref[...]
    shape = (N_EXPERTS, info.shape[1])
    expert = lax.broadcasted_iota(I32, shape, 0)
    oh1 = (expert == info[0:1, :].astype(I32)).astype(F32)
    oh2 = (expert == info[1:2, :].astype(I32)).astype(F32)
    both = oh1 + oh2

    @pl.when(t == 0)
    def _():
        c = cnt_ref[...].astype(I32)
        padded = ((c + (MOE_BLOCK - 1)) // MOE_BLOCK) * MOE_BLOCK
        e = lax.broadcasted_iota(I32, (N_EXPERTS, LANES), 0)
        scan = padded
        for d in (1, 2, 4, 8, 16):
            scan = scan + jnp.where(e >= d, pltpu.roll(scan, d, axis=0), 0)
        pstart[...] = (scan - padded)[:, 0:1].astype(F32)
        carry[...] = jnp.zeros_like(carry)

    before = jnp.dot(both.astype(BF16), tri_ref[...], preferred_element_type=F32)
    row_of = before + (carry[...] + pstart[...])
    r1 = jnp.sum(oh1 * row_of, axis=0, keepdims=True)
    r2 = jnp.sum(oh2 * row_of, axis=0, keepdims=True)
    dest_ref[...] = jnp.concatenate([r1, r2], axis=0).astype(I32)
    carry[...] += jnp.sum(both, axis=1, keepdims=True)


def _route(info_t, cnt, cols):
    n = info_t.shape[1]
    tri = jnp.asarray(np.triu(np.ones((cols, cols), np.float32), 1), BF16)
    return pl.pallas_call(
        _route_kernel,
        grid=(n // cols,),
        in_specs=[pl.BlockSpec((SUBLANES, cols), lambda t: (0, t)),
                  pl.BlockSpec((N_EXPERTS, LANES), lambda t: (0, 0)),
                  pl.BlockSpec((cols, cols), lambda t: (0, 0))],
        out_specs=pl.BlockSpec((TOP_K, cols), lambda t: (0, t)),
        out_shape=jax.ShapeDtypeStruct((TOP_K, n), I32),
        scratch_shapes=[pltpu.VMEM((N_EXPERTS, 1), F32), pltpu.VMEM((N_EXPERTS, 1), F32)],
        compiler_params=_cparams(1),
        name="route",
    )(info_t, cnt, tri)


def _sc_mesh():
    return plsc.VectorSubcoreMesh(core_axis_name="core", subcore_axis_name="subcore")


def _sc_worker_id():
    return lax.axis_index("subcore") * SC_CORES + lax.axis_index("core")


def _sc_scatter_rows(rows, dest, cap):
    n, width = rows.shape
    per_worker = n // SC_WORKERS
    n_win = per_worker // SC_WINDOW

    def body(x_hbm, d_hbm, o_hbm, i0_v, i1_v, rows_v, isem, rsem, sem0, sem1):
        wid = _sc_worker_id()

        def indices(j):
            win = pl.ds(wid * per_worker + j * SC_WINDOW, SC_WINDOW)
            return (pltpu.make_async_copy(d_hbm.at[0, win], i0_v.at[j], isem.at[0, j]),
                    pltpu.make_async_copy(d_hbm.at[1, win], i1_v.at[j], isem.at[1, j]))

        def read(j):
            src = x_hbm.at[pl.ds(wid * per_worker + j * SC_WINDOW, SC_WINDOW)]
            return pltpu.make_async_copy(src, rows_v.at[j % 2], rsem.at[j % 2])

        def scatters(j):
            return (pltpu.make_async_copy(rows_v.at[j % 2], o_hbm.at[i0_v.at[j]], sem0.at[j % 2]),
                    pltpu.make_async_copy(rows_v.at[j % 2], o_hbm.at[i1_v.at[j]], sem1.at[j % 2]))

        read(0).start()
        for j in range(n_win):
            for cp in indices(j):
                cp.start()
        for j in range(n_win):
            for cp in indices(j):
                cp.wait()
        for j in range(n_win):
            if j + 1 < n_win:
                if j >= 1:
                    for cp in scatters(j - 1):
                        cp.wait()
                read(j + 1).start()
            read(j).wait()
            for cp in scatters(j):
                cp.start()
        for j in range(max(n_win - 2, 0), n_win):
            for cp in scatters(j):
                cp.wait()

    return pl.kernel(
        body,
        out_type=jax.ShapeDtypeStruct((cap, width), rows.dtype),
        mesh=_sc_mesh(),
        scratch_types=[pltpu.VMEM((n_win, SC_WINDOW), I32), pltpu.VMEM((n_win, SC_WINDOW), I32),
                       pltpu.VMEM((2, SC_WINDOW, width), rows.dtype),
                       pltpu.SemaphoreType.DMA((TOP_K, n_win)),
                       pltpu.SemaphoreType.DMA((2,)), pltpu.SemaphoreType.DMA((2,)),
                       pltpu.SemaphoreType.DMA((2,))],
        name="dispatch",
    )(rows, dest)


def _sc_gather_rows(table, dest, start, length):
    width = table.shape[1]
    per_worker = TOP_K * length // SC_WORKERS
    n_win = per_worker // SC_WINDOW
    workers_per_choice = SC_WORKERS // TOP_K

    def body(y_hbm, d_hbm, o_hbm, i_v, rows_v, isem, gsem, wsem):
        wid = _sc_worker_id()
        choice = wid // workers_per_choice
        first = start + (wid % workers_per_choice) * per_worker

        def indices(j):
            win = pl.ds(first + j * SC_WINDOW, SC_WINDOW)
            return pltpu.make_async_copy(d_hbm.at[choice, win], i_v.at[j], isem.at[j])

        for j in range(n_win):
            indices(j).start()
        for j in range(n_win):
            indices(j).wait()

        def gather(j):
            return pltpu.make_async_copy(y_hbm.at[i_v.at[j]], rows_v.at[j % 2], gsem.at[j % 2])

        def write(j):
            dst = o_hbm.at[pl.ds(wid * per_worker + j * SC_WINDOW, SC_WINDOW)]
            return pltpu.make_async_copy(rows_v.at[j % 2], dst, wsem.at[j % 2])

        gather(0).start()
        for j in range(n_win):
            if j + 1 < n_win:
                if j >= 1:
                    write(j - 1).wait()
                gather(j + 1).start()
            gather(j).wait()
            write(j).start()
        for j in range(max(n_win - 2, 0), n_win):
            write(j).wait()

    return pl.kernel(
        body,
        out_type=jax.ShapeDtypeStruct((SC_WORKERS * per_worker, width), table.dtype),
        mesh=_sc_mesh(),
        scratch_types=[pltpu.VMEM((n_win, SC_WINDOW), I32),
                       pltpu.VMEM((2, SC_WINDOW, width), table.dtype),
                       pltpu.SemaphoreType.DMA((n_win,)),
                       pltpu.SemaphoreType.DMA((2,)), pltpu.SemaphoreType.DMA((2,))],
        name="collect",
    )(table, dest)


def _expert_kernel(bstart_ref, nblk_ref, nused_ref, xs_hbm, wg_ref, wu_ref, wd_ref, yb_hbm,
                   xbuf, ybuf, zbuf, xsem, ysem, zsem, wg_b, wu_b, wd_b):
    e = pl.program_id(0)
    nused = nused_ref[0]
    n_blocks = yb_hbm.shape[0] // MOE_BLOCK

    def rows(b):
        return pl.ds(pl.multiple_of(b * MOE_BLOCK, MOE_BLOCK), MOE_BLOCK)

    def x_copy(b):
        slot = b % X_RING
        return pltpu.make_async_copy(xs_hbm.at[rows(b)], xbuf.at[slot], xsem.at[slot])

    def y_copy(b):
        slot = b % Y_RING
        return pltpu.make_async_copy(ybuf.at[slot], yb_hbm.at[rows(b)], ysem.at[slot])

    @pl.when(e == 0)
    def _():
        for b in range(X_AHEAD):
            @pl.when(b < nused)
            def _():
                x_copy(b).start()

    wg_b[...] = wg_ref[...].astype(BF16)
    wu_b[...] = wu_ref[...].astype(BF16)
    wd_b[...] = wd_ref[...].astype(BF16)

    def run_blocks(b, count):
        for k in range(count):
            x_copy(b + k).wait()
        for k in range(count):
            nxt = b + X_AHEAD + k

            @pl.when(nxt < nused)
            def _():
                x_copy(nxt).start()

            @pl.when(b + k >= Y_RING)
            def _():
                y_copy(b + k - Y_RING).wait()

        words = jnp.concatenate([xbuf[(b + k) % X_RING] for k in range(count)], axis=0)
        lo, hi = _unpack_rows(words)
        x = jnp.concatenate([lo.astype(BF16), hi.astype(BF16)], axis=1)
        g = jnp.dot(x, wg_b[...], preferred_element_type=F32)
        u = jnp.dot(x, wu_b[...], preferred_element_type=F32)
        mid = (g * _sigmoid(g) * u).astype(BF16)
        y = _pack_rows(jnp.dot(mid, wd_b[...], preferred_element_type=F32))
        for k in range(count):
            ybuf[(b + k) % Y_RING] = y[k * MOE_BLOCK:(k + 1) * MOE_BLOCK]
            y_copy(b + k).start()

    b0 = bstart_ref[e]
    nb = nblk_ref[e]

    def group(i, carry):
        run_blocks(b0 + X_GROUP * i, X_GROUP)
        return carry

    lax.fori_loop(0, nb // X_GROUP, group, 0)
    done = nb - nb % X_GROUP
    size = X_GROUP // 2
    while size >= 1:
        @pl.when((nb // size) % 2 == 1)
        def _(size=size, done=done):
            run_blocks(b0 + done, size)

        done = done + (nb // size) % 2 * size
        size //= 2

    @pl.when(e == pl.num_programs(0) - 1)
    def _():
        for back in range(Y_RING, 0, -1):
            @pl.when(nused >= back)
            def _():
                y_copy(nused - back).wait()

        zbuf[...] = jnp.zeros_like(zbuf)

        def z_copy(b):
            return pltpu.make_async_copy(zbuf, yb_hbm.at[rows(b)], zsem.at[0])

        def z_start(b, carry):
            z_copy(b).start()
            return carry

        def z_wait(b, carry):
            z_copy(b).wait()
            return carry

        lax.fori_loop(nused, n_blocks, z_start, 0)
        lax.fori_loop(nused, n_blocks, z_wait, 0)


def _experts(bstart, nblk, nused, xs, w_gate, w_up, w_down):
    cap = xs.shape[0]
    w_idx = lambda e, bs, nb, nu: (e, 0, 0)
    grid_spec = pltpu.PrefetchScalarGridSpec(
        num_scalar_prefetch=3,
        grid=(N_EXPERTS,),
        in_specs=[pl.BlockSpec(memory_space=pl.ANY),
                  pl.BlockSpec((None, D_MODEL, D_FF), w_idx),
                  pl.BlockSpec((None, D_MODEL, D_FF), w_idx),
                  pl.BlockSpec((None, D_FF, D_MODEL), w_idx)],
        out_specs=pl.BlockSpec(memory_space=pl.ANY),
        scratch_shapes=[pltpu.VMEM((X_RING, MOE_BLOCK, PACKED), U32),
                        pltpu.VMEM((Y_RING, MOE_BLOCK, PACKED), U32),
                        pltpu.VMEM((MOE_BLOCK, PACKED), U32),
                        pltpu.SemaphoreType.DMA((X_RING,)),
                        pltpu.SemaphoreType.DMA((Y_RING,)),
                        pltpu.SemaphoreType.DMA((1,)),
                        pltpu.VMEM((D_MODEL, D_FF), BF16),
                        pltpu.VMEM((D_MODEL, D_FF), BF16),
                        pltpu.VMEM((D_FF, D_MODEL), BF16)])
    return pl.pallas_call(
        _expert_kernel,
        grid_spec=grid_spec,
        out_shape=jax.ShapeDtypeStruct((cap, PACKED), U32),
        compiler_params=_cparams(1),
        name="experts",
    )(bstart, nblk, nused, xs, w_gate, w_up, w_down)


def _combine_kernel(y0_ref, y1_ref, h_ref, info_ref, g_ref, b_ref, o_ref):
    info = info_ref[...].T
    g0 = info[:, 2:3]
    g1 = info[:, 3:4]
    lo0, hi0 = _unpack_rows(y0_ref[...])
    lo1, hi1 = _unpack_rows(y1_ref[...])
    y = jnp.concatenate([g0 * lo0 + g1 * lo1, g0 * hi0 + g1 * hi1], axis=1)
    o_ref[...] = _layer_norm(ALPHA * h_ref[...] + y, g_ref[...], b_ref[...])


def _combine(ys, h, info, ln_g, ln_b, rows, part, n_parts):
    n = h.shape[0]
    steps = n // n_parts // rows
    off = part * steps
    const = lambda i: (0, 0)
    return pl.pallas_call(
        _combine_kernel,
        grid=(steps,),
        in_specs=[pl.BlockSpec((rows, PACKED), lambda i: (i, 0)),
                  pl.BlockSpec((rows, PACKED), lambda i: (i + steps, 0)),
                  pl.BlockSpec((rows, D_MODEL), lambda i: (i + off, 0)),
                  pl.BlockSpec((SUBLANES, rows), lambda i: (0, i + off)),
                  pl.BlockSpec((1, D_MODEL), const),
                  pl.BlockSpec((1, D_MODEL), const)],
        out_specs=pl.BlockSpec((rows, D_MODEL), lambda i: (i + off, 0)),
        out_shape=jax.ShapeDtypeStruct((n, D_MODEL), F32),
        input_output_aliases={2: 0},
        compiler_params=_cparams(1),
        name="combine",
    )(ys, ys, h, info, ln_g, ln_b)


def _alibi_bias():
    qi = np.arange(BLOCK)[:, None]
    kj = np.arange(2 * BLOCK)[None, :]
    dist = qi - kj + BLOCK
    band = (dist >= 0) & (dist < BLOCK)
    slopes = np.exp2(-8.0 * np.arange(1, N_Q_HEADS + 1, dtype=np.float32) / N_Q_HEADS)
    bias = np.where(band[None], -slopes[:, None, None] * dist[None].astype(np.float32), NEG)
    bias = bias * LOG2E
    first = np.where((kj >= PAD_FRONT)[None], bias, NEG)
    out = np.empty((2, N_KV_HEADS, 2 * BLOCK, 4 * BLOCK), np.float32)
    for v, per_head in enumerate((first, bias)):
        for j in range(N_KV_HEADS):
            out[v, j] = np.block([[per_head[4 * j], per_head[4 * j + 1]],
                                  [per_head[4 * j + 2], per_head[4 * j + 3]]])
    return jnp.asarray(out, F32)


def kernel(x, meta_tokens, w_in, conv_w, conv_b, lru_wa, lru_ba, lru_wx, lru_bx, lru_lambda,
           attn_sinks, g_attn, g_lru, w_out, ln1_g, ln1_b, w_group, b_group, w_router,
           b_router, w_gate, w_up, w_down, ln2_g, ln2_b):
    bsz, seq, d = x.shape
    nbx = seq // BLOCK
    n_tok = bsz * seq
    x2d = x.reshape(n_tok, d)
    row = lambda v: v[0:1].reshape(1, -1).astype(F32)

    q_scale = jnp.concatenate([jnp.full((ATTN_WIDTH,), LOG2E * HEAD_DIM ** -0.5, F32),
                               jnp.ones((IN_COLS - ATTN_WIDTH,), F32)])
    w_in_b = (w_in[0] * q_scale).astype(BF16)
    q, kv, xr, yr = _in_proj(x2d, w_in_b, PROJ_ROWS)
    qm, kvm, xrm, yrm = _in_proj(meta_tokens.astype(F32), w_in_b, N_META, front=PAD_FRONT)
    shp = lambda a: a.reshape(bsz, seq, a.shape[-1])

    attn_n = _attention(attn_sinks[0].astype(F32), shp(q), shp(kv), kvm, _alibi_bias(),
                        row(g_attn), bsz, nbx)
    lru_n = _rglru(shp(xr), shp(yr), xrm, yrm, conv_w[0].astype(F32), row(conv_b),
                   lru_wa.astype(F32), lru_wx.astype(F32), lru_ba.astype(F32),
                   lru_bx.astype(F32), row(lru_lambda), row(g_lru), bsz, nbx)

    gpad = SUBLANES - N_GROUPS
    tail = ROUTER_ROWS - SUBLANES - N_EXPERTS
    w_rt = jnp.concatenate(
        [w_group[0].T, jnp.zeros((gpad, d), F32),
         jnp.transpose(w_router[0], (0, 2, 1)).reshape(N_EXPERTS, d),
         jnp.zeros((tail, d), F32)], axis=0).astype(F32)
    b_rt = jnp.concatenate([b_group[0], jnp.zeros((gpad,), F32), b_router[0].reshape(-1),
                            jnp.zeros((tail,), F32)]).astype(F32).reshape(ROUTER_ROWS, 1)
    h1, hp, info, cnt = _out_proj(
        attn_n.reshape(n_tok, ATTN_WIDTH), lru_n.reshape(n_tok, LRU_WIDTH), x2d,
        w_out[0].astype(F32), row(ln1_g), row(ln1_b), w_rt, b_rt, OUT_PROJ_ROWS)

    dest = _route(info, cnt, ROUTE_ROWS)
    n_slots = n_tok * TOP_K
    n_blocks = n_slots // MOE_BLOCK + N_EXPERTS
    cap = n_blocks * MOE_BLOCK
    nblk = (cnt[:, 0].astype(I32) + MOE_BLOCK - 1) // MOE_BLOCK
    bends = jnp.cumsum(nblk)
    bstart = (bends - nblk).astype(I32)
    nused = bends[-1:].astype(I32)

    xs = _sc_scatter_rows(hp, dest, cap)
    yb = _experts(bstart, nblk.astype(I32), nused, xs, w_gate[0], w_up[0], w_down[0])
    out = h1
    part_len = n_tok // COMBINE_PARTS
    for part in range(COMBINE_PARTS):
        ys = _sc_gather_rows(yb, dest, part * part_len, part_len)
        out = _combine(ys, out, info, row(ln2_g), row(ln2_b), COMBINE_ROWS,
                       part, COMBINE_PARTS)
    return out.reshape(bsz, seq, d)
```

```python
import jax
import jax.numpy as jnp
import numpy as np
from jax import lax
from jax.experimental import pallas as pl
from jax.experimental.pallas import tpu as pltpu
from jax.experimental.pallas import tpu_sc as plsc

F32 = jnp.float32
BF16 = jnp.bfloat16
U32 = jnp.uint32
I32 = jnp.int32

D_MODEL = 1024
N_META = 16
BLOCK = 128
PAD_FRONT = BLOCK - N_META
HEAD_DIM = 64
ATTN_WIDTH = 512
LRU_WIDTH = 512
N_Q_HEADS = 8
N_KV_HEADS = 2
KV_WIDTH = N_KV_HEADS * HEAD_DIM
LRU_BLOCKS = 8
CONV_W = 4
LRU_C = 8.0
IN_COLS = ATTN_WIDTH + 2 * KV_WIDTH + 2 * LRU_WIDTH
N_GROUPS = 4
EXPERTS_PER_GROUP = 8
N_EXPERTS = N_GROUPS * EXPERTS_PER_GROUP
TOP_K = 2
D_FF = 512
MOE_BLOCK = 256
ALPHA = 2.0 ** 0.25
EPS = 1e-5
NEG = -1e30
LOG2E = float(np.log2(np.e))
LANES = 128
SUBLANES = 8
PACKED = D_MODEL // 2

PROJ_ROWS = 1024
OUT_PROJ_ROWS = 1024
ROUTE_ROWS = 512
COMBINE_ROWS = 1024
COMBINE_PARTS = 4
X_GROUP = 4
X_AHEAD = 4
X_RING = X_AHEAD + X_GROUP
Y_RING = 2 * X_GROUP
VMEM_LIMIT = 48 * 1024 * 1024

SC_CORES = 2
SC_SUBCORES = 16
SC_WORKERS = SC_CORES * SC_SUBCORES
SC_WINDOW = 64


def _cparams(n_axes):
    return pltpu.CompilerParams(
        dimension_semantics=("arbitrary",) * n_axes, vmem_limit_bytes=VMEM_LIMIT)


def _in_proj_kernel(x_ref, w_ref, q_ref, kv_ref, xr_ref, yr_ref):
    proj = jnp.dot(x_ref[...].astype(BF16), w_ref[...], preferred_element_type=F32)
    front = q_ref.shape[0] - x_ref.shape[0]
    o = 0
    for ref, width in ((q_ref, ATTN_WIDTH), (kv_ref, 2 * KV_WIDTH),
                       (xr_ref, LRU_WIDTH), (yr_ref, LRU_WIDTH)):
        vals = proj[:, o:o + width].astype(ref.dtype)
        if front:
            vals = jnp.concatenate([jnp.zeros((front, width), ref.dtype), vals], axis=0)
        ref[...] = vals
        o += width


def _in_proj(x2d, w_bf16, rows, front=0):
    steps = x2d.shape[0] // rows
    widths = (ATTN_WIDTH, 2 * KV_WIDTH, LRU_WIDTH, LRU_WIDTH)
    return pl.pallas_call(
        _in_proj_kernel,
        grid=(steps,),
        in_specs=[pl.BlockSpec((rows, D_MODEL), lambda i: (i, 0)),
                  pl.BlockSpec((D_MODEL, IN_COLS), lambda i: (0, 0))],
        out_specs=[pl.BlockSpec((front + rows, w), lambda i: (i, 0)) for w in widths],
        out_shape=[jax.ShapeDtypeStruct((steps * (front + rows), w), BF16) for w in widths],
        compiler_params=_cparams(1),
        name="in_proj",
    )(x2d, w_bf16)


def _attn_kernel(sinks_ref, q_ref, kv_ref, kvm_ref, bias_ref, g_ref, o_ref,
                 klo, khi, vlo, vhi):
    nbx = q_ref.shape[0] // BLOCK
    lo_lanes = lax.broadcasted_iota(I32, (BLOCK // 2, LANES), 1) < HEAD_DIM

    def layout_block(n, blk):
        rows = pl.ds(pl.multiple_of(n * BLOCK, BLOCK), BLOCK)
        as_bf16 = lambda words: pltpu.bitcast(words, BF16)
        for src, dst_lo, dst_hi in ((blk[:, :KV_WIDTH], klo, khi), (blk[:, KV_WIDTH:], vlo, vhi)):
            w = pltpu.bitcast(src, U32)
            r = pltpu.roll(w, HEAD_DIM, axis=1)
            zero = jnp.zeros_like(w)
            dst_lo[0, rows, :] = as_bf16(jnp.where(lo_lanes, w, zero))
            dst_hi[0, rows, :] = as_bf16(jnp.where(lo_lanes, zero, r))
            dst_lo[1, rows, :] = as_bf16(jnp.where(lo_lanes, r, zero))
            dst_hi[1, rows, :] = as_bf16(jnp.where(lo_lanes, zero, w))

    layout_block(0, kvm_ref[...])

    def layout_body(n, carry):
        layout_block(n + 1, kv_ref[pl.ds(pl.multiple_of(n * BLOCK, BLOCK), BLOCK), :])
        return carry

    lax.fori_loop(0, nbx, layout_body, 0, unroll=4)

    ones_lo = jnp.where(lax.broadcasted_iota(I32, (2 * BLOCK, LANES), 1) < HEAD_DIM,
                        1.0, 0.0).astype(BF16)
    ones_hi = (1.0 - ones_lo.astype(F32)).astype(BF16)
    top_rows = lax.broadcasted_iota(I32, (2 * BLOCK, 1), 0) < BLOCK
    lo_half = lax.broadcasted_iota(I32, (2 * BLOCK, LANES), 1) < HEAD_DIM
    sinks = [sinks_ref[h] * LOG2E for h in range(N_Q_HEADS)]

    def block(i, carry):
        q_rows = pl.ds(pl.multiple_of(i * BLOCK, BLOCK), BLOCK)
        win = pl.ds(pl.multiple_of(i * BLOCK, BLOCK), 2 * BLOCK)
        q = q_ref[q_rows, :]
        first = jnp.minimum(i, 1)
        outs = []
        for j in range(N_KV_HEADS):
            q2 = jnp.concatenate([q[:, (2 * j) * LANES:(2 * j + 1) * LANES],
                                  q[:, (2 * j + 1) * LANES:(2 * j + 2) * LANES]], axis=0)
            kc = jnp.concatenate([klo[j, win, :], khi[j, win, :]], axis=0)
            s = lax.dot_general(q2, kc, (((1,), (1,)), ((), ())), preferred_element_type=F32)
            s = s + bias_ref[first, j]
            ps, es = [], []
            for c in range(2):
                sink = jnp.where(top_rows, sinks[4 * j + c], sinks[4 * j + 2 + c])
                sc = s[:, c * 2 * BLOCK:(c + 1) * 2 * BLOCK]
                m = jnp.maximum(jnp.max(sc, axis=1, keepdims=True), sink)
                ps.append(jnp.exp2(sc - m).astype(BF16))
                es.append(jnp.exp2(sink - m))
            v_lo = jnp.concatenate([vlo[j, win, :], ones_lo], axis=1)
            v_hi = jnp.concatenate([vhi[j, win, :], ones_hi], axis=1)
            r = jnp.dot(jnp.concatenate(ps, axis=1), jnp.concatenate([v_lo, v_hi], axis=0),
                        preferred_element_type=F32)
            den = r[:, LANES:] + jnp.where(lo_half, es[0], es[1])
            o2 = r[:, :LANES] * (1.0 / den)
            outs += [o2[:BLOCK], o2[BLOCK:]]
        out = jnp.concatenate(outs, axis=1)
        ms = jnp.mean(out * out, axis=1, keepdims=True)
        o_ref[q_rows, :] = (out * lax.rsqrt(ms + EPS) * g_ref[...]).astype(o_ref.dtype)
        return carry

    lax.fori_loop(0, nbx, block, 0, unroll=8)


def _attention(sinks, q, kv, kvm, bias, g_attn, bsz, nbx):
    seq = nbx * BLOCK
    const2 = lambda b: (0, 0)
    kv_scratch = pltpu.VMEM((N_KV_HEADS, seq + BLOCK, LANES), BF16)
    return pl.pallas_call(
        _attn_kernel,
        grid=(bsz,),
        in_specs=[pl.BlockSpec(memory_space=pltpu.SMEM),
                  pl.BlockSpec((None, seq, ATTN_WIDTH), lambda b: (b, 0, 0)),
                  pl.BlockSpec((None, seq, 2 * KV_WIDTH), lambda b: (b, 0, 0)),
                  pl.BlockSpec((BLOCK, 2 * KV_WIDTH), const2),
                  pl.BlockSpec((2, N_KV_HEADS, 2 * BLOCK, 4 * BLOCK), lambda b: (0, 0, 0, 0)),
                  pl.BlockSpec((1, ATTN_WIDTH), const2)],
        out_specs=pl.BlockSpec((None, seq, ATTN_WIDTH), lambda b: (b, 0, 0)),
        out_shape=jax.ShapeDtypeStruct((bsz, seq, ATTN_WIDTH), BF16),
        scratch_shapes=[kv_scratch, kv_scratch, kv_scratch, kv_scratch],
        compiler_params=_cparams(1),
        name="attention",
    )(sinks, q, kv, kvm, bias, g_attn)


def _sigmoid(v):
    return 0.5 * jnp.tanh(0.5 * v) + 0.5


def _gelu_tanh(y):
    c = float(np.sqrt(2.0 / np.pi))
    half = 0.5 * y
    return half + half * jnp.tanh(y * (c + (c * 0.044715) * (y * y)))


LRU_CHUNK = 44
LRU_SEG = SUBLANES * LRU_CHUNK
LRU_SLABS = LRU_WIDTH // LANES


def _lru_kernel(xr_ref, yr_ref, xrm_ref, yrm_ref, cw_ref, cb_ref, wa_ref, wx_ref, ba_ref,
                bx_ref, lam_ref, g_ref, o_ref, x_st, y_st, o_st, s_st, xtail, hcar, wg_ref):
    seq = xr_ref.shape[0]
    n_seg = (seq + BLOCK) // LRU_SEG

    @pl.when(pl.program_id(0) == 0)
    def _():
        c = wa_ref.shape[1]
        per = LANES // c
        zero = jnp.zeros((c, c), F32)
        for s in range(LRU_SLABS):
            rows = [jnp.concatenate([0.5 * w_ref[s * per + p] if q == p else zero
                                     for w_ref in (wa_ref, wx_ref) for q in range(per)], axis=1)
                    for p in range(per)]
            wg_ref[s] = jnp.concatenate(rows, axis=0).astype(BF16)

    xtail[...] = jnp.zeros_like(xtail)
    hcar[...] = jnp.zeros_like(hcar)
    lam = lam_ref[...]
    softplus_neg = jnp.maximum(-lam, 0.0) + jnp.log(1.0 + jnp.exp(-jnp.abs(lam)))
    sub = lax.broadcasted_iota(jnp.int32, (SUBLANES, LANES), 0)

    def half_bias(b_ref):
        per = LANES // b_ref.shape[1]
        return [0.5 * jnp.concatenate([b_ref[s * per + p:s * per + p + 1, :] for p in range(per)],
                                      axis=1) for s in range(LRU_SLABS)]

    half_ba, half_bx = half_bias(ba_ref), half_bias(bx_ref)

    def strided(j):
        return pl.ds(j, SUBLANES, stride=LRU_CHUNK)

    def piece(v, j):
        return v[j * SUBLANES:(j + 1) * SUBLANES, :]

    def segment(k, first):
        if first:
            head = LRU_SEG - BLOCK
            x_nat = jnp.concatenate([xrm_ref[...], xr_ref[0:head, :]], axis=0).astype(F32)
            y_nat = jnp.concatenate([yrm_ref[...], yr_ref[0:head, :]], axis=0).astype(F32)
        else:
            rows = pl.ds(pl.multiple_of(k * LRU_SEG - BLOCK, 2 * SUBLANES), LRU_SEG)
            x_nat = xr_ref[rows, :].astype(F32)
            y_nat = yr_ref[rows, :].astype(F32)
        for c in range(LRU_SLABS):
            x_st[c] = x_nat[:, c * LANES:(c + 1) * LANES]
            y_st[c] = y_nat[:, c * LANES:(c + 1) * LANES]
        first_row = k * LRU_SEG + LRU_CHUNK * sub
        sumsq = [jnp.zeros((SUBLANES, LANES), F32) for _ in range(LRU_CHUNK)]

        for c in range(LRU_SLABS):
            lanes = slice(c * LANES, (c + 1) * LANES)
            x = [x_st[c, strided(j), :] for j in range(LRU_CHUNK)]
            before = []
            for d in range(1, CONV_W):
                from_prev_chunk = pltpu.roll(x[LRU_CHUNK - d], 1, axis=0)
                before.append(jnp.where(sub == 0, xtail[d - 1:d, lanes], from_prev_chunk))
            for d in range(1, CONV_W):
                xtail[d - 1:d, lanes] = x[LRU_CHUNK - d][SUBLANES - 1:SUBLANES, :]

            def x_at(j):
                return x[j] if j >= 0 else before[-j - 1]

            taps = [cw_ref[t:t + 1, lanes] for t in range(CONV_W)]
            bias = cb_ref[:, lanes]
            xc = jnp.concatenate(
                [bias + sum(taps[t] * x_at(j - (CONV_W - 1) + t) for t in range(CONV_W))
                 for j in range(LRU_CHUNK)], axis=0)
            xcb = xc.astype(BF16)
            both = jnp.dot(xcb, wg_ref[c], preferred_element_type=F32)
            tr = jnp.tanh(both[:, :LANES] + half_ba[c])
            ti = jnp.tanh(both[:, LANES:] + half_bx[c])
            log_a_half = (-0.5 * LRU_C) * softplus_neg[:, lanes]
            a = jnp.exp(log_a_half * tr + log_a_half)
            half_xc = 0.5 * xc
            gated_x = half_xc * ti + half_xc
            z = 1.0 - a * a
            u = jnp.where(z > 0.0, z * lax.rsqrt(z), 0.0) * gated_x

            h = jnp.zeros((SUBLANES, LANES), F32)
            p = jnp.ones((SUBLANES, LANES), F32)
            hs, ps = [], []
            for j in range(LRU_CHUNK):
                aj = piece(a, j)
                uj = piece(u, j)
                if first:
                    uj = jnp.where(first_row + j >= PAD_FRONT, uj, 0.0)
                h = aj * h + uj
                p = aj * p
                hs.append(h)
                ps.append(p)
            entry = [hcar[:, lanes]]
            for s in range(SUBLANES):
                entry.append(h[s:s + 1, :] + p[s:s + 1, :] * entry[s])
            hcar[:, lanes] = entry[SUBLANES]
            entry_rows = jnp.concatenate(entry[:SUBLANES], axis=0)

            for j in range(LRU_CHUNK):
                state = hs[j] + ps[j] * entry_rows
                out = state * _gelu_tanh(y_st[c, strided(j), :])
                sumsq[j] = sumsq[j] + out * out
                o_st[c, strided(j), :] = out

        for j in range(LRU_CHUNK):
            ms = jnp.sum(sumsq[j], axis=1, keepdims=True) * (1.0 / LRU_WIDTH)
            s_st[strided(j), :] = jnp.broadcast_to(lax.rsqrt(ms + EPS), (SUBLANES, LANES))
        scale = s_st[...]
        for c in range(LRU_SLABS):
            lanes = slice(c * LANES, (c + 1) * LANES)
            normed = (o_st[c] * scale * g_ref[:, lanes]).astype(o_ref.dtype)
            if first:
                o_ref[0:LRU_SEG - BLOCK, lanes] = normed[BLOCK:, :]
            else:
                o_ref[rows, lanes] = normed

    assert BLOCK <= LRU_SEG
    segment(0, True)

    def later_segment(k, carry):
        segment(k, False)
        return carry

    lax.fori_loop(1, n_seg, later_segment, 0)


def _rglru(xr, yr, xrm, yrm, cw, cb, wa, wx, ba, bx, lam, g_lru, bsz, nbx):
    seq = nbx * BLOCK
    assert (seq + BLOCK) % LRU_SEG == 0
    main = pl.BlockSpec((None, seq, LRU_WIDTH), lambda b: (b, 0, 0))
    const2 = lambda b: (0, 0)
    row_spec = pl.BlockSpec((1, LRU_WIDTH), const2)
    gate_spec = pl.BlockSpec((None,) + wa.shape[1:], lambda b: (0, 0, 0, 0))
    bias_spec = pl.BlockSpec((None,) + ba.shape[1:], lambda b: (0, 0, 0))
    slabs = pltpu.VMEM((LRU_SLABS, LRU_SEG, LANES), F32)
    return pl.pallas_call(
        _lru_kernel,
        grid=(bsz,),
        in_specs=[main, main,
                  pl.BlockSpec((BLOCK, LRU_WIDTH), const2),
                  pl.BlockSpec((BLOCK, LRU_WIDTH), const2),
                  pl.BlockSpec((CONV_W, LRU_WIDTH), const2),
                  row_spec, gate_spec, gate_spec, bias_spec, bias_spec,
                  row_spec, row_spec],
        out_specs=main,
        out_shape=jax.ShapeDtypeStruct((bsz, seq, LRU_WIDTH), BF16),
        scratch_shapes=[slabs, slabs, slabs,
                        pltpu.VMEM((LRU_SEG, LANES), F32),
                        pltpu.VMEM((SUBLANES, LRU_WIDTH), F32),
                        pltpu.VMEM((1, LRU_WIDTH), F32),
                        pltpu.VMEM((LRU_SLABS, LANES, 2 * LANES), BF16)],
        compiler_params=_cparams(1),
        name="rglru",
    )(xr, yr, xrm, yrm, cw, cb, wa, wx, ba, bx, lam, g_lru)


def _pack_rows(v):
    bits = lax.bitcast_convert_type(v.astype(BF16).astype(F32), U32)
    return (bits[:, :PACKED] >> 16) | (bits[:, PACKED:] & jnp.uint32(0xFFFF0000))


def _unpack_rows(w):
    lo = lax.bitcast_convert_type(w << 16, F32)
    hi = lax.bitcast_convert_type(w & jnp.uint32(0xFFFF0000), F32)
    return lo, hi


def _layer_norm(z, g, b):
    mu = jnp.mean(z, axis=1, keepdims=True)
    zc = z - mu
    var = jnp.mean(zc * zc, axis=1, keepdims=True)
    return zc * lax.rsqrt(var + EPS) * g + b


def _out_proj_kernel(a_ref, l_ref, x_ref, w_ref, g_ref, b_ref, wrt_ref, brt_ref, tri_ref,
                     h_ref, hp_ref, info_ref, cnt_ref, dest_ref, w_b, wrt_b):
    @pl.when(pl.program_id(0) == 0)
    def _():
        cnt_ref[...] = jnp.zeros_like(cnt_ref)
        w_b[...] = w_ref[...].astype(BF16)
        hi = wrt_ref[...].astype(BF16)
        wrt_b[0:ROUTER_ROWS, :] = hi
        wrt_b[ROUTER_ROWS:, :] = (wrt_ref[...] - hi.astype(F32)).astype(BF16)

    mix = jnp.dot(jnp.concatenate([a_ref[...], l_ref[...]], axis=1), w_b[...],
                  preferred_element_type=F32)
    h = _layer_norm(ALPHA * x_ref[...] + mix, g_ref[...], b_ref[...])
    h_ref[...] = h
    hp_ref[...] = _pack_rows(h)

    h_hi = h.astype(BF16)
    h_lo = (h - h_hi.astype(F32)).astype(BF16)
    nt = (((1,), (1,)), ((), ()))
    both = lax.dot_general(wrt_b[...], h_hi, nt, preferred_element_type=F32)
    lg = (both[:ROUTER_ROWS] + both[ROUTER_ROWS:]
          + lax.dot_general(wrt_b[0:ROUTER_ROWS, :], h_lo, nt, preferred_element_type=F32)
          ) + brt_ref[...]
    tile_shape = (SUBLANES, h.shape[0])
    sub = lax.broadcasted_iota(I32, tile_shape, 0)
    ninf = -jnp.inf
    t0 = lg[0:SUBLANES]
    gl = jnp.where(sub < N_GROUPS, t0, ninf)
    gmax = jnp.max(gl, axis=0, keepdims=True)
    g_idx = jnp.min(jnp.where(gl == gmax, sub, SUBLANES), axis=0, keepdims=True)
    g_w = 1.0 / jnp.sum(jnp.where(sub < N_GROUPS, jnp.exp(t0 - gmax), 0.0),
                        axis=0, keepdims=True)
    el = lg[SUBLANES:2 * SUBLANES]
    for g in range(1, N_GROUPS):
        el = jnp.where(g_idx == g, lg[(g + 1) * SUBLANES:(g + 2) * SUBLANES], el)
    v1 = jnp.max(el, axis=0, keepdims=True)
    i1 = jnp.min(jnp.where(el == v1, sub, SUBLANES), axis=0, keepdims=True)
    el2 = jnp.where(sub == i1, ninf, el)
    v2 = jnp.max(el2, axis=0, keepdims=True)
    i2 = jnp.min(jnp.where(el2 == v2, sub, SUBLANES), axis=0, keepdims=True)
    t = jnp.exp(v2 - v1)
    w1 = 1.0 / (1.0 + t)
    w2 = t * w1
    e_base = g_idx * EXPERTS_PER_GROUP
    e1 = e_base + i1
    e2 = e_base + i2
    step = pl.program_id(0)
    rows = h.shape[0]
    info_ref[:, pl.ds(pl.multiple_of(step * rows, rows), rows)] = jnp.where(
        sub == 0, e1.astype(F32),
        jnp.where(sub == 1, e2.astype(F32),
                  jnp.where(sub == 2, g_w * w1, jnp.where(sub == 3, g_w * w2, 0.0))))
    expert = lax.broadcasted_iota(I32, (N_EXPERTS, rows), 0)
    chosen = (expert == e1).astype(F32) + (expert == e2).astype(F32)
    cnt_ref[...] += jnp.sum(chosen, axis=1, keepdims=True)

    @pl.when(step == pl.num_programs(0) - 1)
    def _():
        _route_slots(info_ref, cnt_ref, tri_ref, dest_ref)


def _route_slots(info_ref, cnt_ref, tri_ref, dest_ref):
    cols = tri_ref.shape[0]
    c = cnt_ref[...].astype(I32)
    padded = ((c + (MOE_BLOCK - 1)) // MOE_BLOCK) * MOE_BLOCK
    e = lax.broadcasted_iota(I32, (N_EXPERTS, LANES), 0)
    scan = padded
    for d in (1, 2, 4, 8, 16):
        scan = scan + jnp.where(e >= d, pltpu.roll(scan, d, axis=0), 0)
    pstart = (scan - padded)[:, 0:1].astype(F32)
    expert = lax.broadcasted_iota(I32, (N_EXPERTS, cols), 0)

    def tile(t, seen):
        win = pl.ds(pl.multiple_of(t * cols, cols), cols)
        oh1 = (expert == info_ref[0:1, win].astype(I32)).astype(F32)
        oh2 = (expert == info_ref[1:2, win].astype(I32)).astype(F32)
        both = oh1 + oh2
        before = jnp.dot(both.astype(BF16), tri_ref[...], preferred_element_type=F32)
        row_of = before + (seen + pstart)
        r1 = jnp.sum(oh1 * row_of, axis=0, keepdims=True)
        r2 = jnp.sum(oh2 * row_of, axis=0, keepdims=True)
        dest_ref[:, win] = jnp.concatenate([r1, r2], axis=0).astype(I32)
        return seen + jnp.sum(both, axis=1, keepdims=True)

    lax.fori_loop(0, info_ref.shape[1] // cols, tile, jnp.zeros((N_EXPERTS, 1), F32))


ROUTER_ROWS = -(-(N_GROUPS + 1) * SUBLANES // (2 * SUBLANES)) * (2 * SUBLANES)


def _out_proj(attn_n, lru_n, x2d, w_out, ln_g, ln_b, w_rt, b_rt, rows, route_cols):
    n = x2d.shape[0]
    const = lambda i: (0, 0)
    tile = lambda w: pl.BlockSpec((rows, w), lambda i: (i, 0))
    tri = jnp.asarray(np.triu(np.ones((route_cols, route_cols), np.float32), 1), BF16)
    return pl.pallas_call(
        _out_proj_kernel,
        grid=(n // rows,),
        in_specs=[tile(ATTN_WIDTH), tile(LRU_WIDTH), tile(D_MODEL),
                  pl.BlockSpec((D_MODEL, D_MODEL), const),
                  pl.BlockSpec((1, D_MODEL), const),
                  pl.BlockSpec((1, D_MODEL), const),
                  pl.BlockSpec((ROUTER_ROWS, D_MODEL), const),
                  pl.BlockSpec((ROUTER_ROWS, 1), const),
                  pl.BlockSpec((route_cols, route_cols), const)],
        out_specs=[tile(D_MODEL), tile(PACKED),
                   pl.BlockSpec((SUBLANES, n), const),
                   pl.BlockSpec((N_EXPERTS, LANES), const),
                   pl.BlockSpec((TOP_K, n), const)],
        out_shape=[jax.ShapeDtypeStruct((n, D_MODEL), F32),
                   jax.ShapeDtypeStruct((n, PACKED), U32),
                   jax.ShapeDtypeStruct((SUBLANES, n), F32),
                   jax.ShapeDtypeStruct((N_EXPERTS, LANES), F32),
                   jax.ShapeDtypeStruct((TOP_K, n), I32)],
        scratch_shapes=[pltpu.VMEM((D_MODEL, D_MODEL), BF16),
                        pltpu.VMEM((2 * ROUTER_ROWS, D_MODEL), BF16)],
        compiler_params=_cparams(1),
        name="out_proj",
    )(attn_n, lru_n, x2d, w_out, ln_g, ln_b, w_rt, b_rt, tri)


def _sc_mesh():
    return plsc.VectorSubcoreMesh(core_axis_name="core", subcore_axis_name="subcore")


def _sc_worker_id():
    return lax.axis_index("subcore") * SC_CORES + lax.axis_index("core")


def _sc_scatter_rows(rows, dest, cap):
    n, width = rows.shape
    per_worker = n // SC_WORKERS
    n_win = per_worker // SC_WINDOW

    def body(x_hbm, d_hbm, o_hbm, i0_v, i1_v, rows_v, isem, rsem, sem0, sem1):
        wid = _sc_worker_id()

        def indices(j):
            win = pl.ds(wid * per_worker + j * SC_WINDOW, SC_WINDOW)
            return (pltpu.make_async_copy(d_hbm.at[0, win], i0_v.at[j], isem.at[0, j]),
                    pltpu.make_async_copy(d_hbm.at[1, win], i1_v.at[j], isem.at[1, j]))

        def read(j):
            src = x_hbm.at[pl.ds(wid * per_worker + j * SC_WINDOW, SC_WINDOW)]
            return pltpu.make_async_copy(src, rows_v.at[j % 2], rsem.at[j % 2])

        def scatters(j):
            return (pltpu.make_async_copy(rows_v.at[j % 2], o_hbm.at[i0_v.at[j]], sem0.at[j % 2]),
                    pltpu.make_async_copy(rows_v.at[j % 2], o_hbm.at[i1_v.at[j]], sem1.at[j % 2]))

        read(0).start()
        for j in range(n_win):
            for cp in indices(j):
                cp.start()
        for j in range(n_win):
            for cp in indices(j):
                cp.wait()
        for j in range(n_win):
            if j + 1 < n_win:
                if j >= 1:
                    for cp in scatters(j - 1):
                        cp.wait()
                read(j + 1).start()
            read(j).wait()
            for cp in scatters(j):
                cp.start()
        for j in range(max(n_win - 2, 0), n_win):
            for cp in scatters(j):
                cp.wait()

    return pl.kernel(
        body,
        out_type=jax.ShapeDtypeStruct((cap, width), rows.dtype),
        mesh=_sc_mesh(),
        scratch_types=[pltpu.VMEM((n_win, SC_WINDOW), I32), pltpu.VMEM((n_win, SC_WINDOW), I32),
                       pltpu.VMEM((2, SC_WINDOW, width), rows.dtype),
                       pltpu.SemaphoreType.DMA((TOP_K, n_win)),
                       pltpu.SemaphoreType.DMA((2,)), pltpu.SemaphoreType.DMA((2,)),
                       pltpu.SemaphoreType.DMA((2,))],
        name="dispatch",
    )(rows, dest)


def _sc_gather_rows(table, dest, start, length):
    width = table.shape[1]
    per_worker = TOP_K * length // SC_WORKERS
    n_win = per_worker // SC_WINDOW
    workers_per_choice = SC_WORKERS // TOP_K

    def body(y_hbm, d_hbm, o_hbm, i_v, rows_v, isem, gsem, wsem):
        wid = _sc_worker_id()
        choice = wid // workers_per_choice
        first = start + (wid % workers_per_choice) * per_worker

        def indices(j):
            win = pl.ds(first + j * SC_WINDOW, SC_WINDOW)
            return pltpu.make_async_copy(d_hbm.at[choice, win], i_v.at[j], isem.at[j])

        for j in range(n_win):
            indices(j).start()
        for j in range(n_win):
            indices(j).wait()

        def gather(j):
            return pltpu.make_async_copy(y_hbm.at[i_v.at[j]], rows_v.at[j % 2], gsem.at[j % 2])

        def write(j):
            dst = o_hbm.at[pl.ds(wid * per_worker + j * SC_WINDOW, SC_WINDOW)]
            return pltpu.make_async_copy(rows_v.at[j % 2], dst, wsem.at[j % 2])

        gather(0).start()
        for j in range(n_win):
            if j + 1 < n_win:
                if j >= 1:
                    write(j - 1).wait()
                gather(j + 1).start()
            gather(j).wait()
            write(j).start()
        for j in range(max(n_win - 2, 0), n_win):
            write(j).wait()

    return pl.kernel(
        body,
        out_type=jax.ShapeDtypeStruct((SC_WORKERS * per_worker, width), table.dtype),
        mesh=_sc_mesh(),
        scratch_types=[pltpu.VMEM((n_win, SC_WINDOW), I32),
                       pltpu.VMEM((2, SC_WINDOW, width), table.dtype),
                       pltpu.SemaphoreType.DMA((n_win,)),
                       pltpu.SemaphoreType.DMA((2,)), pltpu.SemaphoreType.DMA((2,))],
        name="collect",
    )(table, dest)


def _expert_kernel(bstart_ref, nblk_ref, nused_ref, xs_hbm, wg_ref, wu_ref, wd_ref, yb_hbm,
                   xbuf, ybuf, zbuf, xsem, ysem, zsem, wg_b, wu_b, wd_b):
    e = pl.program_id(0)
    nused = nused_ref[0]
    n_blocks = yb_hbm.shape[0] // MOE_BLOCK

    def rows(b):
        return pl.ds(pl.multiple_of(b * MOE_BLOCK, MOE_BLOCK), MOE_BLOCK)

    def x_copy(b):
        slot = b % X_RING
        return pltpu.make_async_copy(xs_hbm.at[rows(b)], xbuf.at[slot], xsem.at[slot])

    def y_copy(b):
        slot = b % Y_RING
        return pltpu.make_async_copy(ybuf.at[slot], yb_hbm.at[rows(b)], ysem.at[slot])

    @pl.when(e == 0)
    def _():
        for b in range(X_AHEAD):
            @pl.when(b < nused)
            def _():
                x_copy(b).start()

    wg_b[...] = wg_ref[...].astype(BF16)
    wu_b[...] = wu_ref[...].astype(BF16)
    wd_b[...] = wd_ref[...].astype(BF16)

    def run_blocks(b, count):
        for k in range(count):
            x_copy(b + k).wait()
        for k in range(count):
            nxt = b + X_AHEAD + k

            @pl.when(nxt < nused)
            def _():
                x_copy(nxt).start()

            @pl.when(b + k >= Y_RING)
            def _():
                y_copy(b + k - Y_RING).wait()

        words = jnp.concatenate([xbuf[(b + k) % X_RING] for k in range(count)], axis=0)
        lo, hi = _unpack_rows(words)
        x = jnp.concatenate([lo.astype(BF16), hi.astype(BF16)], axis=1)
        g = jnp.dot(x, wg_b[...], preferred_element_type=F32)
        u = jnp.dot(x, wu_b[...], preferred_element_type=F32)
        mid = (g * _sigmoid(g) * u).astype(BF16)
        y = _pack_rows(jnp.dot(mid, wd_b[...], preferred_element_type=F32))
        for k in range(count):
            ybuf[(b + k) % Y_RING] = y[k * MOE_BLOCK:(k + 1) * MOE_BLOCK]
            y_copy(b + k).start()

    b0 = bstart_ref[e]
    nb = nblk_ref[e]

    def group(i, carry):
        run_blocks(b0 + X_GROUP * i, X_GROUP)
        return carry

    lax.fori_loop(0, nb // X_GROUP, group, 0)
    done = nb - nb % X_GROUP
    size = X_GROUP // 2
    while size >= 1:
        @pl.when((nb // size) % 2 == 1)
        def _(size=size, done=done):
            run_blocks(b0 + done, size)

        done = done + (nb // size) % 2 * size
        size //= 2

    @pl.when(e == pl.num_programs(0) - 1)
    def _():
        for back in range(Y_RING, 0, -1):
            @pl.when(nused >= back)
            def _():
                y_copy(nused - back).wait()

        zbuf[...] = jnp.zeros_like(zbuf)

        def z_copy(b):
            return pltpu.make_async_copy(zbuf, yb_hbm.at[rows(b)], zsem.at[0])

        def z_start(b, carry):
            z_copy(b).start()
            return carry

        def z_wait(b, carry):
            z_copy(b).wait()
            return carry

        lax.fori_loop(nused, n_blocks, z_start, 0)
        lax.fori_loop(nused, n_blocks, z_wait, 0)


def _experts(bstart, nblk, nused, xs, w_gate, w_up, w_down):
    cap = xs.shape[0]
    w_idx = lambda e, bs, nb, nu: (e, 0, 0)
    grid_spec = pltpu.PrefetchScalarGridSpec(
        num_scalar_prefetch=3,
        grid=(N_EXPERTS,),
        in_specs=[pl.BlockSpec(memory_space=pl.ANY),
                  pl.BlockSpec((None, D_MODEL, D_FF), w_idx),
                  pl.BlockSpec((None, D_MODEL, D_FF), w_idx),
                  pl.BlockSpec((None, D_FF, D_MODEL), w_idx)],
        out_specs=pl.BlockSpec(memory_space=pl.ANY),
        scratch_shapes=[pltpu.VMEM((X_RING, MOE_BLOCK, PACKED), U32),
                        pltpu.VMEM((Y_RING, MOE_BLOCK, PACKED), U32),
                        pltpu.VMEM((MOE_BLOCK, PACKED), U32),
                        pltpu.SemaphoreType.DMA((X_RING,)),
                        pltpu.SemaphoreType.DMA((Y_RING,)),
                        pltpu.SemaphoreType.DMA((1,)),
                        pltpu.VMEM((D_MODEL, D_FF), BF16),
                        pltpu.VMEM((D_MODEL, D_FF), BF16),
                        pltpu.VMEM((D_FF, D_MODEL), BF16)])
    return pl.pallas_call(
        _expert_kernel,
        grid_spec=grid_spec,
        out_shape=jax.ShapeDtypeStruct((cap, PACKED), U32),
        compiler_params=_cparams(1),
        name="experts",
    )(bstart, nblk, nused, xs, w_gate, w_up, w_down)


def _combine_kernel(y0_ref, y1_ref, h_ref, info_ref, g_ref, b_ref, o_ref):
    info = info_ref[...].T
    g0 = info[:, 2:3]
    g1 = info[:, 3:4]
    lo0, hi0 = _unpack_rows(y0_ref[...])
    lo1, hi1 = _unpack_rows(y1_ref[...])
    y = jnp.concatenate([g0 * lo0 + g1 * lo1, g0 * hi0 + g1 * hi1], axis=1)
    o_ref[...] = _layer_norm(ALPHA * h_ref[...] + y, g_ref[...], b_ref[...])


def _combine(ys, h, info, ln_g, ln_b, rows, part, n_parts):
    n = h.shape[0]
    steps = n // n_parts // rows
    off = part * steps
    const = lambda i: (0, 0)
    return pl.pallas_call(
        _combine_kernel,
        grid=(steps,),
        in_specs=[pl.BlockSpec((rows, PACKED), lambda i: (i, 0)),
                  pl.BlockSpec((rows, PACKED), lambda i: (i + steps, 0)),
                  pl.BlockSpec((rows, D_MODEL), lambda i: (i + off, 0)),
                  pl.BlockSpec((SUBLANES, rows), lambda i: (0, i + off)),
                  pl.BlockSpec((1, D_MODEL), const),
                  pl.BlockSpec((1, D_MODEL), const)],
        out_specs=pl.BlockSpec((rows, D_MODEL), lambda i: (i + off, 0)),
        out_shape=jax.ShapeDtypeStruct((n, D_MODEL), F32),
        input_output_aliases={2: 0},
        compiler_params=_cparams(1),
        name="combine",
    )(ys, ys, h, info, ln_g, ln_b)


def _alibi_bias():
    qi = np.arange(BLOCK)[:, None]
    kj = np.arange(2 * BLOCK)[None, :]
    dist = qi - kj + BLOCK
    band = (dist >= 0) & (dist < BLOCK)
    slopes = np.exp2(-8.0 * np.arange(1, N_Q_HEADS + 1, dtype=np.float32) / N_Q_HEADS)
    bias = np.where(band[None], -slopes[:, None, None] * dist[None].astype(np.float32), NEG)
    bias = bias * LOG2E
    first = np.where((kj >= PAD_FRONT)[None], bias, NEG)
    out = np.empty((2, N_KV_HEADS, 2 * BLOCK, 4 * BLOCK), np.float32)
    for v, per_head in enumerate((first, bias)):
        for j in range(N_KV_HEADS):
            out[v, j] = np.block([[per_head[4 * j], per_head[4 * j + 1]],
                                  [per_head[4 * j + 2], per_head[4 * j + 3]]])
    return jnp.asarray(out, F32)


def kernel(x, meta_tokens, w_in, conv_w, conv_b, lru_wa, lru_ba, lru_wx, lru_bx, lru_lambda,
           attn_sinks, g_attn, g_lru, w_out, ln1_g, ln1_b, w_group, b_group, w_router,
           b_router, w_gate, w_up, w_down, ln2_g, ln2_b):
    bsz, seq, d = x.shape
    nbx = seq // BLOCK
    n_tok = bsz * seq
    x2d = x.reshape(n_tok, d)
    row = lambda v: v[0:1].reshape(1, -1).astype(F32)

    q_scale = jnp.concatenate([jnp.full((ATTN_WIDTH,), LOG2E * HEAD_DIM ** -0.5, F32),
                               jnp.ones((IN_COLS - ATTN_WIDTH,), F32)])
    w_in_b = (w_in[0] * q_scale).astype(BF16)
    q, kv, xr, yr = _in_proj(x2d, w_in_b, PROJ_ROWS)
    qm, kvm, xrm, yrm = _in_proj(meta_tokens.astype(F32), w_in_b, N_META, front=PAD_FRONT)
    shp = lambda a: a.reshape(bsz, seq, a.shape[-1])

    attn_n = _attention(attn_sinks[0].astype(F32), shp(q), shp(kv), kvm, _alibi_bias(),
                        row(g_attn), bsz, nbx)
    lru_n = _rglru(shp(xr), shp(yr), xrm, yrm, conv_w[0].astype(F32), row(conv_b),
                   lru_wa.astype(F32), lru_wx.astype(F32), lru_ba.astype(F32),
                   lru_bx.astype(F32), row(lru_lambda), row(g_lru), bsz, nbx)

    gpad = SUBLANES - N_GROUPS
    tail = ROUTER_ROWS - SUBLANES - N_EXPERTS
    w_rt = jnp.concatenate(
        [w_group[0].T, jnp.zeros((gpad, d), F32),
         jnp.transpose(w_router[0], (0, 2, 1)).reshape(N_EXPERTS, d),
         jnp.zeros((tail, d), F32)], axis=0).astype(F32)
    b_rt = jnp.concatenate([b_group[0], jnp.zeros((gpad,), F32), b_router[0].reshape(-1),
                            jnp.zeros((tail,), F32)]).astype(F32).reshape(ROUTER_ROWS, 1)
    h1, hp, info, cnt, dest = _out_proj(
        attn_n.reshape(n_tok, ATTN_WIDTH), lru_n.reshape(n_tok, LRU_WIDTH), x2d,
        w_out[0].astype(F32), row(ln1_g), row(ln1_b), w_rt, b_rt, OUT_PROJ_ROWS, ROUTE_ROWS)

    n_slots = n_tok * TOP_K
    n_blocks = n_slots // MOE_BLOCK + N_EXPERTS
    cap = n_blocks * MOE_BLOCK
    nblk = (cnt[:, 0].astype(I32) + MOE_BLOCK - 1) // MOE_BLOCK
    bends = jnp.cumsum(nblk)
    bstart = (bends - nblk).astype(I32)
    nused = bends[-1:].astype(I32)

    xs = _sc_scatter_rows(hp, dest, cap)
    yb = _experts(bstart, nblk.astype(I32), nused, xs, w_gate[0], w_up[0], w_down[0])
    out = h1
    part_len = n_tok // COMBINE_PARTS
    for part in range(COMBINE_PARTS):
        ys = _sc_gather_rows(yb, dest, part * part_len, part_len)
        out = _combine(ys, out, info, row(ln2_g), row(ln2_b), COMBINE_ROWS,
                       part, COMBINE_PARTS)
    return out.reshape(bsz, seq, d)
```

```python
import jax
import jax.numpy as jnp
import numpy as np
from jax import lax
from jax.experimental import pallas as pl
from jax.experimental.pallas import tpu as pltpu
from jax.experimental.pallas import tpu_sc as plsc

F32 = jnp.float32
BF16 = jnp.bfloat16
U32 = jnp.uint32
I32 = jnp.int32

D_MODEL = 1024
N_META = 16
BLOCK = 128
PAD_FRONT = BLOCK - N_META
HEAD_DIM = 64
ATTN_WIDTH = 512
LRU_WIDTH = 512
N_Q_HEADS = 8
N_KV_HEADS = 2
KV_WIDTH = N_KV_HEADS * HEAD_DIM
LRU_BLOCKS = 8
CONV_W = 4
LRU_C = 8.0
IN_COLS = ATTN_WIDTH + 2 * KV_WIDTH + 2 * LRU_WIDTH
N_GROUPS = 4
EXPERTS_PER_GROUP = 8
N_EXPERTS = N_GROUPS * EXPERTS_PER_GROUP
TOP_K = 2
D_FF = 512
MOE_BLOCK = 256
ALPHA = 2.0 ** 0.25
EPS = 1e-5
NEG = -1e30
LOG2E = float(np.log2(np.e))
LANES = 128
SUBLANES = 8
PACKED = D_MODEL // 2

PROJ_ROWS = 1024
OUT_PROJ_ROWS = 1024
ROUTE_ROWS = 512
COMBINE_ROWS = 1024
COMBINE_PARTS = 4
X_GROUP = 4
X_AHEAD = 4
X_RING = X_AHEAD + X_GROUP
Y_RING = 2 * X_GROUP
VMEM_LIMIT = 48 * 1024 * 1024

SC_CORES = 2
SC_SUBCORES = 16
SC_WORKERS = SC_CORES * SC_SUBCORES
SC_WINDOW = 64


def _cparams(n_axes):
    return pltpu.CompilerParams(
        dimension_semantics=("arbitrary",) * n_axes, vmem_limit_bytes=VMEM_LIMIT)


def _in_proj_kernel(x_ref, w_ref, q_ref, kv_ref, xr_ref, yr_ref):
    proj = jnp.dot(x_ref[...].astype(BF16), w_ref[...], preferred_element_type=F32)
    front = q_ref.shape[0] - x_ref.shape[0]
    o = 0
    for ref, width in ((q_ref, ATTN_WIDTH), (kv_ref, 2 * KV_WIDTH),
                       (xr_ref, LRU_WIDTH), (yr_ref, LRU_WIDTH)):
        vals = proj[:, o:o + width].astype(ref.dtype)
        if front:
            vals = jnp.concatenate([jnp.zeros((front, width), ref.dtype), vals], axis=0)
        ref[...] = vals
        o += width


def _in_proj(x2d, w_bf16, rows, front=0):
    steps = x2d.shape[0] // rows
    widths = (ATTN_WIDTH, 2 * KV_WIDTH, LRU_WIDTH, LRU_WIDTH)
    return pl.pallas_call(
        _in_proj_kernel,
        grid=(steps,),
        in_specs=[pl.BlockSpec((rows, D_MODEL), lambda i: (i, 0)),
                  pl.BlockSpec((D_MODEL, IN_COLS), lambda i: (0, 0))],
        out_specs=[pl.BlockSpec((front + rows, w), lambda i: (i, 0)) for w in widths],
        out_shape=[jax.ShapeDtypeStruct((steps * (front + rows), w), BF16) for w in widths],
        compiler_params=_cparams(1),
        name="in_proj",
    )(x2d, w_bf16)


def _attn_kernel(sinks_ref, q_ref, kv_ref, kvm_ref, bias_ref, g_ref, o_ref,
                 klo, khi, vlo, vhi):
    nbx = q_ref.shape[0] // BLOCK
    lo_lanes = lax.broadcasted_iota(I32, (BLOCK // 2, LANES), 1) < HEAD_DIM

    def layout_block(n, blk):
        rows = pl.ds(pl.multiple_of(n * BLOCK, BLOCK), BLOCK)
        as_bf16 = lambda words: pltpu.bitcast(words, BF16)
        for src, dst_lo, dst_hi in ((blk[:, :KV_WIDTH], klo, khi), (blk[:, KV_WIDTH:], vlo, vhi)):
            w = pltpu.bitcast(src, U32)
            r = pltpu.roll(w, HEAD_DIM, axis=1)
            zero = jnp.zeros_like(w)
            dst_lo[0, rows, :] = as_bf16(jnp.where(lo_lanes, w, zero))
            dst_hi[0, rows, :] = as_bf16(jnp.where(lo_lanes, zero, r))
            dst_lo[1, rows, :] = as_bf16(jnp.where(lo_lanes, r, zero))
            dst_hi[1, rows, :] = as_bf16(jnp.where(lo_lanes, zero, w))

    layout_block(0, kvm_ref[...])

    def layout_body(n, carry):
        layout_block(n + 1, kv_ref[pl.ds(pl.multiple_of(n * BLOCK, BLOCK), BLOCK), :])
        return carry

    lax.fori_loop(0, nbx, layout_body, 0, unroll=4)

    ones_lo = jnp.where(lax.broadcasted_iota(I32, (2 * BLOCK, LANES), 1) < HEAD_DIM,
                        1.0, 0.0).astype(BF16)
    ones_hi = (1.0 - ones_lo.astype(F32)).astype(BF16)
    top_rows = lax.broadcasted_iota(I32, (2 * BLOCK, 1), 0) < BLOCK
    lo_half = lax.broadcasted_iota(I32, (2 * BLOCK, LANES), 1) < HEAD_DIM
    sinks = [sinks_ref[h] * LOG2E for h in range(N_Q_HEADS)]

    def block(i, carry):
        q_rows = pl.ds(pl.multiple_of(i * BLOCK, BLOCK), BLOCK)
        win = pl.ds(pl.multiple_of(i * BLOCK, BLOCK), 2 * BLOCK)
        q = q_ref[q_rows, :]
        first = jnp.minimum(i, 1)
        outs = []
        for j in range(N_KV_HEADS):
            q2 = jnp.concatenate([q[:, (2 * j) * LANES:(2 * j + 1) * LANES],
                                  q[:, (2 * j + 1) * LANES:(2 * j + 2) * LANES]], axis=0)
            kc = jnp.concatenate([klo[j, win, :], khi[j, win, :]], axis=0)
            s = lax.dot_general(q2, kc, (((1,), (1,)), ((), ())), preferred_element_type=F32)
            s = s + bias_ref[first, j]
            ps, es = [], []
            for c in range(2):
                sink = jnp.where(top_rows, sinks[4 * j + c], sinks[4 * j + 2 + c])
                sc = s[:, c * 2 * BLOCK:(c + 1) * 2 * BLOCK]
                m = jnp.maximum(jnp.max(sc, axis=1, keepdims=True), sink)
                ps.append(jnp.exp2(sc - m).astype(BF16))
                es.append(jnp.exp2(sink - m))
            v_lo = jnp.concatenate([vlo[j, win, :], ones_lo], axis=1)
            v_hi = jnp.concatenate([vhi[j, win, :], ones_hi], axis=1)
            r = jnp.dot(jnp.concatenate(ps, axis=1), jnp.concatenate([v_lo, v_hi], axis=0),
                        preferred_element_type=F32)
            den = r[:, LANES:] + jnp.where(lo_half, es[0], es[1])
            o2 = r[:, :LANES] * (1.0 / den)
            outs += [o2[:BLOCK], o2[BLOCK:]]
        out = jnp.concatenate(outs, axis=1)
        ms = jnp.mean(out * out, axis=1, keepdims=True)
        o_ref[q_rows, :] = (out * lax.rsqrt(ms + EPS) * g_ref[...]).astype(o_ref.dtype)
        return carry

    lax.fori_loop(0, nbx, block, 0, unroll=8)


def _attention(sinks, q, kv, kvm, bias, g_attn, bsz, nbx):
    seq = nbx * BLOCK
    const2 = lambda b: (0, 0)
    kv_scratch = pltpu.VMEM((N_KV_HEADS, seq + BLOCK, LANES), BF16)
    return pl.pallas_call(
        _attn_kernel,
        grid=(bsz,),
        in_specs=[pl.BlockSpec(memory_space=pltpu.SMEM),
                  pl.BlockSpec((None, seq, ATTN_WIDTH), lambda b: (b, 0, 0)),
                  pl.BlockSpec((None, seq, 2 * KV_WIDTH), lambda b: (b, 0, 0)),
                  pl.BlockSpec((BLOCK, 2 * KV_WIDTH), const2),
                  pl.BlockSpec((2, N_KV_HEADS, 2 * BLOCK, 4 * BLOCK), lambda b: (0, 0, 0, 0)),
                  pl.BlockSpec((1, ATTN_WIDTH), const2)],
        out_specs=pl.BlockSpec((None, seq, ATTN_WIDTH), lambda b: (b, 0, 0)),
        out_shape=jax.ShapeDtypeStruct((bsz, seq, ATTN_WIDTH), BF16),
        scratch_shapes=[kv_scratch, kv_scratch, kv_scratch, kv_scratch],
        compiler_params=_cparams(1),
        name="attention",
    )(sinks, q, kv, kvm, bias, g_attn)


def _sigmoid(v):
    return 0.5 * jnp.tanh(0.5 * v) + 0.5


def _gelu_tanh(y):
    c = float(np.sqrt(2.0 / np.pi))
    half = 0.5 * y
    return half + half * jnp.tanh(y * (c + (c * 0.044715) * (y * y)))


LRU_CHUNK = 44
LRU_SEG = SUBLANES * LRU_CHUNK
LRU_SLABS = LRU_WIDTH // LANES


def _lru_kernel(xr_ref, yr_ref, xrm_ref, yrm_ref, cw_ref, cb_ref, wa_ref, wx_ref, ba_ref,
                bx_ref, lam_ref, g_ref, o_ref, x_st, y_st, o_st, s_st, xtail, hcar, wg_ref):
    seq = xr_ref.shape[0]
    n_seg = (seq + BLOCK) // LRU_SEG

    @pl.when(pl.program_id(0) == 0)
    def _():
        c = wa_ref.shape[1]
        per = LANES // c
        zero = jnp.zeros((c, c), F32)
        for s in range(LRU_SLABS):
            rows = [jnp.concatenate([0.5 * w_ref[s * per + p] if q == p else zero
                                     for w_ref in (wa_ref, wx_ref) for q in range(per)], axis=1)
                    for p in range(per)]
            wg_ref[s] = jnp.concatenate(rows, axis=0).astype(BF16)

    xtail[...] = jnp.zeros_like(xtail)
    hcar[...] = jnp.zeros_like(hcar)
    lam = lam_ref[...]
    softplus_neg = jnp.maximum(-lam, 0.0) + jnp.log(1.0 + jnp.exp(-jnp.abs(lam)))
    sub = lax.broadcasted_iota(jnp.int32, (SUBLANES, LANES), 0)

    def half_bias(b_ref):
        per = LANES // b_ref.shape[1]
        return [0.5 * jnp.concatenate([b_ref[s * per + p:s * per + p + 1, :] for p in range(per)],
                                      axis=1) for s in range(LRU_SLABS)]

    half_ba, half_bx = half_bias(ba_ref), half_bias(bx_ref)

    def strided(j):
        return pl.ds(j, SUBLANES, stride=LRU_CHUNK)

    def piece(v, j):
        return v[j * SUBLANES:(j + 1) * SUBLANES, :]

    def segment(k, first):
        if first:
            head = LRU_SEG - BLOCK
            x_nat = jnp.concatenate([xrm_ref[...], xr_ref[0:head, :]], axis=0).astype(F32)
            y_nat = jnp.concatenate([yrm_ref[...], yr_ref[0:head, :]], axis=0).astype(F32)
        else:
            rows = pl.ds(pl.multiple_of(k * LRU_SEG - BLOCK, 2 * SUBLANES), LRU_SEG)
            x_nat = xr_ref[rows, :].astype(F32)
            y_nat = yr_ref[rows, :].astype(F32)
        for c in range(LRU_SLABS):
            x_st[c] = x_nat[:, c * LANES:(c + 1) * LANES]
            y_st[c] = y_nat[:, c * LANES:(c + 1) * LANES]
        first_row = k * LRU_SEG + LRU_CHUNK * sub
        sumsq = [jnp.zeros((SUBLANES, LANES), F32) for _ in range(LRU_CHUNK)]

        for c in range(LRU_SLABS):
            lanes = slice(c * LANES, (c + 1) * LANES)
            x = [x_st[c, strided(j), :] for j in range(LRU_CHUNK)]
            before = []
            for d in range(1, CONV_W):
                from_prev_chunk = pltpu.roll(x[LRU_CHUNK - d], 1, axis=0)
                before.append(jnp.where(sub == 0, xtail[d - 1:d, lanes], from_prev_chunk))
            for d in range(1, CONV_W):
                xtail[d - 1:d, lanes] = x[LRU_CHUNK - d][SUBLANES - 1:SUBLANES, :]

            def x_at(j):
                return x[j] if j >= 0 else before[-j - 1]

            taps = [cw_ref[t:t + 1, lanes] for t in range(CONV_W)]
            bias = cb_ref[:, lanes]
            xc = jnp.concatenate(
                [bias + sum(taps[t] * x_at(j - (CONV_W - 1) + t) for t in range(CONV_W))
                 for j in range(LRU_CHUNK)], axis=0)
            xcb = xc.astype(BF16)
            both = jnp.dot(xcb, wg_ref[c], preferred_element_type=F32)
            tr = jnp.tanh(both[:, :LANES] + half_ba[c])
            ti = jnp.tanh(both[:, LANES:] + half_bx[c])
            log_a_half = (-0.5 * LRU_C) * softplus_neg[:, lanes]
            a = jnp.exp(log_a_half * tr + log_a_half)
            half_xc = 0.5 * xc
            gated_x = half_xc * ti + half_xc
            z = 1.0 - a * a
            u = jnp.where(z > 0.0, z * lax.rsqrt(z), 0.0) * gated_x

            h = jnp.zeros((SUBLANES, LANES), F32)
            p = jnp.ones((SUBLANES, LANES), F32)
            hs, ps = [], []
            for j in range(LRU_CHUNK):
                aj = piece(a, j)
                uj = piece(u, j)
                if first:
                    uj = jnp.where(first_row + j >= PAD_FRONT, uj, 0.0)
                h = aj * h + uj
                p = aj * p
                hs.append(h)
                ps.append(p)
            entry = [hcar[:, lanes]]
            for s in range(SUBLANES):
                entry.append(h[s:s + 1, :] + p[s:s + 1, :] * entry[s])
            hcar[:, lanes] = entry[SUBLANES]
            entry_rows = jnp.concatenate(entry[:SUBLANES], axis=0)

            for j in range(LRU_CHUNK):
                state = hs[j] + ps[j] * entry_rows
                out = state * _gelu_tanh(y_st[c, strided(j), :])
                sumsq[j] = sumsq[j] + out * out
                o_st[c, strided(j), :] = out

        for j in range(LRU_CHUNK):
            ms = jnp.sum(sumsq[j], axis=1, keepdims=True) * (1.0 / LRU_WIDTH)
            s_st[strided(j), :] = jnp.broadcast_to(lax.rsqrt(ms + EPS), (SUBLANES, LANES))
        scale = s_st[...]
        for c in range(LRU_SLABS):
            lanes = slice(c * LANES, (c + 1) * LANES)
            normed = (o_st[c] * scale * g_ref[:, lanes]).astype(o_ref.dtype)
            if first:
                o_ref[0:LRU_SEG - BLOCK, lanes] = normed[BLOCK:, :]
            else:
                o_ref[rows, lanes] = normed

    assert BLOCK <= LRU_SEG
    segment(0, True)

    def later_segment(k, carry):
        segment(k, False)
        return carry

    lax.fori_loop(1, n_seg, later_segment, 0)


def _rglru(xr, yr, xrm, yrm, cw, cb, wa, wx, ba, bx, lam, g_lru, bsz, nbx):
    seq = nbx * BLOCK
    assert (seq + BLOCK) % LRU_SEG == 0
    main = pl.BlockSpec((None, seq, LRU_WIDTH), lambda b: (b, 0, 0))
    const2 = lambda b: (0, 0)
    row_spec = pl.BlockSpec((1, LRU_WIDTH), const2)
    gate_spec = pl.BlockSpec((None,) + wa.shape[1:], lambda b: (0, 0, 0, 0))
    bias_spec = pl.BlockSpec((None,) + ba.shape[1:], lambda b: (0, 0, 0))
    slabs = pltpu.VMEM((LRU_SLABS, LRU_SEG, LANES), F32)
    return pl.pallas_call(
        _lru_kernel,
        grid=(bsz,),
        in_specs=[main, main,
                  pl.BlockSpec((BLOCK, LRU_WIDTH), const2),
                  pl.BlockSpec((BLOCK, LRU_WIDTH), const2),
                  pl.BlockSpec((CONV_W, LRU_WIDTH), const2),
                  row_spec, gate_spec, gate_spec, bias_spec, bias_spec,
                  row_spec, row_spec],
        out_specs=main,
        out_shape=jax.ShapeDtypeStruct((bsz, seq, LRU_WIDTH), BF16),
        scratch_shapes=[slabs, slabs, slabs,
                        pltpu.VMEM((LRU_SEG, LANES), F32),
                        pltpu.VMEM((SUBLANES, LRU_WIDTH), F32),
                        pltpu.VMEM((1, LRU_WIDTH), F32),
                        pltpu.VMEM((LRU_SLABS, LANES, 2 * LANES), BF16)],
        compiler_params=_cparams(1),
        name="rglru",
    )(xr, yr, xrm, yrm, cw, cb, wa, wx, ba, bx, lam, g_lru)


def _pack_rows(v):
    bits = lax.bitcast_convert_type(v.astype(BF16).astype(F32), U32)
    return (bits[:, :PACKED] >> 16) | (bits[:, PACKED:] & jnp.uint32(0xFFFF0000))


def _unpack_rows(w):
    lo = lax.bitcast_convert_type(w << 16, F32)
    hi = lax.bitcast_convert_type(w & jnp.uint32(0xFFFF0000), F32)
    return lo, hi


def _layer_norm(z, g, b):
    mu = jnp.mean(z, axis=1, keepdims=True)
    zc = z - mu
    var = jnp.mean(zc * zc, axis=1, keepdims=True)
    return zc * lax.rsqrt(var + EPS) * g + b


def _out_proj_kernel(a_ref, l_ref, x_ref, w_ref, g_ref, b_ref, wrt_ref, brt_ref, tri_ref,
                     h_ref, hp_ref, info_ref, cnt_ref, dest_ref, w_b, wrt_b):
    @pl.when(pl.program_id(0) == 0)
    def _():
        cnt_ref[...] = jnp.zeros_like(cnt_ref)
        w_b[...] = w_ref[...].astype(BF16)
        hi = wrt_ref[...].astype(BF16)
        wrt_b[0:ROUTER_ROWS, :] = hi
        wrt_b[ROUTER_ROWS:, :] = (wrt_ref[...] - hi.astype(F32)).astype(BF16)

    mix = jnp.dot(jnp.concatenate([a_ref[...], l_ref[...]], axis=1), w_b[...],
                  preferred_element_type=F32)
    h = _layer_norm(ALPHA * x_ref[...] + mix, g_ref[...], b_ref[...])
    h_ref[...] = h
    hp_ref[...] = _pack_rows(h)

    h_hi = h.astype(BF16)
    h_lo = (h - h_hi.astype(F32)).astype(BF16)
    nt = (((1,), (1,)), ((), ()))
    both = lax.dot_general(wrt_b[...], h_hi, nt, preferred_element_type=F32)
    lg = (both[:ROUTER_ROWS] + both[ROUTER_ROWS:]
          + lax.dot_general(wrt_b[0:ROUTER_ROWS, :], h_lo, nt, preferred_element_type=F32)
          ) + brt_ref[...]
    tile_shape = (SUBLANES, h.shape[0])
    sub = lax.broadcasted_iota(I32, tile_shape, 0)
    ninf = -jnp.inf
    t0 = lg[0:SUBLANES]
    gl = jnp.where(sub < N_GROUPS, t0, ninf)
    gmax = jnp.max(gl, axis=0, keepdims=True)
    g_idx = jnp.min(jnp.where(gl == gmax, sub, SUBLANES), axis=0, keepdims=True)
    g_w = 1.0 / jnp.sum(jnp.where(sub < N_GROUPS, jnp.exp(t0 - gmax), 0.0),
                        axis=0, keepdims=True)
    el = lg[SUBLANES:2 * SUBLANES]
    for g in range(1, N_GROUPS):
        el = jnp.where(g_idx == g, lg[(g + 1) * SUBLANES:(g + 2) * SUBLANES], el)
    v1 = jnp.max(el, axis=0, keepdims=True)
    i1 = jnp.min(jnp.where(el == v1, sub, SUBLANES), axis=0, keepdims=True)
    el2 = jnp.where(sub == i1, ninf, el)
    v2 = jnp.max(el2, axis=0, keepdims=True)
    i2 = jnp.min(jnp.where(el2 == v2, sub, SUBLANES), axis=0, keepdims=True)
    t = jnp.exp(v2 - v1)
    w1 = 1.0 / (1.0 + t)
    w2 = t * w1
    e_base = g_idx * EXPERTS_PER_GROUP
    e1 = e_base + i1
    e2 = e_base + i2
    step = pl.program_id(0)
    rows = h.shape[0]
    info_ref[:, pl.ds(pl.multiple_of(step * rows, rows), rows)] = jnp.where(
        sub == 0, e1.astype(F32),
        jnp.where(sub == 1, e2.astype(F32),
                  jnp.where(sub == 2, g_w * w1, jnp.where(sub == 3, g_w * w2, 0.0))))
    expert = lax.broadcasted_iota(I32, (N_EXPERTS, rows), 0)
    chosen = (expert == e1).astype(F32) + (expert == e2).astype(F32)
    cnt_ref[...] += jnp.sum(chosen, axis=1, keepdims=True)

    @pl.when(step == pl.num_programs(0) - 1)
    def _():
        _route_slots(info_ref, cnt_ref, tri_ref, dest_ref)


def _route_slots(info_ref, cnt_ref, tri_ref, dest_ref):
    cols = tri_ref.shape[0]
    c = cnt_ref[...].astype(I32)
    padded = ((c + (MOE_BLOCK - 1)) // MOE_BLOCK) * MOE_BLOCK
    e = lax.broadcasted_iota(I32, (N_EXPERTS, LANES), 0)
    scan = padded
    for d in (1, 2, 4, 8, 16):
        scan = scan + jnp.where(e >= d, pltpu.roll(scan, d, axis=0), 0)
    pstart = (scan - padded)[:, 0:1].astype(F32)
    expert = lax.broadcasted_iota(I32, (N_EXPERTS, cols), 0)

    def tile(t, seen):
        win = pl.ds(pl.multiple_of(t * cols, cols), cols)
        oh1 = (expert == info_ref[0:1, win].astype(I32)).astype(F32)
        oh2 = (expert == info_ref[1:2, win].astype(I32)).astype(F32)
        both = oh1 + oh2
        before = jnp.dot(both.astype(BF16), tri_ref[...], preferred_element_type=F32)
        row_of = before + (seen + pstart)
        r1 = jnp.sum(oh1 * row_of, axis=0, keepdims=True)
        r2 = jnp.sum(oh2 * row_of, axis=0, keepdims=True)
        dest_ref[:, win] = jnp.concatenate([r1, r2], axis=0).astype(I32)
        return seen + jnp.sum(both, axis=1, keepdims=True)

    lax.fori_loop(0, info_ref.shape[1] // cols, tile, jnp.zeros((N_EXPERTS, 1), F32), unroll=4)


ROUTER_ROWS = -(-(N_GROUPS + 1) * SUBLANES // (2 * SUBLANES)) * (2 * SUBLANES)


def _out_proj(attn_n, lru_n, x2d, w_out, ln_g, ln_b, w_rt, b_rt, rows, route_cols):
    n = x2d.shape[0]
    const = lambda i: (0, 0)
    tile = lambda w: pl.BlockSpec((rows, w), lambda i: (i, 0))
    tri = jnp.asarray(np.triu(np.ones((route_cols, route_cols), np.float32), 1), BF16)
    return pl.pallas_call(
        _out_proj_kernel,
        grid=(n // rows,),
        in_specs=[tile(ATTN_WIDTH), tile(LRU_WIDTH), tile(D_MODEL),
                  pl.BlockSpec((D_MODEL, D_MODEL), const),
                  pl.BlockSpec((1, D_MODEL), const),
                  pl.BlockSpec((1, D_MODEL), const),
                  pl.BlockSpec((ROUTER_ROWS, D_MODEL), const),
                  pl.BlockSpec((ROUTER_ROWS, 1), const),
                  pl.BlockSpec((route_cols, route_cols), const)],
        out_specs=[tile(D_MODEL), tile(PACKED),
                   pl.BlockSpec((SUBLANES, n), const),
                   pl.BlockSpec((N_EXPERTS, LANES), const),
                   pl.BlockSpec((TOP_K, n), const)],
        out_shape=[jax.ShapeDtypeStruct((n, D_MODEL), F32),
                   jax.ShapeDtypeStruct((n, PACKED), U32),
                   jax.ShapeDtypeStruct((SUBLANES, n), F32),
                   jax.ShapeDtypeStruct((N_EXPERTS, LANES), F32),
                   jax.ShapeDtypeStruct((TOP_K, n), I32)],
        scratch_shapes=[pltpu.VMEM((D_MODEL, D_MODEL), BF16),
                        pltpu.VMEM((2 * ROUTER_ROWS, D_MODEL), BF16)],
        compiler_params=_cparams(1),
        name="out_proj",
    )(attn_n, lru_n, x2d, w_out, ln_g, ln_b, w_rt, b_rt, tri)


def _sc_mesh():
    return plsc.VectorSubcoreMesh(core_axis_name="core", subcore_axis_name="subcore")


def _sc_worker_id():
    return lax.axis_index("subcore") * SC_CORES + lax.axis_index("core")


def _sc_scatter_rows(rows, dest, cap):
    n, width = rows.shape
    per_worker = n // SC_WORKERS
    n_win = per_worker // SC_WINDOW

    def body(x_hbm, d_hbm, o_hbm, i0_v, i1_v, rows_v, isem, rsem, sem0, sem1):
        wid = _sc_worker_id()

        def indices(j):
            win = pl.ds(wid * per_worker + j * SC_WINDOW, SC_WINDOW)
            return (pltpu.make_async_copy(d_hbm.at[0, win], i0_v.at[j], isem.at[0, j]),
                    pltpu.make_async_copy(d_hbm.at[1, win], i1_v.at[j], isem.at[1, j]))

        def read(j):
            src = x_hbm.at[pl.ds(wid * per_worker + j * SC_WINDOW, SC_WINDOW)]
            return pltpu.make_async_copy(src, rows_v.at[j % 2], rsem.at[j % 2])

        def scatters(j):
            return (pltpu.make_async_copy(rows_v.at[j % 2], o_hbm.at[i0_v.at[j]], sem0.at[j % 2]),
                    pltpu.make_async_copy(rows_v.at[j % 2], o_hbm.at[i1_v.at[j]], sem1.at[j % 2]))

        read(0).start()
        for j in range(n_win):
            for cp in indices(j):
                cp.start()
        for j in range(n_win):
            for cp in indices(j):
                cp.wait()
        for j in range(n_win):
            if j + 1 < n_win:
                if j >= 1:
                    for cp in scatters(j - 1):
                        cp.wait()
                read(j + 1).start()
            read(j).wait()
            for cp in scatters(j):
                cp.start()
        for j in range(max(n_win - 2, 0), n_win):
            for cp in scatters(j):
                cp.wait()

    return pl.kernel(
        body,
        out_type=jax.ShapeDtypeStruct((cap, width), rows.dtype),
        mesh=_sc_mesh(),
        scratch_types=[pltpu.VMEM((n_win, SC_WINDOW), I32), pltpu.VMEM((n_win, SC_WINDOW), I32),
                       pltpu.VMEM((2, SC_WINDOW, width), rows.dtype),
                       pltpu.SemaphoreType.DMA((TOP_K, n_win)),
                       pltpu.SemaphoreType.DMA((2,)), pltpu.SemaphoreType.DMA((2,)),
                       pltpu.SemaphoreType.DMA((2,))],
        name="dispatch",
    )(rows, dest)


def _sc_gather_rows(table, dest, start, length):
    width = table.shape[1]
    per_worker = TOP_K * length // SC_WORKERS
    n_win = per_worker // SC_WINDOW
    workers_per_choice = SC_WORKERS // TOP_K

    def body(y_hbm, d_hbm, o_hbm, i_v, rows_v, isem, gsem, wsem):
        wid = _sc_worker_id()
        choice = wid // workers_per_choice
        first = start + (wid % workers_per_choice) * per_worker

        def indices(j):
            win = pl.ds(first + j * SC_WINDOW, SC_WINDOW)
            return pltpu.make_async_copy(d_hbm.at[choice, win], i_v.at[j], isem.at[j])

        for j in range(n_win):
            indices(j).start()
        for j in range(n_win):
            indices(j).wait()

        def gather(j):
            return pltpu.make_async_copy(y_hbm.at[i_v.at[j]], rows_v.at[j % 2], gsem.at[j % 2])

        def write(j):
            dst = o_hbm.at[pl.ds(wid * per_worker + j * SC_WINDOW, SC_WINDOW)]
            return pltpu.make_async_copy(rows_v.at[j % 2], dst, wsem.at[j % 2])

        gather(0).start()
        for j in range(n_win):
            if j + 1 < n_win:
                if j >= 1:
                    write(j - 1).wait()
                gather(j + 1).start()
            gather(j).wait()
            write(j).start()
        for j in range(max(n_win - 2, 0), n_win):
            write(j).wait()

    return pl.kernel(
        body,
        out_type=jax.ShapeDtypeStruct((SC_WORKERS * per_worker, width), table.dtype),
        mesh=_sc_mesh(),
        scratch_types=[pltpu.VMEM((n_win, SC_WINDOW), I32),
                       pltpu.VMEM((2, SC_WINDOW, width), table.dtype),
                       pltpu.SemaphoreType.DMA((n_win,)),
                       pltpu.SemaphoreType.DMA((2,)), pltpu.SemaphoreType.DMA((2,))],
        name="collect",
    )(table, dest)


def _expert_kernel(bstart_ref, nblk_ref, nused_ref, xs_hbm, wg_ref, wu_ref, wd_ref, yb_hbm,
                   xbuf, ybuf, zbuf, xsem, ysem, zsem, wg_b, wu_b, wd_b):
    e = pl.program_id(0)
    nused = nused_ref[0]
    n_blocks = yb_hbm.shape[0] // MOE_BLOCK

    def rows(b):
        return pl.ds(pl.multiple_of(b * MOE_BLOCK, MOE_BLOCK), MOE_BLOCK)

    def x_copy(b):
        slot = b % X_RING
        return pltpu.make_async_copy(xs_hbm.at[rows(b)], xbuf.at[slot], xsem.at[slot])

    def y_copy(b):
        slot = b % Y_RING
        return pltpu.make_async_copy(ybuf.at[slot], yb_hbm.at[rows(b)], ysem.at[slot])

    @pl.when(e == 0)
    def _():
        for b in range(X_AHEAD):
            @pl.when(b < nused)
            def _():
                x_copy(b).start()

    wg_b[...] = wg_ref[...].astype(BF16)
    wu_b[...] = wu_ref[...].astype(BF16)
    wd_b[...] = wd_ref[...].astype(BF16)

    def run_blocks(b, count):
        for k in range(count):
            x_copy(b + k).wait()
        for k in range(count):
            nxt = b + X_AHEAD + k

            @pl.when(nxt < nused)
            def _():
                x_copy(nxt).start()

            @pl.when(b + k >= Y_RING)
            def _():
                y_copy(b + k - Y_RING).wait()

        words = jnp.concatenate([xbuf[(b + k) % X_RING] for k in range(count)], axis=0)
        lo, hi = _unpack_rows(words)
        x = jnp.concatenate([lo.astype(BF16), hi.astype(BF16)], axis=1)
        g = jnp.dot(x, wg_b[...], preferred_element_type=F32)
        u = jnp.dot(x, wu_b[...], preferred_element_type=F32)
        mid = (g * _sigmoid(g) * u).astype(BF16)
        y = _pack_rows(jnp.dot(mid, wd_b[...], preferred_element_type=F32))
        for k in range(count):
            ybuf[(b + k) % Y_RING] = y[k * MOE_BLOCK:(k + 1) * MOE_BLOCK]
            y_copy(b + k).start()

    b0 = bstart_ref[e]
    nb = nblk_ref[e]

    def group(i, carry):
        run_blocks(b0 + X_GROUP * i, X_GROUP)
        return carry

    lax.fori_loop(0, nb // X_GROUP, group, 0)
    done = nb - nb % X_GROUP
    size = X_GROUP // 2
    while size >= 1:
        @pl.when((nb // size) % 2 == 1)
        def _(size=size, done=done):
            run_blocks(b0 + done, size)

        done = done + (nb // size) % 2 * size
        size //= 2

    @pl.when(e == pl.num_programs(0) - 1)
    def _():
        for back in range(Y_RING, 0, -1):
            @pl.when(nused >= back)
            def _():
                y_copy(nused - back).wait()

        zbuf[...] = jnp.zeros_like(zbuf)

        def z_copy(b):
            return pltpu.make_async_copy(zbuf, yb_hbm.at[rows(b)], zsem.at[0])

        def z_start(b, carry):
            z_copy(b).start()
            return carry

        def z_wait(b, carry):
            z_copy(b).wait()
            return carry

        lax.fori_loop(nused, n_blocks, z_start, 0)
        lax.fori_loop(nused, n_blocks, z_wait, 0)


def _experts(bstart, nblk, nused, xs, w_gate, w_up, w_down):
    cap = xs.shape[0]
    w_idx = lambda e, bs, nb, nu: (e, 0, 0)
    grid_spec = pltpu.PrefetchScalarGridSpec(
        num_scalar_prefetch=3,
        grid=(N_EXPERTS,),
        in_specs=[pl.BlockSpec(memory_space=pl.ANY),
                  pl.BlockSpec((None, D_MODEL, D_FF), w_idx),
                  pl.BlockSpec((None, D_MODEL, D_FF), w_idx),
                  pl.BlockSpec((None, D_FF, D_MODEL), w_idx)],
        out_specs=pl.BlockSpec(memory_space=pl.ANY),
        scratch_shapes=[pltpu.VMEM((X_RING, MOE_BLOCK, PACKED), U32),
                        pltpu.VMEM((Y_RING, MOE_BLOCK, PACKED), U32),
                        pltpu.VMEM((MOE_BLOCK, PACKED), U32),
                        pltpu.SemaphoreType.DMA((X_RING,)),
                        pltpu.SemaphoreType.DMA((Y_RING,)),
                        pltpu.SemaphoreType.DMA((1,)),
                        pltpu.VMEM((D_MODEL, D_FF), BF16),
                        pltpu.VMEM((D_MODEL, D_FF), BF16),
                        pltpu.VMEM((D_FF, D_MODEL), BF16)])
    return pl.pallas_call(
        _expert_kernel,
        grid_spec=grid_spec,
        out_shape=jax.ShapeDtypeStruct((cap, PACKED), U32),
        compiler_params=_cparams(1),
        name="experts",
    )(bstart, nblk, nused, xs, w_gate, w_up, w_down)


def _combine_kernel(y0_ref, y1_ref, h_ref, info_ref, g_ref, b_ref, o_ref):
    info = info_ref[...].T
    g0 = info[:, 2:3]
    g1 = info[:, 3:4]
    lo0, hi0 = _unpack_rows(y0_ref[...])
    lo1, hi1 = _unpack_rows(y1_ref[...])
    y = jnp.concatenate([g0 * lo0 + g1 * lo1, g0 * hi0 + g1 * hi1], axis=1)
    o_ref[...] = _layer_norm(ALPHA * h_ref[...] + y, g_ref[...], b_ref[...])


def _combine(ys, h, info, ln_g, ln_b, rows, part, n_parts):
    n = h.shape[0]
    steps = n // n_parts // rows
    off = part * steps
    const = lambda i: (0, 0)
    return pl.pallas_call(
        _combine_kernel,
        grid=(steps,),
        in_specs=[pl.BlockSpec((rows, PACKED), lambda i: (i, 0)),
                  pl.BlockSpec((rows, PACKED), lambda i: (i + steps, 0)),
                  pl.BlockSpec((rows, D_MODEL), lambda i: (i + off, 0)),
                  pl.BlockSpec((SUBLANES, rows), lambda i: (0, i + off)),
                  pl.BlockSpec((1, D_MODEL), const),
                  pl.BlockSpec((1, D_MODEL), const)],
        out_specs=pl.BlockSpec((rows, D_MODEL), lambda i: (i + off, 0)),
        out_shape=jax.ShapeDtypeStruct((n, D_MODEL), F32),
        input_output_aliases={2: 0},
        compiler_params=_cparams(1),
        name="combine",
    )(ys, ys, h, info, ln_g, ln_b)


def _alibi_bias():
    qi = np.arange(BLOCK)[:, None]
    kj = np.arange(2 * BLOCK)[None, :]
    dist = qi - kj + BLOCK
    band = (dist >= 0) & (dist < BLOCK)
    slopes = np.exp2(-8.0 * np.arange(1, N_Q_HEADS + 1, dtype=np.float32) / N_Q_HEADS)
    bias = np.where(band[None], -slopes[:, None, None] * dist[None].astype(np.float32), NEG)
    bias = bias * LOG2E
    first = np.where((kj >= PAD_FRONT)[None], bias, NEG)
    out = np.empty((2, N_KV_HEADS, 2 * BLOCK, 4 * BLOCK), np.float32)
    for v, per_head in enumerate((first, bias)):
        for j in range(N_KV_HEADS):
            out[v, j] = np.block([[per_head[4 * j], per_head[4 * j + 1]],
                                  [per_head[4 * j + 2], per_head[4 * j + 3]]])
    return jnp.asarray(out, F32)


def kernel(x, meta_tokens, w_in, conv_w, conv_b, lru_wa, lru_ba, lru_wx, lru_bx, lru_lambda,
           attn_sinks, g_attn, g_lru, w_out, ln1_g, ln1_b, w_group, b_group, w_router,
           b_router, w_gate, w_up, w_down, ln2_g, ln2_b):
    bsz, seq, d = x.shape
    nbx = seq // BLOCK
    n_tok = bsz * seq
    x2d = x.reshape(n_tok, d)
    row = lambda v: v[0:1].reshape(1, -1).astype(F32)

    q_scale = jnp.concatenate([jnp.full((ATTN_WIDTH,), LOG2E * HEAD_DIM ** -0.5, F32),
                               jnp.ones((IN_COLS - ATTN_WIDTH,), F32)])
    w_in_b = (w_in[0] * q_scale).astype(BF16)
    q, kv, xr, yr = _in_proj(x2d, w_in_b, PROJ_ROWS)
    qm, kvm, xrm, yrm = _in_proj(meta_tokens.astype(F32), w_in_b, N_META, front=PAD_FRONT)
    shp = lambda a: a.reshape(bsz, seq, a.shape[-1])

    attn_n = _attention(attn_sinks[0].astype(F32), shp(q), shp(kv), kvm, _alibi_bias(),
                        row(g_attn), bsz, nbx)
    lru_n = _rglru(shp(xr), shp(yr), xrm, yrm, conv_w[0].astype(F32), row(conv_b),
                   lru_wa.astype(F32), lru_wx.astype(F32), lru_ba.astype(F32),
                   lru_bx.astype(F32), row(lru_lambda), row(g_lru), bsz, nbx)

    gpad = SUBLANES - N_GROUPS
    tail = ROUTER_ROWS - SUBLANES - N_EXPERTS
    w_rt = jnp.concatenate(
        [w_group[0].T, jnp.zeros((gpad, d), F32),
         jnp.transpose(w_router[0], (0, 2, 1)).reshape(N_EXPERTS, d),
         jnp.zeros((tail, d), F32)], axis=0).astype(F32)
    b_rt = jnp.concatenate([b_group[0], jnp.zeros((gpad,), F32), b_router[0].reshape(-1),
                            jnp.zeros((tail,), F32)]).astype(F32).reshape(ROUTER_ROWS, 1)
    h1, hp, info, cnt, dest = _out_proj(
        attn_n.reshape(n_tok, ATTN_WIDTH), lru_n.reshape(n_tok, LRU_WIDTH), x2d,
        w_out[0].astype(F32), row(ln1_g), row(ln1_b), w_rt, b_rt, OUT_PROJ_ROWS, ROUTE_ROWS)

    n_slots = n_tok * TOP_K
    n_blocks = n_slots // MOE_BLOCK + N_EXPERTS
    cap = n_blocks * MOE_BLOCK
    nblk = (cnt[:, 0].astype(I32) + MOE_BLOCK - 1) // MOE_BLOCK
    bends = jnp.cumsum(nblk)
    bstart = (bends - nblk).astype(I32)
    nused = bends[-1:].astype(I32)

    xs = _sc_scatter_rows(hp, dest, cap)
    yb = _experts(bstart, nblk.astype(I32), nused, xs, w_gate[0], w_up[0], w_down[0])
    out = h1
    part_len = n_tok // COMBINE_PARTS
    for part in range(COMBINE_PARTS):
        ys = _sc_gather_rows(yb, dest, part * part_len, part_len)
        out = _combine(ys, out, info, row(ln2_g), row(ln2_b), COMBINE_ROWS,
                       part, COMBINE_PARTS)
    return out.reshape(bsz, seq, d)
```

```python
import jax
import jax.numpy as jnp
import numpy as np
from jax import lax
from jax.experimental import pallas as pl
from jax.experimental.pallas import tpu as pltpu
from jax.experimental.pallas import tpu_sc as plsc

F32 = jnp.float32
BF16 = jnp.bfloat16
U32 = jnp.uint32
I32 = jnp.int32

D_MODEL = 1024
N_META = 16
BLOCK = 128
PAD_FRONT = BLOCK - N_META
HEAD_DIM = 64
ATTN_WIDTH = 512
LRU_WIDTH = 512
N_Q_HEADS = 8
N_KV_HEADS = 2
KV_WIDTH = N_KV_HEADS * HEAD_DIM
LRU_BLOCKS = 8
CONV_W = 4
LRU_C = 8.0
IN_COLS = ATTN_WIDTH + 2 * KV_WIDTH + 2 * LRU_WIDTH
N_GROUPS = 4
EXPERTS_PER_GROUP = 8
N_EXPERTS = N_GROUPS * EXPERTS_PER_GROUP
TOP_K = 2
D_FF = 512
MOE_BLOCK = 256
ALPHA = 2.0 ** 0.25
EPS = 1e-5
NEG = -1e30
LOG2E = float(np.log2(np.e))
LANES = 128
SUBLANES = 8
PACKED = D_MODEL // 2

PROJ_ROWS = 1024
OUT_PROJ_ROWS = 1024
ROUTE_ROWS = 512
COMBINE_ROWS = 1024
COMBINE_PARTS = 4
X_GROUP = 4
X_AHEAD = 4
X_RING = X_AHEAD + X_GROUP
Y_RING = 2 * X_GROUP
VMEM_LIMIT = 48 * 1024 * 1024

SC_CORES = 2
SC_SUBCORES = 16
SC_WORKERS = SC_CORES * SC_SUBCORES
SC_WINDOW = 64


def _cparams(n_axes):
    return pltpu.CompilerParams(
        dimension_semantics=("arbitrary",) * n_axes, vmem_limit_bytes=VMEM_LIMIT)


def _in_proj_kernel(x_ref, w_ref, q_ref, kv_ref, xr_ref, yr_ref):
    proj = jnp.dot(x_ref[...].astype(BF16), w_ref[...], preferred_element_type=F32)
    front = q_ref.shape[0] - x_ref.shape[0]
    o = 0
    for ref, width in ((q_ref, ATTN_WIDTH), (kv_ref, 2 * KV_WIDTH),
                       (xr_ref, LRU_WIDTH), (yr_ref, LRU_WIDTH)):
        vals = proj[:, o:o + width].astype(ref.dtype)
        if front:
            vals = jnp.concatenate([jnp.zeros((front, width), ref.dtype), vals], axis=0)
        ref[...] = vals
        o += width


def _in_proj(x2d, w_bf16, rows, front=0):
    steps = x2d.shape[0] // rows
    widths = (ATTN_WIDTH, 2 * KV_WIDTH, LRU_WIDTH, LRU_WIDTH)
    return pl.pallas_call(
        _in_proj_kernel,
        grid=(steps,),
        in_specs=[pl.BlockSpec((rows, D_MODEL), lambda i: (i, 0)),
                  pl.BlockSpec((D_MODEL, IN_COLS), lambda i: (0, 0))],
        out_specs=[pl.BlockSpec((front + rows, w), lambda i: (i, 0)) for w in widths],
        out_shape=[jax.ShapeDtypeStruct((steps * (front + rows), w), BF16) for w in widths],
        compiler_params=_cparams(1),
        name="in_proj",
    )(x2d, w_bf16)


def _attn_kernel(sinks_ref, q_ref, kv_ref, kvm_ref, bias_ref, g_ref, o_ref,
                 klo, khi, vlo, vhi):
    nbx = q_ref.shape[0] // BLOCK
    lo_lanes = lax.broadcasted_iota(I32, (BLOCK // 2, LANES), 1) < HEAD_DIM

    def layout_block(n, blk):
        rows = pl.ds(pl.multiple_of(n * BLOCK, BLOCK), BLOCK)
        as_bf16 = lambda words: pltpu.bitcast(words, BF16)
        for src, dst_lo, dst_hi in ((blk[:, :KV_WIDTH], klo, khi), (blk[:, KV_WIDTH:], vlo, vhi)):
            w = pltpu.bitcast(src, U32)
            r = pltpu.roll(w, HEAD_DIM, axis=1)
            zero = jnp.zeros_like(w)
            dst_lo[0, rows, :] = as_bf16(jnp.where(lo_lanes, w, zero))
            dst_hi[0, rows, :] = as_bf16(jnp.where(lo_lanes, zero, r))
            dst_lo[1, rows, :] = as_bf16(jnp.where(lo_lanes, r, zero))
            dst_hi[1, rows, :] = as_bf16(jnp.where(lo_lanes, zero, w))

    layout_block(0, kvm_ref[...])

    def layout_body(n, carry):
        layout_block(n + 1, kv_ref[pl.ds(pl.multiple_of(n * BLOCK, BLOCK), BLOCK), :])
        return carry

    lax.fori_loop(0, nbx, layout_body, 0, unroll=4)

    ones_lo = jnp.where(lax.broadcasted_iota(I32, (2 * BLOCK, LANES), 1) < HEAD_DIM,
                        1.0, 0.0).astype(BF16)
    ones_hi = (1.0 - ones_lo.astype(F32)).astype(BF16)
    top_rows = lax.broadcasted_iota(I32, (2 * BLOCK, 1), 0) < BLOCK
    lo_half = lax.broadcasted_iota(I32, (2 * BLOCK, LANES), 1) < HEAD_DIM
    sinks = [sinks_ref[h] * LOG2E for h in range(N_Q_HEADS)]

    def block(i, carry):
        q_rows = pl.ds(pl.multiple_of(i * BLOCK, BLOCK), BLOCK)
        win = pl.ds(pl.multiple_of(i * BLOCK, BLOCK), 2 * BLOCK)
        q = q_ref[q_rows, :]
        first = jnp.minimum(i, 1)
        outs = []
        for j in range(N_KV_HEADS):
            q2 = jnp.concatenate([q[:, (2 * j) * LANES:(2 * j + 1) * LANES],
                                  q[:, (2 * j + 1) * LANES:(2 * j + 2) * LANES]], axis=0)
            kc = jnp.concatenate([klo[j, win, :], khi[j, win, :]], axis=0)
            s = lax.dot_general(q2, kc, (((1,), (1,)), ((), ())), preferred_element_type=F32)
            s = s + bias_ref[first, j]
            ps, es = [], []
            for c in range(2):
                sink = jnp.where(top_rows, sinks[4 * j + c], sinks[4 * j + 2 + c])
                sc = s[:, c * 2 * BLOCK:(c + 1) * 2 * BLOCK]
                m = jnp.maximum(jnp.max(sc, axis=1, keepdims=True), sink)
                ps.append(jnp.exp2(sc - m).astype(BF16))
                es.append(jnp.exp2(sink - m))
            v_lo = jnp.concatenate([vlo[j, win, :], ones_lo], axis=1)
            v_hi = jnp.concatenate([vhi[j, win, :], ones_hi], axis=1)
            r = jnp.dot(jnp.concatenate(ps, axis=1), jnp.concatenate([v_lo, v_hi], axis=0),
                        preferred_element_type=F32)
            den = r[:, LANES:] + jnp.where(lo_half, es[0], es[1])
            o2 = r[:, :LANES] * (1.0 / den)
            outs += [o2[:BLOCK], o2[BLOCK:]]
        out = jnp.concatenate(outs, axis=1)
        ms = jnp.mean(out * out, axis=1, keepdims=True)
        o_ref[q_rows, :] = (out * lax.rsqrt(ms + EPS) * g_ref[...]).astype(o_ref.dtype)
        return carry

    lax.fori_loop(0, nbx, block, 0, unroll=8)


def _attention(sinks, q, kv, kvm, bias, g_attn, bsz, nbx):
    seq = nbx * BLOCK
    const2 = lambda b: (0, 0)
    kv_scratch = pltpu.VMEM((N_KV_HEADS, seq + BLOCK, LANES), BF16)
    return pl.pallas_call(
        _attn_kernel,
        grid=(bsz,),
        in_specs=[pl.BlockSpec(memory_space=pltpu.SMEM),
                  pl.BlockSpec((None, seq, ATTN_WIDTH), lambda b: (b, 0, 0)),
                  pl.BlockSpec((None, seq, 2 * KV_WIDTH), lambda b: (b, 0, 0)),
                  pl.BlockSpec((BLOCK, 2 * KV_WIDTH), const2),
                  pl.BlockSpec((2, N_KV_HEADS, 2 * BLOCK, 4 * BLOCK), lambda b: (0, 0, 0, 0)),
                  pl.BlockSpec((1, ATTN_WIDTH), const2)],
        out_specs=pl.BlockSpec((None, seq, ATTN_WIDTH), lambda b: (b, 0, 0)),
        out_shape=jax.ShapeDtypeStruct((bsz, seq, ATTN_WIDTH), BF16),
        scratch_shapes=[kv_scratch, kv_scratch, kv_scratch, kv_scratch],
        compiler_params=_cparams(1),
        name="attention",
    )(sinks, q, kv, kvm, bias, g_attn)


def _sigmoid(v):
    return 0.5 * jnp.tanh(0.5 * v) + 0.5


def _gelu_tanh(y):
    c = float(np.sqrt(2.0 / np.pi))
    half = 0.5 * y
    return half + half * jnp.tanh(y * (c + (c * 0.044715) * (y * y)))


LRU_CHUNK = 44
LRU_SEG = SUBLANES * LRU_CHUNK
LRU_SLABS = LRU_WIDTH // LANES


def _lru_kernel(xr_ref, yr_ref, xrm_ref, yrm_ref, cw_ref, cb_ref, wa_ref, wx_ref, ba_ref,
                bx_ref, lam_ref, g_ref, o_ref, x_st, y_st, o_st, s_st, xtail, hcar, wg_ref):
    seq = xr_ref.shape[0]
    n_seg = (seq + BLOCK) // LRU_SEG

    @pl.when(pl.program_id(0) == 0)
    def _():
        c = wa_ref.shape[1]
        per = LANES // c
        zero = jnp.zeros((c, c), F32)
        for s in range(LRU_SLABS):
            rows = [jnp.concatenate([0.5 * w_ref[s * per + p] if q == p else zero
                                     for w_ref in (wa_ref, wx_ref) for q in range(per)], axis=1)
                    for p in range(per)]
            wg_ref[s] = jnp.concatenate(rows, axis=0).astype(BF16)

    xtail[...] = jnp.zeros_like(xtail)
    hcar[...] = jnp.zeros_like(hcar)
    lam = lam_ref[...]
    softplus_neg = jnp.maximum(-lam, 0.0) + jnp.log(1.0 + jnp.exp(-jnp.abs(lam)))
    sub = lax.broadcasted_iota(jnp.int32, (SUBLANES, LANES), 0)

    def half_bias(b_ref):
        per = LANES // b_ref.shape[1]
        return [0.5 * jnp.concatenate([b_ref[s * per + p:s * per + p + 1, :] for p in range(per)],
                                      axis=1) for s in range(LRU_SLABS)]

    half_ba, half_bx = half_bias(ba_ref), half_bias(bx_ref)

    def strided(j):
        return pl.ds(j, SUBLANES, stride=LRU_CHUNK)

    def piece(v, j):
        return v[j * SUBLANES:(j + 1) * SUBLANES, :]

    def segment(k, first):
        if first:
            head = LRU_SEG - BLOCK
            x_nat = jnp.concatenate([xrm_ref[...], xr_ref[0:head, :]], axis=0).astype(F32)
            y_nat = jnp.concatenate([yrm_ref[...], yr_ref[0:head, :]], axis=0).astype(F32)
        else:
            rows = pl.ds(pl.multiple_of(k * LRU_SEG - BLOCK, 2 * SUBLANES), LRU_SEG)
            x_nat = xr_ref[rows, :].astype(F32)
            y_nat = yr_ref[rows, :].astype(F32)
        for c in range(LRU_SLABS):
            x_st[c] = x_nat[:, c * LANES:(c + 1) * LANES]
            y_st[c] = y_nat[:, c * LANES:(c + 1) * LANES]
        first_row = k * LRU_SEG + LRU_CHUNK * sub
        sumsq = [jnp.zeros((SUBLANES, LANES), F32) for _ in range(LRU_CHUNK)]

        for c in range(LRU_SLABS):
            lanes = slice(c * LANES, (c + 1) * LANES)
            x = [x_st[c, strided(j), :] for j in range(LRU_CHUNK)]
            before = []
            for d in range(1, CONV_W):
                from_prev_chunk = pltpu.roll(x[LRU_CHUNK - d], 1, axis=0)
                before.append(jnp.where(sub == 0, xtail[d - 1:d, lanes], from_prev_chunk))
            for d in range(1, CONV_W):
                xtail[d - 1:d, lanes] = x[LRU_CHUNK - d][SUBLANES - 1:SUBLANES, :]

            def x_at(j):
                return x[j] if j >= 0 else before[-j - 1]

            taps = [cw_ref[t:t + 1, lanes] for t in range(CONV_W)]
            bias = cb_ref[:, lanes]
            xc = jnp.concatenate(
                [bias + sum(taps[t] * x_at(j - (CONV_W - 1) + t) for t in range(CONV_W))
                 for j in range(LRU_CHUNK)], axis=0)
            xcb = xc.astype(BF16)
            both = jnp.dot(xcb, wg_ref[c], preferred_element_type=F32)
            tr = jnp.tanh(both[:, :LANES] + half_ba[c])
            ti = jnp.tanh(both[:, LANES:] + half_bx[c])
            log_a_half = (-0.5 * LRU_C) * softplus_neg[:, lanes]
            a = jnp.exp(log_a_half * tr + log_a_half)
            half_xc = 0.5 * xc
            gated_x = half_xc * ti + half_xc
            z = 1.0 - a * a
            u = jnp.where(z > 0.0, z * lax.rsqrt(z), 0.0) * gated_x

            h = jnp.zeros((SUBLANES, LANES), F32)
            p = jnp.ones((SUBLANES, LANES), F32)
            hs, ps = [], []
            for j in range(LRU_CHUNK):
                aj = piece(a, j)
                uj = piece(u, j)
                if first:
                    uj = jnp.where(first_row + j >= PAD_FRONT, uj, 0.0)
                h = aj * h + uj
                p = aj * p
                hs.append(h)
                ps.append(p)
            entry = [hcar[:, lanes]]
            for s in range(SUBLANES):
                entry.append(h[s:s + 1, :] + p[s:s + 1, :] * entry[s])
            hcar[:, lanes] = entry[SUBLANES]
            entry_rows = jnp.concatenate(entry[:SUBLANES], axis=0)

            for j in range(LRU_CHUNK):
                state = hs[j] + ps[j] * entry_rows
                out = state * _gelu_tanh(y_st[c, strided(j), :])
                sumsq[j] = sumsq[j] + out * out
                o_st[c, strided(j), :] = out

        for j in range(LRU_CHUNK):
            ms = jnp.sum(sumsq[j], axis=1, keepdims=True) * (1.0 / LRU_WIDTH)
            s_st[strided(j), :] = jnp.broadcast_to(lax.rsqrt(ms + EPS), (SUBLANES, LANES))
        scale = s_st[...]
        for c in range(LRU_SLABS):
            lanes = slice(c * LANES, (c + 1) * LANES)
            normed = (o_st[c] * scale * g_ref[:, lanes]).astype(o_ref.dtype)
            if first:
                o_ref[0:LRU_SEG - BLOCK, lanes] = normed[BLOCK:, :]
            else:
                o_ref[rows, lanes] = normed

    assert BLOCK <= LRU_SEG
    segment(0, True)

    def later_segment(k, carry):
        segment(k, False)
        return carry

    lax.fori_loop(1, n_seg, later_segment, 0)


def _rglru(xr, yr, xrm, yrm, cw, cb, wa, wx, ba, bx, lam, g_lru, bsz, nbx):
    seq = nbx * BLOCK
    assert (seq + BLOCK) % LRU_SEG == 0
    main = pl.BlockSpec((None, seq, LRU_WIDTH), lambda b: (b, 0, 0))
    const2 = lambda b: (0, 0)
    row_spec = pl.BlockSpec((1, LRU_WIDTH), const2)
    gate_spec = pl.BlockSpec((None,) + wa.shape[1:], lambda b: (0, 0, 0, 0))
    bias_spec = pl.BlockSpec((None,) + ba.shape[1:], lambda b: (0, 0, 0))
    slabs = pltpu.VMEM((LRU_SLABS, LRU_SEG, LANES), F32)
    return pl.pallas_call(
        _lru_kernel,
        grid=(bsz,),
        in_specs=[main, main,
                  pl.BlockSpec((BLOCK, LRU_WIDTH), const2),
                  pl.BlockSpec((BLOCK, LRU_WIDTH), const2),
                  pl.BlockSpec((CONV_W, LRU_WIDTH), const2),
                  row_spec, gate_spec, gate_spec, bias_spec, bias_spec,
                  row_spec, row_spec],
        out_specs=main,
        out_shape=jax.ShapeDtypeStruct((bsz, seq, LRU_WIDTH), BF16),
        scratch_shapes=[slabs, slabs, slabs,
                        pltpu.VMEM((LRU_SEG, LANES), F32),
                        pltpu.VMEM((SUBLANES, LRU_WIDTH), F32),
                        pltpu.VMEM((1, LRU_WIDTH), F32),
                        pltpu.VMEM((LRU_SLABS, LANES, 2 * LANES), BF16)],
        compiler_params=_cparams(1),
        name="rglru",
    )(xr, yr, xrm, yrm, cw, cb, wa, wx, ba, bx, lam, g_lru)


def _pack_rows(v):
    bits = lax.bitcast_convert_type(v.astype(BF16).astype(F32), U32)
    return (bits[:, :PACKED] >> 16) | (bits[:, PACKED:] & jnp.uint32(0xFFFF0000))


def _unpack_rows(w):
    lo = lax.bitcast_convert_type(w << 16, F32)
    hi = lax.bitcast_convert_type(w & jnp.uint32(0xFFFF0000), F32)
    return lo, hi


def _layer_norm(z, g, b):
    mu = jnp.mean(z, axis=1, keepdims=True)
    zc = z - mu
    var = jnp.mean(zc * zc, axis=1, keepdims=True)
    return zc * lax.rsqrt(var + EPS) * g + b


def _out_proj_kernel(a_ref, l_ref, x_ref, w_ref, g_ref, b_ref, wrt_ref, brt_ref, tri_ref,
                     h_ref, hp_ref, info_ref, cnt_ref, dest_ref, w_b, wrt_b):
    @pl.when(pl.program_id(0) == 0)
    def _():
        cnt_ref[...] = jnp.zeros_like(cnt_ref)
        w_b[...] = w_ref[...].astype(BF16)
        hi = wrt_ref[...].astype(BF16)
        wrt_b[0:ROUTER_ROWS, :] = hi
        wrt_b[ROUTER_ROWS:, :] = (wrt_ref[...] - hi.astype(F32)).astype(BF16)

    mix = jnp.dot(jnp.concatenate([a_ref[...], l_ref[...]], axis=1), w_b[...],
                  preferred_element_type=F32)
    h = _layer_norm(ALPHA * x_ref[...] + mix, g_ref[...], b_ref[...])
    h_ref[...] = h
    hp_ref[...] = _pack_rows(h)

    h_hi = h.astype(BF16)
    h_lo = (h - h_hi.astype(F32)).astype(BF16)
    nt = (((1,), (1,)), ((), ()))
    both = lax.dot_general(wrt_b[...], h_hi, nt, preferred_element_type=F32)
    lg = (both[:ROUTER_ROWS] + both[ROUTER_ROWS:]
          + lax.dot_general(wrt_b[0:ROUTER_ROWS, :], h_lo, nt, preferred_element_type=F32)
          ) + brt_ref[...]
    tile_shape = (SUBLANES, h.shape[0])
    sub = lax.broadcasted_iota(I32, tile_shape, 0)
    ninf = -jnp.inf
    t0 = lg[0:SUBLANES]
    gl = jnp.where(sub < N_GROUPS, t0, ninf)
    gmax = jnp.max(gl, axis=0, keepdims=True)
    g_idx = jnp.min(jnp.where(gl == gmax, sub, SUBLANES), axis=0, keepdims=True)
    g_w = 1.0 / jnp.sum(jnp.where(sub < N_GROUPS, jnp.exp(t0 - gmax), 0.0),
                        axis=0, keepdims=True)
    el = lg[SUBLANES:2 * SUBLANES]
    for g in range(1, N_GROUPS):
        el = jnp.where(g_idx == g, lg[(g + 1) * SUBLANES:(g + 2) * SUBLANES], el)
    v1 = jnp.max(el, axis=0, keepdims=True)
    i1 = jnp.min(jnp.where(el == v1, sub, SUBLANES), axis=0, keepdims=True)
    el2 = jnp.where(sub == i1, ninf, el)
    v2 = jnp.max(el2, axis=0, keepdims=True)
    i2 = jnp.min(jnp.where(el2 == v2, sub, SUBLANES), axis=0, keepdims=True)
    t = jnp.exp(v2 - v1)
    w1 = 1.0 / (1.0 + t)
    w2 = t * w1
    e_base = g_idx * EXPERTS_PER_GROUP
    e1 = e_base + i1
    e2 = e_base + i2
    step = pl.program_id(0)
    rows = h.shape[0]
    info_ref[:, pl.ds(pl.multiple_of(step * rows, rows), rows)] = jnp.where(
        sub == 0, e1.astype(F32),
        jnp.where(sub == 1, e2.astype(F32),
                  jnp.where(sub == 2, g_w * w1, jnp.where(sub == 3, g_w * w2, 0.0))))
    expert = lax.broadcasted_iota(I32, (N_EXPERTS, rows), 0)
    chosen = (expert == e1).astype(F32) + (expert == e2).astype(F32)
    cnt_ref[...] += jnp.sum(chosen, axis=1, keepdims=True)

    @pl.when(step == pl.num_programs(0) - 1)
    def _():
        _route_slots(info_ref, cnt_ref, tri_ref, dest_ref)


def _route_slots(info_ref, cnt_ref, tri_ref, dest_ref):
    cols = tri_ref.shape[0]
    c = cnt_ref[...].astype(I32)
    padded = ((c + (MOE_BLOCK - 1)) // MOE_BLOCK) * MOE_BLOCK
    e = lax.broadcasted_iota(I32, (N_EXPERTS, LANES), 0)
    scan = padded
    for d in (1, 2, 4, 8, 16):
        scan = scan + jnp.where(e >= d, pltpu.roll(scan, d, axis=0), 0)
    pstart = (scan - padded)[:, 0:1].astype(F32)
    expert = lax.broadcasted_iota(I32, (N_EXPERTS, cols), 0)

    def one_hots(t):
        win = slice(t * cols, (t + 1) * cols)
        return ((expert == info_ref[0:1, win].astype(I32)).astype(F32),
                (expert == info_ref[1:2, win].astype(I32)).astype(F32))

    n_tiles = info_ref.shape[1] // cols
    stacked = jnp.concatenate([sum(one_hots(t)).astype(BF16) for t in range(n_tiles)], axis=0)
    before = jnp.dot(stacked, tri_ref[...], preferred_element_type=F32)
    seen = jnp.zeros((N_EXPERTS, 1), F32)
    for t in range(n_tiles):
        oh1, oh2 = one_hots(t)
        row_of = before[t * N_EXPERTS:(t + 1) * N_EXPERTS, :] + (seen + pstart)
        r1 = jnp.sum(oh1 * row_of, axis=0, keepdims=True)
        r2 = jnp.sum(oh2 * row_of, axis=0, keepdims=True)
        dest_ref[:, t * cols:(t + 1) * cols] = jnp.concatenate([r1, r2], axis=0).astype(I32)
        seen = seen + jnp.sum(oh1 + oh2, axis=1, keepdims=True)


ROUTER_ROWS = -(-(N_GROUPS + 1) * SUBLANES // (2 * SUBLANES)) * (2 * SUBLANES)


def _out_proj(attn_n, lru_n, x2d, w_out, ln_g, ln_b, w_rt, b_rt, rows, route_cols):
    n = x2d.shape[0]
    const = lambda i: (0, 0)
    tile = lambda w: pl.BlockSpec((rows, w), lambda i: (i, 0))
    tri = jnp.asarray(np.triu(np.ones((route_cols, route_cols), np.float32), 1), BF16)
    return pl.pallas_call(
        _out_proj_kernel,
        grid=(n // rows,),
        in_specs=[tile(ATTN_WIDTH), tile(LRU_WIDTH), tile(D_MODEL),
                  pl.BlockSpec((D_MODEL, D_MODEL), const),
                  pl.BlockSpec((1, D_MODEL), const),
                  pl.BlockSpec((1, D_MODEL), const),
                  pl.BlockSpec((ROUTER_ROWS, D_MODEL), const),
                  pl.BlockSpec((ROUTER_ROWS, 1), const),
                  pl.BlockSpec((route_cols, route_cols), const)],
        out_specs=[tile(D_MODEL), tile(PACKED),
                   pl.BlockSpec((SUBLANES, n), const),
                   pl.BlockSpec((N_EXPERTS, LANES), const),
                   pl.BlockSpec((TOP_K, n), const)],
        out_shape=[jax.ShapeDtypeStruct((n, D_MODEL), F32),
                   jax.ShapeDtypeStruct((n, PACKED), U32),
                   jax.ShapeDtypeStruct((SUBLANES, n), F32),
                   jax.ShapeDtypeStruct((N_EXPERTS, LANES), F32),
                   jax.ShapeDtypeStruct((TOP_K, n), I32)],
        scratch_shapes=[pltpu.VMEM((D_MODEL, D_MODEL), BF16),
                        pltpu.VMEM((2 * ROUTER_ROWS, D_MODEL), BF16)],
        compiler_params=_cparams(1),
        name="out_proj",
    )(attn_n, lru_n, x2d, w_out, ln_g, ln_b, w_rt, b_rt, tri)


def _sc_mesh():
    return plsc.VectorSubcoreMesh(core_axis_name="core", subcore_axis_name="subcore")


def _sc_worker_id():
    return lax.axis_index("subcore") * SC_CORES + lax.axis_index("core")


def _sc_scatter_rows(rows, dest, cap):
    n, width = rows.shape
    per_worker = n // SC_WORKERS
    n_win = per_worker // SC_WINDOW

    def body(x_hbm, d_hbm, o_hbm, i0_v, i1_v, rows_v, isem, rsem, sem0, sem1):
        wid = _sc_worker_id()

        def indices(j):
            win = pl.ds(wid * per_worker + j * SC_WINDOW, SC_WINDOW)
            return (pltpu.make_async_copy(d_hbm.at[0, win], i0_v.at[j], isem.at[0, j]),
                    pltpu.make_async_copy(d_hbm.at[1, win], i1_v.at[j], isem.at[1, j]))

        def read(j):
            src = x_hbm.at[pl.ds(wid * per_worker + j * SC_WINDOW, SC_WINDOW)]
            return pltpu.make_async_copy(src, rows_v.at[j % 2], rsem.at[j % 2])

        def scatters(j):
            return (pltpu.make_async_copy(rows_v.at[j % 2], o_hbm.at[i0_v.at[j]], sem0.at[j % 2]),
                    pltpu.make_async_copy(rows_v.at[j % 2], o_hbm.at[i1_v.at[j]], sem1.at[j % 2]))

        read(0).start()
        for j in range(n_win):
            for cp in indices(j):
                cp.start()
        for j in range(n_win):
            for cp in indices(j):
                cp.wait()
        for j in range(n_win):
            if j + 1 < n_win:
                if j >= 1:
                    for cp in scatters(j - 1):
                        cp.wait()
                read(j + 1).start()
            read(j).wait()
            for cp in scatters(j):
                cp.start()
        for j in range(max(n_win - 2, 0), n_win):
            for cp in scatters(j):
                cp.wait()

    return pl.kernel(
        body,
        out_type=jax.ShapeDtypeStruct((cap, width), rows.dtype),
        mesh=_sc_mesh(),
        scratch_types=[pltpu.VMEM((n_win, SC_WINDOW), I32), pltpu.VMEM((n_win, SC_WINDOW), I32),
                       pltpu.VMEM((2, SC_WINDOW, width), rows.dtype),
                       pltpu.SemaphoreType.DMA((TOP_K, n_win)),
                       pltpu.SemaphoreType.DMA((2,)), pltpu.SemaphoreType.DMA((2,)),
                       pltpu.SemaphoreType.DMA((2,))],
        name="dispatch",
    )(rows, dest)


def _sc_gather_rows(table, dest, start, length):
    width = table.shape[1]
    per_worker = TOP_K * length // SC_WORKERS
    n_win = per_worker // SC_WINDOW
    workers_per_choice = SC_WORKERS // TOP_K

    def body(y_hbm, d_hbm, o_hbm, i_v, rows_v, isem, gsem, wsem):
        wid = _sc_worker_id()
        choice = wid // workers_per_choice
        first = start + (wid % workers_per_choice) * per_worker

        def indices(j):
            win = pl.ds(first + j * SC_WINDOW, SC_WINDOW)
            return pltpu.make_async_copy(d_hbm.at[choice, win], i_v.at[j], isem.at[j])

        for j in range(n_win):
            indices(j).start()
        for j in range(n_win):
            indices(j).wait()

        def gather(j):
            return pltpu.make_async_copy(y_hbm.at[i_v.at[j]], rows_v.at[j % 2], gsem.at[j % 2])

        def write(j):
            dst = o_hbm.at[pl.ds(wid * per_worker + j * SC_WINDOW, SC_WINDOW)]
            return pltpu.make_async_copy(rows_v.at[j % 2], dst, wsem.at[j % 2])

        gather(0).start()
        for j in range(n_win):
            if j + 1 < n_win:
                if j >= 1:
                    write(j - 1).wait()
                gather(j + 1).start()
            gather(j).wait()
            write(j).start()
        for j in range(max(n_win - 2, 0), n_win):
            write(j).wait()

    return pl.kernel(
        body,
        out_type=jax.ShapeDtypeStruct((SC_WORKERS * per_worker, width), table.dtype),
        mesh=_sc_mesh(),
        scratch_types=[pltpu.VMEM((n_win, SC_WINDOW), I32),
                       pltpu.VMEM((2, SC_WINDOW, width), table.dtype),
                       pltpu.SemaphoreType.DMA((n_win,)),
                       pltpu.SemaphoreType.DMA((2,)), pltpu.SemaphoreType.DMA((2,))],
        name="collect",
    )(table, dest)


def _expert_kernel(bstart_ref, nblk_ref, nused_ref, xs_hbm, wg_ref, wu_ref, wd_ref, yb_hbm,
                   xbuf, ybuf, zbuf, xsem, ysem, zsem, wg_b, wu_b, wd_b):
    e = pl.program_id(0)
    nused = nused_ref[0]
    n_blocks = yb_hbm.shape[0] // MOE_BLOCK

    def rows(b):
        return pl.ds(pl.multiple_of(b * MOE_BLOCK, MOE_BLOCK), MOE_BLOCK)

    def x_copy(b):
        slot = b % X_RING
        return pltpu.make_async_copy(xs_hbm.at[rows(b)], xbuf.at[slot], xsem.at[slot])

    def y_copy(b):
        slot = b % Y_RING
        return pltpu.make_async_copy(ybuf.at[slot], yb_hbm.at[rows(b)], ysem.at[slot])

    @pl.when(e == 0)
    def _():
        for b in range(X_AHEAD):
            @pl.when(b < nused)
            def _():
                x_copy(b).start()

    wg_b[...] = wg_ref[...].astype(BF16)
    wu_b[...] = wu_ref[...].astype(BF16)
    wd_b[...] = wd_ref[...].astype(BF16)

    def run_blocks(b, count):
        for k in range(count):
            x_copy(b + k).wait()
        for k in range(count):
            nxt = b + X_AHEAD + k

            @pl.when(nxt < nused)
            def _():
                x_copy(nxt).start()

            @pl.when(b + k >= Y_RING)
            def _():
                y_copy(b + k - Y_RING).wait()

        words = jnp.concatenate([xbuf[(b + k) % X_RING] for k in range(count)], axis=0)
        lo, hi = _unpack_rows(words)
        x = jnp.concatenate([lo.astype(BF16), hi.astype(BF16)], axis=1)
        g = jnp.dot(x, wg_b[...], preferred_element_type=F32)
        u = jnp.dot(x, wu_b[...], preferred_element_type=F32)
        mid = (g * _sigmoid(g) * u).astype(BF16)
        y = _pack_rows(jnp.dot(mid, wd_b[...], preferred_element_type=F32))
        for k in range(count):
            ybuf[(b + k) % Y_RING] = y[k * MOE_BLOCK:(k + 1) * MOE_BLOCK]
            y_copy(b + k).start()

    b0 = bstart_ref[e]
    nb = nblk_ref[e]

    def group(i, carry):
        run_blocks(b0 + X_GROUP * i, X_GROUP)
        return carry

    lax.fori_loop(0, nb // X_GROUP, group, 0)
    done = nb - nb % X_GROUP
    size = X_GROUP // 2
    while size >= 1:
        @pl.when((nb // size) % 2 == 1)
        def _(size=size, done=done):
            run_blocks(b0 + done, size)

        done = done + (nb // size) % 2 * size
        size //= 2

    @pl.when(e == pl.num_programs(0) - 1)
    def _():
        for back in range(Y_RING, 0, -1):
            @pl.when(nused >= back)
            def _():
                y_copy(nused - back).wait()

        zbuf[...] = jnp.zeros_like(zbuf)

        def z_copy(b):
            return pltpu.make_async_copy(zbuf, yb_hbm.at[rows(b)], zsem.at[0])

        def z_start(b, carry):
            z_copy(b).start()
            return carry

        def z_wait(b, carry):
            z_copy(b).wait()
            return carry

        lax.fori_loop(nused, n_blocks, z_start, 0)
        lax.fori_loop(nused, n_blocks, z_wait, 0)


def _experts(bstart, nblk, nused, xs, w_gate, w_up, w_down):
    cap = xs.shape[0]
    w_idx = lambda e, bs, nb, nu: (e, 0, 0)
    grid_spec = pltpu.PrefetchScalarGridSpec(
        num_scalar_prefetch=3,
        grid=(N_EXPERTS,),
        in_specs=[pl.BlockSpec(memory_space=pl.ANY),
                  pl.BlockSpec((None, D_MODEL, D_FF), w_idx),
                  pl.BlockSpec((None, D_MODEL, D_FF), w_idx),
                  pl.BlockSpec((None, D_FF, D_MODEL), w_idx)],
        out_specs=pl.BlockSpec(memory_space=pl.ANY),
        scratch_shapes=[pltpu.VMEM((X_RING, MOE_BLOCK, PACKED), U32),
                        pltpu.VMEM((Y_RING, MOE_BLOCK, PACKED), U32),
                        pltpu.VMEM((MOE_BLOCK, PACKED), U32),
                        pltpu.SemaphoreType.DMA((X_RING,)),
                        pltpu.SemaphoreType.DMA((Y_RING,)),
                        pltpu.SemaphoreType.DMA((1,)),
                        pltpu.VMEM((D_MODEL, D_FF), BF16),
                        pltpu.VMEM((D_MODEL, D_FF), BF16),
                        pltpu.VMEM((D_FF, D_MODEL), BF16)])
    return pl.pallas_call(
        _expert_kernel,
        grid_spec=grid_spec,
        out_shape=jax.ShapeDtypeStruct((cap, PACKED), U32),
        compiler_params=_cparams(1),
        name="experts",
    )(bstart, nblk, nused, xs, w_gate, w_up, w_down)


def _combine_kernel(y0_ref, y1_ref, h_ref, info_ref, g_ref, b_ref, o_ref):
    info = info_ref[...].T
    g0 = info[:, 2:3]
    g1 = info[:, 3:4]
    lo0, hi0 = _unpack_rows(y0_ref[...])
    lo1, hi1 = _unpack_rows(y1_ref[...])
    y = jnp.concatenate([g0 * lo0 + g1 * lo1, g0 * hi0 + g1 * hi1], axis=1)
    o_ref[...] = _layer_norm(ALPHA * h_ref[...] + y, g_ref[...], b_ref[...])


def _combine(ys, h, info, ln_g, ln_b, rows, part, n_parts):
    n = h.shape[0]
    steps = n // n_parts // rows
    off = part * steps
    const = lambda i: (0, 0)
    return pl.pallas_call(
        _combine_kernel,
        grid=(steps,),
        in_specs=[pl.BlockSpec((rows, PACKED), lambda i: (i, 0)),
                  pl.BlockSpec((rows, PACKED), lambda i: (i + steps, 0)),
                  pl.BlockSpec((rows, D_MODEL), lambda i: (i + off, 0)),
                  pl.BlockSpec((SUBLANES, rows), lambda i: (0, i + off)),
                  pl.BlockSpec((1, D_MODEL), const),
                  pl.BlockSpec((1, D_MODEL), const)],
        out_specs=pl.BlockSpec((rows, D_MODEL), lambda i: (i + off, 0)),
        out_shape=jax.ShapeDtypeStruct((n, D_MODEL), F32),
        input_output_aliases={2: 0},
        compiler_params=_cparams(1),
        name="combine",
    )(ys, ys, h, info, ln_g, ln_b)


def _alibi_bias():
    qi = np.arange(BLOCK)[:, None]
    kj = np.arange(2 * BLOCK)[None, :]
    dist = qi - kj + BLOCK
    band = (dist >= 0) & (dist < BLOCK)
    slopes = np.exp2(-8.0 * np.arange(1, N_Q_HEADS + 1, dtype=np.float32) / N_Q_HEADS)
    bias = np.where(band[None], -slopes[:, None, None] * dist[None].astype(np.float32), NEG)
    bias = bias * LOG2E
    first = np.where((kj >= PAD_FRONT)[None], bias, NEG)
    out = np.empty((2, N_KV_HEADS, 2 * BLOCK, 4 * BLOCK), np.float32)
    for v, per_head in enumerate((first, bias)):
        for j in range(N_KV_HEADS):
            out[v, j] = np.block([[per_head[4 * j], per_head[4 * j + 1]],
                                  [per_head[4 * j + 2], per_head[4 * j + 3]]])
    return jnp.asarray(out, F32)


def kernel(x, meta_tokens, w_in, conv_w, conv_b, lru_wa, lru_ba, lru_wx, lru_bx, lru_lambda,
           attn_sinks, g_attn, g_lru, w_out, ln1_g, ln1_b, w_group, b_group, w_router,
           b_router, w_gate, w_up, w_down, ln2_g, ln2_b):
    bsz, seq, d = x.shape
    nbx = seq // BLOCK
    n_tok = bsz * seq
    x2d = x.reshape(n_tok, d)
    row = lambda v: v[0:1].reshape(1, -1).astype(F32)

    q_scale = jnp.concatenate([jnp.full((ATTN_WIDTH,), LOG2E * HEAD_DIM ** -0.5, F32),
                               jnp.ones((IN_COLS - ATTN_WIDTH,), F32)])
    w_in_b = (w_in[0] * q_scale).astype(BF16)
    q, kv, xr, yr = _in_proj(x2d, w_in_b, PROJ_ROWS)
    qm, kvm, xrm, yrm = _in_proj(meta_tokens.astype(F32), w_in_b, N_META, front=PAD_FRONT)
    shp = lambda a: a.reshape(bsz, seq, a.shape[-1])

    attn_n = _attention(attn_sinks[0].astype(F32), shp(q), shp(kv), kvm, _alibi_bias(),
                        row(g_attn), bsz, nbx)
    lru_n = _rglru(shp(xr), shp(yr), xrm, yrm, conv_w[0].astype(F32), row(conv_b),
                   lru_wa.astype(F32), lru_wx.astype(F32), lru_ba.astype(F32),
                   lru_bx.astype(F32), row(lru_lambda), row(g_lru), bsz, nbx)

    gpad = SUBLANES - N_GROUPS
    tail = ROUTER_ROWS - SUBLANES - N_EXPERTS
    w_rt = jnp.concatenate(
        [w_group[0].T, jnp.zeros((gpad, d), F32),
         jnp.transpose(w_router[0], (0, 2, 1)).reshape(N_EXPERTS, d),
         jnp.zeros((tail, d), F32)], axis=0).astype(F32)
    b_rt = jnp.concatenate([b_group[0], jnp.zeros((gpad,), F32), b_router[0].reshape(-1),
                            jnp.zeros((tail,), F32)]).astype(F32).reshape(ROUTER_ROWS, 1)
    h1, hp, info, cnt, dest = _out_proj(
        attn_n.reshape(n_tok, ATTN_WIDTH), lru_n.reshape(n_tok, LRU_WIDTH), x2d,
        w_out[0].astype(F32), row(ln1_g), row(ln1_b), w_rt, b_rt, OUT_PROJ_ROWS, ROUTE_ROWS)

    n_slots = n_tok * TOP_K
    n_blocks = n_slots // MOE_BLOCK + N_EXPERTS
    cap = n_blocks * MOE_BLOCK
    nblk = (cnt[:, 0].astype(I32) + MOE_BLOCK - 1) // MOE_BLOCK
    bends = jnp.cumsum(nblk)
    bstart = (bends - nblk).astype(I32)
    nused = bends[-1:].astype(I32)

    xs = _sc_scatter_rows(hp, dest, cap)
    yb = _experts(bstart, nblk.astype(I32), nused, xs, w_gate[0], w_up[0], w_down[0])
    out = h1
    part_len = n_tok // COMBINE_PARTS
    for part in range(COMBINE_PARTS):
        ys = _sc_gather_rows(yb, dest, part * part_len, part_len)
        out = _combine(ys, out, info, row(ln2_g), row(ln2_b), COMBINE_ROWS,
                       part, COMBINE_PARTS)
    return out.reshape(bsz, seq, d)
```

```python
import jax
import jax.numpy as jnp
import numpy as np
from jax import lax
from jax.experimental import pallas as pl
from jax.experimental.pallas import tpu as pltpu
from jax.experimental.pallas import tpu_sc as plsc

F32 = jnp.float32
BF16 = jnp.bfloat16
U32 = jnp.uint32
I32 = jnp.int32

D_MODEL = 1024
N_META = 16
BLOCK = 128
PAD_FRONT = BLOCK - N_META
HEAD_DIM = 64
ATTN_WIDTH = 512
LRU_WIDTH = 512
N_Q_HEADS = 8
N_KV_HEADS = 2
KV_WIDTH = N_KV_HEADS * HEAD_DIM
LRU_BLOCKS = 8
CONV_W = 4
LRU_C = 8.0
IN_COLS = ATTN_WIDTH + 2 * KV_WIDTH + 2 * LRU_WIDTH
N_GROUPS = 4
EXPERTS_PER_GROUP = 8
N_EXPERTS = N_GROUPS * EXPERTS_PER_GROUP
TOP_K = 2
D_FF = 512
MOE_BLOCK = 256
ALPHA = 2.0 ** 0.25
EPS = 1e-5
NEG = -1e30
LOG2E = float(np.log2(np.e))
LANES = 128
SUBLANES = 8
PACKED = D_MODEL // 2

PROJ_ROWS = 1024
OUT_PROJ_ROWS = 1024
ROUTE_ROWS = 512
ROUTE_TILES = 4
COMBINE_ROWS = 1024
COMBINE_PARTS = 4
X_GROUP = 4
X_AHEAD = 4
X_RING = X_AHEAD + X_GROUP
Y_RING = 2 * X_GROUP
VMEM_LIMIT = 48 * 1024 * 1024

SC_CORES = 2
SC_SUBCORES = 16
SC_WORKERS = SC_CORES * SC_SUBCORES
SC_WINDOW = 64


def _cparams(n_axes):
    return pltpu.CompilerParams(
        dimension_semantics=("arbitrary",) * n_axes, vmem_limit_bytes=VMEM_LIMIT)


def _in_proj_kernel(x_ref, w_ref, q_ref, kv_ref, xr_ref, yr_ref):
    proj = jnp.dot(x_ref[...].astype(BF16), w_ref[...], preferred_element_type=F32)
    front = q_ref.shape[0] - x_ref.shape[0]
    o = 0
    for ref, width in ((q_ref, ATTN_WIDTH), (kv_ref, 2 * KV_WIDTH),
                       (xr_ref, LRU_WIDTH), (yr_ref, LRU_WIDTH)):
        vals = proj[:, o:o + width].astype(ref.dtype)
        if front:
            vals = jnp.concatenate([jnp.zeros((front, width), ref.dtype), vals], axis=0)
        ref[...] = vals
        o += width


def _in_proj(x2d, w_bf16, rows, front=0):
    steps = x2d.shape[0] // rows
    widths = (ATTN_WIDTH, 2 * KV_WIDTH, LRU_WIDTH, LRU_WIDTH)
    return pl.pallas_call(
        _in_proj_kernel,
        grid=(steps,),
        in_specs=[pl.BlockSpec((rows, D_MODEL), lambda i: (i, 0)),
                  pl.BlockSpec((D_MODEL, IN_COLS), lambda i: (0, 0))],
        out_specs=[pl.BlockSpec((front + rows, w), lambda i: (i, 0)) for w in widths],
        out_shape=[jax.ShapeDtypeStruct((steps * (front + rows), w), BF16) for w in widths],
        compiler_params=_cparams(1),
        name="in_proj",
    )(x2d, w_bf16)


def _attn_kernel(sinks_ref, q_ref, kv_ref, kvm_ref, bias_ref, g_ref, o_ref,
                 klo, khi, vlo, vhi):
    nbx = q_ref.shape[0] // BLOCK
    lo_lanes = lax.broadcasted_iota(I32, (BLOCK // 2, LANES), 1) < HEAD_DIM

    def layout_block(n, blk):
        rows = pl.ds(pl.multiple_of(n * BLOCK, BLOCK), BLOCK)
        as_bf16 = lambda words: pltpu.bitcast(words, BF16)
        for src, dst_lo, dst_hi in ((blk[:, :KV_WIDTH], klo, khi), (blk[:, KV_WIDTH:], vlo, vhi)):
            w = pltpu.bitcast(src, U32)
            r = pltpu.roll(w, HEAD_DIM, axis=1)
            zero = jnp.zeros_like(w)
            dst_lo[0, rows, :] = as_bf16(jnp.where(lo_lanes, w, zero))
            dst_hi[0, rows, :] = as_bf16(jnp.where(lo_lanes, zero, r))
            dst_lo[1, rows, :] = as_bf16(jnp.where(lo_lanes, r, zero))
            dst_hi[1, rows, :] = as_bf16(jnp.where(lo_lanes, zero, w))

    layout_block(0, kvm_ref[...])

    def layout_body(n, carry):
        layout_block(n + 1, kv_ref[pl.ds(pl.multiple_of(n * BLOCK, BLOCK), BLOCK), :])
        return carry

    lax.fori_loop(0, nbx, layout_body, 0, unroll=4)

    ones_lo = jnp.where(lax.broadcasted_iota(I32, (2 * BLOCK, LANES), 1) < HEAD_DIM,
                        1.0, 0.0).astype(BF16)
    ones_hi = (1.0 - ones_lo.astype(F32)).astype(BF16)
    top_rows = lax.broadcasted_iota(I32, (2 * BLOCK, 1), 0) < BLOCK
    lo_half = lax.broadcasted_iota(I32, (2 * BLOCK, LANES), 1) < HEAD_DIM
    sinks = [sinks_ref[h] * LOG2E for h in range(N_Q_HEADS)]

    def block(i, carry):
        q_rows = pl.ds(pl.multiple_of(i * BLOCK, BLOCK), BLOCK)
        win = pl.ds(pl.multiple_of(i * BLOCK, BLOCK), 2 * BLOCK)
        q = q_ref[q_rows, :]
        first = jnp.minimum(i, 1)
        outs = []
        for j in range(N_KV_HEADS):
            q2 = jnp.concatenate([q[:, (2 * j) * LANES:(2 * j + 1) * LANES],
                                  q[:, (2 * j + 1) * LANES:(2 * j + 2) * LANES]], axis=0)
            kc = jnp.concatenate([klo[j, win, :], khi[j, win, :]], axis=0)
            s = lax.dot_general(q2, kc, (((1,), (1,)), ((), ())), preferred_element_type=F32)
            s = s + bias_ref[first, j]
            ps, es = [], []
            for c in range(2):
                sink = jnp.where(top_rows, sinks[4 * j + c], sinks[4 * j + 2 + c])
                sc = s[:, c * 2 * BLOCK:(c + 1) * 2 * BLOCK]
                m = jnp.maximum(jnp.max(sc, axis=1, keepdims=True), sink)
                ps.append(jnp.exp2(sc - m).astype(BF16))
                es.append(jnp.exp2(sink - m))
            v_lo = jnp.concatenate([vlo[j, win, :], ones_lo], axis=1)
            v_hi = jnp.concatenate([vhi[j, win, :], ones_hi], axis=1)
            r = jnp.dot(jnp.concatenate(ps, axis=1), jnp.concatenate([v_lo, v_hi], axis=0),
                        preferred_element_type=F32)
            den = r[:, LANES:] + jnp.where(lo_half, es[0], es[1])
            o2 = r[:, :LANES] * (1.0 / den)
            outs += [o2[:BLOCK], o2[BLOCK:]]
        out = jnp.concatenate(outs, axis=1)
        ms = jnp.mean(out * out, axis=1, keepdims=True)
        o_ref[q_rows, :] = (out * lax.rsqrt(ms + EPS) * g_ref[...]).astype(o_ref.dtype)
        return carry

    lax.fori_loop(0, nbx, block, 0, unroll=8)


def _attention(sinks, q, kv, kvm, bias, g_attn, bsz, nbx):
    seq = nbx * BLOCK
    const2 = lambda b: (0, 0)
    kv_scratch = pltpu.VMEM((N_KV_HEADS, seq + BLOCK, LANES), BF16)
    return pl.pallas_call(
        _attn_kernel,
        grid=(bsz,),
        in_specs=[pl.BlockSpec(memory_space=pltpu.SMEM),
                  pl.BlockSpec((None, seq, ATTN_WIDTH), lambda b: (b, 0, 0)),
                  pl.BlockSpec((None, seq, 2 * KV_WIDTH), lambda b: (b, 0, 0)),
                  pl.BlockSpec((BLOCK, 2 * KV_WIDTH), const2),
                  pl.BlockSpec((2, N_KV_HEADS, 2 * BLOCK, 4 * BLOCK), lambda b: (0, 0, 0, 0)),
                  pl.BlockSpec((1, ATTN_WIDTH), const2)],
        out_specs=pl.BlockSpec((None, seq, ATTN_WIDTH), lambda b: (b, 0, 0)),
        out_shape=jax.ShapeDtypeStruct((bsz, seq, ATTN_WIDTH), BF16),
        scratch_shapes=[kv_scratch, kv_scratch, kv_scratch, kv_scratch],
        compiler_params=_cparams(1),
        name="attention",
    )(sinks, q, kv, kvm, bias, g_attn)


def _sigmoid(v):
    return 0.5 * jnp.tanh(0.5 * v) + 0.5


def _gelu_tanh(y):
    c = float(np.sqrt(2.0 / np.pi))
    half = 0.5 * y
    return half + half * jnp.tanh(y * (c + (c * 0.044715) * (y * y)))


LRU_CHUNK = 44
LRU_SEG = SUBLANES * LRU_CHUNK
LRU_SLABS = LRU_WIDTH // LANES


def _lru_kernel(xr_ref, yr_ref, xrm_ref, yrm_ref, cw_ref, cb_ref, wa_ref, wx_ref, ba_ref,
                bx_ref, lam_ref, g_ref, o_ref, x_st, y_st, o_st, s_st, xtail, hcar, wg_ref):
    seq = xr_ref.shape[0]
    n_seg = (seq + BLOCK) // LRU_SEG

    @pl.when(pl.program_id(0) == 0)
    def _():
        c = wa_ref.shape[1]
        per = LANES // c
        zero = jnp.zeros((c, c), F32)
        for s in range(LRU_SLABS):
            rows = [jnp.concatenate([0.5 * w_ref[s * per + p] if q == p else zero
                                     for w_ref in (wa_ref, wx_ref) for q in range(per)], axis=1)
                    for p in range(per)]
            wg_ref[s] = jnp.concatenate(rows, axis=0).astype(BF16)

    xtail[...] = jnp.zeros_like(xtail)
    hcar[...] = jnp.zeros_like(hcar)
    lam = lam_ref[...]
    softplus_neg = jnp.maximum(-lam, 0.0) + jnp.log(1.0 + jnp.exp(-jnp.abs(lam)))
    sub = lax.broadcasted_iota(jnp.int32, (SUBLANES, LANES), 0)

    def half_bias(b_ref):
        per = LANES // b_ref.shape[1]
        return [0.5 * jnp.concatenate([b_ref[s * per + p:s * per + p + 1, :] for p in range(per)],
                                      axis=1) for s in range(LRU_SLABS)]

    half_ba, half_bx = half_bias(ba_ref), half_bias(bx_ref)

    def strided(j):
        return pl.ds(j, SUBLANES, stride=LRU_CHUNK)

    def piece(v, j):
        return v[j * SUBLANES:(j + 1) * SUBLANES, :]

    def segment(k, first):
        if first:
            head = LRU_SEG - BLOCK
            x_nat = jnp.concatenate([xrm_ref[...], xr_ref[0:head, :]], axis=0).astype(F32)
            y_nat = jnp.concatenate([yrm_ref[...], yr_ref[0:head, :]], axis=0).astype(F32)
        else:
            rows = pl.ds(pl.multiple_of(k * LRU_SEG - BLOCK, 2 * SUBLANES), LRU_SEG)
            x_nat = xr_ref[rows, :].astype(F32)
            y_nat = yr_ref[rows, :].astype(F32)
        for c in range(LRU_SLABS):
            x_st[c] = x_nat[:, c * LANES:(c + 1) * LANES]
            y_st[c] = y_nat[:, c * LANES:(c + 1) * LANES]
        first_row = k * LRU_SEG + LRU_CHUNK * sub
        sumsq = [jnp.zeros((SUBLANES, LANES), F32) for _ in range(LRU_CHUNK)]

        for c in range(LRU_SLABS):
            lanes = slice(c * LANES, (c + 1) * LANES)
            x = [x_st[c, strided(j), :] for j in range(LRU_CHUNK)]
            before = []
            for d in range(1, CONV_W):
                from_prev_chunk = pltpu.roll(x[LRU_CHUNK - d], 1, axis=0)
                before.append(jnp.where(sub == 0, xtail[d - 1:d, lanes], from_prev_chunk))
            for d in range(1, CONV_W):
                xtail[d - 1:d, lanes] = x[LRU_CHUNK - d][SUBLANES - 1:SUBLANES, :]

            def x_at(j):
                return x[j] if j >= 0 else before[-j - 1]

            taps = [cw_ref[t:t + 1, lanes] for t in range(CONV_W)]
            bias = cb_ref[:, lanes]
            xc = jnp.concatenate(
                [bias + sum(taps[t] * x_at(j - (CONV_W - 1) + t) for t in range(CONV_W))
                 for j in range(LRU_CHUNK)], axis=0)
            xcb = xc.astype(BF16)
            both = jnp.dot(xcb, wg_ref[c], preferred_element_type=F32)
            tr = jnp.tanh(both[:, :LANES] + half_ba[c])
            ti = jnp.tanh(both[:, LANES:] + half_bx[c])
            log_a_half = (-0.5 * LRU_C) * softplus_neg[:, lanes]
            a = jnp.exp(log_a_half * tr + log_a_half)
            half_xc = 0.5 * xc
            gated_x = half_xc * ti + half_xc
            z = 1.0 - a * a
            u = jnp.where(z > 0.0, z * lax.rsqrt(z), 0.0) * gated_x

            h = jnp.zeros((SUBLANES, LANES), F32)
            p = jnp.ones((SUBLANES, LANES), F32)
            hs, ps = [], []
            for j in range(LRU_CHUNK):
                aj = piece(a, j)
                uj = piece(u, j)
                if first:
                    uj = jnp.where(first_row + j >= PAD_FRONT, uj, 0.0)
                h = aj * h + uj
                p = aj * p
                hs.append(h)
                ps.append(p)
            entry = [hcar[:, lanes]]
            for s in range(SUBLANES):
                entry.append(h[s:s + 1, :] + p[s:s + 1, :] * entry[s])
            hcar[:, lanes] = entry[SUBLANES]
            entry_rows = jnp.concatenate(entry[:SUBLANES], axis=0)

            for j in range(LRU_CHUNK):
                state = hs[j] + ps[j] * entry_rows
                out = state * _gelu_tanh(y_st[c, strided(j), :])
                sumsq[j] = sumsq[j] + out * out
                o_st[c, strided(j), :] = out

        for j in range(LRU_CHUNK):
            ms = jnp.sum(sumsq[j], axis=1, keepdims=True) * (1.0 / LRU_WIDTH)
            s_st[strided(j), :] = jnp.broadcast_to(lax.rsqrt(ms + EPS), (SUBLANES, LANES))
        scale = s_st[...]
        for c in range(LRU_SLABS):
            lanes = slice(c * LANES, (c + 1) * LANES)
            normed = (o_st[c] * scale * g_ref[:, lanes]).astype(o_ref.dtype)
            if first:
                o_ref[0:LRU_SEG - BLOCK, lanes] = normed[BLOCK:, :]
            else:
                o_ref[rows, lanes] = normed

    assert BLOCK <= LRU_SEG
    segment(0, True)

    def later_segment(k, carry):
        segment(k, False)
        return carry

    lax.fori_loop(1, n_seg, later_segment, 0)


def _rglru(xr, yr, xrm, yrm, cw, cb, wa, wx, ba, bx, lam, g_lru, bsz, nbx):
    seq = nbx * BLOCK
    assert (seq + BLOCK) % LRU_SEG == 0
    main = pl.BlockSpec((None, seq, LRU_WIDTH), lambda b: (b, 0, 0))
    const2 = lambda b: (0, 0)
    row_spec = pl.BlockSpec((1, LRU_WIDTH), const2)
    gate_spec = pl.BlockSpec((None,) + wa.shape[1:], lambda b: (0, 0, 0, 0))
    bias_spec = pl.BlockSpec((None,) + ba.shape[1:], lambda b: (0, 0, 0))
    slabs = pltpu.VMEM((LRU_SLABS, LRU_SEG, LANES), F32)
    return pl.pallas_call(
        _lru_kernel,
        grid=(bsz,),
        in_specs=[main, main,
                  pl.BlockSpec((BLOCK, LRU_WIDTH), const2),
                  pl.BlockSpec((BLOCK, LRU_WIDTH), const2),
                  pl.BlockSpec((CONV_W, LRU_WIDTH), const2),
                  row_spec, gate_spec, gate_spec, bias_spec, bias_spec,
                  row_spec, row_spec],
        out_specs=main,
        out_shape=jax.ShapeDtypeStruct((bsz, seq, LRU_WIDTH), BF16),
        scratch_shapes=[slabs, slabs, slabs,
                        pltpu.VMEM((LRU_SEG, LANES), F32),
                        pltpu.VMEM((SUBLANES, LRU_WIDTH), F32),
                        pltpu.VMEM((1, LRU_WIDTH), F32),
                        pltpu.VMEM((LRU_SLABS, LANES, 2 * LANES), BF16)],
        compiler_params=_cparams(1),
        name="rglru",
    )(xr, yr, xrm, yrm, cw, cb, wa, wx, ba, bx, lam, g_lru)


def _pack_rows(v):
    bits = lax.bitcast_convert_type(v.astype(BF16).astype(F32), U32)
    return (bits[:, :PACKED] >> 16) | (bits[:, PACKED:] & jnp.uint32(0xFFFF0000))


def _unpack_rows(w):
    lo = lax.bitcast_convert_type(w << 16, F32)
    hi = lax.bitcast_convert_type(w & jnp.uint32(0xFFFF0000), F32)
    return lo, hi


def _layer_norm(z, g, b):
    mu = jnp.mean(z, axis=1, keepdims=True)
    zc = z - mu
    var = jnp.mean(zc * zc, axis=1, keepdims=True)
    return zc * lax.rsqrt(var + EPS) * g + b


def _out_proj_kernel(a_ref, l_ref, x_ref, w_ref, g_ref, b_ref, wrt_ref, brt_ref,
                     h_ref, hp_ref, info_ref, cnt_ref, w_b, wrt_b):
    @pl.when(pl.program_id(0) == 0)
    def _():
        cnt_ref[...] = jnp.zeros_like(cnt_ref)
        w_b[...] = w_ref[...].astype(BF16)
        hi = wrt_ref[...].astype(BF16)
        wrt_b[0:ROUTER_ROWS, :] = hi
        wrt_b[ROUTER_ROWS:, :] = (wrt_ref[...] - hi.astype(F32)).astype(BF16)

    mix = jnp.dot(jnp.concatenate([a_ref[...], l_ref[...]], axis=1), w_b[...],
                  preferred_element_type=F32)
    h = _layer_norm(ALPHA * x_ref[...] + mix, g_ref[...], b_ref[...])
    h_ref[...] = h
    hp_ref[...] = _pack_rows(h)

    h_hi = h.astype(BF16)
    h_lo = (h - h_hi.astype(F32)).astype(BF16)
    nt = (((1,), (1,)), ((), ()))
    both = lax.dot_general(wrt_b[...], h_hi, nt, preferred_element_type=F32)
    lg = (both[:ROUTER_ROWS] + both[ROUTER_ROWS:]
          + lax.dot_general(wrt_b[0:ROUTER_ROWS, :], h_lo, nt, preferred_element_type=F32)
          ) + brt_ref[...]
    tile_shape = (SUBLANES, h.shape[0])
    sub = lax.broadcasted_iota(I32, tile_shape, 0)
    ninf = -jnp.inf
    t0 = lg[0:SUBLANES]
    gl = jnp.where(sub < N_GROUPS, t0, ninf)
    gmax = jnp.max(gl, axis=0, keepdims=True)
    g_idx = jnp.min(jnp.where(gl == gmax, sub, SUBLANES), axis=0, keepdims=True)
    g_w = 1.0 / jnp.sum(jnp.where(sub < N_GROUPS, jnp.exp(t0 - gmax), 0.0),
                        axis=0, keepdims=True)
    el = lg[SUBLANES:2 * SUBLANES]
    for g in range(1, N_GROUPS):
        el = jnp.where(g_idx == g, lg[(g + 1) * SUBLANES:(g + 2) * SUBLANES], el)
    v1 = jnp.max(el, axis=0, keepdims=True)
    i1 = jnp.min(jnp.where(el == v1, sub, SUBLANES), axis=0, keepdims=True)
    el2 = jnp.where(sub == i1, ninf, el)
    v2 = jnp.max(el2, axis=0, keepdims=True)
    i2 = jnp.min(jnp.where(el2 == v2, sub, SUBLANES), axis=0, keepdims=True)
    t = jnp.exp(v2 - v1)
    w1 = 1.0 / (1.0 + t)
    w2 = t * w1
    e_base = g_idx * EXPERTS_PER_GROUP
    e1 = e_base + i1
    e2 = e_base + i2
    info_ref[...] = jnp.where(sub == 0, e1.astype(F32),
                              jnp.where(sub == 1, e2.astype(F32),
                                        jnp.where(sub == 2, g_w * w1,
                                                  jnp.where(sub == 3, g_w * w2, 0.0))))
    expert = lax.broadcasted_iota(I32, (N_EXPERTS, h.shape[0]), 0)
    chosen = (expert == e1).astype(F32) + (expert == e2).astype(F32)
    cnt_ref[...] += jnp.sum(chosen, axis=1, keepdims=True)


ROUTER_ROWS = -(-(N_GROUPS + 1) * SUBLANES // (2 * SUBLANES)) * (2 * SUBLANES)


def _out_proj(attn_n, lru_n, x2d, w_out, ln_g, ln_b, w_rt, b_rt, rows):
    n = x2d.shape[0]
    const = lambda i: (0, 0)
    tile = lambda w: pl.BlockSpec((rows, w), lambda i: (i, 0))
    return pl.pallas_call(
        _out_proj_kernel,
        grid=(n // rows,),
        in_specs=[tile(ATTN_WIDTH), tile(LRU_WIDTH), tile(D_MODEL),
                  pl.BlockSpec((D_MODEL, D_MODEL), const),
                  pl.BlockSpec((1, D_MODEL), const),
                  pl.BlockSpec((1, D_MODEL), const),
                  pl.BlockSpec((ROUTER_ROWS, D_MODEL), const),
                  pl.BlockSpec((ROUTER_ROWS, 1), const)],
        out_specs=[tile(D_MODEL), tile(PACKED),
                   pl.BlockSpec((SUBLANES, rows), lambda i: (0, i)),
                   pl.BlockSpec((N_EXPERTS, LANES), const)],
        out_shape=[jax.ShapeDtypeStruct((n, D_MODEL), F32),
                   jax.ShapeDtypeStruct((n, PACKED), U32),
                   jax.ShapeDtypeStruct((SUBLANES, n), F32),
                   jax.ShapeDtypeStruct((N_EXPERTS, LANES), F32)],
        scratch_shapes=[pltpu.VMEM((D_MODEL, D_MODEL), BF16),
                        pltpu.VMEM((2 * ROUTER_ROWS, D_MODEL), BF16)],
        compiler_params=_cparams(1),
        name="out_proj",
    )(attn_n, lru_n, x2d, w_out, ln_g, ln_b, w_rt, b_rt)


def _route_kernel(info_ref, cnt_ref, tri_ref, dest_ref, carry, pstart):
    t = pl.program_id(0)
    cols = tri_ref.shape[0]
    n_tiles = info_ref.shape[1] // cols
    expert = lax.broadcasted_iota(I32, (N_EXPERTS, cols), 0)

    def one_hots(k):
        win = slice(k * cols, (k + 1) * cols)
        return ((expert == info_ref[0:1, win].astype(I32)).astype(F32),
                (expert == info_ref[1:2, win].astype(I32)).astype(F32))

    @pl.when(t == 0)
    def _():
        c = cnt_ref[...].astype(I32)
        padded = ((c + (MOE_BLOCK - 1)) // MOE_BLOCK) * MOE_BLOCK
        e = lax.broadcasted_iota(I32, (N_EXPERTS, LANES), 0)
        scan = padded
        for d in (1, 2, 4, 8, 16):
            scan = scan + jnp.where(e >= d, pltpu.roll(scan, d, axis=0), 0)
        pstart[...] = (scan - padded)[:, 0:1].astype(F32)
        carry[...] = jnp.zeros_like(carry)

    stacked = jnp.concatenate([sum(one_hots(k)).astype(BF16) for k in range(n_tiles)], axis=0)
    before = jnp.dot(stacked, tri_ref[...], preferred_element_type=F32)
    seen = carry[...] + pstart[...]
    for k in range(n_tiles):
        oh1, oh2 = one_hots(k)
        row_of = before[k * N_EXPERTS:(k + 1) * N_EXPERTS, :] + seen
        r1 = jnp.sum(oh1 * row_of, axis=0, keepdims=True)
        r2 = jnp.sum(oh2 * row_of, axis=0, keepdims=True)
        dest_ref[:, k * cols:(k + 1) * cols] = jnp.concatenate([r1, r2], axis=0).astype(I32)
        seen = seen + jnp.sum(oh1 + oh2, axis=1, keepdims=True)
    carry[...] = seen - pstart[...]


def _route(info_t, cnt, cols):
    n = info_t.shape[1]
    tri = jnp.asarray(np.triu(np.ones((cols, cols), np.float32), 1), BF16)
    step_cols = ROUTE_TILES * cols
    return pl.pallas_call(
        _route_kernel,
        grid=(n // step_cols,),
        in_specs=[pl.BlockSpec((SUBLANES, step_cols), lambda t: (0, t)),
                  pl.BlockSpec((N_EXPERTS, LANES), lambda t: (0, 0)),
                  pl.BlockSpec((cols, cols), lambda t: (0, 0))],
        out_specs=pl.BlockSpec((TOP_K, step_cols), lambda t: (0, t)),
        out_shape=jax.ShapeDtypeStruct((TOP_K, n), I32),
        scratch_shapes=[pltpu.VMEM((N_EXPERTS, 1), F32), pltpu.VMEM((N_EXPERTS, 1), F32)],
        compiler_params=_cparams(1),
        name="route",
    )(info_t, cnt, tri)


def _sc_mesh():
    return plsc.VectorSubcoreMesh(core_axis_name="core", subcore_axis_name="subcore")


def _sc_worker_id():
    return lax.axis_index("subcore") * SC_CORES + lax.axis_index("core")


def _sc_scatter_rows(rows, dest, cap):
    n, width = rows.shape
    per_worker = n // SC_WORKERS
    n_win = per_worker // SC_WINDOW

    def body(x_hbm, d_hbm, o_hbm, i0_v, i1_v, rows_v, isem, rsem, sem0, sem1):
        wid = _sc_worker_id()

        def indices(j):
            win = pl.ds(wid * per_worker + j * SC_WINDOW, SC_WINDOW)
            return (pltpu.make_async_copy(d_hbm.at[0, win], i0_v.at[j], isem.at[0, j]),
                    pltpu.make_async_copy(d_hbm.at[1, win], i1_v.at[j], isem.at[1, j]))

        def read(j):
            src = x_hbm.at[pl.ds(wid * per_worker + j * SC_WINDOW, SC_WINDOW)]
            return pltpu.make_async_copy(src, rows_v.at[j % 2], rsem.at[j % 2])

        def scatters(j):
            return (pltpu.make_async_copy(rows_v.at[j % 2], o_hbm.at[i0_v.at[j]], sem0.at[j % 2]),
                    pltpu.make_async_copy(rows_v.at[j % 2], o_hbm.at[i1_v.at[j]], sem1.at[j % 2]))

        read(0).start()
        for j in range(n_win):
            for cp in indices(j):
                cp.start()
        for j in range(n_win):
            for cp in indices(j):
                cp.wait()
        for j in range(n_win):
            if j + 1 < n_win:
                if j >= 1:
                    for cp in scatters(j - 1):
                        cp.wait()
                read(j + 1).start()
            read(j).wait()
            for cp in scatters(j):
                cp.start()
        for j in range(max(n_win - 2, 0), n_win):
            for cp in scatters(j):
                cp.wait()

    return pl.kernel(
        body,
        out_type=jax.ShapeDtypeStruct((cap, width), rows.dtype),
        mesh=_sc_mesh(),
        scratch_types=[pltpu.VMEM((n_win, SC_WINDOW), I32), pltpu.VMEM((n_win, SC_WINDOW), I32),
                       pltpu.VMEM((2, SC_WINDOW, width), rows.dtype),
                       pltpu.SemaphoreType.DMA((TOP_K, n_win)),
                       pltpu.SemaphoreType.DMA((2,)), pltpu.SemaphoreType.DMA((2,)),
                       pltpu.SemaphoreType.DMA((2,))],
        name="dispatch",
    )(rows, dest)


def _sc_gather_rows(table, dest, start, length):
    width = table.shape[1]
    per_worker = TOP_K * length // SC_WORKERS
    n_win = per_worker // SC_WINDOW
    workers_per_choice = SC_WORKERS // TOP_K

    def body(y_hbm, d_hbm, o_hbm, i_v, rows_v, isem, gsem, wsem):
        wid = _sc_worker_id()
        choice = wid // workers_per_choice
        first = start + (wid % workers_per_choice) * per_worker

        def indices(j):
            win = pl.ds(first + j * SC_WINDOW, SC_WINDOW)
            return pltpu.make_async_copy(d_hbm.at[choice, win], i_v.at[j], isem.at[j])

        for j in range(n_win):
            indices(j).start()
        for j in range(n_win):
            indices(j).wait()

        def gather(j):
            return pltpu.make_async_copy(y_hbm.at[i_v.at[j]], rows_v.at[j % 2], gsem.at[j % 2])

        def write(j):
            dst = o_hbm.at[pl.ds(wid * per_worker + j * SC_WINDOW, SC_WINDOW)]
            return pltpu.make_async_copy(rows_v.at[j % 2], dst, wsem.at[j % 2])

        gather(0).start()
        for j in range(n_win):
            if j + 1 < n_win:
                if j >= 1:
                    write(j - 1).wait()
                gather(j + 1).start()
            gather(j).wait()
            write(j).start()
        for j in range(max(n_win - 2, 0), n_win):
            write(j).wait()

    return pl.kernel(
        body,
        out_type=jax.ShapeDtypeStruct((SC_WORKERS * per_worker, width), table.dtype),
        mesh=_sc_mesh(),
        scratch_types=[pltpu.VMEM((n_win, SC_WINDOW), I32),
                       pltpu.VMEM((2, SC_WINDOW, width), table.dtype),
                       pltpu.SemaphoreType.DMA((n_win,)),
                       pltpu.SemaphoreType.DMA((2,)), pltpu.SemaphoreType.DMA((2,))],
        name="collect",
    )(table, dest)


def _expert_kernel(bstart_ref, nblk_ref, nused_ref, xs_hbm, wg_ref, wu_ref, wd_ref, yb_hbm,
                   xbuf, ybuf, zbuf, xsem, ysem, zsem, wg_b, wu_b, wd_b):
    e = pl.program_id(0)
    nused = nused_ref[0]
    n_blocks = yb_hbm.shape[0] // MOE_BLOCK

    def rows(b):
        return pl.ds(pl.multiple_of(b * MOE_BLOCK, MOE_BLOCK), MOE_BLOCK)

    def x_copy(b):
        slot = b % X_RING
        return pltpu.make_async_copy(xs_hbm.at[rows(b)], xbuf.at[slot], xsem.at[slot])

    def y_copy(b):
        slot = b % Y_RING
        return pltpu.make_async_copy(ybuf.at[slot], yb_hbm.at[rows(b)], ysem.at[slot])

    @pl.when(e == 0)
    def _():
        for b in range(X_AHEAD):
            @pl.when(b < nused)
            def _():
                x_copy(b).start()

    wg_b[...] = wg_ref[...].astype(BF16)
    wu_b[...] = wu_ref[...].astype(BF16)
    wd_b[...] = wd_ref[...].astype(BF16)

    def run_blocks(b, count):
        for k in range(count):
            x_copy(b + k).wait()
        for k in range(count):
            nxt = b + X_AHEAD + k

            @pl.when(nxt < nused)
            def _():
                x_copy(nxt).start()

            @pl.when(b + k >= Y_RING)
            def _():
                y_copy(b + k - Y_RING).wait()

        words = jnp.concatenate([xbuf[(b + k) % X_RING] for k in range(count)], axis=0)
        lo, hi = _unpack_rows(words)
        x = jnp.concatenate([lo.astype(BF16), hi.astype(BF16)], axis=1)
        g = jnp.dot(x, wg_b[...], preferred_element_type=F32)
        u = jnp.dot(x, wu_b[...], preferred_element_type=F32)
        mid = (g * _sigmoid(g) * u).astype(BF16)
        y = _pack_rows(jnp.dot(mid, wd_b[...], preferred_element_type=F32))
        for k in range(count):
            ybuf[(b + k) % Y_RING] = y[k * MOE_BLOCK:(k + 1) * MOE_BLOCK]
            y_copy(b + k).start()

    b0 = bstart_ref[e]
    nb = nblk_ref[e]

    def group(i, carry):
        run_blocks(b0 + X_GROUP * i, X_GROUP)
        return carry

    lax.fori_loop(0, nb // X_GROUP, group, 0)
    done = nb - nb % X_GROUP
    size = X_GROUP // 2
    while size >= 1:
        @pl.when((nb // size) % 2 == 1)
        def _(size=size, done=done):
            run_blocks(b0 + done, size)

        done = done + (nb // size) % 2 * size
        size //= 2

    @pl.when(e == pl.num_programs(0) - 1)
    def _():
        for back in range(Y_RING, 0, -1):
            @pl.when(nused >= back)
            def _():
                y_copy(nused - back).wait()

        zbuf[...] = jnp.zeros_like(zbuf)

        def z_copy(b):
            return pltpu.make_async_copy(zbuf, yb_hbm.at[rows(b)], zsem.at[0])

        def z_start(b, carry):
            z_copy(b).start()
            return carry

        def z_wait(b, carry):
            z_copy(b).wait()
            return carry

        lax.fori_loop(nused, n_blocks, z_start, 0)
        lax.fori_loop(nused, n_blocks, z_wait, 0)


def _experts(bstart, nblk, nused, xs, w_gate, w_up, w_down):
    cap = xs.shape[0]
    w_idx = lambda e, bs, nb, nu: (e, 0, 0)
    grid_spec = pltpu.PrefetchScalarGridSpec(
        num_scalar_prefetch=3,
        grid=(N_EXPERTS,),
        in_specs=[pl.BlockSpec(memory_space=pl.ANY),
                  pl.BlockSpec((None, D_MODEL, D_FF), w_idx),
                  pl.BlockSpec((None, D_MODEL, D_FF), w_idx),
                  pl.BlockSpec((None, D_FF, D_MODEL), w_idx)],
        out_specs=pl.BlockSpec(memory_space=pl.ANY),
        scratch_shapes=[pltpu.VMEM((X_RING, MOE_BLOCK, PACKED), U32),
                        pltpu.VMEM((Y_RING, MOE_BLOCK, PACKED), U32),
                        pltpu.VMEM((MOE_BLOCK, PACKED), U32),
                        pltpu.SemaphoreType.DMA((X_RING,)),
                        pltpu.SemaphoreType.DMA((Y_RING,)),
                        pltpu.SemaphoreType.DMA((1,)),
                        pltpu.VMEM((D_MODEL, D_FF), BF16),
                        pltpu.VMEM((D_MODEL, D_FF), BF16),
                        pltpu.VMEM((D_FF, D_MODEL), BF16)])
    return pl.pallas_call(
        _expert_kernel,
        grid_spec=grid_spec,
        out_shape=jax.ShapeDtypeStruct((cap, PACKED), U32),
        compiler_params=_cparams(1),
        name="experts",
    )(bstart, nblk, nused, xs, w_gate, w_up, w_down)


def _combine_kernel(y0_ref, y1_ref, h_ref, info_ref, g_ref, b_ref, o_ref):
    info = info_ref[...].T
    g0 = info[:, 2:3]
    g1 = info[:, 3:4]
    lo0, hi0 = _unpack_rows(y0_ref[...])
    lo1, hi1 = _unpack_rows(y1_ref[...])
    y = jnp.concatenate([g0 * lo0 + g1 * lo1, g0 * hi0 + g1 * hi1], axis=1)
    o_ref[...] = _layer_norm(ALPHA * h_ref[...] + y, g_ref[...], b_ref[...])


def _combine(ys, h, info, ln_g, ln_b, rows, part, n_parts):
    n = h.shape[0]
    steps = n // n_parts // rows
    off = part * steps
    const = lambda i: (0, 0)
    return pl.pallas_call(
        _combine_kernel,
        grid=(steps,),
        in_specs=[pl.BlockSpec((rows, PACKED), lambda i: (i, 0)),
                  pl.BlockSpec((rows, PACKED), lambda i: (i + steps, 0)),
                  pl.BlockSpec((rows, D_MODEL), lambda i: (i + off, 0)),
                  pl.BlockSpec((SUBLANES, rows), lambda i: (0, i + off)),
                  pl.BlockSpec((1, D_MODEL), const),
                  pl.BlockSpec((1, D_MODEL), const)],
        out_specs=pl.BlockSpec((rows, D_MODEL), lambda i: (i + off, 0)),
        out_shape=jax.ShapeDtypeStruct((n, D_MODEL), F32),
        input_output_aliases={2: 0},
        compiler_params=_cparams(1),
        name="combine",
    )(ys, ys, h, info, ln_g, ln_b)


def _alibi_bias():
    qi = np.arange(BLOCK)[:, None]
    kj = np.arange(2 * BLOCK)[None, :]
    dist = qi - kj + BLOCK
    band = (dist >= 0) & (dist < BLOCK)
    slopes = np.exp2(-8.0 * np.arange(1, N_Q_HEADS + 1, dtype=np.float32) / N_Q_HEADS)
    bias = np.where(band[None], -slopes[:, None, None] * dist[None].astype(np.float32), NEG)
    bias = bias * LOG2E
    first = np.where((kj >= PAD_FRONT)[None], bias, NEG)
    out = np.empty((2, N_KV_HEADS, 2 * BLOCK, 4 * BLOCK), np.float32)
    for v, per_head in enumerate((first, bias)):
        for j in range(N_KV_HEADS):
            out[v, j] = np.block([[per_head[4 * j], per_head[4 * j + 1]],
                                  [per_head[4 * j + 2], per_head[4 * j + 3]]])
    return jnp.asarray(out, F32)


def kernel(x, meta_tokens, w_in, conv_w, conv_b, lru_wa, lru_ba, lru_wx, lru_bx, lru_lambda,
           attn_sinks, g_attn, g_lru, w_out, ln1_g, ln1_b, w_group, b_group, w_router,
           b_router, w_gate, w_up, w_down, ln2_g, ln2_b):
    bsz, seq, d = x.shape
    nbx = seq // BLOCK
    n_tok = bsz * seq
    x2d = x.reshape(n_tok, d)
    row = lambda v: v[0:1].reshape(1, -1).astype(F32)

    q_scale = jnp.concatenate([jnp.full((ATTN_WIDTH,), LOG2E * HEAD_DIM ** -0.5, F32),
                               jnp.ones((IN_COLS - ATTN_WIDTH,), F32)])
    w_in_b = (w_in[0] * q_scale).astype(BF16)
    q, kv, xr, yr = _in_proj(x2d, w_in_b, PROJ_ROWS)
    qm, kvm, xrm, yrm = _in_proj(meta_tokens.astype(F32), w_in_b, N_META, front=PAD_FRONT)
    shp = lambda a: a.reshape(bsz, seq, a.shape[-1])

    attn_n = _attention(attn_sinks[0].astype(F32), shp(q), shp(kv), kvm, _alibi_bias(),
                        row(g_attn), bsz, nbx)
    lru_n = _rglru(shp(xr), shp(yr), xrm, yrm, conv_w[0].astype(F32), row(conv_b),
                   lru_wa.astype(F32), lru_wx.astype(F32), lru_ba.astype(F32),
                   lru_bx.astype(F32), row(lru_lambda), row(g_lru), bsz, nbx)

    gpad = SUBLANES - N_GROUPS
    tail = ROUTER_ROWS - SUBLANES - N_EXPERTS
    w_rt = jnp.concatenate(
        [w_group[0].T, jnp.zeros((gpad, d), F32),
         jnp.transpose(w_router[0], (0, 2, 1)).reshape(N_EXPERTS, d),
         jnp.zeros((tail, d), F32)], axis=0).astype(F32)
    b_rt = jnp.concatenate([b_group[0], jnp.zeros((gpad,), F32), b_router[0].reshape(-1),
                            jnp.zeros((tail,), F32)]).astype(F32).reshape(ROUTER_ROWS, 1)
    h1, hp, info, cnt = _out_proj(
        attn_n.reshape(n_tok, ATTN_WIDTH), lru_n.reshape(n_tok, LRU_WIDTH), x2d,
        w_out[0].astype(F32), row(ln1_g), row(ln1_b), w_rt, b_rt, OUT_PROJ_ROWS)

    dest = _route(info, cnt, ROUTE_ROWS)
    n_slots = n_tok * TOP_K
    n_blocks = n_slots // MOE_BLOCK + N_EXPERTS
    cap = n_blocks * MOE_BLOCK
    nblk = (cnt[:, 0].astype(I32) + MOE_BLOCK - 1) // MOE_BLOCK
    bends = jnp.cumsum(nblk)
    bstart = (bends - nblk).astype(I32)
    nused = bends[-1:].astype(I32)

    xs = _sc_scatter_rows(hp, dest, cap)
    yb = _experts(bstart, nblk.astype(I32), nused, xs, w_gate[0], w_up[0], w_down[0])
    out = h1
    part_len = n_tok // COMBINE_PARTS
    for part in range(COMBINE_PARTS):
        ys = _sc_gather_rows(yb, dest, part * part_len, part_len)
        out = _combine(ys, out, info, row(ln2_g), row(ln2_b), COMBINE_ROWS,
                       part, COMBINE_PARTS)
    return out.reshape(bsz, seq, d)
```

```python
import jax
import jax.numpy as jnp
import numpy as np
from jax import lax
from jax.experimental import pallas as pl
from jax.experimental.pallas import tpu as pltpu
from jax.experimental.pallas import tpu_sc as plsc

F32 = jnp.float32
BF16 = jnp.bfloat16
U32 = jnp.uint32
I32 = jnp.int32

D_MODEL = 1024
N_META = 16
BLOCK = 128
PAD_FRONT = BLOCK - N_META
HEAD_DIM = 64
ATTN_WIDTH = 512
LRU_WIDTH = 512
N_Q_HEADS = 8
N_KV_HEADS = 2
KV_WIDTH = N_KV_HEADS * HEAD_DIM
LRU_BLOCKS = 8
CONV_W = 4
LRU_C = 8.0
IN_COLS = ATTN_WIDTH + 2 * KV_WIDTH + 2 * LRU_WIDTH
N_GROUPS = 4
EXPERTS_PER_GROUP = 8
N_EXPERTS = N_GROUPS * EXPERTS_PER_GROUP
TOP_K = 2
D_FF = 512
MOE_BLOCK = 256
ALPHA = 2.0 ** 0.25
EPS = 1e-5
NEG = -1e30
LOG2E = float(np.log2(np.e))
LANES = 128
SUBLANES = 8
PACKED = D_MODEL // 2

PROJ_ROWS = 1024
OUT_PROJ_ROWS = 1024
ROUTE_ROWS = 512
ROUTE_TILES = 8
COMBINE_ROWS = 1024
COMBINE_PARTS = 4
X_GROUP = 4
X_AHEAD = 4
X_RING = X_AHEAD + X_GROUP
Y_RING = 2 * X_GROUP
VMEM_LIMIT = 48 * 1024 * 1024

SC_CORES = 2
SC_SUBCORES = 16
SC_WORKERS = SC_CORES * SC_SUBCORES
SC_WINDOW = 64


def _cparams(n_axes):
    return pltpu.CompilerParams(
        dimension_semantics=("arbitrary",) * n_axes, vmem_limit_bytes=VMEM_LIMIT)


def _in_proj_kernel(x_ref, w_ref, q_ref, kv_ref, xr_ref, yr_ref):
    proj = jnp.dot(x_ref[...].astype(BF16), w_ref[...], preferred_element_type=F32)
    front = q_ref.shape[0] - x_ref.shape[0]
    o = 0
    for ref, width in ((q_ref, ATTN_WIDTH), (kv_ref, 2 * KV_WIDTH),
                       (xr_ref, LRU_WIDTH), (yr_ref, LRU_WIDTH)):
        vals = proj[:, o:o + width].astype(ref.dtype)
        if front:
            vals = jnp.concatenate([jnp.zeros((front, width), ref.dtype), vals], axis=0)
        ref[...] = vals
        o += width


def _in_proj(x2d, w_bf16, rows, front=0):
    steps = x2d.shape[0] // rows
    widths = (ATTN_WIDTH, 2 * KV_WIDTH, LRU_WIDTH, LRU_WIDTH)
    return pl.pallas_call(
        _in_proj_kernel,
        grid=(steps,),
        in_specs=[pl.BlockSpec((rows, D_MODEL), lambda i: (i, 0)),
                  pl.BlockSpec((D_MODEL, IN_COLS), lambda i: (0, 0))],
        out_specs=[pl.BlockSpec((front + rows, w), lambda i: (i, 0)) for w in widths],
        out_shape=[jax.ShapeDtypeStruct((steps * (front + rows), w), BF16) for w in widths],
        compiler_params=_cparams(1),
        name="in_proj",
    )(x2d, w_bf16)


def _attn_kernel(sinks_ref, q_ref, kv_ref, kvm_ref, bias_ref, g_ref, o_ref,
                 klo, khi, vlo, vhi):
    nbx = q_ref.shape[0] // BLOCK
    lo_lanes = lax.broadcasted_iota(I32, (BLOCK // 2, LANES), 1) < HEAD_DIM

    def layout_block(n, blk):
        rows = pl.ds(pl.multiple_of(n * BLOCK, BLOCK), BLOCK)
        as_bf16 = lambda words: pltpu.bitcast(words, BF16)
        for src, dst_lo, dst_hi in ((blk[:, :KV_WIDTH], klo, khi), (blk[:, KV_WIDTH:], vlo, vhi)):
            w = pltpu.bitcast(src, U32)
            r = pltpu.roll(w, HEAD_DIM, axis=1)
            zero = jnp.zeros_like(w)
            dst_lo[0, rows, :] = as_bf16(jnp.where(lo_lanes, w, zero))
            dst_hi[0, rows, :] = as_bf16(jnp.where(lo_lanes, zero, r))
            dst_lo[1, rows, :] = as_bf16(jnp.where(lo_lanes, r, zero))
            dst_hi[1, rows, :] = as_bf16(jnp.where(lo_lanes, zero, w))

    layout_block(0, kvm_ref[...])

    def layout_body(n, carry):
        layout_block(n + 1, kv_ref[pl.ds(pl.multiple_of(n * BLOCK, BLOCK), BLOCK), :])
        return carry

    lax.fori_loop(0, nbx, layout_body, 0, unroll=4)

    ones_lo = jnp.where(lax.broadcasted_iota(I32, (2 * BLOCK, LANES), 1) < HEAD_DIM,
                        1.0, 0.0).astype(BF16)
    ones_hi = (1.0 - ones_lo.astype(F32)).astype(BF16)
    top_rows = lax.broadcasted_iota(I32, (2 * BLOCK, 1), 0) < BLOCK
    lo_half = lax.broadcasted_iota(I32, (2 * BLOCK, LANES), 1) < HEAD_DIM
    sinks = [sinks_ref[h] * LOG2E for h in range(N_Q_HEADS)]

    def block(i, carry):
        q_rows = pl.ds(pl.multiple_of(i * BLOCK, BLOCK), BLOCK)
        win = pl.ds(pl.multiple_of(i * BLOCK, BLOCK), 2 * BLOCK)
        q = q_ref[q_rows, :]
        first = jnp.minimum(i, 1)
        outs = []
        for j in range(N_KV_HEADS):
            q2 = jnp.concatenate([q[:, (2 * j) * LANES:(2 * j + 1) * LANES],
                                  q[:, (2 * j + 1) * LANES:(2 * j + 2) * LANES]], axis=0)
            kc = jnp.concatenate([klo[j, win, :], khi[j, win, :]], axis=0)
            s = lax.dot_general(q2, kc, (((1,), (1,)), ((), ())), preferred_element_type=F32)
            s = s + bias_ref[first, j]
            ps, es = [], []
            for c in range(2):
                sink = jnp.where(top_rows, sinks[4 * j + c], sinks[4 * j + 2 + c])
                sc = s[:, c * 2 * BLOCK:(c + 1) * 2 * BLOCK]
                m = jnp.maximum(jnp.max(sc, axis=1, keepdims=True), sink)
                ps.append(jnp.exp2(sc - m).astype(BF16))
                es.append(jnp.exp2(sink - m))
            v_lo = jnp.concatenate([vlo[j, win, :], ones_lo], axis=1)
            v_hi = jnp.concatenate([vhi[j, win, :], ones_hi], axis=1)
            r = jnp.dot(jnp.concatenate(ps, axis=1), jnp.concatenate([v_lo, v_hi], axis=0),
                        preferred_element_type=F32)
            den = r[:, LANES:] + jnp.where(lo_half, es[0], es[1])
            o2 = r[:, :LANES] * (1.0 / den)
            outs += [o2[:BLOCK], o2[BLOCK:]]
        out = jnp.concatenate(outs, axis=1)
        ms = jnp.mean(out * out, axis=1, keepdims=True)
        o_ref[q_rows, :] = (out * lax.rsqrt(ms + EPS) * g_ref[...]).astype(o_ref.dtype)
        return carry

    lax.fori_loop(0, nbx, block, 0, unroll=8)


def _attention(sinks, q, kv, kvm, bias, g_attn, bsz, nbx):
    seq = nbx * BLOCK
    const2 = lambda b: (0, 0)
    kv_scratch = pltpu.VMEM((N_KV_HEADS, seq + BLOCK, LANES), BF16)
    return pl.pallas_call(
        _attn_kernel,
        grid=(bsz,),
        in_specs=[pl.BlockSpec(memory_space=pltpu.SMEM),
                  pl.BlockSpec((None, seq, ATTN_WIDTH), lambda b: (b, 0, 0)),
                  pl.BlockSpec((None, seq, 2 * KV_WIDTH), lambda b: (b, 0, 0)),
                  pl.BlockSpec((BLOCK, 2 * KV_WIDTH), const2),
                  pl.BlockSpec((2, N_KV_HEADS, 2 * BLOCK, 4 * BLOCK), lambda b: (0, 0, 0, 0)),
                  pl.BlockSpec((1, ATTN_WIDTH), const2)],
        out_specs=pl.BlockSpec((None, seq, ATTN_WIDTH), lambda b: (b, 0, 0)),
        out_shape=jax.ShapeDtypeStruct((bsz, seq, ATTN_WIDTH), BF16),
        scratch_shapes=[kv_scratch, kv_scratch, kv_scratch, kv_scratch],
        compiler_params=_cparams(1),
        name="attention",
    )(sinks, q, kv, kvm, bias, g_attn)


def _sigmoid(v):
    return 0.5 * jnp.tanh(0.5 * v) + 0.5


def _gelu_tanh(y):
    c = float(np.sqrt(2.0 / np.pi))
    half = 0.5 * y
    return half + half * jnp.tanh(y * (c + (c * 0.044715) * (y * y)))


LRU_CHUNK = 44
LRU_SEG = SUBLANES * LRU_CHUNK
LRU_SLABS = LRU_WIDTH // LANES


def _lru_kernel(xr_ref, yr_ref, xrm_ref, yrm_ref, cw_ref, cb_ref, wa_ref, wx_ref, ba_ref,
                bx_ref, lam_ref, g_ref, o_ref, x_st, y_st, o_st, s_st, xtail, hcar, wg_ref):
    seq = xr_ref.shape[0]
    n_seg = (seq + BLOCK) // LRU_SEG

    @pl.when(pl.program_id(0) == 0)
    def _():
        c = wa_ref.shape[1]
        per = LANES // c
        zero = jnp.zeros((c, c), F32)
        for s in range(LRU_SLABS):
            rows = [jnp.concatenate([0.5 * w_ref[s * per + p] if q == p else zero
                                     for w_ref in (wa_ref, wx_ref) for q in range(per)], axis=1)
                    for p in range(per)]
            wg_ref[s] = jnp.concatenate(rows, axis=0).astype(BF16)

    xtail[...] = jnp.zeros_like(xtail)
    hcar[...] = jnp.zeros_like(hcar)
    lam = lam_ref[...]
    softplus_neg = jnp.maximum(-lam, 0.0) + jnp.log(1.0 + jnp.exp(-jnp.abs(lam)))
    sub = lax.broadcasted_iota(jnp.int32, (SUBLANES, LANES), 0)

    def half_bias(b_ref):
        per = LANES // b_ref.shape[1]
        return [0.5 * jnp.concatenate([b_ref[s * per + p:s * per + p + 1, :] for p in range(per)],
                                      axis=1) for s in range(LRU_SLABS)]

    half_ba, half_bx = half_bias(ba_ref), half_bias(bx_ref)

    def strided(j):
        return pl.ds(j, SUBLANES, stride=LRU_CHUNK)

    def piece(v, j):
        return v[j * SUBLANES:(j + 1) * SUBLANES, :]

    def segment(k, first):
        if first:
            head = LRU_SEG - BLOCK
            x_nat = jnp.concatenate([xrm_ref[...], xr_ref[0:head, :]], axis=0).astype(F32)
            y_nat = jnp.concatenate([yrm_ref[...], yr_ref[0:head, :]], axis=0).astype(F32)
        else:
            rows = pl.ds(pl.multiple_of(k * LRU_SEG - BLOCK, 2 * SUBLANES), LRU_SEG)
            x_nat = xr_ref[rows, :].astype(F32)
            y_nat = yr_ref[rows, :].astype(F32)
        for c in range(LRU_SLABS):
            x_st[c] = x_nat[:, c * LANES:(c + 1) * LANES]
            y_st[c] = y_nat[:, c * LANES:(c + 1) * LANES]
        first_row = k * LRU_SEG + LRU_CHUNK * sub
        sumsq = [jnp.zeros((SUBLANES, LANES), F32) for _ in range(LRU_CHUNK)]

        for c in range(LRU_SLABS):
            lanes = slice(c * LANES, (c + 1) * LANES)
            x = [x_st[c, strided(j), :] for j in range(LRU_CHUNK)]
            before = []
            for d in range(1, CONV_W):
                from_prev_chunk = pltpu.roll(x[LRU_CHUNK - d], 1, axis=0)
                before.append(jnp.where(sub == 0, xtail[d - 1:d, lanes], from_prev_chunk))
            for d in range(1, CONV_W):
                xtail[d - 1:d, lanes] = x[LRU_CHUNK - d][SUBLANES - 1:SUBLANES, :]

            def x_at(j):
                return x[j] if j >= 0 else before[-j - 1]

            taps = [cw_ref[t:t + 1, lanes] for t in range(CONV_W)]
            bias = cb_ref[:, lanes]
            xc = jnp.concatenate(
                [bias + sum(taps[t] * x_at(j - (CONV_W - 1) + t) for t in range(CONV_W))
                 for j in range(LRU_CHUNK)], axis=0)
            xcb = xc.astype(BF16)
            both = jnp.dot(xcb, wg_ref[c], preferred_element_type=F32)
            tr = jnp.tanh(both[:, :LANES] + half_ba[c])
            ti = jnp.tanh(both[:, LANES:] + half_bx[c])
            log_a_half = (-0.5 * LRU_C) * softplus_neg[:, lanes]
            a = jnp.exp(log_a_half * tr + log_a_half)
            half_xc = 0.5 * xc
            gated_x = half_xc * ti + half_xc
            z = 1.0 - a * a
            u = jnp.where(z > 0.0, z * lax.rsqrt(z), 0.0) * gated_x

            h = jnp.zeros((SUBLANES, LANES), F32)
            p = jnp.ones((SUBLANES, LANES), F32)
            hs, ps = [], []
            for j in range(LRU_CHUNK):
                aj = piece(a, j)
                uj = piece(u, j)
                if first:
                    uj = jnp.where(first_row + j >= PAD_FRONT, uj, 0.0)
                h = aj * h + uj
                p = aj * p
                hs.append(h)
                ps.append(p)
            entry = [hcar[:, lanes]]
            for s in range(SUBLANES):
                entry.append(h[s:s + 1, :] + p[s:s + 1, :] * entry[s])
            hcar[:, lanes] = entry[SUBLANES]
            entry_rows = jnp.concatenate(entry[:SUBLANES], axis=0)

            for j in range(LRU_CHUNK):
                state = hs[j] + ps[j] * entry_rows
                out = state * _gelu_tanh(y_st[c, strided(j), :])
                sumsq[j] = sumsq[j] + out * out
                o_st[c, strided(j), :] = out

        for j in range(LRU_CHUNK):
            ms = jnp.sum(sumsq[j], axis=1, keepdims=True) * (1.0 / LRU_WIDTH)
            s_st[strided(j), :] = jnp.broadcast_to(lax.rsqrt(ms + EPS), (SUBLANES, LANES))
        scale = s_st[...]
        for c in range(LRU_SLABS):
            lanes = slice(c * LANES, (c + 1) * LANES)
            normed = (o_st[c] * scale * g_ref[:, lanes]).astype(o_ref.dtype)
            if first:
                o_ref[0:LRU_SEG - BLOCK, lanes] = normed[BLOCK:, :]
            else:
                o_ref[rows, lanes] = normed

    assert BLOCK <= LRU_SEG
    segment(0, True)

    def later_segment(k, carry):
        segment(k, False)
        return carry

    lax.fori_loop(1, n_seg, later_segment, 0)


def _rglru(xr, yr, xrm, yrm, cw, cb, wa, wx, ba, bx, lam, g_lru, bsz, nbx):
    seq = nbx * BLOCK
    assert (seq + BLOCK) % LRU_SEG == 0
    main = pl.BlockSpec((None, seq, LRU_WIDTH), lambda b: (b, 0, 0))
    const2 = lambda b: (0, 0)
    row_spec = pl.BlockSpec((1, LRU_WIDTH), const2)
    gate_spec = pl.BlockSpec((None,) + wa.shape[1:], lambda b: (0, 0, 0, 0))
    bias_spec = pl.BlockSpec((None,) + ba.shape[1:], lambda b: (0, 0, 0))
    slabs = pltpu.VMEM((LRU_SLABS, LRU_SEG, LANES), F32)
    return pl.pallas_call(
        _lru_kernel,
        grid=(bsz,),
        in_specs=[main, main,
                  pl.BlockSpec((BLOCK, LRU_WIDTH), const2),
                  pl.BlockSpec((BLOCK, LRU_WIDTH), const2),
                  pl.BlockSpec((CONV_W, LRU_WIDTH), const2),
                  row_spec, gate_spec, gate_spec, bias_spec, bias_spec,
                  row_spec, row_spec],
        out_specs=main,
        out_shape=jax.ShapeDtypeStruct((bsz, seq, LRU_WIDTH), BF16),
        scratch_shapes=[slabs, slabs, slabs,
                        pltpu.VMEM((LRU_SEG, LANES), F32),
                        pltpu.VMEM((SUBLANES, LRU_WIDTH), F32),
                        pltpu.VMEM((1, LRU_WIDTH), F32),
                        pltpu.VMEM((LRU_SLABS, LANES, 2 * LANES), BF16)],
        compiler_params=_cparams(1),
        name="rglru",
    )(xr, yr, xrm, yrm, cw, cb, wa, wx, ba, bx, lam, g_lru)


def _pack_rows(v):
    bits = lax.bitcast_convert_type(v.astype(BF16).astype(F32), U32)
    return (bits[:, :PACKED] >> 16) | (bits[:, PACKED:] & jnp.uint32(0xFFFF0000))


def _unpack_rows(w):
    lo = lax.bitcast_convert_type(w << 16, F32)
    hi = lax.bitcast_convert_type(w & jnp.uint32(0xFFFF0000), F32)
    return lo, hi


def _layer_norm(z, g, b):
    mu = jnp.mean(z, axis=1, keepdims=True)
    zc = z - mu
    var = jnp.mean(zc * zc, axis=1, keepdims=True)
    return zc * lax.rsqrt(var + EPS) * g + b


def _out_proj_kernel(a_ref, l_ref, x_ref, w_ref, g_ref, b_ref, wrt_ref, brt_ref,
                     h_ref, hp_ref, info_ref, cnt_ref, w_b, wrt_b):
    @pl.when(pl.program_id(0) == 0)
    def _():
        cnt_ref[...] = jnp.zeros_like(cnt_ref)
        w_b[...] = w_ref[...].astype(BF16)
        hi = wrt_ref[...].astype(BF16)
        wrt_b[0:ROUTER_ROWS, :] = hi
        wrt_b[ROUTER_ROWS:, :] = (wrt_ref[...] - hi.astype(F32)).astype(BF16)

    mix = jnp.dot(jnp.concatenate([a_ref[...], l_ref[...]], axis=1), w_b[...],
                  preferred_element_type=F32)
    h = _layer_norm(ALPHA * x_ref[...] + mix, g_ref[...], b_ref[...])
    h_ref[...] = h
    hp_ref[...] = _pack_rows(h)

    h_hi = h.astype(BF16)
    h_lo = (h - h_hi.astype(F32)).astype(BF16)
    nt = (((1,), (1,)), ((), ()))
    both = lax.dot_general(wrt_b[...], h_hi, nt, preferred_element_type=F32)
    lg = (both[:ROUTER_ROWS] + both[ROUTER_ROWS:]
          + lax.dot_general(wrt_b[0:ROUTER_ROWS, :], h_lo, nt, preferred_element_type=F32)
          ) + brt_ref[...]
    tile_shape = (SUBLANES, h.shape[0])
    sub = lax.broadcasted_iota(I32, tile_shape, 0)
    ninf = -jnp.inf
    t0 = lg[0:SUBLANES]
    gl = jnp.where(sub < N_GROUPS, t0, ninf)
    gmax = jnp.max(gl, axis=0, keepdims=True)
    g_idx = jnp.min(jnp.where(gl == gmax, sub, SUBLANES), axis=0, keepdims=True)
    g_w = 1.0 / jnp.sum(jnp.where(sub < N_GROUPS, jnp.exp(t0 - gmax), 0.0),
                        axis=0, keepdims=True)
    el = lg[SUBLANES:2 * SUBLANES]
    for g in range(1, N_GROUPS):
        el = jnp.where(g_idx == g, lg[(g + 1) * SUBLANES:(g + 2) * SUBLANES], el)
    v1 = jnp.max(el, axis=0, keepdims=True)
    i1 = jnp.min(jnp.where(el == v1, sub, SUBLANES), axis=0, keepdims=True)
    el2 = jnp.where(sub == i1, ninf, el)
    v2 = jnp.max(el2, axis=0, keepdims=True)
    i2 = jnp.min(jnp.where(el2 == v2, sub, SUBLANES), axis=0, keepdims=True)
    t = jnp.exp(v2 - v1)
    w1 = 1.0 / (1.0 + t)
    w2 = t * w1
    e_base = g_idx * EXPERTS_PER_GROUP
    e1 = e_base + i1
    e2 = e_base + i2
    info_ref[...] = jnp.where(sub == 0, e1.astype(F32),
                              jnp.where(sub == 1, e2.astype(F32),
                                        jnp.where(sub == 2, g_w * w1,
                                                  jnp.where(sub == 3, g_w * w2, 0.0))))
    expert = lax.broadcasted_iota(I32, (N_EXPERTS, h.shape[0]), 0)
    chosen = (expert == e1).astype(F32) + (expert == e2).astype(F32)
    cnt_ref[...] += jnp.sum(chosen, axis=1, keepdims=True)


ROUTER_ROWS = -(-(N_GROUPS + 1) * SUBLANES // (2 * SUBLANES)) * (2 * SUBLANES)


def _out_proj(attn_n, lru_n, x2d, w_out, ln_g, ln_b, w_rt, b_rt, rows):
    n = x2d.shape[0]
    const = lambda i: (0, 0)
    tile = lambda w: pl.BlockSpec((rows, w), lambda i: (i, 0))
    return pl.pallas_call(
        _out_proj_kernel,
        grid=(n // rows,),
        in_specs=[tile(ATTN_WIDTH), tile(LRU_WIDTH), tile(D_MODEL),
                  pl.BlockSpec((D_MODEL, D_MODEL), const),
                  pl.BlockSpec((1, D_MODEL), const),
                  pl.BlockSpec((1, D_MODEL), const),
                  pl.BlockSpec((ROUTER_ROWS, D_MODEL), const),
                  pl.BlockSpec((ROUTER_ROWS, 1), const)],
        out_specs=[tile(D_MODEL), tile(PACKED),
                   pl.BlockSpec((SUBLANES, rows), lambda i: (0, i)),
                   pl.BlockSpec((N_EXPERTS, LANES), const)],
        out_shape=[jax.ShapeDtypeStruct((n, D_MODEL), F32),
                   jax.ShapeDtypeStruct((n, PACKED), U32),
                   jax.ShapeDtypeStruct((SUBLANES, n), F32),
                   jax.ShapeDtypeStruct((N_EXPERTS, LANES), F32)],
        scratch_shapes=[pltpu.VMEM((D_MODEL, D_MODEL), BF16),
                        pltpu.VMEM((2 * ROUTER_ROWS, D_MODEL), BF16)],
        compiler_params=_cparams(1),
        name="out_proj",
    )(attn_n, lru_n, x2d, w_out, ln_g, ln_b, w_rt, b_rt)


def _route_kernel(info_ref, cnt_ref, tri_ref, dest_ref, carry, pstart):
    t = pl.program_id(0)
    cols = tri_ref.shape[0]
    n_tiles = info_ref.shape[1] // cols
    expert = lax.broadcasted_iota(I32, (N_EXPERTS, cols), 0)

    def one_hots(k):
        win = slice(k * cols, (k + 1) * cols)
        return ((expert == info_ref[0:1, win].astype(I32)).astype(F32),
                (expert == info_ref[1:2, win].astype(I32)).astype(F32))

    @pl.when(t == 0)
    def _():
        c = cnt_ref[...].astype(I32)
        padded = ((c + (MOE_BLOCK - 1)) // MOE_BLOCK) * MOE_BLOCK
        e = lax.broadcasted_iota(I32, (N_EXPERTS, LANES), 0)
        scan = padded
        for d in (1, 2, 4, 8, 16):
            scan = scan + jnp.where(e >= d, pltpu.roll(scan, d, axis=0), 0)
        pstart[...] = (scan - padded)[:, 0:1].astype(F32)
        carry[...] = jnp.zeros_like(carry)

    stacked = jnp.concatenate([sum(one_hots(k)).astype(BF16) for k in range(n_tiles)], axis=0)
    before = jnp.dot(stacked, tri_ref[...], preferred_element_type=F32)
    seen = carry[...] + pstart[...]
    for k in range(n_tiles):
        oh1, oh2 = one_hots(k)
        row_of = before[k * N_EXPERTS:(k + 1) * N_EXPERTS, :] + seen
        r1 = jnp.sum(oh1 * row_of, axis=0, keepdims=True)
        r2 = jnp.sum(oh2 * row_of, axis=0, keepdims=True)
        dest_ref[:, k * cols:(k + 1) * cols] = jnp.concatenate([r1, r2], axis=0).astype(I32)
        seen = seen + jnp.sum(oh1 + oh2, axis=1, keepdims=True)
    carry[...] = seen - pstart[...]


def _route(info_t, cnt, cols):
    n = info_t.shape[1]
    tri = jnp.asarray(np.triu(np.ones((cols, cols), np.float32), 1), BF16)
    step_cols = ROUTE_TILES * cols
    return pl.pallas_call(
        _route_kernel,
        grid=(n // step_cols,),
        in_specs=[pl.BlockSpec((SUBLANES, step_cols), lambda t: (0, t)),
                  pl.BlockSpec((N_EXPERTS, LANES), lambda t: (0, 0)),
                  pl.BlockSpec((cols, cols), lambda t: (0, 0))],
        out_specs=pl.BlockSpec((TOP_K, step_cols), lambda t: (0, t)),
        out_shape=jax.ShapeDtypeStruct((TOP_K, n), I32),
        scratch_shapes=[pltpu.VMEM((N_EXPERTS, 1), F32), pltpu.VMEM((N_EXPERTS, 1), F32)],
        compiler_params=_cparams(1),
        name="route",
    )(info_t, cnt, tri)


def _sc_mesh():
    return plsc.VectorSubcoreMesh(core_axis_name="core", subcore_axis_name="subcore")


def _sc_worker_id():
    return lax.axis_index("subcore") * SC_CORES + lax.axis_index("core")


def _sc_scatter_rows(rows, dest, cap):
    n, width = rows.shape
    per_worker = n // SC_WORKERS
    n_win = per_worker // SC_WINDOW

    def body(x_hbm, d_hbm, o_hbm, i0_v, i1_v, rows_v, isem, rsem, sem0, sem1):
        wid = _sc_worker_id()

        def indices(j):
            win = pl.ds(wid * per_worker + j * SC_WINDOW, SC_WINDOW)
            return (pltpu.make_async_copy(d_hbm.at[0, win], i0_v.at[j], isem.at[0, j]),
                    pltpu.make_async_copy(d_hbm.at[1, win], i1_v.at[j], isem.at[1, j]))

        def read(j):
            src = x_hbm.at[pl.ds(wid * per_worker + j * SC_WINDOW, SC_WINDOW)]
            return pltpu.make_async_copy(src, rows_v.at[j % 2], rsem.at[j % 2])

        def scatters(j):
            return (pltpu.make_async_copy(rows_v.at[j % 2], o_hbm.at[i0_v.at[j]], sem0.at[j % 2]),
                    pltpu.make_async_copy(rows_v.at[j % 2], o_hbm.at[i1_v.at[j]], sem1.at[j % 2]))

        read(0).start()
        for j in range(n_win):
            for cp in indices(j):
                cp.start()
        for j in range(n_win):
            for cp in indices(j):
                cp.wait()
        for j in range(n_win):
            if j + 1 < n_win:
                if j >= 1:
                    for cp in scatters(j - 1):
                        cp.wait()
                read(j + 1).start()
            read(j).wait()
            for cp in scatters(j):
                cp.start()
        for j in range(max(n_win - 2, 0), n_win):
            for cp in scatters(j):
                cp.wait()

    return pl.kernel(
        body,
        out_type=jax.ShapeDtypeStruct((cap, width), rows.dtype),
        mesh=_sc_mesh(),
        scratch_types=[pltpu.VMEM((n_win, SC_WINDOW), I32), pltpu.VMEM((n_win, SC_WINDOW), I32),
                       pltpu.VMEM((2, SC_WINDOW, width), rows.dtype),
                       pltpu.SemaphoreType.DMA((TOP_K, n_win)),
                       pltpu.SemaphoreType.DMA((2,)), pltpu.SemaphoreType.DMA((2,)),
                       pltpu.SemaphoreType.DMA((2,))],
        name="dispatch",
    )(rows, dest)


def _sc_gather_rows(table, dest, start, length):
    width = table.shape[1]
    per_worker = TOP_K * length // SC_WORKERS
    n_win = per_worker // SC_WINDOW
    workers_per_choice = SC_WORKERS // TOP_K

    def body(y_hbm, d_hbm, o_hbm, i_v, rows_v, isem, gsem, wsem):
        wid = _sc_worker_id()
        choice = wid // workers_per_choice
        first = start + (wid % workers_per_choice) * per_worker

        def indices(j):
            win = pl.ds(first + j * SC_WINDOW, SC_WINDOW)
            return pltpu.make_async_copy(d_hbm.at[choice, win], i_v.at[j], isem.at[j])

        for j in range(n_win):
            indices(j).start()
        for j in range(n_win):
            indices(j).wait()

        def gather(j):
            return pltpu.make_async_copy(y_hbm.at[i_v.at[j]], rows_v.at[j % 2], gsem.at[j % 2])

        def write(j):
            dst = o_hbm.at[pl.ds(wid * per_worker + j * SC_WINDOW, SC_WINDOW)]
            return pltpu.make_async_copy(rows_v.at[j % 2], dst, wsem.at[j % 2])

        gather(0).start()
        for j in range(n_win):
            if j + 1 < n_win:
                if j >= 1:
                    write(j - 1).wait()
                gather(j + 1).start()
            gather(j).wait()
            write(j).start()
        for j in range(max(n_win - 2, 0), n_win):
            write(j).wait()

    return pl.kernel(
        body,
        out_type=jax.ShapeDtypeStruct((SC_WORKERS * per_worker, width), table.dtype),
        mesh=_sc_mesh(),
        scratch_types=[pltpu.VMEM((n_win, SC_WINDOW), I32),
                       pltpu.VMEM((2, SC_WINDOW, width), table.dtype),
                       pltpu.SemaphoreType.DMA((n_win,)),
                       pltpu.SemaphoreType.DMA((2,)), pltpu.SemaphoreType.DMA((2,))],
        name="collect",
    )(table, dest)


def _expert_kernel(bstart_ref, nblk_ref, nused_ref, xs_hbm, wg_ref, wu_ref, wd_ref, yb_hbm,
                   xbuf, ybuf, zbuf, xsem, ysem, zsem, wg_b, wu_b, wd_b):
    e = pl.program_id(0)
    nused = nused_ref[0]
    n_blocks = yb_hbm.shape[0] // MOE_BLOCK

    def rows(b):
        return pl.ds(pl.multiple_of(b * MOE_BLOCK, MOE_BLOCK), MOE_BLOCK)

    def x_copy(b):
        slot = b % X_RING
        return pltpu.make_async_copy(xs_hbm.at[rows(b)], xbuf.at[slot], xsem.at[slot])

    def y_copy(b):
        slot = b % Y_RING
        return pltpu.make_async_copy(ybuf.at[slot], yb_hbm.at[rows(b)], ysem.at[slot])

    @pl.when(e == 0)
    def _():
        for b in range(X_AHEAD):
            @pl.when(b < nused)
            def _():
                x_copy(b).start()

    wg_b[...] = wg_ref[...].astype(BF16)
    wu_b[...] = wu_ref[...].astype(BF16)
    wd_b[...] = wd_ref[...].astype(BF16)

    def run_blocks(b, count):
        for k in range(count):
            x_copy(b + k).wait()
        for k in range(count):
            nxt = b + X_AHEAD + k

            @pl.when(nxt < nused)
            def _():
                x_copy(nxt).start()

            @pl.when(b + k >= Y_RING)
            def _():
                y_copy(b + k - Y_RING).wait()

        words = jnp.concatenate([xbuf[(b + k) % X_RING] for k in range(count)], axis=0)
        lo, hi = _unpack_rows(words)
        x = jnp.concatenate([lo.astype(BF16), hi.astype(BF16)], axis=1)
        g = jnp.dot(x, wg_b[...], preferred_element_type=F32)
        u = jnp.dot(x, wu_b[...], preferred_element_type=F32)
        mid = (g * _sigmoid(g) * u).astype(BF16)
        y = _pack_rows(jnp.dot(mid, wd_b[...], preferred_element_type=F32))
        for k in range(count):
            ybuf[(b + k) % Y_RING] = y[k * MOE_BLOCK:(k + 1) * MOE_BLOCK]
            y_copy(b + k).start()

    b0 = bstart_ref[e]
    nb = nblk_ref[e]

    def group(i, carry):
        run_blocks(b0 + X_GROUP * i, X_GROUP)
        return carry

    lax.fori_loop(0, nb // X_GROUP, group, 0)
    done = nb - nb % X_GROUP
    size = X_GROUP // 2
    while size >= 1:
        @pl.when((nb // size) % 2 == 1)
        def _(size=size, done=done):
            run_blocks(b0 + done, size)

        done = done + (nb // size) % 2 * size
        size //= 2

    @pl.when(e == pl.num_programs(0) - 1)
    def _():
        for back in range(Y_RING, 0, -1):
            @pl.when(nused >= back)
            def _():
                y_copy(nused - back).wait()

        zbuf[...] = jnp.zeros_like(zbuf)

        def z_copy(b):
            return pltpu.make_async_copy(zbuf, yb_hbm.at[rows(b)], zsem.at[0])

        def z_start(b, carry):
            z_copy(b).start()
            return carry

        def z_wait(b, carry):
            z_copy(b).wait()
            return carry

        lax.fori_loop(nused, n_blocks, z_start, 0)
        lax.fori_loop(nused, n_blocks, z_wait, 0)


def _experts(bstart, nblk, nused, xs, w_gate, w_up, w_down):
    cap = xs.shape[0]
    w_idx = lambda e, bs, nb, nu: (e, 0, 0)
    grid_spec = pltpu.PrefetchScalarGridSpec(
        num_scalar_prefetch=3,
        grid=(N_EXPERTS,),
        in_specs=[pl.BlockSpec(memory_space=pl.ANY),
                  pl.BlockSpec((None, D_MODEL, D_FF), w_idx),
                  pl.BlockSpec((None, D_MODEL, D_FF), w_idx),
                  pl.BlockSpec((None, D_FF, D_MODEL), w_idx)],
        out_specs=pl.BlockSpec(memory_space=pl.ANY),
        scratch_shapes=[pltpu.VMEM((X_RING, MOE_BLOCK, PACKED), U32),
                        pltpu.VMEM((Y_RING, MOE_BLOCK, PACKED), U32),
                        pltpu.VMEM((MOE_BLOCK, PACKED), U32),
                        pltpu.SemaphoreType.DMA((X_RING,)),
                        pltpu.SemaphoreType.DMA((Y_RING,)),
                        pltpu.SemaphoreType.DMA((1,)),
                        pltpu.VMEM((D_MODEL, D_FF), BF16),
                        pltpu.VMEM((D_MODEL, D_FF), BF16),
                        pltpu.VMEM((D_FF, D_MODEL), BF16)])
    return pl.pallas_call(
        _expert_kernel,
        grid_spec=grid_spec,
        out_shape=jax.ShapeDtypeStruct((cap, PACKED), U32),
        compiler_params=_cparams(1),
        name="experts",
    )(bstart, nblk, nused, xs, w_gate, w_up, w_down)


def _combine_kernel(y0_ref, y1_ref, h_ref, info_ref, g_ref, b_ref, o_ref):
    info = info_ref[...].T
    g0 = info[:, 2:3]
    g1 = info[:, 3:4]
    lo0, hi0 = _unpack_rows(y0_ref[...])
    lo1, hi1 = _unpack_rows(y1_ref[...])
    y = jnp.concatenate([g0 * lo0 + g1 * lo1, g0 * hi0 + g1 * hi1], axis=1)
    o_ref[...] = _layer_norm(ALPHA * h_ref[...] + y, g_ref[...], b_ref[...])


def _combine(ys, h, info, ln_g, ln_b, rows, part, n_parts):
    n = h.shape[0]
    steps = n // n_parts // rows
    off = part * steps
    const = lambda i: (0, 0)
    return pl.pallas_call(
        _combine_kernel,
        grid=(steps,),
        in_specs=[pl.BlockSpec((rows, PACKED), lambda i: (i, 0)),
                  pl.BlockSpec((rows, PACKED), lambda i: (i + steps, 0)),
                  pl.BlockSpec((rows, D_MODEL), lambda i: (i + off, 0)),
                  pl.BlockSpec((SUBLANES, rows), lambda i: (0, i + off)),
                  pl.BlockSpec((1, D_MODEL), const),
                  pl.BlockSpec((1, D_MODEL), const)],
        out_specs=pl.BlockSpec((rows, D_MODEL), lambda i: (i + off, 0)),
        out_shape=jax.ShapeDtypeStruct((n, D_MODEL), F32),
        input_output_aliases={2: 0},
        compiler_params=_cparams(1),
        name="combine",
    )(ys, ys, h, info, ln_g, ln_b)


def _alibi_bias():
    qi = np.arange(BLOCK)[:, None]
    kj = np.arange(2 * BLOCK)[None, :]
    dist = qi - kj + BLOCK
    band = (dist >= 0) & (dist < BLOCK)
    slopes = np.exp2(-8.0 * np.arange(1, N_Q_HEADS + 1, dtype=np.float32) / N_Q_HEADS)
    bias = np.where(band[None], -slopes[:, None, None] * dist[None].astype(np.float32), NEG)
    bias = bias * LOG2E
    first = np.where((kj >= PAD_FRONT)[None], bias, NEG)
    out = np.empty((2, N_KV_HEADS, 2 * BLOCK, 4 * BLOCK), np.float32)
    for v, per_head in enumerate((first, bias)):
        for j in range(N_KV_HEADS):
            out[v, j] = np.block([[per_head[4 * j], per_head[4 * j + 1]],
                                  [per_head[4 * j + 2], per_head[4 * j + 3]]])
    return jnp.asarray(out, F32)


def kernel(x, meta_tokens, w_in, conv_w, conv_b, lru_wa, lru_ba, lru_wx, lru_bx, lru_lambda,
           attn_sinks, g_attn, g_lru, w_out, ln1_g, ln1_b, w_group, b_group, w_router,
           b_router, w_gate, w_up, w_down, ln2_g, ln2_b):
    bsz, seq, d = x.shape
    nbx = seq // BLOCK
    n_tok = bsz * seq
    x2d = x.reshape(n_tok, d)
    row = lambda v: v[0:1].reshape(1, -1).astype(F32)

    q_scale = jnp.concatenate([jnp.full((ATTN_WIDTH,), LOG2E * HEAD_DIM ** -0.5, F32),
                               jnp.ones((IN_COLS - ATTN_WIDTH,), F32)])
    w_in_b = (w_in[0] * q_scale).astype(BF16)
    q, kv, xr, yr = _in_proj(x2d, w_in_b, PROJ_ROWS)
    qm, kvm, xrm, yrm = _in_proj(meta_tokens.astype(F32), w_in_b, N_META, front=PAD_FRONT)
    shp = lambda a: a.reshape(bsz, seq, a.shape[-1])

    attn_n = _attention(attn_sinks[0].astype(F32), shp(q), shp(kv), kvm, _alibi_bias(),
                        row(g_attn), bsz, nbx)
    lru_n = _rglru(shp(xr), shp(yr), xrm, yrm, conv_w[0].astype(F32), row(conv_b),
                   lru_wa.astype(F32), lru_wx.astype(F32), lru_ba.astype(F32),
                   lru_bx.astype(F32), row(lru_lambda), row(g_lru), bsz, nbx)

    gpad = SUBLANES - N_GROUPS
    tail = ROUTER_ROWS - SUBLANES - N_EXPERTS
    w_rt = jnp.concatenate(
        [w_group[0].T, jnp.zeros((gpad, d), F32),
         jnp.transpose(w_router[0], (0, 2, 1)).reshape(N_EXPERTS, d),
         jnp.zeros((tail, d), F32)], axis=0).astype(F32)
    b_rt = jnp.concatenate([b_group[0], jnp.zeros((gpad,), F32), b_router[0].reshape(-1),
                            jnp.zeros((tail,), F32)]).astype(F32).reshape(ROUTER_ROWS, 1)
    h1, hp, info, cnt = _out_proj(
        attn_n.reshape(n_tok, ATTN_WIDTH), lru_n.reshape(n_tok, LRU_WIDTH), x2d,
        w_out[0].astype(F32), row(ln1_g), row(ln1_b), w_rt, b_rt, OUT_PROJ_ROWS)

    dest = _route(info, cnt, ROUTE_ROWS)
    n_slots = n_tok * TOP_K
    n_blocks = n_slots // MOE_BLOCK + N_EXPERTS
    cap = n_blocks * MOE_BLOCK
    nblk = (cnt[:, 0].astype(I32) + MOE_BLOCK - 1) // MOE_BLOCK
    bends = jnp.cumsum(nblk)
    bstart = (bends - nblk).astype(I32)
    nused = bends[-1:].astype(I32)

    xs = _sc_scatter_rows(hp, dest, cap)
    yb = _experts(bstart, nblk.astype(I32), nused, xs, w_gate[0], w_up[0], w_down[0])
    out = h1
    part_len = n_tok // COMBINE_PARTS
    for part in range(COMBINE_PARTS):
        ys = _sc_gather_rows(yb, dest, part * part_len, part_len)
        out = _combine(ys, out, info, row(ln2_g), row(ln2_b), COMBINE_ROWS,
                       part, COMBINE_PARTS)
    return out.reshape(bsz, seq, d)
```
